```python
import jax, jax.numpy as jnp
from jax import lax
import numpy as np

D_MODEL = 1024
BATCH = 8
SEQ = 4096
DEPTH = 4

CHUNK = 64
N_MIXERS = 3
N_HEADS = 16
HEAD_DIM = D_MODEL // N_HEADS
Q_BLOCK = 128
SG_CHUNK = 128
SG_WIDTH = D_MODEL
SG_GROUPS = 8
SG_GROUP_DIM = SG_WIDTH // SG_GROUPS
CONV_WIDTH = 31
D_FF = 4 * D_MODEL
N_A = (DEPTH + 2) // N_MIXERS
N_B = (DEPTH + 1) // N_MIXERS
N_C = DEPTH // N_MIXERS
EPS = 1e-6

kernel_name = "chunk_causal_hybrid_fox_gmlp_conformer"


def rms_norm(x, g):
    xf = x.astype(jnp.float32)
    y = xf * lax.rsqrt(jnp.mean(xf * xf, axis=-1, keepdims=True) + EPS)
    return (y * g.astype(jnp.float32)).astype(x.dtype)


def layer_norm(x, g, b):
    xf = x.astype(jnp.float32)
    mu = jnp.mean(xf, axis=-1, keepdims=True)
    xc = xf - mu
    y = xc * lax.rsqrt(jnp.mean(xc * xc, axis=-1, keepdims=True) + EPS)
    return (y * g.astype(jnp.float32) + b.astype(jnp.float32)).astype(x.dtype)


def fox_mixer(h, w_in, b_f, q_g, k_g, w_out):
    B, S, D = h.shape
    proj = h @ w_in
    q, k, v, f_pre = jnp.split(proj, [D, 2 * D, 3 * D], axis=-1)
    q = rms_norm(q.reshape(B, S, N_HEADS, HEAD_DIM), q_g)
    k = rms_norm(k.reshape(B, S, N_HEADS, HEAD_DIM), k_g)
    v = v.reshape(B, S, N_HEADS, HEAD_DIM)
    log_f = jax.nn.log_sigmoid(f_pre.astype(jnp.float32) + b_f.astype(jnp.float32))
    F = jnp.cumsum(log_f, axis=1).transpose(0, 2, 1)
    nb = S // Q_BLOCK
    qb = q.reshape(B, nb, Q_BLOCK, N_HEADS, HEAD_DIM).swapaxes(0, 1)
    Fq = F.reshape(B, N_HEADS, nb, Q_BLOCK).transpose(2, 0, 1, 3)
    k_pos = jnp.arange(S)
    scale = HEAD_DIM ** -0.5

    def block(args):
        q_i, F_i, b_i = args
        logits = (jnp.einsum('bqhd,bkhd->bhqk', q_i, k).astype(jnp.float32) * scale
                  + (F_i[..., :, None] - F[..., None, :]))
        q_pos = b_i * Q_BLOCK + jnp.arange(Q_BLOCK)
        logits = jnp.where(k_pos[None, :] <= q_pos[:, None], logits, -jnp.inf)
        p = jax.nn.softmax(logits, axis=-1).astype(v.dtype)
        return jnp.einsum('bhqk,bkhd->bqhd', p, v)

    o = lax.map(block, (qb, Fq, jnp.arange(nb)))
    o = o.swapaxes(0, 1).reshape(B, S, D)
    return o @ w_out


def gmlp_mixer(h, w_in, ln_g, ln_b, w_s, b_s, w_out):
    B, S, _ = h.shape
    uv = jax.nn.gelu(h @ w_in)
    u, v = jnp.split(uv, 2, axis=-1)
    v = layer_norm(v, ln_g, ln_b)
    v = v.reshape(B, S // SG_CHUNK, SG_CHUNK, SG_GROUPS, SG_GROUP_DIM)
    cid = jnp.arange(SG_CHUNK) // CHUNK
    mask = cid[None, :] <= cid[:, None]
    ws = jnp.where(mask[None], w_s, jnp.zeros_like(w_s))
    v = jnp.einsum('gts,bnsgc->bntgc', ws, v) + b_s.T[:, :, None]
    v = v.reshape(B, S, SG_WIDTH)
    return (u * v) @ w_out


def conv_mixer(h, w_pw1, b_pw1, w_dw, b_dw, ln_g, ln_b, w_pw2, b_pw2):
    D = h.shape[-1]
    y = jax.nn.glu(h @ w_pw1 + b_pw1, axis=-1)
    y = lax.conv_general_dilated(y, w_dw[:, None, :], window_strides=(1,),
                                 padding=[(CONV_WIDTH - 1, 0)],
                                 dimension_numbers=('NWC', 'WIO', 'NWC'),
                                 feature_group_count=D) + b_dw
    y = jax.nn.silu(layer_norm(y, ln_g, ln_b))
    return y @ w_pw2 + b_pw2


def _fwd_setup_inputs(seed: int = 0) -> dict:
    key = jax.random.key(seed)
    ks = iter(jax.random.split(key, 40))
    D = D_MODEL

    def nrm(shape, scale):
        return jax.random.normal(next(ks), shape, jnp.float32) * scale

    def gain(shape):
        return 1.0 + nrm(shape, 0.05)

    return {
        "x": nrm((BATCH, SEQ, D), 1.0),
        "c": nrm((BATCH, D), 1.0),
        "norm_mix": gain((DEPTH, D)),
        "norm_mlp": gain((DEPTH, D)),
        "w_ada": nrm((DEPTH, D, 6 * D), 0.5 * D ** -0.5),
        "b_ada": nrm((DEPTH, 6 * D), 0.02),
        "w_mlp_in": nrm((DEPTH, D, D_FF), D ** -0.5),
        "w_mlp_out": nrm((DEPTH, D_FF, D), D_FF ** -0.5),
        "fox_w_in": nrm((N_A, D, 3 * D + N_HEADS), D ** -0.5),
        "fox_b_f": jax.random.uniform(next(ks), (N_A, N_HEADS), jnp.float32, 1.0, 6.0),
        "fox_q_norm": gain((N_A, HEAD_DIM)),
        "fox_k_norm": gain((N_A, HEAD_DIM)),
        "fox_w_out": nrm((N_A, D, D), D ** -0.5),
        "sg_w_in": nrm((N_B, D, 2 * SG_WIDTH), D ** -0.5),
        "sg_ln_g": gain((N_B, SG_WIDTH)),
        "sg_ln_b": nrm((N_B, SG_WIDTH), 0.02),
        "sg_w_s": nrm((N_B, SG_GROUPS, SG_CHUNK, SG_CHUNK), 0.5 * SG_CHUNK ** -0.5),
        "sg_b_s": 1.0 + nrm((N_B, SG_GROUPS, SG_CHUNK), 0.02),
        "sg_w_out": nrm((N_B, SG_WIDTH, D), SG_WIDTH ** -0.5),
        "cv_w_pw1": nrm((N_C, D, 2 * D), D ** -0.5),
        "cv_b_pw1": nrm((N_C, 2 * D), 0.02),
        "cv_w_dw": nrm((N_C, CONV_WIDTH, D), CONV_WIDTH ** -0.5),
        "cv_b_dw": nrm((N_C, D), 0.02),
        "cv_ln_g": gain((N_C, D)),
        "cv_ln_b": nrm((N_C, D), 0.02),
        "cv_w_pw2": nrm((N_C, D, D), D ** -0.5),
        "cv_b_pw2": nrm((N_C, D), 0.02),
    }


def _fwd_reference(x, c, norm_mix, norm_mlp, w_ada, b_ada, w_mlp_in, w_mlp_out,
              fox_w_in, fox_b_f, fox_q_norm, fox_k_norm, fox_w_out,
              sg_w_in, sg_ln_g, sg_ln_b, sg_w_s, sg_b_s, sg_w_out,
              cv_w_pw1, cv_b_pw1, cv_w_dw, cv_b_dw, cv_ln_g, cv_ln_b, cv_w_pw2, cv_b_pw2):
    c_act = jax.nn.silu(c)
    for i in range(DEPTH):
        kind = i % N_MIXERS
        j = i // N_MIXERS
        mod = c_act @ w_ada[i] + b_ada[i]
        sh_m, sc_m, g_m, sh_f, sc_f, g_f = [m[:, None, :] for m in jnp.split(mod, 6, axis=-1)]
        h = rms_norm(x, norm_mix[i]) * (1 + sc_m) + sh_m
        if kind == 0:
            y = fox_mixer(h, fox_w_in[j], fox_b_f[j], fox_q_norm[j], fox_k_norm[j], fox_w_out[j])
        elif kind == 1:
            y = gmlp_mixer(h, sg_w_in[j], sg_ln_g[j], sg_ln_b[j], sg_w_s[j], sg_b_s[j], sg_w_out[j])
        else:
            y = conv_mixer(h, cv_w_pw1[j], cv_b_pw1[j], cv_w_dw[j], cv_b_dw[j],
                           cv_ln_g[j], cv_ln_b[j], cv_w_pw2[j], cv_b_pw2[j])
        x = x + g_m * y
        h = rms_norm(x, norm_mlp[i]) * (1 + sc_f) + sh_f
        x = x + g_f * (jnp.square(jax.nn.relu(h @ w_mlp_in[i])) @ w_mlp_out[i])
    return x


import jax as _jax
import jax.numpy as _jnp

TWIN_FORMAT = 'train_step'
FWD_PARAMS = ['x', 'c', 'norm_mix', 'norm_mlp', 'w_ada', 'b_ada', 'w_mlp_in', 'w_mlp_out', 'fox_w_in', 'fox_b_f', 'fox_q_norm', 'fox_k_norm', 'fox_w_out', 'sg_w_in', 'sg_ln_g', 'sg_ln_b', 'sg_w_s', 'sg_b_s', 'sg_w_out', 'cv_w_pw1', 'cv_b_pw1', 'cv_w_dw', 'cv_b_dw', 'cv_ln_g', 'cv_ln_b', 'cv_w_pw2', 'cv_b_pw2']
TWIN_WEIGHTS = ['norm_mix', 'norm_mlp', 'w_ada', 'b_ada', 'w_mlp_in', 'w_mlp_out', 'fox_w_in', 'fox_b_f', 'fox_q_norm', 'fox_k_norm', 'fox_w_out', 'sg_w_in', 'sg_ln_g', 'sg_ln_b', 'sg_w_s', 'sg_b_s', 'sg_w_out', 'cv_w_pw1', 'cv_b_pw1', 'cv_w_dw', 'cv_b_dw', 'cv_ln_g', 'cv_ln_b', 'cv_w_pw2', 'cv_b_pw2']
TWIN_DIFF_INPUT = 'x'
TWIN_INPUTS = ['x', 'c', 'norm_mix', 'norm_mlp', 'w_ada', 'b_ada', 'w_mlp_in', 'w_mlp_out', 'fox_w_in', 'fox_b_f', 'fox_q_norm', 'fox_k_norm', 'fox_w_out', 'sg_w_in', 'sg_ln_g', 'sg_ln_b', 'sg_w_s', 'sg_b_s', 'sg_w_out', 'cv_w_pw1', 'cv_b_pw1', 'cv_w_dw', 'cv_b_dw', 'cv_ln_g', 'cv_ln_b', 'cv_w_pw2', 'cv_b_pw2', 'loss_target', 'm_norm_mix', 'm_norm_mlp', 'm_w_ada', 'm_b_ada', 'm_w_mlp_in', 'm_w_mlp_out', 'm_fox_w_in', 'm_fox_b_f', 'm_fox_q_norm', 'm_fox_k_norm', 'm_fox_w_out', 'm_sg_w_in', 'm_sg_ln_g', 'm_sg_ln_b', 'm_sg_w_s', 'm_sg_b_s', 'm_sg_w_out', 'm_cv_w_pw1', 'm_cv_b_pw1', 'm_cv_w_dw', 'm_cv_b_dw', 'm_cv_ln_g', 'm_cv_ln_b', 'm_cv_w_pw2', 'm_cv_b_pw2', 'v_norm_mix', 'v_norm_mlp', 'v_w_ada', 'v_b_ada', 'v_w_mlp_in', 'v_w_mlp_out', 'v_fox_w_in', 'v_fox_b_f', 'v_fox_q_norm', 'v_fox_k_norm', 'v_fox_w_out', 'v_sg_w_in', 'v_sg_ln_g', 'v_sg_ln_b', 'v_sg_w_s', 'v_sg_b_s', 'v_sg_w_out', 'v_cv_w_pw1', 'v_cv_b_pw1', 'v_cv_w_dw', 'v_cv_b_dw', 'v_cv_ln_g', 'v_cv_ln_b', 'v_cv_w_pw2', 'v_cv_b_pw2']
TWIN_OUTPUTS = ['loss', 'grad_x', 'grad_norm_mix', 'grad_norm_mlp', 'grad_w_ada', 'grad_b_ada', 'grad_w_mlp_in', 'grad_w_mlp_out', 'grad_fox_w_in', 'grad_fox_b_f', 'grad_fox_q_norm', 'grad_fox_k_norm', 'grad_fox_w_out', 'grad_sg_w_in', 'grad_sg_ln_g', 'grad_sg_ln_b', 'grad_sg_w_s', 'grad_sg_b_s', 'grad_sg_w_out', 'grad_cv_w_pw1', 'grad_cv_b_pw1', 'grad_cv_w_dw', 'grad_cv_b_dw', 'grad_cv_ln_g', 'grad_cv_ln_b', 'grad_cv_w_pw2', 'grad_cv_b_pw2', 'delta_norm_mix', 'delta_norm_mlp', 'delta_w_ada', 'delta_b_ada', 'delta_w_mlp_in', 'delta_w_mlp_out', 'delta_fox_w_in', 'delta_fox_b_f', 'delta_fox_q_norm', 'delta_fox_k_norm', 'delta_fox_w_out', 'delta_sg_w_in', 'delta_sg_ln_g', 'delta_sg_ln_b', 'delta_sg_w_s', 'delta_sg_b_s', 'delta_sg_w_out', 'delta_cv_w_pw1', 'delta_cv_b_pw1', 'delta_cv_w_dw', 'delta_cv_b_dw', 'delta_cv_ln_g', 'delta_cv_ln_b', 'delta_cv_w_pw2', 'delta_cv_b_pw2', 'new_m_norm_mix', 'new_m_norm_mlp', 'new_m_w_ada', 'new_m_b_ada', 'new_m_w_mlp_in', 'new_m_w_mlp_out', 'new_m_fox_w_in', 'new_m_fox_b_f', 'new_m_fox_q_norm', 'new_m_fox_k_norm', 'new_m_fox_w_out', 'new_m_sg_w_in', 'new_m_sg_ln_g', 'new_m_sg_ln_b', 'new_m_sg_w_s', 'new_m_sg_b_s', 'new_m_sg_w_out', 'new_m_cv_w_pw1', 'new_m_cv_b_pw1', 'new_m_cv_w_dw', 'new_m_cv_b_dw', 'new_m_cv_ln_g', 'new_m_cv_ln_b', 'new_m_cv_w_pw2', 'new_m_cv_b_pw2', 'new_v_norm_mix', 'new_v_norm_mlp', 'new_v_w_ada', 'new_v_b_ada', 'new_v_w_mlp_in', 'new_v_w_mlp_out', 'new_v_fox_w_in', 'new_v_fox_b_f', 'new_v_fox_q_norm', 'new_v_fox_k_norm', 'new_v_fox_w_out', 'new_v_sg_w_in', 'new_v_sg_ln_g', 'new_v_sg_ln_b', 'new_v_sg_w_s', 'new_v_sg_b_s', 'new_v_sg_w_out', 'new_v_cv_w_pw1', 'new_v_cv_b_pw1', 'new_v_cv_w_dw', 'new_v_cv_b_dw', 'new_v_cv_ln_g', 'new_v_cv_ln_b', 'new_v_cv_w_pw2', 'new_v_cv_b_pw2']
TWIN_LEAF_KINDS = {'loss': 'loss', 'grad_x': 'grad_x', 'grad_norm_mix': 'grad_w', 'grad_norm_mlp': 'grad_w', 'grad_w_ada': 'grad_w', 'grad_b_ada': 'grad_w', 'grad_w_mlp_in': 'grad_w', 'grad_w_mlp_out': 'grad_w', 'grad_fox_w_in': 'grad_w', 'grad_fox_b_f': 'grad_w', 'grad_fox_q_norm': 'grad_w', 'grad_fox_k_norm': 'grad_w', 'grad_fox_w_out': 'grad_w', 'grad_sg_w_in': 'grad_w', 'grad_sg_ln_g': 'grad_w', 'grad_sg_ln_b': 'grad_w', 'grad_sg_w_s': 'grad_w', 'grad_sg_b_s': 'grad_w', 'grad_sg_w_out': 'grad_w', 'grad_cv_w_pw1': 'grad_w', 'grad_cv_b_pw1': 'grad_w', 'grad_cv_w_dw': 'grad_w', 'grad_cv_b_dw': 'grad_w', 'grad_cv_ln_g': 'grad_w', 'grad_cv_ln_b': 'grad_w', 'grad_cv_w_pw2': 'grad_w', 'grad_cv_b_pw2': 'grad_w', 'delta_norm_mix': 'delta_w', 'delta_norm_mlp': 'delta_w', 'delta_w_ada': 'delta_w', 'delta_b_ada': 'delta_w', 'delta_w_mlp_in': 'delta_w', 'delta_w_mlp_out': 'delta_w', 'delta_fox_w_in': 'delta_w', 'delta_fox_b_f': 'delta_w', 'delta_fox_q_norm': 'delta_w', 'delta_fox_k_norm': 'delta_w', 'delta_fox_w_out': 'delta_w', 'delta_sg_w_in': 'delta_w', 'delta_sg_ln_g': 'delta_w', 'delta_sg_ln_b': 'delta_w', 'delta_sg_w_s': 'delta_w', 'delta_sg_b_s': 'delta_w', 'delta_sg_w_out': 'delta_w', 'delta_cv_w_pw1': 'delta_w', 'delta_cv_b_pw1': 'delta_w', 'delta_cv_w_dw': 'delta_w', 'delta_cv_b_dw': 'delta_w', 'delta_cv_ln_g': 'delta_w', 'delta_cv_ln_b': 'delta_w', 'delta_cv_w_pw2': 'delta_w', 'delta_cv_b_pw2': 'delta_w', 'new_m_norm_mix': 'new_m', 'new_m_norm_mlp': 'new_m', 'new_m_w_ada': 'new_m', 'new_m_b_ada': 'new_m', 'new_m_w_mlp_in': 'new_m', 'new_m_w_mlp_out': 'new_m', 'new_m_fox_w_in': 'new_m', 'new_m_fox_b_f': 'new_m', 'new_m_fox_q_norm': 'new_m', 'new_m_fox_k_norm': 'new_m', 'new_m_fox_w_out': 'new_m', 'new_m_sg_w_in': 'new_m', 'new_m_sg_ln_g': 'new_m', 'new_m_sg_ln_b': 'new_m', 'new_m_sg_w_s': 'new_m', 'new_m_sg_b_s': 'new_m', 'new_m_sg_w_out': 'new_m', 'new_m_cv_w_pw1': 'new_m', 'new_m_cv_b_pw1': 'new_m', 'new_m_cv_w_dw': 'new_m', 'new_m_cv_b_dw': 'new_m', 'new_m_cv_ln_g': 'new_m', 'new_m_cv_ln_b': 'new_m', 'new_m_cv_w_pw2': 'new_m', 'new_m_cv_b_pw2': 'new_m', 'new_v_norm_mix': 'new_v', 'new_v_norm_mlp': 'new_v', 'new_v_w_ada': 'new_v', 'new_v_b_ada': 'new_v', 'new_v_w_mlp_in': 'new_v', 'new_v_w_mlp_out': 'new_v', 'new_v_fox_w_in': 'new_v', 'new_v_fox_b_f': 'new_v', 'new_v_fox_q_norm': 'new_v', 'new_v_fox_k_norm': 'new_v', 'new_v_fox_w_out': 'new_v', 'new_v_sg_w_in': 'new_v', 'new_v_sg_ln_g': 'new_v', 'new_v_sg_ln_b': 'new_v', 'new_v_sg_w_s': 'new_v', 'new_v_sg_b_s': 'new_v', 'new_v_sg_w_out': 'new_v', 'new_v_cv_w_pw1': 'new_v', 'new_v_cv_b_pw1': 'new_v', 'new_v_cv_w_dw': 'new_v', 'new_v_cv_b_dw': 'new_v', 'new_v_cv_ln_g': 'new_v', 'new_v_cv_ln_b': 'new_v', 'new_v_cv_w_pw2': 'new_v', 'new_v_cv_b_pw2': 'new_v'}


def _forward(args):
    return _fwd_reference(*[args[k] for k in FWD_PARAMS])


def _output_shape():
    def fwd():
        inp = _fwd_setup_inputs(0)
        return _fwd_reference(*[inp[k] for k in FWD_PARAMS])
    out = _jax.eval_shape(fwd)
    return out.shape, out.dtype

N_MICROBATCH = 1
ADAM_LR = 0.001
ADAM_B1 = 0.9
ADAM_B2 = 0.999
ADAM_EPS = 1e-08
ADAM_WD = 0.01
ADAM_STEP = 10
PER_EXAMPLE_BATCH_AXIS = {'x': 0, 'c': 0, 'loss_target': 0}
SHARED_INPUTS = []
_WEIGHT_DTYPES = {'norm_mix': _jnp.float32, 'norm_mlp': _jnp.float32, 'w_ada': _jnp.float32, 'b_ada': _jnp.float32, 'w_mlp_in': _jnp.float32, 'w_mlp_out': _jnp.float32, 'fox_w_in': _jnp.float32, 'fox_b_f': _jnp.float32, 'fox_q_norm': _jnp.float32, 'fox_k_norm': _jnp.float32, 'fox_w_out': _jnp.float32, 'sg_w_in': _jnp.float32, 'sg_ln_g': _jnp.float32, 'sg_ln_b': _jnp.float32, 'sg_w_s': _jnp.float32, 'sg_b_s': _jnp.float32, 'sg_w_out': _jnp.float32, 'cv_w_pw1': _jnp.float32, 'cv_b_pw1': _jnp.float32, 'cv_w_dw': _jnp.float32, 'cv_b_dw': _jnp.float32, 'cv_ln_g': _jnp.float32, 'cv_ln_b': _jnp.float32, 'cv_w_pw2': _jnp.float32, 'cv_b_pw2': _jnp.float32}
MOMENT_SCALE = {'norm_mix': 1.111367e+00, 'norm_mlp': 1.201067e+01, 'w_ada': 2.890784e+00, 'b_ada': 7.094164e+00, 'w_mlp_in': 5.373085e-01, 'w_mlp_out': 1.883050e+00, 'fox_w_in': 4.620550e-01, 'fox_b_f': 6.599568e+00, 'fox_q_norm': 1.344611e+00, 'fox_k_norm': 1.349150e+00, 'fox_w_out': 7.756925e-01, 'sg_w_in': 3.523776e-01, 'sg_ln_g': 2.778156e-01, 'sg_ln_b': 9.117266e-02, 'sg_w_s': 1.013795e-01, 'sg_b_s': 1.534917e+00, 'sg_w_out': 9.523307e-01, 'cv_w_pw1': 2.773463e-01, 'cv_b_pw1': 7.882201e-01, 'cv_w_dw': 4.138962e-01, 'cv_b_dw': 1.907910e+00, 'cv_ln_g': 1.689078e+00, 'cv_ln_b': 1.498871e+00, 'cv_w_pw2': 6.143388e-01, 'cv_b_pw2': 2.372046e+00}


def _to_microbatches(a, axis):
    t = _jnp.moveaxis(a, axis, 0)
    t = t.reshape((N_MICROBATCH, t.shape[0] // N_MICROBATCH) + t.shape[1:])
    return _jnp.moveaxis(t, 1, axis + 1)


def setup_inputs(seed: int = 0) -> dict:
    inp = _fwd_setup_inputs(seed)
    key = _jax.random.fold_in(_jax.random.key(seed), 7919)
    shape, _ = _output_shape()
    out = dict(inp)
    out["loss_target"] = _jax.random.normal(_jax.random.fold_in(key, 0), shape, _jnp.float32)
    for i, name in enumerate(TWIN_WEIGHTS):
        w = inp[name].astype(_jnp.float32)
        if MOMENT_SCALE is None:
            s = _jnp.sqrt(_jnp.mean(_jnp.square(w)) + 1e-30)
        else:
            s = MOMENT_SCALE[name]
        km, kv = _jax.random.split(_jax.random.fold_in(key, i + 1))
        out[name] = w
        out["m_" + name] = s * _jax.random.normal(km, w.shape, _jnp.float32)
        out["v_" + name] = (s * s) * _jax.random.uniform(kv, w.shape, _jnp.float32, 0.5, 1.5)
    if N_MICROBATCH > 1:
        for name, axis in PER_EXAMPLE_BATCH_AXIS.items():
            out[name] = _to_microbatches(out[name], axis)
    return {'x': out['x'], 'c': out['c'], 'norm_mix': out['norm_mix'], 'norm_mlp': out['norm_mlp'], 'w_ada': out['w_ada'], 'b_ada': out['b_ada'], 'w_mlp_in': out['w_mlp_in'], 'w_mlp_out': out['w_mlp_out'], 'fox_w_in': out['fox_w_in'], 'fox_b_f': out['fox_b_f'], 'fox_q_norm': out['fox_q_norm'], 'fox_k_norm': out['fox_k_norm'], 'fox_w_out': out['fox_w_out'], 'sg_w_in': out['sg_w_in'], 'sg_ln_g': out['sg_ln_g'], 'sg_ln_b': out['sg_ln_b'], 'sg_w_s': out['sg_w_s'], 'sg_b_s': out['sg_b_s'], 'sg_w_out': out['sg_w_out'], 'cv_w_pw1': out['cv_w_pw1'], 'cv_b_pw1': out['cv_b_pw1'], 'cv_w_dw': out['cv_w_dw'], 'cv_b_dw': out['cv_b_dw'], 'cv_ln_g': out['cv_ln_g'], 'cv_ln_b': out['cv_ln_b'], 'cv_w_pw2': out['cv_w_pw2'], 'cv_b_pw2': out['cv_b_pw2'], 'loss_target': out['loss_target'], 'm_norm_mix': out['m_norm_mix'], 'm_norm_mlp': out['m_norm_mlp'], 'm_w_ada': out['m_w_ada'], 'm_b_ada': out['m_b_ada'], 'm_w_mlp_in': out['m_w_mlp_in'], 'm_w_mlp_out': out['m_w_mlp_out'], 'm_fox_w_in': out['m_fox_w_in'], 'm_fox_b_f': out['m_fox_b_f'], 'm_fox_q_norm': out['m_fox_q_norm'], 'm_fox_k_norm': out['m_fox_k_norm'], 'm_fox_w_out': out['m_fox_w_out'], 'm_sg_w_in': out['m_sg_w_in'], 'm_sg_ln_g': out['m_sg_ln_g'], 'm_sg_ln_b': out['m_sg_ln_b'], 'm_sg_w_s': out['m_sg_w_s'], 'm_sg_b_s': out['m_sg_b_s'], 'm_sg_w_out': out['m_sg_w_out'], 'm_cv_w_pw1': out['m_cv_w_pw1'], 'm_cv_b_pw1': out['m_cv_b_pw1'], 'm_cv_w_dw': out['m_cv_w_dw'], 'm_cv_b_dw': out['m_cv_b_dw'], 'm_cv_ln_g': out['m_cv_ln_g'], 'm_cv_ln_b': out['m_cv_ln_b'], 'm_cv_w_pw2': out['m_cv_w_pw2'], 'm_cv_b_pw2': out['m_cv_b_pw2'], 'v_norm_mix': out['v_norm_mix'], 'v_norm_mlp': out['v_norm_mlp'], 'v_w_ada': out['v_w_ada'], 'v_b_ada': out['v_b_ada'], 'v_w_mlp_in': out['v_w_mlp_in'], 'v_w_mlp_out': out['v_w_mlp_out'], 'v_fox_w_in': out['v_fox_w_in'], 'v_fox_b_f': out['v_fox_b_f'], 'v_fox_q_norm': out['v_fox_q_norm'], 'v_fox_k_norm': out['v_fox_k_norm'], 'v_fox_w_out': out['v_fox_w_out'], 'v_sg_w_in': out['v_sg_w_in'], 'v_sg_ln_g': out['v_sg_ln_g'], 'v_sg_ln_b': out['v_sg_ln_b'], 'v_sg_w_s': out['v_sg_w_s'], 'v_sg_b_s': out['v_sg_b_s'], 'v_sg_w_out': out['v_sg_w_out'], 'v_cv_w_pw1': out['v_cv_w_pw1'], 'v_cv_b_pw1': out['v_cv_b_pw1'], 'v_cv_w_dw': out['v_cv_w_dw'], 'v_cv_b_dw': out['v_cv_b_dw'], 'v_cv_ln_g': out['v_cv_ln_g'], 'v_cv_ln_b': out['v_cv_ln_b'], 'v_cv_w_pw2': out['v_cv_w_pw2'], 'v_cv_b_pw2': out['v_cv_b_pw2']}


def _loss(weights, diff, rest, loss_target):
    with _jax.named_scope("forward"):
        args = {**rest, TWIN_DIFF_INPUT: diff, **{k: w.astype(_WEIGHT_DTYPES[k]) for k, w in weights.items()}}
        y = _forward(args)
    with _jax.named_scope("loss_head"):
        err = _jnp.square(y.astype(_jnp.float32) - loss_target)
        return 0.5 * _jnp.sum(_jnp.mean(err, axis=-1)) if err.ndim else 0.5 * err


def _adamw(w, g, m, v):
    m = ADAM_B1 * m + (1.0 - ADAM_B1) * g
    v = ADAM_B2 * v + (1.0 - ADAM_B2) * _jnp.square(g)
    m_hat = m / (1.0 - ADAM_B1 ** ADAM_STEP)
    v_hat = v / (1.0 - ADAM_B2 ** ADAM_STEP)
    delta = -ADAM_LR * (m_hat / (_jnp.sqrt(v_hat) + ADAM_EPS) + ADAM_WD * w)
    return delta, m, v


def reference(x, c, norm_mix, norm_mlp, w_ada, b_ada, w_mlp_in, w_mlp_out, fox_w_in, fox_b_f, fox_q_norm, fox_k_norm, fox_w_out, sg_w_in, sg_ln_g, sg_ln_b, sg_w_s, sg_b_s, sg_w_out, cv_w_pw1, cv_b_pw1, cv_w_dw, cv_b_dw, cv_ln_g, cv_ln_b, cv_w_pw2, cv_b_pw2, loss_target, m_norm_mix, m_norm_mlp, m_w_ada, m_b_ada, m_w_mlp_in, m_w_mlp_out, m_fox_w_in, m_fox_b_f, m_fox_q_norm, m_fox_k_norm, m_fox_w_out, m_sg_w_in, m_sg_ln_g, m_sg_ln_b, m_sg_w_s, m_sg_b_s, m_sg_w_out, m_cv_w_pw1, m_cv_b_pw1, m_cv_w_dw, m_cv_b_dw, m_cv_ln_g, m_cv_ln_b, m_cv_w_pw2, m_cv_b_pw2, v_norm_mix, v_norm_mlp, v_w_ada, v_b_ada, v_w_mlp_in, v_w_mlp_out, v_fox_w_in, v_fox_b_f, v_fox_q_norm, v_fox_k_norm, v_fox_w_out, v_sg_w_in, v_sg_ln_g, v_sg_ln_b, v_sg_w_s, v_sg_b_s, v_sg_w_out, v_cv_w_pw1, v_cv_b_pw1, v_cv_w_dw, v_cv_b_dw, v_cv_ln_g, v_cv_ln_b, v_cv_w_pw2, v_cv_b_pw2):
    given = dict(x=x, c=c, norm_mix=norm_mix, norm_mlp=norm_mlp, w_ada=w_ada, b_ada=b_ada, w_mlp_in=w_mlp_in, w_mlp_out=w_mlp_out, fox_w_in=fox_w_in, fox_b_f=fox_b_f, fox_q_norm=fox_q_norm, fox_k_norm=fox_k_norm, fox_w_out=fox_w_out, sg_w_in=sg_w_in, sg_ln_g=sg_ln_g, sg_ln_b=sg_ln_b, sg_w_s=sg_w_s, sg_b_s=sg_b_s, sg_w_out=sg_w_out, cv_w_pw1=cv_w_pw1, cv_b_pw1=cv_b_pw1, cv_w_dw=cv_w_dw, cv_b_dw=cv_b_dw, cv_ln_g=cv_ln_g, cv_ln_b=cv_ln_b, cv_w_pw2=cv_w_pw2, cv_b_pw2=cv_b_pw2, loss_target=loss_target, m_norm_mix=m_norm_mix, m_norm_mlp=m_norm_mlp, m_w_ada=m_w_ada, m_b_ada=m_b_ada, m_w_mlp_in=m_w_mlp_in, m_w_mlp_out=m_w_mlp_out, m_fox_w_in=m_fox_w_in, m_fox_b_f=m_fox_b_f, m_fox_q_norm=m_fox_q_norm, m_fox_k_norm=m_fox_k_norm, m_fox_w_out=m_fox_w_out, m_sg_w_in=m_sg_w_in, m_sg_ln_g=m_sg_ln_g, m_sg_ln_b=m_sg_ln_b, m_sg_w_s=m_sg_w_s, m_sg_b_s=m_sg_b_s, m_sg_w_out=m_sg_w_out, m_cv_w_pw1=m_cv_w_pw1, m_cv_b_pw1=m_cv_b_pw1, m_cv_w_dw=m_cv_w_dw, m_cv_b_dw=m_cv_b_dw, m_cv_ln_g=m_cv_ln_g, m_cv_ln_b=m_cv_ln_b, m_cv_w_pw2=m_cv_w_pw2, m_cv_b_pw2=m_cv_b_pw2, v_norm_mix=v_norm_mix, v_norm_mlp=v_norm_mlp, v_w_ada=v_w_ada, v_b_ada=v_b_ada, v_w_mlp_in=v_w_mlp_in, v_w_mlp_out=v_w_mlp_out, v_fox_w_in=v_fox_w_in, v_fox_b_f=v_fox_b_f, v_fox_q_norm=v_fox_q_norm, v_fox_k_norm=v_fox_k_norm, v_fox_w_out=v_fox_w_out, v_sg_w_in=v_sg_w_in, v_sg_ln_g=v_sg_ln_g, v_sg_ln_b=v_sg_ln_b, v_sg_w_s=v_sg_w_s, v_sg_b_s=v_sg_b_s, v_sg_w_out=v_sg_w_out, v_cv_w_pw1=v_cv_w_pw1, v_cv_b_pw1=v_cv_b_pw1, v_cv_w_dw=v_cv_w_dw, v_cv_b_dw=v_cv_b_dw, v_cv_ln_g=v_cv_ln_g, v_cv_ln_b=v_cv_ln_b, v_cv_w_pw2=v_cv_w_pw2, v_cv_b_pw2=v_cv_b_pw2)
    weights = {n: given[n] for n in TWIN_WEIGHTS}
    shared = {n: given[n] for n in SHARED_INPUTS}
    per_example = {n: given[n] for n in ['x', 'c']}
    grad_fn = _jax.value_and_grad(_loss, argnums=(0, 1))

    def one_microbatch(ex, loss_target):
        ex = dict(ex)
        diff = ex.pop(TWIN_DIFF_INPUT)
        return grad_fn(weights, diff, {**shared, **ex}, loss_target)

    if N_MICROBATCH == 1:
        loss, (grad_w, grad_x) = one_microbatch(per_example, given["loss_target"])
    else:
        def body(carry, xs):
            loss_sum, grad_sum = carry
            l_k, (gw_k, gx_k) = one_microbatch(xs[0], xs[1])
            with _jax.named_scope("update"):
                return (loss_sum + l_k, _jax.tree.map(_jnp.add, grad_sum, gw_k)), gx_k

        init = (_jnp.zeros((), _jnp.float32), _jax.tree.map(_jnp.zeros_like, weights))
        (loss, grad_w), grad_x = _jax.lax.scan(body, init, (per_example, given["loss_target"]))
    with _jax.named_scope("update"):
        delta_w, new_m, new_v = {}, {}, {}
        for n in TWIN_WEIGHTS:
            delta_w[n], new_m[n], new_v[n] = _adamw(weights[n], grad_w[n], given["m_" + n], given["v_" + n])
    return (loss, grad_x, *[grad_w[n] for n in TWIN_WEIGHTS], *[delta_w[n] for n in TWIN_WEIGHTS],
            *[new_m[n] for n in TWIN_WEIGHTS], *[new_v[n] for n in TWIN_WEIGHTS])
```

```python
import functools
import math

import jax
import jax.numpy as jnp
from jax import lax
from jax.experimental import pallas as pl
from jax.experimental.pallas import tpu as pltpu

F32 = jnp.float32
BF16 = jnp.bfloat16

EPS = 1e-6
HEAD_DIM = 64
LANE = 128
CONV_WIDTH = 31
CONV_HALO = 32
SG_CHUNK = 128
SG_CAUSAL = 64
SG_GROUPS = 8
N_MIXERS = 3
VMEM_LIMIT = 56 * 1024 * 1024
NEG = -1e30

ADAM_LR = 0.001
ADAM_B1 = 0.9
ADAM_B2 = 0.999
ADAM_EPS = 1e-08
ADAM_WD = 0.01
ADAM_STEP = 10

MESH = pl.DeviceIdType.MESH
ANY = pl.BlockSpec(memory_space=pl.ANY)


def _cparams(*sem):
    return pltpu.CompilerParams(dimension_semantics=sem, vmem_limit_bytes=VMEM_LIMIT)


def _row_tile(t, want=512):
    return min(t, want)


def _matmul(a, b, *, name, ta=False, tb=False, tm=512, tn=1024, tk=1024,
            extras=(), epilogue=None, out_dtypes=(F32,)):
    M, K = (a.shape[1], a.shape[0]) if ta else a.shape
    N = b.shape[0] if tb else b.shape[1]
    assert (b.shape[1] if tb else b.shape[0]) == K
    tm, tn, tk = min(tm, M), min(tn, N), min(tk, K)
    assert M % tm == 0 and N % tn == 0 and K % tk == 0, (name, M, N, K, tm, tn, tk)
    nk = K // tk
    a_spec = (pl.BlockSpec((tk, tm), lambda i, j, k: (k, i)) if ta
              else pl.BlockSpec((tm, tk), lambda i, j, k: (i, k)))
    b_spec = (pl.BlockSpec((tn, tk), lambda i, j, k: (j, k)) if tb
              else pl.BlockSpec((tk, tn), lambda i, j, k: (k, j)))
    ex_specs = []
    for arr, kind in extras:
        if kind == 'tile':
            ex_specs.append(pl.BlockSpec((tm, tn), lambda i, j, k: (i, j)))
        else:
            ex_specs.append(pl.BlockSpec((1, tn), lambda i, j, k: (0, j)))
    dims = (((0,) if ta else (1,), (1,) if tb else (0,)), ((), ()))
    n_ex, n_out = len(extras), len(out_dtypes)

    def body(*refs):
        a_ref, b_ref = refs[0], refs[1]
        ex = refs[2:2 + n_ex]
        outs = refs[2 + n_ex:2 + n_ex + n_out]

        def finish(acc):
            vals = epilogue(acc, *[r[...] for r in ex]) if epilogue else (acc,)
            for o, v in zip(outs, vals):
                o[...] = v.astype(o.dtype)

        part = lax.dot_general(a_ref[...].astype(BF16), b_ref[...].astype(BF16), dims,
                               preferred_element_type=F32)
        if nk == 1:
            finish(part)
        else:
            acc_ref = refs[-1]
            k = pl.program_id(2)

            @pl.when(k == 0)
            def _():
                acc_ref[...] = part

            @pl.when(k > 0)
            def _():
                acc_ref[...] += part

            @pl.when(k == nk - 1)
            def _():
                finish(acc_ref[...])

    outs = pl.pallas_call(
        body, name=name,
        grid=(M // tm, N // tn, nk),
        in_specs=[a_spec, b_spec] + ex_specs,
        out_specs=[pl.BlockSpec((tm, tn), lambda i, j, k: (i, j)) for _ in out_dtypes],
        out_shape=[jax.ShapeDtypeStruct((M, N), dt) for dt in out_dtypes],
        scratch_shapes=[pltpu.VMEM((tm, tn), F32)] if nk > 1 else [],
        compiler_params=_cparams("parallel", "parallel", "arbitrary"),
    )(a, b, *[arr for arr, _ in extras])
    return outs if n_out > 1 else outs[0]


def _norm_mod_fwd(x, w, sc, sh, *, name):
    T, D = x.shape
    tr = _row_tile(T)

    def body(x_ref, w_ref, sc_ref, sh_ref, h_ref):
        xv = x_ref[...]
        r = lax.rsqrt(jnp.mean(xv * xv, axis=-1, keepdims=True) + EPS)
        h_ref[...] = ((xv * r) * w_ref[...] * (1.0 + sc_ref[...]) + sh_ref[...]).astype(BF16)

    row = pl.BlockSpec((1, D), lambda i: (0, 0))
    return pl.pallas_call(
        body, name=name, grid=(T // tr,),
        in_specs=[pl.BlockSpec((tr, D), lambda i: (i, 0)), row, row, row],
        out_specs=pl.BlockSpec((tr, D), lambda i: (i, 0)),
        out_shape=jax.ShapeDtypeStruct((T, D), BF16),
        compiler_params=_cparams("parallel"),
    )(x, w, sc, sh)


def _norm_mod_bwd(dh, x, dres, w, sc, *, name, gate=None):
    T, D = x.shape
    tr = _row_tile(T)
    with_gate = gate is not None

    def body(*refs):
        if with_gate:
            dh_ref, x_ref, dres_ref, w_ref, sc_ref, y_ref, g_ref, dx_ref, sums_ref, dy_ref = refs
        else:
            dh_ref, x_ref, dres_ref, w_ref, sc_ref, dx_ref, sums_ref = refs
        i = pl.program_id(0)
        xv, dhv = x_ref[...], dh_ref[...].astype(F32)
        r = lax.rsqrt(jnp.mean(xv * xv, axis=-1, keepdims=True) + EPS)
        n = xv * r
        wv, scale = w_ref[...], 1.0 + sc_ref[...]
        dn = dhv * (wv * scale)
        dx = dres_ref[...] + r * (dn - n * jnp.mean(dn * n, axis=-1, keepdims=True))
        dx_ref[...] = dx
        rows = [jnp.sum(dhv, axis=0, keepdims=True),
                jnp.sum(dhv * (n * wv), axis=0, keepdims=True),
                jnp.sum(dhv * n * scale, axis=0, keepdims=True)]
        if with_gate:
            dy_ref[...] = (dx * g_ref[...]).astype(BF16)
            rows.append(jnp.sum(dx * y_ref[...], axis=0, keepdims=True))
            rows.append(jnp.sum(dx * g_ref[...], axis=0, keepdims=True))
        part = jnp.concatenate(rows + [jnp.zeros((8 - len(rows), D), F32)], axis=0)

        @pl.when(i == 0)
        def _():
            sums_ref[...] = part

        @pl.when(i > 0)
        def _():
            sums_ref[...] += part

    blk = pl.BlockSpec((tr, D), lambda i: (i, 0))
    row = pl.BlockSpec((1, D), lambda i: (0, 0))
    in_specs = [blk, blk, blk, row, row]
    args = [dh, x, dres, w, sc]
    out_specs = [blk, pl.BlockSpec((8, D), lambda i: (0, 0))]
    out_shape = [jax.ShapeDtypeStruct((T, D), F32), jax.ShapeDtypeStruct((8, D), F32)]
    if with_gate:
        in_specs += [blk, row]
        args += list(gate)
        out_specs.append(blk)
        out_shape.append(jax.ShapeDtypeStruct((T, D), BF16))
    return pl.pallas_call(
        body, name=name, grid=(T // tr,), in_specs=in_specs, out_specs=out_specs,
        out_shape=out_shape, compiler_params=_cparams("arbitrary"),
    )(*args)


def _gate_bwd(dx, y, g, *, name):
    T, D = dx.shape
    tr = _row_tile(T)

    def body(dx_ref, y_ref, g_ref, dy_ref, dg_ref):
        i = pl.program_id(0)
        dxv = dx_ref[...]
        dy_ref[...] = (dxv * g_ref[...]).astype(BF16)
        part = jnp.concatenate([jnp.sum(dxv * y_ref[...], axis=0, keepdims=True),
                                jnp.zeros((7, D), F32)], axis=0)

        @pl.when(i == 0)
        def _():
            dg_ref[...] = part

        @pl.when(i > 0)
        def _():
            dg_ref[...] += part

    blk = pl.BlockSpec((tr, D), lambda i: (i, 0))
    return pl.pallas_call(
        body, name=name, grid=(T // tr,),
        in_specs=[blk, blk, pl.BlockSpec((1, D), lambda i: (0, 0))],
        out_specs=[blk, pl.BlockSpec((8, D), lambda i: (0, 0))],
        out_shape=[jax.ShapeDtypeStruct((T, D), BF16), jax.ShapeDtypeStruct((8, D), F32)],
        compiler_params=_cparams("arbitrary"),
    )(dx, y, g)


def _loss_head(y, target, *, name):
    T, D = y.shape
    tr = _row_tile(T)

    def body(y_ref, t_ref, loss_ref, dy_ref):
        i = pl.program_id(0)
        e = y_ref[...] - t_ref[...]
        dy_ref[...] = e * (1.0 / D)
        part = jnp.full((8, LANE), 0.5 / D * jnp.sum(e * e), F32)

        @pl.when(i == 0)
        def _():
            loss_ref[...] = part

        @pl.when(i > 0)
        def _():
            loss_ref[...] += part

    blk = pl.BlockSpec((tr, D), lambda i: (i, 0))
    return pl.pallas_call(
        body, name=name, grid=(T // tr,), in_specs=[blk, blk],
        out_specs=[pl.BlockSpec((8, LANE), lambda i: (0, 0)), blk],
        out_shape=[jax.ShapeDtypeStruct((8, LANE), F32), jax.ShapeDtypeStruct((T, D), F32)],
        compiler_params=_cparams("arbitrary"),
    )(y, target)


def _half_sums(x, lo):
    s_lo = jnp.sum(jnp.where(lo, x, 0.0), axis=-1, keepdims=True)
    s_hi = jnp.sum(jnp.where(lo, 0.0, x), axis=-1, keepdims=True)
    return jnp.where(lo, s_lo, s_hi)


def _fox_prep_fwd(proj, qg, kg, *, d_model, name):
    T = proj.shape[0]
    nhp = d_model // LANE
    tr = _row_tile(T)

    def body(q_ref, k_ref, v_ref, qg_ref, kg_ref, qo_ref, ko_ref, vo_ref):
        lo = lax.broadcasted_iota(jnp.int32, (tr, LANE), 1) < HEAD_DIM

        def norm(xv, g):
            ms = _half_sums(xv * xv, lo) * (1.0 / HEAD_DIM)
            return (xv * lax.rsqrt(ms + EPS)) * g

        qo_ref[...] = (norm(q_ref[...], qg_ref[...]) * (HEAD_DIM ** -0.5)).astype(BF16)
        ko_ref[...] = norm(k_ref[...], kg_ref[...]).astype(BF16)
        vo_ref[...] = v_ref[...].astype(BF16)

    gain = pl.BlockSpec((1, LANE), lambda i, h: (0, 0))
    out = pl.BlockSpec((tr, LANE), lambda i, h: (i, h))
    return pl.pallas_call(
        body, name=name, grid=(T // tr, nhp),
        in_specs=[pl.BlockSpec((tr, LANE), lambda i, h: (i, h)),
                  pl.BlockSpec((tr, LANE), lambda i, h: (i, nhp + h)),
                  pl.BlockSpec((tr, LANE), lambda i, h: (i, 2 * nhp + h)), gain, gain],
        out_specs=[out, out, out],
        out_shape=[jax.ShapeDtypeStruct((T, d_model), BF16)] * 3,
        compiler_params=_cparams("parallel", "parallel"),
    )(proj, proj, proj, qg, kg)


def _fox_prep_bwd(proj, dq, dk, dv, qg, kg, *, d_model, name):
    T = proj.shape[0]
    nhp = d_model // LANE
    tr = _row_tile(T)

    def body(q_ref, k_ref, dq_ref, dk_ref, dv_ref, qg_ref, kg_ref, dqo_ref, dko_ref, dvo_ref, sums_ref):
        first = (pl.program_id(0) == 0) & (pl.program_id(1) == 0)
        lo = lax.broadcasted_iota(jnp.int32, (tr, LANE), 1) < HEAD_DIM

        def bwd(xv, dxhat, g):
            ms = _half_sums(xv * xv, lo) * (1.0 / HEAD_DIM)
            r = lax.rsqrt(ms + EPS)
            n = xv * r
            dn = dxhat * g
            dx = r * (dn - n * (_half_sums(dn * n, lo) * (1.0 / HEAD_DIM)))
            dg = jnp.sum(dxhat * n, axis=0, keepdims=True)
            return dx, dg + pltpu.roll(dg, HEAD_DIM, 1)

        dxq, dgq = bwd(q_ref[...], dq_ref[...] * (HEAD_DIM ** -0.5), qg_ref[...])
        dxk, dgk = bwd(k_ref[...], dk_ref[...], kg_ref[...])
        dqo_ref[...] = dxq.astype(BF16)
        dko_ref[...] = dxk.astype(BF16)
        dvo_ref[...] = dv_ref[...].astype(BF16)
        part = jnp.concatenate([dgq, dgk, jnp.zeros((6, LANE), F32)], axis=0)

        @pl.when(first)
        def _():
            sums_ref[...] = part

        @pl.when(jnp.logical_not(first))
        def _():
            sums_ref[...] += part

    gain = pl.BlockSpec((1, LANE), lambda i, h: (0, 0))
    blk = pl.BlockSpec((tr, LANE), lambda i, h: (i, h))
    return pl.pallas_call(
        body, name=name, grid=(T // tr, nhp),
        in_specs=[blk, pl.BlockSpec((tr, LANE), lambda i, h: (i, nhp + h)), blk, blk, blk, gain, gain],
        out_specs=[blk, blk, blk, pl.BlockSpec((8, LANE), lambda i, h: (0, 0))],
        out_shape=[jax.ShapeDtypeStruct((T, d_model), BF16)] * 3 + [jax.ShapeDtypeStruct((8, LANE), F32)],
        compiler_params=_cparams("arbitrary", "arbitrary"),
    )(proj, proj, dq, dk, dv, qg, kg)


def _scan_lanes(x, reverse):
    n = x.shape[-1]
    lane = lax.broadcasted_iota(jnp.int32, x.shape, 1)
    sh = 1
    while sh < n:
        if reverse:
            x = x + jnp.where(lane < n - sh, pltpu.roll(x, n - sh, 1), 0.0)
        else:
            x = x + jnp.where(lane >= sh, pltpu.roll(x, sh, 1), 0.0)
        sh *= 2
    return x


def _fox_gate_fwd(fpre_t, bf, *, name):
    def body(f_ref, b_ref, o_ref):
        xv = f_ref[...] + b_ref[...]
        logf = jnp.minimum(xv, 0.0) - jnp.log1p(jnp.exp(-jnp.abs(xv)))
        o_ref[...] = _scan_lanes(logf, reverse=False)

    return pl.pallas_call(body, name=name, out_shape=jax.ShapeDtypeStruct(fpre_t.shape, F32))(fpre_t, bf)


def _fox_gate_bwd(dcol, drow, fpre_t, bf, *, name):
    H = fpre_t.shape[0]

    def body(dc_ref, dr_ref, f_ref, b_ref, o_ref, db_ref):
        xv = f_ref[...] + b_ref[...]
        e = dc_ref[...] - dr_ref[...]
        dlogf = _scan_lanes(e, reverse=False) - e
        dpre = dlogf * (1.0 - jax.nn.sigmoid(xv))
        o_ref[...] = dpre
        db_ref[...] = jnp.broadcast_to(jnp.sum(dpre, axis=-1, keepdims=True), (H, LANE))

    return pl.pallas_call(
        body, name=name,
        out_shape=[jax.ShapeDtypeStruct(fpre_t.shape, F32), jax.ShapeDtypeStruct((H, LANE), F32)],
    )(dcol, drow, fpre_t, bf)


_NT = (((1,), (1,)), ((), ()))
_TN = (((0,), (0,)), ((), ()))
_NN = (((1,), (0,)), ((), ()))


def _attn_tile(T):
    return min(T, 512)


def _fox_attn_fwd(q, k, v, fcol, frow, *, name):
    T, D = q.shape
    nhp = D // LANE
    tq = tk = _attn_tile(T)
    nq = T // tq

    def body(q_ref, k_ref, v_ref, fc_ref, fr_ref, o_ref, lse_ref, m_sc, l_sc, acc_sc):
        i, j = pl.program_id(1), pl.program_id(2)

        @pl.when(j == 0)
        def _():
            m_sc[...] = jnp.full(m_sc.shape, NEG, F32)
            l_sc[...] = jnp.zeros(l_sc.shape, F32)
            acc_sc[...] = jnp.zeros(acc_sc.shape, F32)

        def block(diagonal):
            for h in range(2):
                sl = slice(h * HEAD_DIM, (h + 1) * HEAD_DIM)
                s = lax.dot_general(q_ref[:, sl], k_ref[:, sl], _NT, preferred_element_type=F32)
                s = s + (fc_ref[h] - fr_ref[h])
                if diagonal:
                    row = lax.broadcasted_iota(jnp.int32, (tq, tk), 0)
                    col = lax.broadcasted_iota(jnp.int32, (tq, tk), 1)
                    s = jnp.where(col <= row, s, NEG)
                m_prev = m_sc[h]
                m_new = jnp.maximum(m_prev, jnp.max(s, axis=-1, keepdims=True))
                alpha = jnp.exp(m_prev - m_new)
                p = jnp.exp(s - m_new)
                l_sc[h] = alpha * l_sc[h] + jnp.sum(p, axis=-1, keepdims=True)
                m_sc[h] = m_new
                pv = lax.dot_general(p.astype(BF16), v_ref[:, sl], _NN, preferred_element_type=F32)
                acc_sc[h] = alpha * acc_sc[h] + pv

        @pl.when(j < i)
        def _():
            block(False)

        @pl.when(j == i)
        def _():
            block(True)
            o_ref[...] = jnp.concatenate([acc_sc[0] / l_sc[0], acc_sc[1] / l_sc[1]], axis=1)
            for h in range(2):
                lse_ref[h] = m_sc[h] + jnp.log(l_sc[h])

    return pl.pallas_call(
        body, name=name, grid=(nhp, nq, nq),
        in_specs=[pl.BlockSpec((tq, LANE), lambda h, i, j: (i, h)),
                  pl.BlockSpec((tk, LANE), lambda h, i, j: (jnp.minimum(j, i), h)),
                  pl.BlockSpec((tk, LANE), lambda h, i, j: (jnp.minimum(j, i), h)),
                  pl.BlockSpec((2, tq, 1), lambda h, i, j: (h, i, 0)),
                  pl.BlockSpec((2, 1, tk), lambda h, i, j: (h, 0, jnp.minimum(j, i)))],
        out_specs=[pl.BlockSpec((tq, LANE), lambda h, i, j: (i, h)),
                   pl.BlockSpec((2, tq, 1), lambda h, i, j: (h, i, 0))],
        out_shape=[jax.ShapeDtypeStruct((T, D), F32), jax.ShapeDtypeStruct((2 * nhp, T, 1), F32)],
        scratch_shapes=[pltpu.VMEM((2, tq, 1), F32), pltpu.VMEM((2, tq, 1), F32),
                        pltpu.VMEM((2, tq, HEAD_DIM), F32)],
        compiler_params=_cparams("parallel", "parallel", "arbitrary"),
    )(q, k, v, fcol, frow)


def _fox_attn_bwd(q, k, v, o, do, lse, fcol, frow, *, name):
    T, D = q.shape
    nhp = D // LANE
    tq = tk = _attn_tile(T)
    nq = T // tq

    def body(q_ref, k_ref, v_ref, o_ref, do_ref, lse_ref, fc_ref, fr_ref,
             dq_ref, dk_ref, dv_ref, df_ref, drow_ref, dk_sc, dv_sc, df_sc):
        j, i = pl.program_id(1), pl.program_id(2)

        @pl.when((j == 0) & (i == 0))
        def _():
            dq_ref[...] = jnp.zeros(dq_ref.shape, F32)
            drow_ref[...] = jnp.zeros(drow_ref.shape, F32)

        @pl.when(i == 0)
        def _():
            dk_sc[...] = jnp.zeros(dk_sc.shape, F32)
            dv_sc[...] = jnp.zeros(dv_sc.shape, F32)
            df_sc[...] = jnp.zeros(df_sc.shape, F32)

        def block(diagonal):
            dq_parts = []
            rows = pl.ds(pl.multiple_of(i * tq, tq), tq)
            for h in range(2):
                sl = slice(h * HEAD_DIM, (h + 1) * HEAD_DIM)
                qh, kh, vh = q_ref[:, sl], k_ref[:, sl], v_ref[:, sl]
                dob = do_ref[:, sl].astype(BF16)
                delta = jnp.sum(dob.astype(F32) * o_ref[:, sl], axis=-1, keepdims=True)
                s = lax.dot_general(qh, kh, _NT, preferred_element_type=F32) + (fc_ref[h] - fr_ref[h])
                p = jnp.exp(s - lse_ref[h])
                if diagonal:
                    row = lax.broadcasted_iota(jnp.int32, (tq, tk), 0)
                    col = lax.broadcasted_iota(jnp.int32, (tq, tk), 1)
                    p = jnp.where(col <= row, p, 0.0)
                dp = lax.dot_general(dob, vh, _NT, preferred_element_type=F32)
                dl = p * (dp - delta)
                dlb = dl.astype(BF16)
                dv_sc[h] += lax.dot_general(p.astype(BF16), dob, _TN, preferred_element_type=F32)
                dk_sc[h] += lax.dot_general(dlb, qh, _TN, preferred_element_type=F32)
                df_sc[h] += jnp.sum(dl, axis=0, keepdims=True)
                drow_ref[h, rows, :] += jnp.sum(dl, axis=1, keepdims=True)
                dq_parts.append(lax.dot_general(dlb, kh, _NN, preferred_element_type=F32))
            dq_ref[rows, :] += jnp.concatenate(dq_parts, axis=1)

        @pl.when(i > j)
        def _():
            block(False)

        @pl.when(i == j)
        def _():
            block(True)

        @pl.when(i == nq - 1)
        def _():
            dk_ref[...] = jnp.concatenate([dk_sc[0], dk_sc[1]], axis=1)
            dv_ref[...] = jnp.concatenate([dv_sc[0], dv_sc[1]], axis=1)
            df_ref[...] = df_sc[...]

    qblk = pl.BlockSpec((tq, LANE), lambda h, j, i: (jnp.maximum(i, j), h))
    kblk = pl.BlockSpec((tk, LANE), lambda h, j, i: (j, h))
    qcol = pl.BlockSpec((2, tq, 1), lambda h, j, i: (h, jnp.maximum(i, j), 0))
    krow = pl.BlockSpec((2, 1, tk), lambda h, j, i: (h, 0, j))
    return pl.pallas_call(
        body, name=name, grid=(nhp, nq, nq),
        in_specs=[qblk, kblk, kblk, qblk, qblk, qcol, qcol, krow],
        out_specs=[pl.BlockSpec((T, LANE), lambda h, j, i: (0, h)), kblk, kblk, krow,
                   pl.BlockSpec((2, T, 1), lambda h, j, i: (h, 0, 0))],
        out_shape=[jax.ShapeDtypeStruct((T, D), F32)] * 3 + [jax.ShapeDtypeStruct((2 * nhp, 1, T), F32),
                                                              jax.ShapeDtypeStruct((2 * nhp, T, 1), F32)],
        scratch_shapes=[pltpu.VMEM((2, tk, HEAD_DIM), F32), pltpu.VMEM((2, tk, HEAD_DIM), F32),
                        pltpu.VMEM((2, 1, tk), F32)],
        compiler_params=_cparams("parallel", "arbitrary", "arbitrary"),
    )(q, k, v, o, do, lse, fcol, frow)


_GELU_C = math.sqrt(2.0 / math.pi)
_GELU_A = 0.044715


def _gelu(x):
    t = jnp.tanh(_GELU_C * (x + _GELU_A * (x * x * x)))
    return x * (0.5 * (1.0 + t)), t


def _gelu_grad(x, t):
    return 0.5 * (1.0 + t) + 0.5 * x * (1.0 - t * t) * (_GELU_C * (1.0 + 3.0 * _GELU_A * x * x))


def _layer_norm_stats(v):
    mu = jnp.mean(v, axis=-1, keepdims=True)
    vc = v - mu
    rstd = lax.rsqrt(jnp.mean(vc * vc, axis=-1, keepdims=True) + EPS)
    return vc * rstd, rstd


def _layer_norm_bwd(dyhat, yhat, rstd):
    return rstd * (dyhat - jnp.mean(dyhat, axis=-1, keepdims=True)
                   - yhat * jnp.mean(dyhat * yhat, axis=-1, keepdims=True))


def _sg_mask():
    t = lax.broadcasted_iota(jnp.int32, (SG_CHUNK, SG_CHUNK), 0) // SG_CAUSAL
    s = lax.broadcasted_iota(jnp.int32, (SG_CHUNK, SG_CHUNK), 1) // SG_CAUSAL
    return s <= t


def _sg_mix(ws_ref, bc_ref, vln_sc, vo_sc, tr, gd):
    mask = _sg_mask()
    for g in range(SG_GROUPS):
        wg = jnp.where(mask, ws_ref[g], 0.0).astype(BF16)
        cols = slice(g * gd, (g + 1) * gd)
        for n in range(tr // SG_CHUNK):
            rows = slice(n * SG_CHUNK, (n + 1) * SG_CHUNK)
            vo_sc[rows, cols] = lax.dot_general(wg, vln_sc[rows, cols], _NN,
                                                preferred_element_type=F32) + bc_ref[g]


def _sg_fwd(a_uv, ln_g, ln_b, ws, bcol, *, name):
    T, W = a_uv.shape[0], a_uv.shape[1] // 2
    gd = W // SG_GROUPS
    tr = _row_tile(T)

    def body(u_ref, v_ref, g_ref, b_ref, ws_ref, bc_ref, o_ref, vln_sc, vo_sc):
        u, _ = _gelu(u_ref[...])
        v, _ = _gelu(v_ref[...])
        vhat, _ = _layer_norm_stats(v)
        vln_sc[...] = (vhat * g_ref[...] + b_ref[...]).astype(BF16)
        _sg_mix(ws_ref, bc_ref, vln_sc, vo_sc, tr, gd)
        o_ref[...] = (u * vo_sc[...]).astype(BF16)

    row = pl.BlockSpec((1, W), lambda i: (0, 0))
    return pl.pallas_call(
        body, name=name, grid=(T // tr,),
        in_specs=[pl.BlockSpec((tr, W), lambda i: (i, 0)), pl.BlockSpec((tr, W), lambda i: (i, 1)), row, row,
                  pl.BlockSpec((SG_GROUPS, SG_CHUNK, SG_CHUNK), lambda i: (0, 0, 0)),
                  pl.BlockSpec((SG_GROUPS, SG_CHUNK, 1), lambda i: (0, 0, 0))],
        out_specs=pl.BlockSpec((tr, W), lambda i: (i, 0)),
        out_shape=jax.ShapeDtypeStruct((T, W), BF16),
        scratch_shapes=[pltpu.VMEM((tr, W), BF16), pltpu.VMEM((tr, W), F32)],
        compiler_params=_cparams("parallel"),
    )(a_uv, a_uv, ln_g, ln_b, ws, bcol)


def _sg_bwd(a_uv, dgate, ln_g, ln_b, ws, bcol, *, name):
    T, W = a_uv.shape[0], a_uv.shape[1] // 2
    gd = W // SG_GROUPS
    tr = _row_tile(T)

    def body(u_ref, v_ref, dg_ref, g_ref, b_ref, ws_ref, bc_ref,
             da_ref, dws_ref, dbs_ref, sums_ref, vln_sc, vo_sc, dvo_sc, dvln_sc):
        i = pl.program_id(0)

        @pl.when(i == 0)
        def _():
            dws_ref[...] = jnp.zeros(dws_ref.shape, F32)
            dbs_ref[...] = jnp.zeros(dbs_ref.shape, F32)
            sums_ref[...] = jnp.zeros(sums_ref.shape, F32)

        ua, va = u_ref[...], v_ref[...]
        u, tu = _gelu(ua)
        v, tv = _gelu(va)
        vhat, rstd = _layer_norm_stats(v)
        vln_sc[...] = (vhat * g_ref[...] + b_ref[...]).astype(BF16)
        _sg_mix(ws_ref, bc_ref, vln_sc, vo_sc, tr, gd)
        dgt = dg_ref[...]
        du = dgt * vo_sc[...]
        dvo_sc[...] = dgt * u
        mask = _sg_mask()
        for g in range(SG_GROUPS):
            wg = jnp.where(mask, ws_ref[g], 0.0).astype(BF16)
            cols = slice(g * gd, (g + 1) * gd)
            acc_w = jnp.zeros((SG_CHUNK, SG_CHUNK), F32)
            acc_b = jnp.zeros((SG_CHUNK, 1), F32)
            for n in range(tr // SG_CHUNK):
                rows = slice(n * SG_CHUNK, (n + 1) * SG_CHUNK)
                dvo = dvo_sc[rows, cols]
                dvob = dvo.astype(BF16)
                dvln_sc[rows, cols] = lax.dot_general(wg, dvob, _TN, preferred_element_type=F32)
                acc_w += lax.dot_general(dvob, vln_sc[rows, cols], _NT, preferred_element_type=F32)
                acc_b += jnp.sum(dvo, axis=1, keepdims=True)
            dws_ref[g] += jnp.where(mask, acc_w, 0.0)
            dbs_ref[g] += acc_b
        dvln = dvln_sc[...]
        sums_ref[...] += jnp.concatenate([jnp.sum(dvln * vhat, axis=0, keepdims=True),
                                          jnp.sum(dvln, axis=0, keepdims=True),
                                          jnp.zeros((6, W), F32)], axis=0)
        dv = _layer_norm_bwd(dvln * g_ref[...], vhat, rstd)
        da_ref[:, :W] = (du * _gelu_grad(ua, tu)).astype(BF16)
        da_ref[:, W:] = (dv * _gelu_grad(va, tv)).astype(BF16)

    row = pl.BlockSpec((1, W), lambda i: (0, 0))
    wspec = pl.BlockSpec((SG_GROUPS, SG_CHUNK, SG_CHUNK), lambda i: (0, 0, 0))
    bspec = pl.BlockSpec((SG_GROUPS, SG_CHUNK, 1), lambda i: (0, 0, 0))
    return pl.pallas_call(
        body, name=name, grid=(T // tr,),
        in_specs=[pl.BlockSpec((tr, W), lambda i: (i, 0)), pl.BlockSpec((tr, W), lambda i: (i, 1)),
                  pl.BlockSpec((tr, W), lambda i: (i, 0)), row, row, wspec, bspec],
        out_specs=[pl.BlockSpec((tr, 2 * W), lambda i: (i, 0)), wspec, bspec,
                   pl.BlockSpec((8, W), lambda i: (0, 0))],
        out_shape=[jax.ShapeDtypeStruct((T, 2 * W), BF16),
                   jax.ShapeDtypeStruct((SG_GROUPS, SG_CHUNK, SG_CHUNK), F32),
                   jax.ShapeDtypeStruct((SG_GROUPS, SG_CHUNK, 1), F32),
                   jax.ShapeDtypeStruct((8, W), F32)],
        scratch_shapes=[pltpu.VMEM((tr, W), BF16), pltpu.VMEM((tr, W), F32),
                        pltpu.VMEM((tr, W), F32), pltpu.VMEM((tr, W), F32)],
        compiler_params=_cparams("arbitrary"),
    )(a_uv, a_uv, dgate, ln_g, ln_b, ws, bcol)


def _cv_glu_conv(a_ref, b_ref, ap_ref, bp_ref, w_ref, bd_ref, xc_sc, tr):
    i = pl.program_id(0)
    prev = ap_ref[...] * jax.nn.sigmoid(bp_ref[...])
    xc_sc[0:CONV_HALO, :] = jnp.where(i > 0, prev, 0.0)
    xc_sc[CONV_HALO:, :] = a_ref[...] * jax.nn.sigmoid(b_ref[...])
    acc = jnp.broadcast_to(bd_ref[...], (tr, bd_ref.shape[1]))
    for k in range(CONV_WIDTH):
        acc = acc + w_ref[k:k + 1, :] * xc_sc[pl.ds(CONV_HALO - (CONV_WIDTH - 1) + k, tr), :]
    return acc


def _cv_specs(T, C, tr):
    hb = tr // CONV_HALO
    cur = lambda col: pl.BlockSpec((tr, C), lambda i: (i, col))
    prev = lambda col: pl.BlockSpec((CONV_HALO, C), lambda i: (jnp.maximum(i * hb - 1, 0), col))
    row = pl.BlockSpec((1, C), lambda i: (0, 0))
    wspec = pl.BlockSpec((CONV_HALO, C), lambda i: (0, 0))
    return cur, prev, row, wspec


def _cv_fwd(p, w_dw, b_dw, ln_g, ln_b, *, name):
    T, C = p.shape[0], p.shape[1] // 2
    tr = _row_tile(T)
    cur, prev, row, wspec = _cv_specs(T, C, tr)

    def body(a_ref, b_ref, ap_ref, bp_ref, w_ref, bd_ref, g_ref, be_ref, o_ref, xc_sc):
        y2 = _cv_glu_conv(a_ref, b_ref, ap_ref, bp_ref, w_ref, bd_ref, xc_sc, tr)
        yhat, _ = _layer_norm_stats(y2)
        yln = yhat * g_ref[...] + be_ref[...]
        o_ref[...] = (yln * jax.nn.sigmoid(yln)).astype(BF16)

    return pl.pallas_call(
        body, name=name, grid=(T // tr,),
        in_specs=[cur(0), cur(1), prev(0), prev(1), wspec, row, row, row],
        out_specs=pl.BlockSpec((tr, C), lambda i: (i, 0)),
        out_shape=jax.ShapeDtypeStruct((T, C), BF16),
        scratch_shapes=[pltpu.VMEM((tr + CONV_HALO, C), F32)],
        compiler_params=_cparams("parallel"),
    )(p, p, p, p, w_dw, b_dw, ln_g, ln_b)


def _cv_bwd_ln(p, dy3, w_dw, b_dw, ln_g, ln_b, *, name):
    T, C = p.shape[0], p.shape[1] // 2
    tr = _row_tile(T)
    cur, prev, row, wspec = _cv_specs(T, C, tr)

    def body(a_ref, b_ref, ap_ref, bp_ref, dy_ref, w_ref, bd_ref, g_ref, be_ref,
             dy2_ref, dw_ref, sums_ref, xc_sc):
        i = pl.program_id(0)
        y2 = _cv_glu_conv(a_ref, b_ref, ap_ref, bp_ref, w_ref, bd_ref, xc_sc, tr)
        yhat, rstd = _layer_norm_stats(y2)
        yln = yhat * g_ref[...] + be_ref[...]
        s = jax.nn.sigmoid(yln)
        dyln = dy_ref[...] * (s + yln * s * (1.0 - s))
        dy2 = _layer_norm_bwd(dyln * g_ref[...], yhat, rstd)
        dy2_ref[...] = dy2
        sums = jnp.concatenate([jnp.sum(dy2, axis=0, keepdims=True),
                                jnp.sum(dyln * yhat, axis=0, keepdims=True),
                                jnp.sum(dyln, axis=0, keepdims=True),
                                jnp.zeros((5, C), F32)], axis=0)
        taps = [jnp.sum(dy2 * xc_sc[pl.ds(CONV_HALO - (CONV_WIDTH - 1) + k, tr), :], axis=0, keepdims=True)
                for k in range(CONV_WIDTH)]
        dw = jnp.concatenate(taps + [jnp.zeros((CONV_HALO - CONV_WIDTH, C), F32)], axis=0)

        @pl.when(i == 0)
        def _():
            sums_ref[...] = sums
            dw_ref[...] = dw

        @pl.when(i > 0)
        def _():
            sums_ref[...] += sums
            dw_ref[...] += dw

    blk = pl.BlockSpec((tr, C), lambda i: (i, 0))
    return pl.pallas_call(
        body, name=name, grid=(T // tr,),
        in_specs=[cur(0), cur(1), prev(0), prev(1), blk, wspec, row, row, row],
        out_specs=[blk, wspec, pl.BlockSpec((8, C), lambda i: (0, 0))],
        out_shape=[jax.ShapeDtypeStruct((T, C), F32), jax.ShapeDtypeStruct((CONV_HALO, C), F32),
                   jax.ShapeDtypeStruct((8, C), F32)],
        scratch_shapes=[pltpu.VMEM((tr + CONV_HALO, C), F32)],
        compiler_params=_cparams("arbitrary"),
    )(p, p, p, p, dy3, w_dw, b_dw, ln_g, ln_b)


def _cv_bwd_in(p, dy2, w_dw, *, name):
    T, C = p.shape[0], p.shape[1] // 2
    tr = _row_tile(T)
    hb = tr // CONV_HALO
    nblk = T // tr
    last_halo = T // CONV_HALO - 1

    def body(a_ref, b_ref, dy_ref, dyn_ref, w_ref, dp_ref, sums_ref, xc_sc):
        i = pl.program_id(0)
        xc_sc[0:tr, :] = dy_ref[...]
        xc_sc[tr:, :] = jnp.where(i < nblk - 1, dyn_ref[...], 0.0)
        dy1 = jnp.zeros((tr, C), F32)
        for k in range(CONV_WIDTH):
            dy1 = dy1 + w_ref[k:k + 1, :] * xc_sc[pl.ds(CONV_WIDTH - 1 - k, tr), :]
        a = a_ref[...]
        sb = jax.nn.sigmoid(b_ref[...])
        da = dy1 * sb
        db = dy1 * a * sb * (1.0 - sb)
        dp_ref[:, :C] = da.astype(BF16)
        dp_ref[:, C:] = db.astype(BF16)
        sums = jnp.concatenate([
            jnp.concatenate([jnp.sum(da, axis=0, keepdims=True), jnp.sum(db, axis=0, keepdims=True)], axis=1),
            jnp.zeros((7, 2 * C), F32)], axis=0)

        @pl.when(i == 0)
        def _():
            sums_ref[...] = sums

        @pl.when(i > 0)
        def _():
            sums_ref[...] += sums

    blk = lambda col: pl.BlockSpec((tr, C), lambda i: (i, col))
    return pl.pallas_call(
        body, name=name, grid=(nblk,),
        in_specs=[blk(0), blk(1), blk(0),
                  pl.BlockSpec((CONV_HALO, C), lambda i: (jnp.minimum((i + 1) * hb, last_halo), 0)),
                  pl.BlockSpec((CONV_HALO, C), lambda i: (0, 0))],
        out_specs=[pl.BlockSpec((tr, 2 * C), lambda i: (i, 0)), pl.BlockSpec((8, 2 * C), lambda i: (0, 0))],
        out_shape=[jax.ShapeDtypeStruct((T, 2 * C), BF16), jax.ShapeDtypeStruct((8, 2 * C), F32)],
        scratch_shapes=[pltpu.VMEM((tr + CONV_HALO, C), F32)],
        compiler_params=_cparams("arbitrary"),
    )(p, p, dy2, dy2, w_dw)


def _col_tile(n, want=1024):
    best = LANE
    for t in range(LANE, min(n, want) + 1, LANE):
        if n % t == 0:
            best = t
    return best if n % LANE == 0 else n


def _mm(a, b, *, name, ta=False, tb=False, **kw):
    M = a.shape[1] if ta else a.shape[0]
    N = b.shape[0] if tb else b.shape[1]
    K = a.shape[0] if ta else a.shape[1]
    kw.setdefault('tm', _col_tile(M, 1024 if ta else 512))
    kw.setdefault('tn', _col_tile(N, 1024))
    kw.setdefault('tk', _col_tile(K, 512 if ta else 1024))
    return _matmul(a, b, name=name, ta=ta, tb=tb, **kw)


def _relu2_epilogue(acc):
    r = jnp.maximum(acc, 0.0)
    return acc, r * r


def _residual_epilogue(acc, x, g):
    return acc, x + g * acc


def _residual_bias_epilogue(acc, x, g, b):
    y = acc + b
    return y, x + g * y


def _relu2_bwd_epilogue(acc, a):
    return (acc * (2.0 * jnp.maximum(a, 0.0)),)


def _bias_epilogue(acc, b):
    return (acc + b,)


def _fox_forward(h1, P, j, D):
    H = D // HEAD_DIM
    proj = _mm(h1, P['fox_w_in'][j], name='fox_proj')
    qg = jnp.tile(P['fox_q_norm'][j][None, :], (1, 2))
    kg = jnp.tile(P['fox_k_norm'][j][None, :], (1, 2))
    q, k, v = _fox_prep_fwd(proj, qg, kg, d_model=D, name='fox_prep_fwd')
    fpre_t = proj[:, 3 * D:3 * D + H].T
    bf = P['fox_b_f'][j][:, None]
    fcum = _fox_gate_fwd(fpre_t, bf, name='fox_gate_fwd')
    fcol, frow = fcum[:, :, None], fcum[:, None, :]
    o, lse = _fox_attn_fwd(q, k, v, fcol, frow, name='fox_attn_fwd')
    saved = dict(proj=proj, qg=qg, kg=kg, q=q, k=k, v=v, fpre_t=fpre_t, bf=bf, fcol=fcol, frow=frow,
                 o=o, lse=lse)
    return o, saved


def _fox_backward(dy, h1, S, P, j, D):
    H = D // HEAD_DIM
    T = dy.shape[0]
    w_out, w_in = P['fox_w_out'][j], P['fox_w_in'][j]
    g = {}
    g['fox_w_out'] = _mm(S['o'], dy, ta=True, name='fox_dw_out')
    do = _mm(dy, w_out, tb=True, name='fox_do')
    dq, dk, dv, dcol, drow = _fox_attn_bwd(S['q'], S['k'], S['v'], S['o'], do, S['lse'], S['fcol'], S['frow'],
                                   name='fox_attn_bwd')
    dqp, dkp, dvp, gsum = _fox_prep_bwd(S['proj'], dq, dk, dv, S['qg'], S['kg'], d_model=D, name='fox_prep_bwd')
    dfpre_t, dbf = _fox_gate_bwd(dcol[:, 0, :], drow[:, :, 0], S['fpre_t'], S['bf'], name='fox_gate_bwd')
    dfpre = jnp.pad(dfpre_t.T.astype(BF16), ((0, 0), (0, LANE - H)))
    dproj = jnp.concatenate([dqp, dkp, dvp, dfpre], axis=1)
    g['fox_w_in'] = _mm(h1, dproj, ta=True, name='fox_dw_in')[:, :3 * D + H]
    g['fox_b_f'] = dbf[:, 0]
    g['fox_q_norm'] = gsum[0, :HEAD_DIM]
    g['fox_k_norm'] = gsum[1, :HEAD_DIM]
    dh1 = _mm(dproj, w_in, tb=True, name='fox_dh')
    return dh1, g


def _sg_forward(h1, P, D):
    a_uv = _mm(h1, P['sg_w_in'], name='sg_in')
    bcol = P['sg_b_s'][:, :, None]
    gate = _sg_fwd(a_uv, P['sg_ln_g'], P['sg_ln_b'], P['sg_w_s'], bcol, name='sg_fwd')
    return gate, dict(a_uv=a_uv, bcol=bcol, gate=gate)


def _sg_backward(dy, h1, S, P, D):
    g = {}
    g['sg_w_out'] = _mm(S['gate'], dy, ta=True, name='sg_dw_out')
    dgate = _mm(dy, P['sg_w_out'], tb=True, name='sg_dgate')
    da, dws, dbs, sums = _sg_bwd(S['a_uv'], dgate, P['sg_ln_g'], P['sg_ln_b'], P['sg_w_s'], S['bcol'],
                                 name='sg_bwd')
    g['sg_w_s'], g['sg_b_s'] = dws, dbs[:, :, 0]
    g['sg_ln_g'], g['sg_ln_b'] = sums[0], sums[1]
    g['sg_w_in'] = _mm(h1, da, ta=True, name='sg_dw_in')
    dh1 = _mm(da, P['sg_w_in'], tb=True, name='sg_dh')
    return dh1, g


def _cv_forward(h1, P, D):
    p = _mm(h1, P['cv_w_pw1'], name='cv_pw1', extras=[(P['cv_b_pw1'], 'row')], epilogue=_bias_epilogue)
    w_dw = jnp.pad(P['cv_w_dw'], ((0, CONV_HALO - CONV_WIDTH), (0, 0)))
    y3 = _cv_fwd(p, w_dw, P['cv_b_dw'], P['cv_ln_g'], P['cv_ln_b'], name='cv_fwd')
    return y3, dict(p=p, w_dw=w_dw, y3=y3)


def _cv_backward(dy, h1, S, P, D):
    g = {}
    g['cv_w_pw2'] = _mm(S['y3'], dy, ta=True, name='cv_dw_pw2')
    dy3 = _mm(dy, P['cv_w_pw2'], tb=True, name='cv_dy3')
    dy2, dw, sums = _cv_bwd_ln(S['p'], dy3, S['w_dw'], P['cv_b_dw'], P['cv_ln_g'], P['cv_ln_b'], name='cv_bwd_ln')
    g['cv_w_dw'] = dw[:CONV_WIDTH]
    g['cv_b_dw'], g['cv_ln_g'], g['cv_ln_b'] = sums[0], sums[1], sums[2]
    dp, psum = _cv_bwd_in(S['p'], dy2, S['w_dw'], name='cv_bwd_in')
    g['cv_b_pw1'] = psum[0]
    g['cv_w_pw1'] = _mm(h1, dp, ta=True, name='cv_dw_pw1')
    dh1 = _mm(dp, P['cv_w_pw1'], tb=True, name='cv_dh')
    return dh1, g


def _local_step(x, target, mod, P):
    T, D = x.shape
    L = mod.shape[0]
    saved = []
    for i in range(L):
        kind, j = i % N_MIXERS, i // N_MIXERS
        m = [mod[i:i + 1, k * D:(k + 1) * D] for k in range(6)]
        sh_m, sc_m, g_m, sh_f, sc_f, g_f = m
        w_mix, w_mlp = P['norm_mix'][i:i + 1], P['norm_mlp'][i:i + 1]
        h1 = _norm_mod_fwd(x, w_mix, sc_m, sh_m, name='norm_mix_fwd')
        if kind == 0:
            op, S = _fox_forward(h1, P, j, D)
            y, x1 = _mm(op, P['fox_w_out'][j], name='fox_out', extras=[(x, 'tile'), (g_m, 'row')],
                        epilogue=_residual_epilogue, out_dtypes=(F32, F32))
        elif kind == 1:
            op, S = _sg_forward(h1, P, D)
            y, x1 = _mm(op, P['sg_w_out'], name='sg_out', extras=[(x, 'tile'), (g_m, 'row')],
                        epilogue=_residual_epilogue, out_dtypes=(F32, F32))
        else:
            op, S = _cv_forward(h1, P, D)
            y, x1 = _mm(op, P['cv_w_pw2'], name='cv_out',
                        extras=[(x, 'tile'), (g_m, 'row'), (P['cv_b_pw2'], 'row')],
                        epilogue=_residual_bias_epilogue, out_dtypes=(F32, F32))
        h2 = _norm_mod_fwd(x1, w_mlp, sc_f, sh_f, name='norm_mlp_fwd')
        a, r = _mm(h2, P['w_mlp_in'][i], name='mlp_in', epilogue=_relu2_epilogue, out_dtypes=(F32, BF16))
        z, x2 = _mm(r, P['w_mlp_out'][i], name='mlp_out', extras=[(x1, 'tile'), (g_f, 'row')],
                    epilogue=_residual_epilogue, out_dtypes=(F32, F32))
        saved.append(dict(x=x, h1=h1, S=S, y=y, x1=x1, h2=h2, a=a, r=r, z=z, m=m))
        x = x2

    loss_part, dx = _loss_head(x, target, name='loss_head')

    grads = {k: [None] * L for k in ('w_mlp_in', 'w_mlp_out', 'norm_mix', 'norm_mlp')}
    mix_grads = {}
    dmod = [None] * L
    for i in reversed(range(L)):
        kind, j = i % N_MIXERS, i // N_MIXERS
        sv = saved[i]
        sh_m, sc_m, g_m, sh_f, sc_f, g_f = sv['m']
        w_mix, w_mlp = P['norm_mix'][i:i + 1], P['norm_mlp'][i:i + 1]
        dz, dgf = _gate_bwd(dx, sv['z'], g_f, name='mlp_gate_bwd')
        grads['w_mlp_out'][i] = _mm(sv['r'], dz, ta=True, name='mlp_dw_out')
        da = _mm(dz, P['w_mlp_out'][i], tb=True, name='mlp_da', extras=[(sv['a'], 'tile')],
                 epilogue=_relu2_bwd_epilogue, out_dtypes=(BF16,))
        grads['w_mlp_in'][i] = _mm(sv['h2'], da, ta=True, name='mlp_dw_in')
        dh2 = _mm(da, P['w_mlp_in'][i], tb=True, name='mlp_dh')
        dx1, sums_f, dy = _norm_mod_bwd(dh2, sv['x1'], dx, w_mlp, sc_f, name='norm_mlp_bwd', gate=(sv['y'], g_m))
        if kind == 0:
            dh1, g = _fox_backward(dy, sv['h1'], sv['S'], P, j, D)
        elif kind == 1:
            dh1, g = _sg_backward(dy, sv['h1'], sv['S'], P, D)
        else:
            dh1, g = _cv_backward(dy, sv['h1'], sv['S'], P, D)
            g['cv_b_pw2'] = sums_f[4]
        for k, val in g.items():
            mix_grads.setdefault(k, {})[j] = val
        dx, sums_m = _norm_mod_bwd(dh1, sv['x'], dx1, w_mix, sc_m, name='norm_mix_bwd')
        grads['norm_mlp'][i], grads['norm_mix'][i] = sums_f[2], sums_m[2]
        dmod[i] = jnp.concatenate([sums_m[0], sums_m[1], sums_f[3], sums_f[0], sums_f[1], dgf[0]])

    out = {k: jnp.stack(v) for k, v in grads.items()}
    for k, per_j in mix_grads.items():
        out[k] = jnp.stack([per_j[j] for j in sorted(per_j)])
    return loss_part, dx, jnp.stack(dmod), out


def _position():
    return lax.axis_index("x"), lax.axis_index("y"), lax.axis_index("c")


def _all_gather8(blocks, *, name):
    n = len(blocks)

    def body(*refs):
        x_refs, out_refs = refs[:n], refs[n:2 * n]
        send_sems, recv_sems, local_sems = refs[2 * n:]
        x, y, c = _position()
        me, sibling = (x, y, c), (x, y, 1 - c)
        chips = [(1 - x, y), (x, 1 - y), (1 - x, 1 - y)]

        def slot(a, px, py, pc):
            return out_refs[a].at[4 * px + 2 * py + pc]

        def copy(a, k, blk, to, src=None):
            return pltpu.make_async_remote_copy(
                src_ref=slot(a, *blk) if src is None else src, dst_ref=slot(a, *blk),
                send_sem=send_sems.at[7 * a + k], recv_sem=recv_sems.at[7 * a + k],
                device_id=to, device_id_type=MESH)

        mine = [pltpu.make_async_copy(x_refs[a], slot(a, *me), local_sems.at[a]) for a in range(n)]
        for cp in mine:
            cp.start()
        first = []
        for j, chip in enumerate(chips):
            first += [copy(a, 1 + j, me, (*chip, c), src=x_refs[a]) for a in range(n)]
        first += [copy(a, 0, me, sibling, src=x_refs[a]) for a in range(n)]
        for cp in first:
            cp.start()
        passed = []
        for j, chip in enumerate(chips):
            for a in range(n):
                copy(a, 1 + j, (*chip, c), me).wait_recv()
                passed.append(copy(a, 4 + j, (*chip, c), sibling))
                passed[-1].start()
        for a in range(n):
            copy(a, 0, sibling, me).wait_recv()
        for j, chip in enumerate(chips):
            for a in range(n):
                copy(a, 4 + j, (*chip, 1 - c), me).wait_recv()
        for cp in first + passed:
            cp.wait_send()
        for cp in mine:
            cp.wait()

    return pl.pallas_call(
        body, name=name, in_specs=[ANY] * n, out_specs=[ANY] * n,
        out_shape=[jax.ShapeDtypeStruct((8,) + b.shape, b.dtype) for b in blocks],
        scratch_shapes=[pltpu.SemaphoreType.DMA((7 * n,)), pltpu.SemaphoreType.DMA((7 * n,)),
                        pltpu.SemaphoreType.DMA((n,))],
    )(*blocks)


def _exchange(srcs, out_shapes, plan, n_remote, n_local, *, name):
    ns, no = len(srcs), len(out_shapes)

    def body(*refs):
        src_ref, out_ref = refs[:ns], refs[ns:ns + no]
        send_sems, recv_sems, local_sems = refs[ns + no:]
        x, y, c = _position()
        remote, local = plan(src_ref, out_ref, x, y, c)
        local_copies = [pltpu.make_async_copy(s, d, local_sems.at[i]) for i, (s, d) in enumerate(local)]
        for lc in local_copies:
            lc.start()

        def copy(k, s, d, peer):
            return pltpu.make_async_remote_copy(src_ref=s, dst_ref=d, send_sem=send_sems.at[k],
                                                recv_sem=recv_sems.at[k], device_id=peer, device_id_type=MESH)

        copies = [copy(k, s, d, peer) for k, (s, d, peer, _) in enumerate(remote)]
        for cp in copies:
            cp.start()
        for k, (s, _, peer, landing) in enumerate(remote):
            copy(k, s, landing, peer).wait_recv()
        for cp in copies:
            cp.wait_send()
        for lc in local_copies:
            lc.wait()

    return pl.pallas_call(
        body, name=name, in_specs=[ANY] * ns, out_specs=[ANY] * no, out_shape=list(out_shapes),
        scratch_shapes=[pltpu.SemaphoreType.DMA((n_remote,)), pltpu.SemaphoreType.DMA((n_remote,)),
                        pltpu.SemaphoreType.DMA((max(n_local, 1),))],
    )(*srcs)


CHIP_FLIPS = ((1, 0), (0, 1), (1, 1))


def _flip(v, f):
    return 1 - v if f else v


def _sum_rows_tile(R, C, budget=3 << 20):
    best = None
    for t in range(8, R + 1, 8):
        if R % t == 0 and t * C * 4 <= budget:
            best = t
    return best if best is not None else R


def _reduce_scatter(gps, *, wire_dtype):
    n = len(gps)
    x, y, c = _position()
    c_arr = jnp.reshape(c, (1,)).astype(jnp.int32)
    bc_arr = jnp.stack([2 * x + y, c]).astype(jnp.int32)

    def plan1(src, out, x, y, c):
        return [(src[a].at[b, 1 - c], out[a].at[b], (x, y, 1 - c), out[a].at[b])
                for a in range(n) for b in range(4)], []

    got1 = _exchange(gps, [jax.ShapeDtypeStruct((4,) + g.shape[2:], F32) for g in gps], plan1, 4 * n, 0,
                     name='rs_sibling')

    def sum_chip(c_ref, mine_ref, got_ref, out_ref):
        out_ref[...] = (mine_ref[...] + got_ref[...]).astype(out_ref.dtype)

    parts = []
    for gp, g1 in zip(gps, got1):
        _, _, R, C = gp.shape
        tr = _sum_rows_tile(R, C)
        parts.append(pl.pallas_call(
            sum_chip, name='rs_sum_chip',
            grid_spec=pltpu.PrefetchScalarGridSpec(
                num_scalar_prefetch=1, grid=(4, R // tr),
                in_specs=[pl.BlockSpec((None, None, tr, C), lambda b, r, cr: (b, cr[0], r, 0)),
                          pl.BlockSpec((None, tr, C), lambda b, r, cr: (b, r, 0))],
                out_specs=pl.BlockSpec((None, tr, C), lambda b, r, cr: (b, r, 0))),
            out_shape=jax.ShapeDtypeStruct((4, R, C), wire_dtype),
            compiler_params=_cparams("parallel", "parallel"),
        )(c_arr, gp, g1))

    def plan2(src, out, x, y, c):
        remote = []
        for k, (fx, fy) in enumerate(CHIP_FLIPS):
            px, py = _flip(x, fx), _flip(y, fy)
            remote += [(src[a].at[2 * px + py], out[a].at[k], (px, py, c), out[a].at[k]) for a in range(n)]
        return remote, []

    got2 = _exchange(parts, [jax.ShapeDtypeStruct((3,) + p.shape[1:], wire_dtype) for p in parts], plan2,
                     3 * n, 0, name='rs_chips')

    def sum_final(bc_ref, mine_ref, got1_ref, got2_ref, out_ref):
        acc = mine_ref[...] + got1_ref[...]
        for k in range(3):
            acc = acc + got2_ref[k].astype(F32)
        out_ref[...] = acc

    halves = []
    for gp, g1, g2 in zip(gps, got1, got2):
        _, _, R, C = gp.shape
        tr = _sum_rows_tile(R, C, budget=2 << 20)
        halves.append(pl.pallas_call(
            sum_final, name='rs_sum_final',
            grid_spec=pltpu.PrefetchScalarGridSpec(
                num_scalar_prefetch=1, grid=(R // tr,),
                in_specs=[pl.BlockSpec((None, None, tr, C), lambda r, bc: (bc[0], bc[1], r, 0)),
                          pl.BlockSpec((None, tr, C), lambda r, bc: (bc[0], r, 0)),
                          pl.BlockSpec((3, tr, C), lambda r, bc: (0, r, 0))],
                out_specs=pl.BlockSpec((tr, C), lambda r, bc: (r, 0))),
            out_shape=jax.ShapeDtypeStruct((R, C), F32),
            compiler_params=_cparams("parallel"),
        )(bc_arr, gp, g1, g2))

    def plan3(src, out, x, y, c):
        return ([(src[a], out[a].at[c], (x, y, 1 - c), out[a].at[1 - c]) for a in range(n)],
                [(src[a], out[a].at[c]) for a in range(n)])

    return _exchange(halves, [jax.ShapeDtypeStruct((2,) + h.shape, F32) for h in halves], plan3, n, n,
                     name='rs_swap')


def _sum8(gathered, *, name):
    _, R, C = gathered.shape

    def body(g_ref, o_ref):
        acc = g_ref[0]
        for k in range(1, 8):
            acc = acc + g_ref[k]
        o_ref[...] = acc

    return pl.pallas_call(body, name=name, out_shape=jax.ShapeDtypeStruct((R, C), F32))(gathered)


def _adamw(w, g, m, v, *, name):
    shape = w.shape
    cols = shape[-1]
    rows = w.size // cols
    tr = _sum_rows_tile(rows, cols, budget=1 << 20)

    def body(w_ref, g_ref, m_ref, v_ref, d_ref, mo_ref, vo_ref):
        gv = g_ref[...]
        mn = ADAM_B1 * m_ref[...] + (1.0 - ADAM_B1) * gv
        vn = ADAM_B2 * v_ref[...] + (1.0 - ADAM_B2) * (gv * gv)
        m_hat = mn / (1.0 - ADAM_B1 ** ADAM_STEP)
        v_hat = vn / (1.0 - ADAM_B2 ** ADAM_STEP)
        d_ref[...] = -ADAM_LR * (m_hat / (jnp.sqrt(v_hat) + ADAM_EPS) + ADAM_WD * w_ref[...])
        mo_ref[...] = mn
        vo_ref[...] = vn

    blk = pl.BlockSpec((tr, cols), lambda i: (i, 0))
    outs = pl.pallas_call(
        body, name=name, grid=(rows // tr,), in_specs=[blk] * 4, out_specs=[blk] * 3,
        out_shape=[jax.ShapeDtypeStruct((rows, cols), F32)] * 3,
        compiler_params=_cparams("parallel"),
    )(*[a.reshape(rows, cols) for a in (w, g, m, v)])
    return tuple(o.reshape(shape) for o in outs)


WEIGHTS = ['norm_mix', 'norm_mlp', 'w_ada', 'b_ada', 'w_mlp_in', 'w_mlp_out', 'fox_w_in', 'fox_b_f',
           'fox_q_norm', 'fox_k_norm', 'fox_w_out', 'sg_w_in', 'sg_ln_g', 'sg_ln_b', 'sg_w_s', 'sg_b_s',
           'sg_w_out', 'cv_w_pw1', 'cv_b_pw1', 'cv_w_dw', 'cv_b_dw', 'cv_ln_g', 'cv_ln_b', 'cv_w_pw2',
           'cv_b_pw2']
BIG = {'w_mlp_in': 2, 'w_mlp_out': 1, 'fox_w_in': 2, 'fox_w_out': 1, 'sg_w_in': 2, 'sg_w_out': 1,
       'cv_w_pw1': 2, 'cv_w_pw2': 1}
SMALL_SHARDED = ['cv_b_pw1', 'cv_w_dw', 'cv_b_dw', 'cv_ln_g', 'cv_ln_b', 'cv_b_pw2']
SMALL_GRADS = ['norm_mix', 'norm_mlp', 'fox_b_f', 'fox_q_norm', 'fox_k_norm', 'sg_ln_g', 'sg_ln_b', 'sg_w_s',
               'sg_b_s'] + SMALL_SHARDED
GRAD_WIRE_DTYPE = BF16


def _pack_rows(parts, cols):
    flat = jnp.concatenate([p.reshape(-1) for p in parts])
    rows = -(-flat.size // (8 * cols)) * 8
    return jnp.pad(flat, (0, rows * cols - flat.size)).reshape(rows, cols)


def _unpack(flat, shapes):
    out, off = [], 0
    for s in shapes:
        n = math.prod(s)
        out.append(flat[..., off:off + n].reshape(flat.shape[:-1] + tuple(s)))
        off += n
    return out


def _merge_chips(a, axis):
    a = jnp.moveaxis(a, 0, axis)
    return a.reshape(a.shape[:axis] + (a.shape[axis] * a.shape[axis + 1],) + a.shape[axis + 2:])


def _split_chips(a, axis):
    a = a.reshape(a.shape[:axis] + (4, a.shape[axis] // 4) + a.shape[axis + 1:])
    return jnp.moveaxis(a, axis, 0)


def _step(a):
    x, y, c = _position()
    me = 4 * x + 2 * y + c
    chip = 2 * x + y
    T, D = a['x'].shape[1], a['x'].shape[2]
    L = a['norm_mix'].shape[0]

    small_shapes = [(D,)] + [a[n].shape for n in SMALL_SHARDED]
    small = _all_gather8([_pack_rows([a['c']] + [a[n] for n in SMALL_SHARDED], LANE)], name='ag_small')[0]
    small = small.reshape(8, -1)
    c_all = _unpack(small, small_shapes[:1])[0]
    sharded = _unpack(small[0::2, D:], small_shapes[1:])
    P = {n: _merge_chips(v, v.ndim - 2) for n, v in zip(SMALL_SHARDED, sharded)}

    c_act = _silu_rows(c_all, name='c_act')
    mod_cols = jnp.stack([
        _mm(c_act, a['w_ada'][i], name='ada_mod', tm=8, tn=_col_tile(a['w_ada'].shape[2], 768),
            extras=[(lax.dynamic_slice_in_dim(a['b_ada'][i:i + 1], chip * a['w_ada'].shape[2],
                                              a['w_ada'].shape[2], axis=1), 'row')],
            epilogue=_bias_epilogue)
        for i in range(L)])
    mod_all = _all_gather8([mod_cols.reshape(L * 8, -1)], name='ag_mod')[0].reshape(8, L, 8, -1)
    mod = lax.dynamic_index_in_dim(mod_all[0::2], me, axis=2, keepdims=False)
    mod = jnp.moveaxis(mod, 0, 1).reshape(L, 6 * D)

    halves = [lax.dynamic_index_in_dim(a[n].astype(BF16).reshape(2, -1, a[n].shape[-1]), c, axis=0,
                                       keepdims=False) for n in BIG]
    gathered = _all_gather8(halves, name='ag_weights')
    n_heads = D // HEAD_DIM
    for n, gth in zip(BIG, gathered):
        blocks = gth.reshape((4,) + a[n].shape)
        if n == 'fox_w_in':
            pad = jnp.zeros(a[n].shape[:2] + (LANE - n_heads,), BF16)
            P[n] = jnp.concatenate([blocks[0], blocks[1], blocks[2], blocks[3], pad], axis=-1)
        else:
            P[n] = _merge_chips(blocks, BIG[n])
    for n in ('sg_w_in', 'sg_w_out', 'cv_w_pw1', 'cv_w_pw2', 'sg_w_s', 'sg_b_s', 'cv_w_dw'):
        P[n] = (P[n] if n in P else a[n])[0]
    for n in ('norm_mix', 'norm_mlp', 'fox_b_f', 'fox_q_norm', 'fox_k_norm', 'sg_ln_g', 'sg_ln_b'):
        P[n] = a[n]

    loss_part, grad_x, dmod, g = _local_step(a['x'][0], a['loss_target'][0], mod, P)

    small_g = [dmod, loss_part[0:1, 0:1]] + [g[n] for n in SMALL_GRADS]
    small_g_shapes = [s.shape for s in small_g]
    all_small = _all_gather8([_pack_rows(small_g, LANE)], name='ag_small_grads')[0]
    summed = _sum8(all_small, name='sum_small_grads').reshape(-1)
    sums = _unpack(summed, small_g_shapes)
    loss = sums[1][0, 0]
    grads = dict(zip(SMALL_GRADS, sums[2:]))
    grads['b_ada'] = sums[0]
    for n in SMALL_SHARDED:
        blk = a[n].shape[-1]
        grads[n] = lax.dynamic_slice_in_dim(grads[n], chip * blk, blk, axis=grads[n].ndim - 1)
    dmod_all = all_small.reshape(8, -1)[:, :dmod.size].reshape(8, L, 6 * D)
    cols = a['w_ada'].shape[2]
    dmod_cols = lax.dynamic_slice_in_dim(dmod_all, chip * cols, cols, axis=2)
    pad8 = lambda t: jnp.pad(t, ((0, LANE - 8), (0, 0)))
    c_act_pad = pad8(c_act)
    grads['w_ada'] = jnp.stack([
        _mm(c_act_pad, pad8(dmod_cols[:, i]), ta=True, name='ada_dw', tn=_col_tile(cols, 768))
        for i in range(L)])

    gps = [_split_chips(g[n], BIG[n]).reshape(4, 2, -1, a[n].shape[-1]) for n in BIG]
    for n, shard in zip(BIG, _reduce_scatter(gps, wire_dtype=GRAD_WIRE_DTYPE)):
        grads[n] = shard.reshape(a[n].shape)

    deltas, new_m, new_v = {}, {}, {}
    for n in WEIGHTS:
        deltas[n], new_m[n], new_v[n] = _adamw(a[n], grads[n], a['m_' + n], a['v_' + n], name='adamw')
    return (loss, grad_x[None], *[grads[n] for n in WEIGHTS], *[deltas[n] for n in WEIGHTS],
            *[new_m[n] for n in WEIGHTS], *[new_v[n] for n in WEIGHTS])


def _silu_rows(x, *, name):
    def body(x_ref, o_ref):
        xv = x_ref[...]
        o_ref[...] = (xv * jax.nn.sigmoid(xv)).astype(BF16)

    return pl.pallas_call(body, name=name, out_shape=jax.ShapeDtypeStruct(x.shape, BF16))(x)


def kernel(x, c, norm_mix, norm_mlp, w_ada, b_ada, w_mlp_in, w_mlp_out, fox_w_in, fox_b_f, fox_q_norm, fox_k_norm, fox_w_out, sg_w_in, sg_ln_g, sg_ln_b, sg_w_s, sg_b_s, sg_w_out, cv_w_pw1, cv_b_pw1, cv_w_dw, cv_b_dw, cv_ln_g, cv_ln_b, cv_w_pw2, cv_b_pw2, loss_target, m_norm_mix, m_norm_mlp, m_w_ada, m_b_ada, m_w_mlp_in, m_w_mlp_out, m_fox_w_in, m_fox_b_f, m_fox_q_norm, m_fox_k_norm, m_fox_w_out, m_sg_w_in, m_sg_ln_g, m_sg_ln_b, m_sg_w_s, m_sg_b_s, m_sg_w_out, m_cv_w_pw1, m_cv_b_pw1, m_cv_w_dw, m_cv_b_dw, m_cv_ln_g, m_cv_ln_b, m_cv_w_pw2, m_cv_b_pw2, v_norm_mix, v_norm_mlp, v_w_ada, v_b_ada, v_w_mlp_in, v_w_mlp_out, v_fox_w_in, v_fox_b_f, v_fox_q_norm, v_fox_k_norm, v_fox_w_out, v_sg_w_in, v_sg_ln_g, v_sg_ln_b, v_sg_w_s, v_sg_b_s, v_sg_w_out, v_cv_w_pw1, v_cv_b_pw1, v_cv_w_dw, v_cv_b_dw, v_cv_ln_g, v_cv_ln_b, v_cv_w_pw2, v_cv_b_pw2):
    return _step(dict(locals()))
```

```python
import functools
import math

import jax
import jax.numpy as jnp
from jax import lax
from jax.experimental import pallas as pl
from jax.experimental.pallas import tpu as pltpu

F32 = jnp.float32
BF16 = jnp.bfloat16

EPS = 1e-6
HEAD_DIM = 64
LANE = 128
CONV_WIDTH = 31
CONV_HALO = 32
SG_CHUNK = 128
SG_CAUSAL = 64
SG_GROUPS = 8
N_MIXERS = 3
VMEM_LIMIT = 56 * 1024 * 1024
NEG = -1e30

ADAM_LR = 0.001
ADAM_B1 = 0.9
ADAM_B2 = 0.999
ADAM_EPS = 1e-08
ADAM_WD = 0.01
ADAM_STEP = 10

MESH = pl.DeviceIdType.MESH
ANY = pl.BlockSpec(memory_space=pl.ANY)


def _cparams(*sem):
    return pltpu.CompilerParams(dimension_semantics=sem, vmem_limit_bytes=VMEM_LIMIT)


def _row_tile(t, want=512):
    return min(t, want)


def _matmul(a, b, *, name, ta=False, tb=False, tm=512, tn=1024, tk=1024,
            extras=(), epilogue=None, out_dtypes=(F32,), b_outer=False):
    M, K = (a.shape[1], a.shape[0]) if ta else a.shape
    N = b.shape[0] if tb else b.shape[1]
    assert (b.shape[1] if tb else b.shape[0]) == K
    tm, tn, tk = min(tm, M), min(tn, N), min(tk, K)
    assert M % tm == 0 and N % tn == 0 and K % tk == 0, (name, M, N, K, tm, tn, tk)
    nk = K // tk

    def spec(shape, pick):
        if b_outer:
            return pl.BlockSpec(shape, lambda j, i, k: pick(i, j, k))
        return pl.BlockSpec(shape, pick)

    a_spec = spec((tk, tm), lambda i, j, k: (k, i)) if ta else spec((tm, tk), lambda i, j, k: (i, k))
    b_spec = spec((tn, tk), lambda i, j, k: (j, k)) if tb else spec((tk, tn), lambda i, j, k: (k, j))
    ex_specs = [spec((tm, tn), lambda i, j, k: (i, j)) if kind == 'tile' else spec((1, tn), lambda i, j, k: (0, j))
                for _, kind in extras]
    dims = (((0,) if ta else (1,), (1,) if tb else (0,)), ((), ()))
    n_ex, n_out = len(extras), len(out_dtypes)

    def body(*refs):
        a_ref, b_ref = refs[0], refs[1]
        ex = refs[2:2 + n_ex]
        outs = refs[2 + n_ex:2 + n_ex + n_out]

        def finish(acc):
            vals = epilogue(acc, *[r[...] for r in ex]) if epilogue else (acc,)
            for o, v in zip(outs, vals):
                o[...] = v.astype(o.dtype)

        part = lax.dot_general(a_ref[...].astype(BF16), b_ref[...].astype(BF16), dims,
                               preferred_element_type=F32)
        if nk == 1:
            finish(part)
        else:
            acc_ref = refs[-1]
            k = pl.program_id(2)

            @pl.when(k == 0)
            def _():
                acc_ref[...] = part

            @pl.when(k > 0)
            def _():
                acc_ref[...] += part

            @pl.when(k == nk - 1)
            def _():
                finish(acc_ref[...])

    outs = pl.pallas_call(
        body, name=name,
        grid=(N // tn, M // tm, nk) if b_outer else (M // tm, N // tn, nk),
        in_specs=[a_spec, b_spec] + ex_specs,
        out_specs=[spec((tm, tn), lambda i, j, k: (i, j)) for _ in out_dtypes],
        out_shape=[jax.ShapeDtypeStruct((M, N), dt) for dt in out_dtypes],
        scratch_shapes=[pltpu.VMEM((tm, tn), F32)] if nk > 1 else [],
        compiler_params=_cparams("parallel", "parallel", "arbitrary"),
    )(a, b, *[arr for arr, _ in extras])
    return outs if n_out > 1 else outs[0]


def _norm_mod_fwd(x, w, sc, sh, *, name):
    T, D = x.shape
    tr = _row_tile(T)

    def body(x_ref, w_ref, sc_ref, sh_ref, h_ref):
        xv = x_ref[...]
        r = lax.rsqrt(jnp.mean(xv * xv, axis=-1, keepdims=True) + EPS)
        h_ref[...] = ((xv * r) * w_ref[...] * (1.0 + sc_ref[...]) + sh_ref[...]).astype(BF16)

    row = pl.BlockSpec((1, D), lambda i: (0, 0))
    return pl.pallas_call(
        body, name=name, grid=(T // tr,),
        in_specs=[pl.BlockSpec((tr, D), lambda i: (i, 0)), row, row, row],
        out_specs=pl.BlockSpec((tr, D), lambda i: (i, 0)),
        out_shape=jax.ShapeDtypeStruct((T, D), BF16),
        compiler_params=_cparams("parallel"),
    )(x, w, sc, sh)


def _norm_mod_bwd(dh, x, dres, w, sc, *, name, gate=None):
    T, D = x.shape
    tr = _row_tile(T)
    with_gate = gate is not None

    def body(*refs):
        if with_gate:
            dh_ref, x_ref, dres_ref, w_ref, sc_ref, y_ref, g_ref, dx_ref, sums_ref, dy_ref = refs
        else:
            dh_ref, x_ref, dres_ref, w_ref, sc_ref, dx_ref, sums_ref = refs
        i = pl.program_id(0)
        xv, dhv = x_ref[...], dh_ref[...].astype(F32)
        r = lax.rsqrt(jnp.mean(xv * xv, axis=-1, keepdims=True) + EPS)
        n = xv * r
        wv, scale = w_ref[...], 1.0 + sc_ref[...]
        dn = dhv * (wv * scale)
        dx = dres_ref[...] + r * (dn - n * jnp.mean(dn * n, axis=-1, keepdims=True))
        dx_ref[...] = dx
        rows = [jnp.sum(dhv, axis=0, keepdims=True),
                jnp.sum(dhv * (n * wv), axis=0, keepdims=True),
                jnp.sum(dhv * n * scale, axis=0, keepdims=True)]
        if with_gate:
            dy_ref[...] = (dx * g_ref[...]).astype(BF16)
            rows.append(jnp.sum(dx * y_ref[...], axis=0, keepdims=True))
            rows.append(jnp.sum(dx * g_ref[...], axis=0, keepdims=True))
        part = jnp.concatenate(rows + [jnp.zeros((8 - len(rows), D), F32)], axis=0)

        @pl.when(i == 0)
        def _():
            sums_ref[...] = part

        @pl.when(i > 0)
        def _():
            sums_ref[...] += part

    blk = pl.BlockSpec((tr, D), lambda i: (i, 0))
    row = pl.BlockSpec((1, D), lambda i: (0, 0))
    in_specs = [blk, blk, blk, row, row]
    args = [dh, x, dres, w, sc]
    out_specs = [blk, pl.BlockSpec((8, D), lambda i: (0, 0))]
    out_shape = [jax.ShapeDtypeStruct((T, D), F32), jax.ShapeDtypeStruct((8, D), F32)]
    if with_gate:
        in_specs += [blk, row]
        args += list(gate)
        out_specs.append(blk)
        out_shape.append(jax.ShapeDtypeStruct((T, D), BF16))
    return pl.pallas_call(
        body, name=name, grid=(T // tr,), in_specs=in_specs, out_specs=out_specs,
        out_shape=out_shape, compiler_params=_cparams("arbitrary"),
    )(*args)


def _gate_bwd(dx, y, g, *, name):
    T, D = dx.shape
    tr = _row_tile(T)

    def body(dx_ref, y_ref, g_ref, dy_ref, dg_ref):
        i = pl.program_id(0)
        dxv = dx_ref[...]
        dy_ref[...] = (dxv * g_ref[...]).astype(BF16)
        part = jnp.concatenate([jnp.sum(dxv * y_ref[...], axis=0, keepdims=True),
                                jnp.zeros((7, D), F32)], axis=0)

        @pl.when(i == 0)
        def _():
            dg_ref[...] = part

        @pl.when(i > 0)
        def _():
            dg_ref[...] += part

    blk = pl.BlockSpec((tr, D), lambda i: (i, 0))
    return pl.pallas_call(
        body, name=name, grid=(T // tr,),
        in_specs=[blk, blk, pl.BlockSpec((1, D), lambda i: (0, 0))],
        out_specs=[blk, pl.BlockSpec((8, D), lambda i: (0, 0))],
        out_shape=[jax.ShapeDtypeStruct((T, D), BF16), jax.ShapeDtypeStruct((8, D), F32)],
        compiler_params=_cparams("arbitrary"),
    )(dx, y, g)


def _loss_head(y, target, *, name):
    T, D = y.shape
    tr = _row_tile(T)

    def body(y_ref, t_ref, loss_ref, dy_ref):
        i = pl.program_id(0)
        e = y_ref[...] - t_ref[...]
        dy_ref[...] = e * (1.0 / D)
        part = jnp.full((8, LANE), 0.5 / D * jnp.sum(e * e), F32)

        @pl.when(i == 0)
        def _():
            loss_ref[...] = part

        @pl.when(i > 0)
        def _():
            loss_ref[...] += part

    blk = pl.BlockSpec((tr, D), lambda i: (i, 0))
    return pl.pallas_call(
        body, name=name, grid=(T // tr,), in_specs=[blk, blk],
        out_specs=[pl.BlockSpec((8, LANE), lambda i: (0, 0)), blk],
        out_shape=[jax.ShapeDtypeStruct((8, LANE), F32), jax.ShapeDtypeStruct((T, D), F32)],
        compiler_params=_cparams("arbitrary"),
    )(y, target)


AUG_F = HEAD_DIM
AUG_LSE = HEAD_DIM + 6


def _half_cols(x, lo):
    return (jnp.sum(jnp.where(lo, x, 0.0), axis=-1, keepdims=True),
            jnp.sum(jnp.where(lo, 0.0, x), axis=-1, keepdims=True))


def _half_sums(x, lo):
    s_lo, s_hi = _half_cols(x, lo)
    return jnp.where(lo, s_lo, s_hi)


def _split3(x):
    a = x.astype(BF16).astype(F32)
    r = x - a
    b = r.astype(BF16).astype(F32)
    return a, b, (r - b).astype(BF16).astype(F32)


def _aug(lane, base, terms):
    out = jnp.zeros(lane.shape, F32)
    for i, t in enumerate(terms):
        out = jnp.where(lane == base + i, t, out)
    return out


def _head_lanes(x2, h):
    return x2 if h == 0 else pltpu.roll(x2, HEAD_DIM, 1)


def _fox_prep_fwd(proj, qg, kg, fcol, *, d_model, name):
    T = proj.shape[0]
    nhp = d_model // LANE
    tr = _row_tile(T)

    def body(q_ref, k_ref, v_ref, qg_ref, kg_ref, f_ref, qa_ref, qta_ref, ka_ref, kta_ref, va_ref, vta_ref):
        lane = lax.broadcasted_iota(jnp.int32, (tr, LANE), 1)
        lo = lane < HEAD_DIM

        def norm(xv, g):
            ms = _half_sums(xv * xv, lo) * (1.0 / HEAD_DIM)
            return (xv * lax.rsqrt(ms + EPS)) * g

        qn = norm(q_ref[...], qg_ref[...]) * (HEAD_DIM ** -0.5)
        kn = norm(k_ref[...], kg_ref[...])
        vv = v_ref[...]
        qa, ka, va, vta = [], [], [], []
        for h in range(2):
            f1, f2, f3 = _split3(f_ref[h])
            qa.append(jnp.where(lo, _head_lanes(qn, h), _aug(lane, AUG_F, [f1, f2, f3, 1.0, 1.0, 1.0])))
            ka.append(jnp.where(lo, _head_lanes(kn, h),
                                _aug(lane, AUG_F, [1.0, 1.0, 1.0, -f1, -f2, -f3, 1.0, 1.0, 1.0])))
            va.append(jnp.where(lo if h == 0 else jnp.logical_not(lo), vv, 0.0))
            vta.append(jnp.where(lo, _head_lanes(vv, h), _aug(lane, AUG_F, [1.0, 1.0, 1.0])))
        for parts, ref, tref in ((qa, qa_ref, qta_ref), (ka, ka_ref, kta_ref), (va, va_ref, None),
                                 (vta, None, vta_ref)):
            both = jnp.concatenate(parts, axis=1)
            if ref is not None:
                ref[...] = both.astype(BF16)
            if tref is not None:
                tref[...] = both.T.astype(BF16)

    gain = pl.BlockSpec((1, LANE), lambda i, h: (0, 0))
    rows = pl.BlockSpec((tr, 2 * LANE), lambda i, h: (i, h))
    cols = pl.BlockSpec((2 * LANE, tr), lambda i, h: (h, i))
    wide, tall = jax.ShapeDtypeStruct((T, 2 * d_model), BF16), jax.ShapeDtypeStruct((2 * d_model, T), BF16)
    return pl.pallas_call(
        body, name=name, grid=(T // tr, nhp),
        in_specs=[pl.BlockSpec((tr, LANE), lambda i, h: (i, h)),
                  pl.BlockSpec((tr, LANE), lambda i, h: (i, nhp + h)),
                  pl.BlockSpec((tr, LANE), lambda i, h: (i, 2 * nhp + h)), gain, gain,
                  pl.BlockSpec((2, tr, 1), lambda i, h: (h, i, 0))],
        out_specs=[rows, cols, rows, cols, rows, cols],
        out_shape=[wide, tall, wide, tall, wide, tall],
        compiler_params=_cparams("parallel", "parallel"),
    )(proj, proj, proj, qg, kg, fcol)


def _fox_do_prep(do, o, *, name):
    T, D = do.shape
    nhp = D // LANE
    tr = _row_tile(T)

    def body(do_ref, o_ref, doa_ref, dota_ref):
        lane = lax.broadcasted_iota(jnp.int32, (tr, LANE), 1)
        lo = lane < HEAD_DIM
        dob = do_ref[...].astype(BF16).astype(F32)
        deltas = _half_cols(dob * o_ref[...], lo)
        both = jnp.concatenate(
            [jnp.where(lo, _head_lanes(dob, h), _aug(lane, AUG_F, _split3(-deltas[h]))) for h in range(2)], axis=1)
        doa_ref[...] = both.astype(BF16)
        dota_ref[...] = both.T.astype(BF16)

    blk = pl.BlockSpec((tr, LANE), lambda i, h: (i, h))
    return pl.pallas_call(
        body, name=name, grid=(T // tr, nhp), in_specs=[blk, blk],
        out_specs=[pl.BlockSpec((tr, 2 * LANE), lambda i, h: (i, h)),
                   pl.BlockSpec((2 * LANE, tr), lambda i, h: (h, i))],
        out_shape=[jax.ShapeDtypeStruct((T, 2 * D), BF16), jax.ShapeDtypeStruct((2 * D, T), BF16)],
        compiler_params=_cparams("parallel", "parallel"),
    )(do, o)


def _fox_prep_bwd(proj, dq, dkt, dvt, qg, kg, *, d_model, name):
    T = proj.shape[0]
    nhp = d_model // LANE
    tr = _row_tile(T)

    def body(q_ref, k_ref, dq_ref, dkt_ref, dvt_ref, qg_ref, kg_ref, dqo_ref, dko_ref, dvo_ref, sums_ref):
        first = (pl.program_id(0) == 0) & (pl.program_id(1) == 0)
        lo = lax.broadcasted_iota(jnp.int32, (tr, LANE), 1) < HEAD_DIM

        def pair(x2):
            return jnp.where(lo, x2[:, :LANE], pltpu.roll(x2[:, LANE:], HEAD_DIM, 1))

        def bwd(xv, dxhat, g):
            ms = _half_sums(xv * xv, lo) * (1.0 / HEAD_DIM)
            r = lax.rsqrt(ms + EPS)
            n = xv * r
            dn = dxhat * g
            dx = r * (dn - n * (_half_sums(dn * n, lo) * (1.0 / HEAD_DIM)))
            dg = jnp.sum(dxhat * n, axis=0, keepdims=True)
            return dx, dg + pltpu.roll(dg, HEAD_DIM, 1)

        dxq, dgq = bwd(q_ref[...], pair(dq_ref[...]) * (HEAD_DIM ** -0.5), qg_ref[...])
        dxk, dgk = bwd(k_ref[...], pair(dkt_ref[...].T), kg_ref[...])
        dqo_ref[...] = dxq.astype(BF16)
        dko_ref[...] = dxk.astype(BF16)
        dvo_ref[...] = pair(dvt_ref[...].T).astype(BF16)
        part = jnp.concatenate([dgq, dgk, jnp.zeros((6, LANE), F32)], axis=0)

        @pl.when(first)
        def _():
            sums_ref[...] = part

        @pl.when(jnp.logical_not(first))
        def _():
            sums_ref[...] += part

    gain = pl.BlockSpec((1, LANE), lambda i, h: (0, 0))
    blk = pl.BlockSpec((tr, LANE), lambda i, h: (i, h))
    tall = pl.BlockSpec((2 * LANE, tr), lambda i, h: (h, i))
    return pl.pallas_call(
        body, name=name, grid=(T // tr, nhp),
        in_specs=[blk, pl.BlockSpec((tr, LANE), lambda i, h: (i, nhp + h)),
                  pl.BlockSpec((tr, 2 * LANE), lambda i, h: (i, h)), tall, tall, gain, gain],
        out_specs=[blk, blk, blk, pl.BlockSpec((8, LANE), lambda i, h: (0, 0))],
        out_shape=[jax.ShapeDtypeStruct((T, d_model), BF16)] * 3 + [jax.ShapeDtypeStruct((8, LANE), F32)],
        compiler_params=_cparams("arbitrary", "arbitrary"),
    )(proj, proj, dq, dkt, dvt, qg, kg)


def _scan_lanes(x, reverse):
    n = x.shape[-1]
    lane = lax.broadcasted_iota(jnp.int32, x.shape, 1)
    sh = 1
    while sh < n:
        if reverse:
            x = x + jnp.where(lane < n - sh, pltpu.roll(x, n - sh, 1), 0.0)
        else:
            x = x + jnp.where(lane >= sh, pltpu.roll(x, sh, 1), 0.0)
        sh *= 2
    return x


def _fox_gate_fwd(fpre_t, bf, *, name):
    def body(f_ref, b_ref, o_ref):
        xv = f_ref[...] + b_ref[...]
        logf = jnp.minimum(xv, 0.0) - jnp.log1p(jnp.exp(-jnp.abs(xv)))
        o_ref[...] = _scan_lanes(logf, reverse=False)

    return pl.pallas_call(body, name=name, out_shape=jax.ShapeDtypeStruct(fpre_t.shape, F32))(fpre_t, bf)


def _fox_gate_bwd(dcol, drow, fpre_t, bf, *, name):
    H = fpre_t.shape[0]

    def body(dc_ref, dr_ref, f_ref, b_ref, o_ref, db_ref):
        xv = f_ref[...] + b_ref[...]
        e = dc_ref[...] - dr_ref[...]
        dlogf = _scan_lanes(e, reverse=False) - e
        dpre = dlogf * (1.0 - jax.nn.sigmoid(xv))
        o_ref[...] = dpre
        db_ref[...] = jnp.broadcast_to(jnp.sum(dpre, axis=-1, keepdims=True), (H, LANE))

    return pl.pallas_call(
        body, name=name,
        out_shape=[jax.ShapeDtypeStruct(fpre_t.shape, F32), jax.ShapeDtypeStruct((H, LANE), F32)],
    )(dcol, drow, fpre_t, bf)


_NT = (((1,), (1,)), ((), ()))
_TN = (((0,), (0,)), ((), ()))
_NN = (((1,), (0,)), ((), ()))


def _attn_tile(T):
    return min(T, 512)


def _causal(tq, tk):
    return lax.broadcasted_iota(jnp.int32, (tq, tk), 1) <= lax.broadcasted_iota(jnp.int32, (tq, tk), 0)


def _fox_attn_fwd(qa, kta, va, *, name):
    T = qa.shape[0]
    nhp = qa.shape[1] // (2 * LANE)
    tq = tk = _attn_tile(T)
    nq = T // tq

    def body(qa_ref, kta_ref, va_ref, o_ref, qb_ref, m_sc, l_sc, acc_sc):
        i, j = pl.program_id(1), pl.program_id(2)

        @pl.when(j == 0)
        def _():
            m_sc[...] = jnp.full(m_sc.shape, NEG, F32)
            l_sc[...] = jnp.zeros(l_sc.shape, F32)
            acc_sc[...] = jnp.zeros(acc_sc.shape, F32)

        def block(diagonal):
            for h in range(2):
                hs = slice(h * LANE, (h + 1) * LANE)
                s = lax.dot_general(qa_ref[:, hs], kta_ref[hs, :], _NN, preferred_element_type=F32)
                if diagonal:
                    s = jnp.where(_causal(tq, tk), s, NEG)
                m_prev = m_sc[h]
                m_next = jnp.maximum(m_prev, jnp.max(s, axis=1, keepdims=True))
                p = jnp.exp(s - jnp.tile(m_next, (1, tk // LANE)))
                alpha = jnp.exp(m_prev - m_next)
                l_sc[h] = alpha * l_sc[h] + jnp.sum(p, axis=1, keepdims=True)
                m_sc[h] = m_next
                acc_sc[h] = alpha * acc_sc[h] + lax.dot_general(p.astype(BF16), va_ref[:, hs], _NN,
                                                                preferred_element_type=F32)

        @pl.when(j < i)
        def _():
            block(False)

        @pl.when(j == i)
        def _():
            block(True)
            o_ref[...] = acc_sc[0] / l_sc[0] + acc_sc[1] / l_sc[1]
            lane = lax.broadcasted_iota(jnp.int32, (tq, LANE), 1)
            for h in range(2):
                hs = slice(h * LANE, (h + 1) * LANE)
                pieces = _split3(-(m_sc[h] + jnp.log(l_sc[h])))
                qb = qa_ref[:, hs].astype(F32)
                for n, piece in enumerate(pieces):
                    qb = jnp.where(lane == AUG_LSE + n, piece, qb)
                qb_ref[:, hs] = qb.astype(BF16)

    return pl.pallas_call(
        body, name=name, grid=(nhp, nq, nq),
        in_specs=[pl.BlockSpec((tq, 2 * LANE), lambda h, i, j: (i, h)),
                  pl.BlockSpec((2 * LANE, tk), lambda h, i, j: (h, jnp.minimum(j, i))),
                  pl.BlockSpec((tk, 2 * LANE), lambda h, i, j: (jnp.minimum(j, i), h))],
        out_specs=[pl.BlockSpec((tq, LANE), lambda h, i, j: (i, h)),
                   pl.BlockSpec((tq, 2 * LANE), lambda h, i, j: (i, h))],
        out_shape=[jax.ShapeDtypeStruct((T, nhp * LANE), F32), jax.ShapeDtypeStruct(qa.shape, BF16)],
        scratch_shapes=[pltpu.VMEM((2, tq, LANE), F32), pltpu.VMEM((2, tq, LANE), F32),
                        pltpu.VMEM((2, tq, LANE), F32)],
        compiler_params=_cparams("parallel", "parallel", "arbitrary"),
    )(qa, kta, va)


def _fox_attn_bwd(qb, qta, ka, kta, vta, doa, dota, *, name):
    T = qb.shape[0]
    nhp = qb.shape[1] // (2 * LANE)
    tq = tk = _attn_tile(T)
    nq = T // tq

    def body(qb_ref, qta_ref, ka_ref, kta_ref, vta_ref, doa_ref, dota_ref,
             dq_ref, dkt_ref, dvt_ref, dcol_ref, drow_ref, dkt_sc, dvt_sc, dcol_sc):
        j, i = pl.program_id(1), pl.program_id(2)

        @pl.when((j == 0) & (i == 0))
        def _():
            dq_ref[...] = jnp.zeros(dq_ref.shape, F32)
            drow_ref[...] = jnp.zeros(drow_ref.shape, F32)

        @pl.when(i == 0)
        def _():
            dkt_sc[...] = jnp.zeros(dkt_sc.shape, F32)
            dvt_sc[...] = jnp.zeros(dvt_sc.shape, F32)
            dcol_sc[...] = jnp.zeros(dcol_sc.shape, F32)

        def block(diagonal):
            rows = pl.ds(pl.multiple_of(i * tq, tq), tq)
            for h in range(2):
                hs = slice(h * LANE, (h + 1) * LANE)
                p = jnp.exp(lax.dot_general(qb_ref[:, hs], kta_ref[hs, :], _NN, preferred_element_type=F32))
                if diagonal:
                    p = jnp.where(_causal(tq, tk), p, 0.0)
                dl = p * lax.dot_general(doa_ref[:, hs], vta_ref[hs, :], _NN, preferred_element_type=F32)
                dlb = dl.astype(BF16)
                dvt_sc[h] += lax.dot_general(dota_ref[hs, :], p.astype(BF16), _NN, preferred_element_type=F32)
                dkt_sc[h] += lax.dot_general(qta_ref[hs, :], dlb, _NN, preferred_element_type=F32)
                dq_ref[rows, hs] += lax.dot_general(dlb, ka_ref[:, hs], _NN, preferred_element_type=F32)
                dcol_sc[h] += jnp.sum(dl, axis=0, keepdims=True)
                drow_ref[h, rows, :] += jnp.sum(dl, axis=1, keepdims=True)

        @pl.when(i > j)
        def _():
            block(False)

        @pl.when(i == j)
        def _():
            block(True)

        @pl.when(i == nq - 1)
        def _():
            dkt_ref[...] = jnp.concatenate([dkt_sc[0], dkt_sc[1]], axis=0)
            dvt_ref[...] = jnp.concatenate([dvt_sc[0], dvt_sc[1]], axis=0)
            dcol_ref[...] = dcol_sc[...]

    qrow = pl.BlockSpec((tq, 2 * LANE), lambda h, j, i: (jnp.maximum(i, j), h))
    qcol = pl.BlockSpec((2 * LANE, tq), lambda h, j, i: (h, jnp.maximum(i, j)))
    krow = pl.BlockSpec((tk, 2 * LANE), lambda h, j, i: (j, h))
    kcol = pl.BlockSpec((2 * LANE, tk), lambda h, j, i: (h, j))
    tall = jax.ShapeDtypeStruct((qb.shape[1], T), F32)
    return pl.pallas_call(
        body, name=name, grid=(nhp, nq, nq),
        in_specs=[qrow, qcol, krow, kcol, kcol, qrow, qcol],
        out_specs=[pl.BlockSpec((T, 2 * LANE), lambda h, j, i: (0, h)), kcol, kcol,
                   pl.BlockSpec((2, 1, tk), lambda h, j, i: (h, 0, j)),
                   pl.BlockSpec((2, T, 1), lambda h, j, i: (h, 0, 0))],
        out_shape=[jax.ShapeDtypeStruct(qb.shape, F32), tall, tall,
                   jax.ShapeDtypeStruct((2 * nhp, 1, T), F32), jax.ShapeDtypeStruct((2 * nhp, T, 1), F32)],
        scratch_shapes=[pltpu.VMEM((2, LANE, tk), F32), pltpu.VMEM((2, LANE, tk), F32),
                        pltpu.VMEM((2, 1, tk), F32)],
        compiler_params=_cparams("parallel", "arbitrary", "arbitrary"),
    )(qb, qta, ka, kta, vta, doa, dota)


_GELU_C = math.sqrt(2.0 / math.pi)
_GELU_A = 0.044715


def _gelu(x):
    t = jnp.tanh(_GELU_C * (x + _GELU_A * (x * x * x)))
    return x * (0.5 * (1.0 + t)), t


def _gelu_grad(x, t):
    return 0.5 * (1.0 + t) + 0.5 * x * (1.0 - t * t) * (_GELU_C * (1.0 + 3.0 * _GELU_A * x * x))


def _layer_norm_stats(v):
    mu = jnp.mean(v, axis=-1, keepdims=True)
    vc = v - mu
    rstd = lax.rsqrt(jnp.mean(vc * vc, axis=-1, keepdims=True) + EPS)
    return vc * rstd, rstd


def _layer_norm_bwd(dyhat, yhat, rstd):
    return rstd * (dyhat - jnp.mean(dyhat, axis=-1, keepdims=True)
                   - yhat * jnp.mean(dyhat * yhat, axis=-1, keepdims=True))


def _sg_mask():
    t = lax.broadcasted_iota(jnp.int32, (SG_CHUNK, SG_CHUNK), 0) // SG_CAUSAL
    s = lax.broadcasted_iota(jnp.int32, (SG_CHUNK, SG_CHUNK), 1) // SG_CAUSAL
    return s <= t


def _sg_mix(ws_ref, bc_ref, vln_sc, vo_sc, tr, gd):
    mask = _sg_mask()
    for g in range(SG_GROUPS):
        wg = jnp.where(mask, ws_ref[g], 0.0).astype(BF16)
        cols = slice(g * gd, (g + 1) * gd)
        for n in range(tr // SG_CHUNK):
            rows = slice(n * SG_CHUNK, (n + 1) * SG_CHUNK)
            vo_sc[rows, cols] = lax.dot_general(wg, vln_sc[rows, cols], _NN,
                                                preferred_element_type=F32) + bc_ref[g]


def _sg_fwd(a_uv, ln_g, ln_b, ws, bcol, *, name):
    T, W = a_uv.shape[0], a_uv.shape[1] // 2
    gd = W // SG_GROUPS
    tr = _row_tile(T)

    def body(u_ref, v_ref, g_ref, b_ref, ws_ref, bc_ref, o_ref, vln_sc, vo_sc):
        u, _ = _gelu(u_ref[...])
        v, _ = _gelu(v_ref[...])
        vhat, _ = _layer_norm_stats(v)
        vln_sc[...] = (vhat * g_ref[...] + b_ref[...]).astype(BF16)
        _sg_mix(ws_ref, bc_ref, vln_sc, vo_sc, tr, gd)
        o_ref[...] = (u * vo_sc[...]).astype(BF16)

    row = pl.BlockSpec((1, W), lambda i: (0, 0))
    return pl.pallas_call(
        body, name=name, grid=(T // tr,),
        in_specs=[pl.BlockSpec((tr, W), lambda i: (i, 0)), pl.BlockSpec((tr, W), lambda i: (i, 1)), row, row,
                  pl.BlockSpec((SG_GROUPS, SG_CHUNK, SG_CHUNK), lambda i: (0, 0, 0)),
                  pl.BlockSpec((SG_GROUPS, SG_CHUNK, 1), lambda i: (0, 0, 0))],
        out_specs=pl.BlockSpec((tr, W), lambda i: (i, 0)),
        out_shape=jax.ShapeDtypeStruct((T, W), BF16),
        scratch_shapes=[pltpu.VMEM((tr, W), BF16), pltpu.VMEM((tr, W), F32)],
        compiler_params=_cparams("parallel"),
    )(a_uv, a_uv, ln_g, ln_b, ws, bcol)


def _sg_bwd(a_uv, dgate, ln_g, ln_b, ws, bcol, *, name):
    T, W = a_uv.shape[0], a_uv.shape[1] // 2
    gd = W // SG_GROUPS
    tr = _row_tile(T)

    def body(u_ref, v_ref, dg_ref, g_ref, b_ref, ws_ref, bc_ref,
             da_ref, dws_ref, dbs_ref, sums_ref, vln_sc, vo_sc, dvo_sc, dvln_sc):
        i = pl.program_id(0)

        @pl.when(i == 0)
        def _():
            dws_ref[...] = jnp.zeros(dws_ref.shape, F32)
            dbs_ref[...] = jnp.zeros(dbs_ref.shape, F32)
            sums_ref[...] = jnp.zeros(sums_ref.shape, F32)

        ua, va = u_ref[...], v_ref[...]
        u, tu = _gelu(ua)
        v, tv = _gelu(va)
        vhat, rstd = _layer_norm_stats(v)
        vln_sc[...] = (vhat * g_ref[...] + b_ref[...]).astype(BF16)
        _sg_mix(ws_ref, bc_ref, vln_sc, vo_sc, tr, gd)
        dgt = dg_ref[...]
        du = dgt * vo_sc[...]
        dvo_sc[...] = dgt * u
        mask = _sg_mask()
        for g in range(SG_GROUPS):
            wg = jnp.where(mask, ws_ref[g], 0.0).astype(BF16)
            cols = slice(g * gd, (g + 1) * gd)
            acc_w = jnp.zeros((SG_CHUNK, SG_CHUNK), F32)
            acc_b = jnp.zeros((SG_CHUNK, 1), F32)
            for n in range(tr // SG_CHUNK):
                rows = slice(n * SG_CHUNK, (n + 1) * SG_CHUNK)
                dvo = dvo_sc[rows, cols]
                dvob = dvo.astype(BF16)
                dvln_sc[rows, cols] = lax.dot_general(wg, dvob, _TN, preferred_element_type=F32)
                acc_w += lax.dot_general(dvob, vln_sc[rows, cols], _NT, preferred_element_type=F32)
                acc_b += jnp.sum(dvo, axis=1, keepdims=True)
            dws_ref[g] += jnp.where(mask, acc_w, 0.0)
            dbs_ref[g] += acc_b
        dvln = dvln_sc[...]
        sums_ref[...] += jnp.concatenate([jnp.sum(dvln * vhat, axis=0, keepdims=True),
                                          jnp.sum(dvln, axis=0, keepdims=True),
                                          jnp.zeros((6, W), F32)], axis=0)
        dv = _layer_norm_bwd(dvln * g_ref[...], vhat, rstd)
        da_ref[:, :W] = (du * _gelu_grad(ua, tu)).astype(BF16)
        da_ref[:, W:] = (dv * _gelu_grad(va, tv)).astype(BF16)

    row = pl.BlockSpec((1, W), lambda i: (0, 0))
    wspec = pl.BlockSpec((SG_GROUPS, SG_CHUNK, SG_CHUNK), lambda i: (0, 0, 0))
    bspec = pl.BlockSpec((SG_GROUPS, SG_CHUNK, 1), lambda i: (0, 0, 0))
    return pl.pallas_call(
        body, name=name, grid=(T // tr,),
        in_specs=[pl.BlockSpec((tr, W), lambda i: (i, 0)), pl.BlockSpec((tr, W), lambda i: (i, 1)),
                  pl.BlockSpec((tr, W), lambda i: (i, 0)), row, row, wspec, bspec],
        out_specs=[pl.BlockSpec((tr, 2 * W), lambda i: (i, 0)), wspec, bspec,
                   pl.BlockSpec((8, W), lambda i: (0, 0))],
        out_shape=[jax.ShapeDtypeStruct((T, 2 * W), BF16),
                   jax.ShapeDtypeStruct((SG_GROUPS, SG_CHUNK, SG_CHUNK), F32),
                   jax.ShapeDtypeStruct((SG_GROUPS, SG_CHUNK, 1), F32),
                   jax.ShapeDtypeStruct((8, W), F32)],
        scratch_shapes=[pltpu.VMEM((tr, W), BF16), pltpu.VMEM((tr, W), F32),
                        pltpu.VMEM((tr, W), F32), pltpu.VMEM((tr, W), F32)],
        compiler_params=_cparams("arbitrary"),
    )(a_uv, a_uv, dgate, ln_g, ln_b, ws, bcol)


def _cv_glu_conv(a_ref, b_ref, ap_ref, bp_ref, w_ref, bd_ref, xc_sc, tr):
    i = pl.program_id(0)
    prev = ap_ref[...] * jax.nn.sigmoid(bp_ref[...])
    xc_sc[0:CONV_HALO, :] = jnp.where(i > 0, prev, 0.0)
    xc_sc[CONV_HALO:, :] = a_ref[...] * jax.nn.sigmoid(b_ref[...])
    acc = jnp.broadcast_to(bd_ref[...], (tr, bd_ref.shape[1]))
    for k in range(CONV_WIDTH):
        acc = acc + w_ref[k:k + 1, :] * xc_sc[pl.ds(CONV_HALO - (CONV_WIDTH - 1) + k, tr), :]
    return acc


def _cv_specs(T, C, tr):
    hb = tr // CONV_HALO
    cur = lambda col: pl.BlockSpec((tr, C), lambda i: (i, col))
    prev = lambda col: pl.BlockSpec((CONV_HALO, C), lambda i: (jnp.maximum(i * hb - 1, 0), col))
    row = pl.BlockSpec((1, C), lambda i: (0, 0))
    wspec = pl.BlockSpec((CONV_HALO, C), lambda i: (0, 0))
    return cur, prev, row, wspec


def _cv_fwd(p, w_dw, b_dw, ln_g, ln_b, *, name):
    T, C = p.shape[0], p.shape[1] // 2
    tr = _row_tile(T)
    cur, prev, row, wspec = _cv_specs(T, C, tr)

    def body(a_ref, b_ref, ap_ref, bp_ref, w_ref, bd_ref, g_ref, be_ref, o_ref, xc_sc):
        y2 = _cv_glu_conv(a_ref, b_ref, ap_ref, bp_ref, w_ref, bd_ref, xc_sc, tr)
        yhat, _ = _layer_norm_stats(y2)
        yln = yhat * g_ref[...] + be_ref[...]
        o_ref[...] = (yln * jax.nn.sigmoid(yln)).astype(BF16)

    return pl.pallas_call(
        body, name=name, grid=(T // tr,),
        in_specs=[cur(0), cur(1), prev(0), prev(1), wspec, row, row, row],
        out_specs=pl.BlockSpec((tr, C), lambda i: (i, 0)),
        out_shape=jax.ShapeDtypeStruct((T, C), BF16),
        scratch_shapes=[pltpu.VMEM((tr + CONV_HALO, C), F32)],
        compiler_params=_cparams("parallel"),
    )(p, p, p, p, w_dw, b_dw, ln_g, ln_b)


def _cv_bwd_ln(p, dy3, w_dw, b_dw, ln_g, ln_b, *, name):
    T, C = p.shape[0], p.shape[1] // 2
    tr = _row_tile(T)
    cur, prev, row, wspec = _cv_specs(T, C, tr)

    def body(a_ref, b_ref, ap_ref, bp_ref, dy_ref, w_ref, bd_ref, g_ref, be_ref,
             dy2_ref, dw_ref, sums_ref, xc_sc):
        i = pl.program_id(0)
        y2 = _cv_glu_conv(a_ref, b_ref, ap_ref, bp_ref, w_ref, bd_ref, xc_sc, tr)
        yhat, rstd = _layer_norm_stats(y2)
        yln = yhat * g_ref[...] + be_ref[...]
        s = jax.nn.sigmoid(yln)
        dyln = dy_ref[...] * (s + yln * s * (1.0 - s))
        dy2 = _layer_norm_bwd(dyln * g_ref[...], yhat, rstd)
        dy2_ref[...] = dy2
        sums = jnp.concatenate([jnp.sum(dy2, axis=0, keepdims=True),
                                jnp.sum(dyln * yhat, axis=0, keepdims=True),
                                jnp.sum(dyln, axis=0, keepdims=True),
                                jnp.zeros((5, C), F32)], axis=0)
        taps = [jnp.sum(dy2 * xc_sc[pl.ds(CONV_HALO - (CONV_WIDTH - 1) + k, tr), :], axis=0, keepdims=True)
                for k in range(CONV_WIDTH)]
        dw = jnp.concatenate(taps + [jnp.zeros((CONV_HALO - CONV_WIDTH, C), F32)], axis=0)

        @pl.when(i == 0)
        def _():
            sums_ref[...] = sums
            dw_ref[...] = dw

        @pl.when(i > 0)
        def _():
            sums_ref[...] += sums
            dw_ref[...] += dw

    blk = pl.BlockSpec((tr, C), lambda i: (i, 0))
    return pl.pallas_call(
        body, name=name, grid=(T // tr,),
        in_specs=[cur(0), cur(1), prev(0), prev(1), blk, wspec, row, row, row],
        out_specs=[blk, wspec, pl.BlockSpec((8, C), lambda i: (0, 0))],
        out_shape=[jax.ShapeDtypeStruct((T, C), F32), jax.ShapeDtypeStruct((CONV_HALO, C), F32),
                   jax.ShapeDtypeStruct((8, C), F32)],
        scratch_shapes=[pltpu.VMEM((tr + CONV_HALO, C), F32)],
        compiler_params=_cparams("arbitrary"),
    )(p, p, p, p, dy3, w_dw, b_dw, ln_g, ln_b)


def _cv_bwd_in(p, dy2, w_dw, *, name):
    T, C = p.shape[0], p.shape[1] // 2
    tr = _row_tile(T)
    hb = tr // CONV_HALO
    nblk = T // tr
    last_halo = T // CONV_HALO - 1

    def body(a_ref, b_ref, dy_ref, dyn_ref, w_ref, dp_ref, sums_ref, xc_sc):
        i = pl.program_id(0)
        xc_sc[0:tr, :] = dy_ref[...]
        xc_sc[tr:, :] = jnp.where(i < nblk - 1, dyn_ref[...], 0.0)
        dy1 = jnp.zeros((tr, C), F32)
        for k in range(CONV_WIDTH):
            dy1 = dy1 + w_ref[k:k + 1, :] * xc_sc[pl.ds(CONV_WIDTH - 1 - k, tr), :]
        a = a_ref[...]
        sb = jax.nn.sigmoid(b_ref[...])
        da = dy1 * sb
        db = dy1 * a * sb * (1.0 - sb)
        dp_ref[:, :C] = da.astype(BF16)
        dp_ref[:, C:] = db.astype(BF16)
        sums = jnp.concatenate([
            jnp.concatenate([jnp.sum(da, axis=0, keepdims=True), jnp.sum(db, axis=0, keepdims=True)], axis=1),
            jnp.zeros((7, 2 * C), F32)], axis=0)

        @pl.when(i == 0)
        def _():
            sums_ref[...] = sums

        @pl.when(i > 0)
        def _():
            sums_ref[...] += sums

    blk = lambda col: pl.BlockSpec((tr, C), lambda i: (i, col))
    return pl.pallas_call(
        body, name=name, grid=(nblk,),
        in_specs=[blk(0), blk(1), blk(0),
                  pl.BlockSpec((CONV_HALO, C), lambda i: (jnp.minimum((i + 1) * hb, last_halo), 0)),
                  pl.BlockSpec((CONV_HALO, C), lambda i: (0, 0))],
        out_specs=[pl.BlockSpec((tr, 2 * C), lambda i: (i, 0)), pl.BlockSpec((8, 2 * C), lambda i: (0, 0))],
        out_shape=[jax.ShapeDtypeStruct((T, 2 * C), BF16), jax.ShapeDtypeStruct((8, 2 * C), F32)],
        scratch_shapes=[pltpu.VMEM((tr + CONV_HALO, C), F32)],
        compiler_params=_cparams("arbitrary"),
    )(p, p, dy2, dy2, w_dw)


def _col_tile(n, want=1024):
    best = LANE
    for t in range(LANE, min(n, want) + 1, LANE):
        if n % t == 0:
            best = t
    return best if n % LANE == 0 else n


def _mm(a, b, *, name, ta=False, tb=False, **kw):
    M = a.shape[1] if ta else a.shape[0]
    N = b.shape[0] if tb else b.shape[1]
    K = a.shape[0] if ta else a.shape[1]
    kw.setdefault('tm', _col_tile(M, 1024 if ta else 512))
    kw.setdefault('tn', _col_tile(N, 1024))
    kw.setdefault('tk', _col_tile(K, 512 if ta else 1024))
    return _matmul(a, b, name=name, ta=ta, tb=tb, **kw)


def _relu2_epilogue(acc):
    r = jnp.maximum(acc, 0.0)
    return acc, r * r


def _residual_epilogue(acc, x, g):
    return acc, x + g * acc


def _residual_bias_epilogue(acc, x, g, b):
    y = acc + b
    return y, x + g * y


def _relu2_bwd_epilogue(acc, a):
    return (acc * (2.0 * jnp.maximum(a, 0.0)),)


def _bias_epilogue(acc, b):
    return (acc + b,)


def _fox_forward(h1, P, j, D):
    H = D // HEAD_DIM
    proj = _mm(h1, P['fox_w_in'][j], name='fox_proj', b_outer=True)
    qg = jnp.tile(P['fox_q_norm'][j][None, :], (1, 2))
    kg = jnp.tile(P['fox_k_norm'][j][None, :], (1, 2))
    fpre_t = proj[:, 3 * D:3 * D + H].T
    bf = P['fox_b_f'][j][:, None]
    fcum = _fox_gate_fwd(fpre_t, bf, name='fox_gate_fwd')
    qa, qta, ka, kta, va, vta = _fox_prep_fwd(proj, qg, kg, fcum[:, :, None], d_model=D, name='fox_prep_fwd')
    o, qb = _fox_attn_fwd(qa, kta, va, name='fox_attn_fwd')
    saved = dict(proj=proj, qg=qg, kg=kg, fpre_t=fpre_t, bf=bf, o=o, qb=qb, qta=qta, ka=ka, kta=kta, vta=vta)
    return o, saved


def _fox_backward(dy, h1, S, P, j, D):
    H = D // HEAD_DIM
    w_out, w_in = P['fox_w_out'][j], P['fox_w_in'][j]
    g = {}
    g['fox_w_out'] = _mm(S['o'], dy, ta=True, name='fox_dw_out')
    do = _mm(dy, w_out, tb=True, name='fox_do')
    doa, dota = _fox_do_prep(do, S['o'], name='fox_do_prep')
    dq, dkt, dvt, dcol, drow = _fox_attn_bwd(S['qb'], S['qta'], S['ka'], S['kta'], S['vta'], doa, dota,
                                             name='fox_attn_bwd')
    dqp, dkp, dvp, gsum = _fox_prep_bwd(S['proj'], dq, dkt, dvt, S['qg'], S['kg'], d_model=D, name='fox_prep_bwd')
    dfpre_t, dbf = _fox_gate_bwd(dcol[:, 0, :], drow[:, :, 0], S['fpre_t'], S['bf'], name='fox_gate_bwd')
    dfpre = jnp.pad(dfpre_t.T.astype(BF16), ((0, 0), (0, LANE - H)))
    dproj = jnp.concatenate([dqp, dkp, dvp, dfpre], axis=1)
    g['fox_w_in'] = _mm(h1, dproj, ta=True, name='fox_dw_in')[:, :3 * D + H]
    g['fox_b_f'] = dbf[:, 0]
    g['fox_q_norm'] = gsum[0, :HEAD_DIM]
    g['fox_k_norm'] = gsum[1, :HEAD_DIM]
    dh1 = _mm(dproj, w_in, tb=True, name='fox_dh')
    return dh1, g


def _sg_forward(h1, P, D):
    a_uv = _mm(h1, P['sg_w_in'], name='sg_in', b_outer=True)
    bcol = P['sg_b_s'][:, :, None]
    gate = _sg_fwd(a_uv, P['sg_ln_g'], P['sg_ln_b'], P['sg_w_s'], bcol, name='sg_fwd')
    return gate, dict(a_uv=a_uv, bcol=bcol, gate=gate)


def _sg_backward(dy, h1, S, P, D):
    g = {}
    g['sg_w_out'] = _mm(S['gate'], dy, ta=True, name='sg_dw_out')
    dgate = _mm(dy, P['sg_w_out'], tb=True, name='sg_dgate')
    da, dws, dbs, sums = _sg_bwd(S['a_uv'], dgate, P['sg_ln_g'], P['sg_ln_b'], P['sg_w_s'], S['bcol'],
                                 name='sg_bwd')
    g['sg_w_s'], g['sg_b_s'] = dws, dbs[:, :, 0]
    g['sg_ln_g'], g['sg_ln_b'] = sums[0], sums[1]
    g['sg_w_in'] = _mm(h1, da, ta=True, name='sg_dw_in')
    dh1 = _mm(da, P['sg_w_in'], tb=True, name='sg_dh')
    return dh1, g


def _cv_forward(h1, P, D):
    p = _mm(h1, P['cv_w_pw1'], name='cv_pw1', extras=[(P['cv_b_pw1'], 'row')], epilogue=_bias_epilogue,
            b_outer=True)
    w_dw = jnp.pad(P['cv_w_dw'], ((0, CONV_HALO - CONV_WIDTH), (0, 0)))
    y3 = _cv_fwd(p, w_dw, P['cv_b_dw'], P['cv_ln_g'], P['cv_ln_b'], name='cv_fwd')
    return y3, dict(p=p, w_dw=w_dw, y3=y3)


def _cv_backward(dy, h1, S, P, D):
    g = {}
    g['cv_w_pw2'] = _mm(S['y3'], dy, ta=True, name='cv_dw_pw2')
    dy3 = _mm(dy, P['cv_w_pw2'], tb=True, name='cv_dy3')
    dy2, dw, sums = _cv_bwd_ln(S['p'], dy3, S['w_dw'], P['cv_b_dw'], P['cv_ln_g'], P['cv_ln_b'], name='cv_bwd_ln')
    g['cv_w_dw'] = dw[:CONV_WIDTH]
    g['cv_b_dw'], g['cv_ln_g'], g['cv_ln_b'] = sums[0], sums[1], sums[2]
    dp, psum = _cv_bwd_in(S['p'], dy2, S['w_dw'], name='cv_bwd_in')
    g['cv_b_pw1'] = psum[0]
    g['cv_w_pw1'] = _mm(h1, dp, ta=True, name='cv_dw_pw1')
    dh1 = _mm(dp, P['cv_w_pw1'], tb=True, name='cv_dh')
    return dh1, g


def _local_step(x, target, mod, P):
    T, D = x.shape
    L = mod.shape[0]
    saved = []
    for i in range(L):
        kind, j = i % N_MIXERS, i // N_MIXERS
        m = [mod[i:i + 1, k * D:(k + 1) * D] for k in range(6)]
        sh_m, sc_m, g_m, sh_f, sc_f, g_f = m
        w_mix, w_mlp = P['norm_mix'][i:i + 1], P['norm_mlp'][i:i + 1]
        h1 = _norm_mod_fwd(x, w_mix, sc_m, sh_m, name='norm_mix_fwd')
        if kind == 0:
            op, S = _fox_forward(h1, P, j, D)
            y, x1 = _mm(op, P['fox_w_out'][j], name='fox_out', extras=[(x, 'tile'), (g_m, 'row')],
                        epilogue=_residual_epilogue, out_dtypes=(F32, F32))
        elif kind == 1:
            op, S = _sg_forward(h1, P, D)
            y, x1 = _mm(op, P['sg_w_out'], name='sg_out', extras=[(x, 'tile'), (g_m, 'row')],
                        epilogue=_residual_epilogue, out_dtypes=(F32, F32))
        else:
            op, S = _cv_forward(h1, P, D)
            y, x1 = _mm(op, P['cv_w_pw2'], name='cv_out',
                        extras=[(x, 'tile'), (g_m, 'row'), (P['cv_b_pw2'], 'row')],
                        epilogue=_residual_bias_epilogue, out_dtypes=(F32, F32))
        h2 = _norm_mod_fwd(x1, w_mlp, sc_f, sh_f, name='norm_mlp_fwd')
        a, r = _mm(h2, P['w_mlp_in'][i], name='mlp_in', epilogue=_relu2_epilogue, out_dtypes=(F32, BF16),
                   b_outer=True)
        z, x2 = _mm(r, P['w_mlp_out'][i], name='mlp_out', extras=[(x1, 'tile'), (g_f, 'row')],
                    epilogue=_residual_epilogue, out_dtypes=(F32, F32), tk=P['w_mlp_out'].shape[1])
        saved.append(dict(x=x, h1=h1, S=S, y=y, x1=x1, h2=h2, a=a, r=r, z=z, m=m))
        x = x2

    loss_part, dx = _loss_head(x, target, name='loss_head')

    grads = {k: [None] * L for k in ('w_mlp_in', 'w_mlp_out', 'norm_mix', 'norm_mlp')}
    mix_grads = {}
    dmod = [None] * L
    for i in reversed(range(L)):
        kind, j = i % N_MIXERS, i // N_MIXERS
        sv = saved[i]
        sh_m, sc_m, g_m, sh_f, sc_f, g_f = sv['m']
        w_mix, w_mlp = P['norm_mix'][i:i + 1], P['norm_mlp'][i:i + 1]
        dz, dgf = _gate_bwd(dx, sv['z'], g_f, name='mlp_gate_bwd')
        grads['w_mlp_out'][i] = _mm(sv['r'], dz, ta=True, name='mlp_dw_out')
        da = _mm(dz, P['w_mlp_out'][i], tb=True, name='mlp_da', extras=[(sv['a'], 'tile')],
                 epilogue=_relu2_bwd_epilogue, out_dtypes=(BF16,), b_outer=True)
        grads['w_mlp_in'][i] = _mm(sv['h2'], da, ta=True, name='mlp_dw_in')
        dh2 = _mm(da, P['w_mlp_in'][i], tb=True, name='mlp_dh', tk=P['w_mlp_in'].shape[2])
        dx1, sums_f, dy = _norm_mod_bwd(dh2, sv['x1'], dx, w_mlp, sc_f, name='norm_mlp_bwd', gate=(sv['y'], g_m))
        if kind == 0:
            dh1, g = _fox_backward(dy, sv['h1'], sv['S'], P, j, D)
        elif kind == 1:
            dh1, g = _sg_backward(dy, sv['h1'], sv['S'], P, D)
        else:
            dh1, g = _cv_backward(dy, sv['h1'], sv['S'], P, D)
            g['cv_b_pw2'] = sums_f[4]
        for k, val in g.items():
            mix_grads.setdefault(k, {})[j] = val
        dx, sums_m = _norm_mod_bwd(dh1, sv['x'], dx1, w_mix, sc_m, name='norm_mix_bwd')
        grads['norm_mlp'][i], grads['norm_mix'][i] = sums_f[2], sums_m[2]
        dmod[i] = jnp.concatenate([sums_m[0], sums_m[1], sums_f[3], sums_f[0], sums_f[1], dgf[0]])

    out = {k: jnp.stack(v) for k, v in grads.items()}
    for k, per_j in mix_grads.items():
        out[k] = jnp.stack([per_j[j] for j in sorted(per_j)])
    return loss_part, dx, jnp.stack(dmod), out


def _position():
    return lax.axis_index("x"), lax.axis_index("y"), lax.axis_index("c")


def _all_gather8(blocks, *, name):
    n = len(blocks)

    def body(*refs):
        x_refs, out_refs = refs[:n], refs[n:2 * n]
        send_sems, recv_sems, local_sems = refs[2 * n:]
        x, y, c = _position()
        me, sibling = (x, y, c), (x, y, 1 - c)
        chips = [(1 - x, y), (x, 1 - y), (1 - x, 1 - y)]

        def slot(a, px, py, pc):
            return out_refs[a].at[4 * px + 2 * py + pc]

        def copy(a, k, blk, to, src=None):
            return pltpu.make_async_remote_copy(
                src_ref=slot(a, *blk) if src is None else src, dst_ref=slot(a, *blk),
                send_sem=send_sems.at[7 * a + k], recv_sem=recv_sems.at[7 * a + k],
                device_id=to, device_id_type=MESH)

        mine = [pltpu.make_async_copy(x_refs[a], slot(a, *me), local_sems.at[a]) for a in range(n)]
        for cp in mine:
            cp.start()
        first = []
        for j, chip in enumerate(chips):
            first += [copy(a, 1 + j, me, (*chip, c), src=x_refs[a]) for a in range(n)]
        first += [copy(a, 0, me, sibling, src=x_refs[a]) for a in range(n)]
        for cp in first:
            cp.start()
        passed = []
        for j, chip in enumerate(chips):
            for a in range(n):
                copy(a, 1 + j, (*chip, c), me).wait_recv()
                passed.append(copy(a, 4 + j, (*chip, c), sibling))
                passed[-1].start()
        for a in range(n):
            copy(a, 0, sibling, me).wait_recv()
        for j, chip in enumerate(chips):
            for a in range(n):
                copy(a, 4 + j, (*chip, 1 - c), me).wait_recv()
        for cp in first + passed:
            cp.wait_send()
        for cp in mine:
            cp.wait()

    return pl.pallas_call(
        body, name=name, in_specs=[ANY] * n, out_specs=[ANY] * n,
        out_shape=[jax.ShapeDtypeStruct((8,) + b.shape, b.dtype) for b in blocks],
        scratch_shapes=[pltpu.SemaphoreType.DMA((7 * n,)), pltpu.SemaphoreType.DMA((7 * n,)),
                        pltpu.SemaphoreType.DMA((n,))],
    )(*blocks)


def _exchange(srcs, out_shapes, plan, n_remote, n_local, *, name, aliases=None):
    ns, no = len(srcs), len(out_shapes)

    def body(*refs):
        src_ref, out_ref = refs[:ns], refs[ns:ns + no]
        send_sems, recv_sems, local_sems = refs[ns + no:]
        x, y, c = _position()
        remote, local = plan(src_ref, out_ref, x, y, c)
        local_copies = [pltpu.make_async_copy(s, d, local_sems.at[i]) for i, (s, d) in enumerate(local)]
        for lc in local_copies:
            lc.start()

        def copy(k, s, d, peer):
            return pltpu.make_async_remote_copy(src_ref=s, dst_ref=d, send_sem=send_sems.at[k],
                                                recv_sem=recv_sems.at[k], device_id=peer, device_id_type=MESH)

        copies = [copy(k, s, d, peer) for k, (s, d, peer, _) in enumerate(remote)]
        for cp in copies:
            cp.start()
        for k, (s, _, peer, landing) in enumerate(remote):
            copy(k, s, landing, peer).wait_recv()
        for cp in copies:
            cp.wait_send()
        for lc in local_copies:
            lc.wait()

    return pl.pallas_call(
        body, name=name, in_specs=[ANY] * ns, out_specs=[ANY] * no, out_shape=list(out_shapes),
        scratch_shapes=[pltpu.SemaphoreType.DMA((n_remote,)), pltpu.SemaphoreType.DMA((n_remote,)),
                        pltpu.SemaphoreType.DMA((max(n_local, 1),))],
        input_output_aliases=aliases or {},
    )(*srcs)


CHIP_FLIPS = ((1, 0), (0, 1), (1, 1))


def _flip(v, f):
    return 1 - v if f else v


def _sum_rows_tile(R, C, budget=3 << 20):
    best = None
    for t in range(8, R + 1, 8):
        if R % t == 0 and t * C * 4 <= budget:
            best = t
    return best if best is not None else R


def _reduce_scatter(gps, *, wire_dtype):
    n = len(gps)
    x, y, c = _position()
    c_arr = jnp.reshape(c, (1,)).astype(jnp.int32)
    bc_arr = jnp.stack([2 * x + y, c]).astype(jnp.int32)

    def plan1(src, out, x, y, c):
        return [(src[a].at[b, 1 - c], out[a].at[b], (x, y, 1 - c), out[a].at[b])
                for a in range(n) for b in range(4)], []

    got1 = _exchange(gps, [jax.ShapeDtypeStruct((4,) + g.shape[2:], F32) for g in gps], plan1, 4 * n, 0,
                     name='rs_sibling')

    def sum_chip(c_ref, mine_ref, got_ref, out_ref):
        out_ref[...] = (mine_ref[...] + got_ref[...]).astype(out_ref.dtype)

    parts = []
    for gp, g1 in zip(gps, got1):
        _, _, R, C = gp.shape
        tr = _sum_rows_tile(R, C)
        parts.append(pl.pallas_call(
            sum_chip, name='rs_sum_chip',
            grid_spec=pltpu.PrefetchScalarGridSpec(
                num_scalar_prefetch=1, grid=(4, R // tr),
                in_specs=[pl.BlockSpec((None, None, tr, C), lambda b, r, cr: (b, cr[0], r, 0)),
                          pl.BlockSpec((None, tr, C), lambda b, r, cr: (b, r, 0))],
                out_specs=pl.BlockSpec((None, tr, C), lambda b, r, cr: (b, r, 0))),
            out_shape=jax.ShapeDtypeStruct((4, R, C), wire_dtype),
            compiler_params=_cparams("parallel", "parallel"),
        )(c_arr, gp, g1))

    def plan2(src, out, x, y, c):
        remote = []
        for k, (fx, fy) in enumerate(CHIP_FLIPS):
            px, py = _flip(x, fx), _flip(y, fy)
            remote += [(src[a].at[2 * px + py], out[a].at[k], (px, py, c), out[a].at[k]) for a in range(n)]
        return remote, []

    got2 = _exchange(parts, [jax.ShapeDtypeStruct((3,) + p.shape[1:], wire_dtype) for p in parts], plan2,
                     3 * n, 0, name='rs_chips')

    def sum_final(bc_ref, mine_ref, got1_ref, got2_ref, out_ref):
        acc = mine_ref[...] + got1_ref[...]
        for k in range(3):
            acc = acc + got2_ref[k].astype(F32)
        out_ref[...] = acc

    halves = []
    for gp, g1, g2 in zip(gps, got1, got2):
        _, _, R, C = gp.shape
        tr = _sum_rows_tile(R, C, budget=2 << 20)
        halves.append(pl.pallas_call(
            sum_final, name='rs_sum_final',
            grid_spec=pltpu.PrefetchScalarGridSpec(
                num_scalar_prefetch=1, grid=(R // tr,),
                in_specs=[pl.BlockSpec((None, None, tr, C), lambda r, bc: (bc[0], bc[1], r, 0)),
                          pl.BlockSpec((None, tr, C), lambda r, bc: (bc[0], r, 0)),
                          pl.BlockSpec((3, tr, C), lambda r, bc: (0, r, 0))],
                out_specs=pl.BlockSpec((None, tr, C), lambda r, bc: (bc[1], r, 0))),
            out_shape=jax.ShapeDtypeStruct((2, R, C), F32),
            compiler_params=_cparams("parallel"),
        )(bc_arr, gp, g1, g2))

    def plan3(src, out, x, y, c):
        return [(src[a].at[c], out[a].at[c], (x, y, 1 - c), out[a].at[1 - c]) for a in range(n)], []

    return _exchange(halves, [jax.ShapeDtypeStruct(h.shape, F32) for h in halves], plan3, n, 0,
                     name='rs_swap', aliases={a: a for a in range(n)})


def _sum8(gathered, *, name):
    _, R, C = gathered.shape

    def body(g_ref, o_ref):
        acc = g_ref[0]
        for k in range(1, 8):
            acc = acc + g_ref[k]
        o_ref[...] = acc

    return pl.pallas_call(body, name=name, out_shape=jax.ShapeDtypeStruct((R, C), F32))(gathered)


def _adamw(w, g, m, v, *, name):
    shape = w.shape
    cols = shape[-1]
    rows = w.size // cols
    tr = _sum_rows_tile(rows, cols, budget=1 << 20)

    def body(w_ref, g_ref, m_ref, v_ref, d_ref, mo_ref, vo_ref):
        gv = g_ref[...]
        mn = ADAM_B1 * m_ref[...] + (1.0 - ADAM_B1) * gv
        vn = ADAM_B2 * v_ref[...] + (1.0 - ADAM_B2) * (gv * gv)
        m_hat = mn / (1.0 - ADAM_B1 ** ADAM_STEP)
        v_hat = vn / (1.0 - ADAM_B2 ** ADAM_STEP)
        d_ref[...] = -ADAM_LR * (m_hat / (jnp.sqrt(v_hat) + ADAM_EPS) + ADAM_WD * w_ref[...])
        mo_ref[...] = mn
        vo_ref[...] = vn

    blk = pl.BlockSpec((tr, cols), lambda i: (i, 0))
    outs = pl.pallas_call(
        body, name=name, grid=(rows // tr,), in_specs=[blk] * 4, out_specs=[blk] * 3,
        out_shape=[jax.ShapeDtypeStruct((rows, cols), F32)] * 3,
        compiler_params=_cparams("parallel"),
    )(*[a.reshape(rows, cols) for a in (w, g, m, v)])
    return tuple(o.reshape(shape) for o in outs)


WEIGHTS = ['norm_mix', 'norm_mlp', 'w_ada', 'b_ada', 'w_mlp_in', 'w_mlp_out', 'fox_w_in', 'fox_b_f',
           'fox_q_norm', 'fox_k_norm', 'fox_w_out', 'sg_w_in', 'sg_ln_g', 'sg_ln_b', 'sg_w_s', 'sg_b_s',
           'sg_w_out', 'cv_w_pw1', 'cv_b_pw1', 'cv_w_dw', 'cv_b_dw', 'cv_ln_g', 'cv_ln_b', 'cv_w_pw2',
           'cv_b_pw2']
BIG = {'w_mlp_in': 2, 'w_mlp_out': 1, 'fox_w_in': 2, 'fox_w_out': 1, 'sg_w_in': 2, 'sg_w_out': 1,
       'cv_w_pw1': 2, 'cv_w_pw2': 1}
SMALL_SHARDED = ['cv_b_pw1', 'cv_w_dw', 'cv_b_dw', 'cv_ln_g', 'cv_ln_b', 'cv_b_pw2']
SMALL_GRADS = ['norm_mix', 'norm_mlp', 'fox_b_f', 'fox_q_norm', 'fox_k_norm', 'sg_ln_g', 'sg_ln_b', 'sg_w_s',
               'sg_b_s'] + SMALL_SHARDED
GRAD_WIRE_DTYPE = BF16


def _pack_rows(parts, cols):
    flat = jnp.concatenate([p.reshape(-1) for p in parts])
    rows = -(-flat.size // (8 * cols)) * 8
    return jnp.pad(flat, (0, rows * cols - flat.size)).reshape(rows, cols)


def _unpack(flat, shapes):
    out, off = [], 0
    for s in shapes:
        n = math.prod(s)
        out.append(flat[..., off:off + n].reshape(flat.shape[:-1] + tuple(s)))
        off += n
    return out


def _merge_chips(a, axis):
    a = jnp.moveaxis(a, 0, axis)
    return a.reshape(a.shape[:axis] + (a.shape[axis] * a.shape[axis + 1],) + a.shape[axis + 2:])


def _split_chips(a, axis):
    a = a.reshape(a.shape[:axis] + (4, a.shape[axis] // 4) + a.shape[axis + 1:])
    return jnp.moveaxis(a, axis, 0)


def _step(a):
    x, y, c = _position()
    me = 4 * x + 2 * y + c
    chip = 2 * x + y
    T, D = a['x'].shape[1], a['x'].shape[2]
    L = a['norm_mix'].shape[0]

    small_shapes = [(D,)] + [a[n].shape for n in SMALL_SHARDED]
    small = _all_gather8([_pack_rows([a['c']] + [a[n] for n in SMALL_SHARDED], LANE)], name='ag_small')[0]
    small = small.reshape(8, -1)
    c_all = _unpack(small, small_shapes[:1])[0]
    sharded = _unpack(small[0::2, D:], small_shapes[1:])
    P = {n: _merge_chips(v, v.ndim - 2) for n, v in zip(SMALL_SHARDED, sharded)}

    c_act = _silu_rows(c_all, name='c_act')
    mod_cols = jnp.stack([
        _mm(c_act, a['w_ada'][i], name='ada_mod', tm=8, tn=_col_tile(a['w_ada'].shape[2], 768),
            extras=[(lax.dynamic_slice_in_dim(a['b_ada'][i:i + 1], chip * a['w_ada'].shape[2],
                                              a['w_ada'].shape[2], axis=1), 'row')],
            epilogue=_bias_epilogue)
        for i in range(L)])
    mod_all = _all_gather8([mod_cols.reshape(L * 8, -1)], name='ag_mod')[0].reshape(8, L, 8, -1)
    mod = lax.dynamic_index_in_dim(mod_all[0::2], me, axis=2, keepdims=False)
    mod = jnp.moveaxis(mod, 0, 1).reshape(L, 6 * D)

    halves = [lax.dynamic_index_in_dim(a[n].astype(BF16).reshape(2, -1, a[n].shape[-1]), c, axis=0,
                                       keepdims=False) for n in BIG]
    gathered = _all_gather8(halves, name='ag_weights')
    n_heads = D // HEAD_DIM
    for n, gth in zip(BIG, gathered):
        blocks = gth.reshape((4,) + a[n].shape)
        if n == 'fox_w_in':
            pad = jnp.zeros(a[n].shape[:2] + (LANE - n_heads,), BF16)
            P[n] = jnp.concatenate([blocks[0], blocks[1], blocks[2], blocks[3], pad], axis=-1)
        else:
            P[n] = _merge_chips(blocks, BIG[n])
    for n in ('sg_w_in', 'sg_w_out', 'cv_w_pw1', 'cv_w_pw2', 'sg_w_s', 'sg_b_s', 'cv_w_dw'):
        P[n] = (P[n] if n in P else a[n])[0]
    for n in ('norm_mix', 'norm_mlp', 'fox_b_f', 'fox_q_norm', 'fox_k_norm', 'sg_ln_g', 'sg_ln_b'):
        P[n] = a[n]

    loss_part, grad_x, dmod, g = _local_step(a['x'][0], a['loss_target'][0], mod, P)

    small_g = [dmod, loss_part[0:1, 0:1]] + [g[n] for n in SMALL_GRADS]
    small_g_shapes = [s.shape for s in small_g]
    all_small = _all_gather8([_pack_rows(small_g, LANE)], name='ag_small_grads')[0]
    summed = _sum8(all_small, name='sum_small_grads').reshape(-1)
    sums = _unpack(summed, small_g_shapes)
    loss = sums[1][0, 0]
    grads = dict(zip(SMALL_GRADS, sums[2:]))
    grads['b_ada'] = sums[0]
    for n in SMALL_SHARDED:
        blk = a[n].shape[-1]
        grads[n] = lax.dynamic_slice_in_dim(grads[n], chip * blk, blk, axis=grads[n].ndim - 1)
    dmod_all = all_small.reshape(8, -1)[:, :dmod.size].reshape(8, L, 6 * D)
    cols = a['w_ada'].shape[2]
    dmod_cols = lax.dynamic_slice_in_dim(dmod_all, chip * cols, cols, axis=2)
    pad8 = lambda t: jnp.pad(t, ((0, LANE - 8), (0, 0)))
    c_act_pad = pad8(c_act)
    grads['w_ada'] = jnp.stack([
        _mm(c_act_pad, pad8(dmod_cols[:, i]), ta=True, name='ada_dw', tn=_col_tile(cols, 768))
        for i in range(L)])

    gps = [_split_chips(g[n], BIG[n]).reshape(4, 2, -1, a[n].shape[-1]) for n in BIG]
    for n, shard in zip(BIG, _reduce_scatter(gps, wire_dtype=GRAD_WIRE_DTYPE)):
        grads[n] = shard.reshape(a[n].shape)

    deltas, new_m, new_v = {}, {}, {}
    for n in WEIGHTS:
        deltas[n], new_m[n], new_v[n] = _adamw(a[n], grads[n], a['m_' + n], a['v_' + n], name='adamw')
    return (loss, grad_x[None], *[grads[n] for n in WEIGHTS], *[deltas[n] for n in WEIGHTS],
            *[new_m[n] for n in WEIGHTS], *[new_v[n] for n in WEIGHTS])


def _silu_rows(x, *, name):
    def body(x_ref, o_ref):
        xv = x_ref[...]
        o_ref[...] = (xv * jax.nn.sigmoid(xv)).astype(BF16)

    return pl.pallas_call(body, name=name, out_shape=jax.ShapeDtypeStruct(x.shape, BF16))(x)


def kernel(x, c, norm_mix, norm_mlp, w_ada, b_ada, w_mlp_in, w_mlp_out, fox_w_in, fox_b_f, fox_q_norm, fox_k_norm, fox_w_out, sg_w_in, sg_ln_g, sg_ln_b, sg_w_s, sg_b_s, sg_w_out, cv_w_pw1, cv_b_pw1, cv_w_dw, cv_b_dw, cv_ln_g, cv_ln_b, cv_w_pw2, cv_b_pw2, loss_target, m_norm_mix, m_norm_mlp, m_w_ada, m_b_ada, m_w_mlp_in, m_w_mlp_out, m_fox_w_in, m_fox_b_f, m_fox_q_norm, m_fox_k_norm, m_fox_w_out, m_sg_w_in, m_sg_ln_g, m_sg_ln_b, m_sg_w_s, m_sg_b_s, m_sg_w_out, m_cv_w_pw1, m_cv_b_pw1, m_cv_w_dw, m_cv_b_dw, m_cv_ln_g, m_cv_ln_b, m_cv_w_pw2, m_cv_b_pw2, v_norm_mix, v_norm_mlp, v_w_ada, v_b_ada, v_w_mlp_in, v_w_mlp_out, v_fox_w_in, v_fox_b_f, v_fox_q_norm, v_fox_k_norm, v_fox_w_out, v_sg_w_in, v_sg_ln_g, v_sg_ln_b, v_sg_w_s, v_sg_b_s, v_sg_w_out, v_cv_w_pw1, v_cv_b_pw1, v_cv_w_dw, v_cv_b_dw, v_cv_ln_g, v_cv_ln_b, v_cv_w_pw2, v_cv_b_pw2):
    return _step(dict(locals()))
```

```python
import math
from typing import Callable, NamedTuple

import jax
import jax.numpy as jnp
from jax import lax
from jax.experimental import pallas as pl
from jax.experimental.pallas import tpu as pltpu

F32 = jnp.float32
BF16 = jnp.bfloat16

EPS = 1e-6
HEAD_DIM = 64
LANE = 128
CONV_WIDTH = 31
CONV_HALO = 32
SG_CHUNK = 128
SG_CAUSAL = 64
SG_GROUPS = 8
N_MIXERS = 3
VMEM_LIMIT = 56 * 1024 * 1024
NEG = -1e30

ADAM_LR = 0.001
ADAM_B1 = 0.9
ADAM_B2 = 0.999
ADAM_EPS = 1e-08
ADAM_WD = 0.01
ADAM_STEP = 10

MESH = pl.DeviceIdType.MESH
ANY = pl.BlockSpec(memory_space=pl.ANY)


def _cparams(*sem):
    return pltpu.CompilerParams(dimension_semantics=sem, vmem_limit_bytes=VMEM_LIMIT)


def _row_tile(t, want=512):
    return min(t, want)


def _matmul(a, b, *, name, ta=False, tb=False, tm=512, tn=1024, tk=1024,
            extras=(), epilogue=None, out_dtypes=(F32,), b_outer=False):
    M, K = (a.shape[1], a.shape[0]) if ta else a.shape
    N = b.shape[0] if tb else b.shape[1]
    assert (b.shape[1] if tb else b.shape[0]) == K
    tm, tn, tk = min(tm, M), min(tn, N), min(tk, K)
    assert M % tm == 0 and N % tn == 0 and K % tk == 0, (name, M, N, K, tm, tn, tk)
    nk = K // tk

    def spec(shape, pick):
        if b_outer:
            return pl.BlockSpec(shape, lambda j, i, k: pick(i, j, k))
        return pl.BlockSpec(shape, pick)

    a_spec = spec((tk, tm), lambda i, j, k: (k, i)) if ta else spec((tm, tk), lambda i, j, k: (i, k))
    b_spec = spec((tn, tk), lambda i, j, k: (j, k)) if tb else spec((tk, tn), lambda i, j, k: (k, j))
    ex_specs = [spec((tm, tn), lambda i, j, k: (i, j)) if kind == 'tile' else spec((1, tn), lambda i, j, k: (0, j))
                for _, kind in extras]
    dims = (((0,) if ta else (1,), (1,) if tb else (0,)), ((), ()))
    n_ex, n_out = len(extras), len(out_dtypes)

    def body(*refs):
        a_ref, b_ref = refs[0], refs[1]
        ex = refs[2:2 + n_ex]
        outs = refs[2 + n_ex:2 + n_ex + n_out]

        def finish(acc):
            vals = epilogue(acc, *[r[...] for r in ex]) if epilogue else (acc,)
            for o, v in zip(outs, vals):
                o[...] = v.astype(o.dtype)

        part = lax.dot_general(a_ref[...].astype(BF16), b_ref[...].astype(BF16), dims,
                               preferred_element_type=F32)
        if nk == 1:
            finish(part)
        else:
            acc_ref = refs[-1]
            k = pl.program_id(2)

            @pl.when(k == 0)
            def _():
                acc_ref[...] = part

            @pl.when(k > 0)
            def _():
                acc_ref[...] += part

            @pl.when(k == nk - 1)
            def _():
                finish(acc_ref[...])

    outs = pl.pallas_call(
        body, name=name,
        grid=(N // tn, M // tm, nk) if b_outer else (M // tm, N // tn, nk),
        in_specs=[a_spec, b_spec] + ex_specs,
        out_specs=[spec((tm, tn), lambda i, j, k: (i, j)) for _ in out_dtypes],
        out_shape=[jax.ShapeDtypeStruct((M, N), dt) for dt in out_dtypes],
        scratch_shapes=[pltpu.VMEM((tm, tn), F32)] if nk > 1 else [],
        compiler_params=_cparams("parallel", "parallel", "arbitrary"),
    )(a, b, *[arr for arr, _ in extras])
    return outs if n_out > 1 else outs[0]


def _norm_mod_fwd(x, w, sc, sh, *, name):
    T, D = x.shape
    tr = _row_tile(T)

    def body(x_ref, w_ref, sc_ref, sh_ref, h_ref):
        xv = x_ref[...]
        r = lax.rsqrt(jnp.mean(xv * xv, axis=-1, keepdims=True) + EPS)
        h_ref[...] = ((xv * r) * w_ref[...] * (1.0 + sc_ref[...]) + sh_ref[...]).astype(BF16)

    row = pl.BlockSpec((1, D), lambda i: (0, 0))
    return pl.pallas_call(
        body, name=name, grid=(T // tr,),
        in_specs=[pl.BlockSpec((tr, D), lambda i: (i, 0)), row, row, row],
        out_specs=pl.BlockSpec((tr, D), lambda i: (i, 0)),
        out_shape=jax.ShapeDtypeStruct((T, D), BF16),
        compiler_params=_cparams("parallel"),
    )(x, w, sc, sh)


def _norm_mod_bwd(dh, x, dres, w, sc, *, name, gate=None):
    T, D = x.shape
    tr = _row_tile(T)
    with_gate = gate is not None

    def body(*refs):
        if with_gate:
            dh_ref, x_ref, dres_ref, w_ref, sc_ref, y_ref, g_ref, dx_ref, sums_ref, dy_ref = refs
        else:
            dh_ref, x_ref, dres_ref, w_ref, sc_ref, dx_ref, sums_ref = refs
        i = pl.program_id(0)
        xv, dhv = x_ref[...], dh_ref[...].astype(F32)
        r = lax.rsqrt(jnp.mean(xv * xv, axis=-1, keepdims=True) + EPS)
        n = xv * r
        wv, scale = w_ref[...], 1.0 + sc_ref[...]
        dn = dhv * (wv * scale)
        dx = dres_ref[...] + r * (dn - n * jnp.mean(dn * n, axis=-1, keepdims=True))
        dx_ref[...] = dx
        rows = [jnp.sum(dhv, axis=0, keepdims=True),
                jnp.sum(dhv * (n * wv), axis=0, keepdims=True),
                jnp.sum(dhv * n * scale, axis=0, keepdims=True)]
        if with_gate:
            dy_ref[...] = (dx * g_ref[...]).astype(BF16)
            rows.append(jnp.sum(dx * y_ref[...], axis=0, keepdims=True))
            rows.append(jnp.sum(dx * g_ref[...], axis=0, keepdims=True))
        part = jnp.concatenate(rows + [jnp.zeros((8 - len(rows), D), F32)], axis=0)

        @pl.when(i == 0)
        def _():
            sums_ref[...] = part

        @pl.when(i > 0)
        def _():
            sums_ref[...] += part

    blk = pl.BlockSpec((tr, D), lambda i: (i, 0))
    row = pl.BlockSpec((1, D), lambda i: (0, 0))
    in_specs = [blk, blk, blk, row, row]
    args = [dh, x, dres, w, sc]
    out_specs = [blk, pl.BlockSpec((8, D), lambda i: (0, 0))]
    out_shape = [jax.ShapeDtypeStruct((T, D), F32), jax.ShapeDtypeStruct((8, D), F32)]
    if with_gate:
        in_specs += [blk, row]
        args += list(gate)
        out_specs.append(blk)
        out_shape.append(jax.ShapeDtypeStruct((T, D), BF16))
    return pl.pallas_call(
        body, name=name, grid=(T // tr,), in_specs=in_specs, out_specs=out_specs,
        out_shape=out_shape, compiler_params=_cparams("arbitrary"),
    )(*args)


def _gate_bwd(dx, y, g, *, name):
    T, D = dx.shape
    tr = _row_tile(T)

    def body(dx_ref, y_ref, g_ref, dy_ref, dg_ref):
        i = pl.program_id(0)
        dxv = dx_ref[...]
        dy_ref[...] = (dxv * g_ref[...]).astype(BF16)
        part = jnp.concatenate([jnp.sum(dxv * y_ref[...], axis=0, keepdims=True),
                                jnp.zeros((7, D), F32)], axis=0)

        @pl.when(i == 0)
        def _():
            dg_ref[...] = part

        @pl.when(i > 0)
        def _():
            dg_ref[...] += part

    blk = pl.BlockSpec((tr, D), lambda i: (i, 0))
    return pl.pallas_call(
        body, name=name, grid=(T // tr,),
        in_specs=[blk, blk, pl.BlockSpec((1, D), lambda i: (0, 0))],
        out_specs=[blk, pl.BlockSpec((8, D), lambda i: (0, 0))],
        out_shape=[jax.ShapeDtypeStruct((T, D), BF16), jax.ShapeDtypeStruct((8, D), F32)],
        compiler_params=_cparams("arbitrary"),
    )(dx, y, g)


def _loss_head(y, target, *, name):
    T, D = y.shape
    tr = _row_tile(T)

    def body(y_ref, t_ref, loss_ref, dy_ref):
        i = pl.program_id(0)
        e = y_ref[...] - t_ref[...]
        dy_ref[...] = e * (1.0 / D)
        part = jnp.full((8, LANE), 0.5 / D * jnp.sum(e * e), F32)

        @pl.when(i == 0)
        def _():
            loss_ref[...] = part

        @pl.when(i > 0)
        def _():
            loss_ref[...] += part

    blk = pl.BlockSpec((tr, D), lambda i: (i, 0))
    return pl.pallas_call(
        body, name=name, grid=(T // tr,), in_specs=[blk, blk],
        out_specs=[pl.BlockSpec((8, LANE), lambda i: (0, 0)), blk],
        out_shape=[jax.ShapeDtypeStruct((8, LANE), F32), jax.ShapeDtypeStruct((T, D), F32)],
        compiler_params=_cparams("arbitrary"),
    )(y, target)


def _position():
    return lax.axis_index("x"), lax.axis_index("y"), lax.axis_index("c")


class Comm(NamedTuple):
    srcs: list
    out_shapes: list
    plan: Callable
    n_remote: int
    n_local: int

    def scratch(self):
        return [pltpu.SemaphoreType.DMA((self.n_remote,)), pltpu.SemaphoreType.DMA((self.n_remote,)),
                pltpu.SemaphoreType.DMA((max(self.n_local, 1),))]


def _comm_copies(plan, src_refs, out_refs, send_sems, recv_sems, local_sems):
    x, y, c = _position()
    remote, local = plan(src_refs, out_refs, x, y, c)

    def copy(k, s, d, peer):
        return pltpu.make_async_remote_copy(src_ref=s, dst_ref=d, send_sem=send_sems.at[k],
                                            recv_sem=recv_sems.at[k], device_id=peer, device_id_type=MESH)

    sends = [copy(k, s, d, peer) for k, (s, d, peer, _) in enumerate(remote)]
    recvs = [copy(k, s, landing, peer) for k, (s, _, peer, landing) in enumerate(remote)]
    local_copies = [pltpu.make_async_copy(s, d, local_sems.at[i]) for i, (s, d) in enumerate(local)]
    return sends, recvs, local_copies


def _comm_start(copies):
    sends, _, local_copies = copies
    for cp in local_copies + sends:
        cp.start()


def _comm_wait(copies):
    sends, recvs, local_copies = copies
    for cp in recvs:
        cp.wait_recv()
    for cp in sends:
        cp.wait_send()
    for cp in local_copies:
        cp.wait()


def _split_comm_refs(refs, n_in, n_out, n_scratch, comm):
    ns, nd = (len(comm.srcs), len(comm.out_shapes)) if comm else (0, 0)
    cuts = [n_in, ns, n_out, nd, n_scratch]
    parts, at = [], 0
    for n in cuts:
        parts.append(refs[at:at + n])
        at += n
    return (*parts, refs[at:])


AUG_F = HEAD_DIM
AUG_LSE = HEAD_DIM + 6


def _half_cols(x, lo):
    return (jnp.sum(jnp.where(lo, x, 0.0), axis=-1, keepdims=True),
            jnp.sum(jnp.where(lo, 0.0, x), axis=-1, keepdims=True))


def _half_sums(x, lo):
    s_lo, s_hi = _half_cols(x, lo)
    return jnp.where(lo, s_lo, s_hi)


def _split3(x):
    a = x.astype(BF16).astype(F32)
    r = x - a
    b = r.astype(BF16).astype(F32)
    return a, b, (r - b).astype(BF16).astype(F32)


def _aug(lane, base, terms):
    out = jnp.zeros(lane.shape, F32)
    for i, t in enumerate(terms):
        out = jnp.where(lane == base + i, t, out)
    return out


def _head_lanes(x2, h):
    return x2 if h == 0 else pltpu.roll(x2, HEAD_DIM, 1)


def _fox_prep_fwd(proj, qg, kg, fcol, *, d_model, name):
    T = proj.shape[0]
    nhp = d_model // LANE
    tr = _row_tile(T)

    def body(q_ref, k_ref, v_ref, qg_ref, kg_ref, f_ref, qa_ref, qta_ref, ka_ref, kta_ref, va_ref, vta_ref):
        lane = lax.broadcasted_iota(jnp.int32, (tr, LANE), 1)
        lo = lane < HEAD_DIM

        def norm(xv, g):
            ms = _half_sums(xv * xv, lo) * (1.0 / HEAD_DIM)
            return (xv * lax.rsqrt(ms + EPS)) * g

        qn = norm(q_ref[...], qg_ref[...]) * (HEAD_DIM ** -0.5)
        kn = norm(k_ref[...], kg_ref[...])
        vv = v_ref[...]
        qa, ka, va, vta = [], [], [], []
        for h in range(2):
            f1, f2, f3 = _split3(f_ref[h])
            qa.append(jnp.where(lo, _head_lanes(qn, h), _aug(lane, AUG_F, [f1, f2, f3, 1.0, 1.0, 1.0])))
            ka.append(jnp.where(lo, _head_lanes(kn, h),
                                _aug(lane, AUG_F, [1.0, 1.0, 1.0, -f1, -f2, -f3, 1.0, 1.0, 1.0])))
            va.append(jnp.where(lo if h == 0 else jnp.logical_not(lo), vv, 0.0))
            vta.append(jnp.where(lo, _head_lanes(vv, h), _aug(lane, AUG_F, [1.0, 1.0, 1.0])))
        for parts, ref, tref in ((qa, qa_ref, qta_ref), (ka, ka_ref, kta_ref), (va, va_ref, None),
                                 (vta, None, vta_ref)):
            both = jnp.concatenate(parts, axis=1)
            if ref is not None:
                ref[...] = both.astype(BF16)
            if tref is not None:
                tref[...] = both.T.astype(BF16)

    gain = pl.BlockSpec((1, LANE), lambda i, h: (0, 0))
    rows = pl.BlockSpec((tr, 2 * LANE), lambda i, h: (i, h))
    cols = pl.BlockSpec((2 * LANE, tr), lambda i, h: (h, i))
    wide, tall = jax.ShapeDtypeStruct((T, 2 * d_model), BF16), jax.ShapeDtypeStruct((2 * d_model, T), BF16)
    return pl.pallas_call(
        body, name=name, grid=(T // tr, nhp),
        in_specs=[pl.BlockSpec((tr, LANE), lambda i, h: (i, h)),
                  pl.BlockSpec((tr, LANE), lambda i, h: (i, nhp + h)),
                  pl.BlockSpec((tr, LANE), lambda i, h: (i, 2 * nhp + h)), gain, gain,
                  pl.BlockSpec((2, tr, 1), lambda i, h: (h, i, 0))],
        out_specs=[rows, cols, rows, cols, rows, cols],
        out_shape=[wide, tall, wide, tall, wide, tall],
        compiler_params=_cparams("parallel", "parallel"),
    )(proj, proj, proj, qg, kg, fcol)


def _fox_do_prep(do, o, *, name):
    T, D = do.shape
    nhp = D // LANE
    tr = _row_tile(T)

    def body(do_ref, o_ref, doa_ref, dota_ref):
        lane = lax.broadcasted_iota(jnp.int32, (tr, LANE), 1)
        lo = lane < HEAD_DIM
        dob = do_ref[...].astype(BF16).astype(F32)
        deltas = _half_cols(dob * o_ref[...], lo)
        both = jnp.concatenate(
            [jnp.where(lo, _head_lanes(dob, h), _aug(lane, AUG_F, _split3(-deltas[h]))) for h in range(2)], axis=1)
        doa_ref[...] = both.astype(BF16)
        dota_ref[...] = both.T.astype(BF16)

    blk = pl.BlockSpec((tr, LANE), lambda i, h: (i, h))
    return pl.pallas_call(
        body, name=name, grid=(T // tr, nhp), in_specs=[blk, blk],
        out_specs=[pl.BlockSpec((tr, 2 * LANE), lambda i, h: (i, h)),
                   pl.BlockSpec((2 * LANE, tr), lambda i, h: (h, i))],
        out_shape=[jax.ShapeDtypeStruct((T, 2 * D), BF16), jax.ShapeDtypeStruct((2 * D, T), BF16)],
        compiler_params=_cparams("parallel", "parallel"),
    )(do, o)


def _fox_prep_bwd(proj, dq, dkt, dvt, qg, kg, *, d_model, name):
    T = proj.shape[0]
    nhp = d_model // LANE
    tr = _row_tile(T)

    def body(q_ref, k_ref, dq_ref, dkt_ref, dvt_ref, qg_ref, kg_ref, dqo_ref, dko_ref, dvo_ref, sums_ref):
        first = (pl.program_id(0) == 0) & (pl.program_id(1) == 0)
        lo = lax.broadcasted_iota(jnp.int32, (tr, LANE), 1) < HEAD_DIM

        def pair(x2):
            return jnp.where(lo, x2[:, :LANE], pltpu.roll(x2[:, LANE:], HEAD_DIM, 1))

        def bwd(xv, dxhat, g):
            ms = _half_sums(xv * xv, lo) * (1.0 / HEAD_DIM)
            r = lax.rsqrt(ms + EPS)
            n = xv * r
            dn = dxhat * g
            dx = r * (dn - n * (_half_sums(dn * n, lo) * (1.0 / HEAD_DIM)))
            dg = jnp.sum(dxhat * n, axis=0, keepdims=True)
            return dx, dg + pltpu.roll(dg, HEAD_DIM, 1)

        dxq, dgq = bwd(q_ref[...], pair(dq_ref[...]) * (HEAD_DIM ** -0.5), qg_ref[...])
        dxk, dgk = bwd(k_ref[...], pair(dkt_ref[...].T), kg_ref[...])
        dqo_ref[...] = dxq.astype(BF16)
        dko_ref[...] = dxk.astype(BF16)
        dvo_ref[...] = pair(dvt_ref[...].T).astype(BF16)
        part = jnp.concatenate([dgq, dgk, jnp.zeros((6, LANE), F32)], axis=0)

        @pl.when(first)
        def _():
            sums_ref[...] = part

        @pl.when(jnp.logical_not(first))
        def _():
            sums_ref[...] += part

    gain = pl.BlockSpec((1, LANE), lambda i, h: (0, 0))
    blk = pl.BlockSpec((tr, LANE), lambda i, h: (i, h))
    tall = pl.BlockSpec((2 * LANE, tr), lambda i, h: (h, i))
    return pl.pallas_call(
        body, name=name, grid=(T // tr, nhp),
        in_specs=[blk, pl.BlockSpec((tr, LANE), lambda i, h: (i, nhp + h)),
                  pl.BlockSpec((tr, 2 * LANE), lambda i, h: (i, h)), tall, tall, gain, gain],
        out_specs=[blk, blk, blk, pl.BlockSpec((8, LANE), lambda i, h: (0, 0))],
        out_shape=[jax.ShapeDtypeStruct((T, d_model), BF16)] * 3 + [jax.ShapeDtypeStruct((8, LANE), F32)],
        compiler_params=_cparams("arbitrary", "arbitrary"),
    )(proj, proj, dq, dkt, dvt, qg, kg)


def _scan_lanes(x, reverse):
    n = x.shape[-1]
    lane = lax.broadcasted_iota(jnp.int32, x.shape, 1)
    sh = 1
    while sh < n:
        if reverse:
            x = x + jnp.where(lane < n - sh, pltpu.roll(x, n - sh, 1), 0.0)
        else:
            x = x + jnp.where(lane >= sh, pltpu.roll(x, sh, 1), 0.0)
        sh *= 2
    return x


def _fox_gate_fwd(fpre_t, bf, *, name):
    def body(f_ref, b_ref, o_ref):
        xv = f_ref[...] + b_ref[...]
        logf = jnp.minimum(xv, 0.0) - jnp.log1p(jnp.exp(-jnp.abs(xv)))
        o_ref[...] = _scan_lanes(logf, reverse=False)

    return pl.pallas_call(body, name=name, out_shape=jax.ShapeDtypeStruct(fpre_t.shape, F32))(fpre_t, bf)


def _fox_gate_bwd(dcol, drow, fpre_t, bf, *, name):
    H = fpre_t.shape[0]

    def body(dc_ref, dr_ref, f_ref, b_ref, o_ref, db_ref):
        xv = f_ref[...] + b_ref[...]
        e = dc_ref[...] - dr_ref[...]
        dlogf = _scan_lanes(e, reverse=False) - e
        dpre = dlogf * (1.0 - jax.nn.sigmoid(xv))
        o_ref[...] = dpre
        db_ref[...] = jnp.broadcast_to(jnp.sum(dpre, axis=-1, keepdims=True), (H, LANE))

    return pl.pallas_call(
        body, name=name,
        out_shape=[jax.ShapeDtypeStruct(fpre_t.shape, F32), jax.ShapeDtypeStruct((H, LANE), F32)],
    )(dcol, drow, fpre_t, bf)


_NT = (((1,), (1,)), ((), ()))
_TN = (((0,), (0,)), ((), ()))
_NN = (((1,), (0,)), ((), ()))


def _attn_tile(T):
    return min(T, 512)


def _causal(tq, tk):
    return lax.broadcasted_iota(jnp.int32, (tq, tk), 1) <= lax.broadcasted_iota(jnp.int32, (tq, tk), 0)


def _fox_attn_fwd(qa, kta, va, *, name, comm=None):
    T = qa.shape[0]
    nhp = qa.shape[1] // (2 * LANE)
    tq = tk = _attn_tile(T)
    nq = T // tq

    def body(*refs):
        (qa_ref, kta_ref, va_ref), src_refs, (o_ref, qb_ref), dst_refs, (m_sc, l_sc, acc_sc), sems = (
            _split_comm_refs(refs, 3, 2, 3, comm))
        hp, i, j = pl.program_id(0), pl.program_id(1), pl.program_id(2)
        if comm:
            @pl.when((hp == 0) & (i == 0) & (j == 0))
            def _():
                _comm_start(_comm_copies(comm.plan, src_refs, dst_refs, *sems))

        @pl.when(j == 0)
        def _():
            m_sc[...] = jnp.full(m_sc.shape, NEG, F32)
            l_sc[...] = jnp.zeros(l_sc.shape, F32)
            acc_sc[...] = jnp.zeros(acc_sc.shape, F32)

        def block(diagonal):
            for h in range(2):
                hs = slice(h * LANE, (h + 1) * LANE)
                s = lax.dot_general(qa_ref[:, hs], kta_ref[hs, :], _NN, preferred_element_type=F32)
                if diagonal:
                    s = jnp.where(_causal(tq, tk), s, NEG)
                m_prev = m_sc[h]
                m_next = jnp.maximum(m_prev, jnp.max(s, axis=1, keepdims=True))
                p = jnp.exp(s - jnp.tile(m_next, (1, tk // LANE)))
                alpha = jnp.exp(m_prev - m_next)
                l_sc[h] = alpha * l_sc[h] + jnp.sum(p, axis=1, keepdims=True)
                m_sc[h] = m_next
                acc_sc[h] = alpha * acc_sc[h] + lax.dot_general(p.astype(BF16), va_ref[:, hs], _NN,
                                                                preferred_element_type=F32)

        @pl.when(j < i)
        def _():
            block(False)

        @pl.when(j == i)
        def _():
            block(True)
            o_ref[...] = acc_sc[0] / l_sc[0] + acc_sc[1] / l_sc[1]
            lane = lax.broadcasted_iota(jnp.int32, (tq, LANE), 1)
            for h in range(2):
                hs = slice(h * LANE, (h + 1) * LANE)
                pieces = _split3(-(m_sc[h] + jnp.log(l_sc[h])))
                qb = qa_ref[:, hs].astype(F32)
                for n, piece in enumerate(pieces):
                    qb = jnp.where(lane == AUG_LSE + n, piece, qb)
                qb_ref[:, hs] = qb.astype(BF16)

        if comm:
            @pl.when((hp == nhp - 1) & (i == nq - 1) & (j == nq - 1))
            def _():
                _comm_wait(_comm_copies(comm.plan, src_refs, dst_refs, *sems))

    outs = pl.pallas_call(
        body, name=name, grid=(nhp, nq, nq),
        in_specs=[pl.BlockSpec((tq, 2 * LANE), lambda h, i, j: (i, h)),
                  pl.BlockSpec((2 * LANE, tk), lambda h, i, j: (h, jnp.minimum(j, i))),
                  pl.BlockSpec((tk, 2 * LANE), lambda h, i, j: (jnp.minimum(j, i), h))]
        + ([ANY] * len(comm.srcs) if comm else []),
        out_specs=[pl.BlockSpec((tq, LANE), lambda h, i, j: (i, h)),
                   pl.BlockSpec((tq, 2 * LANE), lambda h, i, j: (i, h))]
        + ([ANY] * len(comm.out_shapes) if comm else []),
        out_shape=[jax.ShapeDtypeStruct((T, nhp * LANE), F32), jax.ShapeDtypeStruct(qa.shape, BF16)]
        + (list(comm.out_shapes) if comm else []),
        scratch_shapes=[pltpu.VMEM((2, tq, LANE), F32), pltpu.VMEM((2, tq, LANE), F32),
                        pltpu.VMEM((2, tq, LANE), F32)] + (comm.scratch() if comm else []),
        compiler_params=(_cparams("arbitrary", "arbitrary", "arbitrary") if comm
                         else _cparams("parallel", "parallel", "arbitrary")),
    )(qa, kta, va, *(comm.srcs if comm else []))
    return outs[0], outs[1], outs[2:]


def _fox_attn_bwd(qb, qta, ka, kta, vta, doa, dota, *, name, comm=None):
    T = qb.shape[0]
    nhp = qb.shape[1] // (2 * LANE)
    tq = tk = _attn_tile(T)
    nq = T // tq

    def body(*refs):
        ((qb_ref, qta_ref, ka_ref, kta_ref, vta_ref, doa_ref, dota_ref), src_refs,
         (dq_ref, dkt_ref, dvt_ref, dcol_ref, drow_ref), dst_refs, (dkt_sc, dvt_sc, dcol_sc), sems) = (
            _split_comm_refs(refs, 7, 5, 3, comm))
        hp, j, i = pl.program_id(0), pl.program_id(1), pl.program_id(2)
        if comm:
            @pl.when((hp == 0) & (j == 0) & (i == 0))
            def _():
                _comm_start(_comm_copies(comm.plan, src_refs, dst_refs, *sems))

        @pl.when((j == 0) & (i == 0))
        def _():
            dq_ref[...] = jnp.zeros(dq_ref.shape, F32)
            drow_ref[...] = jnp.zeros(drow_ref.shape, F32)

        @pl.when(i == 0)
        def _():
            dkt_sc[...] = jnp.zeros(dkt_sc.shape, F32)
            dvt_sc[...] = jnp.zeros(dvt_sc.shape, F32)
            dcol_sc[...] = jnp.zeros(dcol_sc.shape, F32)

        def block(diagonal):
            rows = pl.ds(pl.multiple_of(i * tq, tq), tq)
            for h in range(2):
                hs = slice(h * LANE, (h + 1) * LANE)
                p = jnp.exp(lax.dot_general(qb_ref[:, hs], kta_ref[hs, :], _NN, preferred_element_type=F32))
                if diagonal:
                    p = jnp.where(_causal(tq, tk), p, 0.0)
                dl = p * lax.dot_general(doa_ref[:, hs], vta_ref[hs, :], _NN, preferred_element_type=F32)
                dlb = dl.astype(BF16)
                dvt_sc[h] += lax.dot_general(dota_ref[hs, :], p.astype(BF16), _NN, preferred_element_type=F32)
                dkt_sc[h] += lax.dot_general(qta_ref[hs, :], dlb, _NN, preferred_element_type=F32)
                dq_ref[rows, hs] += lax.dot_general(dlb, ka_ref[:, hs], _NN, preferred_element_type=F32)
                dcol_sc[h] += jnp.sum(dl, axis=0, keepdims=True)
                drow_ref[h, rows, :] += jnp.sum(dl, axis=1, keepdims=True)

        @pl.when(i > j)
        def _():
            block(False)

        @pl.when(i == j)
        def _():
            block(True)

        @pl.when(i == nq - 1)
        def _():
            dkt_ref[...] = jnp.concatenate([dkt_sc[0], dkt_sc[1]], axis=0)
            dvt_ref[...] = jnp.concatenate([dvt_sc[0], dvt_sc[1]], axis=0)
            dcol_ref[...] = dcol_sc[...]

        if comm:
            @pl.when((hp == nhp - 1) & (j == nq - 1) & (i == nq - 1))
            def _():
                _comm_wait(_comm_copies(comm.plan, src_refs, dst_refs, *sems))

    qrow = pl.BlockSpec((tq, 2 * LANE), lambda h, j, i: (jnp.maximum(i, j), h))
    qcol = pl.BlockSpec((2 * LANE, tq), lambda h, j, i: (h, jnp.maximum(i, j)))
    krow = pl.BlockSpec((tk, 2 * LANE), lambda h, j, i: (j, h))
    kcol = pl.BlockSpec((2 * LANE, tk), lambda h, j, i: (h, j))
    tall = jax.ShapeDtypeStruct((qb.shape[1], T), F32)
    outs = pl.pallas_call(
        body, name=name, grid=(nhp, nq, nq),
        in_specs=[qrow, qcol, krow, kcol, kcol, qrow, qcol] + ([ANY] * len(comm.srcs) if comm else []),
        out_specs=[pl.BlockSpec((T, 2 * LANE), lambda h, j, i: (0, h)), kcol, kcol,
                   pl.BlockSpec((2, 1, tk), lambda h, j, i: (h, 0, j)),
                   pl.BlockSpec((2, T, 1), lambda h, j, i: (h, 0, 0))]
        + ([ANY] * len(comm.out_shapes) if comm else []),
        out_shape=[jax.ShapeDtypeStruct(qb.shape, F32), tall, tall,
                   jax.ShapeDtypeStruct((2 * nhp, 1, T), F32), jax.ShapeDtypeStruct((2 * nhp, T, 1), F32)]
        + (list(comm.out_shapes) if comm else []),
        scratch_shapes=[pltpu.VMEM((2, LANE, tk), F32), pltpu.VMEM((2, LANE, tk), F32),
                        pltpu.VMEM((2, 1, tk), F32)] + (comm.scratch() if comm else []),
        compiler_params=_cparams("arbitrary" if comm else "parallel", "arbitrary", "arbitrary"),
    )(qb, qta, ka, kta, vta, doa, dota, *(comm.srcs if comm else []))
    return (*outs[:5], outs[5:])


_GELU_C = math.sqrt(2.0 / math.pi)
_GELU_A = 0.044715


def _gelu(x):
    t = jnp.tanh(_GELU_C * (x + _GELU_A * (x * x * x)))
    return x * (0.5 * (1.0 + t)), t


def _gelu_grad(x, t):
    return 0.5 * (1.0 + t) + 0.5 * x * (1.0 - t * t) * (_GELU_C * (1.0 + 3.0 * _GELU_A * x * x))


def _layer_norm_stats(v):
    mu = jnp.mean(v, axis=-1, keepdims=True)
    vc = v - mu
    rstd = lax.rsqrt(jnp.mean(vc * vc, axis=-1, keepdims=True) + EPS)
    return vc * rstd, rstd


def _layer_norm_bwd(dyhat, yhat, rstd):
    return rstd * (dyhat - jnp.mean(dyhat, axis=-1, keepdims=True)
                   - yhat * jnp.mean(dyhat * yhat, axis=-1, keepdims=True))


def _sg_mask():
    t = lax.broadcasted_iota(jnp.int32, (SG_CHUNK, SG_CHUNK), 0) // SG_CAUSAL
    s = lax.broadcasted_iota(jnp.int32, (SG_CHUNK, SG_CHUNK), 1) // SG_CAUSAL
    return s <= t


def _sg_mix(ws_ref, bc_ref, vln_sc, vo_sc, tr, gd):
    mask = _sg_mask()
    for g in range(SG_GROUPS):
        wg = jnp.where(mask, ws_ref[g], 0.0).astype(BF16)
        cols = slice(g * gd, (g + 1) * gd)
        for n in range(tr // SG_CHUNK):
            rows = slice(n * SG_CHUNK, (n + 1) * SG_CHUNK)
            vo_sc[rows, cols] = lax.dot_general(wg, vln_sc[rows, cols], _NN,
                                                preferred_element_type=F32) + bc_ref[g]


def _sg_fwd(a_uv, ln_g, ln_b, ws, bcol, *, name):
    T, W = a_uv.shape[0], a_uv.shape[1] // 2
    gd = W // SG_GROUPS
    tr = _row_tile(T)

    def body(u_ref, v_ref, g_ref, b_ref, ws_ref, bc_ref, o_ref, vln_sc, vo_sc):
        u, _ = _gelu(u_ref[...])
        v, _ = _gelu(v_ref[...])
        vhat, _ = _layer_norm_stats(v)
        vln_sc[...] = (vhat * g_ref[...] + b_ref[...]).astype(BF16)
        _sg_mix(ws_ref, bc_ref, vln_sc, vo_sc, tr, gd)
        o_ref[...] = (u * vo_sc[...]).astype(BF16)

    row = pl.BlockSpec((1, W), lambda i: (0, 0))
    return pl.pallas_call(
        body, name=name, grid=(T // tr,),
        in_specs=[pl.BlockSpec((tr, W), lambda i: (i, 0)), pl.BlockSpec((tr, W), lambda i: (i, 1)), row, row,
                  pl.BlockSpec((SG_GROUPS, SG_CHUNK, SG_CHUNK), lambda i: (0, 0, 0)),
                  pl.BlockSpec((SG_GROUPS, SG_CHUNK, 1), lambda i: (0, 0, 0))],
        out_specs=pl.BlockSpec((tr, W), lambda i: (i, 0)),
        out_shape=jax.ShapeDtypeStruct((T, W), BF16),
        scratch_shapes=[pltpu.VMEM((tr, W), BF16), pltpu.VMEM((tr, W), F32)],
        compiler_params=_cparams("parallel"),
    )(a_uv, a_uv, ln_g, ln_b, ws, bcol)


def _sg_bwd(a_uv, dgate, ln_g, ln_b, ws, bcol, *, name):
    T, W = a_uv.shape[0], a_uv.shape[1] // 2
    gd = W // SG_GROUPS
    tr = _row_tile(T)

    def body(u_ref, v_ref, dg_ref, g_ref, b_ref, ws_ref, bc_ref,
             da_ref, dws_ref, dbs_ref, sums_ref, vln_sc, vo_sc, dvo_sc, dvln_sc):
        i = pl.program_id(0)

        @pl.when(i == 0)
        def _():
            dws_ref[...] = jnp.zeros(dws_ref.shape, F32)
            dbs_ref[...] = jnp.zeros(dbs_ref.shape, F32)
            sums_ref[...] = jnp.zeros(sums_ref.shape, F32)

        ua, va = u_ref[...], v_ref[...]
        u, tu = _gelu(ua)
        v, tv = _gelu(va)
        vhat, rstd = _layer_norm_stats(v)
        vln_sc[...] = (vhat * g_ref[...] + b_ref[...]).astype(BF16)
        _sg_mix(ws_ref, bc_ref, vln_sc, vo_sc, tr, gd)
        dgt = dg_ref[...]
        du = dgt * vo_sc[...]
        dvo_sc[...] = dgt * u
        mask = _sg_mask()
        for g in range(SG_GROUPS):
            wg = jnp.where(mask, ws_ref[g], 0.0).astype(BF16)
            cols = slice(g * gd, (g + 1) * gd)
            acc_w = jnp.zeros((SG_CHUNK, SG_CHUNK), F32)
            acc_b = jnp.zeros((SG_CHUNK, 1), F32)
            for n in range(tr // SG_CHUNK):
                rows = slice(n * SG_CHUNK, (n + 1) * SG_CHUNK)
                dvo = dvo_sc[rows, cols]
                dvob = dvo.astype(BF16)
                dvln_sc[rows, cols] = lax.dot_general(wg, dvob, _TN, preferred_element_type=F32)
                acc_w += lax.dot_general(dvob, vln_sc[rows, cols], _NT, preferred_element_type=F32)
                acc_b += jnp.sum(dvo, axis=1, keepdims=True)
            dws_ref[g] += jnp.where(mask, acc_w, 0.0)
            dbs_ref[g] += acc_b
        dvln = dvln_sc[...]
        sums_ref[...] += jnp.concatenate([jnp.sum(dvln * vhat, axis=0, keepdims=True),
                                          jnp.sum(dvln, axis=0, keepdims=True),
                                          jnp.zeros((6, W), F32)], axis=0)
        dv = _layer_norm_bwd(dvln * g_ref[...], vhat, rstd)
        da_ref[:, :W] = (du * _gelu_grad(ua, tu)).astype(BF16)
        da_ref[:, W:] = (dv * _gelu_grad(va, tv)).astype(BF16)

    row = pl.BlockSpec((1, W), lambda i: (0, 0))
    wspec = pl.BlockSpec((SG_GROUPS, SG_CHUNK, SG_CHUNK), lambda i: (0, 0, 0))
    bspec = pl.BlockSpec((SG_GROUPS, SG_CHUNK, 1), lambda i: (0, 0, 0))
    return pl.pallas_call(
        body, name=name, grid=(T // tr,),
        in_specs=[pl.BlockSpec((tr, W), lambda i: (i, 0)), pl.BlockSpec((tr, W), lambda i: (i, 1)),
                  pl.BlockSpec((tr, W), lambda i: (i, 0)), row, row, wspec, bspec],
        out_specs=[pl.BlockSpec((tr, 2 * W), lambda i: (i, 0)), wspec, bspec,
                   pl.BlockSpec((8, W), lambda i: (0, 0))],
        out_shape=[jax.ShapeDtypeStruct((T, 2 * W), BF16),
                   jax.ShapeDtypeStruct((SG_GROUPS, SG_CHUNK, SG_CHUNK), F32),
                   jax.ShapeDtypeStruct((SG_GROUPS, SG_CHUNK, 1), F32),
                   jax.ShapeDtypeStruct((8, W), F32)],
        scratch_shapes=[pltpu.VMEM((tr, W), BF16), pltpu.VMEM((tr, W), F32),
                        pltpu.VMEM((tr, W), F32), pltpu.VMEM((tr, W), F32)],
        compiler_params=_cparams("arbitrary"),
    )(a_uv, a_uv, dgate, ln_g, ln_b, ws, bcol)


def _cv_glu_conv(a_ref, b_ref, ap_ref, bp_ref, w_ref, bd_ref, xc_sc, tr):
    i = pl.program_id(0)
    prev = ap_ref[...] * jax.nn.sigmoid(bp_ref[...])
    xc_sc[0:CONV_HALO, :] = jnp.where(i > 0, prev, 0.0)
    xc_sc[CONV_HALO:, :] = a_ref[...] * jax.nn.sigmoid(b_ref[...])
    acc = jnp.broadcast_to(bd_ref[...], (tr, bd_ref.shape[1]))
    for k in range(CONV_WIDTH):
        acc = acc + w_ref[k:k + 1, :] * xc_sc[pl.ds(CONV_HALO - (CONV_WIDTH - 1) + k, tr), :]
    return acc


def _cv_specs(T, C, tr):
    hb = tr // CONV_HALO
    cur = lambda col: pl.BlockSpec((tr, C), lambda i: (i, col))
    prev = lambda col: pl.BlockSpec((CONV_HALO, C), lambda i: (jnp.maximum(i * hb - 1, 0), col))
    row = pl.BlockSpec((1, C), lambda i: (0, 0))
    wspec = pl.BlockSpec((CONV_HALO, C), lambda i: (0, 0))
    return cur, prev, row, wspec


def _cv_fwd(p, w_dw, b_dw, ln_g, ln_b, *, name):
    T, C = p.shape[0], p.shape[1] // 2
    tr = _row_tile(T)
    cur, prev, row, wspec = _cv_specs(T, C, tr)

    def body(a_ref, b_ref, ap_ref, bp_ref, w_ref, bd_ref, g_ref, be_ref, o_ref, xc_sc):
        y2 = _cv_glu_conv(a_ref, b_ref, ap_ref, bp_ref, w_ref, bd_ref, xc_sc, tr)
        yhat, _ = _layer_norm_stats(y2)
        yln = yhat * g_ref[...] + be_ref[...]
        o_ref[...] = (yln * jax.nn.sigmoid(yln)).astype(BF16)

    return pl.pallas_call(
        body, name=name, grid=(T // tr,),
        in_specs=[cur(0), cur(1), prev(0), prev(1), wspec, row, row, row],
        out_specs=pl.BlockSpec((tr, C), lambda i: (i, 0)),
        out_shape=jax.ShapeDtypeStruct((T, C), BF16),
        scratch_shapes=[pltpu.VMEM((tr + CONV_HALO, C), F32)],
        compiler_params=_cparams("parallel"),
    )(p, p, p, p, w_dw, b_dw, ln_g, ln_b)


def _cv_bwd_ln(p, dy3, w_dw, b_dw, ln_g, ln_b, *, name):
    T, C = p.shape[0], p.shape[1] // 2
    tr = _row_tile(T)
    cur, prev, row, wspec = _cv_specs(T, C, tr)

    def body(a_ref, b_ref, ap_ref, bp_ref, dy_ref, w_ref, bd_ref, g_ref, be_ref,
             dy2_ref, dw_ref, sums_ref, xc_sc):
        i = pl.program_id(0)
        y2 = _cv_glu_conv(a_ref, b_ref, ap_ref, bp_ref, w_ref, bd_ref, xc_sc, tr)
        yhat, rstd = _layer_norm_stats(y2)
        yln = yhat * g_ref[...] + be_ref[...]
        s = jax.nn.sigmoid(yln)
        dyln = dy_ref[...] * (s + yln * s * (1.0 - s))
        dy2 = _layer_norm_bwd(dyln * g_ref[...], yhat, rstd)
        dy2_ref[...] = dy2
        sums = jnp.concatenate([jnp.sum(dy2, axis=0, keepdims=True),
                                jnp.sum(dyln * yhat, axis=0, keepdims=True),
                                jnp.sum(dyln, axis=0, keepdims=True),
                                jnp.zeros((5, C), F32)], axis=0)
        taps = [jnp.sum(dy2 * xc_sc[pl.ds(CONV_HALO - (CONV_WIDTH - 1) + k, tr), :], axis=0, keepdims=True)
                for k in range(CONV_WIDTH)]
        dw = jnp.concatenate(taps + [jnp.zeros((CONV_HALO - CONV_WIDTH, C), F32)], axis=0)

        @pl.when(i == 0)
        def _():
            sums_ref[...] = sums
            dw_ref[...] = dw

        @pl.when(i > 0)
        def _():
            sums_ref[...] += sums
            dw_ref[...] += dw

    blk = pl.BlockSpec((tr, C), lambda i: (i, 0))
    return pl.pallas_call(
        body, name=name, grid=(T // tr,),
        in_specs=[cur(0), cur(1), prev(0), prev(1), blk, wspec, row, row, row],
        out_specs=[blk, wspec, pl.BlockSpec((8, C), lambda i: (0, 0))],
        out_shape=[jax.ShapeDtypeStruct((T, C), F32), jax.ShapeDtypeStruct((CONV_HALO, C), F32),
                   jax.ShapeDtypeStruct((8, C), F32)],
        scratch_shapes=[pltpu.VMEM((tr + CONV_HALO, C), F32)],
        compiler_params=_cparams("arbitrary"),
    )(p, p, p, p, dy3, w_dw, b_dw, ln_g, ln_b)


def _cv_bwd_in(p, dy2, w_dw, *, name):
    T, C = p.shape[0], p.shape[1] // 2
    tr = _row_tile(T)
    hb = tr // CONV_HALO
    nblk = T // tr
    last_halo = T // CONV_HALO - 1

    def body(a_ref, b_ref, dy_ref, dyn_ref, w_ref, dp_ref, sums_ref, xc_sc):
        i = pl.program_id(0)
        xc_sc[0:tr, :] = dy_ref[...]
        xc_sc[tr:, :] = jnp.where(i < nblk - 1, dyn_ref[...], 0.0)
        dy1 = jnp.zeros((tr, C), F32)
        for k in range(CONV_WIDTH):
            dy1 = dy1 + w_ref[k:k + 1, :] * xc_sc[pl.ds(CONV_WIDTH - 1 - k, tr), :]
        a = a_ref[...]
        sb = jax.nn.sigmoid(b_ref[...])
        da = dy1 * sb
        db = dy1 * a * sb * (1.0 - sb)
        dp_ref[:, :C] = da.astype(BF16)
        dp_ref[:, C:] = db.astype(BF16)
        sums = jnp.concatenate([
            jnp.concatenate([jnp.sum(da, axis=0, keepdims=True), jnp.sum(db, axis=0, keepdims=True)], axis=1),
            jnp.zeros((7, 2 * C), F32)], axis=0)

        @pl.when(i == 0)
        def _():
            sums_ref[...] = sums

        @pl.when(i > 0)
        def _():
            sums_ref[...] += sums

    blk = lambda col: pl.BlockSpec((tr, C), lambda i: (i, col))
    return pl.pallas_call(
        body, name=name, grid=(nblk,),
        in_specs=[blk(0), blk(1), blk(0),
                  pl.BlockSpec((CONV_HALO, C), lambda i: (jnp.minimum((i + 1) * hb, last_halo), 0)),
                  pl.BlockSpec((CONV_HALO, C), lambda i: (0, 0))],
        out_specs=[pl.BlockSpec((tr, 2 * C), lambda i: (i, 0)), pl.BlockSpec((8, 2 * C), lambda i: (0, 0))],
        out_shape=[jax.ShapeDtypeStruct((T, 2 * C), BF16), jax.ShapeDtypeStruct((8, 2 * C), F32)],
        scratch_shapes=[pltpu.VMEM((tr + CONV_HALO, C), F32)],
        compiler_params=_cparams("arbitrary"),
    )(p, p, dy2, dy2, w_dw)


def _col_tile(n, want=1024):
    best = LANE
    for t in range(LANE, min(n, want) + 1, LANE):
        if n % t == 0:
            best = t
    return best if n % LANE == 0 else n


def _mm(a, b, *, name, ta=False, tb=False, **kw):
    M = a.shape[1] if ta else a.shape[0]
    N = b.shape[0] if tb else b.shape[1]
    K = a.shape[0] if ta else a.shape[1]
    kw.setdefault('tm', _col_tile(M, 1024 if ta else 512))
    kw.setdefault('tn', _col_tile(N, 1024))
    kw.setdefault('tk', _col_tile(K, 512 if ta else 1024))
    return _matmul(a, b, name=name, ta=ta, tb=tb, **kw)


def _relu2_epilogue(acc):
    r = jnp.maximum(acc, 0.0)
    return acc, r * r


def _residual_epilogue(acc, x, g):
    return acc, x + g * acc


def _residual_bias_epilogue(acc, x, g, b):
    y = acc + b
    return y, x + g * y


def _relu2_bwd_epilogue(acc, a):
    return (acc * (2.0 * jnp.maximum(a, 0.0)),)


def _bias_epilogue(acc, b):
    return (acc + b,)


def _fox_forward(h1, P, j, D, comm=None):
    H = D // HEAD_DIM
    proj = _mm(h1, P['fox_w_in'][j], name='fox_proj', b_outer=True)
    qg = jnp.tile(P['fox_q_norm'][j][None, :], (1, 2))
    kg = jnp.tile(P['fox_k_norm'][j][None, :], (1, 2))
    fpre_t = proj[:, 3 * D:3 * D + H].T
    bf = P['fox_b_f'][j][:, None]
    fcum = _fox_gate_fwd(fpre_t, bf, name='fox_gate_fwd')
    qa, qta, ka, kta, va, vta = _fox_prep_fwd(proj, qg, kg, fcum[:, :, None], d_model=D, name='fox_prep_fwd')
    o, qb, comm_outs = _fox_attn_fwd(qa, kta, va, name='fox_attn_fwd', comm=comm)
    saved = dict(proj=proj, qg=qg, kg=kg, fpre_t=fpre_t, bf=bf, o=o, qb=qb, qta=qta, ka=ka, kta=kta, vta=vta)
    return o, saved, comm_outs


def _fox_backward(dy, h1, S, P, j, D, comm=None):
    H = D // HEAD_DIM
    w_out, w_in = P['fox_w_out'][j], P['fox_w_in'][j]
    g = {}
    g['fox_w_out'] = _mm(S['o'], dy, ta=True, name='fox_dw_out')
    do = _mm(dy, w_out, tb=True, name='fox_do')
    doa, dota = _fox_do_prep(do, S['o'], name='fox_do_prep')
    dq, dkt, dvt, dcol, drow, comm_outs = _fox_attn_bwd(S['qb'], S['qta'], S['ka'], S['kta'], S['vta'], doa, dota,
                                                        name='fox_attn_bwd', comm=comm)
    dqp, dkp, dvp, gsum = _fox_prep_bwd(S['proj'], dq, dkt, dvt, S['qg'], S['kg'], d_model=D, name='fox_prep_bwd')
    dfpre_t, dbf = _fox_gate_bwd(dcol[:, 0, :], drow[:, :, 0], S['fpre_t'], S['bf'], name='fox_gate_bwd')
    dfpre = jnp.pad(dfpre_t.T.astype(BF16), ((0, 0), (0, LANE - H)))
    dproj = jnp.concatenate([dqp, dkp, dvp, dfpre], axis=1)
    g['fox_w_in'] = _mm(h1, dproj, ta=True, name='fox_dw_in')[:, :3 * D + H]
    g['fox_b_f'] = dbf[:, 0]
    g['fox_q_norm'] = gsum[0, :HEAD_DIM]
    g['fox_k_norm'] = gsum[1, :HEAD_DIM]
    dh1 = _mm(dproj, w_in, tb=True, name='fox_dh')
    return dh1, g, comm_outs


def _sg_forward(h1, P, D):
    a_uv = _mm(h1, P['sg_w_in'], name='sg_in', b_outer=True)
    bcol = P['sg_b_s'][:, :, None]
    gate = _sg_fwd(a_uv, P['sg_ln_g'], P['sg_ln_b'], P['sg_w_s'], bcol, name='sg_fwd')
    return gate, dict(a_uv=a_uv, bcol=bcol, gate=gate)


def _sg_backward(dy, h1, S, P, D):
    g = {}
    g['sg_w_out'] = _mm(S['gate'], dy, ta=True, name='sg_dw_out')
    dgate = _mm(dy, P['sg_w_out'], tb=True, name='sg_dgate')
    da, dws, dbs, sums = _sg_bwd(S['a_uv'], dgate, P['sg_ln_g'], P['sg_ln_b'], P['sg_w_s'], S['bcol'],
                                 name='sg_bwd')
    g['sg_w_s'], g['sg_b_s'] = dws, dbs[:, :, 0]
    g['sg_ln_g'], g['sg_ln_b'] = sums[0], sums[1]
    g['sg_w_in'] = _mm(h1, da, ta=True, name='sg_dw_in')
    dh1 = _mm(da, P['sg_w_in'], tb=True, name='sg_dh')
    return dh1, g


def _cv_forward(h1, P, D):
    p = _mm(h1, P['cv_w_pw1'], name='cv_pw1', extras=[(P['cv_b_pw1'], 'row')], epilogue=_bias_epilogue,
            b_outer=True)
    w_dw = jnp.pad(P['cv_w_dw'], ((0, CONV_HALO - CONV_WIDTH), (0, 0)))
    y3 = _cv_fwd(p, w_dw, P['cv_b_dw'], P['cv_ln_g'], P['cv_ln_b'], name='cv_fwd')
    return y3, dict(p=p, w_dw=w_dw, y3=y3)


def _cv_backward(dy, h1, S, P, D):
    g = {}
    g['cv_w_pw2'] = _mm(S['y3'], dy, ta=True, name='cv_dw_pw2')
    dy3 = _mm(dy, P['cv_w_pw2'], tb=True, name='cv_dy3')
    dy2, dw, sums = _cv_bwd_ln(S['p'], dy3, S['w_dw'], P['cv_b_dw'], P['cv_ln_g'], P['cv_ln_b'], name='cv_bwd_ln')
    g['cv_w_dw'] = dw[:CONV_WIDTH]
    g['cv_b_dw'], g['cv_ln_g'], g['cv_ln_b'] = sums[0], sums[1], sums[2]
    dp, psum = _cv_bwd_in(S['p'], dy2, S['w_dw'], name='cv_bwd_in')
    g['cv_b_pw1'] = psum[0]
    g['cv_w_pw1'] = _mm(h1, dp, ta=True, name='cv_dw_pw1')
    dh1 = _mm(dp, P['cv_w_pw1'], tb=True, name='cv_dh')
    return dh1, g


class Hooks(NamedTuple):
    fwd_comm: Comm
    fwd_done: Callable
    bwd_comm: Callable
    bwd_done: Callable


def _local_step(x, target, mod, P, hooks=None):
    T, D = x.shape
    L = mod.shape[0]
    saved = []
    for i in range(L):
        kind, j = i % N_MIXERS, i // N_MIXERS
        m = [mod[i:i + 1, k * D:(k + 1) * D] for k in range(6)]
        sh_m, sc_m, g_m, sh_f, sc_f, g_f = m
        w_mix, w_mlp = P['norm_mix'][i:i + 1], P['norm_mlp'][i:i + 1]
        h1 = _norm_mod_fwd(x, w_mix, sc_m, sh_m, name='norm_mix_fwd')
        if kind == 0:
            carried = hooks is not None and i == 0
            op, S, comm_outs = _fox_forward(h1, P, j, D, comm=hooks.fwd_comm if carried else None)
            if carried:
                hooks.fwd_done(comm_outs)
            y, x1 = _mm(op, P['fox_w_out'][j], name='fox_out', extras=[(x, 'tile'), (g_m, 'row')],
                        epilogue=_residual_epilogue, out_dtypes=(F32, F32))
        elif kind == 1:
            op, S = _sg_forward(h1, P, D)
            y, x1 = _mm(op, P['sg_w_out'], name='sg_out', extras=[(x, 'tile'), (g_m, 'row')],
                        epilogue=_residual_epilogue, out_dtypes=(F32, F32))
        else:
            op, S = _cv_forward(h1, P, D)
            y, x1 = _mm(op, P['cv_w_pw2'], name='cv_out',
                        extras=[(x, 'tile'), (g_m, 'row'), (P['cv_b_pw2'], 'row')],
                        epilogue=_residual_bias_epilogue, out_dtypes=(F32, F32))
        h2 = _norm_mod_fwd(x1, w_mlp, sc_f, sh_f, name='norm_mlp_fwd')
        a, r = _mm(h2, P['w_mlp_in'][i], name='mlp_in', epilogue=_relu2_epilogue, out_dtypes=(F32, BF16),
                   b_outer=True)
        z, x2 = _mm(r, P['w_mlp_out'][i], name='mlp_out', extras=[(x1, 'tile'), (g_f, 'row')],
                    epilogue=_residual_epilogue, out_dtypes=(F32, F32), tk=P['w_mlp_out'][i].shape[0])
        saved.append(dict(x=x, h1=h1, S=S, y=y, x1=x1, h2=h2, a=a, r=r, z=z, m=m))
        x = x2

    loss_part, dx = _loss_head(x, target, name='loss_head')

    grads = {k: [None] * L for k in ('norm_mix', 'norm_mlp')}
    mix_grads, mat = {}, {}
    dmod = [None] * L
    for i in reversed(range(L)):
        kind, j = i % N_MIXERS, i // N_MIXERS
        sv = saved[i]
        sh_m, sc_m, g_m, sh_f, sc_f, g_f = sv['m']
        w_mix, w_mlp = P['norm_mix'][i:i + 1], P['norm_mlp'][i:i + 1]
        dz, dgf = _gate_bwd(dx, sv['z'], g_f, name='mlp_gate_bwd')
        mat['w_mlp_out', i] = _mm(sv['r'], dz, ta=True, name='mlp_dw_out')
        da = _mm(dz, P['w_mlp_out'][i], tb=True, name='mlp_da', extras=[(sv['a'], 'tile')],
                 epilogue=_relu2_bwd_epilogue, out_dtypes=(BF16,), b_outer=True)
        mat['w_mlp_in', i] = _mm(sv['h2'], da, ta=True, name='mlp_dw_in')
        dh2 = _mm(da, P['w_mlp_in'][i], tb=True, name='mlp_dh', tk=P['w_mlp_in'][i].shape[1])
        dx1, sums_f, dy = _norm_mod_bwd(dh2, sv['x1'], dx, w_mlp, sc_f, name='norm_mlp_bwd', gate=(sv['y'], g_m))
        if kind == 0:
            carried = hooks is not None and i == 0
            dh1, g, comm_outs = _fox_backward(dy, sv['h1'], sv['S'], P, j, D,
                                              comm=hooks.bwd_comm(mat) if carried else None)
            if carried:
                hooks.bwd_done(comm_outs)
        elif kind == 1:
            dh1, g = _sg_backward(dy, sv['h1'], sv['S'], P, D)
        else:
            dh1, g = _cv_backward(dy, sv['h1'], sv['S'], P, D)
            g['cv_b_pw2'] = sums_f[4]
        for k, val in g.items():
            if k in BIG:
                mat[k, j] = val
            else:
                mix_grads.setdefault(k, {})[j] = val
        dx, sums_m = _norm_mod_bwd(dh1, sv['x'], dx1, w_mix, sc_m, name='norm_mix_bwd')
        grads['norm_mlp'][i], grads['norm_mix'][i] = sums_f[2], sums_m[2]
        dmod[i] = jnp.concatenate([sums_m[0], sums_m[1], sums_f[3], sums_f[0], sums_f[1], dgf[0]])

    out = {k: jnp.stack(v) for k, v in grads.items()}
    for k, per_j in mix_grads.items():
        out[k] = jnp.stack([per_j[j] for j in sorted(per_j)])
    return loss_part, dx, jnp.stack(dmod), out, mat


def _all_gather8(blocks, *, name):
    n = len(blocks)

    def body(*refs):
        x_refs, out_refs = refs[:n], refs[n:2 * n]
        send_sems, recv_sems, local_sems = refs[2 * n:]
        x, y, c = _position()
        me, sibling = (x, y, c), (x, y, 1 - c)
        chips = [(1 - x, y), (x, 1 - y), (1 - x, 1 - y)]

        def slot(a, px, py, pc):
            return out_refs[a].at[4 * px + 2 * py + pc]

        def copy(a, k, blk, to, src=None):
            return pltpu.make_async_remote_copy(
                src_ref=slot(a, *blk) if src is None else src, dst_ref=slot(a, *blk),
                send_sem=send_sems.at[7 * a + k], recv_sem=recv_sems.at[7 * a + k],
                device_id=to, device_id_type=MESH)

        mine = [pltpu.make_async_copy(x_refs[a], slot(a, *me), local_sems.at[a]) for a in range(n)]
        for cp in mine:
            cp.start()
        first = []
        for j, chip in enumerate(chips):
            first += [copy(a, 1 + j, me, (*chip, c), src=x_refs[a]) for a in range(n)]
        first += [copy(a, 0, me, sibling, src=x_refs[a]) for a in range(n)]
        for cp in first:
            cp.start()
        passed = []
        for j, chip in enumerate(chips):
            for a in range(n):
                copy(a, 1 + j, (*chip, c), me).wait_recv()
                passed.append(copy(a, 4 + j, (*chip, c), sibling))
                passed[-1].start()
        for a in range(n):
            copy(a, 0, sibling, me).wait_recv()
        for j, chip in enumerate(chips):
            for a in range(n):
                copy(a, 4 + j, (*chip, 1 - c), me).wait_recv()
        for cp in first + passed:
            cp.wait_send()
        for cp in mine:
            cp.wait()

    return pl.pallas_call(
        body, name=name, in_specs=[ANY] * n, out_specs=[ANY] * n,
        out_shape=[jax.ShapeDtypeStruct((8,) + b.shape, b.dtype) for b in blocks],
        scratch_shapes=[pltpu.SemaphoreType.DMA((7 * n,)), pltpu.SemaphoreType.DMA((7 * n,)),
                        pltpu.SemaphoreType.DMA((n,))],
    )(*blocks)


def _exchange(comm, *, name, aliases=None):
    ns, no = len(comm.srcs), len(comm.out_shapes)

    def body(*refs):
        copies = _comm_copies(comm.plan, refs[:ns], refs[ns:ns + no], *refs[ns + no:])
        _comm_start(copies)
        _comm_wait(copies)

    return pl.pallas_call(
        body, name=name, in_specs=[ANY] * ns, out_specs=[ANY] * no, out_shape=list(comm.out_shapes),
        scratch_shapes=comm.scratch(), input_output_aliases=aliases or {},
    )(*comm.srcs)


def _gather_comm(halves):
    n = len(halves)

    def plan(src, out, x, y, c):
        mine = 4 * x + 2 * y + c
        remote = [(src[a], out[a].at[mine], (x, y, 1 - c), out[a].at[4 * x + 2 * y + 1 - c]) for a in range(n)]
        for fx, fy in CHIP_FLIPS:
            px, py = _flip(x, fx), _flip(y, fy)
            remote += [(src[a], out[a].at[mine], (px, py, c), out[a].at[4 * px + 2 * py + c]) for a in range(n)]
        return remote, [(src[a], out[a].at[mine]) for a in range(n)]

    return Comm(list(halves), [jax.ShapeDtypeStruct((8,) + h.shape, h.dtype) for h in halves], plan, 4 * n, n)


def _gather_forward(bufs, *, name):
    n = len(bufs)

    def plan(src, out, x, y, c):
        remote = []
        for fx, fy in CHIP_FLIPS:
            px, py = _flip(x, fx), _flip(y, fy)
            remote += [(src[a].at[4 * px + 2 * py + c], out[a].at[4 * px + 2 * py + c], (x, y, 1 - c),
                        out[a].at[4 * px + 2 * py + 1 - c]) for a in range(n)]
        return remote, []

    comm = Comm(list(bufs), [jax.ShapeDtypeStruct(b.shape, b.dtype) for b in bufs], plan, 3 * n, 0)
    return _exchange(comm, name=name, aliases={a: a for a in range(n)})


CHIP_FLIPS = ((1, 0), (0, 1), (1, 1))


def _flip(v, f):
    return 1 - v if f else v


def _sum_rows_tile(R, C, budget=3 << 20):
    best = None
    for t in range(8, R + 1, 8):
        if R % t == 0 and t * C * 4 <= budget:
            best = t
    return best if best is not None else R


def _rs_begin(gps, *, wire_dtype):
    n = len(gps)
    c_arr = jnp.reshape(_position()[2], (1,)).astype(jnp.int32)

    def plan(src, out, x, y, c):
        return [(src[a].at[b, 1 - c], out[a].at[b], (x, y, 1 - c), out[a].at[b])
                for a in range(n) for b in range(4)], []

    got1 = _exchange(Comm(list(gps), [jax.ShapeDtypeStruct((4,) + g.shape[2:], F32) for g in gps], plan, 4 * n, 0),
                     name='rs_sibling')

    def sum_chip(c_ref, mine_ref, got_ref, out_ref):
        out_ref[...] = (mine_ref[...] + got_ref[...]).astype(out_ref.dtype)

    parts = []
    for gp, g1 in zip(gps, got1):
        _, _, R, C = gp.shape
        tr = _sum_rows_tile(R, C)
        parts.append(pl.pallas_call(
            sum_chip, name='rs_sum_chip',
            grid_spec=pltpu.PrefetchScalarGridSpec(
                num_scalar_prefetch=1, grid=(4, R // tr),
                in_specs=[pl.BlockSpec((None, None, tr, C), lambda b, r, cr: (b, cr[0], r, 0)),
                          pl.BlockSpec((None, tr, C), lambda b, r, cr: (b, r, 0))],
                out_specs=pl.BlockSpec((None, tr, C), lambda b, r, cr: (b, r, 0))),
            out_shape=jax.ShapeDtypeStruct((4, R, C), wire_dtype),
            compiler_params=_cparams("parallel", "parallel"),
        )(c_arr, gp, g1))
    return got1, parts


def _rs_chips_comm(parts):
    n = len(parts)

    def plan(src, out, x, y, c):
        remote = []
        for k, (fx, fy) in enumerate(CHIP_FLIPS):
            px, py = _flip(x, fx), _flip(y, fy)
            remote += [(src[a].at[2 * px + py], out[a].at[k], (px, py, c), out[a].at[k]) for a in range(n)]
        return remote, []

    return Comm(list(parts), [jax.ShapeDtypeStruct((3,) + p.shape[1:], p.dtype) for p in parts], plan, 3 * n, 0)


def _rs_finish(gps, got1, got2):
    n = len(gps)
    x, y, c = _position()
    bc_arr = jnp.stack([2 * x + y, c]).astype(jnp.int32)

    def sum_final(bc_ref, mine_ref, got1_ref, got2_ref, out_ref):
        acc = mine_ref[...] + got1_ref[...]
        for k in range(3):
            acc = acc + got2_ref[k].astype(F32)
        out_ref[...] = acc

    halves = []
    for gp, g1, g2 in zip(gps, got1, got2):
        _, _, R, C = gp.shape
        tr = _sum_rows_tile(R, C, budget=2 << 20)
        halves.append(pl.pallas_call(
            sum_final, name='rs_sum_final',
            grid_spec=pltpu.PrefetchScalarGridSpec(
                num_scalar_prefetch=1, grid=(R // tr,),
                in_specs=[pl.BlockSpec((None, None, tr, C), lambda r, bc: (bc[0], bc[1], r, 0)),
                          pl.BlockSpec((None, tr, C), lambda r, bc: (bc[0], r, 0)),
                          pl.BlockSpec((3, tr, C), lambda r, bc: (0, r, 0))],
                out_specs=pl.BlockSpec((None, tr, C), lambda r, bc: (bc[1], r, 0))),
            out_shape=jax.ShapeDtypeStruct((2, R, C), F32),
            compiler_params=_cparams("parallel"),
        )(bc_arr, gp, g1, g2))

    def plan(src, out, x, y, c):
        return [(src[a].at[c], out[a].at[c], (x, y, 1 - c), out[a].at[1 - c]) for a in range(n)], []

    comm = Comm(halves, [jax.ShapeDtypeStruct(h.shape, F32) for h in halves], plan, n, 0)
    return _exchange(comm, name='rs_swap', aliases={a: a for a in range(n)})


def _sum8(gathered, *, name):
    _, R, C = gathered.shape

    def body(g_ref, o_ref):
        acc = g_ref[0]
        for k in range(1, 8):
            acc = acc + g_ref[k]
        o_ref[...] = acc

    return pl.pallas_call(body, name=name, out_shape=jax.ShapeDtypeStruct((R, C), F32))(gathered)


def _adamw(w, g, m, v, *, name):
    shape = w.shape
    cols = shape[-1]
    rows = w.size // cols
    tr = _sum_rows_tile(rows, cols, budget=1 << 20)

    def body(w_ref, g_ref, m_ref, v_ref, d_ref, mo_ref, vo_ref):
        gv = g_ref[...]
        mn = ADAM_B1 * m_ref[...] + (1.0 - ADAM_B1) * gv
        vn = ADAM_B2 * v_ref[...] + (1.0 - ADAM_B2) * (gv * gv)
        m_hat = mn / (1.0 - ADAM_B1 ** ADAM_STEP)
        v_hat = vn / (1.0 - ADAM_B2 ** ADAM_STEP)
        d_ref[...] = -ADAM_LR * (m_hat / (jnp.sqrt(v_hat) + ADAM_EPS) + ADAM_WD * w_ref[...])
        mo_ref[...] = mn
        vo_ref[...] = vn

    blk = pl.BlockSpec((tr, cols), lambda i: (i, 0))
    outs = pl.pallas_call(
        body, name=name, grid=(rows // tr,), in_specs=[blk] * 4, out_specs=[blk] * 3,
        out_shape=[jax.ShapeDtypeStruct((rows, cols), F32)] * 3,
        compiler_params=_cparams("parallel"),
    )(*[a.reshape(rows, cols) for a in (w, g, m, v)])
    return tuple(o.reshape(shape) for o in outs)


WEIGHTS = ['norm_mix', 'norm_mlp', 'w_ada', 'b_ada', 'w_mlp_in', 'w_mlp_out', 'fox_w_in', 'fox_b_f',
           'fox_q_norm', 'fox_k_norm', 'fox_w_out', 'sg_w_in', 'sg_ln_g', 'sg_ln_b', 'sg_w_s', 'sg_b_s',
           'sg_w_out', 'cv_w_pw1', 'cv_b_pw1', 'cv_w_dw', 'cv_b_dw', 'cv_ln_g', 'cv_ln_b', 'cv_w_pw2',
           'cv_b_pw2']
BIG = {'w_mlp_in': 2, 'w_mlp_out': 1, 'fox_w_in': 2, 'fox_w_out': 1, 'sg_w_in': 2, 'sg_w_out': 1,
       'cv_w_pw1': 2, 'cv_w_pw2': 1}
SMALL_SHARDED = ['cv_b_pw1', 'cv_w_dw', 'cv_b_dw', 'cv_ln_g', 'cv_ln_b', 'cv_b_pw2']
SMALL_GRADS = ['norm_mix', 'norm_mlp', 'fox_b_f', 'fox_q_norm', 'fox_k_norm', 'sg_ln_g', 'sg_ln_b', 'sg_w_s',
               'sg_b_s'] + SMALL_SHARDED
GRAD_WIRE_DTYPE = BF16


def _pack_rows(parts, cols):
    flat = jnp.concatenate([p.reshape(-1) for p in parts])
    rows = -(-flat.size // (8 * cols)) * 8
    return jnp.pad(flat, (0, rows * cols - flat.size)).reshape(rows, cols)


def _unpack(flat, shapes):
    out, off = [], 0
    for s in shapes:
        n = math.prod(s)
        out.append(flat[..., off:off + n].reshape(flat.shape[:-1] + tuple(s)))
        off += n
    return out


def _merge_chips(a, axis):
    a = jnp.moveaxis(a, 0, axis)
    return a.reshape(a.shape[:axis] + (a.shape[axis] * a.shape[axis + 1],) + a.shape[axis + 2:])


def _split_chips(a, axis):
    a = a.reshape(a.shape[:axis] + (4, a.shape[axis] // 4) + a.shape[axis + 1:])
    return jnp.moveaxis(a, axis, 0)


def _step(a):
    x, y, c = _position()
    me = 4 * x + 2 * y + c
    chip = 2 * x + y
    T, D = a['x'].shape[1], a['x'].shape[2]
    L = a['norm_mix'].shape[0]

    small_shapes = [(D,)] + [a[n].shape for n in SMALL_SHARDED]
    small = _all_gather8([_pack_rows([a['c']] + [a[n] for n in SMALL_SHARDED], LANE)], name='ag_small')[0]
    small = small.reshape(8, -1)
    c_all = _unpack(small, small_shapes[:1])[0]
    sharded = _unpack(small[0::2, D:], small_shapes[1:])
    P = {n: _merge_chips(v, v.ndim - 2) for n, v in zip(SMALL_SHARDED, sharded)}

    c_act = _silu_rows(c_all, name='c_act')
    mod_cols = jnp.stack([
        _mm(c_act, a['w_ada'][i], name='ada_mod', tm=8, tn=_col_tile(a['w_ada'].shape[2], 768),
            extras=[(lax.dynamic_slice_in_dim(a['b_ada'][i:i + 1], chip * a['w_ada'].shape[2],
                                              a['w_ada'].shape[2], axis=1), 'row')],
            epilogue=_bias_epilogue)
        for i in range(L)])
    mod_all = _all_gather8([mod_cols.reshape(L * 8, -1)], name='ag_mod')[0].reshape(8, L, 8, -1)
    mod = lax.dynamic_index_in_dim(mod_all[0::2], me, axis=2, keepdims=False)
    mod = jnp.moveaxis(mod, 0, 1).reshape(L, 6 * D)

    units = _matrix_units(L)
    first, later = units[:UNITS_PER_LAYER], units[UNITS_PER_LAYER:]
    n_heads = D // HEAD_DIM

    def half_block(unit):
        blk = a[unit[0]][unit[1]]
        return lax.dynamic_index_in_dim(blk.astype(BF16).reshape(2, blk.shape[0] // 2, blk.shape[1]), c, axis=0,
                                        keepdims=False)

    def install(group, gathered):
        for (name, idx), gth in zip(group, gathered):
            blocks = gth.reshape((4,) + a[name].shape[1:])
            if name == 'fox_w_in':
                pad = jnp.zeros((blocks.shape[1], LANE - n_heads), BF16)
                full = jnp.concatenate([blocks[0], blocks[1], blocks[2], blocks[3], pad], axis=-1)
            else:
                full = _merge_chips(blocks, BIG[name] - 1)
            if name in ('w_mlp_in', 'w_mlp_out', 'fox_w_in', 'fox_w_out'):
                P.setdefault(name, {})[idx] = full
            else:
                P[name] = full

    install(first, _all_gather8([half_block(u) for u in first], name='ag_weights_first'))
    for n in ('sg_w_s', 'sg_b_s', 'cv_w_dw'):
        P[n] = (P[n] if n in P else a[n])[0]
    for n in ('norm_mix', 'norm_mlp', 'fox_b_f', 'fox_q_norm', 'fox_k_norm', 'sg_ln_g', 'sg_ln_b'):
        P[n] = a[n]

    def split_grad(unit, grad):
        name = unit[0]
        if name == 'fox_w_in':
            grad = grad[:, :a[name].shape[2] * 4]
        blk = _split_chips(grad, BIG[name] - 1)
        return blk.reshape(4, 2, blk.shape[1] // 2, blk.shape[2])

    state = {}

    def fwd_done(outs):
        install(later, _gather_forward(outs, name='ag_weights_forward'))

    def bwd_comm(mat):
        state['gps'] = [split_grad(u, mat[u]) for u in later]
        state['got1'], parts = _rs_begin(state['gps'], wire_dtype=GRAD_WIRE_DTYPE)
        return _rs_chips_comm(parts)

    def bwd_done(outs):
        state['got2'] = outs

    hooks = Hooks(_gather_comm([half_block(u) for u in later]), fwd_done, bwd_comm, bwd_done)
    loss_part, grad_x, dmod, g, mat = _local_step(a['x'][0], a['loss_target'][0], mod, P, hooks)

    small_g = [dmod, loss_part[0:1, 0:1]] + [g[n] for n in SMALL_GRADS]
    small_g_shapes = [s.shape for s in small_g]
    all_small = _all_gather8([_pack_rows(small_g, LANE)], name='ag_small_grads')[0]
    summed = _sum8(all_small, name='sum_small_grads').reshape(-1)
    sums = _unpack(summed, small_g_shapes)
    loss = sums[1][0, 0]
    grads = dict(zip(SMALL_GRADS, sums[2:]))
    grads['b_ada'] = sums[0]
    for n in SMALL_SHARDED:
        blk = a[n].shape[-1]
        grads[n] = lax.dynamic_slice_in_dim(grads[n], chip * blk, blk, axis=grads[n].ndim - 1)
    dmod_all = all_small.reshape(8, -1)[:, :dmod.size].reshape(8, L, 6 * D)
    cols = a['w_ada'].shape[2]
    dmod_cols = lax.dynamic_slice_in_dim(dmod_all, chip * cols, cols, axis=2)
    pad8 = lambda t: jnp.pad(t, ((0, LANE - 8), (0, 0)))
    c_act_pad = pad8(c_act)
    grads['w_ada'] = jnp.stack([
        _mm(c_act_pad, pad8(dmod_cols[:, i]), ta=True, name='ada_dw', tn=_col_tile(cols, 768))
        for i in range(L)])

    shards = dict(zip(later, _rs_finish(state['gps'], state['got1'], state['got2'])))
    gps = [split_grad(u, mat[u]) for u in first]
    got1, parts = _rs_begin(gps, wire_dtype=GRAD_WIRE_DTYPE)
    got2 = _exchange(_rs_chips_comm(parts), name='rs_chips')
    shards.update(zip(first, _rs_finish(gps, got1, got2)))
    for n in BIG:
        grads[n] = jnp.stack([shards[n, idx].reshape(a[n].shape[1:]) for idx in range(a[n].shape[0])])

    deltas, new_m, new_v = {}, {}, {}
    for n in WEIGHTS:
        deltas[n], new_m[n], new_v[n] = _adamw(a[n], grads[n], a['m_' + n], a['v_' + n], name='adamw')
    return (loss, grad_x[None], *[grads[n] for n in WEIGHTS], *[deltas[n] for n in WEIGHTS],
            *[new_m[n] for n in WEIGHTS], *[new_v[n] for n in WEIGHTS])


UNITS_PER_LAYER = 4


def _matrix_units(n_layers):
    mixers = (('fox_w_in', 'fox_w_out'), ('sg_w_in', 'sg_w_out'), ('cv_w_pw1', 'cv_w_pw2'))
    units = []
    for i in range(n_layers):
        units += [(n, i // N_MIXERS) for n in mixers[i % N_MIXERS]] + [('w_mlp_in', i), ('w_mlp_out', i)]
    return units


def _silu_rows(x, *, name):
    def body(x_ref, o_ref):
        xv = x_ref[...]
        o_ref[...] = (xv * jax.nn.sigmoid(xv)).astype(BF16)

    return pl.pallas_call(body, name=name, out_shape=jax.ShapeDtypeStruct(x.shape, BF16))(x)


def kernel(x, c, norm_mix, norm_mlp, w_ada, b_ada, w_mlp_in, w_mlp_out, fox_w_in, fox_b_f, fox_q_norm, fox_k_norm, fox_w_out, sg_w_in, sg_ln_g, sg_ln_b, sg_w_s, sg_b_s, sg_w_out, cv_w_pw1, cv_b_pw1, cv_w_dw, cv_b_dw, cv_ln_g, cv_ln_b, cv_w_pw2, cv_b_pw2, loss_target, m_norm_mix, m_norm_mlp, m_w_ada, m_b_ada, m_w_mlp_in, m_w_mlp_out, m_fox_w_in, m_fox_b_f, m_fox_q_norm, m_fox_k_norm, m_fox_w_out, m_sg_w_in, m_sg_ln_g, m_sg_ln_b, m_sg_w_s, m_sg_b_s, m_sg_w_out, m_cv_w_pw1, m_cv_b_pw1, m_cv_w_dw, m_cv_b_dw, m_cv_ln_g, m_cv_ln_b, m_cv_w_pw2, m_cv_b_pw2, v_norm_mix, v_norm_mlp, v_w_ada, v_b_ada, v_w_mlp_in, v_w_mlp_out, v_fox_w_in, v_fox_b_f, v_fox_q_norm, v_fox_k_norm, v_fox_w_out, v_sg_w_in, v_sg_ln_g, v_sg_ln_b, v_sg_w_s, v_sg_b_s, v_sg_w_out, v_cv_w_pw1, v_cv_b_pw1, v_cv_w_dw, v_cv_b_dw, v_cv_ln_g, v_cv_ln_b, v_cv_w_pw2, v_cv_b_pw2):
    return _step(dict(locals()))
```

```python
import math
from typing import Callable, NamedTuple

import jax
import jax.numpy as jnp
from jax import lax
from jax.experimental import pallas as pl
from jax.experimental.pallas import tpu as pltpu

F32 = jnp.float32
BF16 = jnp.bfloat16

EPS = 1e-6
HEAD_DIM = 64
LANE = 128
CONV_WIDTH = 31
CONV_HALO = 32
SG_CHUNK = 128
SG_CAUSAL = 64
SG_GROUPS = 8
N_MIXERS = 3
VMEM_LIMIT = 56 * 1024 * 1024
NEG = -1e30

ADAM_LR = 0.001
ADAM_B1 = 0.9
ADAM_B2 = 0.999
ADAM_EPS = 1e-08
ADAM_WD = 0.01
ADAM_STEP = 10

MESH = pl.DeviceIdType.MESH
ANY = pl.BlockSpec(memory_space=pl.ANY)


def _cparams(*sem):
    return pltpu.CompilerParams(dimension_semantics=sem, vmem_limit_bytes=VMEM_LIMIT)


def _row_tile(t, want=512):
    return min(t, want)


def _matmul(a, b, *, name, ta=False, tb=False, tm=512, tn=1024, tk=1024,
            extras=(), epilogue=None, out_dtypes=(F32,), b_outer=False):
    M, K = (a.shape[1], a.shape[0]) if ta else a.shape
    N = b.shape[0] if tb else b.shape[1]
    assert (b.shape[1] if tb else b.shape[0]) == K
    tm, tn, tk = min(tm, M), min(tn, N), min(tk, K)
    assert M % tm == 0 and N % tn == 0 and K % tk == 0, (name, M, N, K, tm, tn, tk)
    nk = K // tk

    def spec(shape, pick):
        if b_outer:
            return pl.BlockSpec(shape, lambda j, i, k: pick(i, j, k))
        return pl.BlockSpec(shape, pick)

    a_spec = spec((tk, tm), lambda i, j, k: (k, i)) if ta else spec((tm, tk), lambda i, j, k: (i, k))
    b_spec = spec((tn, tk), lambda i, j, k: (j, k)) if tb else spec((tk, tn), lambda i, j, k: (k, j))
    ex_specs = [spec((tm, tn), lambda i, j, k: (i, j)) if kind == 'tile' else spec((1, tn), lambda i, j, k: (0, j))
                for _, kind in extras]
    dims = (((0,) if ta else (1,), (1,) if tb else (0,)), ((), ()))
    n_ex, n_out = len(extras), len(out_dtypes)

    def body(*refs):
        a_ref, b_ref = refs[0], refs[1]
        ex = refs[2:2 + n_ex]
        outs = refs[2 + n_ex:2 + n_ex + n_out]

        def finish(acc):
            vals = epilogue(acc, *[r[...] for r in ex]) if epilogue else (acc,)
            for o, v in zip(outs, vals):
                o[...] = v.astype(o.dtype)

        part = lax.dot_general(a_ref[...].astype(BF16), b_ref[...].astype(BF16), dims,
                               preferred_element_type=F32)
        if nk == 1:
            finish(part)
        else:
            acc_ref = refs[-1]
            k = pl.program_id(2)

            @pl.when(k == 0)
            def _():
                acc_ref[...] = part

            @pl.when(k > 0)
            def _():
                acc_ref[...] += part

            @pl.when(k == nk - 1)
            def _():
                finish(acc_ref[...])

    outs = pl.pallas_call(
        body, name=name,
        grid=(N // tn, M // tm, nk) if b_outer else (M // tm, N // tn, nk),
        in_specs=[a_spec, b_spec] + ex_specs,
        out_specs=[spec((tm, tn), lambda i, j, k: (i, j)) for _ in out_dtypes],
        out_shape=[jax.ShapeDtypeStruct((M, N), dt) for dt in out_dtypes],
        scratch_shapes=[pltpu.VMEM((tm, tn), F32)] if nk > 1 else [],
        compiler_params=_cparams("parallel", "parallel", "arbitrary"),
    )(a, b, *[arr for arr, _ in extras])
    return outs if n_out > 1 else outs[0]


def _norm_mod_fwd(x, w, sc, sh, *, name):
    T, D = x.shape
    tr = _row_tile(T)

    def body(x_ref, w_ref, sc_ref, sh_ref, h_ref):
        xv = x_ref[...]
        r = lax.rsqrt(jnp.mean(xv * xv, axis=-1, keepdims=True) + EPS)
        h_ref[...] = ((xv * r) * w_ref[...] * (1.0 + sc_ref[...]) + sh_ref[...]).astype(BF16)

    row = pl.BlockSpec((1, D), lambda i: (0, 0))
    return pl.pallas_call(
        body, name=name, grid=(T // tr,),
        in_specs=[pl.BlockSpec((tr, D), lambda i: (i, 0)), row, row, row],
        out_specs=pl.BlockSpec((tr, D), lambda i: (i, 0)),
        out_shape=jax.ShapeDtypeStruct((T, D), BF16),
        compiler_params=_cparams("parallel"),
    )(x, w, sc, sh)


def _norm_mod_bwd(dh, x, dres, w, sc, *, name, gate=None):
    T, D = x.shape
    tr = _row_tile(T)
    with_gate = gate is not None

    def body(*refs):
        if with_gate:
            dh_ref, x_ref, dres_ref, w_ref, sc_ref, y_ref, g_ref, dx_ref, sums_ref, dy_ref = refs
        else:
            dh_ref, x_ref, dres_ref, w_ref, sc_ref, dx_ref, sums_ref = refs
        i = pl.program_id(0)
        xv, dhv = x_ref[...], dh_ref[...].astype(F32)
        r = lax.rsqrt(jnp.mean(xv * xv, axis=-1, keepdims=True) + EPS)
        n = xv * r
        wv, scale = w_ref[...], 1.0 + sc_ref[...]
        dn = dhv * (wv * scale)
        dx = dres_ref[...] + r * (dn - n * jnp.mean(dn * n, axis=-1, keepdims=True))
        dx_ref[...] = dx
        rows = [jnp.sum(dhv, axis=0, keepdims=True),
                jnp.sum(dhv * (n * wv), axis=0, keepdims=True),
                jnp.sum(dhv * n * scale, axis=0, keepdims=True)]
        if with_gate:
            dy_ref[...] = (dx * g_ref[...]).astype(BF16)
            rows.append(jnp.sum(dx * y_ref[...], axis=0, keepdims=True))
            rows.append(jnp.sum(dx * g_ref[...], axis=0, keepdims=True))
        part = jnp.concatenate(rows + [jnp.zeros((8 - len(rows), D), F32)], axis=0)

        @pl.when(i == 0)
        def _():
            sums_ref[...] = part

        @pl.when(i > 0)
        def _():
            sums_ref[...] += part

    blk = pl.BlockSpec((tr, D), lambda i: (i, 0))
    row = pl.BlockSpec((1, D), lambda i: (0, 0))
    in_specs = [blk, blk, blk, row, row]
    args = [dh, x, dres, w, sc]
    out_specs = [blk, pl.BlockSpec((8, D), lambda i: (0, 0))]
    out_shape = [jax.ShapeDtypeStruct((T, D), F32), jax.ShapeDtypeStruct((8, D), F32)]
    if with_gate:
        in_specs += [blk, row]
        args += list(gate)
        out_specs.append(blk)
        out_shape.append(jax.ShapeDtypeStruct((T, D), BF16))
    return pl.pallas_call(
        body, name=name, grid=(T // tr,), in_specs=in_specs, out_specs=out_specs,
        out_shape=out_shape, compiler_params=_cparams("arbitrary"),
    )(*args)


def _gate_bwd(dx, y, g, *, name):
    T, D = dx.shape
    tr = _row_tile(T)

    def body(dx_ref, y_ref, g_ref, dy_ref, dg_ref):
        i = pl.program_id(0)
        dxv = dx_ref[...]
        dy_ref[...] = (dxv * g_ref[...]).astype(BF16)
        part = jnp.concatenate([jnp.sum(dxv * y_ref[...], axis=0, keepdims=True),
                                jnp.zeros((7, D), F32)], axis=0)

        @pl.when(i == 0)
        def _():
            dg_ref[...] = part

        @pl.when(i > 0)
        def _():
            dg_ref[...] += part

    blk = pl.BlockSpec((tr, D), lambda i: (i, 0))
    return pl.pallas_call(
        body, name=name, grid=(T // tr,),
        in_specs=[blk, blk, pl.BlockSpec((1, D), lambda i: (0, 0))],
        out_specs=[blk, pl.BlockSpec((8, D), lambda i: (0, 0))],
        out_shape=[jax.ShapeDtypeStruct((T, D), BF16), jax.ShapeDtypeStruct((8, D), F32)],
        compiler_params=_cparams("arbitrary"),
    )(dx, y, g)


def _loss_head(y, target, *, name):
    T, D = y.shape
    tr = _row_tile(T)

    def body(y_ref, t_ref, loss_ref, dy_ref):
        i = pl.program_id(0)
        e = y_ref[...] - t_ref[...]
        dy_ref[...] = e * (1.0 / D)
        part = jnp.full((8, LANE), 0.5 / D * jnp.sum(e * e), F32)

        @pl.when(i == 0)
        def _():
            loss_ref[...] = part

        @pl.when(i > 0)
        def _():
            loss_ref[...] += part

    blk = pl.BlockSpec((tr, D), lambda i: (i, 0))
    return pl.pallas_call(
        body, name=name, grid=(T // tr,), in_specs=[blk, blk],
        out_specs=[pl.BlockSpec((8, LANE), lambda i: (0, 0)), blk],
        out_shape=[jax.ShapeDtypeStruct((8, LANE), F32), jax.ShapeDtypeStruct((T, D), F32)],
        compiler_params=_cparams("arbitrary"),
    )(y, target)


def _position():
    return lax.axis_index("x"), lax.axis_index("y"), lax.axis_index("c")


class Comm(NamedTuple):
    srcs: list
    out_shapes: list
    plan: Callable
    n_remote: int
    n_local: int

    def scratch(self):
        return [pltpu.SemaphoreType.DMA((self.n_remote,)), pltpu.SemaphoreType.DMA((self.n_remote,)),
                pltpu.SemaphoreType.DMA((max(self.n_local, 1),))]


def _comm_copies(plan, src_refs, out_refs, send_sems, recv_sems, local_sems):
    x, y, c = _position()
    remote, local = plan(src_refs, out_refs, x, y, c)

    def copy(k, s, d, peer):
        return pltpu.make_async_remote_copy(src_ref=s, dst_ref=d, send_sem=send_sems.at[k],
                                            recv_sem=recv_sems.at[k], device_id=peer, device_id_type=MESH)

    sends = [copy(k, s, d, peer) for k, (s, d, peer, _) in enumerate(remote)]
    recvs = [copy(k, s, landing, peer) for k, (s, _, peer, landing) in enumerate(remote)]
    local_copies = [pltpu.make_async_copy(s, d, local_sems.at[i]) for i, (s, d) in enumerate(local)]
    return sends, recvs, local_copies


def _comm_start(copies):
    sends, _, local_copies = copies
    for cp in local_copies + sends:
        cp.start()


def _comm_wait(copies):
    sends, recvs, local_copies = copies
    for cp in recvs:
        cp.wait_recv()
    for cp in sends:
        cp.wait_send()
    for cp in local_copies:
        cp.wait()


def _split_comm_refs(refs, n_in, n_out, n_scratch, comm):
    ns, nd = (len(comm.srcs), len(comm.out_shapes)) if comm else (0, 0)
    cuts = [n_in, ns, n_out, nd, n_scratch]
    parts, at = [], 0
    for n in cuts:
        parts.append(refs[at:at + n])
        at += n
    return (*parts, refs[at:])


AUG_F = HEAD_DIM
AUG_LSE = HEAD_DIM + 6


def _half_cols(x, lo):
    return (jnp.sum(jnp.where(lo, x, 0.0), axis=-1, keepdims=True),
            jnp.sum(jnp.where(lo, 0.0, x), axis=-1, keepdims=True))


def _half_sums(x, lo):
    s_lo, s_hi = _half_cols(x, lo)
    return jnp.where(lo, s_lo, s_hi)


def _split3(x):
    a = x.astype(BF16).astype(F32)
    r = x - a
    b = r.astype(BF16).astype(F32)
    return a, b, (r - b).astype(BF16).astype(F32)


def _aug(lane, base, terms):
    out = jnp.zeros(lane.shape, F32)
    for i, t in enumerate(terms):
        out = jnp.where(lane == base + i, t, out)
    return out


def _head_lanes(x2, h):
    return x2 if h == 0 else pltpu.roll(x2, HEAD_DIM, 1)


def _fox_prep_fwd(proj, qg, kg, fcol, *, d_model, name):
    T = proj.shape[0]
    nhp = d_model // LANE
    tr = _row_tile(T)

    def body(q_ref, k_ref, v_ref, qg_ref, kg_ref, f_ref, qa_ref, qta_ref, ka_ref, kta_ref, va_ref, vta_ref):
        lane = lax.broadcasted_iota(jnp.int32, (tr, LANE), 1)
        lo = lane < HEAD_DIM

        def norm(xv, g):
            ms = _half_sums(xv * xv, lo) * (1.0 / HEAD_DIM)
            return (xv * lax.rsqrt(ms + EPS)) * g

        qn = norm(q_ref[...], qg_ref[...]) * (HEAD_DIM ** -0.5)
        kn = norm(k_ref[...], kg_ref[...])
        vv = v_ref[...]
        qa, ka, va, vta = [], [], [], []
        for h in range(2):
            f1, f2, f3 = _split3(f_ref[h])
            qa.append(jnp.where(lo, _head_lanes(qn, h), _aug(lane, AUG_F, [f1, f2, f3, 1.0, 1.0, 1.0])))
            ka.append(jnp.where(lo, _head_lanes(kn, h),
                                _aug(lane, AUG_F, [1.0, 1.0, 1.0, -f1, -f2, -f3, 1.0, 1.0, 1.0])))
            va.append(jnp.where(lo if h == 0 else jnp.logical_not(lo), vv, 0.0))
            vta.append(jnp.where(lo, _head_lanes(vv, h), _aug(lane, AUG_F, [1.0, 1.0, 1.0])))
        for parts, ref, tref in ((qa, qa_ref, qta_ref), (ka, ka_ref, kta_ref), (va, va_ref, None),
                                 (vta, None, vta_ref)):
            both = jnp.concatenate(parts, axis=1)
            if ref is not None:
                ref[...] = both.astype(BF16)
            if tref is not None:
                tref[...] = both.T.astype(BF16)

    gain = pl.BlockSpec((1, LANE), lambda i, h: (0, 0))
    rows = pl.BlockSpec((tr, 2 * LANE), lambda i, h: (i, h))
    cols = pl.BlockSpec((2 * LANE, tr), lambda i, h: (h, i))
    wide, tall = jax.ShapeDtypeStruct((T, 2 * d_model), BF16), jax.ShapeDtypeStruct((2 * d_model, T), BF16)
    return pl.pallas_call(
        body, name=name, grid=(T // tr, nhp),
        in_specs=[pl.BlockSpec((tr, LANE), lambda i, h: (i, h)),
                  pl.BlockSpec((tr, LANE), lambda i, h: (i, nhp + h)),
                  pl.BlockSpec((tr, LANE), lambda i, h: (i, 2 * nhp + h)), gain, gain,
                  pl.BlockSpec((2, tr, 1), lambda i, h: (h, i, 0))],
        out_specs=[rows, cols, rows, cols, rows, cols],
        out_shape=[wide, tall, wide, tall, wide, tall],
        compiler_params=_cparams("parallel", "parallel"),
    )(proj, proj, proj, qg, kg, fcol)


def _fox_do_prep(do, o, *, name):
    T, D = do.shape
    nhp = D // LANE
    tr = _row_tile(T)

    def body(do_ref, o_ref, doa_ref, dota_ref):
        lane = lax.broadcasted_iota(jnp.int32, (tr, LANE), 1)
        lo = lane < HEAD_DIM
        dob = do_ref[...].astype(BF16).astype(F32)
        deltas = _half_cols(dob * o_ref[...], lo)
        both = jnp.concatenate(
            [jnp.where(lo, _head_lanes(dob, h), _aug(lane, AUG_F, _split3(-deltas[h]))) for h in range(2)], axis=1)
        doa_ref[...] = both.astype(BF16)
        dota_ref[...] = both.T.astype(BF16)

    blk = pl.BlockSpec((tr, LANE), lambda i, h: (i, h))
    return pl.pallas_call(
        body, name=name, grid=(T // tr, nhp), in_specs=[blk, blk],
        out_specs=[pl.BlockSpec((tr, 2 * LANE), lambda i, h: (i, h)),
                   pl.BlockSpec((2 * LANE, tr), lambda i, h: (h, i))],
        out_shape=[jax.ShapeDtypeStruct((T, 2 * D), BF16), jax.ShapeDtypeStruct((2 * D, T), BF16)],
        compiler_params=_cparams("parallel", "parallel"),
    )(do, o)


def _fox_prep_bwd(proj, dq, dkt, dvt, qg, kg, *, d_model, name):
    T = proj.shape[0]
    nhp = d_model // LANE
    tr = _row_tile(T)

    def body(q_ref, k_ref, dq_ref, dkt_ref, dvt_ref, qg_ref, kg_ref, dqo_ref, dko_ref, dvo_ref, sums_ref):
        first = (pl.program_id(0) == 0) & (pl.program_id(1) == 0)
        lo = lax.broadcasted_iota(jnp.int32, (tr, LANE), 1) < HEAD_DIM

        def pair(x2):
            return jnp.where(lo, x2[:, :LANE], pltpu.roll(x2[:, LANE:], HEAD_DIM, 1))

        def bwd(xv, dxhat, g):
            ms = _half_sums(xv * xv, lo) * (1.0 / HEAD_DIM)
            r = lax.rsqrt(ms + EPS)
            n = xv * r
            dn = dxhat * g
            dx = r * (dn - n * (_half_sums(dn * n, lo) * (1.0 / HEAD_DIM)))
            dg = jnp.sum(dxhat * n, axis=0, keepdims=True)
            return dx, dg + pltpu.roll(dg, HEAD_DIM, 1)

        dxq, dgq = bwd(q_ref[...], pair(dq_ref[...]) * (HEAD_DIM ** -0.5), qg_ref[...])
        dxk, dgk = bwd(k_ref[...], pair(dkt_ref[...].T), kg_ref[...])
        dqo_ref[...] = dxq.astype(BF16)
        dko_ref[...] = dxk.astype(BF16)
        dvo_ref[...] = pair(dvt_ref[...].T).astype(BF16)
        part = jnp.concatenate([dgq, dgk, jnp.zeros((6, LANE), F32)], axis=0)

        @pl.when(first)
        def _():
            sums_ref[...] = part

        @pl.when(jnp.logical_not(first))
        def _():
            sums_ref[...] += part

    gain = pl.BlockSpec((1, LANE), lambda i, h: (0, 0))
    blk = pl.BlockSpec((tr, LANE), lambda i, h: (i, h))
    tall = pl.BlockSpec((2 * LANE, tr), lambda i, h: (h, i))
    return pl.pallas_call(
        body, name=name, grid=(T // tr, nhp),
        in_specs=[blk, pl.BlockSpec((tr, LANE), lambda i, h: (i, nhp + h)),
                  pl.BlockSpec((tr, 2 * LANE), lambda i, h: (i, h)), tall, tall, gain, gain],
        out_specs=[blk, blk, blk, pl.BlockSpec((8, LANE), lambda i, h: (0, 0))],
        out_shape=[jax.ShapeDtypeStruct((T, d_model), BF16)] * 3 + [jax.ShapeDtypeStruct((8, LANE), F32)],
        compiler_params=_cparams("arbitrary", "arbitrary"),
    )(proj, proj, dq, dkt, dvt, qg, kg)


def _scan_lanes(x, reverse):
    n = x.shape[-1]
    lane = lax.broadcasted_iota(jnp.int32, x.shape, 1)
    sh = 1
    while sh < n:
        if reverse:
            x = x + jnp.where(lane < n - sh, pltpu.roll(x, n - sh, 1), 0.0)
        else:
            x = x + jnp.where(lane >= sh, pltpu.roll(x, sh, 1), 0.0)
        sh *= 2
    return x


def _fox_gate_fwd(fpre_t, bf, *, name):
    def body(f_ref, b_ref, o_ref):
        xv = f_ref[...] + b_ref[...]
        logf = jnp.minimum(xv, 0.0) - jnp.log1p(jnp.exp(-jnp.abs(xv)))
        o_ref[...] = _scan_lanes(logf, reverse=False)

    return pl.pallas_call(body, name=name, out_shape=jax.ShapeDtypeStruct(fpre_t.shape, F32))(fpre_t, bf)


def _fox_gate_bwd(dcol, drow, fpre_t, bf, *, name):
    H = fpre_t.shape[0]

    def body(dc_ref, dr_ref, f_ref, b_ref, o_ref, db_ref):
        xv = f_ref[...] + b_ref[...]
        e = dc_ref[...] - dr_ref[...]
        dlogf = _scan_lanes(e, reverse=False) - e
        dpre = dlogf * (1.0 - jax.nn.sigmoid(xv))
        o_ref[...] = dpre
        db_ref[...] = jnp.broadcast_to(jnp.sum(dpre, axis=-1, keepdims=True), (H, LANE))

    return pl.pallas_call(
        body, name=name,
        out_shape=[jax.ShapeDtypeStruct(fpre_t.shape, F32), jax.ShapeDtypeStruct((H, LANE), F32)],
    )(dcol, drow, fpre_t, bf)


_NT = (((1,), (1,)), ((), ()))
_TN = (((0,), (0,)), ((), ()))
_NN = (((1,), (0,)), ((), ()))


def _attn_tile(T):
    return min(T, 512)


def _causal(tq, tk):
    return lax.broadcasted_iota(jnp.int32, (tq, tk), 1) <= lax.broadcasted_iota(jnp.int32, (tq, tk), 0)


def _fox_attn_fwd(qa, kta, va, *, name, comm=None):
    T = qa.shape[0]
    nhp = qa.shape[1] // (2 * LANE)
    tq = tk = _attn_tile(T)
    nq = T // tq

    def body(*refs):
        (qa_ref, kta_ref, va_ref), src_refs, (o_ref, qb_ref), dst_refs, (m_sc, l_sc, acc_sc), sems = (
            _split_comm_refs(refs, 3, 2, 3, comm))
        hp, i, j = pl.program_id(0), pl.program_id(1), pl.program_id(2)
        if comm:
            @pl.when((hp == 0) & (i == 0) & (j == 0))
            def _():
                _comm_start(_comm_copies(comm.plan, src_refs, dst_refs, *sems))

        @pl.when(j == 0)
        def _():
            m_sc[...] = jnp.full(m_sc.shape, NEG, F32)
            l_sc[...] = jnp.zeros(l_sc.shape, F32)
            acc_sc[...] = jnp.zeros(acc_sc.shape, F32)

        def block(diagonal):
            for h in range(2):
                hs = slice(h * LANE, (h + 1) * LANE)
                s = lax.dot_general(qa_ref[:, hs], kta_ref[hs, :], _NN, preferred_element_type=F32)
                if diagonal:
                    s = jnp.where(_causal(tq, tk), s, NEG)
                m_prev = m_sc[h]
                m_next = jnp.maximum(m_prev, jnp.max(s, axis=1, keepdims=True))
                p = jnp.exp(s - jnp.tile(m_next, (1, tk // LANE)))
                alpha = jnp.exp(m_prev - m_next)
                l_sc[h] = alpha * l_sc[h] + jnp.sum(p, axis=1, keepdims=True)
                m_sc[h] = m_next
                acc_sc[h] = alpha * acc_sc[h] + lax.dot_general(p.astype(BF16), va_ref[:, hs], _NN,
                                                                preferred_element_type=F32)

        @pl.when(j < i)
        def _():
            block(False)

        @pl.when(j == i)
        def _():
            block(True)
            o_ref[...] = acc_sc[0] / l_sc[0] + acc_sc[1] / l_sc[1]
            lane = lax.broadcasted_iota(jnp.int32, (tq, LANE), 1)
            for h in range(2):
                hs = slice(h * LANE, (h + 1) * LANE)
                pieces = _split3(-(m_sc[h] + jnp.log(l_sc[h])))
                qb = qa_ref[:, hs].astype(F32)
                for n, piece in enumerate(pieces):
                    qb = jnp.where(lane == AUG_LSE + n, piece, qb)
                qb_ref[:, hs] = qb.astype(BF16)

        if comm:
            @pl.when((hp == nhp - 1) & (i == nq - 1) & (j == nq - 1))
            def _():
                _comm_wait(_comm_copies(comm.plan, src_refs, dst_refs, *sems))

    outs = pl.pallas_call(
        body, name=name, grid=(nhp, nq, nq),
        in_specs=[pl.BlockSpec((tq, 2 * LANE), lambda h, i, j: (i, h)),
                  pl.BlockSpec((2 * LANE, tk), lambda h, i, j: (h, jnp.minimum(j, i))),
                  pl.BlockSpec((tk, 2 * LANE), lambda h, i, j: (jnp.minimum(j, i), h))]
        + ([ANY] * len(comm.srcs) if comm else []),
        out_specs=[pl.BlockSpec((tq, LANE), lambda h, i, j: (i, h)),
                   pl.BlockSpec((tq, 2 * LANE), lambda h, i, j: (i, h))]
        + ([ANY] * len(comm.out_shapes) if comm else []),
        out_shape=[jax.ShapeDtypeStruct((T, nhp * LANE), F32), jax.ShapeDtypeStruct(qa.shape, BF16)]
        + (list(comm.out_shapes) if comm else []),
        scratch_shapes=[pltpu.VMEM((2, tq, LANE), F32), pltpu.VMEM((2, tq, LANE), F32),
                        pltpu.VMEM((2, tq, LANE), F32)] + (comm.scratch() if comm else []),
        compiler_params=(_cparams("arbitrary", "arbitrary", "arbitrary") if comm
                         else _cparams("parallel", "parallel", "arbitrary")),
    )(qa, kta, va, *(comm.srcs if comm else []))
    return outs[0], outs[1], outs[2:]


def _fox_attn_bwd(qb, qta, ka, kta, vta, doa, dota, *, name, comm=None):
    T = qb.shape[0]
    nhp = qb.shape[1] // (2 * LANE)
    tq = tk = _attn_tile(T)
    nq = T // tq

    def body(*refs):
        ((qb_ref, qta_ref, ka_ref, kta_ref, vta_ref, doa_ref, dota_ref), src_refs,
         (dq_ref, dkt_ref, dvt_ref, dcol_ref, drow_ref), dst_refs, (dkt_sc, dvt_sc, dcol_sc), sems) = (
            _split_comm_refs(refs, 7, 5, 3, comm))
        hp, j, i = pl.program_id(0), pl.program_id(1), pl.program_id(2)
        if comm:
            @pl.when((hp == 0) & (j == 0) & (i == 0))
            def _():
                _comm_start(_comm_copies(comm.plan, src_refs, dst_refs, *sems))

        @pl.when((j == 0) & (i == 0))
        def _():
            dq_ref[...] = jnp.zeros(dq_ref.shape, F32)
            drow_ref[...] = jnp.zeros(drow_ref.shape, F32)

        @pl.when(i == 0)
        def _():
            dkt_sc[...] = jnp.zeros(dkt_sc.shape, F32)
            dvt_sc[...] = jnp.zeros(dvt_sc.shape, F32)
            dcol_sc[...] = jnp.zeros(dcol_sc.shape, F32)

        def block(diagonal):
            rows = pl.ds(pl.multiple_of(i * tq, tq), tq)
            for h in range(2):
                hs = slice(h * LANE, (h + 1) * LANE)
                p = jnp.exp(lax.dot_general(qb_ref[:, hs], kta_ref[hs, :], _NN, preferred_element_type=F32))
                if diagonal:
                    p = jnp.where(_causal(tq, tk), p, 0.0)
                dl = p * lax.dot_general(doa_ref[:, hs], vta_ref[hs, :], _NN, preferred_element_type=F32)
                dlb = dl.astype(BF16)
                dvt_sc[h] += lax.dot_general(dota_ref[hs, :], p.astype(BF16), _NN, preferred_element_type=F32)
                dkt_sc[h] += lax.dot_general(qta_ref[hs, :], dlb, _NN, preferred_element_type=F32)
                dq_ref[rows, hs] += lax.dot_general(dlb, ka_ref[:, hs], _NN, preferred_element_type=F32)
                dcol_sc[h] += jnp.sum(dl, axis=0, keepdims=True)
                drow_ref[h, rows, :] += jnp.sum(dl, axis=1, keepdims=True)

        @pl.when(i > j)
        def _():
            block(False)

        @pl.when(i == j)
        def _():
            block(True)

        @pl.when(i == nq - 1)
        def _():
            dkt_ref[...] = jnp.concatenate([dkt_sc[0], dkt_sc[1]], axis=0)
            dvt_ref[...] = jnp.concatenate([dvt_sc[0], dvt_sc[1]], axis=0)
            dcol_ref[...] = dcol_sc[...]

        if comm:
            @pl.when((hp == nhp - 1) & (j == nq - 1) & (i == nq - 1))
            def _():
                _comm_wait(_comm_copies(comm.plan, src_refs, dst_refs, *sems))

    qrow = pl.BlockSpec((tq, 2 * LANE), lambda h, j, i: (jnp.maximum(i, j), h))
    qcol = pl.BlockSpec((2 * LANE, tq), lambda h, j, i: (h, jnp.maximum(i, j)))
    krow = pl.BlockSpec((tk, 2 * LANE), lambda h, j, i: (j, h))
    kcol = pl.BlockSpec((2 * LANE, tk), lambda h, j, i: (h, j))
    tall = jax.ShapeDtypeStruct((qb.shape[1], T), F32)
    outs = pl.pallas_call(
        body, name=name, grid=(nhp, nq, nq),
        in_specs=[qrow, qcol, krow, kcol, kcol, qrow, qcol] + ([ANY] * len(comm.srcs) if comm else []),
        out_specs=[pl.BlockSpec((T, 2 * LANE), lambda h, j, i: (0, h)), kcol, kcol,
                   pl.BlockSpec((2, 1, tk), lambda h, j, i: (h, 0, j)),
                   pl.BlockSpec((2, T, 1), lambda h, j, i: (h, 0, 0))]
        + ([ANY] * len(comm.out_shapes) if comm else []),
        out_shape=[jax.ShapeDtypeStruct(qb.shape, F32), tall, tall,
                   jax.ShapeDtypeStruct((2 * nhp, 1, T), F32), jax.ShapeDtypeStruct((2 * nhp, T, 1), F32)]
        + (list(comm.out_shapes) if comm else []),
        scratch_shapes=[pltpu.VMEM((2, LANE, tk), F32), pltpu.VMEM((2, LANE, tk), F32),
                        pltpu.VMEM((2, 1, tk), F32)] + (comm.scratch() if comm else []),
        compiler_params=_cparams("arbitrary" if comm else "parallel", "arbitrary", "arbitrary"),
    )(qb, qta, ka, kta, vta, doa, dota, *(comm.srcs if comm else []))
    return (*outs[:5], outs[5:])


_GELU_C = math.sqrt(2.0 / math.pi)
_GELU_A = 0.044715


def _gelu(x):
    t = jnp.tanh(_GELU_C * (x + _GELU_A * (x * x * x)))
    return x * (0.5 * (1.0 + t)), t


def _gelu_grad(x, t):
    return 0.5 * (1.0 + t) + 0.5 * x * (1.0 - t * t) * (_GELU_C * (1.0 + 3.0 * _GELU_A * x * x))


def _layer_norm_stats(v):
    mu = jnp.mean(v, axis=-1, keepdims=True)
    vc = v - mu
    rstd = lax.rsqrt(jnp.mean(vc * vc, axis=-1, keepdims=True) + EPS)
    return vc * rstd, rstd


def _layer_norm_bwd(dyhat, yhat, rstd):
    return rstd * (dyhat - jnp.mean(dyhat, axis=-1, keepdims=True)
                   - yhat * jnp.mean(dyhat * yhat, axis=-1, keepdims=True))


def _sg_mask():
    t = lax.broadcasted_iota(jnp.int32, (SG_CHUNK, SG_CHUNK), 0) // SG_CAUSAL
    s = lax.broadcasted_iota(jnp.int32, (SG_CHUNK, SG_CHUNK), 1) // SG_CAUSAL
    return s <= t


def _sg_mix(ws_ref, bc_ref, vln_sc, vo_sc, tr, gd):
    mask = _sg_mask()
    for g in range(SG_GROUPS):
        wg = jnp.where(mask, ws_ref[g], 0.0).astype(BF16)
        cols = slice(g * gd, (g + 1) * gd)
        for n in range(tr // SG_CHUNK):
            rows = slice(n * SG_CHUNK, (n + 1) * SG_CHUNK)
            vo_sc[rows, cols] = lax.dot_general(wg, vln_sc[rows, cols], _NN,
                                                preferred_element_type=F32) + bc_ref[g]


def _sg_fwd(a_uv, ln_g, ln_b, ws, bcol, *, name):
    T, W = a_uv.shape[0], a_uv.shape[1] // 2
    gd = W // SG_GROUPS
    tr = _row_tile(T)

    def body(u_ref, v_ref, g_ref, b_ref, ws_ref, bc_ref, o_ref, vln_sc, vo_sc):
        u, _ = _gelu(u_ref[...])
        v, _ = _gelu(v_ref[...])
        vhat, _ = _layer_norm_stats(v)
        vln_sc[...] = (vhat * g_ref[...] + b_ref[...]).astype(BF16)
        _sg_mix(ws_ref, bc_ref, vln_sc, vo_sc, tr, gd)
        o_ref[...] = (u * vo_sc[...]).astype(BF16)

    row = pl.BlockSpec((1, W), lambda i: (0, 0))
    return pl.pallas_call(
        body, name=name, grid=(T // tr,),
        in_specs=[pl.BlockSpec((tr, W), lambda i: (i, 0)), pl.BlockSpec((tr, W), lambda i: (i, 1)), row, row,
                  pl.BlockSpec((SG_GROUPS, SG_CHUNK, SG_CHUNK), lambda i: (0, 0, 0)),
                  pl.BlockSpec((SG_GROUPS, SG_CHUNK, 1), lambda i: (0, 0, 0))],
        out_specs=pl.BlockSpec((tr, W), lambda i: (i, 0)),
        out_shape=jax.ShapeDtypeStruct((T, W), BF16),
        scratch_shapes=[pltpu.VMEM((tr, W), BF16), pltpu.VMEM((tr, W), F32)],
        compiler_params=_cparams("parallel"),
    )(a_uv, a_uv, ln_g, ln_b, ws, bcol)


def _sg_bwd(a_uv, dgate, ln_g, ln_b, ws, bcol, *, name):
    T, W = a_uv.shape[0], a_uv.shape[1] // 2
    gd = W // SG_GROUPS
    tr = _row_tile(T)

    def body(u_ref, v_ref, dg_ref, g_ref, b_ref, ws_ref, bc_ref,
             da_ref, dws_ref, dbs_ref, sums_ref, vln_sc, vo_sc, dvo_sc, dvln_sc):
        i = pl.program_id(0)

        @pl.when(i == 0)
        def _():
            dws_ref[...] = jnp.zeros(dws_ref.shape, F32)
            dbs_ref[...] = jnp.zeros(dbs_ref.shape, F32)
            sums_ref[...] = jnp.zeros(sums_ref.shape, F32)

        ua, va = u_ref[...], v_ref[...]
        u, tu = _gelu(ua)
        v, tv = _gelu(va)
        vhat, rstd = _layer_norm_stats(v)
        vln_sc[...] = (vhat * g_ref[...] + b_ref[...]).astype(BF16)
        _sg_mix(ws_ref, bc_ref, vln_sc, vo_sc, tr, gd)
        dgt = dg_ref[...]
        du = dgt * vo_sc[...]
        dvo_sc[...] = dgt * u
        mask = _sg_mask()
        for g in range(SG_GROUPS):
            wg = jnp.where(mask, ws_ref[g], 0.0).astype(BF16)
            cols = slice(g * gd, (g + 1) * gd)
            acc_w = jnp.zeros((SG_CHUNK, SG_CHUNK), F32)
            acc_b = jnp.zeros((SG_CHUNK, 1), F32)
            for n in range(tr // SG_CHUNK):
                rows = slice(n * SG_CHUNK, (n + 1) * SG_CHUNK)
                dvo = dvo_sc[rows, cols]
                dvob = dvo.astype(BF16)
                dvln_sc[rows, cols] = lax.dot_general(wg, dvob, _TN, preferred_element_type=F32)
                acc_w += lax.dot_general(dvob, vln_sc[rows, cols], _NT, preferred_element_type=F32)
                acc_b += jnp.sum(dvo, axis=1, keepdims=True)
            dws_ref[g] += jnp.where(mask, acc_w, 0.0)
            dbs_ref[g] += acc_b
        dvln = dvln_sc[...]
        sums_ref[...] += jnp.concatenate([jnp.sum(dvln * vhat, axis=0, keepdims=True),
                                          jnp.sum(dvln, axis=0, keepdims=True),
                                          jnp.zeros((6, W), F32)], axis=0)
        dv = _layer_norm_bwd(dvln * g_ref[...], vhat, rstd)
        da_ref[:, :W] = (du * _gelu_grad(ua, tu)).astype(BF16)
        da_ref[:, W:] = (dv * _gelu_grad(va, tv)).astype(BF16)

    row = pl.BlockSpec((1, W), lambda i: (0, 0))
    wspec = pl.BlockSpec((SG_GROUPS, SG_CHUNK, SG_CHUNK), lambda i: (0, 0, 0))
    bspec = pl.BlockSpec((SG_GROUPS, SG_CHUNK, 1), lambda i: (0, 0, 0))
    return pl.pallas_call(
        body, name=name, grid=(T // tr,),
        in_specs=[pl.BlockSpec((tr, W), lambda i: (i, 0)), pl.BlockSpec((tr, W), lambda i: (i, 1)),
                  pl.BlockSpec((tr, W), lambda i: (i, 0)), row, row, wspec, bspec],
        out_specs=[pl.BlockSpec((tr, 2 * W), lambda i: (i, 0)), wspec, bspec,
                   pl.BlockSpec((8, W), lambda i: (0, 0))],
        out_shape=[jax.ShapeDtypeStruct((T, 2 * W), BF16),
                   jax.ShapeDtypeStruct((SG_GROUPS, SG_CHUNK, SG_CHUNK), F32),
                   jax.ShapeDtypeStruct((SG_GROUPS, SG_CHUNK, 1), F32),
                   jax.ShapeDtypeStruct((8, W), F32)],
        scratch_shapes=[pltpu.VMEM((tr, W), BF16), pltpu.VMEM((tr, W), F32),
                        pltpu.VMEM((tr, W), F32), pltpu.VMEM((tr, W), F32)],
        compiler_params=_cparams("arbitrary"),
    )(a_uv, a_uv, dgate, ln_g, ln_b, ws, bcol)


SUBLANES = 8


def _shift_rows(xc_sc, xs_sc):
    rows = xs_sc.shape[1]
    for p in range(1, SUBLANES):
        xs_sc[p - 1] = xc_sc[pl.ds(p, rows), :]


def _rows_at(xc_sc, xs_sc, offset, tr):
    p = offset % SUBLANES
    base = offset - p
    return xc_sc[pl.ds(base, tr), :] if p == 0 else xs_sc[p - 1, pl.ds(base, tr), :]


def _shift_scratch(tr, C):
    return pltpu.VMEM((SUBLANES - 1, tr + CONV_HALO - SUBLANES, C), F32)


def _cv_glu_conv(a_ref, b_ref, ap_ref, bp_ref, w_ref, bd_ref, xc_sc, xs_sc, tr):
    i = pl.program_id(0)
    prev = ap_ref[...] * jax.nn.sigmoid(bp_ref[...])
    xc_sc[0:CONV_HALO, :] = jnp.where(i > 0, prev, 0.0)
    xc_sc[CONV_HALO:, :] = a_ref[...] * jax.nn.sigmoid(b_ref[...])
    _shift_rows(xc_sc, xs_sc)
    acc = jnp.broadcast_to(bd_ref[...], (tr, bd_ref.shape[1]))
    for k in range(CONV_WIDTH):
        acc = acc + w_ref[k:k + 1, :] * _rows_at(xc_sc, xs_sc, CONV_HALO - (CONV_WIDTH - 1) + k, tr)
    return acc


def _cv_specs(T, C, tr):
    hb = tr // CONV_HALO
    cur = lambda col: pl.BlockSpec((tr, C), lambda i: (i, col))
    prev = lambda col: pl.BlockSpec((CONV_HALO, C), lambda i: (jnp.maximum(i * hb - 1, 0), col))
    row = pl.BlockSpec((1, C), lambda i: (0, 0))
    wspec = pl.BlockSpec((CONV_HALO, C), lambda i: (0, 0))
    return cur, prev, row, wspec


def _cv_fwd(p, w_dw, b_dw, ln_g, ln_b, *, name):
    T, C = p.shape[0], p.shape[1] // 2
    tr = _row_tile(T)
    cur, prev, row, wspec = _cv_specs(T, C, tr)

    def body(a_ref, b_ref, ap_ref, bp_ref, w_ref, bd_ref, g_ref, be_ref, o_ref, xc_sc, xs_sc):
        y2 = _cv_glu_conv(a_ref, b_ref, ap_ref, bp_ref, w_ref, bd_ref, xc_sc, xs_sc, tr)
        yhat, _ = _layer_norm_stats(y2)
        yln = yhat * g_ref[...] + be_ref[...]
        o_ref[...] = (yln * jax.nn.sigmoid(yln)).astype(BF16)

    return pl.pallas_call(
        body, name=name, grid=(T // tr,),
        in_specs=[cur(0), cur(1), prev(0), prev(1), wspec, row, row, row],
        out_specs=pl.BlockSpec((tr, C), lambda i: (i, 0)),
        out_shape=jax.ShapeDtypeStruct((T, C), BF16),
        scratch_shapes=[pltpu.VMEM((tr + CONV_HALO, C), F32), _shift_scratch(tr, C)],
        compiler_params=_cparams("parallel"),
    )(p, p, p, p, w_dw, b_dw, ln_g, ln_b)


def _cv_bwd_ln(p, dy3, w_dw, b_dw, ln_g, ln_b, *, name):
    T, C = p.shape[0], p.shape[1] // 2
    tr = _row_tile(T)
    cur, prev, row, wspec = _cv_specs(T, C, tr)

    def body(a_ref, b_ref, ap_ref, bp_ref, dy_ref, w_ref, bd_ref, g_ref, be_ref,
             dy2_ref, dw_ref, sums_ref, xc_sc, xs_sc):
        i = pl.program_id(0)
        y2 = _cv_glu_conv(a_ref, b_ref, ap_ref, bp_ref, w_ref, bd_ref, xc_sc, xs_sc, tr)
        yhat, rstd = _layer_norm_stats(y2)
        yln = yhat * g_ref[...] + be_ref[...]
        s = jax.nn.sigmoid(yln)
        dyln = dy_ref[...] * (s + yln * s * (1.0 - s))
        dy2 = _layer_norm_bwd(dyln * g_ref[...], yhat, rstd)
        dy2_ref[...] = dy2
        sums = jnp.concatenate([jnp.sum(dy2, axis=0, keepdims=True),
                                jnp.sum(dyln * yhat, axis=0, keepdims=True),
                                jnp.sum(dyln, axis=0, keepdims=True),
                                jnp.zeros((5, C), F32)], axis=0)
        taps = [jnp.sum(dy2 * _rows_at(xc_sc, xs_sc, CONV_HALO - (CONV_WIDTH - 1) + k, tr), axis=0, keepdims=True)
                for k in range(CONV_WIDTH)]
        dw = jnp.concatenate(taps + [jnp.zeros((CONV_HALO - CONV_WIDTH, C), F32)], axis=0)

        @pl.when(i == 0)
        def _():
            sums_ref[...] = sums
            dw_ref[...] = dw

        @pl.when(i > 0)
        def _():
            sums_ref[...] += sums
            dw_ref[...] += dw

    blk = pl.BlockSpec((tr, C), lambda i: (i, 0))
    return pl.pallas_call(
        body, name=name, grid=(T // tr,),
        in_specs=[cur(0), cur(1), prev(0), prev(1), blk, wspec, row, row, row],
        out_specs=[blk, wspec, pl.BlockSpec((8, C), lambda i: (0, 0))],
        out_shape=[jax.ShapeDtypeStruct((T, C), F32), jax.ShapeDtypeStruct((CONV_HALO, C), F32),
                   jax.ShapeDtypeStruct((8, C), F32)],
        scratch_shapes=[pltpu.VMEM((tr + CONV_HALO, C), F32), _shift_scratch(tr, C)],
        compiler_params=_cparams("arbitrary"),
    )(p, p, p, p, dy3, w_dw, b_dw, ln_g, ln_b)


def _cv_bwd_in(p, dy2, w_dw, *, name):
    T, C = p.shape[0], p.shape[1] // 2
    tr = _row_tile(T)
    hb = tr // CONV_HALO
    nblk = T // tr
    last_halo = T // CONV_HALO - 1

    def body(a_ref, b_ref, dy_ref, dyn_ref, w_ref, dp_ref, sums_ref, xc_sc, xs_sc):
        i = pl.program_id(0)
        xc_sc[0:tr, :] = dy_ref[...]
        xc_sc[tr:, :] = jnp.where(i < nblk - 1, dyn_ref[...], 0.0)
        _shift_rows(xc_sc, xs_sc)
        dy1 = jnp.zeros((tr, C), F32)
        for k in range(CONV_WIDTH):
            dy1 = dy1 + w_ref[k:k + 1, :] * _rows_at(xc_sc, xs_sc, CONV_WIDTH - 1 - k, tr)
        a = a_ref[...]
        sb = jax.nn.sigmoid(b_ref[...])
        da = dy1 * sb
        db = dy1 * a * sb * (1.0 - sb)
        dp_ref[:, :C] = da.astype(BF16)
        dp_ref[:, C:] = db.astype(BF16)
        sums = jnp.concatenate([
            jnp.concatenate([jnp.sum(da, axis=0, keepdims=True), jnp.sum(db, axis=0, keepdims=True)], axis=1),
            jnp.zeros((7, 2 * C), F32)], axis=0)

        @pl.when(i == 0)
        def _():
            sums_ref[...] = sums

        @pl.when(i > 0)
        def _():
            sums_ref[...] += sums

    blk = lambda col: pl.BlockSpec((tr, C), lambda i: (i, col))
    return pl.pallas_call(
        body, name=name, grid=(nblk,),
        in_specs=[blk(0), blk(1), blk(0),
                  pl.BlockSpec((CONV_HALO, C), lambda i: (jnp.minimum((i + 1) * hb, last_halo), 0)),
                  pl.BlockSpec((CONV_HALO, C), lambda i: (0, 0))],
        out_specs=[pl.BlockSpec((tr, 2 * C), lambda i: (i, 0)), pl.BlockSpec((8, 2 * C), lambda i: (0, 0))],
        out_shape=[jax.ShapeDtypeStruct((T, 2 * C), BF16), jax.ShapeDtypeStruct((8, 2 * C), F32)],
        scratch_shapes=[pltpu.VMEM((tr + CONV_HALO, C), F32), _shift_scratch(tr, C)],
        compiler_params=_cparams("arbitrary"),
    )(p, p, dy2, dy2, w_dw)


def _col_tile(n, want=1024):
    best = LANE
    for t in range(LANE, min(n, want) + 1, LANE):
        if n % t == 0:
            best = t
    return best if n % LANE == 0 else n


def _mm(a, b, *, name, ta=False, tb=False, **kw):
    M = a.shape[1] if ta else a.shape[0]
    N = b.shape[0] if tb else b.shape[1]
    K = a.shape[0] if ta else a.shape[1]
    kw.setdefault('tm', _col_tile(M, 1024 if ta else 512))
    kw.setdefault('tn', _col_tile(N, 1024))
    kw.setdefault('tk', K if tb else _col_tile(K, 1024))
    return _matmul(a, b, name=name, ta=ta, tb=tb, **kw)


def _relu2_epilogue(acc):
    r = jnp.maximum(acc, 0.0)
    return acc, r * r


def _residual_epilogue(acc, x, g):
    return acc, x + g * acc


def _residual_bias_epilogue(acc, x, g, b):
    y = acc + b
    return y, x + g * y


def _relu2_bwd_epilogue(acc, a):
    return (acc * (2.0 * jnp.maximum(a, 0.0)),)


def _bias_epilogue(acc, b):
    return (acc + b,)


def _fox_forward(h1, P, j, D, comm=None):
    H = D // HEAD_DIM
    proj = _mm(h1, P['fox_w_in'][j], name='fox_proj', b_outer=True)
    qg = jnp.tile(P['fox_q_norm'][j][None, :], (1, 2))
    kg = jnp.tile(P['fox_k_norm'][j][None, :], (1, 2))
    fpre_t = proj[:, 3 * D:3 * D + H].T
    bf = P['fox_b_f'][j][:, None]
    fcum = _fox_gate_fwd(fpre_t, bf, name='fox_gate_fwd')
    qa, qta, ka, kta, va, vta = _fox_prep_fwd(proj, qg, kg, fcum[:, :, None], d_model=D, name='fox_prep_fwd')
    o, qb, comm_outs = _fox_attn_fwd(qa, kta, va, name='fox_attn_fwd', comm=comm)
    saved = dict(proj=proj, qg=qg, kg=kg, fpre_t=fpre_t, bf=bf, o=o, qb=qb, qta=qta, ka=ka, kta=kta, vta=vta)
    return o, saved, comm_outs


def _fox_backward(dy, h1, S, P, j, D, comm=None):
    H = D // HEAD_DIM
    w_out, w_in = P['fox_w_out'][j], P['fox_w_in'][j]
    g = {}
    g['fox_w_out'] = _mm(S['o'], dy, ta=True, name='fox_dw_out')
    do = _mm(dy, w_out, tb=True, name='fox_do')
    doa, dota = _fox_do_prep(do, S['o'], name='fox_do_prep')
    dq, dkt, dvt, dcol, drow, comm_outs = _fox_attn_bwd(S['qb'], S['qta'], S['ka'], S['kta'], S['vta'], doa, dota,
                                                        name='fox_attn_bwd', comm=comm)
    dqp, dkp, dvp, gsum = _fox_prep_bwd(S['proj'], dq, dkt, dvt, S['qg'], S['kg'], d_model=D, name='fox_prep_bwd')
    dfpre_t, dbf = _fox_gate_bwd(dcol[:, 0, :], drow[:, :, 0], S['fpre_t'], S['bf'], name='fox_gate_bwd')
    dfpre = jnp.pad(dfpre_t.T.astype(BF16), ((0, 0), (0, LANE - H)))
    dproj = jnp.concatenate([dqp, dkp, dvp, dfpre], axis=1)
    g['fox_w_in'] = _mm(h1, dproj, ta=True, name='fox_dw_in')[:, :3 * D + H]
    g['fox_b_f'] = dbf[:, 0]
    g['fox_q_norm'] = gsum[0, :HEAD_DIM]
    g['fox_k_norm'] = gsum[1, :HEAD_DIM]
    dh1 = _mm(dproj, w_in, tb=True, name='fox_dh')
    return dh1, g, comm_outs


def _sg_forward(h1, P, D):
    a_uv = _mm(h1, P['sg_w_in'], name='sg_in', b_outer=True)
    bcol = P['sg_b_s'][:, :, None]
    gate = _sg_fwd(a_uv, P['sg_ln_g'], P['sg_ln_b'], P['sg_w_s'], bcol, name='sg_fwd')
    return gate, dict(a_uv=a_uv, bcol=bcol, gate=gate)


def _sg_backward(dy, h1, S, P, D):
    g = {}
    g['sg_w_out'] = _mm(S['gate'], dy, ta=True, name='sg_dw_out')
    dgate = _mm(dy, P['sg_w_out'], tb=True, name='sg_dgate')
    da, dws, dbs, sums = _sg_bwd(S['a_uv'], dgate, P['sg_ln_g'], P['sg_ln_b'], P['sg_w_s'], S['bcol'],
                                 name='sg_bwd')
    g['sg_w_s'], g['sg_b_s'] = dws, dbs[:, :, 0]
    g['sg_ln_g'], g['sg_ln_b'] = sums[0], sums[1]
    g['sg_w_in'] = _mm(h1, da, ta=True, name='sg_dw_in')
    dh1 = _mm(da, P['sg_w_in'], tb=True, name='sg_dh')
    return dh1, g


def _cv_forward(h1, P, D):
    p = _mm(h1, P['cv_w_pw1'], name='cv_pw1', extras=[(P['cv_b_pw1'], 'row')], epilogue=_bias_epilogue,
            b_outer=True)
    w_dw = jnp.pad(P['cv_w_dw'], ((0, CONV_HALO - CONV_WIDTH), (0, 0)))
    y3 = _cv_fwd(p, w_dw, P['cv_b_dw'], P['cv_ln_g'], P['cv_ln_b'], name='cv_fwd')
    return y3, dict(p=p, w_dw=w_dw, y3=y3)


def _cv_backward(dy, h1, S, P, D):
    g = {}
    g['cv_w_pw2'] = _mm(S['y3'], dy, ta=True, name='cv_dw_pw2')
    dy3 = _mm(dy, P['cv_w_pw2'], tb=True, name='cv_dy3')
    dy2, dw, sums = _cv_bwd_ln(S['p'], dy3, S['w_dw'], P['cv_b_dw'], P['cv_ln_g'], P['cv_ln_b'], name='cv_bwd_ln')
    g['cv_w_dw'] = dw[:CONV_WIDTH]
    g['cv_b_dw'], g['cv_ln_g'], g['cv_ln_b'] = sums[0], sums[1], sums[2]
    dp, psum = _cv_bwd_in(S['p'], dy2, S['w_dw'], name='cv_bwd_in')
    g['cv_b_pw1'] = psum[0]
    g['cv_w_pw1'] = _mm(h1, dp, ta=True, name='cv_dw_pw1')
    dh1 = _mm(dp, P['cv_w_pw1'], tb=True, name='cv_dh')
    return dh1, g


class Hooks(NamedTuple):
    fwd_comm: Comm
    fwd_done: Callable
    bwd_comm: Callable
    bwd_done: Callable


def _local_step(x, target, mod, P, hooks=None):
    T, D = x.shape
    L = mod.shape[0]
    saved = []
    for i in range(L):
        kind, j = i % N_MIXERS, i // N_MIXERS
        m = [mod[i:i + 1, k * D:(k + 1) * D] for k in range(6)]
        sh_m, sc_m, g_m, sh_f, sc_f, g_f = m
        w_mix, w_mlp = P['norm_mix'][i:i + 1], P['norm_mlp'][i:i + 1]
        h1 = _norm_mod_fwd(x, w_mix, sc_m, sh_m, name='norm_mix_fwd')
        if kind == 0:
            carried = hooks is not None and i == 0
            op, S, comm_outs = _fox_forward(h1, P, j, D, comm=hooks.fwd_comm if carried else None)
            if carried:
                hooks.fwd_done(comm_outs)
            y, x1 = _mm(op, P['fox_w_out'][j], name='fox_out', extras=[(x, 'tile'), (g_m, 'row')],
                        epilogue=_residual_epilogue, out_dtypes=(F32, F32))
        elif kind == 1:
            op, S = _sg_forward(h1, P, D)
            y, x1 = _mm(op, P['sg_w_out'], name='sg_out', extras=[(x, 'tile'), (g_m, 'row')],
                        epilogue=_residual_epilogue, out_dtypes=(F32, F32))
        else:
            op, S = _cv_forward(h1, P, D)
            y, x1 = _mm(op, P['cv_w_pw2'], name='cv_out',
                        extras=[(x, 'tile'), (g_m, 'row'), (P['cv_b_pw2'], 'row')],
                        epilogue=_residual_bias_epilogue, out_dtypes=(F32, F32))
        h2 = _norm_mod_fwd(x1, w_mlp, sc_f, sh_f, name='norm_mlp_fwd')
        a, r = _mm(h2, P['w_mlp_in'][i], name='mlp_in', epilogue=_relu2_epilogue, out_dtypes=(F32, BF16),
                   b_outer=True)
        z, x2 = _mm(r, P['w_mlp_out'][i], name='mlp_out', extras=[(x1, 'tile'), (g_f, 'row')],
                    epilogue=_residual_epilogue, out_dtypes=(F32, F32), tk=P['w_mlp_out'][i].shape[0])
        saved.append(dict(x=x, h1=h1, S=S, y=y, x1=x1, h2=h2, a=a, r=r, z=z, m=m))
        x = x2

    loss_part, dx = _loss_head(x, target, name='loss_head')

    grads = {k: [None] * L for k in ('norm_mix', 'norm_mlp')}
    mix_grads, mat = {}, {}
    dmod = [None] * L
    for i in reversed(range(L)):
        kind, j = i % N_MIXERS, i // N_MIXERS
        sv = saved[i]
        sh_m, sc_m, g_m, sh_f, sc_f, g_f = sv['m']
        w_mix, w_mlp = P['norm_mix'][i:i + 1], P['norm_mlp'][i:i + 1]
        dz, dgf = _gate_bwd(dx, sv['z'], g_f, name='mlp_gate_bwd')
        mat['w_mlp_out', i] = _mm(sv['r'], dz, ta=True, name='mlp_dw_out')
        da = _mm(dz, P['w_mlp_out'][i], tb=True, name='mlp_da', extras=[(sv['a'], 'tile')],
                 epilogue=_relu2_bwd_epilogue, out_dtypes=(BF16,), b_outer=True)
        mat['w_mlp_in', i] = _mm(sv['h2'], da, ta=True, name='mlp_dw_in')
        dh2 = _mm(da, P['w_mlp_in'][i], tb=True, name='mlp_dh', tk=P['w_mlp_in'][i].shape[1])
        dx1, sums_f, dy = _norm_mod_bwd(dh2, sv['x1'], dx, w_mlp, sc_f, name='norm_mlp_bwd', gate=(sv['y'], g_m))
        if kind == 0:
            carried = hooks is not None and i == 0
            dh1, g, comm_outs = _fox_backward(dy, sv['h1'], sv['S'], P, j, D,
                                              comm=hooks.bwd_comm(mat) if carried else None)
            if carried:
                hooks.bwd_done(comm_outs)
        elif kind == 1:
            dh1, g = _sg_backward(dy, sv['h1'], sv['S'], P, D)
        else:
            dh1, g = _cv_backward(dy, sv['h1'], sv['S'], P, D)
            g['cv_b_pw2'] = sums_f[4]
        for k, val in g.items():
            if k in BIG:
                mat[k, j] = val
            else:
                mix_grads.setdefault(k, {})[j] = val
        dx, sums_m = _norm_mod_bwd(dh1, sv['x'], dx1, w_mix, sc_m, name='norm_mix_bwd')
        grads['norm_mlp'][i], grads['norm_mix'][i] = sums_f[2], sums_m[2]
        dmod[i] = jnp.concatenate([sums_m[0], sums_m[1], sums_f[3], sums_f[0], sums_f[1], dgf[0]])

    out = {k: jnp.stack(v) for k, v in grads.items()}
    for k, per_j in mix_grads.items():
        out[k] = jnp.stack([per_j[j] for j in sorted(per_j)])
    return loss_part, dx, jnp.stack(dmod), out, mat


def _all_gather8(blocks, *, name):
    n = len(blocks)

    def body(*refs):
        x_refs, out_refs = refs[:n], refs[n:2 * n]
        send_sems, recv_sems, local_sems = refs[2 * n:]
        x, y, c = _position()
        me, sibling = (x, y, c), (x, y, 1 - c)
        chips = [(1 - x, y), (x, 1 - y), (1 - x, 1 - y)]

        def slot(a, px, py, pc):
            return out_refs[a].at[4 * px + 2 * py + pc]

        def copy(a, k, blk, to, src=None):
            return pltpu.make_async_remote_copy(
                src_ref=slot(a, *blk) if src is None else src, dst_ref=slot(a, *blk),
                send_sem=send_sems.at[7 * a + k], recv_sem=recv_sems.at[7 * a + k],
                device_id=to, device_id_type=MESH)

        mine = [pltpu.make_async_copy(x_refs[a], slot(a, *me), local_sems.at[a]) for a in range(n)]
        for cp in mine:
            cp.start()
        first = []
        for j, chip in enumerate(chips):
            first += [copy(a, 1 + j, me, (*chip, c), src=x_refs[a]) for a in range(n)]
        first += [copy(a, 0, me, sibling, src=x_refs[a]) for a in range(n)]
        for cp in first:
            cp.start()
        passed = []
        for j, chip in enumerate(chips):
            for a in range(n):
                copy(a, 1 + j, (*chip, c), me).wait_recv()
                passed.append(copy(a, 4 + j, (*chip, c), sibling))
                passed[-1].start()
        for a in range(n):
            copy(a, 0, sibling, me).wait_recv()
        for j, chip in enumerate(chips):
            for a in range(n):
                copy(a, 4 + j, (*chip, 1 - c), me).wait_recv()
        for cp in first + passed:
            cp.wait_send()
        for cp in mine:
            cp.wait()

    return pl.pallas_call(
        body, name=name, in_specs=[ANY] * n, out_specs=[ANY] * n,
        out_shape=[jax.ShapeDtypeStruct((8,) + b.shape, b.dtype) for b in blocks],
        scratch_shapes=[pltpu.SemaphoreType.DMA((7 * n,)), pltpu.SemaphoreType.DMA((7 * n,)),
                        pltpu.SemaphoreType.DMA((n,))],
    )(*blocks)


def _exchange(comm, *, name, aliases=None):
    ns, no = len(comm.srcs), len(comm.out_shapes)

    def body(*refs):
        copies = _comm_copies(comm.plan, refs[:ns], refs[ns:ns + no], *refs[ns + no:])
        _comm_start(copies)
        _comm_wait(copies)

    return pl.pallas_call(
        body, name=name, in_specs=[ANY] * ns, out_specs=[ANY] * no, out_shape=list(comm.out_shapes),
        scratch_shapes=comm.scratch(), input_output_aliases=aliases or {},
    )(*comm.srcs)


def _gather_comm(halves):
    n = len(halves)

    def plan(src, out, x, y, c):
        mine = 4 * x + 2 * y + c
        remote = [(src[a], out[a].at[mine], (x, y, 1 - c), out[a].at[4 * x + 2 * y + 1 - c]) for a in range(n)]
        for fx, fy in CHIP_FLIPS:
            px, py = _flip(x, fx), _flip(y, fy)
            remote += [(src[a], out[a].at[mine], (px, py, c), out[a].at[4 * px + 2 * py + c]) for a in range(n)]
        return remote, [(src[a], out[a].at[mine]) for a in range(n)]

    return Comm(list(halves), [jax.ShapeDtypeStruct((8,) + h.shape, h.dtype) for h in halves], plan, 4 * n, n)


def _gather_forward(bufs, *, name):
    n = len(bufs)

    def plan(src, out, x, y, c):
        remote = []
        for fx, fy in CHIP_FLIPS:
            px, py = _flip(x, fx), _flip(y, fy)
            remote += [(src[a].at[4 * px + 2 * py + c], out[a].at[4 * px + 2 * py + c], (x, y, 1 - c),
                        out[a].at[4 * px + 2 * py + 1 - c]) for a in range(n)]
        return remote, []

    comm = Comm(list(bufs), [jax.ShapeDtypeStruct(b.shape, b.dtype) for b in bufs], plan, 3 * n, 0)
    return _exchange(comm, name=name, aliases={a: a for a in range(n)})


CHIP_FLIPS = ((1, 0), (0, 1), (1, 1))


def _flip(v, f):
    return 1 - v if f else v


def _sum_rows_tile(R, C, budget=3 << 20):
    best = None
    for t in range(8, R + 1, 8):
        if R % t == 0 and t * C * 4 <= budget:
            best = t
    return best if best is not None else R


def _rs_begin(gps, *, wire_dtype):
    n = len(gps)
    c_arr = jnp.reshape(_position()[2], (1,)).astype(jnp.int32)

    def plan(src, out, x, y, c):
        return [(src[a].at[b, 1 - c], out[a].at[b], (x, y, 1 - c), out[a].at[b])
                for a in range(n) for b in range(4)], []

    got1 = _exchange(Comm(list(gps), [jax.ShapeDtypeStruct((4,) + g.shape[2:], F32) for g in gps], plan, 4 * n, 0),
                     name='rs_sibling')

    def sum_chip(c_ref, mine_ref, got_ref, out_ref):
        out_ref[...] = (mine_ref[...] + got_ref[...]).astype(out_ref.dtype)

    parts = []
    for gp, g1 in zip(gps, got1):
        _, _, R, C = gp.shape
        tr = _sum_rows_tile(R, C)
        parts.append(pl.pallas_call(
            sum_chip, name='rs_sum_chip',
            grid_spec=pltpu.PrefetchScalarGridSpec(
                num_scalar_prefetch=1, grid=(4, R // tr),
                in_specs=[pl.BlockSpec((None, None, tr, C), lambda b, r, cr: (b, cr[0], r, 0)),
                          pl.BlockSpec((None, tr, C), lambda b, r, cr: (b, r, 0))],
                out_specs=pl.BlockSpec((None, tr, C), lambda b, r, cr: (b, r, 0))),
            out_shape=jax.ShapeDtypeStruct((4, R, C), wire_dtype),
            compiler_params=_cparams("parallel", "parallel"),
        )(c_arr, gp, g1))
    return got1, parts


def _rs_chips_comm(parts):
    n = len(parts)

    def plan(src, out, x, y, c):
        remote = []
        for k, (fx, fy) in enumerate(CHIP_FLIPS):
            px, py = _flip(x, fx), _flip(y, fy)
            remote += [(src[a].at[2 * px + py], out[a].at[k], (px, py, c), out[a].at[k]) for a in range(n)]
        return remote, []

    return Comm(list(parts), [jax.ShapeDtypeStruct((3,) + p.shape[1:], p.dtype) for p in parts], plan, 3 * n, 0)


def _rs_finish(gps, got1, got2):
    n = len(gps)
    x, y, c = _position()
    bc_arr = jnp.stack([2 * x + y, c]).astype(jnp.int32)

    def sum_final(bc_ref, mine_ref, got1_ref, got2_ref, out_ref):
        acc = mine_ref[...] + got1_ref[...]
        for k in range(3):
            acc = acc + got2_ref[k].astype(F32)
        out_ref[...] = acc

    halves = []
    for gp, g1, g2 in zip(gps, got1, got2):
        _, _, R, C = gp.shape
        tr = _sum_rows_tile(R, C, budget=2 << 20)
        halves.append(pl.pallas_call(
            sum_final, name='rs_sum_final',
            grid_spec=pltpu.PrefetchScalarGridSpec(
                num_scalar_prefetch=1, grid=(R // tr,),
                in_specs=[pl.BlockSpec((None, None, tr, C), lambda r, bc: (bc[0], bc[1], r, 0)),
                          pl.BlockSpec((None, tr, C), lambda r, bc: (bc[0], r, 0)),
                          pl.BlockSpec((3, tr, C), lambda r, bc: (0, r, 0))],
                out_specs=pl.BlockSpec((None, tr, C), lambda r, bc: (bc[1], r, 0))),
            out_shape=jax.ShapeDtypeStruct((2, R, C), F32),
            compiler_params=_cparams("parallel"),
        )(bc_arr, gp, g1, g2))

    def plan(src, out, x, y, c):
        return [(src[a].at[c], out[a].at[c], (x, y, 1 - c), out[a].at[1 - c]) for a in range(n)], []

    comm = Comm(halves, [jax.ShapeDtypeStruct(h.shape, F32) for h in halves], plan, n, 0)
    return _exchange(comm, name='rs_swap', aliases={a: a for a in range(n)})


def _sum8(gathered, *, name):
    _, R, C = gathered.shape

    def body(g_ref, o_ref):
        acc = g_ref[0]
        for k in range(1, 8):
            acc = acc + g_ref[k]
        o_ref[...] = acc

    return pl.pallas_call(body, name=name, out_shape=jax.ShapeDtypeStruct((R, C), F32))(gathered)


def _adamw(w, g, m, v, *, name):
    shape = w.shape
    cols = shape[-1]
    rows = w.size // cols
    tr = _sum_rows_tile(rows, cols, budget=1 << 20)

    def body(w_ref, g_ref, m_ref, v_ref, d_ref, mo_ref, vo_ref):
        gv = g_ref[...]
        mn = ADAM_B1 * m_ref[...] + (1.0 - ADAM_B1) * gv
        vn = ADAM_B2 * v_ref[...] + (1.0 - ADAM_B2) * (gv * gv)
        m_hat = mn / (1.0 - ADAM_B1 ** ADAM_STEP)
        v_hat = vn / (1.0 - ADAM_B2 ** ADAM_STEP)
        d_ref[...] = -ADAM_LR * (m_hat / (jnp.sqrt(v_hat) + ADAM_EPS) + ADAM_WD * w_ref[...])
        mo_ref[...] = mn
        vo_ref[...] = vn

    blk = pl.BlockSpec((tr, cols), lambda i: (i, 0))
    outs = pl.pallas_call(
        body, name=name, grid=(rows // tr,), in_specs=[blk] * 4, out_specs=[blk] * 3,
        out_shape=[jax.ShapeDtypeStruct((rows, cols), F32)] * 3,
        compiler_params=_cparams("parallel"),
    )(*[a.reshape(rows, cols) for a in (w, g, m, v)])
    return tuple(o.reshape(shape) for o in outs)


WEIGHTS = ['norm_mix', 'norm_mlp', 'w_ada', 'b_ada', 'w_mlp_in', 'w_mlp_out', 'fox_w_in', 'fox_b_f',
           'fox_q_norm', 'fox_k_norm', 'fox_w_out', 'sg_w_in', 'sg_ln_g', 'sg_ln_b', 'sg_w_s', 'sg_b_s',
           'sg_w_out', 'cv_w_pw1', 'cv_b_pw1', 'cv_w_dw', 'cv_b_dw', 'cv_ln_g', 'cv_ln_b', 'cv_w_pw2',
           'cv_b_pw2']
BIG = {'w_mlp_in': 2, 'w_mlp_out': 1, 'fox_w_in': 2, 'fox_w_out': 1, 'sg_w_in': 2, 'sg_w_out': 1,
       'cv_w_pw1': 2, 'cv_w_pw2': 1}
SMALL_SHARDED = ['cv_b_pw1', 'cv_w_dw', 'cv_b_dw', 'cv_ln_g', 'cv_ln_b', 'cv_b_pw2']
SMALL_GRADS = ['norm_mix', 'norm_mlp', 'fox_b_f', 'fox_q_norm', 'fox_k_norm', 'sg_ln_g', 'sg_ln_b', 'sg_w_s',
               'sg_b_s'] + SMALL_SHARDED
GRAD_WIRE_DTYPE = BF16


def _pack_rows(parts, cols):
    flat = jnp.concatenate([p.reshape(-1) for p in parts])
    rows = -(-flat.size // (8 * cols)) * 8
    return jnp.pad(flat, (0, rows * cols - flat.size)).reshape(rows, cols)


def _unpack(flat, shapes):
    out, off = [], 0
    for s in shapes:
        n = math.prod(s)
        out.append(flat[..., off:off + n].reshape(flat.shape[:-1] + tuple(s)))
        off += n
    return out


def _merge_chips(a, axis):
    a = jnp.moveaxis(a, 0, axis)
    return a.reshape(a.shape[:axis] + (a.shape[axis] * a.shape[axis + 1],) + a.shape[axis + 2:])


def _split_chips(a, axis):
    a = a.reshape(a.shape[:axis] + (4, a.shape[axis] // 4) + a.shape[axis + 1:])
    return jnp.moveaxis(a, axis, 0)


def _step(a):
    x, y, c = _position()
    me = 4 * x + 2 * y + c
    chip = 2 * x + y
    T, D = a['x'].shape[1], a['x'].shape[2]
    L = a['norm_mix'].shape[0]

    small_shapes = [(D,)] + [a[n].shape for n in SMALL_SHARDED]
    small = _all_gather8([_pack_rows([a['c']] + [a[n] for n in SMALL_SHARDED], LANE)], name='ag_small')[0]
    small = small.reshape(8, -1)
    c_all = _unpack(small, small_shapes[:1])[0]
    sharded = _unpack(small[0::2, D:], small_shapes[1:])
    P = {n: _merge_chips(v, v.ndim - 2) for n, v in zip(SMALL_SHARDED, sharded)}

    c_act = _silu_rows(c_all, name='c_act')
    mod_cols = jnp.stack([
        _mm(c_act, a['w_ada'][i], name='ada_mod', tm=8, tn=_col_tile(a['w_ada'].shape[2], 768),
            extras=[(lax.dynamic_slice_in_dim(a['b_ada'][i:i + 1], chip * a['w_ada'].shape[2],
                                              a['w_ada'].shape[2], axis=1), 'row')],
            epilogue=_bias_epilogue)
        for i in range(L)])
    mod_all = _all_gather8([mod_cols.reshape(L * 8, -1)], name='ag_mod')[0].reshape(8, L, 8, -1)
    mod = lax.dynamic_index_in_dim(mod_all[0::2], me, axis=2, keepdims=False)
    mod = jnp.moveaxis(mod, 0, 1).reshape(L, 6 * D)

    units = _matrix_units(L)
    first, later = units[:UNITS_PER_LAYER], units[UNITS_PER_LAYER:]
    n_heads = D // HEAD_DIM

    def half_block(unit):
        blk = a[unit[0]][unit[1]]
        return lax.dynamic_index_in_dim(blk.astype(BF16).reshape(2, blk.shape[0] // 2, blk.shape[1]), c, axis=0,
                                        keepdims=False)

    def install(group, gathered):
        for (name, idx), gth in zip(group, gathered):
            blocks = gth.reshape((4,) + a[name].shape[1:])
            if name == 'fox_w_in':
                pad = jnp.zeros((blocks.shape[1], LANE - n_heads), BF16)
                full = jnp.concatenate([blocks[0], blocks[1], blocks[2], blocks[3], pad], axis=-1)
            else:
                full = _merge_chips(blocks, BIG[name] - 1)
            if name in ('w_mlp_in', 'w_mlp_out', 'fox_w_in', 'fox_w_out'):
                P.setdefault(name, {})[idx] = full
            else:
                P[name] = full

    install(first, _all_gather8([half_block(u) for u in first], name='ag_weights_first'))
    for n in ('sg_w_s', 'sg_b_s', 'cv_w_dw'):
        P[n] = (P[n] if n in P else a[n])[0]
    for n in ('norm_mix', 'norm_mlp', 'fox_b_f', 'fox_q_norm', 'fox_k_norm', 'sg_ln_g', 'sg_ln_b'):
        P[n] = a[n]

    def split_grad(unit, grad):
        name = unit[0]
        if name == 'fox_w_in':
            grad = grad[:, :a[name].shape[2] * 4]
        blk = _split_chips(grad, BIG[name] - 1)
        return blk.reshape(4, 2, blk.shape[1] // 2, blk.shape[2])

    state = {}

    def fwd_done(outs):
        install(later, _gather_forward(outs, name='ag_weights_forward'))

    def bwd_comm(mat):
        state['gps'] = [split_grad(u, mat[u]) for u in later]
        state['got1'], parts = _rs_begin(state['gps'], wire_dtype=GRAD_WIRE_DTYPE)
        return _rs_chips_comm(parts)

    def bwd_done(outs):
        state['got2'] = outs

    hooks = Hooks(_gather_comm([half_block(u) for u in later]), fwd_done, bwd_comm, bwd_done)
    loss_part, grad_x, dmod, g, mat = _local_step(a['x'][0], a['loss_target'][0], mod, P, hooks)

    small_g = [dmod, loss_part[0:1, 0:1]] + [g[n] for n in SMALL_GRADS]
    small_g_shapes = [s.shape for s in small_g]
    all_small = _all_gather8([_pack_rows(small_g, LANE)], name='ag_small_grads')[0]
    summed = _sum8(all_small, name='sum_small_grads').reshape(-1)
    sums = _unpack(summed, small_g_shapes)
    loss = sums[1][0, 0]
    grads = dict(zip(SMALL_GRADS, sums[2:]))
    grads['b_ada'] = sums[0]
    for n in SMALL_SHARDED:
        blk = a[n].shape[-1]
        grads[n] = lax.dynamic_slice_in_dim(grads[n], chip * blk, blk, axis=grads[n].ndim - 1)
    dmod_all = all_small.reshape(8, -1)[:, :dmod.size].reshape(8, L, 6 * D)
    cols = a['w_ada'].shape[2]
    dmod_cols = lax.dynamic_slice_in_dim(dmod_all, chip * cols, cols, axis=2)
    pad8 = lambda t: jnp.pad(t, ((0, LANE - 8), (0, 0)))
    c_act_pad = pad8(c_act)
    grads['w_ada'] = jnp.stack([
        _mm(c_act_pad, pad8(dmod_cols[:, i]), ta=True, name='ada_dw', tn=_col_tile(cols, 768))
        for i in range(L)])

    shards = dict(zip(later, _rs_finish(state['gps'], state['got1'], state['got2'])))
    gps = [split_grad(u, mat[u]) for u in first]
    got1, parts = _rs_begin(gps, wire_dtype=GRAD_WIRE_DTYPE)
    got2 = _exchange(_rs_chips_comm(parts), name='rs_chips')
    shards.update(zip(first, _rs_finish(gps, got1, got2)))
    for n in BIG:
        grads[n] = jnp.stack([shards[n, idx].reshape(a[n].shape[1:]) for idx in range(a[n].shape[0])])

    deltas, new_m, new_v = {}, {}, {}
    for n in WEIGHTS:
        deltas[n], new_m[n], new_v[n] = _adamw(a[n], grads[n], a['m_' + n], a['v_' + n], name='adamw')
    return (loss, grad_x[None], *[grads[n] for n in WEIGHTS], *[deltas[n] for n in WEIGHTS],
            *[new_m[n] for n in WEIGHTS], *[new_v[n] for n in WEIGHTS])


UNITS_PER_LAYER = 4


def _matrix_units(n_layers):
    mixers = (('fox_w_in', 'fox_w_out'), ('sg_w_in', 'sg_w_out'), ('cv_w_pw1', 'cv_w_pw2'))
    units = []
    for i in range(n_layers):
        units += [(n, i // N_MIXERS) for n in mixers[i % N_MIXERS]] + [('w_mlp_in', i), ('w_mlp_out', i)]
    return units


def _silu_rows(x, *, name):
    def body(x_ref, o_ref):
        xv = x_ref[...]
        o_ref[...] = (xv * jax.nn.sigmoid(xv)).astype(BF16)

    return pl.pallas_call(body, name=name, out_shape=jax.ShapeDtypeStruct(x.shape, BF16))(x)


def kernel(x, c, norm_mix, norm_mlp, w_ada, b_ada, w_mlp_in, w_mlp_out, fox_w_in, fox_b_f, fox_q_norm, fox_k_norm, fox_w_out, sg_w_in, sg_ln_g, sg_ln_b, sg_w_s, sg_b_s, sg_w_out, cv_w_pw1, cv_b_pw1, cv_w_dw, cv_b_dw, cv_ln_g, cv_ln_b, cv_w_pw2, cv_b_pw2, loss_target, m_norm_mix, m_norm_mlp, m_w_ada, m_b_ada, m_w_mlp_in, m_w_mlp_out, m_fox_w_in, m_fox_b_f, m_fox_q_norm, m_fox_k_norm, m_fox_w_out, m_sg_w_in, m_sg_ln_g, m_sg_ln_b, m_sg_w_s, m_sg_b_s, m_sg_w_out, m_cv_w_pw1, m_cv_b_pw1, m_cv_w_dw, m_cv_b_dw, m_cv_ln_g, m_cv_ln_b, m_cv_w_pw2, m_cv_b_pw2, v_norm_mix, v_norm_mlp, v_w_ada, v_b_ada, v_w_mlp_in, v_w_mlp_out, v_fox_w_in, v_fox_b_f, v_fox_q_norm, v_fox_k_norm, v_fox_w_out, v_sg_w_in, v_sg_ln_g, v_sg_ln_b, v_sg_w_s, v_sg_b_s, v_sg_w_out, v_cv_w_pw1, v_cv_b_pw1, v_cv_w_dw, v_cv_b_dw, v_cv_ln_g, v_cv_ln_b, v_cv_w_pw2, v_cv_b_pw2):
    return _step(dict(locals()))
```

```python
import math
from typing import Callable, NamedTuple

import jax
import jax.numpy as jnp
from jax import lax
from jax.experimental import pallas as pl
from jax.experimental.pallas import tpu as pltpu

F32 = jnp.float32
BF16 = jnp.bfloat16

EPS = 1e-6
HEAD_DIM = 64
LANE = 128
CONV_WIDTH = 31
CONV_HALO = 32
SG_CHUNK = 128
SG_CAUSAL = 64
SG_GROUPS = 8
N_MIXERS = 3
N_CHIPS = 4
VMEM_LIMIT = 56 * 1024 * 1024
NEG = -1e30

ADAM_LR = 0.001
ADAM_B1 = 0.9
ADAM_B2 = 0.999
ADAM_EPS = 1e-08
ADAM_WD = 0.01
ADAM_STEP = 10

MESH = pl.DeviceIdType.MESH
ANY = pl.BlockSpec(memory_space=pl.ANY)


def _cparams(*sem):
    return pltpu.CompilerParams(dimension_semantics=sem, vmem_limit_bytes=VMEM_LIMIT)


def _row_tile(t, want=512):
    return min(t, want)


def _matmul(a, b, *, name, ta=False, tb=False, tm=512, tn=1024, tk=1024,
            extras=(), epilogue=None, out_dtypes=(F32,), b_outer=False, out_chips=None):
    M, K = (a.shape[1], a.shape[0]) if ta else a.shape
    N = b.shape[0] if tb else b.shape[1]
    assert (b.shape[1] if tb else b.shape[0]) == K
    n_own = N // out_chips if out_chips else N
    tm, tn, tk = min(tm, M), min(tn, n_own), min(tk, K)
    assert M % tm == 0 and n_own % tn == 0 and K % tk == 0, (name, M, N, K, tm, tn, tk)
    nk = K // tk

    def spec(shape, pick):
        if b_outer:
            return pl.BlockSpec(shape, lambda j, i, k: pick(i, j, k))
        return pl.BlockSpec(shape, pick)

    a_spec = spec((tk, tm), lambda i, j, k: (k, i)) if ta else spec((tm, tk), lambda i, j, k: (i, k))
    b_spec = spec((tn, tk), lambda i, j, k: (j, k)) if tb else spec((tk, tn), lambda i, j, k: (k, j))
    ex_specs = [spec((tm, tn), lambda i, j, k: (i, j)) if kind == 'tile' else spec((1, tn), lambda i, j, k: (0, j))
                for _, kind in extras]
    dims = (((0,) if ta else (1,), (1,) if tb else (0,)), ((), ()))
    n_ex, n_out = len(extras), len(out_dtypes)

    def body(*refs):
        a_ref, b_ref = refs[0], refs[1]
        ex = refs[2:2 + n_ex]
        outs = refs[2 + n_ex:2 + n_ex + n_out]

        def finish(acc):
            vals = epilogue(acc, *[r[...] for r in ex]) if epilogue else (acc,)
            for o, v in zip(outs, vals):
                o[...] = v.astype(o.dtype)

        part = lax.dot_general(a_ref[...].astype(BF16), b_ref[...].astype(BF16), dims,
                               preferred_element_type=F32)
        if nk == 1:
            finish(part)
        else:
            acc_ref = refs[-1]
            k = pl.program_id(2)

            @pl.when(k == 0)
            def _():
                acc_ref[...] = part

            @pl.when(k > 0)
            def _():
                acc_ref[...] += part

            @pl.when(k == nk - 1)
            def _():
                finish(acc_ref[...])

    outs = pl.pallas_call(
        body, name=name,
        grid=(N // tn, M // tm, nk) if b_outer else (M // tm, N // tn, nk),
        in_specs=[a_spec, b_spec] + ex_specs,
        out_specs=[spec((None, tm, tn), lambda i, j, k: (j // (n_own // tn), i, j % (n_own // tn)))
                   if out_chips else spec((tm, tn), lambda i, j, k: (i, j)) for _ in out_dtypes],
        out_shape=[jax.ShapeDtypeStruct((out_chips, M, n_own) if out_chips else (M, N), dt) for dt in out_dtypes],
        scratch_shapes=[pltpu.VMEM((tm, tn), F32)] if nk > 1 else [],
        compiler_params=_cparams("parallel", "parallel", "arbitrary"),
    )(a, b, *[arr for arr, _ in extras])
    return outs if n_out > 1 else outs[0]


def _norm_mod_fwd(x, w, sc, sh, *, name):
    T, D = x.shape
    tr = _row_tile(T)

    def body(x_ref, w_ref, sc_ref, sh_ref, h_ref):
        xv = x_ref[...]
        r = lax.rsqrt(jnp.mean(xv * xv, axis=-1, keepdims=True) + EPS)
        h_ref[...] = ((xv * r) * w_ref[...] * (1.0 + sc_ref[...]) + sh_ref[...]).astype(BF16)

    row = pl.BlockSpec((1, D), lambda i: (0, 0))
    return pl.pallas_call(
        body, name=name, grid=(T // tr,),
        in_specs=[pl.BlockSpec((tr, D), lambda i: (i, 0)), row, row, row],
        out_specs=pl.BlockSpec((tr, D), lambda i: (i, 0)),
        out_shape=jax.ShapeDtypeStruct((T, D), BF16),
        compiler_params=_cparams("parallel"),
    )(x, w, sc, sh)


def _norm_mod_bwd(dh, x, dres, w, sc, *, name, gate=None):
    T, D = x.shape
    tr = _row_tile(T)
    with_gate = gate is not None

    def body(*refs):
        if with_gate:
            dh_ref, x_ref, dres_ref, w_ref, sc_ref, y_ref, g_ref, dx_ref, sums_ref, dy_ref = refs
        else:
            dh_ref, x_ref, dres_ref, w_ref, sc_ref, dx_ref, sums_ref = refs
        i = pl.program_id(0)
        xv, dhv = x_ref[...], dh_ref[...].astype(F32)
        r = lax.rsqrt(jnp.mean(xv * xv, axis=-1, keepdims=True) + EPS)
        n = xv * r
        wv, scale = w_ref[...], 1.0 + sc_ref[...]
        dn = dhv * (wv * scale)
        dx = dres_ref[...] + r * (dn - n * jnp.mean(dn * n, axis=-1, keepdims=True))
        dx_ref[...] = dx
        rows = [jnp.sum(dhv, axis=0, keepdims=True),
                jnp.sum(dhv * (n * wv), axis=0, keepdims=True),
                jnp.sum(dhv * n * scale, axis=0, keepdims=True)]
        if with_gate:
            dy_ref[...] = (dx * g_ref[...]).astype(BF16)
            rows.append(jnp.sum(dx * y_ref[...], axis=0, keepdims=True))
            rows.append(jnp.sum(dx * g_ref[...], axis=0, keepdims=True))
        part = jnp.concatenate(rows + [jnp.zeros((8 - len(rows), D), F32)], axis=0)

        @pl.when(i == 0)
        def _():
            sums_ref[...] = part

        @pl.when(i > 0)
        def _():
            sums_ref[...] += part

    blk = pl.BlockSpec((tr, D), lambda i: (i, 0))
    row = pl.BlockSpec((1, D), lambda i: (0, 0))
    in_specs = [blk, blk, blk, row, row]
    args = [dh, x, dres, w, sc]
    out_specs = [blk, pl.BlockSpec((8, D), lambda i: (0, 0))]
    out_shape = [jax.ShapeDtypeStruct((T, D), F32), jax.ShapeDtypeStruct((8, D), F32)]
    if with_gate:
        in_specs += [blk, row]
        args += list(gate)
        out_specs.append(blk)
        out_shape.append(jax.ShapeDtypeStruct((T, D), BF16))
    return pl.pallas_call(
        body, name=name, grid=(T // tr,), in_specs=in_specs, out_specs=out_specs,
        out_shape=out_shape, compiler_params=_cparams("arbitrary"),
    )(*args)


def _gate_bwd(dx, y, g, *, name):
    T, D = dx.shape
    tr = _row_tile(T)

    def body(dx_ref, y_ref, g_ref, dy_ref, dg_ref):
        i = pl.program_id(0)
        dxv = dx_ref[...]
        dy_ref[...] = (dxv * g_ref[...]).astype(BF16)
        part = jnp.concatenate([jnp.sum(dxv * y_ref[...], axis=0, keepdims=True),
                                jnp.zeros((7, D), F32)], axis=0)

        @pl.when(i == 0)
        def _():
            dg_ref[...] = part

        @pl.when(i > 0)
        def _():
            dg_ref[...] += part

    blk = pl.BlockSpec((tr, D), lambda i: (i, 0))
    return pl.pallas_call(
        body, name=name, grid=(T // tr,),
        in_specs=[blk, blk, pl.BlockSpec((1, D), lambda i: (0, 0))],
        out_specs=[blk, pl.BlockSpec((8, D), lambda i: (0, 0))],
        out_shape=[jax.ShapeDtypeStruct((T, D), BF16), jax.ShapeDtypeStruct((8, D), F32)],
        compiler_params=_cparams("arbitrary"),
    )(dx, y, g)


def _loss_head(y, target, *, name):
    T, D = y.shape
    tr = _row_tile(T)

    def body(y_ref, t_ref, loss_ref, dy_ref):
        i = pl.program_id(0)
        e = y_ref[...] - t_ref[...]
        dy_ref[...] = e * (1.0 / D)
        part = jnp.full((8, LANE), 0.5 / D * jnp.sum(e * e), F32)

        @pl.when(i == 0)
        def _():
            loss_ref[...] = part

        @pl.when(i > 0)
        def _():
            loss_ref[...] += part

    blk = pl.BlockSpec((tr, D), lambda i: (i, 0))
    return pl.pallas_call(
        body, name=name, grid=(T // tr,), in_specs=[blk, blk],
        out_specs=[pl.BlockSpec((8, LANE), lambda i: (0, 0)), blk],
        out_shape=[jax.ShapeDtypeStruct((8, LANE), F32), jax.ShapeDtypeStruct((T, D), F32)],
        compiler_params=_cparams("arbitrary"),
    )(y, target)


def _position():
    return lax.axis_index("x"), lax.axis_index("y"), lax.axis_index("c")


class Comm(NamedTuple):
    srcs: list
    out_shapes: list
    plan: Callable
    n_remote: int
    n_local: int

    def scratch(self):
        return [pltpu.SemaphoreType.DMA((self.n_remote,)), pltpu.SemaphoreType.DMA((self.n_remote,)),
                pltpu.SemaphoreType.DMA((max(self.n_local, 1),))]


def _comm_copies(plan, src_refs, out_refs, send_sems, recv_sems, local_sems):
    x, y, c = _position()
    remote, local = plan(src_refs, out_refs, x, y, c)

    def copy(k, s, d, peer):
        return pltpu.make_async_remote_copy(src_ref=s, dst_ref=d, send_sem=send_sems.at[k],
                                            recv_sem=recv_sems.at[k], device_id=peer, device_id_type=MESH)

    sends = [copy(k, s, d, peer) for k, (s, d, peer, _) in enumerate(remote)]
    recvs = [copy(k, s, landing, peer) for k, (s, _, peer, landing) in enumerate(remote)]
    local_copies = [pltpu.make_async_copy(s, d, local_sems.at[i]) for i, (s, d) in enumerate(local)]
    return sends, recvs, local_copies


def _comm_start(copies):
    sends, _, local_copies = copies
    for cp in local_copies + sends:
        cp.start()


def _comm_wait(copies):
    sends, recvs, local_copies = copies
    for cp in recvs:
        cp.wait_recv()
    for cp in sends:
        cp.wait_send()
    for cp in local_copies:
        cp.wait()


def _split_comm_refs(refs, n_in, n_out, n_scratch, comm):
    ns, nd = (len(comm.srcs), len(comm.out_shapes)) if comm else (0, 0)
    cuts = [n_in, ns, n_out, nd, n_scratch]
    parts, at = [], 0
    for n in cuts:
        parts.append(refs[at:at + n])
        at += n
    return (*parts, refs[at:])


AUG_F = HEAD_DIM
AUG_LSE = HEAD_DIM + 6


def _half_cols(x, lo):
    return (jnp.sum(jnp.where(lo, x, 0.0), axis=-1, keepdims=True),
            jnp.sum(jnp.where(lo, 0.0, x), axis=-1, keepdims=True))


def _half_sums(x, lo):
    s_lo, s_hi = _half_cols(x, lo)
    return jnp.where(lo, s_lo, s_hi)


def _split3(x):
    a = x.astype(BF16).astype(F32)
    r = x - a
    b = r.astype(BF16).astype(F32)
    return a, b, (r - b).astype(BF16).astype(F32)


def _aug(lane, base, terms):
    out = jnp.zeros(lane.shape, F32)
    for i, t in enumerate(terms):
        out = jnp.where(lane == base + i, t, out)
    return out


def _head_lanes(x2, h):
    return x2 if h == 0 else pltpu.roll(x2, HEAD_DIM, 1)


def _fox_prep_fwd(proj, qg, kg, fcol, *, d_model, name):
    T = proj.shape[0]
    nhp = d_model // LANE
    tr = _row_tile(T)

    def body(q_ref, k_ref, v_ref, qg_ref, kg_ref, f_ref, qa_ref, qta_ref, ka_ref, kta_ref, va_ref, vta_ref):
        lane = lax.broadcasted_iota(jnp.int32, (tr, LANE), 1)
        lo = lane < HEAD_DIM

        def norm(xv, g):
            ms = _half_sums(xv * xv, lo) * (1.0 / HEAD_DIM)
            return (xv * lax.rsqrt(ms + EPS)) * g

        qn = norm(q_ref[...], qg_ref[...]) * (HEAD_DIM ** -0.5)
        kn = norm(k_ref[...], kg_ref[...])
        vv = v_ref[...]
        qa, ka, va, vta = [], [], [], []
        for h in range(2):
            f1, f2, f3 = _split3(f_ref[h])
            qa.append(jnp.where(lo, _head_lanes(qn, h), _aug(lane, AUG_F, [f1, f2, f3, 1.0, 1.0, 1.0])))
            ka.append(jnp.where(lo, _head_lanes(kn, h),
                                _aug(lane, AUG_F, [1.0, 1.0, 1.0, -f1, -f2, -f3, 1.0, 1.0, 1.0])))
            va.append(jnp.where(lo if h == 0 else jnp.logical_not(lo), vv, 0.0))
            vta.append(jnp.where(lo, _head_lanes(vv, h), _aug(lane, AUG_F, [1.0, 1.0, 1.0])))
        for parts, ref, tref in ((qa, qa_ref, qta_ref), (ka, ka_ref, kta_ref), (va, va_ref, None),
                                 (vta, None, vta_ref)):
            both = jnp.concatenate(parts, axis=1)
            if ref is not None:
                ref[...] = both.astype(BF16)
            if tref is not None:
                tref[...] = both.astype(BF16).T

    gain = pl.BlockSpec((1, LANE), lambda i, h: (0, 0))
    rows = pl.BlockSpec((tr, 2 * LANE), lambda i, h: (i, h))
    cols = pl.BlockSpec((2 * LANE, tr), lambda i, h: (h, i))
    wide, tall = jax.ShapeDtypeStruct((T, 2 * d_model), BF16), jax.ShapeDtypeStruct((2 * d_model, T), BF16)
    return pl.pallas_call(
        body, name=name, grid=(T // tr, nhp),
        in_specs=[pl.BlockSpec((tr, LANE), lambda i, h: (i, h)),
                  pl.BlockSpec((tr, LANE), lambda i, h: (i, nhp + h)),
                  pl.BlockSpec((tr, LANE), lambda i, h: (i, 2 * nhp + h)), gain, gain,
                  pl.BlockSpec((2, tr, 1), lambda i, h: (h, i, 0))],
        out_specs=[rows, cols, rows, cols, rows, cols],
        out_shape=[wide, tall, wide, tall, wide, tall],
        compiler_params=_cparams("parallel", "parallel"),
    )(proj, proj, proj, qg, kg, fcol)


def _fox_do_prep(do, o, *, name):
    T, D = do.shape
    nhp = D // LANE
    tr = _row_tile(T)

    def body(do_ref, o_ref, doa_ref, dota_ref):
        lane = lax.broadcasted_iota(jnp.int32, (tr, LANE), 1)
        lo = lane < HEAD_DIM
        dob = do_ref[...].astype(BF16).astype(F32)
        deltas = _half_cols(dob * o_ref[...], lo)
        both = jnp.concatenate(
            [jnp.where(lo, _head_lanes(dob, h), _aug(lane, AUG_F, _split3(-deltas[h]))) for h in range(2)], axis=1)
        doa_ref[...] = both.astype(BF16)
        dota_ref[...] = both.astype(BF16).T

    blk = pl.BlockSpec((tr, LANE), lambda i, h: (i, h))
    return pl.pallas_call(
        body, name=name, grid=(T // tr, nhp), in_specs=[blk, blk],
        out_specs=[pl.BlockSpec((tr, 2 * LANE), lambda i, h: (i, h)),
                   pl.BlockSpec((2 * LANE, tr), lambda i, h: (h, i))],
        out_shape=[jax.ShapeDtypeStruct((T, 2 * D), BF16), jax.ShapeDtypeStruct((2 * D, T), BF16)],
        compiler_params=_cparams("parallel", "parallel"),
    )(do, o)


def _fox_prep_bwd(proj, dq, dkt, dvt, qg, kg, *, d_model, name):
    T = proj.shape[0]
    nhp = d_model // LANE
    tr = _row_tile(T)

    def body(q_ref, k_ref, dq_ref, dkt_ref, dvt_ref, qg_ref, kg_ref, dqo_ref, dko_ref, dvo_ref, sums_ref):
        first = (pl.program_id(0) == 0) & (pl.program_id(1) == 0)
        lo = lax.broadcasted_iota(jnp.int32, (tr, LANE), 1) < HEAD_DIM

        def pair(x2):
            return jnp.where(lo, x2[:, :LANE], pltpu.roll(x2[:, LANE:], HEAD_DIM, 1))

        def bwd(xv, dxhat, g):
            ms = _half_sums(xv * xv, lo) * (1.0 / HEAD_DIM)
            r = lax.rsqrt(ms + EPS)
            n = xv * r
            dn = dxhat * g
            dx = r * (dn - n * (_half_sums(dn * n, lo) * (1.0 / HEAD_DIM)))
            dg = jnp.sum(dxhat * n, axis=0, keepdims=True)
            return dx, dg + pltpu.roll(dg, HEAD_DIM, 1)

        dxq, dgq = bwd(q_ref[...], pair(dq_ref[...]) * (HEAD_DIM ** -0.5), qg_ref[...])
        dxk, dgk = bwd(k_ref[...], pair(dkt_ref[...].T), kg_ref[...])
        dqo_ref[...] = dxq.astype(BF16)
        dko_ref[...] = dxk.astype(BF16)
        dvo_ref[...] = pair(dvt_ref[...].T.astype(F32)).astype(BF16)
        part = jnp.concatenate([dgq, dgk, jnp.zeros((6, LANE), F32)], axis=0)

        @pl.when(first)
        def _():
            sums_ref[...] = part

        @pl.when(jnp.logical_not(first))
        def _():
            sums_ref[...] += part

    gain = pl.BlockSpec((1, LANE), lambda i, h: (0, 0))
    blk = pl.BlockSpec((tr, LANE), lambda i, h: (i, h))
    tall = pl.BlockSpec((2 * LANE, tr), lambda i, h: (h, i))
    return pl.pallas_call(
        body, name=name, grid=(T // tr, nhp),
        in_specs=[blk, pl.BlockSpec((tr, LANE), lambda i, h: (i, nhp + h)),
                  pl.BlockSpec((tr, 2 * LANE), lambda i, h: (i, h)), tall, tall, gain, gain],
        out_specs=[blk, blk, blk, pl.BlockSpec((8, LANE), lambda i, h: (0, 0))],
        out_shape=[jax.ShapeDtypeStruct((T, d_model), BF16)] * 3 + [jax.ShapeDtypeStruct((8, LANE), F32)],
        compiler_params=_cparams("arbitrary", "arbitrary"),
    )(proj, proj, dq, dkt, dvt, qg, kg)


def _scan_lanes(x, reverse):
    n = x.shape[-1]
    lane = lax.broadcasted_iota(jnp.int32, x.shape, 1)
    sh = 1
    while sh < n:
        if reverse:
            x = x + jnp.where(lane < n - sh, pltpu.roll(x, n - sh, 1), 0.0)
        else:
            x = x + jnp.where(lane >= sh, pltpu.roll(x, sh, 1), 0.0)
        sh *= 2
    return x


def _fox_gate_fwd(fpre_t, bf, *, name):
    def body(f_ref, b_ref, o_ref):
        xv = f_ref[...] + b_ref[...]
        logf = jnp.minimum(xv, 0.0) - jnp.log1p(jnp.exp(-jnp.abs(xv)))
        o_ref[...] = _scan_lanes(logf, reverse=False)

    return pl.pallas_call(body, name=name, out_shape=jax.ShapeDtypeStruct(fpre_t.shape, F32))(fpre_t, bf)


def _fox_gate_bwd(dcol, drow, fpre_t, bf, *, name):
    H = fpre_t.shape[0]

    def body(dc_ref, dr_ref, f_ref, b_ref, o_ref, db_ref):
        xv = f_ref[...] + b_ref[...]
        e = dc_ref[...] - dr_ref[...]
        dlogf = _scan_lanes(e, reverse=False) - e
        dpre = dlogf * (1.0 - jax.nn.sigmoid(xv))
        o_ref[...] = dpre
        db_ref[...] = jnp.broadcast_to(jnp.sum(dpre, axis=-1, keepdims=True), (H, LANE))

    return pl.pallas_call(
        body, name=name,
        out_shape=[jax.ShapeDtypeStruct(fpre_t.shape, F32), jax.ShapeDtypeStruct((H, LANE), F32)],
    )(dcol, drow, fpre_t, bf)


_NT = (((1,), (1,)), ((), ()))
_TN = (((0,), (0,)), ((), ()))
_NN = (((1,), (0,)), ((), ()))


def _attn_tile(T):
    return min(T, 512)


def _causal(tq, tk):
    return lax.broadcasted_iota(jnp.int32, (tq, tk), 1) <= lax.broadcasted_iota(jnp.int32, (tq, tk), 0)


def _fox_attn_fwd(qa, kta, va, *, name, comm=None):
    T = qa.shape[0]
    nhp = qa.shape[1] // (2 * LANE)
    tq = tk = _attn_tile(T)
    nq = T // tq

    def body(*refs):
        (qa_ref, kta_ref, va_ref), src_refs, (o_ref, qb_ref), dst_refs, (m_sc, l_sc, acc_sc), sems = (
            _split_comm_refs(refs, 3, 2, 3, comm))
        hp, i, j = pl.program_id(0), pl.program_id(1), pl.program_id(2)
        if comm:
            @pl.when((hp == 0) & (i == 0) & (j == 0))
            def _():
                _comm_start(_comm_copies(comm.plan, src_refs, dst_refs, *sems))

        @pl.when(j == 0)
        def _():
            m_sc[...] = jnp.full(m_sc.shape, NEG, F32)
            l_sc[...] = jnp.zeros(l_sc.shape, F32)
            acc_sc[...] = jnp.zeros(acc_sc.shape, F32)

        def block(diagonal):
            for h in range(2):
                hs = slice(h * LANE, (h + 1) * LANE)
                s = lax.dot_general(qa_ref[:, hs], kta_ref[hs, :], _NN, preferred_element_type=F32)
                if diagonal:
                    s = jnp.where(_causal(tq, tk), s, NEG)
                m_prev = m_sc[h]
                m_next = jnp.maximum(m_prev, jnp.max(s, axis=1, keepdims=True))
                p = jnp.exp(s - jnp.tile(m_next, (1, tk // LANE)))
                alpha = jnp.exp(m_prev - m_next)
                l_sc[h] = alpha * l_sc[h] + jnp.sum(p, axis=1, keepdims=True)
                m_sc[h] = m_next
                acc_sc[h] = alpha * acc_sc[h] + lax.dot_general(p.astype(BF16), va_ref[:, hs], _NN,
                                                                preferred_element_type=F32)

        @pl.when(j < i)
        def _():
            block(False)

        @pl.when(j == i)
        def _():
            block(True)
            o_ref[...] = acc_sc[0] / l_sc[0] + acc_sc[1] / l_sc[1]
            lane = lax.broadcasted_iota(jnp.int32, (tq, LANE), 1)
            for h in range(2):
                hs = slice(h * LANE, (h + 1) * LANE)
                pieces = _split3(-(m_sc[h] + jnp.log(l_sc[h])))
                qb = qa_ref[:, hs].astype(F32)
                for n, piece in enumerate(pieces):
                    qb = jnp.where(lane == AUG_LSE + n, piece, qb)
                qb_ref[:, hs] = qb.astype(BF16)

        if comm:
            @pl.when((hp == nhp - 1) & (i == nq - 1) & (j == nq - 1))
            def _():
                _comm_wait(_comm_copies(comm.plan, src_refs, dst_refs, *sems))

    outs = pl.pallas_call(
        body, name=name, grid=(nhp, nq, nq),
        in_specs=[pl.BlockSpec((tq, 2 * LANE), lambda h, i, j: (i, h)),
                  pl.BlockSpec((2 * LANE, tk), lambda h, i, j: (h, jnp.minimum(j, i))),
                  pl.BlockSpec((tk, 2 * LANE), lambda h, i, j: (jnp.minimum(j, i), h))]
        + ([ANY] * len(comm.srcs) if comm else []),
        out_specs=[pl.BlockSpec((tq, LANE), lambda h, i, j: (i, h)),
                   pl.BlockSpec((tq, 2 * LANE), lambda h, i, j: (i, h))]
        + ([ANY] * len(comm.out_shapes) if comm else []),
        out_shape=[jax.ShapeDtypeStruct((T, nhp * LANE), F32), jax.ShapeDtypeStruct(qa.shape, BF16)]
        + (list(comm.out_shapes) if comm else []),
        scratch_shapes=[pltpu.VMEM((2, tq, LANE), F32), pltpu.VMEM((2, tq, LANE), F32),
                        pltpu.VMEM((2, tq, LANE), F32)] + (comm.scratch() if comm else []),
        compiler_params=(_cparams("arbitrary", "arbitrary", "arbitrary") if comm
                         else _cparams("parallel", "parallel", "arbitrary")),
    )(qa, kta, va, *(comm.srcs if comm else []))
    return outs[0], outs[1], outs[2:]


def _fox_attn_bwd(qb, qta, ka, kta, vta, doa, dota, *, name, comm=None):
    T = qb.shape[0]
    nhp = qb.shape[1] // (2 * LANE)
    tq = tk = _attn_tile(T)
    nq = T // tq

    def body(*refs):
        ((qb_ref, qta_ref, ka_ref, kta_ref, vta_ref, doa_ref, dota_ref), src_refs,
         (dq_ref, dkt_ref, dvt_ref, dcol_ref, drow_ref), dst_refs, (dkt_sc, dvt_sc, dcol_sc), sems) = (
            _split_comm_refs(refs, 7, 5, 3, comm))
        hp, j, i = pl.program_id(0), pl.program_id(1), pl.program_id(2)
        if comm:
            @pl.when((hp == 0) & (j == 0) & (i == 0))
            def _():
                _comm_start(_comm_copies(comm.plan, src_refs, dst_refs, *sems))

        @pl.when((j == 0) & (i == 0))
        def _():
            dq_ref[...] = jnp.zeros(dq_ref.shape, F32)
            drow_ref[...] = jnp.zeros(drow_ref.shape, F32)

        @pl.when(i == 0)
        def _():
            dkt_sc[...] = jnp.zeros(dkt_sc.shape, F32)
            dvt_sc[...] = jnp.zeros(dvt_sc.shape, F32)
            dcol_sc[...] = jnp.zeros(dcol_sc.shape, F32)

        def block(diagonal):
            rows = pl.ds(pl.multiple_of(i * tq, tq), tq)
            for h in range(2):
                hs = slice(h * LANE, (h + 1) * LANE)
                p = jnp.exp(lax.dot_general(qb_ref[:, hs], kta_ref[hs, :], _NN, preferred_element_type=F32))
                if diagonal:
                    p = jnp.where(_causal(tq, tk), p, 0.0)
                dl = p * lax.dot_general(doa_ref[:, hs], vta_ref[hs, :], _NN, preferred_element_type=F32)
                dlb = dl.astype(BF16)
                dvt_sc[h] += lax.dot_general(dota_ref[hs, :], p.astype(BF16), _NN, preferred_element_type=F32)
                dkt_sc[h] += lax.dot_general(qta_ref[hs, :], dlb, _NN, preferred_element_type=F32)
                dq_ref[rows, hs] += lax.dot_general(dlb, ka_ref[:, hs], _NN, preferred_element_type=F32)
                dcol_sc[h] += jnp.sum(dl, axis=0, keepdims=True)
                drow_ref[h, rows, :] += jnp.sum(dl, axis=1, keepdims=True)

        @pl.when(i > j)
        def _():
            block(False)

        @pl.when(i == j)
        def _():
            block(True)

        @pl.when(i == nq - 1)
        def _():
            dkt_ref[...] = jnp.concatenate([dkt_sc[0], dkt_sc[1]], axis=0)
            dvt_ref[...] = jnp.concatenate([dvt_sc[0], dvt_sc[1]], axis=0).astype(BF16)
            dcol_ref[...] = dcol_sc[...]

        if comm:
            @pl.when((hp == nhp - 1) & (j == nq - 1) & (i == nq - 1))
            def _():
                _comm_wait(_comm_copies(comm.plan, src_refs, dst_refs, *sems))

    qrow = pl.BlockSpec((tq, 2 * LANE), lambda h, j, i: (jnp.maximum(i, j), h))
    qcol = pl.BlockSpec((2 * LANE, tq), lambda h, j, i: (h, jnp.maximum(i, j)))
    krow = pl.BlockSpec((tk, 2 * LANE), lambda h, j, i: (j, h))
    kcol = pl.BlockSpec((2 * LANE, tk), lambda h, j, i: (h, j))
    tall = jax.ShapeDtypeStruct((qb.shape[1], T), F32)
    outs = pl.pallas_call(
        body, name=name, grid=(nhp, nq, nq),
        in_specs=[qrow, qcol, krow, kcol, kcol, qrow, qcol] + ([ANY] * len(comm.srcs) if comm else []),
        out_specs=[pl.BlockSpec((T, 2 * LANE), lambda h, j, i: (0, h)), kcol, kcol,
                   pl.BlockSpec((2, 1, tk), lambda h, j, i: (h, 0, j)),
                   pl.BlockSpec((2, T, 1), lambda h, j, i: (h, 0, 0))]
        + ([ANY] * len(comm.out_shapes) if comm else []),
        out_shape=[jax.ShapeDtypeStruct(qb.shape, F32), tall, jax.ShapeDtypeStruct(tall.shape, BF16),
                   jax.ShapeDtypeStruct((2 * nhp, 1, T), F32), jax.ShapeDtypeStruct((2 * nhp, T, 1), F32)]
        + (list(comm.out_shapes) if comm else []),
        scratch_shapes=[pltpu.VMEM((2, LANE, tk), F32), pltpu.VMEM((2, LANE, tk), F32),
                        pltpu.VMEM((2, 1, tk), F32)] + (comm.scratch() if comm else []),
        compiler_params=_cparams("arbitrary" if comm else "parallel", "arbitrary", "arbitrary"),
    )(qb, qta, ka, kta, vta, doa, dota, *(comm.srcs if comm else []))
    return (*outs[:5], outs[5:])


_GELU_C = math.sqrt(2.0 / math.pi)
_GELU_A = 0.044715


def _gelu(x):
    t = jnp.tanh(_GELU_C * (x + _GELU_A * (x * x * x)))
    return x * (0.5 * (1.0 + t)), t


def _gelu_grad(x, t):
    return 0.5 * (1.0 + t) + 0.5 * x * (1.0 - t * t) * (_GELU_C * (1.0 + 3.0 * _GELU_A * x * x))


def _layer_norm_stats(v):
    mu = jnp.mean(v, axis=-1, keepdims=True)
    vc = v - mu
    rstd = lax.rsqrt(jnp.mean(vc * vc, axis=-1, keepdims=True) + EPS)
    return vc * rstd, rstd


def _layer_norm_bwd(dyhat, yhat, rstd):
    return rstd * (dyhat - jnp.mean(dyhat, axis=-1, keepdims=True)
                   - yhat * jnp.mean(dyhat * yhat, axis=-1, keepdims=True))


def _sg_mask():
    t = lax.broadcasted_iota(jnp.int32, (SG_CHUNK, SG_CHUNK), 0) // SG_CAUSAL
    s = lax.broadcasted_iota(jnp.int32, (SG_CHUNK, SG_CHUNK), 1) // SG_CAUSAL
    return s <= t


def _sg_mix(ws_ref, bc_ref, vln_sc, vo_sc, tr, gd):
    mask = _sg_mask()
    for g in range(SG_GROUPS):
        wg = jnp.where(mask, ws_ref[g], 0.0).astype(BF16)
        cols = slice(g * gd, (g + 1) * gd)
        for n in range(tr // SG_CHUNK):
            rows = slice(n * SG_CHUNK, (n + 1) * SG_CHUNK)
            vo_sc[rows, cols] = lax.dot_general(wg, vln_sc[rows, cols], _NN,
                                                preferred_element_type=F32) + bc_ref[g]


def _sg_fwd(a_uv, ln_g, ln_b, ws, bcol, *, name):
    T, W = a_uv.shape[0], a_uv.shape[1] // 2
    gd = W // SG_GROUPS
    tr = _row_tile(T)

    def body(u_ref, v_ref, g_ref, b_ref, ws_ref, bc_ref, o_ref, vln_sc, vo_sc):
        u, _ = _gelu(u_ref[...])
        v, _ = _gelu(v_ref[...])
        vhat, _ = _layer_norm_stats(v)
        vln_sc[...] = (vhat * g_ref[...] + b_ref[...]).astype(BF16)
        _sg_mix(ws_ref, bc_ref, vln_sc, vo_sc, tr, gd)
        o_ref[...] = (u * vo_sc[...]).astype(BF16)

    row = pl.BlockSpec((1, W), lambda i: (0, 0))
    return pl.pallas_call(
        body, name=name, grid=(T // tr,),
        in_specs=[pl.BlockSpec((tr, W), lambda i: (i, 0)), pl.BlockSpec((tr, W), lambda i: (i, 1)), row, row,
                  pl.BlockSpec((SG_GROUPS, SG_CHUNK, SG_CHUNK), lambda i: (0, 0, 0)),
                  pl.BlockSpec((SG_GROUPS, SG_CHUNK, 1), lambda i: (0, 0, 0))],
        out_specs=pl.BlockSpec((tr, W), lambda i: (i, 0)),
        out_shape=jax.ShapeDtypeStruct((T, W), BF16),
        scratch_shapes=[pltpu.VMEM((tr, W), BF16), pltpu.VMEM((tr, W), F32)],
        compiler_params=_cparams("parallel"),
    )(a_uv, a_uv, ln_g, ln_b, ws, bcol)


def _sg_bwd(a_uv, dgate, ln_g, ln_b, ws, bcol, *, name):
    T, W = a_uv.shape[0], a_uv.shape[1] // 2
    gd = W // SG_GROUPS
    tr = _row_tile(T)

    def body(u_ref, v_ref, dg_ref, g_ref, b_ref, ws_ref, bc_ref,
             da_ref, dws_ref, dbs_ref, sums_ref, vln_sc, vo_sc, dvo_sc, dvln_sc):
        i = pl.program_id(0)

        @pl.when(i == 0)
        def _():
            dws_ref[...] = jnp.zeros(dws_ref.shape, F32)
            dbs_ref[...] = jnp.zeros(dbs_ref.shape, F32)
            sums_ref[...] = jnp.zeros(sums_ref.shape, F32)

        ua, va = u_ref[...], v_ref[...]
        u, tu = _gelu(ua)
        v, tv = _gelu(va)
        vhat, rstd = _layer_norm_stats(v)
        vln_sc[...] = (vhat * g_ref[...] + b_ref[...]).astype(BF16)
        _sg_mix(ws_ref, bc_ref, vln_sc, vo_sc, tr, gd)
        dgt = dg_ref[...]
        du = dgt * vo_sc[...]
        dvo_sc[...] = dgt * u
        mask = _sg_mask()
        for g in range(SG_GROUPS):
            wg = jnp.where(mask, ws_ref[g], 0.0).astype(BF16)
            cols = slice(g * gd, (g + 1) * gd)
            acc_w = jnp.zeros((SG_CHUNK, SG_CHUNK), F32)
            acc_b = jnp.zeros((SG_CHUNK, 1), F32)
            for n in range(tr // SG_CHUNK):
                rows = slice(n * SG_CHUNK, (n + 1) * SG_CHUNK)
                dvo = dvo_sc[rows, cols]
                dvob = dvo.astype(BF16)
                dvln_sc[rows, cols] = lax.dot_general(wg, dvob, _TN, preferred_element_type=F32)
                acc_w += lax.dot_general(dvob, vln_sc[rows, cols], _NT, preferred_element_type=F32)
                acc_b += jnp.sum(dvo, axis=1, keepdims=True)
            dws_ref[g] += jnp.where(mask, acc_w, 0.0)
            dbs_ref[g] += acc_b
        dvln = dvln_sc[...]
        sums_ref[...] += jnp.concatenate([jnp.sum(dvln * vhat, axis=0, keepdims=True),
                                          jnp.sum(dvln, axis=0, keepdims=True),
                                          jnp.zeros((6, W), F32)], axis=0)
        dv = _layer_norm_bwd(dvln * g_ref[...], vhat, rstd)
        da_ref[:, :W] = (du * _gelu_grad(ua, tu)).astype(BF16)
        da_ref[:, W:] = (dv * _gelu_grad(va, tv)).astype(BF16)

    row = pl.BlockSpec((1, W), lambda i: (0, 0))
    wspec = pl.BlockSpec((SG_GROUPS, SG_CHUNK, SG_CHUNK), lambda i: (0, 0, 0))
    bspec = pl.BlockSpec((SG_GROUPS, SG_CHUNK, 1), lambda i: (0, 0, 0))
    return pl.pallas_call(
        body, name=name, grid=(T // tr,),
        in_specs=[pl.BlockSpec((tr, W), lambda i: (i, 0)), pl.BlockSpec((tr, W), lambda i: (i, 1)),
                  pl.BlockSpec((tr, W), lambda i: (i, 0)), row, row, wspec, bspec],
        out_specs=[pl.BlockSpec((tr, 2 * W), lambda i: (i, 0)), wspec, bspec,
                   pl.BlockSpec((8, W), lambda i: (0, 0))],
        out_shape=[jax.ShapeDtypeStruct((T, 2 * W), BF16),
                   jax.ShapeDtypeStruct((SG_GROUPS, SG_CHUNK, SG_CHUNK), F32),
                   jax.ShapeDtypeStruct((SG_GROUPS, SG_CHUNK, 1), F32),
                   jax.ShapeDtypeStruct((8, W), F32)],
        scratch_shapes=[pltpu.VMEM((tr, W), BF16), pltpu.VMEM((tr, W), F32),
                        pltpu.VMEM((tr, W), F32), pltpu.VMEM((tr, W), F32)],
        compiler_params=_cparams("arbitrary"),
    )(a_uv, a_uv, dgate, ln_g, ln_b, ws, bcol)


SUBLANES = 8


def _shift_rows(xc_sc, xs_sc):
    rows = xs_sc.shape[1]
    for p in range(1, SUBLANES):
        xs_sc[p - 1] = xc_sc[pl.ds(p, rows), :]


def _rows_at(xc_sc, xs_sc, offset, tr):
    p = offset % SUBLANES
    base = offset - p
    return xc_sc[pl.ds(base, tr), :] if p == 0 else xs_sc[p - 1, pl.ds(base, tr), :]


def _shift_scratch(tr, C):
    return pltpu.VMEM((SUBLANES - 1, tr + CONV_HALO - SUBLANES, C), F32)


def _cv_glu_conv(a_ref, b_ref, ap_ref, bp_ref, w_ref, bd_ref, xc_sc, xs_sc, tr):
    i = pl.program_id(0)
    prev = ap_ref[...] * jax.nn.sigmoid(bp_ref[...])
    xc_sc[0:CONV_HALO, :] = jnp.where(i > 0, prev, 0.0)
    xc_sc[CONV_HALO:, :] = a_ref[...] * jax.nn.sigmoid(b_ref[...])
    _shift_rows(xc_sc, xs_sc)
    acc = jnp.broadcast_to(bd_ref[...], (tr, bd_ref.shape[1]))
    for k in range(CONV_WIDTH):
        acc = acc + w_ref[k:k + 1, :] * _rows_at(xc_sc, xs_sc, CONV_HALO - (CONV_WIDTH - 1) + k, tr)
    return acc


def _cv_specs(T, C, tr):
    hb = tr // CONV_HALO
    cur = lambda col: pl.BlockSpec((tr, C), lambda i: (i, col))
    prev = lambda col: pl.BlockSpec((CONV_HALO, C), lambda i: (jnp.maximum(i * hb - 1, 0), col))
    row = pl.BlockSpec((1, C), lambda i: (0, 0))
    wspec = pl.BlockSpec((CONV_HALO, C), lambda i: (0, 0))
    return cur, prev, row, wspec


def _cv_fwd(p, w_dw, b_dw, ln_g, ln_b, *, name):
    T, C = p.shape[0], p.shape[1] // 2
    tr = _row_tile(T)
    cur, prev, row, wspec = _cv_specs(T, C, tr)

    def body(a_ref, b_ref, ap_ref, bp_ref, w_ref, bd_ref, g_ref, be_ref, o_ref, xc_sc, xs_sc):
        y2 = _cv_glu_conv(a_ref, b_ref, ap_ref, bp_ref, w_ref, bd_ref, xc_sc, xs_sc, tr)
        yhat, _ = _layer_norm_stats(y2)
        yln = yhat * g_ref[...] + be_ref[...]
        o_ref[...] = (yln * jax.nn.sigmoid(yln)).astype(BF16)

    return pl.pallas_call(
        body, name=name, grid=(T // tr,),
        in_specs=[cur(0), cur(1), prev(0), prev(1), wspec, row, row, row],
        out_specs=pl.BlockSpec((tr, C), lambda i: (i, 0)),
        out_shape=jax.ShapeDtypeStruct((T, C), BF16),
        scratch_shapes=[pltpu.VMEM((tr + CONV_HALO, C), F32), _shift_scratch(tr, C)],
        compiler_params=_cparams("parallel"),
    )(p, p, p, p, w_dw, b_dw, ln_g, ln_b)


def _cv_bwd_ln(p, dy3, w_dw, b_dw, ln_g, ln_b, *, name):
    T, C = p.shape[0], p.shape[1] // 2
    tr = _row_tile(T)
    cur, prev, row, wspec = _cv_specs(T, C, tr)

    def body(a_ref, b_ref, ap_ref, bp_ref, dy_ref, w_ref, bd_ref, g_ref, be_ref,
             dy2_ref, dw_ref, sums_ref, xc_sc, xs_sc):
        i = pl.program_id(0)
        y2 = _cv_glu_conv(a_ref, b_ref, ap_ref, bp_ref, w_ref, bd_ref, xc_sc, xs_sc, tr)
        yhat, rstd = _layer_norm_stats(y2)
        yln = yhat * g_ref[...] + be_ref[...]
        s = jax.nn.sigmoid(yln)
        dyln = dy_ref[...] * (s + yln * s * (1.0 - s))
        dy2 = _layer_norm_bwd(dyln * g_ref[...], yhat, rstd)
        dy2_ref[...] = dy2
        sums = jnp.concatenate([jnp.sum(dy2, axis=0, keepdims=True),
                                jnp.sum(dyln * yhat, axis=0, keepdims=True),
                                jnp.sum(dyln, axis=0, keepdims=True),
                                jnp.zeros((5, C), F32)], axis=0)
        taps = [jnp.sum(dy2 * _rows_at(xc_sc, xs_sc, CONV_HALO - (CONV_WIDTH - 1) + k, tr), axis=0, keepdims=True)
                for k in range(CONV_WIDTH)]
        dw = jnp.concatenate(taps + [jnp.zeros((CONV_HALO - CONV_WIDTH, C), F32)], axis=0)

        @pl.when(i == 0)
        def _():
            sums_ref[...] = sums
            dw_ref[...] = dw

        @pl.when(i > 0)
        def _():
            sums_ref[...] += sums
            dw_ref[...] += dw

    blk = pl.BlockSpec((tr, C), lambda i: (i, 0))
    return pl.pallas_call(
        body, name=name, grid=(T // tr,),
        in_specs=[cur(0), cur(1), prev(0), prev(1), blk, wspec, row, row, row],
        out_specs=[blk, wspec, pl.BlockSpec((8, C), lambda i: (0, 0))],
        out_shape=[jax.ShapeDtypeStruct((T, C), F32), jax.ShapeDtypeStruct((CONV_HALO, C), F32),
                   jax.ShapeDtypeStruct((8, C), F32)],
        scratch_shapes=[pltpu.VMEM((tr + CONV_HALO, C), F32), _shift_scratch(tr, C)],
        compiler_params=_cparams("arbitrary"),
    )(p, p, p, p, dy3, w_dw, b_dw, ln_g, ln_b)


def _cv_bwd_in(p, dy2, w_dw, *, name):
    T, C = p.shape[0], p.shape[1] // 2
    tr = _row_tile(T)
    hb = tr // CONV_HALO
    nblk = T // tr
    last_halo = T // CONV_HALO - 1

    def body(a_ref, b_ref, dy_ref, dyn_ref, w_ref, dp_ref, sums_ref, xc_sc, xs_sc):
        i = pl.program_id(0)
        xc_sc[0:tr, :] = dy_ref[...]
        xc_sc[tr:, :] = jnp.where(i < nblk - 1, dyn_ref[...], 0.0)
        _shift_rows(xc_sc, xs_sc)
        dy1 = jnp.zeros((tr, C), F32)
        for k in range(CONV_WIDTH):
            dy1 = dy1 + w_ref[k:k + 1, :] * _rows_at(xc_sc, xs_sc, CONV_WIDTH - 1 - k, tr)
        a = a_ref[...]
        sb = jax.nn.sigmoid(b_ref[...])
        da = dy1 * sb
        db = dy1 * a * sb * (1.0 - sb)
        dp_ref[:, :C] = da.astype(BF16)
        dp_ref[:, C:] = db.astype(BF16)
        sums = jnp.concatenate([
            jnp.concatenate([jnp.sum(da, axis=0, keepdims=True), jnp.sum(db, axis=0, keepdims=True)], axis=1),
            jnp.zeros((7, 2 * C), F32)], axis=0)

        @pl.when(i == 0)
        def _():
            sums_ref[...] = sums

        @pl.when(i > 0)
        def _():
            sums_ref[...] += sums

    blk = lambda col: pl.BlockSpec((tr, C), lambda i: (i, col))
    return pl.pallas_call(
        body, name=name, grid=(nblk,),
        in_specs=[blk(0), blk(1), blk(0),
                  pl.BlockSpec((CONV_HALO, C), lambda i: (jnp.minimum((i + 1) * hb, last_halo), 0)),
                  pl.BlockSpec((CONV_HALO, C), lambda i: (0, 0))],
        out_specs=[pl.BlockSpec((tr, 2 * C), lambda i: (i, 0)), pl.BlockSpec((8, 2 * C), lambda i: (0, 0))],
        out_shape=[jax.ShapeDtypeStruct((T, 2 * C), BF16), jax.ShapeDtypeStruct((8, 2 * C), F32)],
        scratch_shapes=[pltpu.VMEM((tr + CONV_HALO, C), F32), _shift_scratch(tr, C)],
        compiler_params=_cparams("arbitrary"),
    )(p, p, dy2, dy2, w_dw)


def _col_tile(n, want=1024):
    best = LANE
    for t in range(LANE, min(n, want) + 1, LANE):
        if n % t == 0:
            best = t
    return best if n % LANE == 0 else n


def _mm(a, b, *, name, ta=False, tb=False, **kw):
    M = a.shape[1] if ta else a.shape[0]
    N = b.shape[0] if tb else b.shape[1]
    K = a.shape[0] if ta else a.shape[1]
    kw.setdefault('tm', _col_tile(M, 1024 if ta else 512))
    kw.setdefault('tn', _col_tile(N, 1024))
    kw.setdefault('tk', K if tb else _col_tile(K, 1024))
    return _matmul(a, b, name=name, ta=ta, tb=tb, **kw)


def _relu2_epilogue(acc):
    r = jnp.maximum(acc, 0.0)
    return acc, r * r


def _residual_epilogue(acc, x, g):
    return acc, x + g * acc


def _residual_bias_epilogue(acc, x, g, b):
    y = acc + b
    return y, x + g * y


def _relu2_bwd_epilogue(acc, a):
    return (acc * (2.0 * jnp.maximum(a, 0.0)),)


def _bias_epilogue(acc, b):
    return (acc + b,)


def _fox_forward(h1, P, j, D, comm=None):
    H = D // HEAD_DIM
    proj = _mm(h1, P['fox_w_in'][j], name='fox_proj', b_outer=True)
    qg = jnp.tile(P['fox_q_norm'][j][None, :], (1, 2))
    kg = jnp.tile(P['fox_k_norm'][j][None, :], (1, 2))
    fpre_t = proj[:, 3 * D:3 * D + H].T
    bf = P['fox_b_f'][j][:, None]
    fcum = _fox_gate_fwd(fpre_t, bf, name='fox_gate_fwd')
    qa, qta, ka, kta, va, vta = _fox_prep_fwd(proj, qg, kg, fcum[:, :, None], d_model=D, name='fox_prep_fwd')
    o, qb, comm_outs = _fox_attn_fwd(qa, kta, va, name='fox_attn_fwd', comm=comm)
    saved = dict(proj=proj, qg=qg, kg=kg, fpre_t=fpre_t, bf=bf, o=o, qb=qb, qta=qta, ka=ka, kta=kta, vta=vta)
    return o, saved, comm_outs


def _fox_backward(dy, h1, S, P, j, D, comm=None):
    H = D // HEAD_DIM
    w_out, w_in = P['fox_w_out'][j], P['fox_w_in'][j]
    g = {}
    g['fox_w_out'] = _mm(S['o'], dy, ta=True, name='fox_dw_out')
    do = _mm(dy, w_out, tb=True, name='fox_do')
    doa, dota = _fox_do_prep(do, S['o'], name='fox_do_prep')
    dq, dkt, dvt, dcol, drow, comm_outs = _fox_attn_bwd(S['qb'], S['qta'], S['ka'], S['kta'], S['vta'], doa, dota,
                                                        name='fox_attn_bwd', comm=comm)
    dqp, dkp, dvp, gsum = _fox_prep_bwd(S['proj'], dq, dkt, dvt, S['qg'], S['kg'], d_model=D, name='fox_prep_bwd')
    dfpre_t, dbf = _fox_gate_bwd(dcol[:, 0, :], drow[:, :, 0], S['fpre_t'], S['bf'], name='fox_gate_bwd')
    dfpre = jnp.pad(dfpre_t.T.astype(BF16), ((0, 0), (0, LANE - H)))
    dproj = jnp.concatenate([dqp, dkp, dvp, dfpre], axis=1)
    g['fox_w_in'] = _mm(h1, dproj, ta=True, name='fox_dw_in')[:, :3 * D + H]
    g['fox_b_f'] = dbf[:, 0]
    g['fox_q_norm'] = gsum[0, :HEAD_DIM]
    g['fox_k_norm'] = gsum[1, :HEAD_DIM]
    dh1 = _mm(dproj, w_in, tb=True, name='fox_dh')
    return dh1, g, comm_outs


def _sg_forward(h1, P, D):
    a_uv = _mm(h1, P['sg_w_in'], name='sg_in', b_outer=True)
    bcol = P['sg_b_s'][:, :, None]
    gate = _sg_fwd(a_uv, P['sg_ln_g'], P['sg_ln_b'], P['sg_w_s'], bcol, name='sg_fwd')
    return gate, dict(a_uv=a_uv, bcol=bcol, gate=gate)


def _sg_backward(dy, h1, S, P, D):
    g = {}
    g['sg_w_out'] = _mm(S['gate'], dy, ta=True, name='sg_dw_out')
    dgate = _mm(dy, P['sg_w_out'], tb=True, name='sg_dgate')
    da, dws, dbs, sums = _sg_bwd(S['a_uv'], dgate, P['sg_ln_g'], P['sg_ln_b'], P['sg_w_s'], S['bcol'],
                                 name='sg_bwd')
    g['sg_w_s'], g['sg_b_s'] = dws, dbs[:, :, 0]
    g['sg_ln_g'], g['sg_ln_b'] = sums[0], sums[1]
    g['sg_w_in'] = _mm(h1, da, ta=True, name='sg_dw_in', out_chips=N_CHIPS)
    dh1 = _mm(da, P['sg_w_in'], tb=True, name='sg_dh')
    return dh1, g


def _cv_forward(h1, P, D):
    p = _mm(h1, P['cv_w_pw1'], name='cv_pw1', extras=[(P['cv_b_pw1'], 'row')], epilogue=_bias_epilogue,
            b_outer=True)
    w_dw = jnp.pad(P['cv_w_dw'], ((0, CONV_HALO - CONV_WIDTH), (0, 0)))
    y3 = _cv_fwd(p, w_dw, P['cv_b_dw'], P['cv_ln_g'], P['cv_ln_b'], name='cv_fwd')
    return y3, dict(p=p, w_dw=w_dw, y3=y3)


def _cv_backward(dy, h1, S, P, D):
    g = {}
    g['cv_w_pw2'] = _mm(S['y3'], dy, ta=True, name='cv_dw_pw2')
    dy3 = _mm(dy, P['cv_w_pw2'], tb=True, name='cv_dy3')
    dy2, dw, sums = _cv_bwd_ln(S['p'], dy3, S['w_dw'], P['cv_b_dw'], P['cv_ln_g'], P['cv_ln_b'], name='cv_bwd_ln')
    g['cv_w_dw'] = dw[:CONV_WIDTH]
    g['cv_b_dw'], g['cv_ln_g'], g['cv_ln_b'] = sums[0], sums[1], sums[2]
    dp, psum = _cv_bwd_in(S['p'], dy2, S['w_dw'], name='cv_bwd_in')
    g['cv_b_pw1'] = psum[0]
    g['cv_w_pw1'] = _mm(h1, dp, ta=True, name='cv_dw_pw1', out_chips=N_CHIPS)
    dh1 = _mm(dp, P['cv_w_pw1'], tb=True, name='cv_dh')
    return dh1, g


class Hooks(NamedTuple):
    fwd_comm: Comm
    fwd_done: Callable
    bwd_comm: Callable
    bwd_done: Callable


def _local_step(x, target, mod, P, hooks=None):
    T, D = x.shape
    L = mod.shape[0]
    saved = []
    for i in range(L):
        kind, j = i % N_MIXERS, i // N_MIXERS
        m = [mod[i:i + 1, k * D:(k + 1) * D] for k in range(6)]
        sh_m, sc_m, g_m, sh_f, sc_f, g_f = m
        w_mix, w_mlp = P['norm_mix'][i:i + 1], P['norm_mlp'][i:i + 1]
        h1 = _norm_mod_fwd(x, w_mix, sc_m, sh_m, name='norm_mix_fwd')
        if kind == 0:
            carried = hooks is not None and i == 0
            op, S, comm_outs = _fox_forward(h1, P, j, D, comm=hooks.fwd_comm if carried else None)
            if carried:
                hooks.fwd_done(comm_outs)
            y, x1 = _mm(op, P['fox_w_out'][j], name='fox_out', extras=[(x, 'tile'), (g_m, 'row')],
                        epilogue=_residual_epilogue, out_dtypes=(F32, F32))
        elif kind == 1:
            op, S = _sg_forward(h1, P, D)
            y, x1 = _mm(op, P['sg_w_out'], name='sg_out', extras=[(x, 'tile'), (g_m, 'row')],
                        epilogue=_residual_epilogue, out_dtypes=(F32, F32))
        else:
            op, S = _cv_forward(h1, P, D)
            y, x1 = _mm(op, P['cv_w_pw2'], name='cv_out',
                        extras=[(x, 'tile'), (g_m, 'row'), (P['cv_b_pw2'], 'row')],
                        epilogue=_residual_bias_epilogue, out_dtypes=(F32, F32))
        h2 = _norm_mod_fwd(x1, w_mlp, sc_f, sh_f, name='norm_mlp_fwd')
        a, r = _mm(h2, P['w_mlp_in'][i], name='mlp_in', epilogue=_relu2_epilogue, out_dtypes=(F32, BF16),
                   b_outer=True)
        z, x2 = _mm(r, P['w_mlp_out'][i], name='mlp_out', extras=[(x1, 'tile'), (g_f, 'row')],
                    epilogue=_residual_epilogue, out_dtypes=(F32, F32), tk=P['w_mlp_out'][i].shape[0])
        saved.append(dict(x=x, h1=h1, S=S, y=y, x1=x1, h2=h2, a=a, r=r, z=z, m=m))
        x = x2

    loss_part, dx = _loss_head(x, target, name='loss_head')

    grads = {k: [None] * L for k in ('norm_mix', 'norm_mlp')}
    mix_grads, mat = {}, {}
    dmod = [None] * L
    for i in reversed(range(L)):
        kind, j = i % N_MIXERS, i // N_MIXERS
        sv = saved[i]
        sh_m, sc_m, g_m, sh_f, sc_f, g_f = sv['m']
        w_mix, w_mlp = P['norm_mix'][i:i + 1], P['norm_mlp'][i:i + 1]
        dz, dgf = _gate_bwd(dx, sv['z'], g_f, name='mlp_gate_bwd')
        mat['w_mlp_out', i] = _mm(sv['r'], dz, ta=True, name='mlp_dw_out')
        da = _mm(dz, P['w_mlp_out'][i], tb=True, name='mlp_da', extras=[(sv['a'], 'tile')],
                 epilogue=_relu2_bwd_epilogue, out_dtypes=(BF16,), b_outer=True)
        mat['w_mlp_in', i] = _mm(sv['h2'], da, ta=True, name='mlp_dw_in', out_chips=N_CHIPS)
        dh2 = _mm(da, P['w_mlp_in'][i], tb=True, name='mlp_dh', tk=P['w_mlp_in'][i].shape[1])
        dx1, sums_f, dy = _norm_mod_bwd(dh2, sv['x1'], dx, w_mlp, sc_f, name='norm_mlp_bwd', gate=(sv['y'], g_m))
        if kind == 0:
            carried = hooks is not None and i == 0
            dh1, g, comm_outs = _fox_backward(dy, sv['h1'], sv['S'], P, j, D,
                                              comm=hooks.bwd_comm(mat) if carried else None)
            if carried:
                hooks.bwd_done(comm_outs)
        elif kind == 1:
            dh1, g = _sg_backward(dy, sv['h1'], sv['S'], P, D)
        else:
            dh1, g = _cv_backward(dy, sv['h1'], sv['S'], P, D)
            g['cv_b_pw2'] = sums_f[4]
        for k, val in g.items():
            if k in BIG:
                mat[k, j] = val
            else:
                mix_grads.setdefault(k, {})[j] = val
        dx, sums_m = _norm_mod_bwd(dh1, sv['x'], dx1, w_mix, sc_m, name='norm_mix_bwd')
        grads['norm_mlp'][i], grads['norm_mix'][i] = sums_f[2], sums_m[2]
        dmod[i] = jnp.concatenate([sums_m[0], sums_m[1], sums_f[3], sums_f[0], sums_f[1], dgf[0]])

    out = {k: jnp.stack(v) for k, v in grads.items()}
    for k, per_j in mix_grads.items():
        out[k] = jnp.stack([per_j[j] for j in sorted(per_j)])
    return loss_part, dx, jnp.stack(dmod), out, mat


def _all_gather8(blocks, *, name):
    n = len(blocks)

    def body(*refs):
        x_refs, out_refs = refs[:n], refs[n:2 * n]
        send_sems, recv_sems, local_sems = refs[2 * n:]
        x, y, c = _position()
        me, sibling = (x, y, c), (x, y, 1 - c)
        chips = [(1 - x, y), (x, 1 - y), (1 - x, 1 - y)]

        def slot(a, px, py, pc):
            return out_refs[a].at[4 * px + 2 * py + pc]

        def copy(a, k, blk, to, src=None):
            return pltpu.make_async_remote_copy(
                src_ref=slot(a, *blk) if src is None else src, dst_ref=slot(a, *blk),
                send_sem=send_sems.at[7 * a + k], recv_sem=recv_sems.at[7 * a + k],
                device_id=to, device_id_type=MESH)

        mine = [pltpu.make_async_copy(x_refs[a], slot(a, *me), local_sems.at[a]) for a in range(n)]
        for cp in mine:
            cp.start()
        first = []
        for j, chip in enumerate(chips):
            first += [copy(a, 1 + j, me, (*chip, c), src=x_refs[a]) for a in range(n)]
        first += [copy(a, 0, me, sibling, src=x_refs[a]) for a in range(n)]
        for cp in first:
            cp.start()
        passed = []
        for j, chip in enumerate(chips):
            for a in range(n):
                copy(a, 1 + j, (*chip, c), me).wait_recv()
                passed.append(copy(a, 4 + j, (*chip, c), sibling))
                passed[-1].start()
        for a in range(n):
            copy(a, 0, sibling, me).wait_recv()
        for j, chip in enumerate(chips):
            for a in range(n):
                copy(a, 4 + j, (*chip, 1 - c), me).wait_recv()
        for cp in first + passed:
            cp.wait_send()
        for cp in mine:
            cp.wait()

    return pl.pallas_call(
        body, name=name, in_specs=[ANY] * n, out_specs=[ANY] * n,
        out_shape=[jax.ShapeDtypeStruct((8,) + b.shape, b.dtype) for b in blocks],
        scratch_shapes=[pltpu.SemaphoreType.DMA((7 * n,)), pltpu.SemaphoreType.DMA((7 * n,)),
                        pltpu.SemaphoreType.DMA((n,))],
    )(*blocks)


def _exchange(comm, *, name, aliases=None):
    ns, no = len(comm.srcs), len(comm.out_shapes)

    def body(*refs):
        copies = _comm_copies(comm.plan, refs[:ns], refs[ns:ns + no], *refs[ns + no:])
        _comm_start(copies)
        _comm_wait(copies)

    return pl.pallas_call(
        body, name=name, in_specs=[ANY] * ns, out_specs=[ANY] * no, out_shape=list(comm.out_shapes),
        scratch_shapes=comm.scratch(), input_output_aliases=aliases or {},
    )(*comm.srcs)


def _gather_comm(halves):
    n = len(halves)

    def plan(src, out, x, y, c):
        mine = 4 * x + 2 * y + c
        remote = [(src[a], out[a].at[mine], (x, y, 1 - c), out[a].at[4 * x + 2 * y + 1 - c]) for a in range(n)]
        for fx, fy in CHIP_FLIPS:
            px, py = _flip(x, fx), _flip(y, fy)
            remote += [(src[a], out[a].at[mine], (px, py, c), out[a].at[4 * px + 2 * py + c]) for a in range(n)]
        return remote, [(src[a], out[a].at[mine]) for a in range(n)]

    return Comm(list(halves), [jax.ShapeDtypeStruct((8,) + h.shape, h.dtype) for h in halves], plan, 4 * n, n)


def _gather_forward(bufs, *, name):
    n = len(bufs)

    def plan(src, out, x, y, c):
        remote = []
        for fx, fy in CHIP_FLIPS:
            px, py = _flip(x, fx), _flip(y, fy)
            remote += [(src[a].at[4 * px + 2 * py + c], out[a].at[4 * px + 2 * py + c], (x, y, 1 - c),
                        out[a].at[4 * px + 2 * py + 1 - c]) for a in range(n)]
        return remote, []

    comm = Comm(list(bufs), [jax.ShapeDtypeStruct(b.shape, b.dtype) for b in bufs], plan, 3 * n, 0)
    return _exchange(comm, name=name, aliases={a: a for a in range(n)})


CHIP_FLIPS = ((1, 0), (0, 1), (1, 1))


def _flip(v, f):
    return 1 - v if f else v


def _sum_rows_tile(R, C, budget=3 << 20):
    best = None
    for t in range(8, R + 1, 8):
        if R % t == 0 and t * C * 4 <= budget:
            best = t
    return best if best is not None else R


def _rs_begin(gps, *, wire_dtype):
    n = len(gps)
    c_arr = jnp.reshape(_position()[2], (1,)).astype(jnp.int32)

    def plan(src, out, x, y, c):
        return [(src[a].at[b, 1 - c], out[a].at[b], (x, y, 1 - c), out[a].at[b])
                for a in range(n) for b in range(4)], []

    got1 = _exchange(Comm(list(gps), [jax.ShapeDtypeStruct((4,) + g.shape[2:], F32) for g in gps], plan, 4 * n, 0),
                     name='rs_sibling')

    def sum_chip(c_ref, mine_ref, got_ref, out_ref):
        out_ref[...] = (mine_ref[...] + got_ref[...]).astype(out_ref.dtype)

    parts = []
    for gp, g1 in zip(gps, got1):
        _, _, R, C = gp.shape
        tr = _sum_rows_tile(R, C)
        parts.append(pl.pallas_call(
            sum_chip, name='rs_sum_chip',
            grid_spec=pltpu.PrefetchScalarGridSpec(
                num_scalar_prefetch=1, grid=(4, R // tr),
                in_specs=[pl.BlockSpec((None, None, tr, C), lambda b, r, cr: (b, cr[0], r, 0)),
                          pl.BlockSpec((None, tr, C), lambda b, r, cr: (b, r, 0))],
                out_specs=pl.BlockSpec((None, tr, C), lambda b, r, cr: (b, r, 0))),
            out_shape=jax.ShapeDtypeStruct((4, R, C), wire_dtype),
            compiler_params=_cparams("parallel", "parallel"),
        )(c_arr, gp, g1))
    return got1, parts


def _rs_chips_comm(parts):
    n = len(parts)

    def plan(src, out, x, y, c):
        remote = []
        for k, (fx, fy) in enumerate(CHIP_FLIPS):
            px, py = _flip(x, fx), _flip(y, fy)
            remote += [(src[a].at[2 * px + py], out[a].at[k], (px, py, c), out[a].at[k]) for a in range(n)]
        return remote, []

    return Comm(list(parts), [jax.ShapeDtypeStruct((3,) + p.shape[1:], p.dtype) for p in parts], plan, 3 * n, 0)


def _rs_finish(gps, got1, got2):
    n = len(gps)
    x, y, c = _position()
    bc_arr = jnp.stack([2 * x + y, c]).astype(jnp.int32)

    def sum_final(bc_ref, mine_ref, got1_ref, got2_ref, out_ref):
        acc = mine_ref[...] + got1_ref[...]
        for k in range(3):
            acc = acc + got2_ref[k].astype(F32)
        out_ref[...] = acc

    halves = []
    for gp, g1, g2 in zip(gps, got1, got2):
        _, _, R, C = gp.shape
        tr = _sum_rows_tile(R, C, budget=2 << 20)
        halves.append(pl.pallas_call(
            sum_final, name='rs_sum_final',
            grid_spec=pltpu.PrefetchScalarGridSpec(
                num_scalar_prefetch=1, grid=(R // tr,),
                in_specs=[pl.BlockSpec((None, None, tr, C), lambda r, bc: (bc[0], bc[1], r, 0)),
                          pl.BlockSpec((None, tr, C), lambda r, bc: (bc[0], r, 0)),
                          pl.BlockSpec((3, tr, C), lambda r, bc: (0, r, 0))],
                out_specs=pl.BlockSpec((None, tr, C), lambda r, bc: (bc[1], r, 0))),
            out_shape=jax.ShapeDtypeStruct((2, R, C), F32),
            compiler_params=_cparams("parallel"),
        )(bc_arr, gp, g1, g2))

    def plan(src, out, x, y, c):
        return [(src[a].at[c], out[a].at[c], (x, y, 1 - c), out[a].at[1 - c]) for a in range(n)], []

    comm = Comm(halves, [jax.ShapeDtypeStruct(h.shape, F32) for h in halves], plan, n, 0)
    return _exchange(comm, name='rs_swap', aliases={a: a for a in range(n)})


def _sum8(gathered, *, name):
    _, R, C = gathered.shape

    def body(g_ref, o_ref):
        acc = g_ref[0]
        for k in range(1, 8):
            acc = acc + g_ref[k]
        o_ref[...] = acc

    return pl.pallas_call(body, name=name, out_shape=jax.ShapeDtypeStruct((R, C), F32))(gathered)


def _adamw(w, g, m, v, *, name):
    shape = w.shape
    cols = shape[-1]
    rows = w.size // cols
    tr = _sum_rows_tile(rows, cols, budget=1 << 20)

    def body(w_ref, g_ref, m_ref, v_ref, d_ref, mo_ref, vo_ref):
        gv = g_ref[...]
        mn = ADAM_B1 * m_ref[...] + (1.0 - ADAM_B1) * gv
        vn = ADAM_B2 * v_ref[...] + (1.0 - ADAM_B2) * (gv * gv)
        m_hat = mn / (1.0 - ADAM_B1 ** ADAM_STEP)
        v_hat = vn / (1.0 - ADAM_B2 ** ADAM_STEP)
        d_ref[...] = -ADAM_LR * (m_hat / (jnp.sqrt(v_hat) + ADAM_EPS) + ADAM_WD * w_ref[...])
        mo_ref[...] = mn
        vo_ref[...] = vn

    blk = pl.BlockSpec((tr, cols), lambda i: (i, 0))
    outs = pl.pallas_call(
        body, name=name, grid=(rows // tr,), in_specs=[blk] * 4, out_specs=[blk] * 3,
        out_shape=[jax.ShapeDtypeStruct((rows, cols), F32)] * 3,
        compiler_params=_cparams("parallel"),
    )(*[a.reshape(rows, cols) for a in (w, g, m, v)])
    return tuple(o.reshape(shape) for o in outs)


WEIGHTS = ['norm_mix', 'norm_mlp', 'w_ada', 'b_ada', 'w_mlp_in', 'w_mlp_out', 'fox_w_in', 'fox_b_f',
           'fox_q_norm', 'fox_k_norm', 'fox_w_out', 'sg_w_in', 'sg_ln_g', 'sg_ln_b', 'sg_w_s', 'sg_b_s',
           'sg_w_out', 'cv_w_pw1', 'cv_b_pw1', 'cv_w_dw', 'cv_b_dw', 'cv_ln_g', 'cv_ln_b', 'cv_w_pw2',
           'cv_b_pw2']
BIG = {'w_mlp_in': 2, 'w_mlp_out': 1, 'fox_w_in': 2, 'fox_w_out': 1, 'sg_w_in': 2, 'sg_w_out': 1,
       'cv_w_pw1': 2, 'cv_w_pw2': 1}
SMALL_SHARDED = ['cv_b_pw1', 'cv_w_dw', 'cv_b_dw', 'cv_ln_g', 'cv_ln_b', 'cv_b_pw2']
SMALL_GRADS = ['norm_mix', 'norm_mlp', 'fox_b_f', 'fox_q_norm', 'fox_k_norm', 'sg_ln_g', 'sg_ln_b', 'sg_w_s',
               'sg_b_s'] + SMALL_SHARDED
GRAD_WIRE_DTYPE = BF16


def _pack_rows(parts, cols):
    flat = jnp.concatenate([p.reshape(-1) for p in parts])
    rows = -(-flat.size // (8 * cols)) * 8
    return jnp.pad(flat, (0, rows * cols - flat.size)).reshape(rows, cols)


def _unpack(flat, shapes):
    out, off = [], 0
    for s in shapes:
        n = math.prod(s)
        out.append(flat[..., off:off + n].reshape(flat.shape[:-1] + tuple(s)))
        off += n
    return out


def _merge_chips(a, axis):
    a = jnp.moveaxis(a, 0, axis)
    return a.reshape(a.shape[:axis] + (a.shape[axis] * a.shape[axis + 1],) + a.shape[axis + 2:])


def _split_chips(a, axis):
    a = a.reshape(a.shape[:axis] + (4, a.shape[axis] // 4) + a.shape[axis + 1:])
    return jnp.moveaxis(a, axis, 0)


def _step(a):
    x, y, c = _position()
    me = 4 * x + 2 * y + c
    chip = 2 * x + y
    T, D = a['x'].shape[1], a['x'].shape[2]
    L = a['norm_mix'].shape[0]

    small_shapes = [(D,)] + [a[n].shape for n in SMALL_SHARDED]
    small = _all_gather8([_pack_rows([a['c']] + [a[n] for n in SMALL_SHARDED], LANE)], name='ag_small')[0]
    small = small.reshape(8, -1)
    c_all = _unpack(small, small_shapes[:1])[0]
    sharded = _unpack(small[0::2, D:], small_shapes[1:])
    P = {n: _merge_chips(v, v.ndim - 2) for n, v in zip(SMALL_SHARDED, sharded)}

    c_act = _silu_rows(c_all, name='c_act')
    mod_cols = jnp.stack([
        _mm(c_act, a['w_ada'][i], name='ada_mod', tm=8, tn=_col_tile(a['w_ada'].shape[2], 768),
            extras=[(lax.dynamic_slice_in_dim(a['b_ada'][i:i + 1], chip * a['w_ada'].shape[2],
                                              a['w_ada'].shape[2], axis=1), 'row')],
            epilogue=_bias_epilogue)
        for i in range(L)])
    mod_all = _all_gather8([mod_cols.reshape(L * 8, -1)], name='ag_mod')[0].reshape(8, L, 8, -1)
    mod = lax.dynamic_index_in_dim(mod_all[0::2], me, axis=2, keepdims=False)
    mod = jnp.moveaxis(mod, 0, 1).reshape(L, 6 * D)

    units = _matrix_units(L)
    first, later = units[:1], units[1:]
    last, earlier = units[:2], units[2:]
    n_heads = D // HEAD_DIM

    def half_block(unit):
        blk = a[unit[0]][unit[1]]
        return lax.dynamic_index_in_dim(blk.astype(BF16).reshape(2, blk.shape[0] // 2, blk.shape[1]), c, axis=0,
                                        keepdims=False)

    def install(group, gathered):
        for (name, idx), gth in zip(group, gathered):
            blocks = gth.reshape((4,) + a[name].shape[1:])
            if name == 'fox_w_in':
                pad = jnp.zeros((blocks.shape[1], LANE - n_heads), BF16)
                full = jnp.concatenate([blocks[0], blocks[1], blocks[2], blocks[3], pad], axis=-1)
            else:
                full = _merge_chips(blocks, BIG[name] - 1)
            if name in ('w_mlp_in', 'w_mlp_out', 'fox_w_in', 'fox_w_out'):
                P.setdefault(name, {})[idx] = full
            else:
                P[name] = full

    install(first, _all_gather8([half_block(u) for u in first], name='ag_weights_first'))
    for n in ('sg_w_s', 'sg_b_s', 'cv_w_dw'):
        P[n] = (P[n] if n in P else a[n])[0]
    for n in ('norm_mix', 'norm_mlp', 'fox_b_f', 'fox_q_norm', 'fox_k_norm', 'sg_ln_g', 'sg_ln_b'):
        P[n] = a[n]

    def split_grad(unit, grad):
        name = unit[0]
        if name == 'fox_w_in':
            grad = grad[:, :a[name].shape[2] * N_CHIPS]
        blk = grad if grad.ndim == 3 else _split_chips(grad, BIG[name] - 1)
        return blk.reshape(N_CHIPS, 2, blk.shape[1] // 2, blk.shape[2])

    state = {}

    def fwd_done(outs):
        install(later, _gather_forward(outs, name='ag_weights_forward'))

    def bwd_comm(mat):
        state['gps'] = [split_grad(u, mat[u]) for u in earlier]
        state['got1'], parts = _rs_begin(state['gps'], wire_dtype=GRAD_WIRE_DTYPE)
        return _rs_chips_comm(parts)

    def bwd_done(outs):
        state['got2'] = outs

    hooks = Hooks(_gather_comm([half_block(u) for u in later]), fwd_done, bwd_comm, bwd_done)
    loss_part, grad_x, dmod, g, mat = _local_step(a['x'][0], a['loss_target'][0], mod, P, hooks)

    small_g = [dmod, loss_part[0:1, 0:1]] + [g[n] for n in SMALL_GRADS]
    small_g_shapes = [s.shape for s in small_g]
    all_small = _all_gather8([_pack_rows(small_g, LANE)], name='ag_small_grads')[0]
    summed = _sum8(all_small, name='sum_small_grads').reshape(-1)
    sums = _unpack(summed, small_g_shapes)
    loss = sums[1][0, 0]
    grads = dict(zip(SMALL_GRADS, sums[2:]))
    grads['b_ada'] = sums[0]
    for n in SMALL_SHARDED:
        blk = a[n].shape[-1]
        grads[n] = lax.dynamic_slice_in_dim(grads[n], chip * blk, blk, axis=grads[n].ndim - 1)
    dmod_all = all_small.reshape(8, -1)[:, :dmod.size].reshape(8, L, 6 * D)
    cols = a['w_ada'].shape[2]
    dmod_cols = lax.dynamic_slice_in_dim(dmod_all, chip * cols, cols, axis=2)
    pad8 = lambda t: jnp.pad(t, ((0, LANE - 8), (0, 0)))
    c_act_pad = pad8(c_act)
    grads['w_ada'] = jnp.stack([
        _mm(c_act_pad, pad8(dmod_cols[:, i]), ta=True, name='ada_dw', tn=_col_tile(cols, 768))
        for i in range(L)])

    shards = dict(zip(earlier, _rs_finish(state['gps'], state['got1'], state['got2'])))
    gps = [split_grad(u, mat[u]) for u in last]
    got1, parts = _rs_begin(gps, wire_dtype=GRAD_WIRE_DTYPE)
    got2 = _exchange(_rs_chips_comm(parts), name='rs_chips')
    shards.update(zip(last, _rs_finish(gps, got1, got2)))
    for n in BIG:
        grads[n] = jnp.stack([shards[n, idx].reshape(a[n].shape[1:]) for idx in range(a[n].shape[0])])

    deltas, new_m, new_v = {}, {}, {}
    for n in WEIGHTS:
        deltas[n], new_m[n], new_v[n] = _adamw(a[n], grads[n], a['m_' + n], a['v_' + n], name='adamw')
    return (loss, grad_x[None], *[grads[n] for n in WEIGHTS], *[deltas[n] for n in WEIGHTS],
            *[new_m[n] for n in WEIGHTS], *[new_v[n] for n in WEIGHTS])


def _matrix_units(n_layers):
    mixers = (('fox_w_in', 'fox_w_out'), ('sg_w_in', 'sg_w_out'), ('cv_w_pw1', 'cv_w_pw2'))
    units = []
    for i in range(n_layers):
        units += [(n, i // N_MIXERS) for n in mixers[i % N_MIXERS]] + [('w_mlp_in', i), ('w_mlp_out', i)]
    return units


def _silu_rows(x, *, name):
    def body(x_ref, o_ref):
        xv = x_ref[...]
        o_ref[...] = (xv * jax.nn.sigmoid(xv)).astype(BF16)

    return pl.pallas_call(body, name=name, out_shape=jax.ShapeDtypeStruct(x.shape, BF16))(x)


def kernel(x, c, norm_mix, norm_mlp, w_ada, b_ada, w_mlp_in, w_mlp_out, fox_w_in, fox_b_f, fox_q_norm, fox_k_norm, fox_w_out, sg_w_in, sg_ln_g, sg_ln_b, sg_w_s, sg_b_s, sg_w_out, cv_w_pw1, cv_b_pw1, cv_w_dw, cv_b_dw, cv_ln_g, cv_ln_b, cv_w_pw2, cv_b_pw2, loss_target, m_norm_mix, m_norm_mlp, m_w_ada, m_b_ada, m_w_mlp_in, m_w_mlp_out, m_fox_w_in, m_fox_b_f, m_fox_q_norm, m_fox_k_norm, m_fox_w_out, m_sg_w_in, m_sg_ln_g, m_sg_ln_b, m_sg_w_s, m_sg_b_s, m_sg_w_out, m_cv_w_pw1, m_cv_b_pw1, m_cv_w_dw, m_cv_b_dw, m_cv_ln_g, m_cv_ln_b, m_cv_w_pw2, m_cv_b_pw2, v_norm_mix, v_norm_mlp, v_w_ada, v_b_ada, v_w_mlp_in, v_w_mlp_out, v_fox_w_in, v_fox_b_f, v_fox_q_norm, v_fox_k_norm, v_fox_w_out, v_sg_w_in, v_sg_ln_g, v_sg_ln_b, v_sg_w_s, v_sg_b_s, v_sg_w_out, v_cv_w_pw1, v_cv_b_pw1, v_cv_w_dw, v_cv_b_dw, v_cv_ln_g, v_cv_ln_b, v_cv_w_pw2, v_cv_b_pw2):
    return _step(dict(locals()))
```

```python
import math
from typing import Callable, NamedTuple

import jax
import jax.numpy as jnp
from jax import lax
from jax.experimental import pallas as pl
from jax.experimental.pallas import tpu as pltpu

F32 = jnp.float32
BF16 = jnp.bfloat16

EPS = 1e-6
HEAD_DIM = 64
LANE = 128
CONV_WIDTH = 31
CONV_HALO = 32
SG_CHUNK = 128
SG_CAUSAL = 64
SG_GROUPS = 8
N_MIXERS = 3
N_CHIPS = 4
VMEM_LIMIT = 56 * 1024 * 1024
NEG = -1e30

ADAM_LR = 0.001
ADAM_B1 = 0.9
ADAM_B2 = 0.999
ADAM_EPS = 1e-08
ADAM_WD = 0.01
ADAM_STEP = 10

MESH = pl.DeviceIdType.MESH
ANY = pl.BlockSpec(memory_space=pl.ANY)


def _cparams(*sem):
    return pltpu.CompilerParams(dimension_semantics=sem, vmem_limit_bytes=VMEM_LIMIT)


def _row_tile(t, want=512):
    return min(t, want)


def _matmul(a, b, *, name, ta=False, tb=False, tm=512, tn=1024, tk=1024,
            extras=(), epilogue=None, out_dtypes=(F32,), b_outer=False, out_chips=None):
    M, K = (a.shape[1], a.shape[0]) if ta else a.shape
    N = b.shape[0] if tb else b.shape[1]
    assert (b.shape[1] if tb else b.shape[0]) == K
    n_own = N // out_chips if out_chips else N
    tm, tn, tk = min(tm, M), min(tn, n_own), min(tk, K)
    assert M % tm == 0 and n_own % tn == 0 and K % tk == 0, (name, M, N, K, tm, tn, tk)
    nk = K // tk

    def spec(shape, pick):
        if b_outer:
            return pl.BlockSpec(shape, lambda j, i, k: pick(i, j, k))
        return pl.BlockSpec(shape, pick)

    a_spec = spec((tk, tm), lambda i, j, k: (k, i)) if ta else spec((tm, tk), lambda i, j, k: (i, k))
    b_spec = spec((tn, tk), lambda i, j, k: (j, k)) if tb else spec((tk, tn), lambda i, j, k: (k, j))
    ex_specs = [spec((tm, tn), lambda i, j, k: (i, j)) if kind == 'tile' else spec((1, tn), lambda i, j, k: (0, j))
                for _, kind in extras]
    dims = (((0,) if ta else (1,), (1,) if tb else (0,)), ((), ()))
    n_ex, n_out = len(extras), len(out_dtypes)

    def body(*refs):
        a_ref, b_ref = refs[0], refs[1]
        ex = refs[2:2 + n_ex]
        outs = refs[2 + n_ex:2 + n_ex + n_out]

        def finish(acc):
            vals = epilogue(acc, *[r[...] for r in ex]) if epilogue else (acc,)
            for o, v in zip(outs, vals):
                o[...] = v.astype(o.dtype)

        part = lax.dot_general(a_ref[...].astype(BF16), b_ref[...].astype(BF16), dims,
                               preferred_element_type=F32)
        if nk == 1:
            finish(part)
        else:
            acc_ref = refs[-1]
            k = pl.program_id(2)

            @pl.when(k == 0)
            def _():
                acc_ref[...] = part

            @pl.when(k > 0)
            def _():
                acc_ref[...] += part

            @pl.when(k == nk - 1)
            def _():
                finish(acc_ref[...])

    outs = pl.pallas_call(
        body, name=name,
        grid=(N // tn, M // tm, nk) if b_outer else (M // tm, N // tn, nk),
        in_specs=[a_spec, b_spec] + ex_specs,
        out_specs=[spec((None, tm, tn), lambda i, j, k: (j // (n_own // tn), i, j % (n_own // tn)))
                   if out_chips else spec((tm, tn), lambda i, j, k: (i, j)) for _ in out_dtypes],
        out_shape=[jax.ShapeDtypeStruct((out_chips, M, n_own) if out_chips else (M, N), dt) for dt in out_dtypes],
        scratch_shapes=[pltpu.VMEM((tm, tn), F32)] if nk > 1 else [],
        compiler_params=_cparams("parallel", "parallel", "arbitrary"),
    )(a, b, *[arr for arr, _ in extras])
    return outs if n_out > 1 else outs[0]


def _norm_mod_fwd(x, w, sc, sh, *, name):
    T, D = x.shape
    tr = _row_tile(T)

    def body(x_ref, w_ref, sc_ref, sh_ref, h_ref):
        xv = x_ref[...]
        r = lax.rsqrt(jnp.mean(xv * xv, axis=-1, keepdims=True) + EPS)
        h_ref[...] = ((xv * r) * w_ref[...] * (1.0 + sc_ref[...]) + sh_ref[...]).astype(BF16)

    row = pl.BlockSpec((1, D), lambda i: (0, 0))
    return pl.pallas_call(
        body, name=name, grid=(T // tr,),
        in_specs=[pl.BlockSpec((tr, D), lambda i: (i, 0)), row, row, row],
        out_specs=pl.BlockSpec((tr, D), lambda i: (i, 0)),
        out_shape=jax.ShapeDtypeStruct((T, D), BF16),
        compiler_params=_cparams("parallel"),
    )(x, w, sc, sh)


def _norm_mod_bwd(dh, x, dres, w, sc, *, name, gate=None):
    T, D = x.shape
    tr = _row_tile(T)
    with_gate = gate is not None

    def body(*refs):
        if with_gate:
            dh_ref, x_ref, dres_ref, w_ref, sc_ref, y_ref, g_ref, dx_ref, sums_ref, dy_ref = refs
        else:
            dh_ref, x_ref, dres_ref, w_ref, sc_ref, dx_ref, sums_ref = refs
        i = pl.program_id(0)
        xv, dhv = x_ref[...], dh_ref[...].astype(F32)
        r = lax.rsqrt(jnp.mean(xv * xv, axis=-1, keepdims=True) + EPS)
        n = xv * r
        wv, scale = w_ref[...], 1.0 + sc_ref[...]
        dn = dhv * (wv * scale)
        dx = dres_ref[...] + r * (dn - n * jnp.mean(dn * n, axis=-1, keepdims=True))
        dx_ref[...] = dx
        rows = [jnp.sum(dhv, axis=0, keepdims=True),
                jnp.sum(dhv * (n * wv), axis=0, keepdims=True),
                jnp.sum(dhv * n * scale, axis=0, keepdims=True)]
        if with_gate:
            dy_ref[...] = (dx * g_ref[...]).astype(BF16)
            rows.append(jnp.sum(dx * y_ref[...], axis=0, keepdims=True))
            rows.append(jnp.sum(dx * g_ref[...], axis=0, keepdims=True))
        part = jnp.concatenate(rows + [jnp.zeros((8 - len(rows), D), F32)], axis=0)

        @pl.when(i == 0)
        def _():
            sums_ref[...] = part

        @pl.when(i > 0)
        def _():
            sums_ref[...] += part

    blk = pl.BlockSpec((tr, D), lambda i: (i, 0))
    row = pl.BlockSpec((1, D), lambda i: (0, 0))
    in_specs = [blk, blk, blk, row, row]
    args = [dh, x, dres, w, sc]
    out_specs = [blk, pl.BlockSpec((8, D), lambda i: (0, 0))]
    out_shape = [jax.ShapeDtypeStruct((T, D), F32), jax.ShapeDtypeStruct((8, D), F32)]
    if with_gate:
        in_specs += [blk, row]
        args += list(gate)
        out_specs.append(blk)
        out_shape.append(jax.ShapeDtypeStruct((T, D), BF16))
    return pl.pallas_call(
        body, name=name, grid=(T // tr,), in_specs=in_specs, out_specs=out_specs,
        out_shape=out_shape, compiler_params=_cparams("arbitrary"),
    )(*args)


def _gate_bwd(dx, y, g, *, name):
    T, D = dx.shape
    tr = _row_tile(T)

    def body(dx_ref, y_ref, g_ref, dy_ref, dg_ref):
        i = pl.program_id(0)
        dxv = dx_ref[...]
        dy_ref[...] = (dxv * g_ref[...]).astype(BF16)
        part = jnp.concatenate([jnp.sum(dxv * y_ref[...], axis=0, keepdims=True),
                                jnp.zeros((7, D), F32)], axis=0)

        @pl.when(i == 0)
        def _():
            dg_ref[...] = part

        @pl.when(i > 0)
        def _():
            dg_ref[...] += part

    blk = pl.BlockSpec((tr, D), lambda i: (i, 0))
    return pl.pallas_call(
        body, name=name, grid=(T // tr,),
        in_specs=[blk, blk, pl.BlockSpec((1, D), lambda i: (0, 0))],
        out_specs=[blk, pl.BlockSpec((8, D), lambda i: (0, 0))],
        out_shape=[jax.ShapeDtypeStruct((T, D), BF16), jax.ShapeDtypeStruct((8, D), F32)],
        compiler_params=_cparams("arbitrary"),
    )(dx, y, g)


def _loss_head(y, target, *, name):
    T, D = y.shape
    tr = _row_tile(T)

    def body(y_ref, t_ref, loss_ref, dy_ref):
        i = pl.program_id(0)
        e = y_ref[...] - t_ref[...]
        dy_ref[...] = e * (1.0 / D)
        part = jnp.full((8, LANE), 0.5 / D * jnp.sum(e * e), F32)

        @pl.when(i == 0)
        def _():
            loss_ref[...] = part

        @pl.when(i > 0)
        def _():
            loss_ref[...] += part

    blk = pl.BlockSpec((tr, D), lambda i: (i, 0))
    return pl.pallas_call(
        body, name=name, grid=(T // tr,), in_specs=[blk, blk],
        out_specs=[pl.BlockSpec((8, LANE), lambda i: (0, 0)), blk],
        out_shape=[jax.ShapeDtypeStruct((8, LANE), F32), jax.ShapeDtypeStruct((T, D), F32)],
        compiler_params=_cparams("arbitrary"),
    )(y, target)


def _position():
    return lax.axis_index("x"), lax.axis_index("y"), lax.axis_index("c")


class Comm(NamedTuple):
    srcs: list
    out_shapes: list
    plan: Callable
    n_remote: int
    n_local: int

    def scratch(self):
        return [pltpu.SemaphoreType.DMA((self.n_remote,)), pltpu.SemaphoreType.DMA((self.n_remote,)),
                pltpu.SemaphoreType.DMA((max(self.n_local, 1),))]


def _comm_copies(plan, src_refs, out_refs, send_sems, recv_sems, local_sems):
    x, y, c = _position()
    remote, local = plan(src_refs, out_refs, x, y, c)

    def copy(k, s, d, peer):
        return pltpu.make_async_remote_copy(src_ref=s, dst_ref=d, send_sem=send_sems.at[k],
                                            recv_sem=recv_sems.at[k], device_id=peer, device_id_type=MESH)

    sends = [copy(k, s, d, peer) for k, (s, d, peer, _) in enumerate(remote)]
    recvs = [copy(k, s, landing, peer) for k, (s, _, peer, landing) in enumerate(remote)]
    local_copies = [pltpu.make_async_copy(s, d, local_sems.at[i]) for i, (s, d) in enumerate(local)]
    return sends, recvs, local_copies


def _comm_start(copies):
    sends, _, local_copies = copies
    for cp in local_copies + sends:
        cp.start()


def _comm_wait(copies):
    sends, recvs, local_copies = copies
    for cp in recvs:
        cp.wait_recv()
    for cp in sends:
        cp.wait_send()
    for cp in local_copies:
        cp.wait()


def _split_comm_refs(refs, n_in, n_out, n_scratch, comm):
    ns, nd = (len(comm.srcs), len(comm.out_shapes)) if comm else (0, 0)
    cuts = [n_in, ns, n_out, nd, n_scratch]
    parts, at = [], 0
    for n in cuts:
        parts.append(refs[at:at + n])
        at += n
    return (*parts, refs[at:])


AUG_F = HEAD_DIM
AUG_LSE = HEAD_DIM + 6


def _half_cols(x, lo):
    return (jnp.sum(jnp.where(lo, x, 0.0), axis=-1, keepdims=True),
            jnp.sum(jnp.where(lo, 0.0, x), axis=-1, keepdims=True))


def _half_sums(x, lo):
    s_lo, s_hi = _half_cols(x, lo)
    return jnp.where(lo, s_lo, s_hi)


def _split3(x):
    a = x.astype(BF16).astype(F32)
    r = x - a
    b = r.astype(BF16).astype(F32)
    return a, b, (r - b).astype(BF16).astype(F32)


def _aug(lane, base, terms):
    out = jnp.zeros(lane.shape, F32)
    for i, t in enumerate(terms):
        out = jnp.where(lane == base + i, t, out)
    return out


def _head_lanes(x2, h):
    return x2 if h == 0 else pltpu.roll(x2, HEAD_DIM, 1)


def _fox_prep_fwd(proj, qg, kg, fcol, *, d_model, name):
    T = proj.shape[0]
    nhp = d_model // LANE
    tr = _row_tile(T)

    def body(q_ref, k_ref, v_ref, qg_ref, kg_ref, f_ref, qa_ref, qta_ref, ka_ref, kta_ref, va_ref, vta_ref):
        lane = lax.broadcasted_iota(jnp.int32, (tr, LANE), 1)
        lo = lane < HEAD_DIM

        def norm(xv, g):
            ms = _half_sums(xv * xv, lo) * (1.0 / HEAD_DIM)
            return (xv * lax.rsqrt(ms + EPS)) * g

        qn = norm(q_ref[...], qg_ref[...]) * (HEAD_DIM ** -0.5)
        kn = norm(k_ref[...], kg_ref[...])
        vv = v_ref[...]
        qa, ka, va, vta = [], [], [], []
        for h in range(2):
            f1, f2, f3 = _split3(f_ref[h])
            qa.append(jnp.where(lo, _head_lanes(qn, h), _aug(lane, AUG_F, [f1, f2, f3, 1.0, 1.0, 1.0])))
            ka.append(jnp.where(lo, _head_lanes(kn, h),
                                _aug(lane, AUG_F, [1.0, 1.0, 1.0, -f1, -f2, -f3, 1.0, 1.0, 1.0])))
            va.append(jnp.where(lo if h == 0 else jnp.logical_not(lo), vv, 0.0))
            vta.append(jnp.where(lo, _head_lanes(vv, h), _aug(lane, AUG_F, [1.0, 1.0, 1.0])))
        for parts, ref, tref in ((qa, qa_ref, qta_ref), (ka, ka_ref, kta_ref), (va, va_ref, None),
                                 (vta, None, vta_ref)):
            both = jnp.concatenate(parts, axis=1)
            if ref is not None:
                ref[...] = both.astype(BF16)
            if tref is not None:
                tref[...] = both.astype(BF16).T

    gain = pl.BlockSpec((1, LANE), lambda i, h: (0, 0))
    rows = pl.BlockSpec((tr, 2 * LANE), lambda i, h: (i, h))
    cols = pl.BlockSpec((2 * LANE, tr), lambda i, h: (h, i))
    wide, tall = jax.ShapeDtypeStruct((T, 2 * d_model), BF16), jax.ShapeDtypeStruct((2 * d_model, T), BF16)
    return pl.pallas_call(
        body, name=name, grid=(T // tr, nhp),
        in_specs=[pl.BlockSpec((tr, LANE), lambda i, h: (i, h)),
                  pl.BlockSpec((tr, LANE), lambda i, h: (i, nhp + h)),
                  pl.BlockSpec((tr, LANE), lambda i, h: (i, 2 * nhp + h)), gain, gain,
                  pl.BlockSpec((2, tr, 1), lambda i, h: (h, i, 0))],
        out_specs=[rows, cols, rows, cols, rows, cols],
        out_shape=[wide, tall, wide, tall, wide, tall],
        compiler_params=_cparams("parallel", "parallel"),
    )(proj, proj, proj, qg, kg, fcol)


def _fox_do_prep(do, o, *, name):
    T, D = do.shape
    nhp = D // LANE
    tr = _row_tile(T)

    def body(do_ref, o_ref, doa_ref, dota_ref):
        lane = lax.broadcasted_iota(jnp.int32, (tr, LANE), 1)
        lo = lane < HEAD_DIM
        dob = do_ref[...].astype(BF16).astype(F32)
        deltas = _half_cols(dob * o_ref[...], lo)
        both = jnp.concatenate(
            [jnp.where(lo, _head_lanes(dob, h), _aug(lane, AUG_F, _split3(-deltas[h]))) for h in range(2)], axis=1)
        doa_ref[...] = both.astype(BF16)
        dota_ref[...] = both.astype(BF16).T

    blk = pl.BlockSpec((tr, LANE), lambda i, h: (i, h))
    return pl.pallas_call(
        body, name=name, grid=(T // tr, nhp), in_specs=[blk, blk],
        out_specs=[pl.BlockSpec((tr, 2 * LANE), lambda i, h: (i, h)),
                   pl.BlockSpec((2 * LANE, tr), lambda i, h: (h, i))],
        out_shape=[jax.ShapeDtypeStruct((T, 2 * D), BF16), jax.ShapeDtypeStruct((2 * D, T), BF16)],
        compiler_params=_cparams("parallel", "parallel"),
    )(do, o)


def _fox_prep_bwd(proj, dq, dkt, dvt, qg, kg, *, d_model, name):
    T = proj.shape[0]
    nhp = d_model // LANE
    tr = _row_tile(T)

    def body(q_ref, k_ref, dq_ref, dkt_ref, dvt_ref, qg_ref, kg_ref, dqo_ref, dko_ref, dvo_ref, sums_ref):
        first = (pl.program_id(0) == 0) & (pl.program_id(1) == 0)
        lo = lax.broadcasted_iota(jnp.int32, (tr, LANE), 1) < HEAD_DIM

        def pair(x2):
            return jnp.where(lo, x2[:, :LANE], pltpu.roll(x2[:, LANE:], HEAD_DIM, 1))

        def bwd(xv, dxhat, g):
            ms = _half_sums(xv * xv, lo) * (1.0 / HEAD_DIM)
            r = lax.rsqrt(ms + EPS)
            n = xv * r
            dn = dxhat * g
            dx = r * (dn - n * (_half_sums(dn * n, lo) * (1.0 / HEAD_DIM)))
            dg = jnp.sum(dxhat * n, axis=0, keepdims=True)
            return dx, dg + pltpu.roll(dg, HEAD_DIM, 1)

        dxq, dgq = bwd(q_ref[...], pair(dq_ref[...]) * (HEAD_DIM ** -0.5), qg_ref[...])
        dxk, dgk = bwd(k_ref[...], pair(dkt_ref[...].T), kg_ref[...])
        dqo_ref[...] = dxq.astype(BF16)
        dko_ref[...] = dxk.astype(BF16)
        dvo_ref[...] = pair(dvt_ref[...].T.astype(F32)).astype(BF16)
        part = jnp.concatenate([dgq, dgk, jnp.zeros((6, LANE), F32)], axis=0)

        @pl.when(first)
        def _():
            sums_ref[...] = part

        @pl.when(jnp.logical_not(first))
        def _():
            sums_ref[...] += part

    gain = pl.BlockSpec((1, LANE), lambda i, h: (0, 0))
    blk = pl.BlockSpec((tr, LANE), lambda i, h: (i, h))
    tall = pl.BlockSpec((2 * LANE, tr), lambda i, h: (h, i))
    return pl.pallas_call(
        body, name=name, grid=(T // tr, nhp),
        in_specs=[blk, pl.BlockSpec((tr, LANE), lambda i, h: (i, nhp + h)),
                  pl.BlockSpec((tr, 2 * LANE), lambda i, h: (i, h)), tall, tall, gain, gain],
        out_specs=[blk, blk, blk, pl.BlockSpec((8, LANE), lambda i, h: (0, 0))],
        out_shape=[jax.ShapeDtypeStruct((T, d_model), BF16)] * 3 + [jax.ShapeDtypeStruct((8, LANE), F32)],
        compiler_params=_cparams("arbitrary", "arbitrary"),
    )(proj, proj, dq, dkt, dvt, qg, kg)


def _scan_lanes(x, reverse):
    n = x.shape[-1]
    lane = lax.broadcasted_iota(jnp.int32, x.shape, 1)
    sh = 1
    while sh < n:
        if reverse:
            x = x + jnp.where(lane < n - sh, pltpu.roll(x, n - sh, 1), 0.0)
        else:
            x = x + jnp.where(lane >= sh, pltpu.roll(x, sh, 1), 0.0)
        sh *= 2
    return x


def _fox_gate_fwd(fpre_t, bf, *, name):
    def body(f_ref, b_ref, o_ref):
        xv = f_ref[...] + b_ref[...]
        logf = jnp.minimum(xv, 0.0) - jnp.log1p(jnp.exp(-jnp.abs(xv)))
        o_ref[...] = _scan_lanes(logf, reverse=False)

    return pl.pallas_call(body, name=name, out_shape=jax.ShapeDtypeStruct(fpre_t.shape, F32))(fpre_t, bf)


def _fox_gate_bwd(dcol, drow, fpre_t, bf, *, name):
    H = fpre_t.shape[0]

    def body(dc_ref, dr_ref, f_ref, b_ref, o_ref, db_ref):
        xv = f_ref[...] + b_ref[...]
        e = dc_ref[...] - dr_ref[...]
        dlogf = _scan_lanes(e, reverse=False) - e
        dpre = dlogf * (1.0 - jax.nn.sigmoid(xv))
        o_ref[...] = dpre
        db_ref[...] = jnp.broadcast_to(jnp.sum(dpre, axis=-1, keepdims=True), (H, LANE))

    return pl.pallas_call(
        body, name=name,
        out_shape=[jax.ShapeDtypeStruct(fpre_t.shape, F32), jax.ShapeDtypeStruct((H, LANE), F32)],
    )(dcol, drow, fpre_t, bf)


_NT = (((1,), (1,)), ((), ()))
_TN = (((0,), (0,)), ((), ()))
_NN = (((1,), (0,)), ((), ()))


def _attn_tile(T):
    return min(T, 512)


def _causal(tq, tk):
    return lax.broadcasted_iota(jnp.int32, (tq, tk), 1) <= lax.broadcasted_iota(jnp.int32, (tq, tk), 0)


def _fox_attn_fwd(qa, kta, va, *, name, comm=None):
    T = qa.shape[0]
    nhp = qa.shape[1] // (2 * LANE)
    tq = tk = _attn_tile(T)
    nq = T // tq

    def body(*refs):
        (qa_ref, kta_ref, va_ref), src_refs, (o_ref, qb_ref), dst_refs, (m_sc, l_sc, acc_sc), sems = (
            _split_comm_refs(refs, 3, 2, 3, comm))
        hp, i, j = pl.program_id(0), pl.program_id(1), pl.program_id(2)
        if comm:
            @pl.when((hp == 0) & (i == 0) & (j == 0))
            def _():
                _comm_start(_comm_copies(comm.plan, src_refs, dst_refs, *sems))

        @pl.when(j == 0)
        def _():
            m_sc[...] = jnp.full(m_sc.shape, NEG, F32)
            l_sc[...] = jnp.zeros(l_sc.shape, F32)
            acc_sc[...] = jnp.zeros(acc_sc.shape, F32)

        def block(diagonal):
            heads = [slice(h * LANE, (h + 1) * LANE) for h in range(2)]
            scores = [lax.dot_general(qa_ref[:, hs], kta_ref[hs, :], _NN, preferred_element_type=F32)
                      for hs in heads]
            state = [(m_sc[h], l_sc[h], acc_sc[h]) for h in range(2)]
            probs, updates = [], []
            for s, (m_prev, l_prev, _) in zip(scores, state):
                if diagonal:
                    s = jnp.where(_causal(tq, tk), s, NEG)
                m_next = jnp.maximum(m_prev, jnp.max(s, axis=1, keepdims=True))
                p = jnp.exp(s - jnp.tile(m_next, (1, tk // LANE)))
                alpha = jnp.exp(m_prev - m_next)
                probs.append(p.astype(BF16))
                updates.append((m_next, alpha, alpha * l_prev + jnp.sum(p, axis=1, keepdims=True)))
            pvs = [lax.dot_general(p, va_ref[:, hs], _NN, preferred_element_type=F32)
                   for p, hs in zip(probs, heads)]
            for h in range(2):
                m_next, alpha, l_next = updates[h]
                m_sc[h] = m_next
                l_sc[h] = l_next
                acc_sc[h] = alpha * state[h][2] + pvs[h]

        @pl.when(j < i)
        def _():
            block(False)

        @pl.when(j == i)
        def _():
            block(True)
            o_ref[...] = acc_sc[0] / l_sc[0] + acc_sc[1] / l_sc[1]
            lane = lax.broadcasted_iota(jnp.int32, (tq, LANE), 1)
            for h in range(2):
                hs = slice(h * LANE, (h + 1) * LANE)
                pieces = _split3(-(m_sc[h] + jnp.log(l_sc[h])))
                qb = qa_ref[:, hs].astype(F32)
                for n, piece in enumerate(pieces):
                    qb = jnp.where(lane == AUG_LSE + n, piece, qb)
                qb_ref[:, hs] = qb.astype(BF16)

        if comm:
            @pl.when((hp == nhp - 1) & (i == nq - 1) & (j == nq - 1))
            def _():
                _comm_wait(_comm_copies(comm.plan, src_refs, dst_refs, *sems))

    outs = pl.pallas_call(
        body, name=name, grid=(nhp, nq, nq),
        in_specs=[pl.BlockSpec((tq, 2 * LANE), lambda h, i, j: (i, h)),
                  pl.BlockSpec((2 * LANE, tk), lambda h, i, j: (h, jnp.minimum(j, i))),
                  pl.BlockSpec((tk, 2 * LANE), lambda h, i, j: (jnp.minimum(j, i), h))]
        + ([ANY] * len(comm.srcs) if comm else []),
        out_specs=[pl.BlockSpec((tq, LANE), lambda h, i, j: (i, h)),
                   pl.BlockSpec((tq, 2 * LANE), lambda h, i, j: (i, h))]
        + ([ANY] * len(comm.out_shapes) if comm else []),
        out_shape=[jax.ShapeDtypeStruct((T, nhp * LANE), F32), jax.ShapeDtypeStruct(qa.shape, BF16)]
        + (list(comm.out_shapes) if comm else []),
        scratch_shapes=[pltpu.VMEM((2, tq, LANE), F32), pltpu.VMEM((2, tq, LANE), F32),
                        pltpu.VMEM((2, tq, LANE), F32)] + (comm.scratch() if comm else []),
        compiler_params=(_cparams("arbitrary", "arbitrary", "arbitrary") if comm
                         else _cparams("parallel", "parallel", "arbitrary")),
    )(qa, kta, va, *(comm.srcs if comm else []))
    return outs[0], outs[1], outs[2:]


def _fox_attn_bwd(qb, qta, ka, kta, vta, doa, dota, *, name, comm=None):
    T = qb.shape[0]
    nhp = qb.shape[1] // (2 * LANE)
    tq = tk = _attn_tile(T)
    nq = T // tq

    def body(*refs):
        ((qb_ref, qta_ref, ka_ref, kta_ref, vta_ref, doa_ref, dota_ref), src_refs,
         (dq_ref, dkt_ref, dvt_ref, dcol_ref, drow_ref), dst_refs, (dkt_sc, dvt_sc, dcol_sc), sems) = (
            _split_comm_refs(refs, 7, 5, 3, comm))
        hp, j, i = pl.program_id(0), pl.program_id(1), pl.program_id(2)
        if comm:
            @pl.when((hp == 0) & (j == 0) & (i == 0))
            def _():
                _comm_start(_comm_copies(comm.plan, src_refs, dst_refs, *sems))

        @pl.when((j == 0) & (i == 0))
        def _():
            dq_ref[...] = jnp.zeros(dq_ref.shape, F32)
            drow_ref[...] = jnp.zeros(drow_ref.shape, F32)

        @pl.when(i == 0)
        def _():
            dkt_sc[...] = jnp.zeros(dkt_sc.shape, F32)
            dvt_sc[...] = jnp.zeros(dvt_sc.shape, F32)
            dcol_sc[...] = jnp.zeros(dcol_sc.shape, F32)

        def block(diagonal):
            rows = pl.ds(pl.multiple_of(i * tq, tq), tq)
            heads = [slice(h * LANE, (h + 1) * LANE) for h in range(2)]
            logits = [lax.dot_general(qb_ref[:, hs], kta_ref[hs, :], _NN, preferred_element_type=F32)
                      for hs in heads]
            dpds = [lax.dot_general(doa_ref[:, hs], vta_ref[hs, :], _NN, preferred_element_type=F32)
                    for hs in heads]
            pbs, dlbs = [], []
            for h in range(2):
                p = jnp.exp(logits[h])
                if diagonal:
                    p = jnp.where(_causal(tq, tk), p, 0.0)
                dl = p * dpds[h]
                pbs.append(p.astype(BF16))
                dlbs.append(dl.astype(BF16))
                dcol_sc[h] += jnp.sum(dl, axis=0, keepdims=True)
                drow_ref[h, rows, :] += jnp.sum(dl, axis=1, keepdims=True)
            for h, hs in enumerate(heads):
                dvt_sc[h] += lax.dot_general(dota_ref[hs, :], pbs[h], _NN, preferred_element_type=F32)
                dkt_sc[h] += lax.dot_general(qta_ref[hs, :], dlbs[h], _NN, preferred_element_type=F32)
                dq_ref[rows, hs] += lax.dot_general(dlbs[h], ka_ref[:, hs], _NN, preferred_element_type=F32)

        @pl.when(i > j)
        def _():
            block(False)

        @pl.when(i == j)
        def _():
            block(True)

        @pl.when(i == nq - 1)
        def _():
            dkt_ref[...] = jnp.concatenate([dkt_sc[0], dkt_sc[1]], axis=0)
            dvt_ref[...] = jnp.concatenate([dvt_sc[0], dvt_sc[1]], axis=0).astype(BF16)
            dcol_ref[...] = dcol_sc[...]

        if comm:
            @pl.when((hp == nhp - 1) & (j == nq - 1) & (i == nq - 1))
            def _():
                _comm_wait(_comm_copies(comm.plan, src_refs, dst_refs, *sems))

    qrow = pl.BlockSpec((tq, 2 * LANE), lambda h, j, i: (jnp.maximum(i, j), h))
    qcol = pl.BlockSpec((2 * LANE, tq), lambda h, j, i: (h, jnp.maximum(i, j)))
    krow = pl.BlockSpec((tk, 2 * LANE), lambda h, j, i: (j, h))
    kcol = pl.BlockSpec((2 * LANE, tk), lambda h, j, i: (h, j))
    tall = jax.ShapeDtypeStruct((qb.shape[1], T), F32)
    outs = pl.pallas_call(
        body, name=name, grid=(nhp, nq, nq),
        in_specs=[qrow, qcol, krow, kcol, kcol, qrow, qcol] + ([ANY] * len(comm.srcs) if comm else []),
        out_specs=[pl.BlockSpec((T, 2 * LANE), lambda h, j, i: (0, h)), kcol, kcol,
                   pl.BlockSpec((2, 1, tk), lambda h, j, i: (h, 0, j)),
                   pl.BlockSpec((2, T, 1), lambda h, j, i: (h, 0, 0))]
        + ([ANY] * len(comm.out_shapes) if comm else []),
        out_shape=[jax.ShapeDtypeStruct(qb.shape, F32), tall, jax.ShapeDtypeStruct(tall.shape, BF16),
                   jax.ShapeDtypeStruct((2 * nhp, 1, T), F32), jax.ShapeDtypeStruct((2 * nhp, T, 1), F32)]
        + (list(comm.out_shapes) if comm else []),
        scratch_shapes=[pltpu.VMEM((2, LANE, tk), F32), pltpu.VMEM((2, LANE, tk), F32),
                        pltpu.VMEM((2, 1, tk), F32)] + (comm.scratch() if comm else []),
        compiler_params=_cparams("arbitrary" if comm else "parallel", "arbitrary", "arbitrary"),
    )(qb, qta, ka, kta, vta, doa, dota, *(comm.srcs if comm else []))
    return (*outs[:5], outs[5:])


_GELU_C = math.sqrt(2.0 / math.pi)
_GELU_A = 0.044715


def _gelu(x):
    t = jnp.tanh(_GELU_C * (x + _GELU_A * (x * x * x)))
    return x * (0.5 * (1.0 + t)), t


def _gelu_grad(x, t):
    return 0.5 * (1.0 + t) + 0.5 * x * (1.0 - t * t) * (_GELU_C * (1.0 + 3.0 * _GELU_A * x * x))


def _layer_norm_stats(v):
    mu = jnp.mean(v, axis=-1, keepdims=True)
    vc = v - mu
    rstd = lax.rsqrt(jnp.mean(vc * vc, axis=-1, keepdims=True) + EPS)
    return vc * rstd, rstd


def _layer_norm_bwd(dyhat, yhat, rstd):
    return rstd * (dyhat - jnp.mean(dyhat, axis=-1, keepdims=True)
                   - yhat * jnp.mean(dyhat * yhat, axis=-1, keepdims=True))


def _sg_mask():
    t = lax.broadcasted_iota(jnp.int32, (SG_CHUNK, SG_CHUNK), 0) // SG_CAUSAL
    s = lax.broadcasted_iota(jnp.int32, (SG_CHUNK, SG_CHUNK), 1) // SG_CAUSAL
    return s <= t


def _sg_mix(ws_ref, bc_ref, vln_sc, vo_sc, tr, gd):
    mask = _sg_mask()
    for g in range(SG_GROUPS):
        wg = jnp.where(mask, ws_ref[g], 0.0).astype(BF16)
        cols = slice(g * gd, (g + 1) * gd)
        for n in range(tr // SG_CHUNK):
            rows = slice(n * SG_CHUNK, (n + 1) * SG_CHUNK)
            vo_sc[rows, cols] = lax.dot_general(wg, vln_sc[rows, cols], _NN,
                                                preferred_element_type=F32) + bc_ref[g]


def _sg_fwd(a_uv, ln_g, ln_b, ws, bcol, *, name):
    T, W = a_uv.shape[0], a_uv.shape[1] // 2
    gd = W // SG_GROUPS
    tr = _row_tile(T)

    def body(u_ref, v_ref, g_ref, b_ref, ws_ref, bc_ref, o_ref, vln_sc, vo_sc):
        u, _ = _gelu(u_ref[...])
        v, _ = _gelu(v_ref[...])
        vhat, _ = _layer_norm_stats(v)
        vln_sc[...] = (vhat * g_ref[...] + b_ref[...]).astype(BF16)
        _sg_mix(ws_ref, bc_ref, vln_sc, vo_sc, tr, gd)
        o_ref[...] = (u * vo_sc[...]).astype(BF16)

    row = pl.BlockSpec((1, W), lambda i: (0, 0))
    return pl.pallas_call(
        body, name=name, grid=(T // tr,),
        in_specs=[pl.BlockSpec((tr, W), lambda i: (i, 0)), pl.BlockSpec((tr, W), lambda i: (i, 1)), row, row,
                  pl.BlockSpec((SG_GROUPS, SG_CHUNK, SG_CHUNK), lambda i: (0, 0, 0)),
                  pl.BlockSpec((SG_GROUPS, SG_CHUNK, 1), lambda i: (0, 0, 0))],
        out_specs=pl.BlockSpec((tr, W), lambda i: (i, 0)),
        out_shape=jax.ShapeDtypeStruct((T, W), BF16),
        scratch_shapes=[pltpu.VMEM((tr, W), BF16), pltpu.VMEM((tr, W), F32)],
        compiler_params=_cparams("parallel"),
    )(a_uv, a_uv, ln_g, ln_b, ws, bcol)


def _sg_bwd(a_uv, dgate, ln_g, ln_b, ws, bcol, *, name):
    T, W = a_uv.shape[0], a_uv.shape[1] // 2
    gd = W // SG_GROUPS
    tr = _row_tile(T)

    def body(u_ref, v_ref, dg_ref, g_ref, b_ref, ws_ref, bc_ref,
             da_ref, dws_ref, dbs_ref, sums_ref, vln_sc, vo_sc, dvo_sc, dvln_sc):
        i = pl.program_id(0)

        @pl.when(i == 0)
        def _():
            dws_ref[...] = jnp.zeros(dws_ref.shape, F32)
            dbs_ref[...] = jnp.zeros(dbs_ref.shape, F32)
            sums_ref[...] = jnp.zeros(sums_ref.shape, F32)

        ua, va = u_ref[...], v_ref[...]
        u, tu = _gelu(ua)
        v, tv = _gelu(va)
        vhat, rstd = _layer_norm_stats(v)
        vln_sc[...] = (vhat * g_ref[...] + b_ref[...]).astype(BF16)
        _sg_mix(ws_ref, bc_ref, vln_sc, vo_sc, tr, gd)
        dgt = dg_ref[...]
        du = dgt * vo_sc[...]
        dvo_sc[...] = dgt * u
        mask = _sg_mask()
        for g in range(SG_GROUPS):
            wg = jnp.where(mask, ws_ref[g], 0.0).astype(BF16)
            cols = slice(g * gd, (g + 1) * gd)
            acc_w = jnp.zeros((SG_CHUNK, SG_CHUNK), F32)
            acc_b = jnp.zeros((SG_CHUNK, 1), F32)
            for n in range(tr // SG_CHUNK):
                rows = slice(n * SG_CHUNK, (n + 1) * SG_CHUNK)
                dvo = dvo_sc[rows, cols]
                dvob = dvo.astype(BF16)
                dvln_sc[rows, cols] = lax.dot_general(wg, dvob, _TN, preferred_element_type=F32)
                acc_w += lax.dot_general(dvob, vln_sc[rows, cols], _NT, preferred_element_type=F32)
                acc_b += jnp.sum(dvo, axis=1, keepdims=True)
            dws_ref[g] += jnp.where(mask, acc_w, 0.0)
            dbs_ref[g] += acc_b
        dvln = dvln_sc[...]
        sums_ref[...] += jnp.concatenate([jnp.sum(dvln * vhat, axis=0, keepdims=True),
                                          jnp.sum(dvln, axis=0, keepdims=True),
                                          jnp.zeros((6, W), F32)], axis=0)
        dv = _layer_norm_bwd(dvln * g_ref[...], vhat, rstd)
        da_ref[:, :W] = (du * _gelu_grad(ua, tu)).astype(BF16)
        da_ref[:, W:] = (dv * _gelu_grad(va, tv)).astype(BF16)

    row = pl.BlockSpec((1, W), lambda i: (0, 0))
    wspec = pl.BlockSpec((SG_GROUPS, SG_CHUNK, SG_CHUNK), lambda i: (0, 0, 0))
    bspec = pl.BlockSpec((SG_GROUPS, SG_CHUNK, 1), lambda i: (0, 0, 0))
    return pl.pallas_call(
        body, name=name, grid=(T // tr,),
        in_specs=[pl.BlockSpec((tr, W), lambda i: (i, 0)), pl.BlockSpec((tr, W), lambda i: (i, 1)),
                  pl.BlockSpec((tr, W), lambda i: (i, 0)), row, row, wspec, bspec],
        out_specs=[pl.BlockSpec((tr, 2 * W), lambda i: (i, 0)), wspec, bspec,
                   pl.BlockSpec((8, W), lambda i: (0, 0))],
        out_shape=[jax.ShapeDtypeStruct((T, 2 * W), BF16),
                   jax.ShapeDtypeStruct((SG_GROUPS, SG_CHUNK, SG_CHUNK), F32),
                   jax.ShapeDtypeStruct((SG_GROUPS, SG_CHUNK, 1), F32),
                   jax.ShapeDtypeStruct((8, W), F32)],
        scratch_shapes=[pltpu.VMEM((tr, W), BF16), pltpu.VMEM((tr, W), F32),
                        pltpu.VMEM((tr, W), F32), pltpu.VMEM((tr, W), F32)],
        compiler_params=_cparams("arbitrary"),
    )(a_uv, a_uv, dgate, ln_g, ln_b, ws, bcol)


SUBLANES = 8


def _shift_rows(xc_sc, xs_sc):
    rows = xs_sc.shape[1]
    for p in range(1, SUBLANES):
        xs_sc[p - 1] = xc_sc[pl.ds(p, rows), :]


def _rows_at(xc_sc, xs_sc, offset, tr):
    p = offset % SUBLANES
    base = offset - p
    return xc_sc[pl.ds(base, tr), :] if p == 0 else xs_sc[p - 1, pl.ds(base, tr), :]


def _shift_scratch(tr, C):
    return pltpu.VMEM((SUBLANES - 1, tr + CONV_HALO - SUBLANES, C), F32)


def _cv_glu_conv(a_ref, b_ref, ap_ref, bp_ref, w_ref, bd_ref, xc_sc, xs_sc, tr):
    i = pl.program_id(0)
    prev = ap_ref[...] * jax.nn.sigmoid(bp_ref[...])
    xc_sc[0:CONV_HALO, :] = jnp.where(i > 0, prev, 0.0)
    xc_sc[CONV_HALO:, :] = a_ref[...] * jax.nn.sigmoid(b_ref[...])
    _shift_rows(xc_sc, xs_sc)
    acc = jnp.broadcast_to(bd_ref[...], (tr, bd_ref.shape[1]))
    for k in range(CONV_WIDTH):
        acc = acc + w_ref[k:k + 1, :] * _rows_at(xc_sc, xs_sc, CONV_HALO - (CONV_WIDTH - 1) + k, tr)
    return acc


def _cv_specs(T, C, tr):
    hb = tr // CONV_HALO
    cur = lambda col: pl.BlockSpec((tr, C), lambda i: (i, col))
    prev = lambda col: pl.BlockSpec((CONV_HALO, C), lambda i: (jnp.maximum(i * hb - 1, 0), col))
    row = pl.BlockSpec((1, C), lambda i: (0, 0))
    wspec = pl.BlockSpec((CONV_HALO, C), lambda i: (0, 0))
    return cur, prev, row, wspec


def _cv_fwd(p, w_dw, b_dw, ln_g, ln_b, *, name):
    T, C = p.shape[0], p.shape[1] // 2
    tr = _row_tile(T)
    cur, prev, row, wspec = _cv_specs(T, C, tr)

    def body(a_ref, b_ref, ap_ref, bp_ref, w_ref, bd_ref, g_ref, be_ref, o_ref, xc_sc, xs_sc):
        y2 = _cv_glu_conv(a_ref, b_ref, ap_ref, bp_ref, w_ref, bd_ref, xc_sc, xs_sc, tr)
        yhat, _ = _layer_norm_stats(y2)
        yln = yhat * g_ref[...] + be_ref[...]
        o_ref[...] = (yln * jax.nn.sigmoid(yln)).astype(BF16)

    return pl.pallas_call(
        body, name=name, grid=(T // tr,),
        in_specs=[cur(0), cur(1), prev(0), prev(1), wspec, row, row, row],
        out_specs=pl.BlockSpec((tr, C), lambda i: (i, 0)),
        out_shape=jax.ShapeDtypeStruct((T, C), BF16),
        scratch_shapes=[pltpu.VMEM((tr + CONV_HALO, C), F32), _shift_scratch(tr, C)],
        compiler_params=_cparams("parallel"),
    )(p, p, p, p, w_dw, b_dw, ln_g, ln_b)


def _cv_bwd_ln(p, dy3, w_dw, b_dw, ln_g, ln_b, *, name):
    T, C = p.shape[0], p.shape[1] // 2
    tr = _row_tile(T)
    cur, prev, row, wspec = _cv_specs(T, C, tr)

    def body(a_ref, b_ref, ap_ref, bp_ref, dy_ref, w_ref, bd_ref, g_ref, be_ref,
             dy2_ref, dw_ref, sums_ref, xc_sc, xs_sc):
        i = pl.program_id(0)
        y2 = _cv_glu_conv(a_ref, b_ref, ap_ref, bp_ref, w_ref, bd_ref, xc_sc, xs_sc, tr)
        yhat, rstd = _layer_norm_stats(y2)
        yln = yhat * g_ref[...] + be_ref[...]
        s = jax.nn.sigmoid(yln)
        dyln = dy_ref[...] * (s + yln * s * (1.0 - s))
        dy2 = _layer_norm_bwd(dyln * g_ref[...], yhat, rstd)
        dy2_ref[...] = dy2
        sums = jnp.concatenate([jnp.sum(dy2, axis=0, keepdims=True),
                                jnp.sum(dyln * yhat, axis=0, keepdims=True),
                                jnp.sum(dyln, axis=0, keepdims=True),
                                jnp.zeros((5, C), F32)], axis=0)
        taps = [jnp.sum(dy2 * _rows_at(xc_sc, xs_sc, CONV_HALO - (CONV_WIDTH - 1) + k, tr), axis=0, keepdims=True)
                for k in range(CONV_WIDTH)]
        dw = jnp.concatenate(taps + [jnp.zeros((CONV_HALO - CONV_WIDTH, C), F32)], axis=0)

        @pl.when(i == 0)
        def _():
            sums_ref[...] = sums
            dw_ref[...] = dw

        @pl.when(i > 0)
        def _():
            sums_ref[...] += sums
            dw_ref[...] += dw

    blk = pl.BlockSpec((tr, C), lambda i: (i, 0))
    return pl.pallas_call(
        body, name=name, grid=(T // tr,),
        in_specs=[cur(0), cur(1), prev(0), prev(1), blk, wspec, row, row, row],
        out_specs=[blk, wspec, pl.BlockSpec((8, C), lambda i: (0, 0))],
        out_shape=[jax.ShapeDtypeStruct((T, C), F32), jax.ShapeDtypeStruct((CONV_HALO, C), F32),
                   jax.ShapeDtypeStruct((8, C), F32)],
        scratch_shapes=[pltpu.VMEM((tr + CONV_HALO, C), F32), _shift_scratch(tr, C)],
        compiler_params=_cparams("arbitrary"),
    )(p, p, p, p, dy3, w_dw, b_dw, ln_g, ln_b)


def _cv_bwd_in(p, dy2, w_dw, *, name):
    T, C = p.shape[0], p.shape[1] // 2
    tr = _row_tile(T)
    hb = tr // CONV_HALO
    nblk = T // tr
    last_halo = T // CONV_HALO - 1

    def body(a_ref, b_ref, dy_ref, dyn_ref, w_ref, dp_ref, sums_ref, xc_sc, xs_sc):
        i = pl.program_id(0)
        xc_sc[0:tr, :] = dy_ref[...]
        xc_sc[tr:, :] = jnp.where(i < nblk - 1, dyn_ref[...], 0.0)
        _shift_rows(xc_sc, xs_sc)
        dy1 = jnp.zeros((tr, C), F32)
        for k in range(CONV_WIDTH):
            dy1 = dy1 + w_ref[k:k + 1, :] * _rows_at(xc_sc, xs_sc, CONV_WIDTH - 1 - k, tr)
        a = a_ref[...]
        sb = jax.nn.sigmoid(b_ref[...])
        da = dy1 * sb
        db = dy1 * a * sb * (1.0 - sb)
        dp_ref[:, :C] = da.astype(BF16)
        dp_ref[:, C:] = db.astype(BF16)
        sums = jnp.concatenate([
            jnp.concatenate([jnp.sum(da, axis=0, keepdims=True), jnp.sum(db, axis=0, keepdims=True)], axis=1),
            jnp.zeros((7, 2 * C), F32)], axis=0)

        @pl.when(i == 0)
        def _():
            sums_ref[...] = sums

        @pl.when(i > 0)
        def _():
            sums_ref[...] += sums

    blk = lambda col: pl.BlockSpec((tr, C), lambda i: (i, col))
    return pl.pallas_call(
        body, name=name, grid=(nblk,),
        in_specs=[blk(0), blk(1), blk(0),
                  pl.BlockSpec((CONV_HALO, C), lambda i: (jnp.minimum((i + 1) * hb, last_halo), 0)),
                  pl.BlockSpec((CONV_HALO, C), lambda i: (0, 0))],
        out_specs=[pl.BlockSpec((tr, 2 * C), lambda i: (i, 0)), pl.BlockSpec((8, 2 * C), lambda i: (0, 0))],
        out_shape=[jax.ShapeDtypeStruct((T, 2 * C), BF16), jax.ShapeDtypeStruct((8, 2 * C), F32)],
        scratch_shapes=[pltpu.VMEM((tr + CONV_HALO, C), F32), _shift_scratch(tr, C)],
        compiler_params=_cparams("arbitrary"),
    )(p, p, dy2, dy2, w_dw)


def _col_tile(n, want=1024):
    best = LANE
    for t in range(LANE, min(n, want) + 1, LANE):
        if n % t == 0:
            best = t
    return best if n % LANE == 0 else n


def _mm(a, b, *, name, ta=False, tb=False, **kw):
    M = a.shape[1] if ta else a.shape[0]
    N = b.shape[0] if tb else b.shape[1]
    K = a.shape[0] if ta else a.shape[1]
    kw.setdefault('tm', _col_tile(M, 1024 if ta else 512))
    kw.setdefault('tn', _col_tile(N, 1024))
    kw.setdefault('tk', K if tb else _col_tile(K, 1024))
    return _matmul(a, b, name=name, ta=ta, tb=tb, **kw)


WIDE_ROWS = 1024


def _relu2_epilogue(acc):
    r = jnp.maximum(acc, 0.0)
    return (r * r,)


def _residual_epilogue(acc, x, g):
    return acc, x + g * acc


def _residual_bias_epilogue(acc, x, g, b):
    y = acc + b
    return y, x + g * y


def _relu2_bwd_epilogue(acc, r):
    return (acc * (2.0 * jnp.sqrt(r.astype(F32))),)


def _bias_epilogue(acc, b):
    return (acc + b,)


def _fox_forward(h1, P, j, D, comm=None):
    H = D // HEAD_DIM
    proj = _mm(h1, P['fox_w_in'][j], name='fox_proj', b_outer=True, tm=WIDE_ROWS)
    qg = jnp.tile(P['fox_q_norm'][j][None, :], (1, 2))
    kg = jnp.tile(P['fox_k_norm'][j][None, :], (1, 2))
    fpre_t = proj[:, 3 * D:3 * D + H].T
    bf = P['fox_b_f'][j][:, None]
    fcum = _fox_gate_fwd(fpre_t, bf, name='fox_gate_fwd')
    qa, qta, ka, kta, va, vta = _fox_prep_fwd(proj, qg, kg, fcum[:, :, None], d_model=D, name='fox_prep_fwd')
    o, qb, comm_outs = _fox_attn_fwd(qa, kta, va, name='fox_attn_fwd', comm=comm)
    saved = dict(proj=proj, qg=qg, kg=kg, fpre_t=fpre_t, bf=bf, o=o, qb=qb, qta=qta, ka=ka, kta=kta, vta=vta)
    return o, saved, comm_outs


def _fox_backward(dy, h1, S, P, j, D, comm=None):
    H = D // HEAD_DIM
    w_out, w_in = P['fox_w_out'][j], P['fox_w_in'][j]
    g = {}
    g['fox_w_out'] = _mm(S['o'], dy, ta=True, name='fox_dw_out')
    do = _mm(dy, w_out, tb=True, name='fox_do')
    doa, dota = _fox_do_prep(do, S['o'], name='fox_do_prep')
    dq, dkt, dvt, dcol, drow, comm_outs = _fox_attn_bwd(S['qb'], S['qta'], S['ka'], S['kta'], S['vta'], doa, dota,
                                                        name='fox_attn_bwd', comm=comm)
    dqp, dkp, dvp, gsum = _fox_prep_bwd(S['proj'], dq, dkt, dvt, S['qg'], S['kg'], d_model=D, name='fox_prep_bwd')
    dfpre_t, dbf = _fox_gate_bwd(dcol[:, 0, :], drow[:, :, 0], S['fpre_t'], S['bf'], name='fox_gate_bwd')
    dfpre = jnp.pad(dfpre_t.T.astype(BF16), ((0, 0), (0, LANE - H)))
    dproj = jnp.concatenate([dqp, dkp, dvp, dfpre], axis=1)
    g['fox_w_in'] = _mm(h1, dproj, ta=True, name='fox_dw_in')[:, :3 * D + H]
    g['fox_b_f'] = dbf[:, 0]
    g['fox_q_norm'] = gsum[0, :HEAD_DIM]
    g['fox_k_norm'] = gsum[1, :HEAD_DIM]
    dh1 = _mm(dproj, w_in, tb=True, name='fox_dh')
    return dh1, g, comm_outs


def _sg_forward(h1, P, D):
    a_uv = _mm(h1, P['sg_w_in'], name='sg_in', b_outer=True, tm=WIDE_ROWS)
    bcol = P['sg_b_s'][:, :, None]
    gate = _sg_fwd(a_uv, P['sg_ln_g'], P['sg_ln_b'], P['sg_w_s'], bcol, name='sg_fwd')
    return gate, dict(a_uv=a_uv, bcol=bcol, gate=gate)


def _sg_backward(dy, h1, S, P, D):
    g = {}
    g['sg_w_out'] = _mm(S['gate'], dy, ta=True, name='sg_dw_out')
    dgate = _mm(dy, P['sg_w_out'], tb=True, name='sg_dgate')
    da, dws, dbs, sums = _sg_bwd(S['a_uv'], dgate, P['sg_ln_g'], P['sg_ln_b'], P['sg_w_s'], S['bcol'],
                                 name='sg_bwd')
    g['sg_w_s'], g['sg_b_s'] = dws, dbs[:, :, 0]
    g['sg_ln_g'], g['sg_ln_b'] = sums[0], sums[1]
    g['sg_w_in'] = _mm(h1, da, ta=True, name='sg_dw_in', out_chips=N_CHIPS)
    dh1 = _mm(da, P['sg_w_in'], tb=True, name='sg_dh')
    return dh1, g


def _cv_forward(h1, P, D):
    p = _mm(h1, P['cv_w_pw1'], name='cv_pw1', extras=[(P['cv_b_pw1'], 'row')], epilogue=_bias_epilogue,
            b_outer=True, tm=WIDE_ROWS)
    w_dw = jnp.pad(P['cv_w_dw'], ((0, CONV_HALO - CONV_WIDTH), (0, 0)))
    y3 = _cv_fwd(p, w_dw, P['cv_b_dw'], P['cv_ln_g'], P['cv_ln_b'], name='cv_fwd')
    return y3, dict(p=p, w_dw=w_dw, y3=y3)


def _cv_backward(dy, h1, S, P, D):
    g = {}
    g['cv_w_pw2'] = _mm(S['y3'], dy, ta=True, name='cv_dw_pw2')
    dy3 = _mm(dy, P['cv_w_pw2'], tb=True, name='cv_dy3')
    dy2, dw, sums = _cv_bwd_ln(S['p'], dy3, S['w_dw'], P['cv_b_dw'], P['cv_ln_g'], P['cv_ln_b'], name='cv_bwd_ln')
    g['cv_w_dw'] = dw[:CONV_WIDTH]
    g['cv_b_dw'], g['cv_ln_g'], g['cv_ln_b'] = sums[0], sums[1], sums[2]
    dp, psum = _cv_bwd_in(S['p'], dy2, S['w_dw'], name='cv_bwd_in')
    g['cv_b_pw1'] = psum[0]
    g['cv_w_pw1'] = _mm(h1, dp, ta=True, name='cv_dw_pw1', out_chips=N_CHIPS)
    dh1 = _mm(dp, P['cv_w_pw1'], tb=True, name='cv_dh')
    return dh1, g


class Hooks(NamedTuple):
    fwd_comm: Comm
    fwd_done: Callable
    bwd_comm: Callable
    bwd_done: Callable


def _local_step(x, target, mod, P, hooks=None):
    T, D = x.shape
    L = mod.shape[0]
    saved = []
    for i in range(L):
        kind, j = i % N_MIXERS, i // N_MIXERS
        m = [mod[i:i + 1, k * D:(k + 1) * D] for k in range(6)]
        sh_m, sc_m, g_m, sh_f, sc_f, g_f = m
        w_mix, w_mlp = P['norm_mix'][i:i + 1], P['norm_mlp'][i:i + 1]
        h1 = _norm_mod_fwd(x, w_mix, sc_m, sh_m, name='norm_mix_fwd')
        if kind == 0:
            carried = hooks is not None and i == 0
            op, S, comm_outs = _fox_forward(h1, P, j, D, comm=hooks.fwd_comm if carried else None)
            if carried:
                hooks.fwd_done(comm_outs)
            y, x1 = _mm(op, P['fox_w_out'][j], name='fox_out', extras=[(x, 'tile'), (g_m, 'row')],
                        epilogue=_residual_epilogue, out_dtypes=(F32, F32))
        elif kind == 1:
            op, S = _sg_forward(h1, P, D)
            y, x1 = _mm(op, P['sg_w_out'], name='sg_out', extras=[(x, 'tile'), (g_m, 'row')],
                        epilogue=_residual_epilogue, out_dtypes=(F32, F32))
        else:
            op, S = _cv_forward(h1, P, D)
            y, x1 = _mm(op, P['cv_w_pw2'], name='cv_out',
                        extras=[(x, 'tile'), (g_m, 'row'), (P['cv_b_pw2'], 'row')],
                        epilogue=_residual_bias_epilogue, out_dtypes=(F32, F32))
        h2 = _norm_mod_fwd(x1, w_mlp, sc_f, sh_f, name='norm_mlp_fwd')
        r = _mm(h2, P['w_mlp_in'][i], name='mlp_in', epilogue=_relu2_epilogue, out_dtypes=(BF16,),
                b_outer=True, tm=WIDE_ROWS)
        z, x2 = _mm(r, P['w_mlp_out'][i], name='mlp_out', extras=[(x1, 'tile'), (g_f, 'row')],
                    epilogue=_residual_epilogue, out_dtypes=(F32, F32), tk=P['w_mlp_out'][i].shape[0])
        saved.append(dict(x=x, h1=h1, S=S, y=y, x1=x1, h2=h2, r=r, z=z, m=m))
        x = x2

    loss_part, dx = _loss_head(x, target, name='loss_head')

    grads = {k: [None] * L for k in ('norm_mix', 'norm_mlp')}
    mix_grads, mat = {}, {}
    dmod = [None] * L
    for i in reversed(range(L)):
        kind, j = i % N_MIXERS, i // N_MIXERS
        sv = saved[i]
        sh_m, sc_m, g_m, sh_f, sc_f, g_f = sv['m']
        w_mix, w_mlp = P['norm_mix'][i:i + 1], P['norm_mlp'][i:i + 1]
        dz, dgf = _gate_bwd(dx, sv['z'], g_f, name='mlp_gate_bwd')
        mat['w_mlp_out', i] = _mm(sv['r'], dz, ta=True, name='mlp_dw_out')
        da = _mm(dz, P['w_mlp_out'][i], tb=True, name='mlp_da', extras=[(sv['r'], 'tile')],
                 epilogue=_relu2_bwd_epilogue, out_dtypes=(BF16,), b_outer=True, tm=WIDE_ROWS)
        mat['w_mlp_in', i] = _mm(sv['h2'], da, ta=True, name='mlp_dw_in', out_chips=N_CHIPS)
        dh2 = _mm(da, P['w_mlp_in'][i], tb=True, name='mlp_dh', tk=P['w_mlp_in'][i].shape[1])
        dx1, sums_f, dy = _norm_mod_bwd(dh2, sv['x1'], dx, w_mlp, sc_f, name='norm_mlp_bwd', gate=(sv['y'], g_m))
        if kind == 0:
            carried = hooks is not None and i == 0
            dh1, g, comm_outs = _fox_backward(dy, sv['h1'], sv['S'], P, j, D,
                                              comm=hooks.bwd_comm(mat) if carried else None)
            if carried:
                hooks.bwd_done(comm_outs)
        elif kind == 1:
            dh1, g = _sg_backward(dy, sv['h1'], sv['S'], P, D)
        else:
            dh1, g = _cv_backward(dy, sv['h1'], sv['S'], P, D)
            g['cv_b_pw2'] = sums_f[4]
        for k, val in g.items():
            if k in BIG:
                mat[k, j] = val
            else:
                mix_grads.setdefault(k, {})[j] = val
        dx, sums_m = _norm_mod_bwd(dh1, sv['x'], dx1, w_mix, sc_m, name='norm_mix_bwd')
        grads['norm_mlp'][i], grads['norm_mix'][i] = sums_f[2], sums_m[2]
        dmod[i] = jnp.concatenate([sums_m[0], sums_m[1], sums_f[3], sums_f[0], sums_f[1], dgf[0]])

    out = {k: jnp.stack(v) for k, v in grads.items()}
    for k, per_j in mix_grads.items():
        out[k] = jnp.stack([per_j[j] for j in sorted(per_j)])
    return loss_part, dx, jnp.stack(dmod), out, mat


def _all_gather8(blocks, *, name):
    n = len(blocks)

    def body(*refs):
        x_refs, out_refs = refs[:n], refs[n:2 * n]
        send_sems, recv_sems, local_sems = refs[2 * n:]
        x, y, c = _position()
        me, sibling = (x, y, c), (x, y, 1 - c)
        chips = [(1 - x, y), (x, 1 - y), (1 - x, 1 - y)]

        def slot(a, px, py, pc):
            return out_refs[a].at[4 * px + 2 * py + pc]

        def copy(a, k, blk, to, src=None):
            return pltpu.make_async_remote_copy(
                src_ref=slot(a, *blk) if src is None else src, dst_ref=slot(a, *blk),
                send_sem=send_sems.at[7 * a + k], recv_sem=recv_sems.at[7 * a + k],
                device_id=to, device_id_type=MESH)

        mine = [pltpu.make_async_copy(x_refs[a], slot(a, *me), local_sems.at[a]) for a in range(n)]
        for cp in mine:
            cp.start()
        first = []
        for j, chip in enumerate(chips):
            first += [copy(a, 1 + j, me, (*chip, c), src=x_refs[a]) for a in range(n)]
        first += [copy(a, 0, me, sibling, src=x_refs[a]) for a in range(n)]
        for cp in first:
            cp.start()
        passed = []
        for j, chip in enumerate(chips):
            for a in range(n):
                copy(a, 1 + j, (*chip, c), me).wait_recv()
                passed.append(copy(a, 4 + j, (*chip, c), sibling))
                passed[-1].start()
        for a in range(n):
            copy(a, 0, sibling, me).wait_recv()
        for j, chip in enumerate(chips):
            for a in range(n):
                copy(a, 4 + j, (*chip, 1 - c), me).wait_recv()
        for cp in first + passed:
            cp.wait_send()
        for cp in mine:
            cp.wait()

    return pl.pallas_call(
        body, name=name, in_specs=[ANY] * n, out_specs=[ANY] * n,
        out_shape=[jax.ShapeDtypeStruct((8,) + b.shape, b.dtype) for b in blocks],
        scratch_shapes=[pltpu.SemaphoreType.DMA((7 * n,)), pltpu.SemaphoreType.DMA((7 * n,)),
                        pltpu.SemaphoreType.DMA((n,))],
    )(*blocks)


def _exchange(comm, *, name, aliases=None):
    ns, no = len(comm.srcs), len(comm.out_shapes)

    def body(*refs):
        copies = _comm_copies(comm.plan, refs[:ns], refs[ns:ns + no], *refs[ns + no:])
        _comm_start(copies)
        _comm_wait(copies)

    return pl.pallas_call(
        body, name=name, in_specs=[ANY] * ns, out_specs=[ANY] * no, out_shape=list(comm.out_shapes),
        scratch_shapes=comm.scratch(), input_output_aliases=aliases or {},
    )(*comm.srcs)


def _gather_comm(halves):
    n = len(halves)

    def plan(src, out, x, y, c):
        mine = 4 * x + 2 * y + c
        remote = [(src[a], out[a].at[mine], (x, y, 1 - c), out[a].at[4 * x + 2 * y + 1 - c]) for a in range(n)]
        for fx, fy in CHIP_FLIPS:
            px, py = _flip(x, fx), _flip(y, fy)
            remote += [(src[a], out[a].at[mine], (px, py, c), out[a].at[4 * px + 2 * py + c]) for a in range(n)]
        return remote, [(src[a], out[a].at[mine]) for a in range(n)]

    return Comm(list(halves), [jax.ShapeDtypeStruct((8,) + h.shape, h.dtype) for h in halves], plan, 4 * n, n)


def _gather_forward(bufs, *, name):
    n = len(bufs)

    def plan(src, out, x, y, c):
        remote = []
        for fx, fy in CHIP_FLIPS:
            px, py = _flip(x, fx), _flip(y, fy)
            remote += [(src[a].at[4 * px + 2 * py + c], out[a].at[4 * px + 2 * py + c], (x, y, 1 - c),
                        out[a].at[4 * px + 2 * py + 1 - c]) for a in range(n)]
        return remote, []

    comm = Comm(list(bufs), [jax.ShapeDtypeStruct(b.shape, b.dtype) for b in bufs], plan, 3 * n, 0)
    return _exchange(comm, name=name, aliases={a: a for a in range(n)})


CHIP_FLIPS = ((1, 0), (0, 1), (1, 1))


def _flip(v, f):
    return 1 - v if f else v


def _sum_rows_tile(R, C, budget=3 << 20):
    best = None
    for t in range(8, R + 1, 8):
        if R % t == 0 and t * C * 4 <= budget:
            best = t
    return best if best is not None else R


def _rs_begin(gps, *, wire_dtype):
    n = len(gps)
    c_arr = jnp.reshape(_position()[2], (1,)).astype(jnp.int32)

    def plan(src, out, x, y, c):
        return [(src[a].at[b, 1 - c], out[a].at[b], (x, y, 1 - c), out[a].at[b])
                for a in range(n) for b in range(4)], []

    got1 = _exchange(Comm(list(gps), [jax.ShapeDtypeStruct((4,) + g.shape[2:], F32) for g in gps], plan, 4 * n, 0),
                     name='rs_sibling')

    def sum_chip(c_ref, mine_ref, got_ref, out_ref):
        out_ref[...] = (mine_ref[...] + got_ref[...]).astype(out_ref.dtype)

    parts = []
    for gp, g1 in zip(gps, got1):
        _, _, R, C = gp.shape
        tr = _sum_rows_tile(R, C)
        parts.append(pl.pallas_call(
            sum_chip, name='rs_sum_chip',
            grid_spec=pltpu.PrefetchScalarGridSpec(
                num_scalar_prefetch=1, grid=(4, R // tr),
                in_specs=[pl.BlockSpec((None, None, tr, C), lambda b, r, cr: (b, cr[0], r, 0)),
                          pl.BlockSpec((None, tr, C), lambda b, r, cr: (b, r, 0))],
                out_specs=pl.BlockSpec((None, tr, C), lambda b, r, cr: (b, r, 0))),
            out_shape=jax.ShapeDtypeStruct((4, R, C), wire_dtype),
            compiler_params=_cparams("parallel", "parallel"),
        )(c_arr, gp, g1))
    return got1, parts


def _rs_chips_comm(parts):
    n = len(parts)

    def plan(src, out, x, y, c):
        remote = []
        for k, (fx, fy) in enumerate(CHIP_FLIPS):
            px, py = _flip(x, fx), _flip(y, fy)
            remote += [(src[a].at[2 * px + py], out[a].at[k], (px, py, c), out[a].at[k]) for a in range(n)]
        return remote, []

    return Comm(list(parts), [jax.ShapeDtypeStruct((3,) + p.shape[1:], p.dtype) for p in parts], plan, 3 * n, 0)


def _rs_finish(gps, got1, got2):
    n = len(gps)
    x, y, c = _position()
    bc_arr = jnp.stack([2 * x + y, c]).astype(jnp.int32)

    def sum_final(bc_ref, mine_ref, got1_ref, got2_ref, out_ref):
        acc = mine_ref[...] + got1_ref[...]
        for k in range(3):
            acc = acc + got2_ref[k].astype(F32)
        out_ref[...] = acc

    halves = []
    for gp, g1, g2 in zip(gps, got1, got2):
        _, _, R, C = gp.shape
        tr = _sum_rows_tile(R, C, budget=2 << 20)
        halves.append(pl.pallas_call(
            sum_final, name='rs_sum_final',
            grid_spec=pltpu.PrefetchScalarGridSpec(
                num_scalar_prefetch=1, grid=(R // tr,),
                in_specs=[pl.BlockSpec((None, None, tr, C), lambda r, bc: (bc[0], bc[1], r, 0)),
                          pl.BlockSpec((None, tr, C), lambda r, bc: (bc[0], r, 0)),
                          pl.BlockSpec((3, tr, C), lambda r, bc: (0, r, 0))],
                out_specs=pl.BlockSpec((None, tr, C), lambda r, bc: (bc[1], r, 0))),
            out_shape=jax.ShapeDtypeStruct((2, R, C), F32),
            compiler_params=_cparams("parallel"),
        )(bc_arr, gp, g1, g2))

    def plan(src, out, x, y, c):
        return [(src[a].at[c], out[a].at[c], (x, y, 1 - c), out[a].at[1 - c]) for a in range(n)], []

    comm = Comm(halves, [jax.ShapeDtypeStruct(h.shape, F32) for h in halves], plan, n, 0)
    return _exchange(comm, name='rs_swap', aliases={a: a for a in range(n)})


def _sum8(gathered, *, name):
    _, R, C = gathered.shape

    def body(g_ref, o_ref):
        acc = g_ref[0]
        for k in range(1, 8):
            acc = acc + g_ref[k]
        o_ref[...] = acc

    return pl.pallas_call(body, name=name, out_shape=jax.ShapeDtypeStruct((R, C), F32))(gathered)


def _adamw(w, g, m, v, *, name):
    shape = w.shape
    cols = shape[-1]
    rows = w.size // cols
    tr = _sum_rows_tile(rows, cols, budget=1 << 20)

    def body(w_ref, g_ref, m_ref, v_ref, d_ref, mo_ref, vo_ref):
        gv = g_ref[...]
        mn = ADAM_B1 * m_ref[...] + (1.0 - ADAM_B1) * gv
        vn = ADAM_B2 * v_ref[...] + (1.0 - ADAM_B2) * (gv * gv)
        m_hat = mn / (1.0 - ADAM_B1 ** ADAM_STEP)
        v_hat = vn / (1.0 - ADAM_B2 ** ADAM_STEP)
        d_ref[...] = -ADAM_LR * (m_hat / (jnp.sqrt(v_hat) + ADAM_EPS) + ADAM_WD * w_ref[...])
        mo_ref[...] = mn
        vo_ref[...] = vn

    blk = pl.BlockSpec((tr, cols), lambda i: (i, 0))
    outs = pl.pallas_call(
        body, name=name, grid=(rows // tr,), in_specs=[blk] * 4, out_specs=[blk] * 3,
        out_shape=[jax.ShapeDtypeStruct((rows, cols), F32)] * 3,
        compiler_params=_cparams("parallel"),
    )(*[a.reshape(rows, cols) for a in (w, g, m, v)])
    return tuple(o.reshape(shape) for o in outs)


WEIGHTS = ['norm_mix', 'norm_mlp', 'w_ada', 'b_ada', 'w_mlp_in', 'w_mlp_out', 'fox_w_in', 'fox_b_f',
           'fox_q_norm', 'fox_k_norm', 'fox_w_out', 'sg_w_in', 'sg_ln_g', 'sg_ln_b', 'sg_w_s', 'sg_b_s',
           'sg_w_out', 'cv_w_pw1', 'cv_b_pw1', 'cv_w_dw', 'cv_b_dw', 'cv_ln_g', 'cv_ln_b', 'cv_w_pw2',
           'cv_b_pw2']
BIG = {'w_mlp_in': 2, 'w_mlp_out': 1, 'fox_w_in': 2, 'fox_w_out': 1, 'sg_w_in': 2, 'sg_w_out': 1,
       'cv_w_pw1': 2, 'cv_w_pw2': 1}
SMALL_SHARDED = ['cv_b_pw1', 'cv_w_dw', 'cv_b_dw', 'cv_ln_g', 'cv_ln_b', 'cv_b_pw2']
SMALL_GRADS = ['norm_mix', 'norm_mlp', 'fox_b_f', 'fox_q_norm', 'fox_k_norm', 'sg_ln_g', 'sg_ln_b', 'sg_w_s',
               'sg_b_s'] + SMALL_SHARDED
GRAD_WIRE_DTYPE = BF16


def _pack_rows(parts, cols):
    flat = jnp.concatenate([p.reshape(-1) for p in parts])
    rows = -(-flat.size // (8 * cols)) * 8
    return jnp.pad(flat, (0, rows * cols - flat.size)).reshape(rows, cols)


def _unpack(flat, shapes):
    out, off = [], 0
    for s in shapes:
        n = math.prod(s)
        out.append(flat[..., off:off + n].reshape(flat.shape[:-1] + tuple(s)))
        off += n
    return out


def _merge_chips(a, axis):
    a = jnp.moveaxis(a, 0, axis)
    return a.reshape(a.shape[:axis] + (a.shape[axis] * a.shape[axis + 1],) + a.shape[axis + 2:])


def _split_chips(a, axis):
    a = a.reshape(a.shape[:axis] + (4, a.shape[axis] // 4) + a.shape[axis + 1:])
    return jnp.moveaxis(a, axis, 0)


def _step(a):
    x, y, c = _position()
    me = 4 * x + 2 * y + c
    chip = 2 * x + y
    T, D = a['x'].shape[1], a['x'].shape[2]
    L = a['norm_mix'].shape[0]

    small_shapes = [(D,)] + [a[n].shape for n in SMALL_SHARDED]
    small = _all_gather8([_pack_rows([a['c']] + [a[n] for n in SMALL_SHARDED], LANE)], name='ag_small')[0]
    small = small.reshape(8, -1)
    c_all = _unpack(small, small_shapes[:1])[0]
    sharded = _unpack(small[0::2, D:], small_shapes[1:])
    P = {n: _merge_chips(v, v.ndim - 2) for n, v in zip(SMALL_SHARDED, sharded)}

    c_act = _silu_rows(c_all, name='c_act')
    mod_cols = jnp.stack([
        _mm(c_act, a['w_ada'][i], name='ada_mod', tm=8, tn=_col_tile(a['w_ada'].shape[2], 768),
            extras=[(lax.dynamic_slice_in_dim(a['b_ada'][i:i + 1], chip * a['w_ada'].shape[2],
                                              a['w_ada'].shape[2], axis=1), 'row')],
            epilogue=_bias_epilogue)
        for i in range(L)])
    mod_all = _all_gather8([mod_cols.reshape(L * 8, -1)], name='ag_mod')[0].reshape(8, L, 8, -1)
    mod = lax.dynamic_index_in_dim(mod_all[0::2], me, axis=2, keepdims=False)
    mod = jnp.moveaxis(mod, 0, 1).reshape(L, 6 * D)

    units = _matrix_units(L)
    first, later = units[:1], units[1:]
    last, earlier = units[:2], units[2:]
    n_heads = D // HEAD_DIM

    def half_block(unit):
        blk = a[unit[0]][unit[1]]
        return lax.dynamic_index_in_dim(blk.astype(BF16).reshape(2, blk.shape[0] // 2, blk.shape[1]), c, axis=0,
                                        keepdims=False)

    def install(group, gathered):
        for (name, idx), gth in zip(group, gathered):
            blocks = gth.reshape((4,) + a[name].shape[1:])
            if name == 'fox_w_in':
                pad = jnp.zeros((blocks.shape[1], LANE - n_heads), BF16)
                full = jnp.concatenate([blocks[0], blocks[1], blocks[2], blocks[3], pad], axis=-1)
            else:
                full = _merge_chips(blocks, BIG[name] - 1)
            if name in ('w_mlp_in', 'w_mlp_out', 'fox_w_in', 'fox_w_out'):
                P.setdefault(name, {})[idx] = full
            else:
                P[name] = full

    install(first, _all_gather8([half_block(u) for u in first], name='ag_weights_first'))
    for n in ('sg_w_s', 'sg_b_s', 'cv_w_dw'):
        P[n] = (P[n] if n in P else a[n])[0]
    for n in ('norm_mix', 'norm_mlp', 'fox_b_f', 'fox_q_norm', 'fox_k_norm', 'sg_ln_g', 'sg_ln_b'):
        P[n] = a[n]

    def split_grad(unit, grad):
        name = unit[0]
        if name == 'fox_w_in':
            grad = grad[:, :a[name].shape[2] * N_CHIPS]
        blk = grad if grad.ndim == 3 else _split_chips(grad, BIG[name] - 1)
        return blk.reshape(N_CHIPS, 2, blk.shape[1] // 2, blk.shape[2])

    state = {}

    def fwd_done(outs):
        install(later, _gather_forward(outs, name='ag_weights_forward'))

    def bwd_comm(mat):
        state['gps'] = [split_grad(u, mat[u]) for u in earlier]
        state['got1'], parts = _rs_begin(state['gps'], wire_dtype=GRAD_WIRE_DTYPE)
        return _rs_chips_comm(parts)

    def bwd_done(outs):
        state['got2'] = outs

    hooks = Hooks(_gather_comm([half_block(u) for u in later]), fwd_done, bwd_comm, bwd_done)
    loss_part, grad_x, dmod, g, mat = _local_step(a['x'][0], a['loss_target'][0], mod, P, hooks)

    small_g = [dmod, loss_part[0:1, 0:1]] + [g[n] for n in SMALL_GRADS]
    small_g_shapes = [s.shape for s in small_g]
    all_small = _all_gather8([_pack_rows(small_g, LANE)], name='ag_small_grads')[0]
    summed = _sum8(all_small, name='sum_small_grads').reshape(-1)
    sums = _unpack(summed, small_g_shapes)
    loss = sums[1][0, 0]
    grads = dict(zip(SMALL_GRADS, sums[2:]))
    grads['b_ada'] = sums[0]
    for n in SMALL_SHARDED:
        blk = a[n].shape[-1]
        grads[n] = lax.dynamic_slice_in_dim(grads[n], chip * blk, blk, axis=grads[n].ndim - 1)
    dmod_all = all_small.reshape(8, -1)[:, :dmod.size].reshape(8, L, 6 * D)
    cols = a['w_ada'].shape[2]
    dmod_cols = lax.dynamic_slice_in_dim(dmod_all, chip * cols, cols, axis=2)
    pad8 = lambda t: jnp.pad(t, ((0, LANE - 8), (0, 0)))
    c_act_pad = pad8(c_act)
    grads['w_ada'] = jnp.stack([
        _mm(c_act_pad, pad8(dmod_cols[:, i]), ta=True, name='ada_dw', tn=_col_tile(cols, 768))
        for i in range(L)])

    shards = dict(zip(earlier, _rs_finish(state['gps'], state['got1'], state['got2'])))
    gps = [split_grad(u, mat[u]) for u in last]
    got1, parts = _rs_begin(gps, wire_dtype=GRAD_WIRE_DTYPE)
    got2 = _exchange(_rs_chips_comm(parts), name='rs_chips')
    shards.update(zip(last, _rs_finish(gps, got1, got2)))
    for n in BIG:
        grads[n] = jnp.stack([shards[n, idx].reshape(a[n].shape[1:]) for idx in range(a[n].shape[0])])

    deltas, new_m, new_v = {}, {}, {}
    for n in WEIGHTS:
        deltas[n], new_m[n], new_v[n] = _adamw(a[n], grads[n], a['m_' + n], a['v_' + n], name='adamw')
    return (loss, grad_x[None], *[grads[n] for n in WEIGHTS], *[deltas[n] for n in WEIGHTS],
            *[new_m[n] for n in WEIGHTS], *[new_v[n] for n in WEIGHTS])


def _matrix_units(n_layers):
    mixers = (('fox_w_in', 'fox_w_out'), ('sg_w_in', 'sg_w_out'), ('cv_w_pw1', 'cv_w_pw2'))
    units = []
    for i in range(n_layers):
        units += [(n, i // N_MIXERS) for n in mixers[i % N_MIXERS]] + [('w_mlp_in', i), ('w_mlp_out', i)]
    return units


def _silu_rows(x, *, name):
    def body(x_ref, o_ref):
        xv = x_ref[...]
        o_ref[...] = (xv * jax.nn.sigmoid(xv)).astype(BF16)

    return pl.pallas_call(body, name=name, out_shape=jax.ShapeDtypeStruct(x.shape, BF16))(x)


def kernel(x, c, norm_mix, norm_mlp, w_ada, b_ada, w_mlp_in, w_mlp_out, fox_w_in, fox_b_f, fox_q_norm, fox_k_norm, fox_w_out, sg_w_in, sg_ln_g, sg_ln_b, sg_w_s, sg_b_s, sg_w_out, cv_w_pw1, cv_b_pw1, cv_w_dw, cv_b_dw, cv_ln_g, cv_ln_b, cv_w_pw2, cv_b_pw2, loss_target, m_norm_mix, m_norm_mlp, m_w_ada, m_b_ada, m_w_mlp_in, m_w_mlp_out, m_fox_w_in, m_fox_b_f, m_fox_q_norm, m_fox_k_norm, m_fox_w_out, m_sg_w_in, m_sg_ln_g, m_sg_ln_b, m_sg_w_s, m_sg_b_s, m_sg_w_out, m_cv_w_pw1, m_cv_b_pw1, m_cv_w_dw, m_cv_b_dw, m_cv_ln_g, m_cv_ln_b, m_cv_w_pw2, m_cv_b_pw2, v_norm_mix, v_norm_mlp, v_w_ada, v_b_ada, v_w_mlp_in, v_w_mlp_out, v_fox_w_in, v_fox_b_f, v_fox_q_norm, v_fox_k_norm, v_fox_w_out, v_sg_w_in, v_sg_ln_g, v_sg_ln_b, v_sg_w_s, v_sg_b_s, v_sg_w_out, v_cv_w_pw1, v_cv_b_pw1, v_cv_w_dw, v_cv_b_dw, v_cv_ln_g, v_cv_ln_b, v_cv_w_pw2, v_cv_b_pw2):
    return _step(dict(locals()))
```

```python
import math
from typing import Callable, NamedTuple

import jax
import jax.numpy as jnp
from jax import lax
from jax.experimental import pallas as pl
from jax.experimental.pallas import tpu as pltpu

F32 = jnp.float32
BF16 = jnp.bfloat16

EPS = 1e-6
HEAD_DIM = 64
LANE = 128
CONV_WIDTH = 31
CONV_HALO = 32
SG_CHUNK = 128
SG_CAUSAL = 64
SG_GROUPS = 8
N_MIXERS = 3
N_CHIPS = 4
VMEM_LIMIT = 56 * 1024 * 1024
NEG = -1e30

ADAM_LR = 0.001
ADAM_B1 = 0.9
ADAM_B2 = 0.999
ADAM_EPS = 1e-08
ADAM_WD = 0.01
ADAM_STEP = 10

MESH = pl.DeviceIdType.MESH
ANY = pl.BlockSpec(memory_space=pl.ANY)


def _cparams(*sem):
    return pltpu.CompilerParams(dimension_semantics=sem, vmem_limit_bytes=VMEM_LIMIT)


def _row_tile(t, want=512):
    return min(t, want)


def _matmul(a, b, *, name, ta=False, tb=False, tm=512, tn=1024, tk=1024,
            extras=(), epilogue=None, out_dtypes=(F32,), b_outer=False, out_chips=None, row_sums=False):
    M, K = (a.shape[1], a.shape[0]) if ta else a.shape
    N = b.shape[0] if tb else b.shape[1]
    assert (b.shape[1] if tb else b.shape[0]) == K
    n_own = N // out_chips if out_chips else N
    tm, tn, tk = min(tm, M), min(tn, n_own), min(tk, K)
    assert M % tm == 0 and n_own % tn == 0 and K % tk == 0, (name, M, N, K, tm, tn, tk)
    nk = K // tk

    def spec(shape, pick):
        if b_outer:
            return pl.BlockSpec(shape, lambda j, i, k: pick(i, j, k))
        return pl.BlockSpec(shape, pick)

    a_spec = spec((tk, tm), lambda i, j, k: (k, i)) if ta else spec((tm, tk), lambda i, j, k: (i, k))
    b_spec = spec((tn, tk), lambda i, j, k: (j, k)) if tb else spec((tk, tn), lambda i, j, k: (k, j))
    ex_specs = [spec((tm, tn), lambda i, j, k: (i, j)) if kind == 'tile' else spec((1, tn), lambda i, j, k: (0, j))
                for _, kind in extras]
    dims = (((0,) if ta else (1,), (1,) if tb else (0,)), ((), ()))
    n_ex, n_out = len(extras), len(out_dtypes) + bool(row_sums)
    assert not row_sums or N == tn

    def body(*refs):
        a_ref, b_ref = refs[0], refs[1]
        ex = refs[2:2 + n_ex]
        outs = refs[2 + n_ex:2 + n_ex + n_out]

        def finish(acc):
            vals = epilogue(acc, *[r[...] for r in ex]) if epilogue else (acc,)
            for o, v in zip(outs[:len(out_dtypes)], vals):
                o[...] = v.astype(o.dtype)
            if row_sums:
                row_tile = pl.program_id(1 if b_outer else 0)

                @pl.when(row_tile == 0)
                def _():
                    outs[-1][...] = vals[-1]

                @pl.when(row_tile > 0)
                def _():
                    outs[-1][...] += vals[-1]

        part = lax.dot_general(a_ref[...].astype(BF16), b_ref[...].astype(BF16), dims,
                               preferred_element_type=F32)
        if nk == 1:
            finish(part)
        else:
            acc_ref = refs[-1]
            k = pl.program_id(2)

            @pl.when(k == 0)
            def _():
                acc_ref[...] = part

            @pl.when(k > 0)
            def _():
                acc_ref[...] += part

            @pl.when(k == nk - 1)
            def _():
                finish(acc_ref[...])

    outs = pl.pallas_call(
        body, name=name,
        grid=(N // tn, M // tm, nk) if b_outer else (M // tm, N // tn, nk),
        in_specs=[a_spec, b_spec] + ex_specs,
        out_specs=[spec((None, tm, tn), lambda i, j, k: (j // (n_own // tn), i, j % (n_own // tn)))
                   if out_chips else spec((tm, tn), lambda i, j, k: (i, j)) for _ in out_dtypes]
        + ([spec((8, tn), lambda i, j, k: (0, j))] if row_sums else []),
        out_shape=[jax.ShapeDtypeStruct((out_chips, M, n_own) if out_chips else (M, N), dt) for dt in out_dtypes]
        + ([jax.ShapeDtypeStruct((8, N), F32)] if row_sums else []),
        scratch_shapes=[pltpu.VMEM((tm, tn), F32)] if nk > 1 else [],
        compiler_params=(_cparams("arbitrary", "arbitrary", "arbitrary") if row_sums
                         else _cparams("parallel", "parallel", "arbitrary")),
    )(a, b, *[arr for arr, _ in extras])
    return outs if n_out > 1 else outs[0]


def _norm_mod_fwd(x, w, sc, sh, *, name):
    T, D = x.shape
    tr = _row_tile(T)

    def body(x_ref, w_ref, sc_ref, sh_ref, h_ref):
        xv = x_ref[...]
        r = lax.rsqrt(jnp.mean(xv * xv, axis=-1, keepdims=True) + EPS)
        h_ref[...] = ((xv * r) * w_ref[...] * (1.0 + sc_ref[...]) + sh_ref[...]).astype(BF16)

    row = pl.BlockSpec((1, D), lambda i: (0, 0))
    return pl.pallas_call(
        body, name=name, grid=(T // tr,),
        in_specs=[pl.BlockSpec((tr, D), lambda i: (i, 0)), row, row, row],
        out_specs=pl.BlockSpec((tr, D), lambda i: (i, 0)),
        out_shape=jax.ShapeDtypeStruct((T, D), BF16),
        compiler_params=_cparams("parallel"),
    )(x, w, sc, sh)


def _gate_bwd(dx, y, g, *, name):
    T, D = dx.shape
    tr = _row_tile(T)

    def body(dx_ref, y_ref, g_ref, dy_ref, dg_ref):
        i = pl.program_id(0)
        dxv = dx_ref[...]
        dy_ref[...] = (dxv * g_ref[...]).astype(BF16)
        part = jnp.concatenate([jnp.sum(dxv * y_ref[...], axis=0, keepdims=True),
                                jnp.zeros((7, D), F32)], axis=0)

        @pl.when(i == 0)
        def _():
            dg_ref[...] = part

        @pl.when(i > 0)
        def _():
            dg_ref[...] += part

    blk = pl.BlockSpec((tr, D), lambda i: (i, 0))
    return pl.pallas_call(
        body, name=name, grid=(T // tr,),
        in_specs=[blk, blk, pl.BlockSpec((1, D), lambda i: (0, 0))],
        out_specs=[blk, pl.BlockSpec((8, D), lambda i: (0, 0))],
        out_shape=[jax.ShapeDtypeStruct((T, D), BF16), jax.ShapeDtypeStruct((8, D), F32)],
        compiler_params=_cparams("arbitrary"),
    )(dx, y, g)


def _loss_head(y, target, *, name):
    T, D = y.shape
    tr = _row_tile(T)

    def body(y_ref, t_ref, loss_ref, dy_ref):
        i = pl.program_id(0)
        e = y_ref[...] - t_ref[...]
        dy_ref[...] = e * (1.0 / D)
        part = jnp.full((8, LANE), 0.5 / D * jnp.sum(e * e), F32)

        @pl.when(i == 0)
        def _():
            loss_ref[...] = part

        @pl.when(i > 0)
        def _():
            loss_ref[...] += part

    blk = pl.BlockSpec((tr, D), lambda i: (i, 0))
    return pl.pallas_call(
        body, name=name, grid=(T // tr,), in_specs=[blk, blk],
        out_specs=[pl.BlockSpec((8, LANE), lambda i: (0, 0)), blk],
        out_shape=[jax.ShapeDtypeStruct((8, LANE), F32), jax.ShapeDtypeStruct((T, D), F32)],
        compiler_params=_cparams("arbitrary"),
    )(y, target)


def _position():
    return lax.axis_index("x"), lax.axis_index("y"), lax.axis_index("c")


class Comm(NamedTuple):
    srcs: list
    out_shapes: list
    plan: Callable
    n_remote: int
    n_local: int

    def scratch(self):
        return [pltpu.SemaphoreType.DMA((self.n_remote,)), pltpu.SemaphoreType.DMA((self.n_remote,)),
                pltpu.SemaphoreType.DMA((max(self.n_local, 1),))]


def _comm_copies(plan, src_refs, out_refs, send_sems, recv_sems, local_sems):
    x, y, c = _position()
    remote, local = plan(src_refs, out_refs, x, y, c)

    def copy(k, s, d, peer):
        return pltpu.make_async_remote_copy(src_ref=s, dst_ref=d, send_sem=send_sems.at[k],
                                            recv_sem=recv_sems.at[k], device_id=peer, device_id_type=MESH)

    sends = [copy(k, s, d, peer) for k, (s, d, peer, _) in enumerate(remote)]
    recvs = [copy(k, s, landing, peer) for k, (s, _, peer, landing) in enumerate(remote)]
    local_copies = [pltpu.make_async_copy(s, d, local_sems.at[i]) for i, (s, d) in enumerate(local)]
    return sends, recvs, local_copies


def _comm_start(copies):
    sends, _, local_copies = copies
    for cp in local_copies + sends:
        cp.start()


def _comm_wait(copies):
    sends, recvs, local_copies = copies
    for cp in recvs:
        cp.wait_recv()
    for cp in sends:
        cp.wait_send()
    for cp in local_copies:
        cp.wait()


def _split_comm_refs(refs, n_in, n_out, n_scratch, comm):
    ns, nd = (len(comm.srcs), len(comm.out_shapes)) if comm else (0, 0)
    cuts = [n_in, ns, n_out, nd, n_scratch]
    parts, at = [], 0
    for n in cuts:
        parts.append(refs[at:at + n])
        at += n
    return (*parts, refs[at:])


AUG_F = HEAD_DIM
AUG_LSE = HEAD_DIM + 6


def _half_cols(x, lo):
    return (jnp.sum(jnp.where(lo, x, 0.0), axis=-1, keepdims=True),
            jnp.sum(jnp.where(lo, 0.0, x), axis=-1, keepdims=True))


def _half_sums(x, lo):
    s_lo, s_hi = _half_cols(x, lo)
    return jnp.where(lo, s_lo, s_hi)


def _split3(x):
    a = x.astype(BF16).astype(F32)
    r = x - a
    b = r.astype(BF16).astype(F32)
    return a, b, (r - b).astype(BF16).astype(F32)


def _aug(lane, base, terms):
    out = jnp.zeros(lane.shape, F32)
    for i, t in enumerate(terms):
        out = jnp.where(lane == base + i, t, out)
    return out


def _head_lanes(x2, h):
    return x2 if h == 0 else pltpu.roll(x2, HEAD_DIM, 1)


def _fox_prep_fwd(proj, qg, kg, fcol, *, d_model, name):
    T = proj.shape[0]
    nhp = d_model // LANE
    tr = _row_tile(T)

    def body(q_ref, k_ref, v_ref, qg_ref, kg_ref, f_ref, qa_ref, qta_ref, ka_ref, kta_ref, va_ref, vta_ref):
        lane = lax.broadcasted_iota(jnp.int32, (tr, LANE), 1)
        lo = lane < HEAD_DIM

        def norm(xv, g):
            ms = _half_sums(xv * xv, lo) * (1.0 / HEAD_DIM)
            return (xv * lax.rsqrt(ms + EPS)) * g

        qn = norm(q_ref[...], qg_ref[...]) * (HEAD_DIM ** -0.5)
        kn = norm(k_ref[...], kg_ref[...])
        vv = v_ref[...]
        qa, ka, va, vta = [], [], [], []
        for h in range(2):
            f1, f2, f3 = _split3(f_ref[h])
            qa.append(jnp.where(lo, _head_lanes(qn, h), _aug(lane, AUG_F, [f1, f2, f3, 1.0, 1.0, 1.0])))
            ka.append(jnp.where(lo, _head_lanes(kn, h),
                                _aug(lane, AUG_F, [1.0, 1.0, 1.0, -f1, -f2, -f3, 1.0, 1.0, 1.0])))
            va.append(jnp.where(lo if h == 0 else jnp.logical_not(lo), vv, 0.0))
            vta.append(jnp.where(lo, _head_lanes(vv, h), _aug(lane, AUG_F, [1.0, 1.0, 1.0])))
        for parts, ref, tref in ((qa, qa_ref, qta_ref), (ka, ka_ref, kta_ref), (va, va_ref, None),
                                 (vta, None, vta_ref)):
            both = jnp.concatenate(parts, axis=1)
            if ref is not None:
                ref[...] = both.astype(BF16)
            if tref is not None:
                tref[...] = both.astype(BF16).T

    gain = pl.BlockSpec((1, LANE), lambda i, h: (0, 0))
    rows = pl.BlockSpec((tr, 2 * LANE), lambda i, h: (i, h))
    cols = pl.BlockSpec((2 * LANE, tr), lambda i, h: (h, i))
    wide, tall = jax.ShapeDtypeStruct((T, 2 * d_model), BF16), jax.ShapeDtypeStruct((2 * d_model, T), BF16)
    return pl.pallas_call(
        body, name=name, grid=(T // tr, nhp),
        in_specs=[pl.BlockSpec((tr, LANE), lambda i, h: (i, h)),
                  pl.BlockSpec((tr, LANE), lambda i, h: (i, nhp + h)),
                  pl.BlockSpec((tr, LANE), lambda i, h: (i, 2 * nhp + h)), gain, gain,
                  pl.BlockSpec((2, tr, 1), lambda i, h: (h, i, 0))],
        out_specs=[rows, cols, rows, cols, rows, cols],
        out_shape=[wide, tall, wide, tall, wide, tall],
        compiler_params=_cparams("parallel", "parallel"),
    )(proj, proj, proj, qg, kg, fcol)


def _fox_do_prep(do, o, *, name):
    T, D = do.shape
    nhp = D // LANE
    tr = _row_tile(T)

    def body(do_ref, o_ref, doa_ref, dota_ref):
        lane = lax.broadcasted_iota(jnp.int32, (tr, LANE), 1)
        lo = lane < HEAD_DIM
        dob = do_ref[...].astype(BF16).astype(F32)
        deltas = _half_cols(dob * o_ref[...], lo)
        both = jnp.concatenate(
            [jnp.where(lo, _head_lanes(dob, h), _aug(lane, AUG_F, _split3(-deltas[h]))) for h in range(2)], axis=1)
        doa_ref[...] = both.astype(BF16)
        dota_ref[...] = both.astype(BF16).T

    blk = pl.BlockSpec((tr, LANE), lambda i, h: (i, h))
    return pl.pallas_call(
        body, name=name, grid=(T // tr, nhp), in_specs=[blk, blk],
        out_specs=[pl.BlockSpec((tr, 2 * LANE), lambda i, h: (i, h)),
                   pl.BlockSpec((2 * LANE, tr), lambda i, h: (h, i))],
        out_shape=[jax.ShapeDtypeStruct((T, 2 * D), BF16), jax.ShapeDtypeStruct((2 * D, T), BF16)],
        compiler_params=_cparams("parallel", "parallel"),
    )(do, o)


def _fox_prep_bwd(proj, dq, dkt, dvt, qg, kg, *, d_model, name):
    T = proj.shape[0]
    nhp = d_model // LANE
    tr = _row_tile(T)

    def body(q_ref, k_ref, dq_ref, dkt_ref, dvt_ref, qg_ref, kg_ref, dqo_ref, dko_ref, dvo_ref, sums_ref):
        first = (pl.program_id(0) == 0) & (pl.program_id(1) == 0)
        lo = lax.broadcasted_iota(jnp.int32, (tr, LANE), 1) < HEAD_DIM

        def pair(x2):
            return jnp.where(lo, x2[:, :LANE], pltpu.roll(x2[:, LANE:], HEAD_DIM, 1))

        def bwd(xv, dxhat, g):
            ms = _half_sums(xv * xv, lo) * (1.0 / HEAD_DIM)
            r = lax.rsqrt(ms + EPS)
            n = xv * r
            dn = dxhat * g
            dx = r * (dn - n * (_half_sums(dn * n, lo) * (1.0 / HEAD_DIM)))
            dg = jnp.sum(dxhat * n, axis=0, keepdims=True)
            return dx, dg + pltpu.roll(dg, HEAD_DIM, 1)

        dxq, dgq = bwd(q_ref[...], pair(dq_ref[...]) * (HEAD_DIM ** -0.5), qg_ref[...])
        dxk, dgk = bwd(k_ref[...], pair(dkt_ref[...].T), kg_ref[...])
        dqo_ref[...] = dxq.astype(BF16)
        dko_ref[...] = dxk.astype(BF16)
        dvo_ref[...] = pair(dvt_ref[...].T.astype(F32)).astype(BF16)
        part = jnp.concatenate([dgq, dgk, jnp.zeros((6, LANE), F32)], axis=0)

        @pl.when(first)
        def _():
            sums_ref[...] = part

        @pl.when(jnp.logical_not(first))
        def _():
            sums_ref[...] += part

    gain = pl.BlockSpec((1, LANE), lambda i, h: (0, 0))
    blk = pl.BlockSpec((tr, LANE), lambda i, h: (i, h))
    tall = pl.BlockSpec((2 * LANE, tr), lambda i, h: (h, i))
    return pl.pallas_call(
        body, name=name, grid=(T // tr, nhp),
        in_specs=[blk, pl.BlockSpec((tr, LANE), lambda i, h: (i, nhp + h)),
                  pl.BlockSpec((tr, 2 * LANE), lambda i, h: (i, h)), tall, tall, gain, gain],
        out_specs=[blk, blk, blk, pl.BlockSpec((8, LANE), lambda i, h: (0, 0))],
        out_shape=[jax.ShapeDtypeStruct((T, d_model), BF16)] * 3 + [jax.ShapeDtypeStruct((8, LANE), F32)],
        compiler_params=_cparams("arbitrary", "arbitrary"),
    )(proj, proj, dq, dkt, dvt, qg, kg)


def _scan_lanes(x, reverse):
    n = x.shape[-1]
    lane = lax.broadcasted_iota(jnp.int32, x.shape, 1)
    sh = 1
    while sh < n:
        if reverse:
            x = x + jnp.where(lane < n - sh, pltpu.roll(x, n - sh, 1), 0.0)
        else:
            x = x + jnp.where(lane >= sh, pltpu.roll(x, sh, 1), 0.0)
        sh *= 2
    return x


def _fox_gate_fwd(fpre_t, bf, *, name):
    def body(f_ref, b_ref, o_ref):
        xv = f_ref[...] + b_ref[...]
        logf = jnp.minimum(xv, 0.0) - jnp.log1p(jnp.exp(-jnp.abs(xv)))
        o_ref[...] = _scan_lanes(logf, reverse=False)

    return pl.pallas_call(body, name=name, out_shape=jax.ShapeDtypeStruct(fpre_t.shape, F32))(fpre_t, bf)


def _fox_gate_bwd(dcol, drow, fpre_t, bf, *, name):
    H = fpre_t.shape[0]

    def body(dc_ref, dr_ref, f_ref, b_ref, o_ref, db_ref):
        xv = f_ref[...] + b_ref[...]
        e = dc_ref[...] - dr_ref[...]
        dlogf = _scan_lanes(e, reverse=False) - e
        dpre = dlogf * (1.0 - jax.nn.sigmoid(xv))
        o_ref[...] = dpre
        db_ref[...] = jnp.broadcast_to(jnp.sum(dpre, axis=-1, keepdims=True), (H, LANE))

    return pl.pallas_call(
        body, name=name,
        out_shape=[jax.ShapeDtypeStruct(fpre_t.shape, F32), jax.ShapeDtypeStruct((H, LANE), F32)],
    )(dcol, drow, fpre_t, bf)


_NT = (((1,), (1,)), ((), ()))
_TN = (((0,), (0,)), ((), ()))
_NN = (((1,), (0,)), ((), ()))


def _attn_tile(T):
    return min(T, 512)


def _causal(tq, tk):
    return lax.broadcasted_iota(jnp.int32, (tq, tk), 1) <= lax.broadcasted_iota(jnp.int32, (tq, tk), 0)


def _fox_attn_fwd(qa, kta, va, *, name, comm=None):
    T = qa.shape[0]
    nhp = qa.shape[1] // (2 * LANE)
    tq = tk = _attn_tile(T)
    nq = T // tq

    def body(*refs):
        (qa_ref, kta_ref, va_ref), src_refs, (o_ref, qb_ref), dst_refs, (m_sc, l_sc, acc_sc), sems = (
            _split_comm_refs(refs, 3, 2, 3, comm))
        hp, i, j = pl.program_id(0), pl.program_id(1), pl.program_id(2)
        if comm:
            @pl.when((hp == 0) & (i == 0) & (j == 0))
            def _():
                _comm_start(_comm_copies(comm.plan, src_refs, dst_refs, *sems))

        @pl.when(j == 0)
        def _():
            m_sc[...] = jnp.full(m_sc.shape, NEG, F32)
            l_sc[...] = jnp.zeros(l_sc.shape, F32)
            acc_sc[...] = jnp.zeros(acc_sc.shape, F32)

        def block(diagonal):
            heads = [slice(h * LANE, (h + 1) * LANE) for h in range(2)]
            scores = [lax.dot_general(qa_ref[:, hs], kta_ref[hs, :], _NN, preferred_element_type=F32)
                      for hs in heads]
            state = [(m_sc[h], l_sc[h], acc_sc[h]) for h in range(2)]
            probs, updates = [], []
            for s, (m_prev, l_prev, _) in zip(scores, state):
                if diagonal:
                    s = jnp.where(_causal(tq, tk), s, NEG)
                m_next = jnp.maximum(m_prev, jnp.max(s, axis=1, keepdims=True))
                p = jnp.exp(s - jnp.tile(m_next, (1, tk // LANE)))
                alpha = jnp.exp(m_prev - m_next)
                probs.append(p.astype(BF16))
                updates.append((m_next, alpha, alpha * l_prev + jnp.sum(p, axis=1, keepdims=True)))
            pvs = [lax.dot_general(p, va_ref[:, hs], _NN, preferred_element_type=F32)
                   for p, hs in zip(probs, heads)]
            for h in range(2):
                m_next, alpha, l_next = updates[h]
                m_sc[h] = m_next
                l_sc[h] = l_next
                acc_sc[h] = alpha * state[h][2] + pvs[h]

        @pl.when(j < i)
        def _():
            block(False)

        @pl.when(j == i)
        def _():
            block(True)
            o_ref[...] = acc_sc[0] / l_sc[0] + acc_sc[1] / l_sc[1]
            lane = lax.broadcasted_iota(jnp.int32, (tq, LANE), 1)
            for h in range(2):
                hs = slice(h * LANE, (h + 1) * LANE)
                pieces = _split3(-(m_sc[h] + jnp.log(l_sc[h])))
                qb = qa_ref[:, hs].astype(F32)
                for n, piece in enumerate(pieces):
                    qb = jnp.where(lane == AUG_LSE + n, piece, qb)
                qb_ref[:, hs] = qb.astype(BF16)

        if comm:
            @pl.when((hp == nhp - 1) & (i == nq - 1) & (j == nq - 1))
            def _():
                _comm_wait(_comm_copies(comm.plan, src_refs, dst_refs, *sems))

    outs = pl.pallas_call(
        body, name=name, grid=(nhp, nq, nq),
        in_specs=[pl.BlockSpec((tq, 2 * LANE), lambda h, i, j: (i, h)),
                  pl.BlockSpec((2 * LANE, tk), lambda h, i, j: (h, jnp.minimum(j, i))),
                  pl.BlockSpec((tk, 2 * LANE), lambda h, i, j: (jnp.minimum(j, i), h))]
        + ([ANY] * len(comm.srcs) if comm else []),
        out_specs=[pl.BlockSpec((tq, LANE), lambda h, i, j: (i, h)),
                   pl.BlockSpec((tq, 2 * LANE), lambda h, i, j: (i, h))]
        + ([ANY] * len(comm.out_shapes) if comm else []),
        out_shape=[jax.ShapeDtypeStruct((T, nhp * LANE), F32), jax.ShapeDtypeStruct(qa.shape, BF16)]
        + (list(comm.out_shapes) if comm else []),
        scratch_shapes=[pltpu.VMEM((2, tq, LANE), F32), pltpu.VMEM((2, tq, LANE), F32),
                        pltpu.VMEM((2, tq, LANE), F32)] + (comm.scratch() if comm else []),
        compiler_params=(_cparams("arbitrary", "arbitrary", "arbitrary") if comm
                         else _cparams("parallel", "parallel", "arbitrary")),
    )(qa, kta, va, *(comm.srcs if comm else []))
    return outs[0], outs[1], outs[2:]


def _fox_attn_bwd(qb, qta, ka, kta, vta, doa, dota, *, name, comm=None):
    T = qb.shape[0]
    nhp = qb.shape[1] // (2 * LANE)
    tq = tk = _attn_tile(T)
    nq = T // tq

    def body(*refs):
        ((qb_ref, qta_ref, ka_ref, kta_ref, vta_ref, doa_ref, dota_ref), src_refs,
         (dq_ref, dkt_ref, dvt_ref, dcol_ref, drow_ref), dst_refs, (dkt_sc, dvt_sc, dcol_sc), sems) = (
            _split_comm_refs(refs, 7, 5, 3, comm))
        hp, j, i = pl.program_id(0), pl.program_id(1), pl.program_id(2)
        if comm:
            @pl.when((hp == 0) & (j == 0) & (i == 0))
            def _():
                _comm_start(_comm_copies(comm.plan, src_refs, dst_refs, *sems))

        @pl.when((j == 0) & (i == 0))
        def _():
            dq_ref[...] = jnp.zeros(dq_ref.shape, F32)
            drow_ref[...] = jnp.zeros(drow_ref.shape, F32)

        @pl.when(i == 0)
        def _():
            dkt_sc[...] = jnp.zeros(dkt_sc.shape, F32)
            dvt_sc[...] = jnp.zeros(dvt_sc.shape, F32)
            dcol_sc[...] = jnp.zeros(dcol_sc.shape, F32)

        def block(diagonal):
            rows = pl.ds(pl.multiple_of(i * tq, tq), tq)
            heads = [slice(h * LANE, (h + 1) * LANE) for h in range(2)]
            logits = [lax.dot_general(qb_ref[:, hs], kta_ref[hs, :], _NN, preferred_element_type=F32)
                      for hs in heads]
            dpds = [lax.dot_general(doa_ref[:, hs], vta_ref[hs, :], _NN, preferred_element_type=F32)
                    for hs in heads]
            pbs, dlbs = [], []
            for h in range(2):
                p = jnp.exp(logits[h])
                if diagonal:
                    p = jnp.where(_causal(tq, tk), p, 0.0)
                dl = p * dpds[h]
                pbs.append(p.astype(BF16))
                dlbs.append(dl.astype(BF16))
                dcol_sc[h] += jnp.sum(dl, axis=0, keepdims=True)
                drow_ref[h, rows, :] += jnp.sum(dl, axis=1, keepdims=True)
            for h, hs in enumerate(heads):
                dvt_sc[h] += lax.dot_general(dota_ref[hs, :], pbs[h], _NN, preferred_element_type=F32)
                dkt_sc[h] += lax.dot_general(qta_ref[hs, :], dlbs[h], _NN, preferred_element_type=F32)
                dq_ref[rows, hs] += lax.dot_general(dlbs[h], ka_ref[:, hs], _NN, preferred_element_type=F32)

        @pl.when(i > j)
        def _():
            block(False)

        @pl.when(i == j)
        def _():
            block(True)

        @pl.when(i == nq - 1)
        def _():
            dkt_ref[...] = jnp.concatenate([dkt_sc[0], dkt_sc[1]], axis=0)
            dvt_ref[...] = jnp.concatenate([dvt_sc[0], dvt_sc[1]], axis=0).astype(BF16)
            dcol_ref[...] = dcol_sc[...]

        if comm:
            @pl.when((hp == nhp - 1) & (j == nq - 1) & (i == nq - 1))
            def _():
                _comm_wait(_comm_copies(comm.plan, src_refs, dst_refs, *sems))

    qrow = pl.BlockSpec((tq, 2 * LANE), lambda h, j, i: (jnp.maximum(i, j), h))
    qcol = pl.BlockSpec((2 * LANE, tq), lambda h, j, i: (h, jnp.maximum(i, j)))
    krow = pl.BlockSpec((tk, 2 * LANE), lambda h, j, i: (j, h))
    kcol = pl.BlockSpec((2 * LANE, tk), lambda h, j, i: (h, j))
    tall = jax.ShapeDtypeStruct((qb.shape[1], T), F32)
    outs = pl.pallas_call(
        body, name=name, grid=(nhp, nq, nq),
        in_specs=[qrow, qcol, krow, kcol, kcol, qrow, qcol] + ([ANY] * len(comm.srcs) if comm else []),
        out_specs=[pl.BlockSpec((T, 2 * LANE), lambda h, j, i: (0, h)), kcol, kcol,
                   pl.BlockSpec((2, 1, tk), lambda h, j, i: (h, 0, j)),
                   pl.BlockSpec((2, T, 1), lambda h, j, i: (h, 0, 0))]
        + ([ANY] * len(comm.out_shapes) if comm else []),
        out_shape=[jax.ShapeDtypeStruct(qb.shape, F32), tall, jax.ShapeDtypeStruct(tall.shape, BF16),
                   jax.ShapeDtypeStruct((2 * nhp, 1, T), F32), jax.ShapeDtypeStruct((2 * nhp, T, 1), F32)]
        + (list(comm.out_shapes) if comm else []),
        scratch_shapes=[pltpu.VMEM((2, LANE, tk), F32), pltpu.VMEM((2, LANE, tk), F32),
                        pltpu.VMEM((2, 1, tk), F32)] + (comm.scratch() if comm else []),
        compiler_params=_cparams("arbitrary" if comm else "parallel", "arbitrary", "arbitrary"),
    )(qb, qta, ka, kta, vta, doa, dota, *(comm.srcs if comm else []))
    return (*outs[:5], outs[5:])


_GELU_C = math.sqrt(2.0 / math.pi)
_GELU_A = 0.044715


def _gelu(x):
    t = jnp.tanh(_GELU_C * (x + _GELU_A * (x * x * x)))
    return x * (0.5 * (1.0 + t)), t


def _gelu_grad(x, t):
    return 0.5 * (1.0 + t) + 0.5 * x * (1.0 - t * t) * (_GELU_C * (1.0 + 3.0 * _GELU_A * x * x))


def _layer_norm_stats(v):
    mu = jnp.mean(v, axis=-1, keepdims=True)
    vc = v - mu
    rstd = lax.rsqrt(jnp.mean(vc * vc, axis=-1, keepdims=True) + EPS)
    return vc * rstd, rstd


def _layer_norm_bwd(dyhat, yhat, rstd):
    return rstd * (dyhat - jnp.mean(dyhat, axis=-1, keepdims=True)
                   - yhat * jnp.mean(dyhat * yhat, axis=-1, keepdims=True))


def _sg_mask():
    t = lax.broadcasted_iota(jnp.int32, (SG_CHUNK, SG_CHUNK), 0) // SG_CAUSAL
    s = lax.broadcasted_iota(jnp.int32, (SG_CHUNK, SG_CHUNK), 1) // SG_CAUSAL
    return s <= t


def _sg_mix(ws_ref, bc_ref, vln_sc, vo_sc, tr, gd):
    mask = _sg_mask()
    for g in range(SG_GROUPS):
        wg = jnp.where(mask, ws_ref[g], 0.0).astype(BF16)
        cols = slice(g * gd, (g + 1) * gd)
        for n in range(tr // SG_CHUNK):
            rows = slice(n * SG_CHUNK, (n + 1) * SG_CHUNK)
            vo_sc[rows, cols] = lax.dot_general(wg, vln_sc[rows, cols], _NN,
                                                preferred_element_type=F32) + bc_ref[g]


def _sg_fwd(a_uv, ln_g, ln_b, ws, bcol, *, name):
    T, W = a_uv.shape[0], a_uv.shape[1] // 2
    gd = W // SG_GROUPS
    tr = _row_tile(T)

    def body(u_ref, v_ref, g_ref, b_ref, ws_ref, bc_ref, o_ref, vln_sc, vo_sc):
        u, _ = _gelu(u_ref[...])
        v, _ = _gelu(v_ref[...])
        vhat, _ = _layer_norm_stats(v)
        vln_sc[...] = (vhat * g_ref[...] + b_ref[...]).astype(BF16)
        _sg_mix(ws_ref, bc_ref, vln_sc, vo_sc, tr, gd)
        o_ref[...] = (u * vo_sc[...]).astype(BF16)

    row = pl.BlockSpec((1, W), lambda i: (0, 0))
    return pl.pallas_call(
        body, name=name, grid=(T // tr,),
        in_specs=[pl.BlockSpec((tr, W), lambda i: (i, 0)), pl.BlockSpec((tr, W), lambda i: (i, 1)), row, row,
                  pl.BlockSpec((SG_GROUPS, SG_CHUNK, SG_CHUNK), lambda i: (0, 0, 0)),
                  pl.BlockSpec((SG_GROUPS, SG_CHUNK, 1), lambda i: (0, 0, 0))],
        out_specs=pl.BlockSpec((tr, W), lambda i: (i, 0)),
        out_shape=jax.ShapeDtypeStruct((T, W), BF16),
        scratch_shapes=[pltpu.VMEM((tr, W), BF16), pltpu.VMEM((tr, W), F32)],
        compiler_params=_cparams("parallel"),
    )(a_uv, a_uv, ln_g, ln_b, ws, bcol)


def _sg_bwd(a_uv, dgate, ln_g, ln_b, ws, bcol, *, name):
    T, W = a_uv.shape[0], a_uv.shape[1] // 2
    gd = W // SG_GROUPS
    tr = _row_tile(T)

    def body(u_ref, v_ref, dg_ref, g_ref, b_ref, ws_ref, bc_ref,
             da_ref, dws_ref, dbs_ref, sums_ref, vln_sc, vo_sc, dvo_sc, dvln_sc):
        i = pl.program_id(0)

        @pl.when(i == 0)
        def _():
            dws_ref[...] = jnp.zeros(dws_ref.shape, F32)
            dbs_ref[...] = jnp.zeros(dbs_ref.shape, F32)
            sums_ref[...] = jnp.zeros(sums_ref.shape, F32)

        ua, va = u_ref[...], v_ref[...]
        u, tu = _gelu(ua)
        v, tv = _gelu(va)
        vhat, rstd = _layer_norm_stats(v)
        vln_sc[...] = (vhat * g_ref[...] + b_ref[...]).astype(BF16)
        _sg_mix(ws_ref, bc_ref, vln_sc, vo_sc, tr, gd)
        dgt = dg_ref[...]
        du = dgt * vo_sc[...]
        dvo_sc[...] = dgt * u
        mask = _sg_mask()
        for g in range(SG_GROUPS):
            wg = jnp.where(mask, ws_ref[g], 0.0).astype(BF16)
            cols = slice(g * gd, (g + 1) * gd)
            acc_w = jnp.zeros((SG_CHUNK, SG_CHUNK), F32)
            acc_b = jnp.zeros((SG_CHUNK, 1), F32)
            for n in range(tr // SG_CHUNK):
                rows = slice(n * SG_CHUNK, (n + 1) * SG_CHUNK)
                dvo = dvo_sc[rows, cols]
                dvob = dvo.astype(BF16)
                dvln_sc[rows, cols] = lax.dot_general(wg, dvob, _TN, preferred_element_type=F32)
                acc_w += lax.dot_general(dvob, vln_sc[rows, cols], _NT, preferred_element_type=F32)
                acc_b += jnp.sum(dvo, axis=1, keepdims=True)
            dws_ref[g] += jnp.where(mask, acc_w, 0.0)
            dbs_ref[g] += acc_b
        dvln = dvln_sc[...]
        sums_ref[...] += jnp.concatenate([jnp.sum(dvln * vhat, axis=0, keepdims=True),
                                          jnp.sum(dvln, axis=0, keepdims=True),
                                          jnp.zeros((6, W), F32)], axis=0)
        dv = _layer_norm_bwd(dvln * g_ref[...], vhat, rstd)
        da_ref[:, :W] = (du * _gelu_grad(ua, tu)).astype(BF16)
        da_ref[:, W:] = (dv * _gelu_grad(va, tv)).astype(BF16)

    row = pl.BlockSpec((1, W), lambda i: (0, 0))
    wspec = pl.BlockSpec((SG_GROUPS, SG_CHUNK, SG_CHUNK), lambda i: (0, 0, 0))
    bspec = pl.BlockSpec((SG_GROUPS, SG_CHUNK, 1), lambda i: (0, 0, 0))
    return pl.pallas_call(
        body, name=name, grid=(T // tr,),
        in_specs=[pl.BlockSpec((tr, W), lambda i: (i, 0)), pl.BlockSpec((tr, W), lambda i: (i, 1)),
                  pl.BlockSpec((tr, W), lambda i: (i, 0)), row, row, wspec, bspec],
        out_specs=[pl.BlockSpec((tr, 2 * W), lambda i: (i, 0)), wspec, bspec,
                   pl.BlockSpec((8, W), lambda i: (0, 0))],
        out_shape=[jax.ShapeDtypeStruct((T, 2 * W), BF16),
                   jax.ShapeDtypeStruct((SG_GROUPS, SG_CHUNK, SG_CHUNK), F32),
                   jax.ShapeDtypeStruct((SG_GROUPS, SG_CHUNK, 1), F32),
                   jax.ShapeDtypeStruct((8, W), F32)],
        scratch_shapes=[pltpu.VMEM((tr, W), BF16), pltpu.VMEM((tr, W), F32),
                        pltpu.VMEM((tr, W), F32), pltpu.VMEM((tr, W), F32)],
        compiler_params=_cparams("arbitrary"),
    )(a_uv, a_uv, dgate, ln_g, ln_b, ws, bcol)


SUBLANES = 8


def _shift_rows(xc_sc, xs_sc):
    rows = xs_sc.shape[1]
    for p in range(1, SUBLANES):
        xs_sc[p - 1] = xc_sc[pl.ds(p, rows), :]


def _rows_at(xc_sc, xs_sc, offset, tr):
    p = offset % SUBLANES
    base = offset - p
    return xc_sc[pl.ds(base, tr), :] if p == 0 else xs_sc[p - 1, pl.ds(base, tr), :]


def _shift_scratch(tr, C):
    return pltpu.VMEM((SUBLANES - 1, tr + CONV_HALO - SUBLANES, C), F32)


def _cv_glu_conv(a_ref, b_ref, ap_ref, bp_ref, w_ref, bd_ref, xc_sc, xs_sc, tr):
    i = pl.program_id(0)
    prev = ap_ref[...] * jax.nn.sigmoid(bp_ref[...])
    xc_sc[0:CONV_HALO, :] = jnp.where(i > 0, prev, 0.0)
    xc_sc[CONV_HALO:, :] = a_ref[...] * jax.nn.sigmoid(b_ref[...])
    _shift_rows(xc_sc, xs_sc)
    acc = jnp.broadcast_to(bd_ref[...], (tr, bd_ref.shape[1]))
    for k in range(CONV_WIDTH):
        acc = acc + w_ref[k:k + 1, :] * _rows_at(xc_sc, xs_sc, CONV_HALO - (CONV_WIDTH - 1) + k, tr)
    return acc


def _cv_specs(T, C, tr):
    hb = tr // CONV_HALO
    cur = lambda col: pl.BlockSpec((tr, C), lambda i: (i, col))
    prev = lambda col: pl.BlockSpec((CONV_HALO, C), lambda i: (jnp.maximum(i * hb - 1, 0), col))
    row = pl.BlockSpec((1, C), lambda i: (0, 0))
    wspec = pl.BlockSpec((CONV_HALO, C), lambda i: (0, 0))
    return cur, prev, row, wspec


def _cv_fwd(p, w_dw, b_dw, ln_g, ln_b, *, name):
    T, C = p.shape[0], p.shape[1] // 2
    tr = _row_tile(T)
    cur, prev, row, wspec = _cv_specs(T, C, tr)

    def body(a_ref, b_ref, ap_ref, bp_ref, w_ref, bd_ref, g_ref, be_ref, o_ref, xc_sc, xs_sc):
        y2 = _cv_glu_conv(a_ref, b_ref, ap_ref, bp_ref, w_ref, bd_ref, xc_sc, xs_sc, tr)
        yhat, _ = _layer_norm_stats(y2)
        yln = yhat * g_ref[...] + be_ref[...]
        o_ref[...] = (yln * jax.nn.sigmoid(yln)).astype(BF16)

    return pl.pallas_call(
        body, name=name, grid=(T // tr,),
        in_specs=[cur(0), cur(1), prev(0), prev(1), wspec, row, row, row],
        out_specs=pl.BlockSpec((tr, C), lambda i: (i, 0)),
        out_shape=jax.ShapeDtypeStruct((T, C), BF16),
        scratch_shapes=[pltpu.VMEM((tr + CONV_HALO, C), F32), _shift_scratch(tr, C)],
        compiler_params=_cparams("parallel"),
    )(p, p, p, p, w_dw, b_dw, ln_g, ln_b)


def _cv_bwd_ln(p, dy3, w_dw, b_dw, ln_g, ln_b, *, name):
    T, C = p.shape[0], p.shape[1] // 2
    tr = _row_tile(T)
    cur, prev, row, wspec = _cv_specs(T, C, tr)

    def body(a_ref, b_ref, ap_ref, bp_ref, dy_ref, w_ref, bd_ref, g_ref, be_ref,
             dy2_ref, dw_ref, sums_ref, xc_sc, xs_sc):
        i = pl.program_id(0)
        y2 = _cv_glu_conv(a_ref, b_ref, ap_ref, bp_ref, w_ref, bd_ref, xc_sc, xs_sc, tr)
        yhat, rstd = _layer_norm_stats(y2)
        yln = yhat * g_ref[...] + be_ref[...]
        s = jax.nn.sigmoid(yln)
        dyln = dy_ref[...] * (s + yln * s * (1.0 - s))
        dy2 = _layer_norm_bwd(dyln * g_ref[...], yhat, rstd)
        dy2_ref[...] = dy2
        sums = jnp.concatenate([jnp.sum(dy2, axis=0, keepdims=True),
                                jnp.sum(dyln * yhat, axis=0, keepdims=True),
                                jnp.sum(dyln, axis=0, keepdims=True),
                                jnp.zeros((5, C), F32)], axis=0)
        taps = [jnp.sum(dy2 * _rows_at(xc_sc, xs_sc, CONV_HALO - (CONV_WIDTH - 1) + k, tr), axis=0, keepdims=True)
                for k in range(CONV_WIDTH)]
        dw = jnp.concatenate(taps + [jnp.zeros((CONV_HALO - CONV_WIDTH, C), F32)], axis=0)

        @pl.when(i == 0)
        def _():
            sums_ref[...] = sums
            dw_ref[...] = dw

        @pl.when(i > 0)
        def _():
            sums_ref[...] += sums
            dw_ref[...] += dw

    blk = pl.BlockSpec((tr, C), lambda i: (i, 0))
    return pl.pallas_call(
        body, name=name, grid=(T // tr,),
        in_specs=[cur(0), cur(1), prev(0), prev(1), blk, wspec, row, row, row],
        out_specs=[blk, wspec, pl.BlockSpec((8, C), lambda i: (0, 0))],
        out_shape=[jax.ShapeDtypeStruct((T, C), F32), jax.ShapeDtypeStruct((CONV_HALO, C), F32),
                   jax.ShapeDtypeStruct((8, C), F32)],
        scratch_shapes=[pltpu.VMEM((tr + CONV_HALO, C), F32), _shift_scratch(tr, C)],
        compiler_params=_cparams("arbitrary"),
    )(p, p, p, p, dy3, w_dw, b_dw, ln_g, ln_b)


def _cv_bwd_in(p, dy2, w_dw, *, name):
    T, C = p.shape[0], p.shape[1] // 2
    tr = _row_tile(T)
    hb = tr // CONV_HALO
    nblk = T // tr
    last_halo = T // CONV_HALO - 1

    def body(a_ref, b_ref, dy_ref, dyn_ref, w_ref, dp_ref, sums_ref, xc_sc, xs_sc):
        i = pl.program_id(0)
        xc_sc[0:tr, :] = dy_ref[...]
        xc_sc[tr:, :] = jnp.where(i < nblk - 1, dyn_ref[...], 0.0)
        _shift_rows(xc_sc, xs_sc)
        dy1 = jnp.zeros((tr, C), F32)
        for k in range(CONV_WIDTH):
            dy1 = dy1 + w_ref[k:k + 1, :] * _rows_at(xc_sc, xs_sc, CONV_WIDTH - 1 - k, tr)
        a = a_ref[...]
        sb = jax.nn.sigmoid(b_ref[...])
        da = dy1 * sb
        db = dy1 * a * sb * (1.0 - sb)
        dp_ref[:, :C] = da.astype(BF16)
        dp_ref[:, C:] = db.astype(BF16)
        sums = jnp.concatenate([
            jnp.concatenate([jnp.sum(da, axis=0, keepdims=True), jnp.sum(db, axis=0, keepdims=True)], axis=1),
            jnp.zeros((7, 2 * C), F32)], axis=0)

        @pl.when(i == 0)
        def _():
            sums_ref[...] = sums

        @pl.when(i > 0)
        def _():
            sums_ref[...] += sums

    blk = lambda col: pl.BlockSpec((tr, C), lambda i: (i, col))
    return pl.pallas_call(
        body, name=name, grid=(nblk,),
        in_specs=[blk(0), blk(1), blk(0),
                  pl.BlockSpec((CONV_HALO, C), lambda i: (jnp.minimum((i + 1) * hb, last_halo), 0)),
                  pl.BlockSpec((CONV_HALO, C), lambda i: (0, 0))],
        out_specs=[pl.BlockSpec((tr, 2 * C), lambda i: (i, 0)), pl.BlockSpec((8, 2 * C), lambda i: (0, 0))],
        out_shape=[jax.ShapeDtypeStruct((T, 2 * C), BF16), jax.ShapeDtypeStruct((8, 2 * C), F32)],
        scratch_shapes=[pltpu.VMEM((tr + CONV_HALO, C), F32), _shift_scratch(tr, C)],
        compiler_params=_cparams("arbitrary"),
    )(p, p, dy2, dy2, w_dw)


def _col_tile(n, want=1024):
    best = LANE
    for t in range(LANE, min(n, want) + 1, LANE):
        if n % t == 0:
            best = t
    return best if n % LANE == 0 else n


def _mm(a, b, *, name, ta=False, tb=False, **kw):
    M = a.shape[1] if ta else a.shape[0]
    N = b.shape[0] if tb else b.shape[1]
    K = a.shape[0] if ta else a.shape[1]
    kw.setdefault('tm', _col_tile(M, 1024 if ta else 512))
    kw.setdefault('tn', _col_tile(N, 1024))
    kw.setdefault('tk', K if tb else _col_tile(K, 2048 if ta else 1024))
    return _matmul(a, b, name=name, ta=ta, tb=tb, **kw)


WIDE_ROWS = 1024


def _relu2_epilogue(acc):
    r = jnp.maximum(acc, 0.0)
    return (r * r,)


def _residual_epilogue(acc, x, g):
    return acc, x + g * acc


def _residual_bias_epilogue(acc, x, g, b):
    y = acc + b
    return y, x + g * y


def _dh_norm_bwd(d_act, w, x, dres, nw, sc, *, name, gate=None):
    D = x.shape[1]

    def epilogue(dh, xv, dresv, wv, scv, *gated):
        r = lax.rsqrt(jnp.mean(xv * xv, axis=-1, keepdims=True) + EPS)
        n = xv * r
        scale = 1.0 + scv
        dn = dh * (wv * scale)
        dx = dresv + r * (dn - n * jnp.mean(dn * n, axis=-1, keepdims=True))
        rows = [jnp.sum(dh, axis=0, keepdims=True),
                jnp.sum(dh * (n * wv), axis=0, keepdims=True),
                jnp.sum(dh * n * scale, axis=0, keepdims=True)]
        outs = [dx]
        if gated:
            yv, gv = gated
            outs.append(dx * gv)
            rows += [jnp.sum(dx * yv, axis=0, keepdims=True), jnp.sum(dx * gv, axis=0, keepdims=True)]
        return (*outs, jnp.concatenate(rows + [jnp.zeros((8 - len(rows), D), F32)], axis=0))

    extras = [(x, 'tile'), (dres, 'tile'), (nw, 'row'), (sc, 'row')]
    if gate:
        extras += [(gate[0], 'tile'), (gate[1], 'row')]
    return _mm(d_act, w, tb=True, name=name, tn=D, extras=extras, epilogue=epilogue,
               out_dtypes=(F32, BF16) if gate else (F32,), row_sums=True)


def _relu2_bwd_epilogue(acc, r):
    return (acc * (2.0 * jnp.sqrt(r.astype(F32))),)


def _bias_epilogue(acc, b):
    return (acc + b,)


def _fox_forward(h1, P, j, D, comm=None):
    H = D // HEAD_DIM
    proj = _mm(h1, P['fox_w_in'][j], name='fox_proj', b_outer=True, tm=WIDE_ROWS)
    qg = jnp.tile(P['fox_q_norm'][j][None, :], (1, 2))
    kg = jnp.tile(P['fox_k_norm'][j][None, :], (1, 2))
    fpre_t = proj[:, 3 * D:3 * D + H].T
    bf = P['fox_b_f'][j][:, None]
    fcum = _fox_gate_fwd(fpre_t, bf, name='fox_gate_fwd')
    qa, qta, ka, kta, va, vta = _fox_prep_fwd(proj, qg, kg, fcum[:, :, None], d_model=D, name='fox_prep_fwd')
    o, qb, comm_outs = _fox_attn_fwd(qa, kta, va, name='fox_attn_fwd', comm=comm)
    saved = dict(proj=proj, qg=qg, kg=kg, fpre_t=fpre_t, bf=bf, o=o, qb=qb, qta=qta, ka=ka, kta=kta, vta=vta)
    return o, saved, comm_outs


def _fox_backward(dy, h1, S, P, j, D, comm=None):
    H = D // HEAD_DIM
    w_out, w_in = P['fox_w_out'][j], P['fox_w_in'][j]
    g = {}
    g['fox_w_out'] = _mm(S['o'], dy, ta=True, name='fox_dw_out')
    do = _mm(dy, w_out, tb=True, name='fox_do')
    doa, dota = _fox_do_prep(do, S['o'], name='fox_do_prep')
    dq, dkt, dvt, dcol, drow, comm_outs = _fox_attn_bwd(S['qb'], S['qta'], S['ka'], S['kta'], S['vta'], doa, dota,
                                                        name='fox_attn_bwd', comm=comm)
    dqp, dkp, dvp, gsum = _fox_prep_bwd(S['proj'], dq, dkt, dvt, S['qg'], S['kg'], d_model=D, name='fox_prep_bwd')
    dfpre_t, dbf = _fox_gate_bwd(dcol[:, 0, :], drow[:, :, 0], S['fpre_t'], S['bf'], name='fox_gate_bwd')
    dfpre = jnp.pad(dfpre_t.T.astype(BF16), ((0, 0), (0, LANE - H)))
    dproj = jnp.concatenate([dqp, dkp, dvp, dfpre], axis=1)
    g['fox_w_in'] = _mm(h1, dproj, ta=True, name='fox_dw_in')[:, :3 * D + H]
    g['fox_b_f'] = dbf[:, 0]
    g['fox_q_norm'] = gsum[0, :HEAD_DIM]
    g['fox_k_norm'] = gsum[1, :HEAD_DIM]
    return (dproj, w_in), g, comm_outs


def _sg_forward(h1, P, D):
    a_uv = _mm(h1, P['sg_w_in'], name='sg_in', b_outer=True, tm=WIDE_ROWS)
    bcol = P['sg_b_s'][:, :, None]
    gate = _sg_fwd(a_uv, P['sg_ln_g'], P['sg_ln_b'], P['sg_w_s'], bcol, name='sg_fwd')
    return gate, dict(a_uv=a_uv, bcol=bcol, gate=gate)


def _sg_backward(dy, h1, S, P, D):
    g = {}
    g['sg_w_out'] = _mm(S['gate'], dy, ta=True, name='sg_dw_out')
    dgate = _mm(dy, P['sg_w_out'], tb=True, name='sg_dgate')
    da, dws, dbs, sums = _sg_bwd(S['a_uv'], dgate, P['sg_ln_g'], P['sg_ln_b'], P['sg_w_s'], S['bcol'],
                                 name='sg_bwd')
    g['sg_w_s'], g['sg_b_s'] = dws, dbs[:, :, 0]
    g['sg_ln_g'], g['sg_ln_b'] = sums[0], sums[1]
    g['sg_w_in'] = _mm(h1, da, ta=True, name='sg_dw_in', out_chips=N_CHIPS)
    return (da, P['sg_w_in']), g


def _cv_forward(h1, P, D):
    p = _mm(h1, P['cv_w_pw1'], name='cv_pw1', extras=[(P['cv_b_pw1'], 'row')], epilogue=_bias_epilogue,
            b_outer=True, tm=WIDE_ROWS)
    w_dw = jnp.pad(P['cv_w_dw'], ((0, CONV_HALO - CONV_WIDTH), (0, 0)))
    y3 = _cv_fwd(p, w_dw, P['cv_b_dw'], P['cv_ln_g'], P['cv_ln_b'], name='cv_fwd')
    return y3, dict(p=p, w_dw=w_dw, y3=y3)


def _cv_backward(dy, h1, S, P, D):
    g = {}
    g['cv_w_pw2'] = _mm(S['y3'], dy, ta=True, name='cv_dw_pw2')
    dy3 = _mm(dy, P['cv_w_pw2'], tb=True, name='cv_dy3')
    dy2, dw, sums = _cv_bwd_ln(S['p'], dy3, S['w_dw'], P['cv_b_dw'], P['cv_ln_g'], P['cv_ln_b'], name='cv_bwd_ln')
    g['cv_w_dw'] = dw[:CONV_WIDTH]
    g['cv_b_dw'], g['cv_ln_g'], g['cv_ln_b'] = sums[0], sums[1], sums[2]
    dp, psum = _cv_bwd_in(S['p'], dy2, S['w_dw'], name='cv_bwd_in')
    g['cv_b_pw1'] = psum[0]
    g['cv_w_pw1'] = _mm(h1, dp, ta=True, name='cv_dw_pw1', out_chips=N_CHIPS)
    return (dp, P['cv_w_pw1']), g


class Hooks(NamedTuple):
    fwd_comm: Comm
    fwd_done: Callable
    bwd_comm: Callable
    bwd_done: Callable


def _local_step(x, target, mod, P, hooks=None):
    T, D = x.shape
    L = mod.shape[0]
    saved = []
    for i in range(L):
        kind, j = i % N_MIXERS, i // N_MIXERS
        m = [mod[i:i + 1, k * D:(k + 1) * D] for k in range(6)]
        sh_m, sc_m, g_m, sh_f, sc_f, g_f = m
        w_mix, w_mlp = P['norm_mix'][i:i + 1], P['norm_mlp'][i:i + 1]
        h1 = _norm_mod_fwd(x, w_mix, sc_m, sh_m, name='norm_mix_fwd')
        if kind == 0:
            carried = hooks is not None and i == 0
            op, S, comm_outs = _fox_forward(h1, P, j, D, comm=hooks.fwd_comm if carried else None)
            if carried:
                hooks.fwd_done(comm_outs)
            y, x1 = _mm(op, P['fox_w_out'][j], name='fox_out', extras=[(x, 'tile'), (g_m, 'row')],
                        epilogue=_residual_epilogue, out_dtypes=(F32, F32))
        elif kind == 1:
            op, S = _sg_forward(h1, P, D)
            y, x1 = _mm(op, P['sg_w_out'], name='sg_out', extras=[(x, 'tile'), (g_m, 'row')],
                        epilogue=_residual_epilogue, out_dtypes=(F32, F32))
        else:
            op, S = _cv_forward(h1, P, D)
            y, x1 = _mm(op, P['cv_w_pw2'], name='cv_out',
                        extras=[(x, 'tile'), (g_m, 'row'), (P['cv_b_pw2'], 'row')],
                        epilogue=_residual_bias_epilogue, out_dtypes=(F32, F32))
        h2 = _norm_mod_fwd(x1, w_mlp, sc_f, sh_f, name='norm_mlp_fwd')
        r = _mm(h2, P['w_mlp_in'][i], name='mlp_in', epilogue=_relu2_epilogue, out_dtypes=(BF16,),
                b_outer=True, tm=WIDE_ROWS)
        z, x2 = _mm(r, P['w_mlp_out'][i], name='mlp_out', extras=[(x1, 'tile'), (g_f, 'row')],
                    epilogue=_residual_epilogue, out_dtypes=(F32, F32), tk=P['w_mlp_out'][i].shape[0])
        saved.append(dict(x=x, h1=h1, S=S, y=y, x1=x1, h2=h2, r=r, z=z, m=m))
        x = x2

    loss_part, dx = _loss_head(x, target, name='loss_head')

    grads = {k: [None] * L for k in ('norm_mix', 'norm_mlp')}
    mix_grads, mat = {}, {}
    dmod = [None] * L
    for i in reversed(range(L)):
        kind, j = i % N_MIXERS, i // N_MIXERS
        sv = saved[i]
        sh_m, sc_m, g_m, sh_f, sc_f, g_f = sv['m']
        w_mix, w_mlp = P['norm_mix'][i:i + 1], P['norm_mlp'][i:i + 1]
        dz, dgf = _gate_bwd(dx, sv['z'], g_f, name='mlp_gate_bwd')
        mat['w_mlp_out', i] = _mm(sv['r'], dz, ta=True, name='mlp_dw_out')
        da = _mm(dz, P['w_mlp_out'][i], tb=True, name='mlp_da', extras=[(sv['r'], 'tile')],
                 epilogue=_relu2_bwd_epilogue, out_dtypes=(BF16,), b_outer=True, tm=WIDE_ROWS)
        mat['w_mlp_in', i] = _mm(sv['h2'], da, ta=True, name='mlp_dw_in', out_chips=N_CHIPS)
        dx1, dy, sums_f = _dh_norm_bwd(da, P['w_mlp_in'][i], sv['x1'], dx, w_mlp, sc_f, name='mlp_dh',
                                       gate=(sv['y'], g_m))
        if kind == 0:
            carried = hooks is not None and i == 0
            last, g, comm_outs = _fox_backward(dy, sv['h1'], sv['S'], P, j, D,
                                              comm=hooks.bwd_comm(mat) if carried else None)
            if carried:
                hooks.bwd_done(comm_outs)
        elif kind == 1:
            last, g = _sg_backward(dy, sv['h1'], sv['S'], P, D)
        else:
            last, g = _cv_backward(dy, sv['h1'], sv['S'], P, D)
            g['cv_b_pw2'] = sums_f[4]
        for k, val in g.items():
            if k in BIG:
                mat[k, j] = val
            else:
                mix_grads.setdefault(k, {})[j] = val
        dx, sums_m = _dh_norm_bwd(*last, sv['x'], dx1, w_mix, sc_m, name='mix_dh')
        grads['norm_mlp'][i], grads['norm_mix'][i] = sums_f[2], sums_m[2]
        dmod[i] = jnp.concatenate([sums_m[0], sums_m[1], sums_f[3], sums_f[0], sums_f[1], dgf[0]])

    out = {k: jnp.stack(v) for k, v in grads.items()}
    for k, per_j in mix_grads.items():
        out[k] = jnp.stack([per_j[j] for j in sorted(per_j)])
    return loss_part, dx, jnp.stack(dmod), out, mat


def _all_gather8(blocks, *, name):
    n = len(blocks)

    def body(*refs):
        x_refs, out_refs = refs[:n], refs[n:2 * n]
        send_sems, recv_sems, local_sems = refs[2 * n:]
        x, y, c = _position()
        me, sibling = (x, y, c), (x, y, 1 - c)
        chips = [(1 - x, y), (x, 1 - y), (1 - x, 1 - y)]

        def slot(a, px, py, pc):
            return out_refs[a].at[4 * px + 2 * py + pc]

        def copy(a, k, blk, to, src=None):
            return pltpu.make_async_remote_copy(
                src_ref=slot(a, *blk) if src is None else src, dst_ref=slot(a, *blk),
                send_sem=send_sems.at[7 * a + k], recv_sem=recv_sems.at[7 * a + k],
                device_id=to, device_id_type=MESH)

        mine = [pltpu.make_async_copy(x_refs[a], slot(a, *me), local_sems.at[a]) for a in range(n)]
        for cp in mine:
            cp.start()
        first = []
        for j, chip in enumerate(chips):
            first += [copy(a, 1 + j, me, (*chip, c), src=x_refs[a]) for a in range(n)]
        first += [copy(a, 0, me, sibling, src=x_refs[a]) for a in range(n)]
        for cp in first:
            cp.start()
        passed = []
        for j, chip in enumerate(chips):
            for a in range(n):
                copy(a, 1 + j, (*chip, c), me).wait_recv()
                passed.append(copy(a, 4 + j, (*chip, c), sibling))
                passed[-1].start()
        for a in range(n):
            copy(a, 0, sibling, me).wait_recv()
        for j, chip in enumerate(chips):
            for a in range(n):
                copy(a, 4 + j, (*chip, 1 - c), me).wait_recv()
        for cp in first + passed:
            cp.wait_send()
        for cp in mine:
            cp.wait()

    return pl.pallas_call(
        body, name=name, in_specs=[ANY] * n, out_specs=[ANY] * n,
        out_shape=[jax.ShapeDtypeStruct((8,) + b.shape, b.dtype) for b in blocks],
        scratch_shapes=[pltpu.SemaphoreType.DMA((7 * n,)), pltpu.SemaphoreType.DMA((7 * n,)),
                        pltpu.SemaphoreType.DMA((n,))],
    )(*blocks)


def _exchange(comm, *, name, aliases=None):
    ns, no = len(comm.srcs), len(comm.out_shapes)

    def body(*refs):
        copies = _comm_copies(comm.plan, refs[:ns], refs[ns:ns + no], *refs[ns + no:])
        _comm_start(copies)
        _comm_wait(copies)

    return pl.pallas_call(
        body, name=name, in_specs=[ANY] * ns, out_specs=[ANY] * no, out_shape=list(comm.out_shapes),
        scratch_shapes=comm.scratch(), input_output_aliases=aliases or {},
    )(*comm.srcs)


def _gather_comm(halves):
    n = len(halves)

    def plan(src, out, x, y, c):
        mine = 4 * x + 2 * y + c
        remote = [(src[a], out[a].at[mine], (x, y, 1 - c), out[a].at[4 * x + 2 * y + 1 - c]) for a in range(n)]
        for fx, fy in CHIP_FLIPS:
            px, py = _flip(x, fx), _flip(y, fy)
            remote += [(src[a], out[a].at[mine], (px, py, c), out[a].at[4 * px + 2 * py + c]) for a in range(n)]
        return remote, [(src[a], out[a].at[mine]) for a in range(n)]

    return Comm(list(halves), [jax.ShapeDtypeStruct((8,) + h.shape, h.dtype) for h in halves], plan, 4 * n, n)


def _gather_forward(bufs, *, name):
    n = len(bufs)

    def plan(src, out, x, y, c):
        remote = []
        for fx, fy in CHIP_FLIPS:
            px, py = _flip(x, fx), _flip(y, fy)
            remote += [(src[a].at[4 * px + 2 * py + c], out[a].at[4 * px + 2 * py + c], (x, y, 1 - c),
                        out[a].at[4 * px + 2 * py + 1 - c]) for a in range(n)]
        return remote, []

    comm = Comm(list(bufs), [jax.ShapeDtypeStruct(b.shape, b.dtype) for b in bufs], plan, 3 * n, 0)
    return _exchange(comm, name=name, aliases={a: a for a in range(n)})


CHIP_FLIPS = ((1, 0), (0, 1), (1, 1))


def _flip(v, f):
    return 1 - v if f else v


def _sum_rows_tile(R, C, budget=3 << 20):
    best = None
    for t in range(8, R + 1, 8):
        if R % t == 0 and t * C * 4 <= budget:
            best = t
    return best if best is not None else R


def _rs_begin(gps, *, wire_dtype):
    n = len(gps)
    c_arr = jnp.reshape(_position()[2], (1,)).astype(jnp.int32)

    def plan(src, out, x, y, c):
        return [(src[a].at[b, 1 - c], out[a].at[b], (x, y, 1 - c), out[a].at[b])
                for a in range(n) for b in range(4)], []

    got1 = _exchange(Comm(list(gps), [jax.ShapeDtypeStruct((4,) + g.shape[2:], F32) for g in gps], plan, 4 * n, 0),
                     name='rs_sibling')

    def sum_chip(c_ref, mine_ref, got_ref, out_ref):
        out_ref[...] = (mine_ref[...] + got_ref[...]).astype(out_ref.dtype)

    parts = []
    for gp, g1 in zip(gps, got1):
        _, _, R, C = gp.shape
        tr = _sum_rows_tile(R, C)
        parts.append(pl.pallas_call(
            sum_chip, name='rs_sum_chip',
            grid_spec=pltpu.PrefetchScalarGridSpec(
                num_scalar_prefetch=1, grid=(4, R // tr),
                in_specs=[pl.BlockSpec((None, None, tr, C), lambda b, r, cr: (b, cr[0], r, 0)),
                          pl.BlockSpec((None, tr, C), lambda b, r, cr: (b, r, 0))],
                out_specs=pl.BlockSpec((None, tr, C), lambda b, r, cr: (b, r, 0))),
            out_shape=jax.ShapeDtypeStruct((4, R, C), wire_dtype),
            compiler_params=_cparams("parallel", "parallel"),
        )(c_arr, gp, g1))
    return got1, parts


def _rs_chips_comm(parts):
    n = len(parts)

    def plan(src, out, x, y, c):
        remote = []
        for k, (fx, fy) in enumerate(CHIP_FLIPS):
            px, py = _flip(x, fx), _flip(y, fy)
            remote += [(src[a].at[2 * px + py], out[a].at[k], (px, py, c), out[a].at[k]) for a in range(n)]
        return remote, []

    return Comm(list(parts), [jax.ShapeDtypeStruct((3,) + p.shape[1:], p.dtype) for p in parts], plan, 3 * n, 0)


def _rs_finish(gps, got1, got2):
    n = len(gps)
    x, y, c = _position()
    bc_arr = jnp.stack([2 * x + y, c]).astype(jnp.int32)

    def sum_final(bc_ref, mine_ref, got1_ref, got2_ref, out_ref):
        acc = mine_ref[...] + got1_ref[...]
        for k in range(3):
            acc = acc + got2_ref[k].astype(F32)
        out_ref[...] = acc

    halves = []
    for gp, g1, g2 in zip(gps, got1, got2):
        _, _, R, C = gp.shape
        tr = _sum_rows_tile(R, C, budget=2 << 20)
        halves.append(pl.pallas_call(
            sum_final, name='rs_sum_final',
            grid_spec=pltpu.PrefetchScalarGridSpec(
                num_scalar_prefetch=1, grid=(R // tr,),
                in_specs=[pl.BlockSpec((None, None, tr, C), lambda r, bc: (bc[0], bc[1], r, 0)),
                          pl.BlockSpec((None, tr, C), lambda r, bc: (bc[0], r, 0)),
                          pl.BlockSpec((3, tr, C), lambda r, bc: (0, r, 0))],
                out_specs=pl.BlockSpec((None, tr, C), lambda r, bc: (bc[1], r, 0))),
            out_shape=jax.ShapeDtypeStruct((2, R, C), F32),
            compiler_params=_cparams("parallel"),
        )(bc_arr, gp, g1, g2))

    def plan(src, out, x, y, c):
        return [(src[a].at[c], out[a].at[c], (x, y, 1 - c), out[a].at[1 - c]) for a in range(n)], []

    comm = Comm(halves, [jax.ShapeDtypeStruct(h.shape, F32) for h in halves], plan, n, 0)
    return _exchange(comm, name='rs_swap', aliases={a: a for a in range(n)})


def _sum8(gathered, *, name):
    _, R, C = gathered.shape

    def body(g_ref, o_ref):
        acc = g_ref[0]
        for k in range(1, 8):
            acc = acc + g_ref[k]
        o_ref[...] = acc

    return pl.pallas_call(body, name=name, out_shape=jax.ShapeDtypeStruct((R, C), F32))(gathered)


def _adamw(w, g, m, v, *, name):
    shape = w.shape
    cols = shape[-1]
    rows = w.size // cols
    tr = _sum_rows_tile(rows, cols, budget=1 << 20)

    def body(w_ref, g_ref, m_ref, v_ref, d_ref, mo_ref, vo_ref):
        gv = g_ref[...]
        mn = ADAM_B1 * m_ref[...] + (1.0 - ADAM_B1) * gv
        vn = ADAM_B2 * v_ref[...] + (1.0 - ADAM_B2) * (gv * gv)
        m_hat = mn / (1.0 - ADAM_B1 ** ADAM_STEP)
        v_hat = vn / (1.0 - ADAM_B2 ** ADAM_STEP)
        d_ref[...] = -ADAM_LR * (m_hat / (jnp.sqrt(v_hat) + ADAM_EPS) + ADAM_WD * w_ref[...])
        mo_ref[...] = mn
        vo_ref[...] = vn

    blk = pl.BlockSpec((tr, cols), lambda i: (i, 0))
    outs = pl.pallas_call(
        body, name=name, grid=(rows // tr,), in_specs=[blk] * 4, out_specs=[blk] * 3,
        out_shape=[jax.ShapeDtypeStruct((rows, cols), F32)] * 3,
        compiler_params=_cparams("parallel"),
    )(*[a.reshape(rows, cols) for a in (w, g, m, v)])
    return tuple(o.reshape(shape) for o in outs)


WEIGHTS = ['norm_mix', 'norm_mlp', 'w_ada', 'b_ada', 'w_mlp_in', 'w_mlp_out', 'fox_w_in', 'fox_b_f',
           'fox_q_norm', 'fox_k_norm', 'fox_w_out', 'sg_w_in', 'sg_ln_g', 'sg_ln_b', 'sg_w_s', 'sg_b_s',
           'sg_w_out', 'cv_w_pw1', 'cv_b_pw1', 'cv_w_dw', 'cv_b_dw', 'cv_ln_g', 'cv_ln_b', 'cv_w_pw2',
           'cv_b_pw2']
BIG = {'w_mlp_in': 2, 'w_mlp_out': 1, 'fox_w_in': 2, 'fox_w_out': 1, 'sg_w_in': 2, 'sg_w_out': 1,
       'cv_w_pw1': 2, 'cv_w_pw2': 1}
SMALL_SHARDED = ['cv_b_pw1', 'cv_w_dw', 'cv_b_dw', 'cv_ln_g', 'cv_ln_b', 'cv_b_pw2']
SMALL_GRADS = ['norm_mix', 'norm_mlp', 'fox_b_f', 'fox_q_norm', 'fox_k_norm', 'sg_ln_g', 'sg_ln_b', 'sg_w_s',
               'sg_b_s'] + SMALL_SHARDED
GRAD_WIRE_DTYPE = BF16


def _pack_rows(parts, cols):
    flat = jnp.concatenate([p.reshape(-1) for p in parts])
    rows = -(-flat.size // (8 * cols)) * 8
    return jnp.pad(flat, (0, rows * cols - flat.size)).reshape(rows, cols)


def _unpack(flat, shapes):
    out, off = [], 0
    for s in shapes:
        n = math.prod(s)
        out.append(flat[..., off:off + n].reshape(flat.shape[:-1] + tuple(s)))
        off += n
    return out


def _merge_chips(a, axis):
    a = jnp.moveaxis(a, 0, axis)
    return a.reshape(a.shape[:axis] + (a.shape[axis] * a.shape[axis + 1],) + a.shape[axis + 2:])


def _split_chips(a, axis):
    a = a.reshape(a.shape[:axis] + (4, a.shape[axis] // 4) + a.shape[axis + 1:])
    return jnp.moveaxis(a, axis, 0)


def _step(a):
    x, y, c = _position()
    me = 4 * x + 2 * y + c
    chip = 2 * x + y
    T, D = a['x'].shape[1], a['x'].shape[2]
    L = a['norm_mix'].shape[0]

    small_shapes = [(D,)] + [a[n].shape for n in SMALL_SHARDED]
    small = _all_gather8([_pack_rows([a['c']] + [a[n] for n in SMALL_SHARDED], LANE)], name='ag_small')[0]
    small = small.reshape(8, -1)
    c_all = _unpack(small, small_shapes[:1])[0]
    sharded = _unpack(small[0::2, D:], small_shapes[1:])
    P = {n: _merge_chips(v, v.ndim - 2) for n, v in zip(SMALL_SHARDED, sharded)}

    c_act = _silu_rows(c_all, name='c_act')
    mod_cols = jnp.stack([
        _mm(c_act, a['w_ada'][i], name='ada_mod', tm=8, tn=_col_tile(a['w_ada'].shape[2], 768),
            extras=[(lax.dynamic_slice_in_dim(a['b_ada'][i:i + 1], chip * a['w_ada'].shape[2],
                                              a['w_ada'].shape[2], axis=1), 'row')],
            epilogue=_bias_epilogue)
        for i in range(L)])
    mod_all = _all_gather8([mod_cols.reshape(L * 8, -1)], name='ag_mod')[0].reshape(8, L, 8, -1)
    mod = lax.dynamic_index_in_dim(mod_all[0::2], me, axis=2, keepdims=False)
    mod = jnp.moveaxis(mod, 0, 1).reshape(L, 6 * D)

    units = _matrix_units(L)
    first, later = units[:1], units[1:]
    last, earlier = units[:2], units[2:]
    n_heads = D // HEAD_DIM

    def half_block(unit):
        blk = a[unit[0]][unit[1]]
        return lax.dynamic_index_in_dim(blk.astype(BF16).reshape(2, blk.shape[0] // 2, blk.shape[1]), c, axis=0,
                                        keepdims=False)

    def install(group, gathered):
        for (name, idx), gth in zip(group, gathered):
            blocks = gth.reshape((4,) + a[name].shape[1:])
            if name == 'fox_w_in':
                pad = jnp.zeros((blocks.shape[1], LANE - n_heads), BF16)
                full = jnp.concatenate([blocks[0], blocks[1], blocks[2], blocks[3], pad], axis=-1)
            else:
                full = _merge_chips(blocks, BIG[name] - 1)
            if name in ('w_mlp_in', 'w_mlp_out', 'fox_w_in', 'fox_w_out'):
                P.setdefault(name, {})[idx] = full
            else:
                P[name] = full

    install(first, _all_gather8([half_block(u) for u in first], name='ag_weights_first'))
    for n in ('sg_w_s', 'sg_b_s', 'cv_w_dw'):
        P[n] = (P[n] if n in P else a[n])[0]
    for n in ('norm_mix', 'norm_mlp', 'fox_b_f', 'fox_q_norm', 'fox_k_norm', 'sg_ln_g', 'sg_ln_b'):
        P[n] = a[n]

    def split_grad(unit, grad):
        name = unit[0]
        if name == 'fox_w_in':
            grad = grad[:, :a[name].shape[2] * N_CHIPS]
        blk = grad if grad.ndim == 3 else _split_chips(grad, BIG[name] - 1)
        return blk.reshape(N_CHIPS, 2, blk.shape[1] // 2, blk.shape[2])

    state = {}

    def fwd_done(outs):
        install(later, _gather_forward(outs, name='ag_weights_forward'))

    def bwd_comm(mat):
        state['gps'] = [split_grad(u, mat[u]) for u in earlier]
        state['got1'], parts = _rs_begin(state['gps'], wire_dtype=GRAD_WIRE_DTYPE)
        return _rs_chips_comm(parts)

    def bwd_done(outs):
        state['got2'] = outs

    hooks = Hooks(_gather_comm([half_block(u) for u in later]), fwd_done, bwd_comm, bwd_done)
    loss_part, grad_x, dmod, g, mat = _local_step(a['x'][0], a['loss_target'][0], mod, P, hooks)

    small_g = [dmod, loss_part[0:1, 0:1]] + [g[n] for n in SMALL_GRADS]
    small_g_shapes = [s.shape for s in small_g]
    all_small = _all_gather8([_pack_rows(small_g, LANE)], name='ag_small_grads')[0]
    summed = _sum8(all_small, name='sum_small_grads').reshape(-1)
    sums = _unpack(summed, small_g_shapes)
    loss = sums[1][0, 0]
    grads = dict(zip(SMALL_GRADS, sums[2:]))
    grads['b_ada'] = sums[0]
    for n in SMALL_SHARDED:
        blk = a[n].shape[-1]
        grads[n] = lax.dynamic_slice_in_dim(grads[n], chip * blk, blk, axis=grads[n].ndim - 1)
    dmod_all = all_small.reshape(8, -1)[:, :dmod.size].reshape(8, L, 6 * D)
    cols = a['w_ada'].shape[2]
    dmod_cols = lax.dynamic_slice_in_dim(dmod_all, chip * cols, cols, axis=2)
    pad8 = lambda t: jnp.pad(t, ((0, LANE - 8), (0, 0)))
    c_act_pad = pad8(c_act)
    grads['w_ada'] = jnp.stack([
        _mm(c_act_pad, pad8(dmod_cols[:, i]), ta=True, name='ada_dw', tn=_col_tile(cols, 768))
        for i in range(L)])

    shards = dict(zip(earlier, _rs_finish(state['gps'], state['got1'], state['got2'])))
    gps = [split_grad(u, mat[u]) for u in last]
    got1, parts = _rs_begin(gps, wire_dtype=GRAD_WIRE_DTYPE)
    got2 = _exchange(_rs_chips_comm(parts), name='rs_chips')
    shards.update(zip(last, _rs_finish(gps, got1, got2)))
    for n in BIG:
        grads[n] = jnp.stack([shards[n, idx].reshape(a[n].shape[1:]) for idx in range(a[n].shape[0])])

    deltas, new_m, new_v = {}, {}, {}
    for n in WEIGHTS:
        deltas[n], new_m[n], new_v[n] = _adamw(a[n], grads[n], a['m_' + n], a['v_' + n], name='adamw')
    return (loss, grad_x[None], *[grads[n] for n in WEIGHTS], *[deltas[n] for n in WEIGHTS],
            *[new_m[n] for n in WEIGHTS], *[new_v[n] for n in WEIGHTS])


def _matrix_units(n_layers):
    mixers = (('fox_w_in', 'fox_w_out'), ('sg_w_in', 'sg_w_out'), ('cv_w_pw1', 'cv_w_pw2'))
    units = []
    for i in range(n_layers):
        units += [(n, i // N_MIXERS) for n in mixers[i % N_MIXERS]] + [('w_mlp_in', i), ('w_mlp_out', i)]
    return units


def _silu_rows(x, *, name):
    def body(x_ref, o_ref):
        xv = x_ref[...]
        o_ref[...] = (xv * jax.nn.sigmoid(xv)).astype(BF16)

    return pl.pallas_call(body, name=name, out_shape=jax.ShapeDtypeStruct(x.shape, BF16))(x)


def kernel(x, c, norm_mix, norm_mlp, w_ada, b_ada, w_mlp_in, w_mlp_out, fox_w_in, fox_b_f, fox_q_norm, fox_k_norm, fox_w_out, sg_w_in, sg_ln_g, sg_ln_b, sg_w_s, sg_b_s, sg_w_out, cv_w_pw1, cv_b_pw1, cv_w_dw, cv_b_dw, cv_ln_g, cv_ln_b, cv_w_pw2, cv_b_pw2, loss_target, m_norm_mix, m_norm_mlp, m_w_ada, m_b_ada, m_w_mlp_in, m_w_mlp_out, m_fox_w_in, m_fox_b_f, m_fox_q_norm, m_fox_k_norm, m_fox_w_out, m_sg_w_in, m_sg_ln_g, m_sg_ln_b, m_sg_w_s, m_sg_b_s, m_sg_w_out, m_cv_w_pw1, m_cv_b_pw1, m_cv_w_dw, m_cv_b_dw, m_cv_ln_g, m_cv_ln_b, m_cv_w_pw2, m_cv_b_pw2, v_norm_mix, v_norm_mlp, v_w_ada, v_b_ada, v_w_mlp_in, v_w_mlp_out, v_fox_w_in, v_fox_b_f, v_fox_q_norm, v_fox_k_norm, v_fox_w_out, v_sg_w_in, v_sg_ln_g, v_sg_ln_b, v_sg_w_s, v_sg_b_s, v_sg_w_out, v_cv_w_pw1, v_cv_b_pw1, v_cv_w_dw, v_cv_b_dw, v_cv_ln_g, v_cv_ln_b, v_cv_w_pw2, v_cv_b_pw2):
    return _step(dict(locals()))
```

```python
import math
from typing import Callable, NamedTuple

import jax
import jax.numpy as jnp
from jax import lax
from jax.experimental import pallas as pl
from jax.experimental.pallas import tpu as pltpu

F32 = jnp.float32
BF16 = jnp.bfloat16

EPS = 1e-6
HEAD_DIM = 64
LANE = 128
CONV_WIDTH = 31
CONV_HALO = 32
SG_CHUNK = 128
SG_CAUSAL = 64
SG_GROUPS = 8
N_MIXERS = 3
N_CHIPS = 4
VMEM_LIMIT = 56 * 1024 * 1024
NEG = -1e30

ADAM_LR = 0.001
ADAM_B1 = 0.9
ADAM_B2 = 0.999
ADAM_EPS = 1e-08
ADAM_WD = 0.01
ADAM_STEP = 10

MESH = pl.DeviceIdType.MESH
ANY = pl.BlockSpec(memory_space=pl.ANY)


def _cparams(*sem):
    return pltpu.CompilerParams(dimension_semantics=sem, vmem_limit_bytes=VMEM_LIMIT)


def _row_tile(t, want=512):
    return min(t, want)


def _matmul(a, b, *, name, ta=False, tb=False, tm=512, tn=1024, tk=1024,
            extras=(), epilogue=None, out_dtypes=(F32,), b_outer=False, out_chips=None, row_sums=False):
    M, K = (a.shape[1], a.shape[0]) if ta else a.shape
    N = b.shape[0] if tb else b.shape[1]
    assert (b.shape[1] if tb else b.shape[0]) == K
    n_own = N // out_chips if out_chips else N
    tm, tn, tk = min(tm, M), min(tn, n_own), min(tk, K)
    assert M % tm == 0 and n_own % tn == 0 and K % tk == 0, (name, M, N, K, tm, tn, tk)
    nk = K // tk

    def spec(shape, pick):
        if b_outer:
            return pl.BlockSpec(shape, lambda j, i, k: pick(i, j, k))
        return pl.BlockSpec(shape, pick)

    a_spec = spec((tk, tm), lambda i, j, k: (k, i)) if ta else spec((tm, tk), lambda i, j, k: (i, k))
    b_spec = spec((tn, tk), lambda i, j, k: (j, k)) if tb else spec((tk, tn), lambda i, j, k: (k, j))
    ex_specs = [spec((tm, tn), lambda i, j, k: (i, j)) if kind == 'tile' else spec((1, tn), lambda i, j, k: (0, j))
                for _, kind in extras]
    dims = (((0,) if ta else (1,), (1,) if tb else (0,)), ((), ()))
    n_ex, n_out = len(extras), len(out_dtypes) + bool(row_sums)
    assert not row_sums or N == tn

    def body(*refs):
        a_ref, b_ref = refs[0], refs[1]
        ex = refs[2:2 + n_ex]
        outs = refs[2 + n_ex:2 + n_ex + n_out]

        def finish(acc):
            vals = epilogue(acc, *[r[...] for r in ex]) if epilogue else (acc,)
            for o, v in zip(outs[:len(out_dtypes)], vals):
                o[...] = v.astype(o.dtype)
            if row_sums:
                row_tile = pl.program_id(1 if b_outer else 0)

                @pl.when(row_tile == 0)
                def _():
                    outs[-1][...] = vals[-1]

                @pl.when(row_tile > 0)
                def _():
                    outs[-1][...] += vals[-1]

        part = lax.dot_general(a_ref[...].astype(BF16), b_ref[...].astype(BF16), dims,
                               preferred_element_type=F32)
        if nk == 1:
            finish(part)
        else:
            acc_ref = refs[-1]
            k = pl.program_id(2)

            @pl.when(k == 0)
            def _():
                acc_ref[...] = part

            @pl.when(k > 0)
            def _():
                acc_ref[...] += part

            @pl.when(k == nk - 1)
            def _():
                finish(acc_ref[...])

    outs = pl.pallas_call(
        body, name=name,
        grid=(N // tn, M // tm, nk) if b_outer else (M // tm, N // tn, nk),
        in_specs=[a_spec, b_spec] + ex_specs,
        out_specs=[spec((None, tm, tn), lambda i, j, k: (j // (n_own // tn), i, j % (n_own // tn)))
                   if out_chips else spec((tm, tn), lambda i, j, k: (i, j)) for _ in out_dtypes]
        + ([spec((8, tn), lambda i, j, k: (0, j))] if row_sums else []),
        out_shape=[jax.ShapeDtypeStruct((out_chips, M, n_own) if out_chips else (M, N), dt) for dt in out_dtypes]
        + ([jax.ShapeDtypeStruct((8, N), F32)] if row_sums else []),
        scratch_shapes=[pltpu.VMEM((tm, tn), F32)] if nk > 1 else [],
        compiler_params=(_cparams("arbitrary", "arbitrary", "arbitrary") if row_sums
                         else _cparams("parallel", "parallel", "arbitrary")),
    )(a, b, *[arr for arr, _ in extras])
    return outs if n_out > 1 else outs[0]


def _norm_mod_fwd(x, w, sc, sh, *, name):
    T, D = x.shape
    tr = _row_tile(T)

    def body(x_ref, w_ref, sc_ref, sh_ref, h_ref):
        xv = x_ref[...]
        r = lax.rsqrt(jnp.mean(xv * xv, axis=-1, keepdims=True) + EPS)
        h_ref[...] = ((xv * r) * w_ref[...] * (1.0 + sc_ref[...]) + sh_ref[...]).astype(BF16)

    row = pl.BlockSpec((1, D), lambda i: (0, 0))
    return pl.pallas_call(
        body, name=name, grid=(T // tr,),
        in_specs=[pl.BlockSpec((tr, D), lambda i: (i, 0)), row, row, row],
        out_specs=pl.BlockSpec((tr, D), lambda i: (i, 0)),
        out_shape=jax.ShapeDtypeStruct((T, D), BF16),
        compiler_params=_cparams("parallel"),
    )(x, w, sc, sh)


def _gate_bwd(dx, y, g, *, name):
    T, D = dx.shape
    tr = _row_tile(T)

    def body(dx_ref, y_ref, g_ref, dy_ref, dg_ref):
        i = pl.program_id(0)
        dxv = dx_ref[...]
        dy_ref[...] = (dxv * g_ref[...]).astype(BF16)
        part = jnp.concatenate([jnp.sum(dxv * y_ref[...], axis=0, keepdims=True),
                                jnp.zeros((7, D), F32)], axis=0)

        @pl.when(i == 0)
        def _():
            dg_ref[...] = part

        @pl.when(i > 0)
        def _():
            dg_ref[...] += part

    blk = pl.BlockSpec((tr, D), lambda i: (i, 0))
    return pl.pallas_call(
        body, name=name, grid=(T // tr,),
        in_specs=[blk, blk, pl.BlockSpec((1, D), lambda i: (0, 0))],
        out_specs=[blk, pl.BlockSpec((8, D), lambda i: (0, 0))],
        out_shape=[jax.ShapeDtypeStruct((T, D), BF16), jax.ShapeDtypeStruct((8, D), F32)],
        compiler_params=_cparams("arbitrary"),
    )(dx, y, g)


def _loss_head(y, target, *, name):
    T, D = y.shape
    tr = _row_tile(T)

    def body(y_ref, t_ref, loss_ref, dy_ref):
        i = pl.program_id(0)
        e = y_ref[...] - t_ref[...]
        dy_ref[...] = e * (1.0 / D)
        part = jnp.full((8, LANE), 0.5 / D * jnp.sum(e * e), F32)

        @pl.when(i == 0)
        def _():
            loss_ref[...] = part

        @pl.when(i > 0)
        def _():
            loss_ref[...] += part

    blk = pl.BlockSpec((tr, D), lambda i: (i, 0))
    return pl.pallas_call(
        body, name=name, grid=(T // tr,), in_specs=[blk, blk],
        out_specs=[pl.BlockSpec((8, LANE), lambda i: (0, 0)), blk],
        out_shape=[jax.ShapeDtypeStruct((8, LANE), F32), jax.ShapeDtypeStruct((T, D), F32)],
        compiler_params=_cparams("arbitrary"),
    )(y, target)


def _position():
    return lax.axis_index("x"), lax.axis_index("y"), lax.axis_index("c")


class Comm(NamedTuple):
    srcs: list
    out_shapes: list
    plan: Callable
    n_remote: int
    n_local: int

    def scratch(self):
        return [pltpu.SemaphoreType.DMA((self.n_remote,)), pltpu.SemaphoreType.DMA((self.n_remote,)),
                pltpu.SemaphoreType.DMA((max(self.n_local, 1),))]


def _comm_copies(plan, src_refs, out_refs, send_sems, recv_sems, local_sems):
    x, y, c = _position()
    remote, local = plan(src_refs, out_refs, x, y, c)

    def copy(k, s, d, peer):
        return pltpu.make_async_remote_copy(src_ref=s, dst_ref=d, send_sem=send_sems.at[k],
                                            recv_sem=recv_sems.at[k], device_id=peer, device_id_type=MESH)

    sends = [copy(k, s, d, peer) for k, (s, d, peer, _) in enumerate(remote)]
    recvs = [copy(k, s, landing, peer) for k, (s, _, peer, landing) in enumerate(remote)]
    local_copies = [pltpu.make_async_copy(s, d, local_sems.at[i]) for i, (s, d) in enumerate(local)]
    return sends, recvs, local_copies


def _comm_start(copies):
    sends, _, local_copies = copies
    for cp in local_copies + sends:
        cp.start()


def _comm_wait(copies):
    sends, recvs, local_copies = copies
    for cp in recvs:
        cp.wait_recv()
    for cp in sends:
        cp.wait_send()
    for cp in local_copies:
        cp.wait()


def _split_comm_refs(refs, n_in, n_out, n_scratch, comm):
    ns, nd = (len(comm.srcs), len(comm.out_shapes)) if comm else (0, 0)
    cuts = [n_in, ns, n_out, nd, n_scratch]
    parts, at = [], 0
    for n in cuts:
        parts.append(refs[at:at + n])
        at += n
    return (*parts, refs[at:])


AUG_F = HEAD_DIM
AUG_LSE = HEAD_DIM + 6


def _half_cols(x, lo):
    return (jnp.sum(jnp.where(lo, x, 0.0), axis=-1, keepdims=True),
            jnp.sum(jnp.where(lo, 0.0, x), axis=-1, keepdims=True))


def _half_sums(x, lo):
    s_lo, s_hi = _half_cols(x, lo)
    return jnp.where(lo, s_lo, s_hi)


def _split3(x):
    a = x.astype(BF16).astype(F32)
    r = x - a
    b = r.astype(BF16).astype(F32)
    return a, b, (r - b).astype(BF16).astype(F32)


def _aug(lane, base, terms):
    out = jnp.zeros(lane.shape, F32)
    for i, t in enumerate(terms):
        out = jnp.where(lane == base + i, t, out)
    return out


def _head_lanes(x2, h):
    return x2 if h == 0 else pltpu.roll(x2, HEAD_DIM, 1)


def _fox_prep_fwd(proj, qg, kg, fcol, *, d_model, name):
    T = proj.shape[0]
    nhp = d_model // LANE
    tr = _row_tile(T)

    def body(q_ref, k_ref, v_ref, qg_ref, kg_ref, f_ref, qa_ref, qta_ref, ka_ref, kta_ref, va_ref, vta_ref):
        lane = lax.broadcasted_iota(jnp.int32, (tr, LANE), 1)
        lo = lane < HEAD_DIM

        def norm(xv, g):
            ms = _half_sums(xv * xv, lo) * (1.0 / HEAD_DIM)
            return (xv * lax.rsqrt(ms + EPS)) * g

        qn = norm(q_ref[...], qg_ref[...]) * (HEAD_DIM ** -0.5)
        kn = norm(k_ref[...], kg_ref[...])
        vv = v_ref[...]
        qa, ka, va, vta = [], [], [], []
        for h in range(2):
            f1, f2, f3 = _split3(f_ref[h])
            qa.append(jnp.where(lo, _head_lanes(qn, h), _aug(lane, AUG_F, [f1, f2, f3, 1.0, 1.0, 1.0])))
            ka.append(jnp.where(lo, _head_lanes(kn, h),
                                _aug(lane, AUG_F, [1.0, 1.0, 1.0, -f1, -f2, -f3, 1.0, 1.0, 1.0])))
            va.append(jnp.where(lo if h == 0 else jnp.logical_not(lo), vv, 0.0))
            vta.append(jnp.where(lo, _head_lanes(vv, h), _aug(lane, AUG_F, [1.0, 1.0, 1.0])))
        for parts, ref, tref in ((qa, qa_ref, qta_ref), (ka, ka_ref, kta_ref), (va, va_ref, None),
                                 (vta, None, vta_ref)):
            both = jnp.concatenate(parts, axis=1)
            if ref is not None:
                ref[...] = both.astype(BF16)
            if tref is not None:
                tref[...] = both.astype(BF16).T

    gain = pl.BlockSpec((1, LANE), lambda i, h: (0, 0))
    rows = pl.BlockSpec((tr, 2 * LANE), lambda i, h: (i, h))
    cols = pl.BlockSpec((2 * LANE, tr), lambda i, h: (h, i))
    wide, tall = jax.ShapeDtypeStruct((T, 2 * d_model), BF16), jax.ShapeDtypeStruct((2 * d_model, T), BF16)
    return pl.pallas_call(
        body, name=name, grid=(T // tr, nhp),
        in_specs=[pl.BlockSpec((tr, LANE), lambda i, h: (i, h)),
                  pl.BlockSpec((tr, LANE), lambda i, h: (i, nhp + h)),
                  pl.BlockSpec((tr, LANE), lambda i, h: (i, 2 * nhp + h)), gain, gain,
                  pl.BlockSpec((2, tr, 1), lambda i, h: (h, i, 0))],
        out_specs=[rows, cols, rows, cols, rows, cols],
        out_shape=[wide, tall, wide, tall, wide, tall],
        compiler_params=_cparams("parallel", "parallel"),
    )(proj, proj, proj, qg, kg, fcol)


def _fox_do_prep(do, o, *, name):
    T, D = do.shape
    nhp = D // LANE
    tr = _row_tile(T)

    def body(do_ref, o_ref, doa_ref, dota_ref):
        lane = lax.broadcasted_iota(jnp.int32, (tr, LANE), 1)
        lo = lane < HEAD_DIM
        dob = do_ref[...].astype(BF16).astype(F32)
        deltas = _half_cols(dob * o_ref[...], lo)
        both = jnp.concatenate(
            [jnp.where(lo, _head_lanes(dob, h), _aug(lane, AUG_F, _split3(-deltas[h]))) for h in range(2)], axis=1)
        doa_ref[...] = both.astype(BF16)
        dota_ref[...] = both.astype(BF16).T

    blk = pl.BlockSpec((tr, LANE), lambda i, h: (i, h))
    return pl.pallas_call(
        body, name=name, grid=(T // tr, nhp), in_specs=[blk, blk],
        out_specs=[pl.BlockSpec((tr, 2 * LANE), lambda i, h: (i, h)),
                   pl.BlockSpec((2 * LANE, tr), lambda i, h: (h, i))],
        out_shape=[jax.ShapeDtypeStruct((T, 2 * D), BF16), jax.ShapeDtypeStruct((2 * D, T), BF16)],
        compiler_params=_cparams("parallel", "parallel"),
    )(do, o)


def _fox_prep_bwd(proj, dq, dkt, dvt, qg, kg, *, d_model, name):
    T = proj.shape[0]
    nhp = d_model // LANE
    tr = _row_tile(T)

    def body(q_ref, k_ref, dq_ref, dkt_ref, dvt_ref, qg_ref, kg_ref, dqo_ref, dko_ref, dvo_ref, sums_ref):
        first = (pl.program_id(0) == 0) & (pl.program_id(1) == 0)
        lo = lax.broadcasted_iota(jnp.int32, (tr, LANE), 1) < HEAD_DIM

        def pair(x2):
            return jnp.where(lo, x2[:, :LANE], pltpu.roll(x2[:, LANE:], HEAD_DIM, 1))

        def bwd(xv, dxhat, g):
            ms = _half_sums(xv * xv, lo) * (1.0 / HEAD_DIM)
            r = lax.rsqrt(ms + EPS)
            n = xv * r
            dn = dxhat * g
            dx = r * (dn - n * (_half_sums(dn * n, lo) * (1.0 / HEAD_DIM)))
            dg = jnp.sum(dxhat * n, axis=0, keepdims=True)
            return dx, dg + pltpu.roll(dg, HEAD_DIM, 1)

        dxq, dgq = bwd(q_ref[...], pair(dq_ref[...]) * (HEAD_DIM ** -0.5), qg_ref[...])
        dxk, dgk = bwd(k_ref[...], pair(dkt_ref[...].T), kg_ref[...])
        dqo_ref[...] = dxq.astype(BF16)
        dko_ref[...] = dxk.astype(BF16)
        dvo_ref[...] = pair(dvt_ref[...].T.astype(F32)).astype(BF16)
        part = jnp.concatenate([dgq, dgk, jnp.zeros((6, LANE), F32)], axis=0)

        @pl.when(first)
        def _():
            sums_ref[...] = part

        @pl.when(jnp.logical_not(first))
        def _():
            sums_ref[...] += part

    gain = pl.BlockSpec((1, LANE), lambda i, h: (0, 0))
    blk = pl.BlockSpec((tr, LANE), lambda i, h: (i, h))
    tall = pl.BlockSpec((2 * LANE, tr), lambda i, h: (h, i))
    return pl.pallas_call(
        body, name=name, grid=(T // tr, nhp),
        in_specs=[blk, pl.BlockSpec((tr, LANE), lambda i, h: (i, nhp + h)),
                  pl.BlockSpec((tr, 2 * LANE), lambda i, h: (i, h)), tall, tall, gain, gain],
        out_specs=[blk, blk, blk, pl.BlockSpec((8, LANE), lambda i, h: (0, 0))],
        out_shape=[jax.ShapeDtypeStruct((T, d_model), BF16)] * 3 + [jax.ShapeDtypeStruct((8, LANE), F32)],
        compiler_params=_cparams("arbitrary", "arbitrary"),
    )(proj, proj, dq, dkt, dvt, qg, kg)


def _scan_lanes(x, reverse):
    n = x.shape[-1]
    lane = lax.broadcasted_iota(jnp.int32, x.shape, 1)
    sh = 1
    while sh < n:
        if reverse:
            x = x + jnp.where(lane < n - sh, pltpu.roll(x, n - sh, 1), 0.0)
        else:
            x = x + jnp.where(lane >= sh, pltpu.roll(x, sh, 1), 0.0)
        sh *= 2
    return x


def _fox_gate_fwd(fpre_t, bf, *, name):
    def body(f_ref, b_ref, o_ref):
        xv = f_ref[...] + b_ref[...]
        logf = jnp.minimum(xv, 0.0) - jnp.log1p(jnp.exp(-jnp.abs(xv)))
        o_ref[...] = _scan_lanes(logf, reverse=False)

    return pl.pallas_call(body, name=name, out_shape=jax.ShapeDtypeStruct(fpre_t.shape, F32))(fpre_t, bf)


def _fox_gate_bwd(dcol, drow, fpre_t, bf, *, name):
    H = fpre_t.shape[0]

    def body(dc_ref, dr_ref, f_ref, b_ref, o_ref, db_ref):
        xv = f_ref[...] + b_ref[...]
        e = dc_ref[...] - dr_ref[...]
        dlogf = _scan_lanes(e, reverse=False) - e
        dpre = dlogf * (1.0 - jax.nn.sigmoid(xv))
        o_ref[...] = dpre
        db_ref[...] = jnp.broadcast_to(jnp.sum(dpre, axis=-1, keepdims=True), (H, LANE))

    return pl.pallas_call(
        body, name=name,
        out_shape=[jax.ShapeDtypeStruct(fpre_t.shape, F32), jax.ShapeDtypeStruct((H, LANE), F32)],
    )(dcol, drow, fpre_t, bf)


_NT = (((1,), (1,)), ((), ()))
_TN = (((0,), (0,)), ((), ()))
_NN = (((1,), (0,)), ((), ()))


def _attn_tile(T):
    return min(T, 1024)


def _causal(tq, tk):
    return lax.broadcasted_iota(jnp.int32, (tq, tk), 1) <= lax.broadcasted_iota(jnp.int32, (tq, tk), 0)


def _fox_attn_fwd(qa, kta, va, *, name, comm=None):
    T = qa.shape[0]
    nhp = qa.shape[1] // (2 * LANE)
    tq = tk = _attn_tile(T)
    nq = T // tq

    def body(*refs):
        (qa_ref, kta_ref, va_ref), src_refs, (o_ref, qb_ref), dst_refs, (m_sc, l_sc, acc_sc), sems = (
            _split_comm_refs(refs, 3, 2, 3, comm))
        hp, i, j = pl.program_id(0), pl.program_id(1), pl.program_id(2)
        if comm:
            @pl.when((hp == 0) & (i == 0) & (j == 0))
            def _():
                _comm_start(_comm_copies(comm.plan, src_refs, dst_refs, *sems))

        @pl.when(j == 0)
        def _():
            m_sc[...] = jnp.full(m_sc.shape, NEG, F32)
            l_sc[...] = jnp.zeros(l_sc.shape, F32)
            acc_sc[...] = jnp.zeros(acc_sc.shape, F32)

        def block(diagonal):
            heads = [slice(h * LANE, (h + 1) * LANE) for h in range(2)]
            scores = [lax.dot_general(qa_ref[:, hs], kta_ref[hs, :], _NN, preferred_element_type=F32)
                      for hs in heads]
            state = [(m_sc[h], l_sc[h], acc_sc[h]) for h in range(2)]
            probs, updates = [], []
            for s, (m_prev, l_prev, _) in zip(scores, state):
                if diagonal:
                    s = jnp.where(_causal(tq, tk), s, NEG)
                m_next = jnp.maximum(m_prev, jnp.max(s, axis=1, keepdims=True))
                p = jnp.exp(s - jnp.tile(m_next, (1, tk // LANE)))
                alpha = jnp.exp(m_prev - m_next)
                probs.append(p.astype(BF16))
                updates.append((m_next, alpha, alpha * l_prev + jnp.sum(p, axis=1, keepdims=True)))
            pvs = [lax.dot_general(p, va_ref[:, hs], _NN, preferred_element_type=F32)
                   for p, hs in zip(probs, heads)]
            for h in range(2):
                m_next, alpha, l_next = updates[h]
                m_sc[h] = m_next
                l_sc[h] = l_next
                acc_sc[h] = alpha * state[h][2] + pvs[h]

        @pl.when(j < i)
        def _():
            block(False)

        @pl.when(j == i)
        def _():
            block(True)
            o_ref[...] = acc_sc[0] / l_sc[0] + acc_sc[1] / l_sc[1]
            lane = lax.broadcasted_iota(jnp.int32, (tq, LANE), 1)
            for h in range(2):
                hs = slice(h * LANE, (h + 1) * LANE)
                pieces = _split3(-(m_sc[h] + jnp.log(l_sc[h])))
                qb = qa_ref[:, hs].astype(F32)
                for n, piece in enumerate(pieces):
                    qb = jnp.where(lane == AUG_LSE + n, piece, qb)
                qb_ref[:, hs] = qb.astype(BF16)

        if comm:
            @pl.when((hp == nhp - 1) & (i == nq - 1) & (j == nq - 1))
            def _():
                _comm_wait(_comm_copies(comm.plan, src_refs, dst_refs, *sems))

    outs = pl.pallas_call(
        body, name=name, grid=(nhp, nq, nq),
        in_specs=[pl.BlockSpec((tq, 2 * LANE), lambda h, i, j: (i, h)),
                  pl.BlockSpec((2 * LANE, tk), lambda h, i, j: (h, jnp.minimum(j, i))),
                  pl.BlockSpec((tk, 2 * LANE), lambda h, i, j: (jnp.minimum(j, i), h))]
        + ([ANY] * len(comm.srcs) if comm else []),
        out_specs=[pl.BlockSpec((tq, LANE), lambda h, i, j: (i, h)),
                   pl.BlockSpec((tq, 2 * LANE), lambda h, i, j: (i, h))]
        + ([ANY] * len(comm.out_shapes) if comm else []),
        out_shape=[jax.ShapeDtypeStruct((T, nhp * LANE), F32), jax.ShapeDtypeStruct(qa.shape, BF16)]
        + (list(comm.out_shapes) if comm else []),
        scratch_shapes=[pltpu.VMEM((2, tq, LANE), F32), pltpu.VMEM((2, tq, LANE), F32),
                        pltpu.VMEM((2, tq, LANE), F32)] + (comm.scratch() if comm else []),
        compiler_params=(_cparams("arbitrary", "arbitrary", "arbitrary") if comm
                         else _cparams("parallel", "parallel", "arbitrary")),
    )(qa, kta, va, *(comm.srcs if comm else []))
    return outs[0], outs[1], outs[2:]


def _fox_attn_bwd(qb, qta, ka, kta, vta, doa, dota, *, name, comm=None):
    T = qb.shape[0]
    nhp = qb.shape[1] // (2 * LANE)
    tq = tk = _attn_tile(T)
    nq = T // tq

    def body(*refs):
        ((qb_ref, qta_ref, ka_ref, kta_ref, vta_ref, doa_ref, dota_ref), src_refs,
         (dq_ref, dkt_ref, dvt_ref, dcol_ref, drow_ref), dst_refs, (dkt_sc, dvt_sc, dcol_sc), sems) = (
            _split_comm_refs(refs, 7, 5, 3, comm))
        hp, j, i = pl.program_id(0), pl.program_id(1), pl.program_id(2)
        if comm:
            @pl.when((hp == 0) & (j == 0) & (i == 0))
            def _():
                _comm_start(_comm_copies(comm.plan, src_refs, dst_refs, *sems))

        @pl.when((j == 0) & (i == 0))
        def _():
            dq_ref[...] = jnp.zeros(dq_ref.shape, F32)
            drow_ref[...] = jnp.zeros(drow_ref.shape, F32)

        @pl.when(i == 0)
        def _():
            dkt_sc[...] = jnp.zeros(dkt_sc.shape, F32)
            dvt_sc[...] = jnp.zeros(dvt_sc.shape, F32)
            dcol_sc[...] = jnp.zeros(dcol_sc.shape, F32)

        def block(diagonal):
            rows = pl.ds(pl.multiple_of(i * tq, tq), tq)
            heads = [slice(h * LANE, (h + 1) * LANE) for h in range(2)]
            logits = [lax.dot_general(qb_ref[:, hs], kta_ref[hs, :], _NN, preferred_element_type=F32)
                      for hs in heads]
            dpds = [lax.dot_general(doa_ref[:, hs], vta_ref[hs, :], _NN, preferred_element_type=F32)
                    for hs in heads]
            pbs, dlbs = [], []
            for h in range(2):
                p = jnp.exp(logits[h])
                if diagonal:
                    p = jnp.where(_causal(tq, tk), p, 0.0)
                dl = p * dpds[h]
                pbs.append(p.astype(BF16))
                dlbs.append(dl.astype(BF16))
                dcol_sc[h] += jnp.sum(dl, axis=0, keepdims=True)
                drow_ref[h, rows, :] += jnp.sum(dl, axis=1, keepdims=True)
            for h, hs in enumerate(heads):
                dvt_sc[h] += lax.dot_general(dota_ref[hs, :], pbs[h], _NN, preferred_element_type=F32)
                dkt_sc[h] += lax.dot_general(qta_ref[hs, :], dlbs[h], _NN, preferred_element_type=F32)
                dq_ref[rows, hs] += lax.dot_general(dlbs[h], ka_ref[:, hs], _NN, preferred_element_type=F32)

        @pl.when(i > j)
        def _():
            block(False)

        @pl.when(i == j)
        def _():
            block(True)

        @pl.when(i == nq - 1)
        def _():
            dkt_ref[...] = jnp.concatenate([dkt_sc[0], dkt_sc[1]], axis=0)
            dvt_ref[...] = jnp.concatenate([dvt_sc[0], dvt_sc[1]], axis=0).astype(BF16)
            dcol_ref[...] = dcol_sc[...]

        if comm:
            @pl.when((hp == nhp - 1) & (j == nq - 1) & (i == nq - 1))
            def _():
                _comm_wait(_comm_copies(comm.plan, src_refs, dst_refs, *sems))

    qrow = pl.BlockSpec((tq, 2 * LANE), lambda h, j, i: (jnp.maximum(i, j), h))
    qcol = pl.BlockSpec((2 * LANE, tq), lambda h, j, i: (h, jnp.maximum(i, j)))
    krow = pl.BlockSpec((tk, 2 * LANE), lambda h, j, i: (j, h))
    kcol = pl.BlockSpec((2 * LANE, tk), lambda h, j, i: (h, j))
    tall = jax.ShapeDtypeStruct((qb.shape[1], T), F32)
    outs = pl.pallas_call(
        body, name=name, grid=(nhp, nq, nq),
        in_specs=[qrow, qcol, krow, kcol, kcol, qrow, qcol] + ([ANY] * len(comm.srcs) if comm else []),
        out_specs=[pl.BlockSpec((T, 2 * LANE), lambda h, j, i: (0, h)), kcol, kcol,
                   pl.BlockSpec((2, 1, tk), lambda h, j, i: (h, 0, j)),
                   pl.BlockSpec((2, T, 1), lambda h, j, i: (h, 0, 0))]
        + ([ANY] * len(comm.out_shapes) if comm else []),
        out_shape=[jax.ShapeDtypeStruct(qb.shape, F32), tall, jax.ShapeDtypeStruct(tall.shape, BF16),
                   jax.ShapeDtypeStruct((2 * nhp, 1, T), F32), jax.ShapeDtypeStruct((2 * nhp, T, 1), F32)]
        + (list(comm.out_shapes) if comm else []),
        scratch_shapes=[pltpu.VMEM((2, LANE, tk), F32), pltpu.VMEM((2, LANE, tk), F32),
                        pltpu.VMEM((2, 1, tk), F32)] + (comm.scratch() if comm else []),
        compiler_params=_cparams("arbitrary" if comm else "parallel", "arbitrary", "arbitrary"),
    )(qb, qta, ka, kta, vta, doa, dota, *(comm.srcs if comm else []))
    return (*outs[:5], outs[5:])


_GELU_C = math.sqrt(2.0 / math.pi)
_GELU_A = 0.044715


def _gelu(x):
    t = jnp.tanh(_GELU_C * (x + _GELU_A * (x * x * x)))
    return x * (0.5 * (1.0 + t)), t


def _gelu_grad(x, t):
    return 0.5 * (1.0 + t) + 0.5 * x * (1.0 - t * t) * (_GELU_C * (1.0 + 3.0 * _GELU_A * x * x))


def _layer_norm_stats(v):
    mu = jnp.mean(v, axis=-1, keepdims=True)
    vc = v - mu
    rstd = lax.rsqrt(jnp.mean(vc * vc, axis=-1, keepdims=True) + EPS)
    return vc * rstd, rstd


def _layer_norm_bwd(dyhat, yhat, rstd):
    return rstd * (dyhat - jnp.mean(dyhat, axis=-1, keepdims=True)
                   - yhat * jnp.mean(dyhat * yhat, axis=-1, keepdims=True))


def _sg_mask():
    t = lax.broadcasted_iota(jnp.int32, (SG_CHUNK, SG_CHUNK), 0) // SG_CAUSAL
    s = lax.broadcasted_iota(jnp.int32, (SG_CHUNK, SG_CHUNK), 1) // SG_CAUSAL
    return s <= t


def _sg_mix(ws_ref, bc_ref, vln_sc, vo_sc, tr, gd):
    mask = _sg_mask()
    for g in range(SG_GROUPS):
        wg = jnp.where(mask, ws_ref[g], 0.0).astype(BF16)
        cols = slice(g * gd, (g + 1) * gd)
        for n in range(tr // SG_CHUNK):
            rows = slice(n * SG_CHUNK, (n + 1) * SG_CHUNK)
            vo_sc[rows, cols] = lax.dot_general(wg, vln_sc[rows, cols], _NN,
                                                preferred_element_type=F32) + bc_ref[g]


def _sg_fwd(a_uv, ln_g, ln_b, ws, bcol, *, name):
    T, W = a_uv.shape[0], a_uv.shape[1] // 2
    gd = W // SG_GROUPS
    tr = _row_tile(T)

    def body(u_ref, v_ref, g_ref, b_ref, ws_ref, bc_ref, o_ref, vln_sc, vo_sc):
        u, _ = _gelu(u_ref[...])
        v, _ = _gelu(v_ref[...])
        vhat, _ = _layer_norm_stats(v)
        vln_sc[...] = (vhat * g_ref[...] + b_ref[...]).astype(BF16)
        _sg_mix(ws_ref, bc_ref, vln_sc, vo_sc, tr, gd)
        o_ref[...] = (u * vo_sc[...]).astype(BF16)

    row = pl.BlockSpec((1, W), lambda i: (0, 0))
    return pl.pallas_call(
        body, name=name, grid=(T // tr,),
        in_specs=[pl.BlockSpec((tr, W), lambda i: (i, 0)), pl.BlockSpec((tr, W), lambda i: (i, 1)), row, row,
                  pl.BlockSpec((SG_GROUPS, SG_CHUNK, SG_CHUNK), lambda i: (0, 0, 0)),
                  pl.BlockSpec((SG_GROUPS, SG_CHUNK, 1), lambda i: (0, 0, 0))],
        out_specs=pl.BlockSpec((tr, W), lambda i: (i, 0)),
        out_shape=jax.ShapeDtypeStruct((T, W), BF16),
        scratch_shapes=[pltpu.VMEM((tr, W), BF16), pltpu.VMEM((tr, W), F32)],
        compiler_params=_cparams("parallel"),
    )(a_uv, a_uv, ln_g, ln_b, ws, bcol)


def _sg_bwd(a_uv, dgate, ln_g, ln_b, ws, bcol, *, name):
    T, W = a_uv.shape[0], a_uv.shape[1] // 2
    gd = W // SG_GROUPS
    tr = _row_tile(T)

    def body(u_ref, v_ref, dg_ref, g_ref, b_ref, ws_ref, bc_ref,
             da_ref, dws_ref, dbs_ref, sums_ref, vln_sc, vo_sc, dvo_sc, dvln_sc):
        i = pl.program_id(0)

        @pl.when(i == 0)
        def _():
            dws_ref[...] = jnp.zeros(dws_ref.shape, F32)
            dbs_ref[...] = jnp.zeros(dbs_ref.shape, F32)
            sums_ref[...] = jnp.zeros(sums_ref.shape, F32)

        ua, va = u_ref[...], v_ref[...]
        u, tu = _gelu(ua)
        v, tv = _gelu(va)
        vhat, rstd = _layer_norm_stats(v)
        vln_sc[...] = (vhat * g_ref[...] + b_ref[...]).astype(BF16)
        _sg_mix(ws_ref, bc_ref, vln_sc, vo_sc, tr, gd)
        dgt = dg_ref[...]
        du = dgt * vo_sc[...]
        dvo_sc[...] = dgt * u
        mask = _sg_mask()
        for g in range(SG_GROUPS):
            wg = jnp.where(mask, ws_ref[g], 0.0).astype(BF16)
            cols = slice(g * gd, (g + 1) * gd)
            acc_w = jnp.zeros((SG_CHUNK, SG_CHUNK), F32)
            acc_b = jnp.zeros((SG_CHUNK, 1), F32)
            for n in range(tr // SG_CHUNK):
                rows = slice(n * SG_CHUNK, (n + 1) * SG_CHUNK)
                dvo = dvo_sc[rows, cols]
                dvob = dvo.astype(BF16)
                dvln_sc[rows, cols] = lax.dot_general(wg, dvob, _TN, preferred_element_type=F32)
                acc_w += lax.dot_general(dvob, vln_sc[rows, cols], _NT, preferred_element_type=F32)
                acc_b += jnp.sum(dvo, axis=1, keepdims=True)
            dws_ref[g] += jnp.where(mask, acc_w, 0.0)
            dbs_ref[g] += acc_b
        dvln = dvln_sc[...]
        sums_ref[...] += jnp.concatenate([jnp.sum(dvln * vhat, axis=0, keepdims=True),
                                          jnp.sum(dvln, axis=0, keepdims=True),
                                          jnp.zeros((6, W), F32)], axis=0)
        dv = _layer_norm_bwd(dvln * g_ref[...], vhat, rstd)
        da_ref[:, :W] = (du * _gelu_grad(ua, tu)).astype(BF16)
        da_ref[:, W:] = (dv * _gelu_grad(va, tv)).astype(BF16)

    row = pl.BlockSpec((1, W), lambda i: (0, 0))
    wspec = pl.BlockSpec((SG_GROUPS, SG_CHUNK, SG_CHUNK), lambda i: (0, 0, 0))
    bspec = pl.BlockSpec((SG_GROUPS, SG_CHUNK, 1), lambda i: (0, 0, 0))
    return pl.pallas_call(
        body, name=name, grid=(T // tr,),
        in_specs=[pl.BlockSpec((tr, W), lambda i: (i, 0)), pl.BlockSpec((tr, W), lambda i: (i, 1)),
                  pl.BlockSpec((tr, W), lambda i: (i, 0)), row, row, wspec, bspec],
        out_specs=[pl.BlockSpec((tr, 2 * W), lambda i: (i, 0)), wspec, bspec,
                   pl.BlockSpec((8, W), lambda i: (0, 0))],
        out_shape=[jax.ShapeDtypeStruct((T, 2 * W), BF16),
                   jax.ShapeDtypeStruct((SG_GROUPS, SG_CHUNK, SG_CHUNK), F32),
                   jax.ShapeDtypeStruct((SG_GROUPS, SG_CHUNK, 1), F32),
                   jax.ShapeDtypeStruct((8, W), F32)],
        scratch_shapes=[pltpu.VMEM((tr, W), BF16), pltpu.VMEM((tr, W), F32),
                        pltpu.VMEM((tr, W), F32), pltpu.VMEM((tr, W), F32)],
        compiler_params=_cparams("arbitrary"),
    )(a_uv, a_uv, dgate, ln_g, ln_b, ws, bcol)


SUBLANES = 8


def _shift_rows(xc_sc, xs_sc):
    rows = xs_sc.shape[1]
    for p in range(1, SUBLANES):
        xs_sc[p - 1] = xc_sc[pl.ds(p, rows), :]


def _rows_at(xc_sc, xs_sc, offset, tr):
    p = offset % SUBLANES
    base = offset - p
    return xc_sc[pl.ds(base, tr), :] if p == 0 else xs_sc[p - 1, pl.ds(base, tr), :]


def _shift_scratch(tr, C):
    return pltpu.VMEM((SUBLANES - 1, tr + CONV_HALO - SUBLANES, C), F32)


def _cv_glu_conv(a_ref, b_ref, ap_ref, bp_ref, w_ref, bd_ref, xc_sc, xs_sc, tr):
    i = pl.program_id(0)
    prev = ap_ref[...] * jax.nn.sigmoid(bp_ref[...])
    xc_sc[0:CONV_HALO, :] = jnp.where(i > 0, prev, 0.0)
    xc_sc[CONV_HALO:, :] = a_ref[...] * jax.nn.sigmoid(b_ref[...])
    _shift_rows(xc_sc, xs_sc)
    acc = jnp.broadcast_to(bd_ref[...], (tr, bd_ref.shape[1]))
    for k in range(CONV_WIDTH):
        acc = acc + w_ref[k:k + 1, :] * _rows_at(xc_sc, xs_sc, CONV_HALO - (CONV_WIDTH - 1) + k, tr)
    return acc


def _cv_specs(T, C, tr):
    hb = tr // CONV_HALO
    cur = lambda col: pl.BlockSpec((tr, C), lambda i: (i, col))
    prev = lambda col: pl.BlockSpec((CONV_HALO, C), lambda i: (jnp.maximum(i * hb - 1, 0), col))
    row = pl.BlockSpec((1, C), lambda i: (0, 0))
    wspec = pl.BlockSpec((CONV_HALO, C), lambda i: (0, 0))
    return cur, prev, row, wspec


def _cv_fwd(p, w_dw, b_dw, ln_g, ln_b, *, name):
    T, C = p.shape[0], p.shape[1] // 2
    tr = _row_tile(T)
    cur, prev, row, wspec = _cv_specs(T, C, tr)

    def body(a_ref, b_ref, ap_ref, bp_ref, w_ref, bd_ref, g_ref, be_ref, o_ref, xc_sc, xs_sc):
        y2 = _cv_glu_conv(a_ref, b_ref, ap_ref, bp_ref, w_ref, bd_ref, xc_sc, xs_sc, tr)
        yhat, _ = _layer_norm_stats(y2)
        yln = yhat * g_ref[...] + be_ref[...]
        o_ref[...] = (yln * jax.nn.sigmoid(yln)).astype(BF16)

    return pl.pallas_call(
        body, name=name, grid=(T // tr,),
        in_specs=[cur(0), cur(1), prev(0), prev(1), wspec, row, row, row],
        out_specs=pl.BlockSpec((tr, C), lambda i: (i, 0)),
        out_shape=jax.ShapeDtypeStruct((T, C), BF16),
        scratch_shapes=[pltpu.VMEM((tr + CONV_HALO, C), F32), _shift_scratch(tr, C)],
        compiler_params=_cparams("parallel"),
    )(p, p, p, p, w_dw, b_dw, ln_g, ln_b)


def _cv_bwd_ln(p, dy3, w_dw, b_dw, ln_g, ln_b, *, name):
    T, C = p.shape[0], p.shape[1] // 2
    tr = _row_tile(T)
    cur, prev, row, wspec = _cv_specs(T, C, tr)

    def body(a_ref, b_ref, ap_ref, bp_ref, dy_ref, w_ref, bd_ref, g_ref, be_ref,
             dy2_ref, dw_ref, sums_ref, xc_sc, xs_sc):
        i = pl.program_id(0)
        y2 = _cv_glu_conv(a_ref, b_ref, ap_ref, bp_ref, w_ref, bd_ref, xc_sc, xs_sc, tr)
        yhat, rstd = _layer_norm_stats(y2)
        yln = yhat * g_ref[...] + be_ref[...]
        s = jax.nn.sigmoid(yln)
        dyln = dy_ref[...] * (s + yln * s * (1.0 - s))
        dy2 = _layer_norm_bwd(dyln * g_ref[...], yhat, rstd)
        dy2_ref[...] = dy2
        sums = jnp.concatenate([jnp.sum(dy2, axis=0, keepdims=True),
                                jnp.sum(dyln * yhat, axis=0, keepdims=True),
                                jnp.sum(dyln, axis=0, keepdims=True),
                                jnp.zeros((5, C), F32)], axis=0)
        taps = [jnp.sum(dy2 * _rows_at(xc_sc, xs_sc, CONV_HALO - (CONV_WIDTH - 1) + k, tr), axis=0, keepdims=True)
                for k in range(CONV_WIDTH)]
        dw = jnp.concatenate(taps + [jnp.zeros((CONV_HALO - CONV_WIDTH, C), F32)], axis=0)

        @pl.when(i == 0)
        def _():
            sums_ref[...] = sums
            dw_ref[...] = dw

        @pl.when(i > 0)
        def _():
            sums_ref[...] += sums
            dw_ref[...] += dw

    blk = pl.BlockSpec((tr, C), lambda i: (i, 0))
    return pl.pallas_call(
        body, name=name, grid=(T // tr,),
        in_specs=[cur(0), cur(1), prev(0), prev(1), blk, wspec, row, row, row],
        out_specs=[blk, wspec, pl.BlockSpec((8, C), lambda i: (0, 0))],
        out_shape=[jax.ShapeDtypeStruct((T, C), F32), jax.ShapeDtypeStruct((CONV_HALO, C), F32),
                   jax.ShapeDtypeStruct((8, C), F32)],
        scratch_shapes=[pltpu.VMEM((tr + CONV_HALO, C), F32), _shift_scratch(tr, C)],
        compiler_params=_cparams("arbitrary"),
    )(p, p, p, p, dy3, w_dw, b_dw, ln_g, ln_b)


def _cv_bwd_in(p, dy2, w_dw, *, name):
    T, C = p.shape[0], p.shape[1] // 2
    tr = _row_tile(T)
    hb = tr // CONV_HALO
    nblk = T // tr
    last_halo = T // CONV_HALO - 1

    def body(a_ref, b_ref, dy_ref, dyn_ref, w_ref, dp_ref, sums_ref, xc_sc, xs_sc):
        i = pl.program_id(0)
        xc_sc[0:tr, :] = dy_ref[...]
        xc_sc[tr:, :] = jnp.where(i < nblk - 1, dyn_ref[...], 0.0)
        _shift_rows(xc_sc, xs_sc)
        dy1 = jnp.zeros((tr, C), F32)
        for k in range(CONV_WIDTH):
            dy1 = dy1 + w_ref[k:k + 1, :] * _rows_at(xc_sc, xs_sc, CONV_WIDTH - 1 - k, tr)
        a = a_ref[...]
        sb = jax.nn.sigmoid(b_ref[...])
        da = dy1 * sb
        db = dy1 * a * sb * (1.0 - sb)
        dp_ref[:, :C] = da.astype(BF16)
        dp_ref[:, C:] = db.astype(BF16)
        sums = jnp.concatenate([
            jnp.concatenate([jnp.sum(da, axis=0, keepdims=True), jnp.sum(db, axis=0, keepdims=True)], axis=1),
            jnp.zeros((7, 2 * C), F32)], axis=0)

        @pl.when(i == 0)
        def _():
            sums_ref[...] = sums

        @pl.when(i > 0)
        def _():
            sums_ref[...] += sums

    blk = lambda col: pl.BlockSpec((tr, C), lambda i: (i, col))
    return pl.pallas_call(
        body, name=name, grid=(nblk,),
        in_specs=[blk(0), blk(1), blk(0),
                  pl.BlockSpec((CONV_HALO, C), lambda i: (jnp.minimum((i + 1) * hb, last_halo), 0)),
                  pl.BlockSpec((CONV_HALO, C), lambda i: (0, 0))],
        out_specs=[pl.BlockSpec((tr, 2 * C), lambda i: (i, 0)), pl.BlockSpec((8, 2 * C), lambda i: (0, 0))],
        out_shape=[jax.ShapeDtypeStruct((T, 2 * C), BF16), jax.ShapeDtypeStruct((8, 2 * C), F32)],
        scratch_shapes=[pltpu.VMEM((tr + CONV_HALO, C), F32), _shift_scratch(tr, C)],
        compiler_params=_cparams("arbitrary"),
    )(p, p, dy2, dy2, w_dw)


def _col_tile(n, want=1024):
    best = LANE
    for t in range(LANE, min(n, want) + 1, LANE):
        if n % t == 0:
            best = t
    return best if n % LANE == 0 else n


def _mm(a, b, *, name, ta=False, tb=False, **kw):
    M = a.shape[1] if ta else a.shape[0]
    N = b.shape[0] if tb else b.shape[1]
    K = a.shape[0] if ta else a.shape[1]
    kw.setdefault('tm', _col_tile(M, 1024 if ta else 512))
    kw.setdefault('tn', _col_tile(N, 1024))
    kw.setdefault('tk', K if tb else _col_tile(K, 2048 if ta else 1024))
    return _matmul(a, b, name=name, ta=ta, tb=tb, **kw)


WIDE_ROWS = 1024


def _relu2_epilogue(acc):
    r = jnp.maximum(acc, 0.0)
    return (r * r,)


def _residual_epilogue(acc, x, g):
    return acc, x + g * acc


def _residual_bias_epilogue(acc, x, g, b):
    y = acc + b
    return y, x + g * y


def _dh_norm_bwd(d_act, w, x, dres, nw, sc, *, name, gate=None):
    D = x.shape[1]

    def epilogue(dh, xv, dresv, wv, scv, *gated):
        r = lax.rsqrt(jnp.mean(xv * xv, axis=-1, keepdims=True) + EPS)
        n = xv * r
        scale = 1.0 + scv
        dn = dh * (wv * scale)
        dx = dresv + r * (dn - n * jnp.mean(dn * n, axis=-1, keepdims=True))
        rows = [jnp.sum(dh, axis=0, keepdims=True),
                jnp.sum(dh * (n * wv), axis=0, keepdims=True),
                jnp.sum(dh * n * scale, axis=0, keepdims=True)]
        outs = [dx]
        if gated:
            yv, gv = gated
            outs.append(dx * gv)
            rows += [jnp.sum(dx * yv, axis=0, keepdims=True), jnp.sum(dx * gv, axis=0, keepdims=True)]
        return (*outs, jnp.concatenate(rows + [jnp.zeros((8 - len(rows), D), F32)], axis=0))

    extras = [(x, 'tile'), (dres, 'tile'), (nw, 'row'), (sc, 'row')]
    if gate:
        extras += [(gate[0], 'tile'), (gate[1], 'row')]
    return _mm(d_act, w, tb=True, name=name, tn=D, extras=extras, epilogue=epilogue,
               out_dtypes=(F32, BF16) if gate else (F32,), row_sums=True)


def _relu2_bwd_epilogue(acc, r):
    return (acc * (2.0 * jnp.sqrt(r.astype(F32))),)


def _bias_epilogue(acc, b):
    return (acc + b,)


def _fox_forward(h1, P, j, D, comm=None):
    H = D // HEAD_DIM
    proj = _mm(h1, P['fox_w_in'][j], name='fox_proj', b_outer=True, tm=WIDE_ROWS)
    qg = jnp.tile(P['fox_q_norm'][j][None, :], (1, 2))
    kg = jnp.tile(P['fox_k_norm'][j][None, :], (1, 2))
    fpre_t = proj[:, 3 * D:3 * D + H].T
    bf = P['fox_b_f'][j][:, None]
    fcum = _fox_gate_fwd(fpre_t, bf, name='fox_gate_fwd')
    qa, qta, ka, kta, va, vta = _fox_prep_fwd(proj, qg, kg, fcum[:, :, None], d_model=D, name='fox_prep_fwd')
    o, qb, comm_outs = _fox_attn_fwd(qa, kta, va, name='fox_attn_fwd', comm=comm)
    saved = dict(proj=proj, qg=qg, kg=kg, fpre_t=fpre_t, bf=bf, o=o, qb=qb, qta=qta, ka=ka, kta=kta, vta=vta)
    return o, saved, comm_outs


def _fox_backward(dy, h1, S, P, j, D, comm=None):
    H = D // HEAD_DIM
    w_out, w_in = P['fox_w_out'][j], P['fox_w_in'][j]
    g = {}
    g['fox_w_out'] = _mm(S['o'], dy, ta=True, name='fox_dw_out')
    do = _mm(dy, w_out, tb=True, name='fox_do')
    doa, dota = _fox_do_prep(do, S['o'], name='fox_do_prep')
    dq, dkt, dvt, dcol, drow, comm_outs = _fox_attn_bwd(S['qb'], S['qta'], S['ka'], S['kta'], S['vta'], doa, dota,
                                                        name='fox_attn_bwd', comm=comm)
    dqp, dkp, dvp, gsum = _fox_prep_bwd(S['proj'], dq, dkt, dvt, S['qg'], S['kg'], d_model=D, name='fox_prep_bwd')
    dfpre_t, dbf = _fox_gate_bwd(dcol[:, 0, :], drow[:, :, 0], S['fpre_t'], S['bf'], name='fox_gate_bwd')
    dfpre = jnp.pad(dfpre_t.T.astype(BF16), ((0, 0), (0, LANE - H)))
    dproj = jnp.concatenate([dqp, dkp, dvp, dfpre], axis=1)
    g['fox_w_in'] = _mm(h1, dproj, ta=True, name='fox_dw_in')[:, :3 * D + H]
    g['fox_b_f'] = dbf[:, 0]
    g['fox_q_norm'] = gsum[0, :HEAD_DIM]
    g['fox_k_norm'] = gsum[1, :HEAD_DIM]
    return (dproj, w_in), g, comm_outs


def _sg_forward(h1, P, D):
    a_uv = _mm(h1, P['sg_w_in'], name='sg_in', b_outer=True, tm=WIDE_ROWS)
    bcol = P['sg_b_s'][:, :, None]
    gate = _sg_fwd(a_uv, P['sg_ln_g'], P['sg_ln_b'], P['sg_w_s'], bcol, name='sg_fwd')
    return gate, dict(a_uv=a_uv, bcol=bcol, gate=gate)


def _sg_backward(dy, h1, S, P, D):
    g = {}
    g['sg_w_out'] = _mm(S['gate'], dy, ta=True, name='sg_dw_out')
    dgate = _mm(dy, P['sg_w_out'], tb=True, name='sg_dgate')
    da, dws, dbs, sums = _sg_bwd(S['a_uv'], dgate, P['sg_ln_g'], P['sg_ln_b'], P['sg_w_s'], S['bcol'],
                                 name='sg_bwd')
    g['sg_w_s'], g['sg_b_s'] = dws, dbs[:, :, 0]
    g['sg_ln_g'], g['sg_ln_b'] = sums[0], sums[1]
    g['sg_w_in'] = _mm(h1, da, ta=True, name='sg_dw_in', out_chips=N_CHIPS)
    return (da, P['sg_w_in']), g


def _cv_forward(h1, P, D):
    p = _mm(h1, P['cv_w_pw1'], name='cv_pw1', extras=[(P['cv_b_pw1'], 'row')], epilogue=_bias_epilogue,
            b_outer=True, tm=WIDE_ROWS)
    w_dw = jnp.pad(P['cv_w_dw'], ((0, CONV_HALO - CONV_WIDTH), (0, 0)))
    y3 = _cv_fwd(p, w_dw, P['cv_b_dw'], P['cv_ln_g'], P['cv_ln_b'], name='cv_fwd')
    return y3, dict(p=p, w_dw=w_dw, y3=y3)


def _cv_backward(dy, h1, S, P, D):
    g = {}
    g['cv_w_pw2'] = _mm(S['y3'], dy, ta=True, name='cv_dw_pw2')
    dy3 = _mm(dy, P['cv_w_pw2'], tb=True, name='cv_dy3')
    dy2, dw, sums = _cv_bwd_ln(S['p'], dy3, S['w_dw'], P['cv_b_dw'], P['cv_ln_g'], P['cv_ln_b'], name='cv_bwd_ln')
    g['cv_w_dw'] = dw[:CONV_WIDTH]
    g['cv_b_dw'], g['cv_ln_g'], g['cv_ln_b'] = sums[0], sums[1], sums[2]
    dp, psum = _cv_bwd_in(S['p'], dy2, S['w_dw'], name='cv_bwd_in')
    g['cv_b_pw1'] = psum[0]
    g['cv_w_pw1'] = _mm(h1, dp, ta=True, name='cv_dw_pw1', out_chips=N_CHIPS)
    return (dp, P['cv_w_pw1']), g


class Hooks(NamedTuple):
    fwd_comm: Comm
    fwd_done: Callable
    bwd_comm: Callable
    bwd_done: Callable


def _local_step(x, target, mod, P, hooks=None):
    T, D = x.shape
    L = mod.shape[0]
    saved = []
    for i in range(L):
        kind, j = i % N_MIXERS, i // N_MIXERS
        m = [mod[i:i + 1, k * D:(k + 1) * D] for k in range(6)]
        sh_m, sc_m, g_m, sh_f, sc_f, g_f = m
        w_mix, w_mlp = P['norm_mix'][i:i + 1], P['norm_mlp'][i:i + 1]
        h1 = _norm_mod_fwd(x, w_mix, sc_m, sh_m, name='norm_mix_fwd')
        if kind == 0:
            carried = hooks is not None and i == 0
            op, S, comm_outs = _fox_forward(h1, P, j, D, comm=hooks.fwd_comm if carried else None)
            if carried:
                hooks.fwd_done(comm_outs)
            y, x1 = _mm(op, P['fox_w_out'][j], name='fox_out', extras=[(x, 'tile'), (g_m, 'row')],
                        epilogue=_residual_epilogue, out_dtypes=(F32, F32))
        elif kind == 1:
            op, S = _sg_forward(h1, P, D)
            y, x1 = _mm(op, P['sg_w_out'], name='sg_out', extras=[(x, 'tile'), (g_m, 'row')],
                        epilogue=_residual_epilogue, out_dtypes=(F32, F32))
        else:
            op, S = _cv_forward(h1, P, D)
            y, x1 = _mm(op, P['cv_w_pw2'], name='cv_out',
                        extras=[(x, 'tile'), (g_m, 'row'), (P['cv_b_pw2'], 'row')],
                        epilogue=_residual_bias_epilogue, out_dtypes=(F32, F32))
        h2 = _norm_mod_fwd(x1, w_mlp, sc_f, sh_f, name='norm_mlp_fwd')
        r = _mm(h2, P['w_mlp_in'][i], name='mlp_in', epilogue=_relu2_epilogue, out_dtypes=(BF16,),
                b_outer=True, tm=WIDE_ROWS)
        z, x2 = _mm(r, P['w_mlp_out'][i], name='mlp_out', extras=[(x1, 'tile'), (g_f, 'row')],
                    epilogue=_residual_epilogue, out_dtypes=(F32, F32), tk=P['w_mlp_out'][i].shape[0])
        saved.append(dict(x=x, h1=h1, S=S, y=y, x1=x1, h2=h2, r=r, z=z, m=m))
        x = x2

    loss_part, dx = _loss_head(x, target, name='loss_head')

    grads = {k: [None] * L for k in ('norm_mix', 'norm_mlp')}
    mix_grads, mat = {}, {}
    dmod = [None] * L
    for i in reversed(range(L)):
        kind, j = i % N_MIXERS, i // N_MIXERS
        sv = saved[i]
        sh_m, sc_m, g_m, sh_f, sc_f, g_f = sv['m']
        w_mix, w_mlp = P['norm_mix'][i:i + 1], P['norm_mlp'][i:i + 1]
        dz, dgf = _gate_bwd(dx, sv['z'], g_f, name='mlp_gate_bwd')
        mat['w_mlp_out', i] = _mm(sv['r'], dz, ta=True, name='mlp_dw_out')
        da = _mm(dz, P['w_mlp_out'][i], tb=True, name='mlp_da', extras=[(sv['r'], 'tile')],
                 epilogue=_relu2_bwd_epilogue, out_dtypes=(BF16,), b_outer=True, tm=WIDE_ROWS)
        mat['w_mlp_in', i] = _mm(sv['h2'], da, ta=True, name='mlp_dw_in', out_chips=N_CHIPS)
        dx1, dy, sums_f = _dh_norm_bwd(da, P['w_mlp_in'][i], sv['x1'], dx, w_mlp, sc_f, name='mlp_dh',
                                       gate=(sv['y'], g_m))
        if kind == 0:
            carried = hooks is not None and i == 0
            last, g, comm_outs = _fox_backward(dy, sv['h1'], sv['S'], P, j, D,
                                              comm=hooks.bwd_comm(mat) if carried else None)
            if carried:
                hooks.bwd_done(comm_outs)
        elif kind == 1:
            last, g = _sg_backward(dy, sv['h1'], sv['S'], P, D)
        else:
            last, g = _cv_backward(dy, sv['h1'], sv['S'], P, D)
            g['cv_b_pw2'] = sums_f[4]
        for k, val in g.items():
            if k in BIG:
                mat[k, j] = val
            else:
                mix_grads.setdefault(k, {})[j] = val
        dx, sums_m = _dh_norm_bwd(*last, sv['x'], dx1, w_mix, sc_m, name='mix_dh')
        grads['norm_mlp'][i], grads['norm_mix'][i] = sums_f[2], sums_m[2]
        dmod[i] = jnp.concatenate([sums_m[0], sums_m[1], sums_f[3], sums_f[0], sums_f[1], dgf[0]])

    out = {k: jnp.stack(v) for k, v in grads.items()}
    for k, per_j in mix_grads.items():
        out[k] = jnp.stack([per_j[j] for j in sorted(per_j)])
    return loss_part, dx, jnp.stack(dmod), out, mat


def _all_gather8(blocks, *, name):
    n = len(blocks)

    def body(*refs):
        x_refs, out_refs = refs[:n], refs[n:2 * n]
        send_sems, recv_sems, local_sems = refs[2 * n:]
        x, y, c = _position()
        me, sibling = (x, y, c), (x, y, 1 - c)
        chips = [(1 - x, y), (x, 1 - y), (1 - x, 1 - y)]

        def slot(a, px, py, pc):
            return out_refs[a].at[4 * px + 2 * py + pc]

        def copy(a, k, blk, to, src=None):
            return pltpu.make_async_remote_copy(
                src_ref=slot(a, *blk) if src is None else src, dst_ref=slot(a, *blk),
                send_sem=send_sems.at[7 * a + k], recv_sem=recv_sems.at[7 * a + k],
                device_id=to, device_id_type=MESH)

        mine = [pltpu.make_async_copy(x_refs[a], slot(a, *me), local_sems.at[a]) for a in range(n)]
        for cp in mine:
            cp.start()
        first = []
        for j, chip in enumerate(chips):
            first += [copy(a, 1 + j, me, (*chip, c), src=x_refs[a]) for a in range(n)]
        first += [copy(a, 0, me, sibling, src=x_refs[a]) for a in range(n)]
        for cp in first:
            cp.start()
        passed = []
        for j, chip in enumerate(chips):
            for a in range(n):
                copy(a, 1 + j, (*chip, c), me).wait_recv()
                passed.append(copy(a, 4 + j, (*chip, c), sibling))
                passed[-1].start()
        for a in range(n):
            copy(a, 0, sibling, me).wait_recv()
        for j, chip in enumerate(chips):
            for a in range(n):
                copy(a, 4 + j, (*chip, 1 - c), me).wait_recv()
        for cp in first + passed:
            cp.wait_send()
        for cp in mine:
            cp.wait()

    return pl.pallas_call(
        body, name=name, in_specs=[ANY] * n, out_specs=[ANY] * n,
        out_shape=[jax.ShapeDtypeStruct((8,) + b.shape, b.dtype) for b in blocks],
        scratch_shapes=[pltpu.SemaphoreType.DMA((7 * n,)), pltpu.SemaphoreType.DMA((7 * n,)),
                        pltpu.SemaphoreType.DMA((n,))],
    )(*blocks)


def _exchange(comm, *, name, aliases=None):
    ns, no = len(comm.srcs), len(comm.out_shapes)

    def body(*refs):
        copies = _comm_copies(comm.plan, refs[:ns], refs[ns:ns + no], *refs[ns + no:])
        _comm_start(copies)
        _comm_wait(copies)

    return pl.pallas_call(
        body, name=name, in_specs=[ANY] * ns, out_specs=[ANY] * no, out_shape=list(comm.out_shapes),
        scratch_shapes=comm.scratch(), input_output_aliases=aliases or {},
    )(*comm.srcs)


def _gather_comm(halves):
    n = len(halves)

    def plan(src, out, x, y, c):
        mine = 4 * x + 2 * y + c
        remote = [(src[a], out[a].at[mine], (x, y, 1 - c), out[a].at[4 * x + 2 * y + 1 - c]) for a in range(n)]
        for fx, fy in CHIP_FLIPS:
            px, py = _flip(x, fx), _flip(y, fy)
            remote += [(src[a], out[a].at[mine], (px, py, c), out[a].at[4 * px + 2 * py + c]) for a in range(n)]
        return remote, [(src[a], out[a].at[mine]) for a in range(n)]

    return Comm(list(halves), [jax.ShapeDtypeStruct((8,) + h.shape, h.dtype) for h in halves], plan, 4 * n, n)


def _gather_forward(bufs, *, name):
    n = len(bufs)

    def plan(src, out, x, y, c):
        remote = []
        for fx, fy in CHIP_FLIPS:
            px, py = _flip(x, fx), _flip(y, fy)
            remote += [(src[a].at[4 * px + 2 * py + c], out[a].at[4 * px + 2 * py + c], (x, y, 1 - c),
                        out[a].at[4 * px + 2 * py + 1 - c]) for a in range(n)]
        return remote, []

    comm = Comm(list(bufs), [jax.ShapeDtypeStruct(b.shape, b.dtype) for b in bufs], plan, 3 * n, 0)
    return _exchange(comm, name=name, aliases={a: a for a in range(n)})


CHIP_FLIPS = ((1, 0), (0, 1), (1, 1))


def _flip(v, f):
    return 1 - v if f else v


def _sum_rows_tile(R, C, budget=3 << 20):
    best = None
    for t in range(8, R + 1, 8):
        if R % t == 0 and t * C * 4 <= budget:
            best = t
    return best if best is not None else R


def _rs_begin(gps, *, wire_dtype):
    n = len(gps)
    c_arr = jnp.reshape(_position()[2], (1,)).astype(jnp.int32)

    def plan(src, out, x, y, c):
        return [(src[a].at[b, 1 - c], out[a].at[b], (x, y, 1 - c), out[a].at[b])
                for a in range(n) for b in range(4)], []

    got1 = _exchange(Comm(list(gps), [jax.ShapeDtypeStruct((4,) + g.shape[2:], F32) for g in gps], plan, 4 * n, 0),
                     name='rs_sibling')

    def sum_chip(c_ref, mine_ref, got_ref, out_ref):
        out_ref[...] = (mine_ref[...] + got_ref[...]).astype(out_ref.dtype)

    parts = []
    for gp, g1 in zip(gps, got1):
        _, _, R, C = gp.shape
        tr = _sum_rows_tile(R, C)
        parts.append(pl.pallas_call(
            sum_chip, name='rs_sum_chip',
            grid_spec=pltpu.PrefetchScalarGridSpec(
                num_scalar_prefetch=1, grid=(4, R // tr),
                in_specs=[pl.BlockSpec((None, None, tr, C), lambda b, r, cr: (b, cr[0], r, 0)),
                          pl.BlockSpec((None, tr, C), lambda b, r, cr: (b, r, 0))],
                out_specs=pl.BlockSpec((None, tr, C), lambda b, r, cr: (b, r, 0))),
            out_shape=jax.ShapeDtypeStruct((4, R, C), wire_dtype),
            compiler_params=_cparams("parallel", "parallel"),
        )(c_arr, gp, g1))
    return got1, parts


def _rs_chips_comm(parts):
    n = len(parts)

    def plan(src, out, x, y, c):
        remote = []
        for k, (fx, fy) in enumerate(CHIP_FLIPS):
            px, py = _flip(x, fx), _flip(y, fy)
            remote += [(src[a].at[2 * px + py], out[a].at[k], (px, py, c), out[a].at[k]) for a in range(n)]
        return remote, []

    return Comm(list(parts), [jax.ShapeDtypeStruct((3,) + p.shape[1:], p.dtype) for p in parts], plan, 3 * n, 0)


def _rs_finish(gps, got1, got2):
    n = len(gps)
    x, y, c = _position()
    bc_arr = jnp.stack([2 * x + y, c]).astype(jnp.int32)

    def sum_final(bc_ref, mine_ref, got1_ref, got2_ref, out_ref):
        acc = mine_ref[...] + got1_ref[...]
        for k in range(3):
            acc = acc + got2_ref[k].astype(F32)
        out_ref[...] = acc

    halves = []
    for gp, g1, g2 in zip(gps, got1, got2):
        _, _, R, C = gp.shape
        tr = _sum_rows_tile(R, C, budget=2 << 20)
        halves.append(pl.pallas_call(
            sum_final, name='rs_sum_final',
            grid_spec=pltpu.PrefetchScalarGridSpec(
                num_scalar_prefetch=1, grid=(R // tr,),
                in_specs=[pl.BlockSpec((None, None, tr, C), lambda r, bc: (bc[0], bc[1], r, 0)),
                          pl.BlockSpec((None, tr, C), lambda r, bc: (bc[0], r, 0)),
                          pl.BlockSpec((3, tr, C), lambda r, bc: (0, r, 0))],
                out_specs=pl.BlockSpec((None, tr, C), lambda r, bc: (bc[1], r, 0))),
            out_shape=jax.ShapeDtypeStruct((2, R, C), F32),
            compiler_params=_cparams("parallel"),
        )(bc_arr, gp, g1, g2))

    def plan(src, out, x, y, c):
        return [(src[a].at[c], out[a].at[c], (x, y, 1 - c), out[a].at[1 - c]) for a in range(n)], []

    comm = Comm(halves, [jax.ShapeDtypeStruct(h.shape, F32) for h in halves], plan, n, 0)
    return _exchange(comm, name='rs_swap', aliases={a: a for a in range(n)})


def _sum8(gathered, *, name):
    _, R, C = gathered.shape

    def body(g_ref, o_ref):
        acc = g_ref[0]
        for k in range(1, 8):
            acc = acc + g_ref[k]
        o_ref[...] = acc

    return pl.pallas_call(body, name=name, out_shape=jax.ShapeDtypeStruct((R, C), F32))(gathered)


def _adamw(w, g, m, v, *, name):
    shape = w.shape
    cols = shape[-1]
    rows = w.size // cols
    tr = _sum_rows_tile(rows, cols, budget=1 << 20)

    def body(w_ref, g_ref, m_ref, v_ref, d_ref, mo_ref, vo_ref):
        gv = g_ref[...]
        mn = ADAM_B1 * m_ref[...] + (1.0 - ADAM_B1) * gv
        vn = ADAM_B2 * v_ref[...] + (1.0 - ADAM_B2) * (gv * gv)
        m_hat = mn / (1.0 - ADAM_B1 ** ADAM_STEP)
        v_hat = vn / (1.0 - ADAM_B2 ** ADAM_STEP)
        d_ref[...] = -ADAM_LR * (m_hat / (jnp.sqrt(v_hat) + ADAM_EPS) + ADAM_WD * w_ref[...])
        mo_ref[...] = mn
        vo_ref[...] = vn

    blk = pl.BlockSpec((tr, cols), lambda i: (i, 0))
    outs = pl.pallas_call(
        body, name=name, grid=(rows // tr,), in_specs=[blk] * 4, out_specs=[blk] * 3,
        out_shape=[jax.ShapeDtypeStruct((rows, cols), F32)] * 3,
        compiler_params=_cparams("parallel"),
    )(*[a.reshape(rows, cols) for a in (w, g, m, v)])
    return tuple(o.reshape(shape) for o in outs)


WEIGHTS = ['norm_mix', 'norm_mlp', 'w_ada', 'b_ada', 'w_mlp_in', 'w_mlp_out', 'fox_w_in', 'fox_b_f',
           'fox_q_norm', 'fox_k_norm', 'fox_w_out', 'sg_w_in', 'sg_ln_g', 'sg_ln_b', 'sg_w_s', 'sg_b_s',
           'sg_w_out', 'cv_w_pw1', 'cv_b_pw1', 'cv_w_dw', 'cv_b_dw', 'cv_ln_g', 'cv_ln_b', 'cv_w_pw2',
           'cv_b_pw2']
BIG = {'w_mlp_in': 2, 'w_mlp_out': 1, 'fox_w_in': 2, 'fox_w_out': 1, 'sg_w_in': 2, 'sg_w_out': 1,
       'cv_w_pw1': 2, 'cv_w_pw2': 1}
SMALL_SHARDED = ['cv_b_pw1', 'cv_w_dw', 'cv_b_dw', 'cv_ln_g', 'cv_ln_b', 'cv_b_pw2']
SMALL_GRADS = ['norm_mix', 'norm_mlp', 'fox_b_f', 'fox_q_norm', 'fox_k_norm', 'sg_ln_g', 'sg_ln_b', 'sg_w_s',
               'sg_b_s'] + SMALL_SHARDED
GRAD_WIRE_DTYPE = BF16


def _pack_rows(parts, cols):
    flat = jnp.concatenate([p.reshape(-1) for p in parts])
    rows = -(-flat.size // (8 * cols)) * 8
    return jnp.pad(flat, (0, rows * cols - flat.size)).reshape(rows, cols)


def _unpack(flat, shapes):
    out, off = [], 0
    for s in shapes:
        n = math.prod(s)
        out.append(flat[..., off:off + n].reshape(flat.shape[:-1] + tuple(s)))
        off += n
    return out


def _merge_chips(a, axis):
    a = jnp.moveaxis(a, 0, axis)
    return a.reshape(a.shape[:axis] + (a.shape[axis] * a.shape[axis + 1],) + a.shape[axis + 2:])


def _split_chips(a, axis):
    a = a.reshape(a.shape[:axis] + (4, a.shape[axis] // 4) + a.shape[axis + 1:])
    return jnp.moveaxis(a, axis, 0)


def _step(a):
    x, y, c = _position()
    me = 4 * x + 2 * y + c
    chip = 2 * x + y
    T, D = a['x'].shape[1], a['x'].shape[2]
    L = a['norm_mix'].shape[0]

    small_shapes = [(D,)] + [a[n].shape for n in SMALL_SHARDED]
    small = _all_gather8([_pack_rows([a['c']] + [a[n] for n in SMALL_SHARDED], LANE)], name='ag_small')[0]
    small = small.reshape(8, -1)
    c_all = _unpack(small, small_shapes[:1])[0]
    sharded = _unpack(small[0::2, D:], small_shapes[1:])
    P = {n: _merge_chips(v, v.ndim - 2) for n, v in zip(SMALL_SHARDED, sharded)}

    c_act = _silu_rows(c_all, name='c_act')
    mod_cols = jnp.stack([
        _mm(c_act, a['w_ada'][i], name='ada_mod', tm=8, tn=_col_tile(a['w_ada'].shape[2], 768),
            extras=[(lax.dynamic_slice_in_dim(a['b_ada'][i:i + 1], chip * a['w_ada'].shape[2],
                                              a['w_ada'].shape[2], axis=1), 'row')],
            epilogue=_bias_epilogue)
        for i in range(L)])
    mod_all = _all_gather8([mod_cols.reshape(L * 8, -1)], name='ag_mod')[0].reshape(8, L, 8, -1)
    mod = lax.dynamic_index_in_dim(mod_all[0::2], me, axis=2, keepdims=False)
    mod = jnp.moveaxis(mod, 0, 1).reshape(L, 6 * D)

    units = _matrix_units(L)
    first, later = units[:1], units[1:]
    last, earlier = units[:2], units[2:]
    n_heads = D // HEAD_DIM

    def half_block(unit):
        blk = a[unit[0]][unit[1]]
        return lax.dynamic_index_in_dim(blk.astype(BF16).reshape(2, blk.shape[0] // 2, blk.shape[1]), c, axis=0,
                                        keepdims=False)

    def install(group, gathered):
        for (name, idx), gth in zip(group, gathered):
            blocks = gth.reshape((4,) + a[name].shape[1:])
            if name == 'fox_w_in':
                pad = jnp.zeros((blocks.shape[1], LANE - n_heads), BF16)
                full = jnp.concatenate([blocks[0], blocks[1], blocks[2], blocks[3], pad], axis=-1)
            else:
                full = _merge_chips(blocks, BIG[name] - 1)
            if name in ('w_mlp_in', 'w_mlp_out', 'fox_w_in', 'fox_w_out'):
                P.setdefault(name, {})[idx] = full
            else:
                P[name] = full

    install(first, _all_gather8([half_block(u) for u in first], name='ag_weights_first'))
    for n in ('sg_w_s', 'sg_b_s', 'cv_w_dw'):
        P[n] = (P[n] if n in P else a[n])[0]
    for n in ('norm_mix', 'norm_mlp', 'fox_b_f', 'fox_q_norm', 'fox_k_norm', 'sg_ln_g', 'sg_ln_b'):
        P[n] = a[n]

    def split_grad(unit, grad):
        name = unit[0]
        if name == 'fox_w_in':
            grad = grad[:, :a[name].shape[2] * N_CHIPS]
        blk = grad if grad.ndim == 3 else _split_chips(grad, BIG[name] - 1)
        return blk.reshape(N_CHIPS, 2, blk.shape[1] // 2, blk.shape[2])

    state = {}

    def fwd_done(outs):
        install(later, _gather_forward(outs, name='ag_weights_forward'))

    def bwd_comm(mat):
        state['gps'] = [split_grad(u, mat[u]) for u in earlier]
        state['got1'], parts = _rs_begin(state['gps'], wire_dtype=GRAD_WIRE_DTYPE)
        return _rs_chips_comm(parts)

    def bwd_done(outs):
        state['got2'] = outs

    hooks = Hooks(_gather_comm([half_block(u) for u in later]), fwd_done, bwd_comm, bwd_done)
    loss_part, grad_x, dmod, g, mat = _local_step(a['x'][0], a['loss_target'][0], mod, P, hooks)

    small_g = [dmod, loss_part[0:1, 0:1]] + [g[n] for n in SMALL_GRADS]
    small_g_shapes = [s.shape for s in small_g]
    all_small = _all_gather8([_pack_rows(small_g, LANE)], name='ag_small_grads')[0]
    summed = _sum8(all_small, name='sum_small_grads').reshape(-1)
    sums = _unpack(summed, small_g_shapes)
    loss = sums[1][0, 0]
    grads = dict(zip(SMALL_GRADS, sums[2:]))
    grads['b_ada'] = sums[0]
    for n in SMALL_SHARDED:
        blk = a[n].shape[-1]
        grads[n] = lax.dynamic_slice_in_dim(grads[n], chip * blk, blk, axis=grads[n].ndim - 1)
    dmod_all = all_small.reshape(8, -1)[:, :dmod.size].reshape(8, L, 6 * D)
    cols = a['w_ada'].shape[2]
    dmod_cols = lax.dynamic_slice_in_dim(dmod_all, chip * cols, cols, axis=2)
    pad8 = lambda t: jnp.pad(t, ((0, LANE - 8), (0, 0)))
    c_act_pad = pad8(c_act)
    grads['w_ada'] = jnp.stack([
        _mm(c_act_pad, pad8(dmod_cols[:, i]), ta=True, name='ada_dw', tn=_col_tile(cols, 768))
        for i in range(L)])

    shards = dict(zip(earlier, _rs_finish(state['gps'], state['got1'], state['got2'])))
    gps = [split_grad(u, mat[u]) for u in last]
    got1, parts = _rs_begin(gps, wire_dtype=GRAD_WIRE_DTYPE)
    got2 = _exchange(_rs_chips_comm(parts), name='rs_chips')
    shards.update(zip(last, _rs_finish(gps, got1, got2)))
    for n in BIG:
        grads[n] = jnp.stack([shards[n, idx].reshape(a[n].shape[1:]) for idx in range(a[n].shape[0])])

    deltas, new_m, new_v = {}, {}, {}
    for n in WEIGHTS:
        deltas[n], new_m[n], new_v[n] = _adamw(a[n], grads[n], a['m_' + n], a['v_' + n], name='adamw')
    return (loss, grad_x[None], *[grads[n] for n in WEIGHTS], *[deltas[n] for n in WEIGHTS],
            *[new_m[n] for n in WEIGHTS], *[new_v[n] for n in WEIGHTS])


def _matrix_units(n_layers):
    mixers = (('fox_w_in', 'fox_w_out'), ('sg_w_in', 'sg_w_out'), ('cv_w_pw1', 'cv_w_pw2'))
    units = []
    for i in range(n_layers):
        units += [(n, i // N_MIXERS) for n in mixers[i % N_MIXERS]] + [('w_mlp_in', i), ('w_mlp_out', i)]
    return units


def _silu_rows(x, *, name):
    def body(x_ref, o_ref):
        xv = x_ref[...]
        o_ref[...] = (xv * jax.nn.sigmoid(xv)).astype(BF16)

    return pl.pallas_call(body, name=name, out_shape=jax.ShapeDtypeStruct(x.shape, BF16))(x)


def kernel(x, c, norm_mix, norm_mlp, w_ada, b_ada, w_mlp_in, w_mlp_out, fox_w_in, fox_b_f, fox_q_norm, fox_k_norm, fox_w_out, sg_w_in, sg_ln_g, sg_ln_b, sg_w_s, sg_b_s, sg_w_out, cv_w_pw1, cv_b_pw1, cv_w_dw, cv_b_dw, cv_ln_g, cv_ln_b, cv_w_pw2, cv_b_pw2, loss_target, m_norm_mix, m_norm_mlp, m_w_ada, m_b_ada, m_w_mlp_in, m_w_mlp_out, m_fox_w_in, m_fox_b_f, m_fox_q_norm, m_fox_k_norm, m_fox_w_out, m_sg_w_in, m_sg_ln_g, m_sg_ln_b, m_sg_w_s, m_sg_b_s, m_sg_w_out, m_cv_w_pw1, m_cv_b_pw1, m_cv_w_dw, m_cv_b_dw, m_cv_ln_g, m_cv_ln_b, m_cv_w_pw2, m_cv_b_pw2, v_norm_mix, v_norm_mlp, v_w_ada, v_b_ada, v_w_mlp_in, v_w_mlp_out, v_fox_w_in, v_fox_b_f, v_fox_q_norm, v_fox_k_norm, v_fox_w_out, v_sg_w_in, v_sg_ln_g, v_sg_ln_b, v_sg_w_s, v_sg_b_s, v_sg_w_out, v_cv_w_pw1, v_cv_b_pw1, v_cv_w_dw, v_cv_b_dw, v_cv_ln_g, v_cv_ln_b, v_cv_w_pw2, v_cv_b_pw2):
    return _step(dict(locals()))
```

```python
import math
from typing import Callable, NamedTuple

import jax
import jax.numpy as jnp
from jax import lax
from jax.experimental import pallas as pl
from jax.experimental.pallas import tpu as pltpu

F32 = jnp.float32
BF16 = jnp.bfloat16

EPS = 1e-6
HEAD_DIM = 64
LANE = 128
CONV_WIDTH = 31
CONV_HALO = 32
SG_CHUNK = 128
SG_CAUSAL = 64
SG_GROUPS = 8
N_MIXERS = 3
N_CHIPS = 4
VMEM_LIMIT = 56 * 1024 * 1024
NEG = -1e30

ADAM_LR = 0.001
ADAM_B1 = 0.9
ADAM_B2 = 0.999
ADAM_EPS = 1e-08
ADAM_WD = 0.01
ADAM_STEP = 10

MESH = pl.DeviceIdType.MESH
ANY = pl.BlockSpec(memory_space=pl.ANY)


def _cparams(*sem):
    return pltpu.CompilerParams(dimension_semantics=sem, vmem_limit_bytes=VMEM_LIMIT)


def _row_tile(t, want=512):
    return min(t, want)


def _matmul(a, b, *, name, ta=False, tb=False, tm=512, tn=1024, tk=1024,
            extras=(), epilogue=None, out_dtypes=(F32,), b_outer=False, out_chips=None, row_sums=False):
    M, K = (a.shape[1], a.shape[0]) if ta else a.shape
    N = b.shape[0] if tb else b.shape[1]
    assert (b.shape[1] if tb else b.shape[0]) == K
    n_own = N // out_chips if out_chips else N
    tm, tn, tk = min(tm, M), min(tn, n_own), min(tk, K)
    assert M % tm == 0 and n_own % tn == 0 and K % tk == 0, (name, M, N, K, tm, tn, tk)
    nk = K // tk

    def spec(shape, pick):
        if b_outer:
            return pl.BlockSpec(shape, lambda j, i, k: pick(i, j, k))
        return pl.BlockSpec(shape, pick)

    a_spec = spec((tk, tm), lambda i, j, k: (k, i)) if ta else spec((tm, tk), lambda i, j, k: (i, k))
    b_spec = spec((tn, tk), lambda i, j, k: (j, k)) if tb else spec((tk, tn), lambda i, j, k: (k, j))
    ex_specs = [spec((tm, tn), lambda i, j, k: (i, j)) if kind == 'tile' else spec((1, tn), lambda i, j, k: (0, j))
                for _, kind in extras]
    dims = (((0,) if ta else (1,), (1,) if tb else (0,)), ((), ()))
    n_ex, n_out = len(extras), len(out_dtypes) + bool(row_sums)
    assert not row_sums or N == tn

    def body(*refs):
        a_ref, b_ref = refs[0], refs[1]
        ex = refs[2:2 + n_ex]
        outs = refs[2 + n_ex:2 + n_ex + n_out]

        def finish(acc):
            vals = epilogue(acc, *[r[...] for r in ex]) if epilogue else (acc,)
            for o, v in zip(outs[:len(out_dtypes)], vals):
                o[...] = v.astype(o.dtype)
            if row_sums:
                row_tile = pl.program_id(1 if b_outer else 0)

                @pl.when(row_tile == 0)
                def _():
                    outs[-1][...] = vals[-1]

                @pl.when(row_tile > 0)
                def _():
                    outs[-1][...] += vals[-1]

        part = lax.dot_general(a_ref[...].astype(BF16), b_ref[...].astype(BF16), dims,
                               preferred_element_type=F32)
        if nk == 1:
            finish(part)
        else:
            acc_ref = refs[-1]
            k = pl.program_id(2)

            @pl.when(k == 0)
            def _():
                acc_ref[...] = part

            @pl.when(k > 0)
            def _():
                acc_ref[...] += part

            @pl.when(k == nk - 1)
            def _():
                finish(acc_ref[...])

    outs = pl.pallas_call(
        body, name=name,
        grid=(N // tn, M // tm, nk) if b_outer else (M // tm, N // tn, nk),
        in_specs=[a_spec, b_spec] + ex_specs,
        out_specs=[spec((None, tm, tn), lambda i, j, k: (j // (n_own // tn), i, j % (n_own // tn)))
                   if out_chips else spec((tm, tn), lambda i, j, k: (i, j)) for _ in out_dtypes]
        + ([spec((8, tn), lambda i, j, k: (0, j))] if row_sums else []),
        out_shape=[jax.ShapeDtypeStruct((out_chips, M, n_own) if out_chips else (M, N), dt) for dt in out_dtypes]
        + ([jax.ShapeDtypeStruct((8, N), F32)] if row_sums else []),
        scratch_shapes=[pltpu.VMEM((tm, tn), F32)] if nk > 1 else [],
        compiler_params=(_cparams("arbitrary", "arbitrary", "arbitrary") if row_sums
                         else _cparams("parallel", "parallel", "arbitrary")),
    )(a, b, *[arr for arr, _ in extras])
    return outs if n_out > 1 else outs[0]


def _norm_mod_fwd(x, w, sc, sh, *, name):
    T, D = x.shape
    tr = _row_tile(T)

    def body(x_ref, w_ref, sc_ref, sh_ref, h_ref):
        xv = x_ref[...]
        r = lax.rsqrt(jnp.mean(xv * xv, axis=-1, keepdims=True) + EPS)
        h_ref[...] = ((xv * r) * w_ref[...] * (1.0 + sc_ref[...]) + sh_ref[...]).astype(BF16)

    row = pl.BlockSpec((1, D), lambda i: (0, 0))
    return pl.pallas_call(
        body, name=name, grid=(T // tr,),
        in_specs=[pl.BlockSpec((tr, D), lambda i: (i, 0)), row, row, row],
        out_specs=pl.BlockSpec((tr, D), lambda i: (i, 0)),
        out_shape=jax.ShapeDtypeStruct((T, D), BF16),
        compiler_params=_cparams("parallel"),
    )(x, w, sc, sh)


def _gate_bwd(dx, y, g, *, name):
    T, D = dx.shape
    tr = _row_tile(T)

    def body(dx_ref, y_ref, g_ref, dy_ref, dg_ref):
        i = pl.program_id(0)
        dxv = dx_ref[...]
        dy_ref[...] = (dxv * g_ref[...]).astype(BF16)
        part = jnp.concatenate([jnp.sum(dxv * y_ref[...], axis=0, keepdims=True),
                                jnp.zeros((7, D), F32)], axis=0)

        @pl.when(i == 0)
        def _():
            dg_ref[...] = part

        @pl.when(i > 0)
        def _():
            dg_ref[...] += part

    blk = pl.BlockSpec((tr, D), lambda i: (i, 0))
    return pl.pallas_call(
        body, name=name, grid=(T // tr,),
        in_specs=[blk, blk, pl.BlockSpec((1, D), lambda i: (0, 0))],
        out_specs=[blk, pl.BlockSpec((8, D), lambda i: (0, 0))],
        out_shape=[jax.ShapeDtypeStruct((T, D), BF16), jax.ShapeDtypeStruct((8, D), F32)],
        compiler_params=_cparams("arbitrary"),
    )(dx, y, g)


def _loss_head(y, target, *, name):
    T, D = y.shape
    tr = _row_tile(T)

    def body(y_ref, t_ref, loss_ref, dy_ref):
        i = pl.program_id(0)
        e = y_ref[...] - t_ref[...]
        dy_ref[...] = e * (1.0 / D)
        part = jnp.full((8, LANE), 0.5 / D * jnp.sum(e * e), F32)

        @pl.when(i == 0)
        def _():
            loss_ref[...] = part

        @pl.when(i > 0)
        def _():
            loss_ref[...] += part

    blk = pl.BlockSpec((tr, D), lambda i: (i, 0))
    return pl.pallas_call(
        body, name=name, grid=(T // tr,), in_specs=[blk, blk],
        out_specs=[pl.BlockSpec((8, LANE), lambda i: (0, 0)), blk],
        out_shape=[jax.ShapeDtypeStruct((8, LANE), F32), jax.ShapeDtypeStruct((T, D), F32)],
        compiler_params=_cparams("arbitrary"),
    )(y, target)


def _position():
    return lax.axis_index("x"), lax.axis_index("y"), lax.axis_index("c")


class Comm(NamedTuple):
    srcs: list
    out_shapes: list
    plan: Callable
    n_remote: int
    n_local: int

    def scratch(self):
        return [pltpu.SemaphoreType.DMA((self.n_remote,)), pltpu.SemaphoreType.DMA((self.n_remote,)),
                pltpu.SemaphoreType.DMA((max(self.n_local, 1),))]


def _comm_copies(plan, src_refs, out_refs, send_sems, recv_sems, local_sems):
    x, y, c = _position()
    remote, local = plan(src_refs, out_refs, x, y, c)

    def copy(k, s, d, peer):
        return pltpu.make_async_remote_copy(src_ref=s, dst_ref=d, send_sem=send_sems.at[k],
                                            recv_sem=recv_sems.at[k], device_id=peer, device_id_type=MESH)

    sends = [copy(k, s, d, peer) for k, (s, d, peer, _) in enumerate(remote)]
    recvs = [copy(k, s, landing, peer) for k, (s, _, peer, landing) in enumerate(remote)]
    local_copies = [pltpu.make_async_copy(s, d, local_sems.at[i]) for i, (s, d) in enumerate(local)]
    return sends, recvs, local_copies


def _comm_start(copies):
    sends, _, local_copies = copies
    for cp in local_copies + sends:
        cp.start()


def _comm_wait(copies):
    sends, recvs, local_copies = copies
    for cp in recvs:
        cp.wait_recv()
    for cp in sends:
        cp.wait_send()
    for cp in local_copies:
        cp.wait()


def _split_comm_refs(refs, n_in, n_out, n_scratch, comm):
    ns, nd = (len(comm.srcs), len(comm.out_shapes)) if comm else (0, 0)
    cuts = [n_in, ns, n_out, nd, n_scratch]
    parts, at = [], 0
    for n in cuts:
        parts.append(refs[at:at + n])
        at += n
    return (*parts, refs[at:])


AUG_F = HEAD_DIM
AUG_LSE = HEAD_DIM + 6


def _half_cols(x, lo):
    return (jnp.sum(jnp.where(lo, x, 0.0), axis=-1, keepdims=True),
            jnp.sum(jnp.where(lo, 0.0, x), axis=-1, keepdims=True))


def _half_sums(x, lo):
    s_lo, s_hi = _half_cols(x, lo)
    return jnp.where(lo, s_lo, s_hi)


def _split3(x):
    a = x.astype(BF16).astype(F32)
    r = x - a
    b = r.astype(BF16).astype(F32)
    return a, b, (r - b).astype(BF16).astype(F32)


def _aug(lane, base, terms):
    out = jnp.zeros(lane.shape, F32)
    for i, t in enumerate(terms):
        out = jnp.where(lane == base + i, t, out)
    return out


def _head_lanes(x2, h):
    return x2 if h == 0 else pltpu.roll(x2, HEAD_DIM, 1)


def _fox_prep_fwd(proj, qg, kg, fcol, *, d_model, name, comm=None):
    T = proj.shape[0]
    nhp = d_model // LANE
    tr = _row_tile(T)

    def body(*refs):
        ((q_ref, k_ref, v_ref, qg_ref, kg_ref, f_ref), src_refs,
         (qa_ref, qta_ref, ka_ref, kta_ref, va_ref, vta_ref), dst_refs, _, sems) = _split_comm_refs(refs, 6, 6, 0, comm)
        if comm:
            @pl.when((pl.program_id(0) == 0) & (pl.program_id(1) == 0))
            def _():
                _comm_start(_comm_copies(comm.plan, src_refs, dst_refs, *sems))
        lane = lax.broadcasted_iota(jnp.int32, (tr, LANE), 1)
        lo = lane < HEAD_DIM

        def norm(xv, g):
            ms = _half_sums(xv * xv, lo) * (1.0 / HEAD_DIM)
            return (xv * lax.rsqrt(ms + EPS)) * g

        qn = norm(q_ref[...], qg_ref[...]) * (HEAD_DIM ** -0.5)
        kn = norm(k_ref[...], kg_ref[...])
        vv = v_ref[...]
        qa, ka, va, vta = [], [], [], []
        for h in range(2):
            f1, f2, f3 = _split3(f_ref[h])
            qa.append(jnp.where(lo, _head_lanes(qn, h), _aug(lane, AUG_F, [f1, f2, f3, 1.0, 1.0, 1.0])))
            ka.append(jnp.where(lo, _head_lanes(kn, h),
                                _aug(lane, AUG_F, [1.0, 1.0, 1.0, -f1, -f2, -f3, 1.0, 1.0, 1.0])))
            va.append(jnp.where(lo if h == 0 else jnp.logical_not(lo), vv, 0.0))
            vta.append(jnp.where(lo, _head_lanes(vv, h), _aug(lane, AUG_F, [1.0, 1.0, 1.0])))
        for parts, ref, tref in ((qa, qa_ref, qta_ref), (ka, ka_ref, kta_ref), (va, va_ref, None),
                                 (vta, None, vta_ref)):
            both = jnp.concatenate(parts, axis=1)
            if ref is not None:
                ref[...] = both.astype(BF16)
            if tref is not None:
                tref[...] = both.astype(BF16).T
        if comm:
            @pl.when((pl.program_id(0) == T // tr - 1) & (pl.program_id(1) == nhp - 1))
            def _():
                _comm_wait(_comm_copies(comm.plan, src_refs, dst_refs, *sems))

    gain = pl.BlockSpec((1, LANE), lambda i, h: (0, 0))
    rows = pl.BlockSpec((tr, 2 * LANE), lambda i, h: (i, h))
    cols = pl.BlockSpec((2 * LANE, tr), lambda i, h: (h, i))
    wide, tall = jax.ShapeDtypeStruct((T, 2 * d_model), BF16), jax.ShapeDtypeStruct((2 * d_model, T), BF16)
    outs = pl.pallas_call(
        body, name=name, grid=(T // tr, nhp),
        in_specs=[pl.BlockSpec((tr, LANE), lambda i, h: (i, h)),
                  pl.BlockSpec((tr, LANE), lambda i, h: (i, nhp + h)),
                  pl.BlockSpec((tr, LANE), lambda i, h: (i, 2 * nhp + h)), gain, gain,
                  pl.BlockSpec((2, tr, 1), lambda i, h: (h, i, 0))] + ([ANY] * len(comm.srcs) if comm else []),
        out_specs=[rows, cols, rows, cols, rows, cols] + ([ANY] * len(comm.out_shapes) if comm else []),
        out_shape=[wide, tall, wide, tall, wide, tall] + (list(comm.out_shapes) if comm else []),
        scratch_shapes=comm.scratch() if comm else [],
        compiler_params=_cparams("arbitrary", "arbitrary") if comm else _cparams("parallel", "parallel"),
    )(proj, proj, proj, qg, kg, fcol, *(comm.srcs if comm else []))
    return outs[:6], outs[6:]


def _fox_do_prep(do, o, *, name):
    T, D = do.shape
    nhp = D // LANE
    tr = _row_tile(T)

    def body(do_ref, o_ref, doa_ref, dota_ref):
        lane = lax.broadcasted_iota(jnp.int32, (tr, LANE), 1)
        lo = lane < HEAD_DIM
        dob = do_ref[...].astype(BF16).astype(F32)
        deltas = _half_cols(dob * o_ref[...], lo)
        both = jnp.concatenate(
            [jnp.where(lo, _head_lanes(dob, h), _aug(lane, AUG_F, _split3(-deltas[h]))) for h in range(2)], axis=1)
        doa_ref[...] = both.astype(BF16)
        dota_ref[...] = both.astype(BF16).T

    blk = pl.BlockSpec((tr, LANE), lambda i, h: (i, h))
    return pl.pallas_call(
        body, name=name, grid=(T // tr, nhp), in_specs=[blk, blk],
        out_specs=[pl.BlockSpec((tr, 2 * LANE), lambda i, h: (i, h)),
                   pl.BlockSpec((2 * LANE, tr), lambda i, h: (h, i))],
        out_shape=[jax.ShapeDtypeStruct((T, 2 * D), BF16), jax.ShapeDtypeStruct((2 * D, T), BF16)],
        compiler_params=_cparams("parallel", "parallel"),
    )(do, o)


def _fox_prep_bwd(proj, dq, dkt, dvt, qg, kg, *, d_model, name):
    T = proj.shape[0]
    nhp = d_model // LANE
    tr = _row_tile(T)

    def body(q_ref, k_ref, dq_ref, dkt_ref, dvt_ref, qg_ref, kg_ref, dqo_ref, dko_ref, dvo_ref, sums_ref):
        first = (pl.program_id(0) == 0) & (pl.program_id(1) == 0)
        lo = lax.broadcasted_iota(jnp.int32, (tr, LANE), 1) < HEAD_DIM

        def pair(x2):
            return jnp.where(lo, x2[:, :LANE], pltpu.roll(x2[:, LANE:], HEAD_DIM, 1))

        def bwd(xv, dxhat, g):
            ms = _half_sums(xv * xv, lo) * (1.0 / HEAD_DIM)
            r = lax.rsqrt(ms + EPS)
            n = xv * r
            dn = dxhat * g
            dx = r * (dn - n * (_half_sums(dn * n, lo) * (1.0 / HEAD_DIM)))
            dg = jnp.sum(dxhat * n, axis=0, keepdims=True)
            return dx, dg + pltpu.roll(dg, HEAD_DIM, 1)

        dxq, dgq = bwd(q_ref[...], pair(dq_ref[...]) * (HEAD_DIM ** -0.5), qg_ref[...])
        dxk, dgk = bwd(k_ref[...], pair(dkt_ref[...].T), kg_ref[...])
        dqo_ref[...] = dxq.astype(BF16)
        dko_ref[...] = dxk.astype(BF16)
        dvo_ref[...] = pair(dvt_ref[...].T.astype(F32)).astype(BF16)
        part = jnp.concatenate([dgq, dgk, jnp.zeros((6, LANE), F32)], axis=0)

        @pl.when(first)
        def _():
            sums_ref[...] = part

        @pl.when(jnp.logical_not(first))
        def _():
            sums_ref[...] += part

    gain = pl.BlockSpec((1, LANE), lambda i, h: (0, 0))
    blk = pl.BlockSpec((tr, LANE), lambda i, h: (i, h))
    tall = pl.BlockSpec((2 * LANE, tr), lambda i, h: (h, i))
    return pl.pallas_call(
        body, name=name, grid=(T // tr, nhp),
        in_specs=[blk, pl.BlockSpec((tr, LANE), lambda i, h: (i, nhp + h)),
                  pl.BlockSpec((tr, 2 * LANE), lambda i, h: (i, h)), tall, tall, gain, gain],
        out_specs=[blk, blk, blk, pl.BlockSpec((8, LANE), lambda i, h: (0, 0))],
        out_shape=[jax.ShapeDtypeStruct((T, d_model), BF16)] * 3 + [jax.ShapeDtypeStruct((8, LANE), F32)],
        compiler_params=_cparams("arbitrary", "arbitrary"),
    )(proj, proj, dq, dkt, dvt, qg, kg)


def _scan_lanes(x, reverse):
    n = x.shape[-1]
    lane = lax.broadcasted_iota(jnp.int32, x.shape, 1)
    sh = 1
    while sh < n:
        if reverse:
            x = x + jnp.where(lane < n - sh, pltpu.roll(x, n - sh, 1), 0.0)
        else:
            x = x + jnp.where(lane >= sh, pltpu.roll(x, sh, 1), 0.0)
        sh *= 2
    return x


def _fox_gate_fwd(fpre_t, bf, *, name):
    def body(f_ref, b_ref, o_ref):
        xv = f_ref[...] + b_ref[...]
        logf = jnp.minimum(xv, 0.0) - jnp.log1p(jnp.exp(-jnp.abs(xv)))
        o_ref[...] = _scan_lanes(logf, reverse=False)

    return pl.pallas_call(body, name=name, out_shape=jax.ShapeDtypeStruct(fpre_t.shape, F32))(fpre_t, bf)


def _fox_gate_bwd(dcol, drow, fpre_t, bf, *, name):
    H = fpre_t.shape[0]

    def body(dc_ref, dr_ref, f_ref, b_ref, o_ref, db_ref):
        xv = f_ref[...] + b_ref[...]
        e = dc_ref[...] - dr_ref[...]
        dlogf = _scan_lanes(e, reverse=False) - e
        dpre = dlogf * (1.0 - jax.nn.sigmoid(xv))
        o_ref[...] = dpre
        db_ref[...] = jnp.broadcast_to(jnp.sum(dpre, axis=-1, keepdims=True), (H, LANE))

    return pl.pallas_call(
        body, name=name,
        out_shape=[jax.ShapeDtypeStruct(fpre_t.shape, F32), jax.ShapeDtypeStruct((H, LANE), F32)],
    )(dcol, drow, fpre_t, bf)


_NT = (((1,), (1,)), ((), ()))
_TN = (((0,), (0,)), ((), ()))
_NN = (((1,), (0,)), ((), ()))


def _attn_tile(T):
    return min(T, 1024)


def _causal(tq, tk):
    return lax.broadcasted_iota(jnp.int32, (tq, tk), 1) <= lax.broadcasted_iota(jnp.int32, (tq, tk), 0)


def _fox_attn_fwd(qa, kta, va, *, name, comm=None):
    T = qa.shape[0]
    nhp = qa.shape[1] // (2 * LANE)
    tq = tk = _attn_tile(T)
    nq = T // tq

    def body(*refs):
        (qa_ref, kta_ref, va_ref), src_refs, (o_ref, qb_ref), dst_refs, (m_sc, l_sc, acc_sc), sems = (
            _split_comm_refs(refs, 3, 2, 3, comm))
        hp, i, j = pl.program_id(0), pl.program_id(1), pl.program_id(2)
        if comm:
            @pl.when((hp == 0) & (i == 0) & (j == 0))
            def _():
                _comm_start(_comm_copies(comm.plan, src_refs, dst_refs, *sems))

        @pl.when(j == 0)
        def _():
            m_sc[...] = jnp.full(m_sc.shape, NEG, F32)
            l_sc[...] = jnp.zeros(l_sc.shape, F32)
            acc_sc[...] = jnp.zeros(acc_sc.shape, F32)

        def block(diagonal):
            heads = [slice(h * LANE, (h + 1) * LANE) for h in range(2)]
            scores = [lax.dot_general(qa_ref[:, hs], kta_ref[hs, :], _NN, preferred_element_type=F32)
                      for hs in heads]
            state = [(m_sc[h], l_sc[h], acc_sc[h]) for h in range(2)]
            probs, updates = [], []
            for s, (m_prev, l_prev, _) in zip(scores, state):
                if diagonal:
                    s = jnp.where(_causal(tq, tk), s, NEG)
                m_next = jnp.maximum(m_prev, jnp.max(s, axis=1, keepdims=True))
                p = jnp.exp(s - jnp.tile(m_next, (1, tk // LANE)))
                alpha = jnp.exp(m_prev - m_next)
                probs.append(p.astype(BF16))
                updates.append((m_next, alpha, alpha * l_prev + jnp.sum(p, axis=1, keepdims=True)))
            pvs = [lax.dot_general(p, va_ref[:, hs], _NN, preferred_element_type=F32)
                   for p, hs in zip(probs, heads)]
            for h in range(2):
                m_next, alpha, l_next = updates[h]
                m_sc[h] = m_next
                l_sc[h] = l_next
                acc_sc[h] = alpha * state[h][2] + pvs[h]

        @pl.when(j < i)
        def _():
            block(False)

        @pl.when(j == i)
        def _():
            block(True)
            o_ref[...] = acc_sc[0] / l_sc[0] + acc_sc[1] / l_sc[1]
            lane = lax.broadcasted_iota(jnp.int32, (tq, LANE), 1)
            for h in range(2):
                hs = slice(h * LANE, (h + 1) * LANE)
                pieces = _split3(-(m_sc[h] + jnp.log(l_sc[h])))
                qb = qa_ref[:, hs].astype(F32)
                for n, piece in enumerate(pieces):
                    qb = jnp.where(lane == AUG_LSE + n, piece, qb)
                qb_ref[:, hs] = qb.astype(BF16)

        if comm:
            @pl.when((hp == nhp - 1) & (i == nq - 1) & (j == nq - 1))
            def _():
                _comm_wait(_comm_copies(comm.plan, src_refs, dst_refs, *sems))

    outs = pl.pallas_call(
        body, name=name, grid=(nhp, nq, nq),
        in_specs=[pl.BlockSpec((tq, 2 * LANE), lambda h, i, j: (i, h)),
                  pl.BlockSpec((2 * LANE, tk), lambda h, i, j: (h, jnp.minimum(j, i))),
                  pl.BlockSpec((tk, 2 * LANE), lambda h, i, j: (jnp.minimum(j, i), h))]
        + ([ANY] * len(comm.srcs) if comm else []),
        out_specs=[pl.BlockSpec((tq, LANE), lambda h, i, j: (i, h)),
                   pl.BlockSpec((tq, 2 * LANE), lambda h, i, j: (i, h))]
        + ([ANY] * len(comm.out_shapes) if comm else []),
        out_shape=[jax.ShapeDtypeStruct((T, nhp * LANE), F32), jax.ShapeDtypeStruct(qa.shape, BF16)]
        + (list(comm.out_shapes) if comm else []),
        scratch_shapes=[pltpu.VMEM((2, tq, LANE), F32), pltpu.VMEM((2, tq, LANE), F32),
                        pltpu.VMEM((2, tq, LANE), F32)] + (comm.scratch() if comm else []),
        compiler_params=(_cparams("arbitrary", "arbitrary", "arbitrary") if comm
                         else _cparams("parallel", "parallel", "arbitrary")),
    )(qa, kta, va, *(comm.srcs if comm else []))
    return outs[0], outs[1], outs[2:]


def _fox_attn_bwd(qb, qta, ka, kta, vta, doa, dota, *, name, comm=None):
    T = qb.shape[0]
    nhp = qb.shape[1] // (2 * LANE)
    tq = tk = _attn_tile(T)
    nq = T // tq

    def body(*refs):
        ((qb_ref, qta_ref, ka_ref, kta_ref, vta_ref, doa_ref, dota_ref), src_refs,
         (dq_ref, dkt_ref, dvt_ref, dcol_ref, drow_ref), dst_refs, (dkt_sc, dvt_sc, dcol_sc), sems) = (
            _split_comm_refs(refs, 7, 5, 3, comm))
        hp, j, i = pl.program_id(0), pl.program_id(1), pl.program_id(2)
        if comm:
            @pl.when((hp == 0) & (j == 0) & (i == 0))
            def _():
                _comm_start(_comm_copies(comm.plan, src_refs, dst_refs, *sems))

        @pl.when((j == 0) & (i == 0))
        def _():
            dq_ref[...] = jnp.zeros(dq_ref.shape, F32)
            drow_ref[...] = jnp.zeros(drow_ref.shape, F32)

        @pl.when(i == 0)
        def _():
            dkt_sc[...] = jnp.zeros(dkt_sc.shape, F32)
            dvt_sc[...] = jnp.zeros(dvt_sc.shape, F32)
            dcol_sc[...] = jnp.zeros(dcol_sc.shape, F32)

        def block(diagonal):
            rows = pl.ds(pl.multiple_of(i * tq, tq), tq)
            heads = [slice(h * LANE, (h + 1) * LANE) for h in range(2)]
            logits = [lax.dot_general(qb_ref[:, hs], kta_ref[hs, :], _NN, preferred_element_type=F32)
                      for hs in heads]
            dpds = [lax.dot_general(doa_ref[:, hs], vta_ref[hs, :], _NN, preferred_element_type=F32)
                    for hs in heads]
            pbs, dlbs = [], []
            for h in range(2):
                p = jnp.exp(logits[h])
                if diagonal:
                    p = jnp.where(_causal(tq, tk), p, 0.0)
                dl = p * dpds[h]
                pbs.append(p.astype(BF16))
                dlbs.append(dl.astype(BF16))
                dcol_sc[h] += jnp.sum(dl, axis=0, keepdims=True)
                drow_ref[h, rows, :] += jnp.sum(dl, axis=1, keepdims=True)
            for h, hs in enumerate(heads):
                dvt_sc[h] += lax.dot_general(dota_ref[hs, :], pbs[h], _NN, preferred_element_type=F32)
                dkt_sc[h] += lax.dot_general(qta_ref[hs, :], dlbs[h], _NN, preferred_element_type=F32)
                dq_ref[rows, hs] += lax.dot_general(dlbs[h], ka_ref[:, hs], _NN, preferred_element_type=F32)

        @pl.when(i > j)
        def _():
            block(False)

        @pl.when(i == j)
        def _():
            block(True)

        @pl.when(i == nq - 1)
        def _():
            dkt_ref[...] = jnp.concatenate([dkt_sc[0], dkt_sc[1]], axis=0)
            dvt_ref[...] = jnp.concatenate([dvt_sc[0], dvt_sc[1]], axis=0).astype(BF16)
            dcol_ref[...] = dcol_sc[...]

        if comm:
            @pl.when((hp == nhp - 1) & (j == nq - 1) & (i == nq - 1))
            def _():
                _comm_wait(_comm_copies(comm.plan, src_refs, dst_refs, *sems))

    qrow = pl.BlockSpec((tq, 2 * LANE), lambda h, j, i: (jnp.maximum(i, j), h))
    qcol = pl.BlockSpec((2 * LANE, tq), lambda h, j, i: (h, jnp.maximum(i, j)))
    krow = pl.BlockSpec((tk, 2 * LANE), lambda h, j, i: (j, h))
    kcol = pl.BlockSpec((2 * LANE, tk), lambda h, j, i: (h, j))
    tall = jax.ShapeDtypeStruct((qb.shape[1], T), F32)
    outs = pl.pallas_call(
        body, name=name, grid=(nhp, nq, nq),
        in_specs=[qrow, qcol, krow, kcol, kcol, qrow, qcol] + ([ANY] * len(comm.srcs) if comm else []),
        out_specs=[pl.BlockSpec((T, 2 * LANE), lambda h, j, i: (0, h)), kcol, kcol,
                   pl.BlockSpec((2, 1, tk), lambda h, j, i: (h, 0, j)),
                   pl.BlockSpec((2, T, 1), lambda h, j, i: (h, 0, 0))]
        + ([ANY] * len(comm.out_shapes) if comm else []),
        out_shape=[jax.ShapeDtypeStruct(qb.shape, F32), tall, jax.ShapeDtypeStruct(tall.shape, BF16),
                   jax.ShapeDtypeStruct((2 * nhp, 1, T), F32), jax.ShapeDtypeStruct((2 * nhp, T, 1), F32)]
        + (list(comm.out_shapes) if comm else []),
        scratch_shapes=[pltpu.VMEM((2, LANE, tk), F32), pltpu.VMEM((2, LANE, tk), F32),
                        pltpu.VMEM((2, 1, tk), F32)] + (comm.scratch() if comm else []),
        compiler_params=_cparams("arbitrary" if comm else "parallel", "arbitrary", "arbitrary"),
    )(qb, qta, ka, kta, vta, doa, dota, *(comm.srcs if comm else []))
    return (*outs[:5], outs[5:])


_GELU_C = math.sqrt(2.0 / math.pi)
_GELU_A = 0.044715


def _gelu(x):
    t = jnp.tanh(_GELU_C * (x + _GELU_A * (x * x * x)))
    return x * (0.5 * (1.0 + t)), t


def _gelu_grad(x, t):
    return 0.5 * (1.0 + t) + 0.5 * x * (1.0 - t * t) * (_GELU_C * (1.0 + 3.0 * _GELU_A * x * x))


def _layer_norm_stats(v):
    mu = jnp.mean(v, axis=-1, keepdims=True)
    vc = v - mu
    rstd = lax.rsqrt(jnp.mean(vc * vc, axis=-1, keepdims=True) + EPS)
    return vc * rstd, rstd


def _layer_norm_bwd(dyhat, yhat, rstd):
    return rstd * (dyhat - jnp.mean(dyhat, axis=-1, keepdims=True)
                   - yhat * jnp.mean(dyhat * yhat, axis=-1, keepdims=True))


def _sg_mask():
    t = lax.broadcasted_iota(jnp.int32, (SG_CHUNK, SG_CHUNK), 0) // SG_CAUSAL
    s = lax.broadcasted_iota(jnp.int32, (SG_CHUNK, SG_CHUNK), 1) // SG_CAUSAL
    return s <= t


def _sg_mix(ws_ref, bc_ref, vln_sc, vo_sc, tr, gd):
    mask = _sg_mask()
    for g in range(SG_GROUPS):
        wg = jnp.where(mask, ws_ref[g], 0.0).astype(BF16)
        cols = slice(g * gd, (g + 1) * gd)
        for n in range(tr // SG_CHUNK):
            rows = slice(n * SG_CHUNK, (n + 1) * SG_CHUNK)
            vo_sc[rows, cols] = lax.dot_general(wg, vln_sc[rows, cols], _NN,
                                                preferred_element_type=F32) + bc_ref[g]


def _sg_fwd(a_uv, ln_g, ln_b, ws, bcol, *, name):
    T, W = a_uv.shape[0], a_uv.shape[1] // 2
    gd = W // SG_GROUPS
    tr = _row_tile(T)

    def body(u_ref, v_ref, g_ref, b_ref, ws_ref, bc_ref, o_ref, vln_sc, vo_sc):
        u, _ = _gelu(u_ref[...])
        v, _ = _gelu(v_ref[...])
        vhat, _ = _layer_norm_stats(v)
        vln_sc[...] = (vhat * g_ref[...] + b_ref[...]).astype(BF16)
        _sg_mix(ws_ref, bc_ref, vln_sc, vo_sc, tr, gd)
        o_ref[...] = (u * vo_sc[...]).astype(BF16)

    row = pl.BlockSpec((1, W), lambda i: (0, 0))
    return pl.pallas_call(
        body, name=name, grid=(T // tr,),
        in_specs=[pl.BlockSpec((tr, W), lambda i: (i, 0)), pl.BlockSpec((tr, W), lambda i: (i, 1)), row, row,
                  pl.BlockSpec((SG_GROUPS, SG_CHUNK, SG_CHUNK), lambda i: (0, 0, 0)),
                  pl.BlockSpec((SG_GROUPS, SG_CHUNK, 1), lambda i: (0, 0, 0))],
        out_specs=pl.BlockSpec((tr, W), lambda i: (i, 0)),
        out_shape=jax.ShapeDtypeStruct((T, W), BF16),
        scratch_shapes=[pltpu.VMEM((tr, W), BF16), pltpu.VMEM((tr, W), F32)],
        compiler_params=_cparams("parallel"),
    )(a_uv, a_uv, ln_g, ln_b, ws, bcol)


def _sg_bwd(a_uv, dgate, ln_g, ln_b, ws, bcol, *, name):
    T, W = a_uv.shape[0], a_uv.shape[1] // 2
    gd = W // SG_GROUPS
    tr = _row_tile(T)

    def body(u_ref, v_ref, dg_ref, g_ref, b_ref, ws_ref, bc_ref,
             da_ref, dws_ref, dbs_ref, sums_ref, vln_sc, vo_sc, dvo_sc, dvln_sc):
        i = pl.program_id(0)

        @pl.when(i == 0)
        def _():
            dws_ref[...] = jnp.zeros(dws_ref.shape, F32)
            dbs_ref[...] = jnp.zeros(dbs_ref.shape, F32)
            sums_ref[...] = jnp.zeros(sums_ref.shape, F32)

        ua, va = u_ref[...], v_ref[...]
        u, tu = _gelu(ua)
        v, tv = _gelu(va)
        vhat, rstd = _layer_norm_stats(v)
        vln_sc[...] = (vhat * g_ref[...] + b_ref[...]).astype(BF16)
        _sg_mix(ws_ref, bc_ref, vln_sc, vo_sc, tr, gd)
        dgt = dg_ref[...]
        du = dgt * vo_sc[...]
        dvo_sc[...] = dgt * u
        mask = _sg_mask()
        for g in range(SG_GROUPS):
            wg = jnp.where(mask, ws_ref[g], 0.0).astype(BF16)
            cols = slice(g * gd, (g + 1) * gd)
            acc_w = jnp.zeros((SG_CHUNK, SG_CHUNK), F32)
            acc_b = jnp.zeros((SG_CHUNK, 1), F32)
            for n in range(tr // SG_CHUNK):
                rows = slice(n * SG_CHUNK, (n + 1) * SG_CHUNK)
                dvo = dvo_sc[rows, cols]
                dvob = dvo.astype(BF16)
                dvln_sc[rows, cols] = lax.dot_general(wg, dvob, _TN, preferred_element_type=F32)
                acc_w += lax.dot_general(dvob, vln_sc[rows, cols], _NT, preferred_element_type=F32)
                acc_b += jnp.sum(dvo, axis=1, keepdims=True)
            dws_ref[g] += jnp.where(mask, acc_w, 0.0)
            dbs_ref[g] += acc_b
        dvln = dvln_sc[...]
        sums_ref[...] += jnp.concatenate([jnp.sum(dvln * vhat, axis=0, keepdims=True),
                                          jnp.sum(dvln, axis=0, keepdims=True),
                                          jnp.zeros((6, W), F32)], axis=0)
        dv = _layer_norm_bwd(dvln * g_ref[...], vhat, rstd)
        da_ref[:, :W] = (du * _gelu_grad(ua, tu)).astype(BF16)
        da_ref[:, W:] = (dv * _gelu_grad(va, tv)).astype(BF16)

    row = pl.BlockSpec((1, W), lambda i: (0, 0))
    wspec = pl.BlockSpec((SG_GROUPS, SG_CHUNK, SG_CHUNK), lambda i: (0, 0, 0))
    bspec = pl.BlockSpec((SG_GROUPS, SG_CHUNK, 1), lambda i: (0, 0, 0))
    return pl.pallas_call(
        body, name=name, grid=(T // tr,),
        in_specs=[pl.BlockSpec((tr, W), lambda i: (i, 0)), pl.BlockSpec((tr, W), lambda i: (i, 1)),
                  pl.BlockSpec((tr, W), lambda i: (i, 0)), row, row, wspec, bspec],
        out_specs=[pl.BlockSpec((tr, 2 * W), lambda i: (i, 0)), wspec, bspec,
                   pl.BlockSpec((8, W), lambda i: (0, 0))],
        out_shape=[jax.ShapeDtypeStruct((T, 2 * W), BF16),
                   jax.ShapeDtypeStruct((SG_GROUPS, SG_CHUNK, SG_CHUNK), F32),
                   jax.ShapeDtypeStruct((SG_GROUPS, SG_CHUNK, 1), F32),
                   jax.ShapeDtypeStruct((8, W), F32)],
        scratch_shapes=[pltpu.VMEM((tr, W), BF16), pltpu.VMEM((tr, W), F32),
                        pltpu.VMEM((tr, W), F32), pltpu.VMEM((tr, W), F32)],
        compiler_params=_cparams("arbitrary"),
    )(a_uv, a_uv, dgate, ln_g, ln_b, ws, bcol)


SUBLANES = 8


def _shift_rows(xc_sc, xs_sc):
    rows = xs_sc.shape[1]
    for p in range(1, SUBLANES):
        xs_sc[p - 1] = xc_sc[pl.ds(p, rows), :]


def _rows_at(xc_sc, xs_sc, offset, tr):
    p = offset % SUBLANES
    base = offset - p
    return xc_sc[pl.ds(base, tr), :] if p == 0 else xs_sc[p - 1, pl.ds(base, tr), :]


def _shift_scratch(tr, C):
    return pltpu.VMEM((SUBLANES - 1, tr + CONV_HALO - SUBLANES, C), F32)


def _cv_glu_conv(a_ref, b_ref, ap_ref, bp_ref, w_ref, bd_ref, xc_sc, xs_sc, tr):
    i = pl.program_id(0)
    prev = ap_ref[...] * jax.nn.sigmoid(bp_ref[...])
    xc_sc[0:CONV_HALO, :] = jnp.where(i > 0, prev, 0.0)
    xc_sc[CONV_HALO:, :] = a_ref[...] * jax.nn.sigmoid(b_ref[...])
    _shift_rows(xc_sc, xs_sc)
    acc = jnp.broadcast_to(bd_ref[...], (tr, bd_ref.shape[1]))
    for k in range(CONV_WIDTH):
        acc = acc + w_ref[k:k + 1, :] * _rows_at(xc_sc, xs_sc, CONV_HALO - (CONV_WIDTH - 1) + k, tr)
    return acc


def _cv_specs(T, C, tr):
    hb = tr // CONV_HALO
    cur = lambda col: pl.BlockSpec((tr, C), lambda i: (i, col))
    prev = lambda col: pl.BlockSpec((CONV_HALO, C), lambda i: (jnp.maximum(i * hb - 1, 0), col))
    row = pl.BlockSpec((1, C), lambda i: (0, 0))
    wspec = pl.BlockSpec((CONV_HALO, C), lambda i: (0, 0))
    return cur, prev, row, wspec


def _cv_fwd(p, w_dw, b_dw, ln_g, ln_b, *, name):
    T, C = p.shape[0], p.shape[1] // 2
    tr = _row_tile(T)
    cur, prev, row, wspec = _cv_specs(T, C, tr)

    def body(a_ref, b_ref, ap_ref, bp_ref, w_ref, bd_ref, g_ref, be_ref, o_ref, xc_sc, xs_sc):
        y2 = _cv_glu_conv(a_ref, b_ref, ap_ref, bp_ref, w_ref, bd_ref, xc_sc, xs_sc, tr)
        yhat, _ = _layer_norm_stats(y2)
        yln = yhat * g_ref[...] + be_ref[...]
        o_ref[...] = (yln * jax.nn.sigmoid(yln)).astype(BF16)

    return pl.pallas_call(
        body, name=name, grid=(T // tr,),
        in_specs=[cur(0), cur(1), prev(0), prev(1), wspec, row, row, row],
        out_specs=pl.BlockSpec((tr, C), lambda i: (i, 0)),
        out_shape=jax.ShapeDtypeStruct((T, C), BF16),
        scratch_shapes=[pltpu.VMEM((tr + CONV_HALO, C), F32), _shift_scratch(tr, C)],
        compiler_params=_cparams("parallel"),
    )(p, p, p, p, w_dw, b_dw, ln_g, ln_b)


def _cv_bwd_ln(p, dy3, w_dw, b_dw, ln_g, ln_b, *, name):
    T, C = p.shape[0], p.shape[1] // 2
    tr = _row_tile(T)
    cur, prev, row, wspec = _cv_specs(T, C, tr)

    def body(a_ref, b_ref, ap_ref, bp_ref, dy_ref, w_ref, bd_ref, g_ref, be_ref,
             dy2_ref, dw_ref, sums_ref, xc_sc, xs_sc):
        i = pl.program_id(0)
        y2 = _cv_glu_conv(a_ref, b_ref, ap_ref, bp_ref, w_ref, bd_ref, xc_sc, xs_sc, tr)
        yhat, rstd = _layer_norm_stats(y2)
        yln = yhat * g_ref[...] + be_ref[...]
        s = jax.nn.sigmoid(yln)
        dyln = dy_ref[...] * (s + yln * s * (1.0 - s))
        dy2 = _layer_norm_bwd(dyln * g_ref[...], yhat, rstd)
        dy2_ref[...] = dy2
        sums = jnp.concatenate([jnp.sum(dy2, axis=0, keepdims=True),
                                jnp.sum(dyln * yhat, axis=0, keepdims=True),
                                jnp.sum(dyln, axis=0, keepdims=True),
                                jnp.zeros((5, C), F32)], axis=0)
        taps = [jnp.sum(dy2 * _rows_at(xc_sc, xs_sc, CONV_HALO - (CONV_WIDTH - 1) + k, tr), axis=0, keepdims=True)
                for k in range(CONV_WIDTH)]
        dw = jnp.concatenate(taps + [jnp.zeros((CONV_HALO - CONV_WIDTH, C), F32)], axis=0)

        @pl.when(i == 0)
        def _():
            sums_ref[...] = sums
            dw_ref[...] = dw

        @pl.when(i > 0)
        def _():
            sums_ref[...] += sums
            dw_ref[...] += dw

    blk = pl.BlockSpec((tr, C), lambda i: (i, 0))
    return pl.pallas_call(
        body, name=name, grid=(T // tr,),
        in_specs=[cur(0), cur(1), prev(0), prev(1), blk, wspec, row, row, row],
        out_specs=[blk, wspec, pl.BlockSpec((8, C), lambda i: (0, 0))],
        out_shape=[jax.ShapeDtypeStruct((T, C), F32), jax.ShapeDtypeStruct((CONV_HALO, C), F32),
                   jax.ShapeDtypeStruct((8, C), F32)],
        scratch_shapes=[pltpu.VMEM((tr + CONV_HALO, C), F32), _shift_scratch(tr, C)],
        compiler_params=_cparams("arbitrary"),
    )(p, p, p, p, dy3, w_dw, b_dw, ln_g, ln_b)


def _cv_bwd_in(p, dy2, w_dw, *, name):
    T, C = p.shape[0], p.shape[1] // 2
    tr = _row_tile(T)
    hb = tr // CONV_HALO
    nblk = T // tr
    last_halo = T // CONV_HALO - 1

    def body(a_ref, b_ref, dy_ref, dyn_ref, w_ref, dp_ref, sums_ref, xc_sc, xs_sc):
        i = pl.program_id(0)
        xc_sc[0:tr, :] = dy_ref[...]
        xc_sc[tr:, :] = jnp.where(i < nblk - 1, dyn_ref[...], 0.0)
        _shift_rows(xc_sc, xs_sc)
        dy1 = jnp.zeros((tr, C), F32)
        for k in range(CONV_WIDTH):
            dy1 = dy1 + w_ref[k:k + 1, :] * _rows_at(xc_sc, xs_sc, CONV_WIDTH - 1 - k, tr)
        a = a_ref[...]
        sb = jax.nn.sigmoid(b_ref[...])
        da = dy1 * sb
        db = dy1 * a * sb * (1.0 - sb)
        dp_ref[:, :C] = da.astype(BF16)
        dp_ref[:, C:] = db.astype(BF16)
        sums = jnp.concatenate([
            jnp.concatenate([jnp.sum(da, axis=0, keepdims=True), jnp.sum(db, axis=0, keepdims=True)], axis=1),
            jnp.zeros((7, 2 * C), F32)], axis=0)

        @pl.when(i == 0)
        def _():
            sums_ref[...] = sums

        @pl.when(i > 0)
        def _():
            sums_ref[...] += sums

    blk = lambda col: pl.BlockSpec((tr, C), lambda i: (i, col))
    return pl.pallas_call(
        body, name=name, grid=(nblk,),
        in_specs=[blk(0), blk(1), blk(0),
                  pl.BlockSpec((CONV_HALO, C), lambda i: (jnp.minimum((i + 1) * hb, last_halo), 0)),
                  pl.BlockSpec((CONV_HALO, C), lambda i: (0, 0))],
        out_specs=[pl.BlockSpec((tr, 2 * C), lambda i: (i, 0)), pl.BlockSpec((8, 2 * C), lambda i: (0, 0))],
        out_shape=[jax.ShapeDtypeStruct((T, 2 * C), BF16), jax.ShapeDtypeStruct((8, 2 * C), F32)],
        scratch_shapes=[pltpu.VMEM((tr + CONV_HALO, C), F32), _shift_scratch(tr, C)],
        compiler_params=_cparams("arbitrary"),
    )(p, p, dy2, dy2, w_dw)


def _col_tile(n, want=1024):
    best = LANE
    for t in range(LANE, min(n, want) + 1, LANE):
        if n % t == 0:
            best = t
    return best if n % LANE == 0 else n


def _mm(a, b, *, name, ta=False, tb=False, **kw):
    M = a.shape[1] if ta else a.shape[0]
    N = b.shape[0] if tb else b.shape[1]
    K = a.shape[0] if ta else a.shape[1]
    kw.setdefault('tm', _col_tile(M, 1024 if ta else 512))
    kw.setdefault('tn', _col_tile(N, 1024))
    kw.setdefault('tk', K if tb else _col_tile(K, 2048 if ta else 1024))
    return _matmul(a, b, name=name, ta=ta, tb=tb, **kw)


WIDE_ROWS = 1024


def _relu2_epilogue(acc):
    r = jnp.maximum(acc, 0.0)
    return (r * r,)


def _residual_epilogue(acc, x, g):
    return acc, x + g * acc


def _residual_bias_epilogue(acc, x, g, b):
    y = acc + b
    return y, x + g * y


def _dh_norm_bwd(d_act, w, x, dres, nw, sc, *, name, gate=None):
    D = x.shape[1]

    def epilogue(dh, xv, dresv, wv, scv, *gated):
        r = lax.rsqrt(jnp.mean(xv * xv, axis=-1, keepdims=True) + EPS)
        n = xv * r
        scale = 1.0 + scv
        dn = dh * (wv * scale)
        dx = dresv + r * (dn - n * jnp.mean(dn * n, axis=-1, keepdims=True))
        rows = [jnp.sum(dh, axis=0, keepdims=True),
                jnp.sum(dh * (n * wv), axis=0, keepdims=True),
                jnp.sum(dh * n * scale, axis=0, keepdims=True)]
        outs = [dx]
        if gated:
            yv, gv = gated
            outs.append(dx * gv)
            rows += [jnp.sum(dx * yv, axis=0, keepdims=True), jnp.sum(dx * gv, axis=0, keepdims=True)]
        return (*outs, jnp.concatenate(rows + [jnp.zeros((8 - len(rows), D), F32)], axis=0))

    extras = [(x, 'tile'), (dres, 'tile'), (nw, 'row'), (sc, 'row')]
    if gate:
        extras += [(gate[0], 'tile'), (gate[1], 'row')]
    return _mm(d_act, w, tb=True, name=name, tn=D, extras=extras, epilogue=epilogue,
               out_dtypes=(F32, BF16) if gate else (F32,), row_sums=True)


def _relu2_bwd_epilogue(acc, r):
    return (acc * (2.0 * jnp.sqrt(r.astype(F32))),)


def _bias_epilogue(acc, b):
    return (acc + b,)


def _fox_forward(h1, P, j, D, carried=None):
    carried = carried or {}

    def ride(kernel):
        return carried[kernel][0] if kernel in carried else None

    def landed(kernel, outs):
        if kernel in carried:
            carried[kernel][1](outs)

    H = D // HEAD_DIM
    proj = _mm(h1, P['fox_w_in'][j], name='fox_proj', b_outer=True, tm=WIDE_ROWS)
    qg = jnp.tile(P['fox_q_norm'][j][None, :], (1, 2))
    kg = jnp.tile(P['fox_k_norm'][j][None, :], (1, 2))
    fpre_t = proj[:, 3 * D:3 * D + H].T
    bf = P['fox_b_f'][j][:, None]
    fcum = _fox_gate_fwd(fpre_t, bf, name='fox_gate_fwd')
    (qa, qta, ka, kta, va, vta), outs = _fox_prep_fwd(proj, qg, kg, fcum[:, :, None], d_model=D, name='fox_prep_fwd',
                                                     comm=ride('prep'))
    landed('prep', outs)
    o, qb, outs = _fox_attn_fwd(qa, kta, va, name='fox_attn_fwd', comm=ride('attn'))
    landed('attn', outs)
    saved = dict(proj=proj, qg=qg, kg=kg, fpre_t=fpre_t, bf=bf, o=o, qb=qb, qta=qta, ka=ka, kta=kta, vta=vta)
    return o, saved


def _fox_backward(dy, h1, S, P, j, D, comm=None):
    H = D // HEAD_DIM
    w_out, w_in = P['fox_w_out'][j], P['fox_w_in'][j]
    g = {}
    g['fox_w_out'] = _mm(S['o'], dy, ta=True, name='fox_dw_out')
    do = _mm(dy, w_out, tb=True, name='fox_do')
    doa, dota = _fox_do_prep(do, S['o'], name='fox_do_prep')
    dq, dkt, dvt, dcol, drow, comm_outs = _fox_attn_bwd(S['qb'], S['qta'], S['ka'], S['kta'], S['vta'], doa, dota,
                                                        name='fox_attn_bwd', comm=comm)
    dqp, dkp, dvp, gsum = _fox_prep_bwd(S['proj'], dq, dkt, dvt, S['qg'], S['kg'], d_model=D, name='fox_prep_bwd')
    dfpre_t, dbf = _fox_gate_bwd(dcol[:, 0, :], drow[:, :, 0], S['fpre_t'], S['bf'], name='fox_gate_bwd')
    dfpre = jnp.pad(dfpre_t.T.astype(BF16), ((0, 0), (0, LANE - H)))
    dproj = jnp.concatenate([dqp, dkp, dvp, dfpre], axis=1)
    g['fox_w_in'] = _mm(h1, dproj, ta=True, name='fox_dw_in')[:, :3 * D + H]
    g['fox_b_f'] = dbf[:, 0]
    g['fox_q_norm'] = gsum[0, :HEAD_DIM]
    g['fox_k_norm'] = gsum[1, :HEAD_DIM]
    return (dproj, w_in), g, comm_outs


def _sg_forward(h1, P, D):
    a_uv = _mm(h1, P['sg_w_in'], name='sg_in', b_outer=True, tm=WIDE_ROWS)
    bcol = P['sg_b_s'][:, :, None]
    gate = _sg_fwd(a_uv, P['sg_ln_g'], P['sg_ln_b'], P['sg_w_s'], bcol, name='sg_fwd')
    return gate, dict(a_uv=a_uv, bcol=bcol, gate=gate)


def _sg_backward(dy, h1, S, P, D):
    g = {}
    g['sg_w_out'] = _mm(S['gate'], dy, ta=True, name='sg_dw_out')
    dgate = _mm(dy, P['sg_w_out'], tb=True, name='sg_dgate')
    da, dws, dbs, sums = _sg_bwd(S['a_uv'], dgate, P['sg_ln_g'], P['sg_ln_b'], P['sg_w_s'], S['bcol'],
                                 name='sg_bwd')
    g['sg_w_s'], g['sg_b_s'] = dws, dbs[:, :, 0]
    g['sg_ln_g'], g['sg_ln_b'] = sums[0], sums[1]
    g['sg_w_in'] = _mm(h1, da, ta=True, name='sg_dw_in', out_chips=N_CHIPS)
    return (da, P['sg_w_in']), g


def _cv_forward(h1, P, D):
    p = _mm(h1, P['cv_w_pw1'], name='cv_pw1', extras=[(P['cv_b_pw1'], 'row')], epilogue=_bias_epilogue,
            b_outer=True, tm=WIDE_ROWS)
    w_dw = jnp.pad(P['cv_w_dw'], ((0, CONV_HALO - CONV_WIDTH), (0, 0)))
    y3 = _cv_fwd(p, w_dw, P['cv_b_dw'], P['cv_ln_g'], P['cv_ln_b'], name='cv_fwd')
    return y3, dict(p=p, w_dw=w_dw, y3=y3)


def _cv_backward(dy, h1, S, P, D):
    g = {}
    g['cv_w_pw2'] = _mm(S['y3'], dy, ta=True, name='cv_dw_pw2')
    dy3 = _mm(dy, P['cv_w_pw2'], tb=True, name='cv_dy3')
    dy2, dw, sums = _cv_bwd_ln(S['p'], dy3, S['w_dw'], P['cv_b_dw'], P['cv_ln_g'], P['cv_ln_b'], name='cv_bwd_ln')
    g['cv_w_dw'] = dw[:CONV_WIDTH]
    g['cv_b_dw'], g['cv_ln_g'], g['cv_ln_b'] = sums[0], sums[1], sums[2]
    dp, psum = _cv_bwd_in(S['p'], dy2, S['w_dw'], name='cv_bwd_in')
    g['cv_b_pw1'] = psum[0]
    g['cv_w_pw1'] = _mm(h1, dp, ta=True, name='cv_dw_pw1', out_chips=N_CHIPS)
    return (dp, P['cv_w_pw1']), g


class Hooks(NamedTuple):
    fwd: dict
    bwd_comm: Callable
    bwd_done: Callable


def _local_step(x, target, mod, P, hooks=None):
    T, D = x.shape
    L = mod.shape[0]
    saved = []
    for i in range(L):
        kind, j = i % N_MIXERS, i // N_MIXERS
        m = [mod[i:i + 1, k * D:(k + 1) * D] for k in range(6)]
        sh_m, sc_m, g_m, sh_f, sc_f, g_f = m
        w_mix, w_mlp = P['norm_mix'][i:i + 1], P['norm_mlp'][i:i + 1]
        h1 = _norm_mod_fwd(x, w_mix, sc_m, sh_m, name='norm_mix_fwd')
        if kind == 0:
            op, S = _fox_forward(h1, P, j, D, carried=hooks.fwd.get(i) if hooks else None)
            y, x1 = _mm(op, P['fox_w_out'][j], name='fox_out', extras=[(x, 'tile'), (g_m, 'row')],
                        epilogue=_residual_epilogue, out_dtypes=(F32, F32))
        elif kind == 1:
            op, S = _sg_forward(h1, P, D)
            y, x1 = _mm(op, P['sg_w_out'], name='sg_out', extras=[(x, 'tile'), (g_m, 'row')],
                        epilogue=_residual_epilogue, out_dtypes=(F32, F32))
        else:
            op, S = _cv_forward(h1, P, D)
            y, x1 = _mm(op, P['cv_w_pw2'], name='cv_out',
                        extras=[(x, 'tile'), (g_m, 'row'), (P['cv_b_pw2'], 'row')],
                        epilogue=_residual_bias_epilogue, out_dtypes=(F32, F32))
        h2 = _norm_mod_fwd(x1, w_mlp, sc_f, sh_f, name='norm_mlp_fwd')
        r = _mm(h2, P['w_mlp_in'][i], name='mlp_in', epilogue=_relu2_epilogue, out_dtypes=(BF16,),
                b_outer=True, tm=WIDE_ROWS)
        z, x2 = _mm(r, P['w_mlp_out'][i], name='mlp_out', extras=[(x1, 'tile'), (g_f, 'row')],
                    epilogue=_residual_epilogue, out_dtypes=(F32, F32), tk=P['w_mlp_out'][i].shape[0])
        saved.append(dict(x=x, h1=h1, S=S, y=y, x1=x1, h2=h2, r=r, z=z, m=m))
        x = x2

    loss_part, dx = _loss_head(x, target, name='loss_head')

    grads = {k: [None] * L for k in ('norm_mix', 'norm_mlp')}
    mix_grads, mat = {}, {}
    dmod = [None] * L
    for i in reversed(range(L)):
        kind, j = i % N_MIXERS, i // N_MIXERS
        sv = saved[i]
        sh_m, sc_m, g_m, sh_f, sc_f, g_f = sv['m']
        w_mix, w_mlp = P['norm_mix'][i:i + 1], P['norm_mlp'][i:i + 1]
        dz, dgf = _gate_bwd(dx, sv['z'], g_f, name='mlp_gate_bwd')
        mat['w_mlp_out', i] = _mm(sv['r'], dz, ta=True, name='mlp_dw_out')
        da = _mm(dz, P['w_mlp_out'][i], tb=True, name='mlp_da', extras=[(sv['r'], 'tile')],
                 epilogue=_relu2_bwd_epilogue, out_dtypes=(BF16,), b_outer=True, tm=WIDE_ROWS)
        mat['w_mlp_in', i] = _mm(sv['h2'], da, ta=True, name='mlp_dw_in', out_chips=N_CHIPS)
        dx1, dy, sums_f = _dh_norm_bwd(da, P['w_mlp_in'][i], sv['x1'], dx, w_mlp, sc_f, name='mlp_dh',
                                       gate=(sv['y'], g_m))
        if kind == 0:
            carried = hooks is not None and i == 0
            last, g, comm_outs = _fox_backward(dy, sv['h1'], sv['S'], P, j, D,
                                              comm=hooks.bwd_comm(mat) if carried else None)
            if carried:
                hooks.bwd_done(comm_outs)
        elif kind == 1:
            last, g = _sg_backward(dy, sv['h1'], sv['S'], P, D)
        else:
            last, g = _cv_backward(dy, sv['h1'], sv['S'], P, D)
            g['cv_b_pw2'] = sums_f[4]
        for k, val in g.items():
            if k in BIG:
                mat[k, j] = val
            else:
                mix_grads.setdefault(k, {})[j] = val
        dx, sums_m = _dh_norm_bwd(*last, sv['x'], dx1, w_mix, sc_m, name='mix_dh')
        grads['norm_mlp'][i], grads['norm_mix'][i] = sums_f[2], sums_m[2]
        dmod[i] = jnp.concatenate([sums_m[0], sums_m[1], sums_f[3], sums_f[0], sums_f[1], dgf[0]])

    out = {k: jnp.stack(v) for k, v in grads.items()}
    for k, per_j in mix_grads.items():
        out[k] = jnp.stack([per_j[j] for j in sorted(per_j)])
    return loss_part, dx, jnp.stack(dmod), out, mat


def _all_gather8(blocks, *, name):
    n = len(blocks)

    def body(*refs):
        x_refs, out_refs = refs[:n], refs[n:2 * n]
        send_sems, recv_sems, local_sems = refs[2 * n:]
        x, y, c = _position()
        me, sibling = (x, y, c), (x, y, 1 - c)
        chips = [(1 - x, y), (x, 1 - y), (1 - x, 1 - y)]

        def slot(a, px, py, pc):
            return out_refs[a].at[4 * px + 2 * py + pc]

        def copy(a, k, blk, to, src=None):
            return pltpu.make_async_remote_copy(
                src_ref=slot(a, *blk) if src is None else src, dst_ref=slot(a, *blk),
                send_sem=send_sems.at[7 * a + k], recv_sem=recv_sems.at[7 * a + k],
                device_id=to, device_id_type=MESH)

        mine = [pltpu.make_async_copy(x_refs[a], slot(a, *me), local_sems.at[a]) for a in range(n)]
        for cp in mine:
            cp.start()
        first = []
        for j, chip in enumerate(chips):
            first += [copy(a, 1 + j, me, (*chip, c), src=x_refs[a]) for a in range(n)]
        first += [copy(a, 0, me, sibling, src=x_refs[a]) for a in range(n)]
        for cp in first:
            cp.start()
        passed = []
        for j, chip in enumerate(chips):
            for a in range(n):
                copy(a, 1 + j, (*chip, c), me).wait_recv()
                passed.append(copy(a, 4 + j, (*chip, c), sibling))
                passed[-1].start()
        for a in range(n):
            copy(a, 0, sibling, me).wait_recv()
        for j, chip in enumerate(chips):
            for a in range(n):
                copy(a, 4 + j, (*chip, 1 - c), me).wait_recv()
        for cp in first + passed:
            cp.wait_send()
        for cp in mine:
            cp.wait()

    return pl.pallas_call(
        body, name=name, in_specs=[ANY] * n, out_specs=[ANY] * n,
        out_shape=[jax.ShapeDtypeStruct((8,) + b.shape, b.dtype) for b in blocks],
        scratch_shapes=[pltpu.SemaphoreType.DMA((7 * n,)), pltpu.SemaphoreType.DMA((7 * n,)),
                        pltpu.SemaphoreType.DMA((n,))],
    )(*blocks)


def _exchange(comm, *, name, aliases=None):
    ns, no = len(comm.srcs), len(comm.out_shapes)

    def body(*refs):
        copies = _comm_copies(comm.plan, refs[:ns], refs[ns:ns + no], *refs[ns + no:])
        _comm_start(copies)
        _comm_wait(copies)

    return pl.pallas_call(
        body, name=name, in_specs=[ANY] * ns, out_specs=[ANY] * no, out_shape=list(comm.out_shapes),
        scratch_shapes=comm.scratch(), input_output_aliases=aliases or {},
    )(*comm.srcs)


def _gather_comm(halves):
    n = len(halves)

    def plan(src, out, x, y, c):
        mine = 4 * x + 2 * y + c
        remote = [(src[a], out[a].at[mine], (x, y, 1 - c), out[a].at[4 * x + 2 * y + 1 - c]) for a in range(n)]
        for fx, fy in CHIP_FLIPS:
            px, py = _flip(x, fx), _flip(y, fy)
            remote += [(src[a], out[a].at[mine], (px, py, c), out[a].at[4 * px + 2 * py + c]) for a in range(n)]
        return remote, [(src[a], out[a].at[mine]) for a in range(n)]

    return Comm(list(halves), [jax.ShapeDtypeStruct((8,) + h.shape, h.dtype) for h in halves], plan, 4 * n, n)


def _gather_forward(bufs, *, name):
    n = len(bufs)

    def plan(src, out, x, y, c):
        remote = []
        for fx, fy in CHIP_FLIPS:
            px, py = _flip(x, fx), _flip(y, fy)
            remote += [(src[a].at[4 * px + 2 * py + c], out[a].at[4 * px + 2 * py + c], (x, y, 1 - c),
                        out[a].at[4 * px + 2 * py + 1 - c]) for a in range(n)]
        return remote, []

    comm = Comm(list(bufs), [jax.ShapeDtypeStruct(b.shape, b.dtype) for b in bufs], plan, 3 * n, 0)
    return _exchange(comm, name=name, aliases={a: a for a in range(n)})


CHIP_FLIPS = ((1, 0), (0, 1), (1, 1))


def _flip(v, f):
    return 1 - v if f else v


def _sum_rows_tile(R, C, budget=3 << 20):
    best = None
    for t in range(8, R + 1, 8):
        if R % t == 0 and t * C * 4 <= budget:
            best = t
    return best if best is not None else R


def _rs_begin(gps, *, wire_dtype):
    n = len(gps)
    c_arr = jnp.reshape(_position()[2], (1,)).astype(jnp.int32)

    def plan(src, out, x, y, c):
        return [(src[a].at[b, 1 - c], out[a].at[b], (x, y, 1 - c), out[a].at[b])
                for a in range(n) for b in range(4)], []

    got1 = _exchange(Comm(list(gps), [jax.ShapeDtypeStruct((4,) + g.shape[2:], F32) for g in gps], plan, 4 * n, 0),
                     name='rs_sibling')

    def sum_chip(c_ref, mine_ref, got_ref, out_ref):
        out_ref[...] = (mine_ref[...] + got_ref[...]).astype(out_ref.dtype)

    parts = []
    for gp, g1 in zip(gps, got1):
        _, _, R, C = gp.shape
        tr = _sum_rows_tile(R, C)
        parts.append(pl.pallas_call(
            sum_chip, name='rs_sum_chip',
            grid_spec=pltpu.PrefetchScalarGridSpec(
                num_scalar_prefetch=1, grid=(4, R // tr),
                in_specs=[pl.BlockSpec((None, None, tr, C), lambda b, r, cr: (b, cr[0], r, 0)),
                          pl.BlockSpec((None, tr, C), lambda b, r, cr: (b, r, 0))],
                out_specs=pl.BlockSpec((None, tr, C), lambda b, r, cr: (b, r, 0))),
            out_shape=jax.ShapeDtypeStruct((4, R, C), wire_dtype),
            compiler_params=_cparams("parallel", "parallel"),
        )(c_arr, gp, g1))
    return got1, parts


def _rs_chips_comm(parts):
    n = len(parts)

    def plan(src, out, x, y, c):
        remote = []
        for k, (fx, fy) in enumerate(CHIP_FLIPS):
            px, py = _flip(x, fx), _flip(y, fy)
            remote += [(src[a].at[2 * px + py], out[a].at[k], (px, py, c), out[a].at[k]) for a in range(n)]
        return remote, []

    return Comm(list(parts), [jax.ShapeDtypeStruct((3,) + p.shape[1:], p.dtype) for p in parts], plan, 3 * n, 0)


def _rs_finish(gps, got1, got2):
    n = len(gps)
    x, y, c = _position()
    bc_arr = jnp.stack([2 * x + y, c]).astype(jnp.int32)

    def sum_final(bc_ref, mine_ref, got1_ref, got2_ref, out_ref):
        acc = mine_ref[...] + got1_ref[...]
        for k in range(3):
            acc = acc + got2_ref[k].astype(F32)
        out_ref[...] = acc

    halves = []
    for gp, g1, g2 in zip(gps, got1, got2):
        _, _, R, C = gp.shape
        tr = _sum_rows_tile(R, C, budget=2 << 20)
        halves.append(pl.pallas_call(
            sum_final, name='rs_sum_final',
            grid_spec=pltpu.PrefetchScalarGridSpec(
                num_scalar_prefetch=1, grid=(R // tr,),
                in_specs=[pl.BlockSpec((None, None, tr, C), lambda r, bc: (bc[0], bc[1], r, 0)),
                          pl.BlockSpec((None, tr, C), lambda r, bc: (bc[0], r, 0)),
                          pl.BlockSpec((3, tr, C), lambda r, bc: (0, r, 0))],
                out_specs=pl.BlockSpec((None, tr, C), lambda r, bc: (bc[1], r, 0))),
            out_shape=jax.ShapeDtypeStruct((2, R, C), F32),
            compiler_params=_cparams("parallel"),
        )(bc_arr, gp, g1, g2))

    def plan(src, out, x, y, c):
        return [(src[a].at[c], out[a].at[c], (x, y, 1 - c), out[a].at[1 - c]) for a in range(n)], []

    comm = Comm(halves, [jax.ShapeDtypeStruct(h.shape, F32) for h in halves], plan, n, 0)
    return _exchange(comm, name='rs_swap', aliases={a: a for a in range(n)})


def _sum8(gathered, *, name):
    _, R, C = gathered.shape

    def body(g_ref, o_ref):
        acc = g_ref[0]
        for k in range(1, 8):
            acc = acc + g_ref[k]
        o_ref[...] = acc

    return pl.pallas_call(body, name=name, out_shape=jax.ShapeDtypeStruct((R, C), F32))(gathered)


def _adamw(w, g, m, v, *, name):
    shape = w.shape
    cols = shape[-1]
    rows = w.size // cols
    tr = _sum_rows_tile(rows, cols, budget=1 << 20)

    def body(w_ref, g_ref, m_ref, v_ref, d_ref, mo_ref, vo_ref):
        gv = g_ref[...]
        mn = ADAM_B1 * m_ref[...] + (1.0 - ADAM_B1) * gv
        vn = ADAM_B2 * v_ref[...] + (1.0 - ADAM_B2) * (gv * gv)
        m_hat = mn / (1.0 - ADAM_B1 ** ADAM_STEP)
        v_hat = vn / (1.0 - ADAM_B2 ** ADAM_STEP)
        d_ref[...] = -ADAM_LR * (m_hat / (jnp.sqrt(v_hat) + ADAM_EPS) + ADAM_WD * w_ref[...])
        mo_ref[...] = mn
        vo_ref[...] = vn

    blk = pl.BlockSpec((tr, cols), lambda i: (i, 0))
    outs = pl.pallas_call(
        body, name=name, grid=(rows // tr,), in_specs=[blk] * 4, out_specs=[blk] * 3,
        out_shape=[jax.ShapeDtypeStruct((rows, cols), F32)] * 3,
        compiler_params=_cparams("parallel"),
    )(*[a.reshape(rows, cols) for a in (w, g, m, v)])
    return tuple(o.reshape(shape) for o in outs)


WEIGHTS = ['norm_mix', 'norm_mlp', 'w_ada', 'b_ada', 'w_mlp_in', 'w_mlp_out', 'fox_w_in', 'fox_b_f',
           'fox_q_norm', 'fox_k_norm', 'fox_w_out', 'sg_w_in', 'sg_ln_g', 'sg_ln_b', 'sg_w_s', 'sg_b_s',
           'sg_w_out', 'cv_w_pw1', 'cv_b_pw1', 'cv_w_dw', 'cv_b_dw', 'cv_ln_g', 'cv_ln_b', 'cv_w_pw2',
           'cv_b_pw2']
BIG = {'w_mlp_in': 2, 'w_mlp_out': 1, 'fox_w_in': 2, 'fox_w_out': 1, 'sg_w_in': 2, 'sg_w_out': 1,
       'cv_w_pw1': 2, 'cv_w_pw2': 1}
SMALL_SHARDED = ['cv_b_pw1', 'cv_w_dw', 'cv_b_dw', 'cv_ln_g', 'cv_ln_b', 'cv_b_pw2']
SMALL_GRADS = ['norm_mix', 'norm_mlp', 'fox_b_f', 'fox_q_norm', 'fox_k_norm', 'sg_ln_g', 'sg_ln_b', 'sg_w_s',
               'sg_b_s'] + SMALL_SHARDED
GRAD_WIRE_DTYPE = BF16


def _pack_rows(parts, cols):
    flat = jnp.concatenate([p.reshape(-1) for p in parts])
    rows = -(-flat.size // (8 * cols)) * 8
    return jnp.pad(flat, (0, rows * cols - flat.size)).reshape(rows, cols)


def _unpack(flat, shapes):
    out, off = [], 0
    for s in shapes:
        n = math.prod(s)
        out.append(flat[..., off:off + n].reshape(flat.shape[:-1] + tuple(s)))
        off += n
    return out


def _merge_chips(a, axis):
    a = jnp.moveaxis(a, 0, axis)
    return a.reshape(a.shape[:axis] + (a.shape[axis] * a.shape[axis + 1],) + a.shape[axis + 2:])


def _split_chips(a, axis):
    a = a.reshape(a.shape[:axis] + (4, a.shape[axis] // 4) + a.shape[axis + 1:])
    return jnp.moveaxis(a, axis, 0)


def _step(a):
    x, y, c = _position()
    me = 4 * x + 2 * y + c
    chip = 2 * x + y
    T, D = a['x'].shape[1], a['x'].shape[2]
    L = a['norm_mix'].shape[0]

    small_shapes = [(D,)] + [a[n].shape for n in SMALL_SHARDED]
    small = _all_gather8([_pack_rows([a['c']] + [a[n] for n in SMALL_SHARDED], LANE)], name='ag_small')[0]
    small = small.reshape(8, -1)
    c_all = _unpack(small, small_shapes[:1])[0]
    sharded = _unpack(small[0::2, D:], small_shapes[1:])
    P = {n: _merge_chips(v, v.ndim - 2) for n, v in zip(SMALL_SHARDED, sharded)}

    c_act = _silu_rows(c_all, name='c_act')
    mod_cols = jnp.stack([
        _mm(c_act, a['w_ada'][i], name='ada_mod', tm=8, tn=_col_tile(a['w_ada'].shape[2], 768),
            extras=[(lax.dynamic_slice_in_dim(a['b_ada'][i:i + 1], chip * a['w_ada'].shape[2],
                                              a['w_ada'].shape[2], axis=1), 'row')],
            epilogue=_bias_epilogue)
        for i in range(L)])
    mod_all = _all_gather8([mod_cols.reshape(L * 8, -1)], name='ag_mod')[0].reshape(8, L, 8, -1)
    mod = lax.dynamic_index_in_dim(mod_all[0::2], me, axis=2, keepdims=False)
    mod = jnp.moveaxis(mod, 0, 1).reshape(L, 6 * D)

    units = _matrix_units(L)
    first, with_prep, with_attn, with_last = units[:1], units[1:4], units[4:-3], units[-3:]
    last, earlier = units[:2], units[2:]
    n_heads = D // HEAD_DIM

    def half_block(unit):
        blk = a[unit[0]][unit[1]]
        return lax.dynamic_index_in_dim(blk.astype(BF16).reshape(2, blk.shape[0] // 2, blk.shape[1]), c, axis=0,
                                        keepdims=False)

    def install(group, gathered):
        for (name, idx), gth in zip(group, gathered):
            blocks = gth.reshape((4,) + a[name].shape[1:])
            if name == 'fox_w_in':
                pad = jnp.zeros((blocks.shape[1], LANE - n_heads), BF16)
                full = jnp.concatenate([blocks[0], blocks[1], blocks[2], blocks[3], pad], axis=-1)
            else:
                full = _merge_chips(blocks, BIG[name] - 1)
            if name in ('w_mlp_in', 'w_mlp_out', 'fox_w_in', 'fox_w_out'):
                P.setdefault(name, {})[idx] = full
            else:
                P[name] = full

    install(first, _all_gather8([half_block(u) for u in first], name='ag_weights_first'))
    for n in ('sg_w_s', 'sg_b_s', 'cv_w_dw'):
        P[n] = (P[n] if n in P else a[n])[0]
    for n in ('norm_mix', 'norm_mlp', 'fox_b_f', 'fox_q_norm', 'fox_k_norm', 'sg_ln_g', 'sg_ln_b'):
        P[n] = a[n]

    def split_grad(unit, grad):
        name = unit[0]
        if name == 'fox_w_in':
            grad = grad[:, :a[name].shape[2] * N_CHIPS]
        blk = grad if grad.ndim == 3 else _split_chips(grad, BIG[name] - 1)
        return blk.reshape(N_CHIPS, 2, blk.shape[1] // 2, blk.shape[2])

    state = {}

    def riding(group):
        return (_gather_comm([half_block(u) for u in group]),
                lambda outs: install(group, _gather_forward(outs, name='ag_weights_forward')))

    def bwd_comm(mat):
        state['gps'] = [split_grad(u, mat[u]) for u in earlier]
        state['got1'], parts = _rs_begin(state['gps'], wire_dtype=GRAD_WIRE_DTYPE)
        return _rs_chips_comm(parts)

    def bwd_done(outs):
        state['got2'] = outs

    last_fox = N_MIXERS * ((L - 1) // N_MIXERS)
    hooks = Hooks({0: {'prep': riding(with_prep), 'attn': riding(with_attn)}, last_fox: {'attn': riding(with_last)}},
                  bwd_comm, bwd_done)
    loss_part, grad_x, dmod, g, mat = _local_step(a['x'][0], a['loss_target'][0], mod, P, hooks)

    small_g = [dmod, loss_part[0:1, 0:1]] + [g[n] for n in SMALL_GRADS]
    small_g_shapes = [s.shape for s in small_g]
    all_small = _all_gather8([_pack_rows(small_g, LANE)], name='ag_small_grads')[0]
    summed = _sum8(all_small, name='sum_small_grads').reshape(-1)
    sums = _unpack(summed, small_g_shapes)
    loss = sums[1][0, 0]
    grads = dict(zip(SMALL_GRADS, sums[2:]))
    grads['b_ada'] = sums[0]
    for n in SMALL_SHARDED:
        blk = a[n].shape[-1]
        grads[n] = lax.dynamic_slice_in_dim(grads[n], chip * blk, blk, axis=grads[n].ndim - 1)
    dmod_all = all_small.reshape(8, -1)[:, :dmod.size].reshape(8, L, 6 * D)
    cols = a['w_ada'].shape[2]
    dmod_cols = lax.dynamic_slice_in_dim(dmod_all, chip * cols, cols, axis=2)
    pad8 = lambda t: jnp.pad(t, ((0, LANE - 8), (0, 0)))
    c_act_pad = pad8(c_act)
    grads['w_ada'] = jnp.stack([
        _mm(c_act_pad, pad8(dmod_cols[:, i]), ta=True, name='ada_dw', tn=_col_tile(cols, 768))
        for i in range(L)])

    shards = dict(zip(earlier, _rs_finish(state['gps'], state['got1'], state['got2'])))
    gps = [split_grad(u, mat[u]) for u in last]
    got1, parts = _rs_begin(gps, wire_dtype=GRAD_WIRE_DTYPE)
    got2 = _exchange(_rs_chips_comm(parts), name='rs_chips')
    shards.update(zip(last, _rs_finish(gps, got1, got2)))
    for n in BIG:
        grads[n] = jnp.stack([shards[n, idx].reshape(a[n].shape[1:]) for idx in range(a[n].shape[0])])

    deltas, new_m, new_v = {}, {}, {}
    for n in WEIGHTS:
        deltas[n], new_m[n], new_v[n] = _adamw(a[n], grads[n], a['m_' + n], a['v_' + n], name='adamw')
    return (loss, grad_x[None], *[grads[n] for n in WEIGHTS], *[deltas[n] for n in WEIGHTS],
            *[new_m[n] for n in WEIGHTS], *[new_v[n] for n in WEIGHTS])


def _matrix_units(n_layers):
    mixers = (('fox_w_in', 'fox_w_out'), ('sg_w_in', 'sg_w_out'), ('cv_w_pw1', 'cv_w_pw2'))
    units = []
    for i in range(n_layers):
        units += [(n, i // N_MIXERS) for n in mixers[i % N_MIXERS]] + [('w_mlp_in', i), ('w_mlp_out', i)]
    return units


def _silu_rows(x, *, name):
    def body(x_ref, o_ref):
        xv = x_ref[...]
        o_ref[...] = (xv * jax.nn.sigmoid(xv)).astype(BF16)

    return pl.pallas_call(body, name=name, out_shape=jax.ShapeDtypeStruct(x.shape, BF16))(x)


def kernel(x, c, norm_mix, norm_mlp, w_ada, b_ada, w_mlp_in, w_mlp_out, fox_w_in, fox_b_f, fox_q_norm, fox_k_norm, fox_w_out, sg_w_in, sg_ln_g, sg_ln_b, sg_w_s, sg_b_s, sg_w_out, cv_w_pw1, cv_b_pw1, cv_w_dw, cv_b_dw, cv_ln_g, cv_ln_b, cv_w_pw2, cv_b_pw2, loss_target, m_norm_mix, m_norm_mlp, m_w_ada, m_b_ada, m_w_mlp_in, m_w_mlp_out, m_fox_w_in, m_fox_b_f, m_fox_q_norm, m_fox_k_norm, m_fox_w_out, m_sg_w_in, m_sg_ln_g, m_sg_ln_b, m_sg_w_s, m_sg_b_s, m_sg_w_out, m_cv_w_pw1, m_cv_b_pw1, m_cv_w_dw, m_cv_b_dw, m_cv_ln_g, m_cv_ln_b, m_cv_w_pw2, m_cv_b_pw2, v_norm_mix, v_norm_mlp, v_w_ada, v_b_ada, v_w_mlp_in, v_w_mlp_out, v_fox_w_in, v_fox_b_f, v_fox_q_norm, v_fox_k_norm, v_fox_w_out, v_sg_w_in, v_sg_ln_g, v_sg_ln_b, v_sg_w_s, v_sg_b_s, v_sg_w_out, v_cv_w_pw1, v_cv_b_pw1, v_cv_w_dw, v_cv_b_dw, v_cv_ln_g, v_cv_ln_b, v_cv_w_pw2, v_cv_b_pw2):
    return _step(dict(locals()))
```

```python
import math
from typing import Callable, NamedTuple

import jax
import jax.numpy as jnp
from jax import lax
from jax.experimental import pallas as pl
from jax.experimental.pallas import tpu as pltpu

F32 = jnp.float32
BF16 = jnp.bfloat16

EPS = 1e-6
HEAD_DIM = 64
LANE = 128
CONV_WIDTH = 31
CONV_HALO = 32
SG_CHUNK = 128
SG_CAUSAL = 64
SG_GROUPS = 8
N_MIXERS = 3
N_CHIPS = 4
VMEM_LIMIT = 56 * 1024 * 1024
NEG = -1e30

ADAM_LR = 0.001
ADAM_B1 = 0.9
ADAM_B2 = 0.999
ADAM_EPS = 1e-08
ADAM_WD = 0.01
ADAM_STEP = 10

MESH = pl.DeviceIdType.MESH
ANY = pl.BlockSpec(memory_space=pl.ANY)


def _cparams(*sem):
    return pltpu.CompilerParams(dimension_semantics=sem, vmem_limit_bytes=VMEM_LIMIT)


WIDE_ROWS = 1024


def _row_tile(t, want=512):
    return min(t, want)


def _matmul(a, b, *, name, ta=False, tb=False, tm=512, tn=1024, tk=1024,
            extras=(), epilogue=None, out_dtypes=(F32,), b_outer=False, out_chips=None, row_sums=False):
    M, K = (a.shape[1], a.shape[0]) if ta else a.shape
    N = b.shape[0] if tb else b.shape[1]
    assert (b.shape[1] if tb else b.shape[0]) == K
    n_own = N // out_chips if out_chips else N
    tm, tn, tk = min(tm, M), min(tn, n_own), min(tk, K)
    assert M % tm == 0 and n_own % tn == 0 and K % tk == 0, (name, M, N, K, tm, tn, tk)
    nk = K // tk

    def spec(shape, pick):
        if b_outer:
            return pl.BlockSpec(shape, lambda j, i, k: pick(i, j, k))
        return pl.BlockSpec(shape, pick)

    a_spec = spec((tk, tm), lambda i, j, k: (k, i)) if ta else spec((tm, tk), lambda i, j, k: (i, k))
    b_spec = spec((tn, tk), lambda i, j, k: (j, k)) if tb else spec((tk, tn), lambda i, j, k: (k, j))
    ex_specs = [spec((tm, tn), lambda i, j, k: (i, j)) if kind == 'tile' else spec((1, tn), lambda i, j, k: (0, j))
                for _, kind in extras]
    dims = (((0,) if ta else (1,), (1,) if tb else (0,)), ((), ()))
    n_ex, n_out = len(extras), len(out_dtypes) + bool(row_sums)
    assert not row_sums or N == tn

    def body(*refs):
        a_ref, b_ref = refs[0], refs[1]
        ex = refs[2:2 + n_ex]
        outs = refs[2 + n_ex:2 + n_ex + n_out]

        def finish(acc):
            vals = epilogue(acc, *[r[...] for r in ex]) if epilogue else (acc,)
            for o, v in zip(outs[:len(out_dtypes)], vals):
                o[...] = v.astype(o.dtype)
            if row_sums:
                row_tile = pl.program_id(1 if b_outer else 0)

                @pl.when(row_tile == 0)
                def _():
                    outs[-1][...] = vals[-1]

                @pl.when(row_tile > 0)
                def _():
                    outs[-1][...] += vals[-1]

        part = lax.dot_general(a_ref[...].astype(BF16), b_ref[...].astype(BF16), dims,
                               preferred_element_type=F32)
        if nk == 1:
            finish(part)
        else:
            acc_ref = refs[-1]
            k = pl.program_id(2)

            @pl.when(k == 0)
            def _():
                acc_ref[...] = part

            @pl.when(k > 0)
            def _():
                acc_ref[...] += part

            @pl.when(k == nk - 1)
            def _():
                finish(acc_ref[...])

    outs = pl.pallas_call(
        body, name=name,
        grid=(N // tn, M // tm, nk) if b_outer else (M // tm, N // tn, nk),
        in_specs=[a_spec, b_spec] + ex_specs,
        out_specs=[spec((None, tm, tn), lambda i, j, k: (j // (n_own // tn), i, j % (n_own // tn)))
                   if out_chips else spec((tm, tn), lambda i, j, k: (i, j)) for _ in out_dtypes]
        + ([spec((8, tn), lambda i, j, k: (0, j))] if row_sums else []),
        out_shape=[jax.ShapeDtypeStruct((out_chips, M, n_own) if out_chips else (M, N), dt) for dt in out_dtypes]
        + ([jax.ShapeDtypeStruct((8, N), F32)] if row_sums else []),
        scratch_shapes=[pltpu.VMEM((tm, tn), F32)] if nk > 1 else [],
        compiler_params=(_cparams("arbitrary", "arbitrary", "arbitrary") if row_sums
                         else _cparams("parallel", "parallel", "arbitrary")),
    )(a, b, *[arr for arr, _ in extras])
    return outs if n_out > 1 else outs[0]


def _norm_mod_fwd(x, w, sc, sh, *, name):
    T, D = x.shape
    tr = _row_tile(T)

    def body(x_ref, w_ref, sc_ref, sh_ref, h_ref):
        xv = x_ref[...]
        r = lax.rsqrt(jnp.mean(xv * xv, axis=-1, keepdims=True) + EPS)
        h_ref[...] = ((xv * r) * w_ref[...] * (1.0 + sc_ref[...]) + sh_ref[...]).astype(BF16)

    row = pl.BlockSpec((1, D), lambda i: (0, 0))
    return pl.pallas_call(
        body, name=name, grid=(T // tr,),
        in_specs=[pl.BlockSpec((tr, D), lambda i: (i, 0)), row, row, row],
        out_specs=pl.BlockSpec((tr, D), lambda i: (i, 0)),
        out_shape=jax.ShapeDtypeStruct((T, D), BF16),
        compiler_params=_cparams("parallel"),
    )(x, w, sc, sh)


def _gate_bwd(dx, y, g, *, name):
    T, D = dx.shape
    tr = _row_tile(T)

    def body(dx_ref, y_ref, g_ref, dy_ref, dg_ref):
        i = pl.program_id(0)
        dxv = dx_ref[...]
        dy_ref[...] = (dxv * g_ref[...]).astype(BF16)
        part = jnp.concatenate([jnp.sum(dxv * y_ref[...], axis=0, keepdims=True),
                                jnp.zeros((7, D), F32)], axis=0)

        @pl.when(i == 0)
        def _():
            dg_ref[...] = part

        @pl.when(i > 0)
        def _():
            dg_ref[...] += part

    blk = pl.BlockSpec((tr, D), lambda i: (i, 0))
    return pl.pallas_call(
        body, name=name, grid=(T // tr,),
        in_specs=[blk, blk, pl.BlockSpec((1, D), lambda i: (0, 0))],
        out_specs=[blk, pl.BlockSpec((8, D), lambda i: (0, 0))],
        out_shape=[jax.ShapeDtypeStruct((T, D), BF16), jax.ShapeDtypeStruct((8, D), F32)],
        compiler_params=_cparams("arbitrary"),
    )(dx, y, g)


def _loss_head(y, target, *, name):
    T, D = y.shape
    tr = _row_tile(T)

    def body(y_ref, t_ref, loss_ref, dy_ref):
        i = pl.program_id(0)
        e = y_ref[...] - t_ref[...]
        dy_ref[...] = e * (1.0 / D)
        part = jnp.full((8, LANE), 0.5 / D * jnp.sum(e * e), F32)

        @pl.when(i == 0)
        def _():
            loss_ref[...] = part

        @pl.when(i > 0)
        def _():
            loss_ref[...] += part

    blk = pl.BlockSpec((tr, D), lambda i: (i, 0))
    return pl.pallas_call(
        body, name=name, grid=(T // tr,), in_specs=[blk, blk],
        out_specs=[pl.BlockSpec((8, LANE), lambda i: (0, 0)), blk],
        out_shape=[jax.ShapeDtypeStruct((8, LANE), F32), jax.ShapeDtypeStruct((T, D), F32)],
        compiler_params=_cparams("arbitrary"),
    )(y, target)


def _position():
    return lax.axis_index("x"), lax.axis_index("y"), lax.axis_index("c")


class Comm(NamedTuple):
    srcs: list
    out_shapes: list
    plan: Callable
    n_remote: int
    n_local: int

    def scratch(self):
        return [pltpu.SemaphoreType.DMA((self.n_remote,)), pltpu.SemaphoreType.DMA((self.n_remote,)),
                pltpu.SemaphoreType.DMA((max(self.n_local, 1),))]


def _comm_copies(plan, src_refs, out_refs, send_sems, recv_sems, local_sems):
    x, y, c = _position()
    remote, local = plan(src_refs, out_refs, x, y, c)

    def copy(k, s, d, peer):
        return pltpu.make_async_remote_copy(src_ref=s, dst_ref=d, send_sem=send_sems.at[k],
                                            recv_sem=recv_sems.at[k], device_id=peer, device_id_type=MESH)

    sends = [copy(k, s, d, peer) for k, (s, d, peer, _) in enumerate(remote)]
    recvs = [copy(k, s, landing, peer) for k, (s, _, peer, landing) in enumerate(remote)]
    local_copies = [pltpu.make_async_copy(s, d, local_sems.at[i]) for i, (s, d) in enumerate(local)]
    return sends, recvs, local_copies


def _comm_start(copies):
    sends, _, local_copies = copies
    for cp in local_copies + sends:
        cp.start()


def _comm_wait(copies):
    sends, recvs, local_copies = copies
    for cp in recvs:
        cp.wait_recv()
    for cp in sends:
        cp.wait_send()
    for cp in local_copies:
        cp.wait()


def _split_comm_refs(refs, n_in, n_out, n_scratch, comm):
    ns, nd = (len(comm.srcs), len(comm.out_shapes)) if comm else (0, 0)
    cuts = [n_in, ns, n_out, nd, n_scratch]
    parts, at = [], 0
    for n in cuts:
        parts.append(refs[at:at + n])
        at += n
    return (*parts, refs[at:])


AUG_F = HEAD_DIM
AUG_LSE = HEAD_DIM + 6


def _half_cols(x, lo):
    return (jnp.sum(jnp.where(lo, x, 0.0), axis=-1, keepdims=True),
            jnp.sum(jnp.where(lo, 0.0, x), axis=-1, keepdims=True))


def _half_sums(x, lo):
    s_lo, s_hi = _half_cols(x, lo)
    return jnp.where(lo, s_lo, s_hi)


def _split3(x):
    a = x.astype(BF16).astype(F32)
    r = x - a
    b = r.astype(BF16).astype(F32)
    return a, b, (r - b).astype(BF16).astype(F32)


def _aug(lane, base, terms):
    out = jnp.zeros(lane.shape, F32)
    for i, t in enumerate(terms):
        out = jnp.where(lane == base + i, t, out)
    return out


def _head_lanes(x2, h):
    return x2 if h == 0 else pltpu.roll(x2, HEAD_DIM, 1)


def _fox_prep_fwd(proj, qg, kg, fcol, *, d_model, name, comm=None):
    T = proj.shape[0]
    nhp = d_model // LANE
    tr = _row_tile(T, WIDE_ROWS)

    def body(*refs):
        ((q_ref, k_ref, v_ref, qg_ref, kg_ref, f_ref), src_refs,
         (qa_ref, qta_ref, ka_ref, kta_ref, va_ref, vta_ref), dst_refs, _, sems) = _split_comm_refs(refs, 6, 6, 0, comm)
        if comm:
            @pl.when((pl.program_id(0) == 0) & (pl.program_id(1) == 0))
            def _():
                _comm_start(_comm_copies(comm.plan, src_refs, dst_refs, *sems))
        lane = lax.broadcasted_iota(jnp.int32, (tr, LANE), 1)
        lo = lane < HEAD_DIM

        def norm(xv, g):
            ms = _half_sums(xv * xv, lo) * (1.0 / HEAD_DIM)
            return (xv * lax.rsqrt(ms + EPS)) * g

        qn = norm(q_ref[...], qg_ref[...]) * (HEAD_DIM ** -0.5)
        kn = norm(k_ref[...], kg_ref[...])
        vv = v_ref[...]
        qa, ka, va, vta = [], [], [], []
        for h in range(2):
            f1, f2, f3 = _split3(f_ref[h])
            qa.append(jnp.where(lo, _head_lanes(qn, h), _aug(lane, AUG_F, [f1, f2, f3, 1.0, 1.0, 1.0])))
            ka.append(jnp.where(lo, _head_lanes(kn, h),
                                _aug(lane, AUG_F, [1.0, 1.0, 1.0, -f1, -f2, -f3, 1.0, 1.0, 1.0])))
            va.append(jnp.where(lo if h == 0 else jnp.logical_not(lo), vv, 0.0))
            vta.append(jnp.where(lo, _head_lanes(vv, h), _aug(lane, AUG_F, [1.0, 1.0, 1.0])))
        for parts, ref, tref in ((qa, qa_ref, qta_ref), (ka, ka_ref, kta_ref), (va, va_ref, None),
                                 (vta, None, vta_ref)):
            both = jnp.concatenate(parts, axis=1)
            if ref is not None:
                ref[...] = both.astype(BF16)
            if tref is not None:
                tref[...] = both.astype(BF16).T
        if comm:
            @pl.when((pl.program_id(0) == T // tr - 1) & (pl.program_id(1) == nhp - 1))
            def _():
                _comm_wait(_comm_copies(comm.plan, src_refs, dst_refs, *sems))

    gain = pl.BlockSpec((1, LANE), lambda i, h: (0, 0))
    rows = pl.BlockSpec((tr, 2 * LANE), lambda i, h: (i, h))
    cols = pl.BlockSpec((2 * LANE, tr), lambda i, h: (h, i))
    wide, tall = jax.ShapeDtypeStruct((T, 2 * d_model), BF16), jax.ShapeDtypeStruct((2 * d_model, T), BF16)
    outs = pl.pallas_call(
        body, name=name, grid=(T // tr, nhp),
        in_specs=[pl.BlockSpec((tr, LANE), lambda i, h: (i, h)),
                  pl.BlockSpec((tr, LANE), lambda i, h: (i, nhp + h)),
                  pl.BlockSpec((tr, LANE), lambda i, h: (i, 2 * nhp + h)), gain, gain,
                  pl.BlockSpec((2, tr, 1), lambda i, h: (h, i, 0))] + ([ANY] * len(comm.srcs) if comm else []),
        out_specs=[rows, cols, rows, cols, rows, cols] + ([ANY] * len(comm.out_shapes) if comm else []),
        out_shape=[wide, tall, wide, tall, wide, tall] + (list(comm.out_shapes) if comm else []),
        scratch_shapes=comm.scratch() if comm else [],
        compiler_params=_cparams("arbitrary", "arbitrary") if comm else _cparams("parallel", "parallel"),
    )(proj, proj, proj, qg, kg, fcol, *(comm.srcs if comm else []))
    return outs[:6], outs[6:]


def _fox_do_prep(do, o, *, name):
    T, D = do.shape
    nhp = D // LANE
    tr = _row_tile(T, WIDE_ROWS)

    def body(do_ref, o_ref, doa_ref, dota_ref):
        lane = lax.broadcasted_iota(jnp.int32, (tr, LANE), 1)
        lo = lane < HEAD_DIM
        dob = do_ref[...].astype(BF16).astype(F32)
        deltas = _half_cols(dob * o_ref[...], lo)
        both = jnp.concatenate(
            [jnp.where(lo, _head_lanes(dob, h), _aug(lane, AUG_F, _split3(-deltas[h]))) for h in range(2)], axis=1)
        doa_ref[...] = both.astype(BF16)
        dota_ref[...] = both.astype(BF16).T

    blk = pl.BlockSpec((tr, LANE), lambda i, h: (i, h))
    return pl.pallas_call(
        body, name=name, grid=(T // tr, nhp), in_specs=[blk, blk],
        out_specs=[pl.BlockSpec((tr, 2 * LANE), lambda i, h: (i, h)),
                   pl.BlockSpec((2 * LANE, tr), lambda i, h: (h, i))],
        out_shape=[jax.ShapeDtypeStruct((T, 2 * D), BF16), jax.ShapeDtypeStruct((2 * D, T), BF16)],
        compiler_params=_cparams("parallel", "parallel"),
    )(do, o)


def _fox_prep_bwd(proj, dq, dkt, dvt, qg, kg, *, d_model, name):
    T = proj.shape[0]
    nhp = d_model // LANE
    tr = _row_tile(T, WIDE_ROWS)

    def body(q_ref, k_ref, dq_ref, dkt_ref, dvt_ref, qg_ref, kg_ref, dqo_ref, dko_ref, dvo_ref, sums_ref):
        first = (pl.program_id(0) == 0) & (pl.program_id(1) == 0)
        lo = lax.broadcasted_iota(jnp.int32, (tr, LANE), 1) < HEAD_DIM

        def pair(x2):
            return jnp.where(lo, x2[:, :LANE], pltpu.roll(x2[:, LANE:], HEAD_DIM, 1))

        def bwd(xv, dxhat, g):
            ms = _half_sums(xv * xv, lo) * (1.0 / HEAD_DIM)
            r = lax.rsqrt(ms + EPS)
            n = xv * r
            dn = dxhat * g
            dx = r * (dn - n * (_half_sums(dn * n, lo) * (1.0 / HEAD_DIM)))
            dg = jnp.sum(dxhat * n, axis=0, keepdims=True)
            return dx, dg + pltpu.roll(dg, HEAD_DIM, 1)

        dxq, dgq = bwd(q_ref[...], pair(dq_ref[...]) * (HEAD_DIM ** -0.5), qg_ref[...])
        dxk, dgk = bwd(k_ref[...], pair(dkt_ref[...].T), kg_ref[...])
        dqo_ref[...] = dxq.astype(BF16)
        dko_ref[...] = dxk.astype(BF16)
        dvo_ref[...] = pair(dvt_ref[...].T.astype(F32)).astype(BF16)
        part = jnp.concatenate([dgq, dgk, jnp.zeros((6, LANE), F32)], axis=0)

        @pl.when(first)
        def _():
            sums_ref[...] = part

        @pl.when(jnp.logical_not(first))
        def _():
            sums_ref[...] += part

    gain = pl.BlockSpec((1, LANE), lambda i, h: (0, 0))
    blk = pl.BlockSpec((tr, LANE), lambda i, h: (i, h))
    tall = pl.BlockSpec((2 * LANE, tr), lambda i, h: (h, i))
    return pl.pallas_call(
        body, name=name, grid=(T // tr, nhp),
        in_specs=[blk, pl.BlockSpec((tr, LANE), lambda i, h: (i, nhp + h)),
                  pl.BlockSpec((tr, 2 * LANE), lambda i, h: (i, h)), tall, tall, gain, gain],
        out_specs=[blk, blk, blk, pl.BlockSpec((8, LANE), lambda i, h: (0, 0))],
        out_shape=[jax.ShapeDtypeStruct((T, d_model), BF16)] * 3 + [jax.ShapeDtypeStruct((8, LANE), F32)],
        compiler_params=_cparams("arbitrary", "arbitrary"),
    )(proj, proj, dq, dkt, dvt, qg, kg)


def _scan_lanes(x, reverse):
    n = x.shape[-1]
    lane = lax.broadcasted_iota(jnp.int32, x.shape, 1)
    sh = 1
    while sh < n:
        if reverse:
            x = x + jnp.where(lane < n - sh, pltpu.roll(x, n - sh, 1), 0.0)
        else:
            x = x + jnp.where(lane >= sh, pltpu.roll(x, sh, 1), 0.0)
        sh *= 2
    return x


def _fox_gate_fwd(fpre_t, bf, *, name):
    def body(f_ref, b_ref, o_ref):
        xv = f_ref[...] + b_ref[...]
        logf = jnp.minimum(xv, 0.0) - jnp.log1p(jnp.exp(-jnp.abs(xv)))
        o_ref[...] = _scan_lanes(logf, reverse=False)

    return pl.pallas_call(body, name=name, out_shape=jax.ShapeDtypeStruct(fpre_t.shape, F32))(fpre_t, bf)


def _fox_gate_bwd(dcol, drow, fpre_t, bf, *, name):
    H = fpre_t.shape[0]

    def body(dc_ref, dr_ref, f_ref, b_ref, o_ref, db_ref):
        xv = f_ref[...] + b_ref[...]
        e = dc_ref[...] - dr_ref[...]
        dlogf = _scan_lanes(e, reverse=False) - e
        dpre = dlogf * (1.0 - jax.nn.sigmoid(xv))
        o_ref[...] = dpre
        db_ref[...] = jnp.broadcast_to(jnp.sum(dpre, axis=-1, keepdims=True), (H, LANE))

    return pl.pallas_call(
        body, name=name,
        out_shape=[jax.ShapeDtypeStruct(fpre_t.shape, F32), jax.ShapeDtypeStruct((H, LANE), F32)],
    )(dcol, drow, fpre_t, bf)


_NT = (((1,), (1,)), ((), ()))
_TN = (((0,), (0,)), ((), ()))
_NN = (((1,), (0,)), ((), ()))


def _attn_tile(T):
    return min(T, 1024)


def _causal(tq, tk):
    return lax.broadcasted_iota(jnp.int32, (tq, tk), 1) <= lax.broadcasted_iota(jnp.int32, (tq, tk), 0)


def _fox_attn_fwd(qa, kta, va, *, name, comm=None):
    T = qa.shape[0]
    nhp = qa.shape[1] // (2 * LANE)
    tq = tk = _attn_tile(T)
    nq = T // tq

    def body(*refs):
        (qa_ref, kta_ref, va_ref), src_refs, (o_ref, qb_ref), dst_refs, (m_sc, l_sc, acc_sc), sems = (
            _split_comm_refs(refs, 3, 2, 3, comm))
        hp, i, j = pl.program_id(0), pl.program_id(1), pl.program_id(2)
        if comm:
            @pl.when((hp == 0) & (i == 0) & (j == 0))
            def _():
                _comm_start(_comm_copies(comm.plan, src_refs, dst_refs, *sems))

        @pl.when(j == 0)
        def _():
            m_sc[...] = jnp.full(m_sc.shape, NEG, F32)
            l_sc[...] = jnp.zeros(l_sc.shape, F32)
            acc_sc[...] = jnp.zeros(acc_sc.shape, F32)

        def block(diagonal):
            heads = [slice(h * LANE, (h + 1) * LANE) for h in range(2)]
            scores = [lax.dot_general(qa_ref[:, hs], kta_ref[hs, :], _NN, preferred_element_type=F32)
                      for hs in heads]
            state = [(m_sc[h], l_sc[h], acc_sc[h]) for h in range(2)]
            probs, updates = [], []
            for s, (m_prev, l_prev, _) in zip(scores, state):
                if diagonal:
                    s = jnp.where(_causal(tq, tk), s, NEG)
                m_next = jnp.maximum(m_prev, jnp.max(s, axis=1, keepdims=True))
                p = jnp.exp(s - jnp.tile(m_next, (1, tk // LANE)))
                alpha = jnp.exp(m_prev - m_next)
                probs.append(p.astype(BF16))
                updates.append((m_next, alpha, alpha * l_prev + jnp.sum(p, axis=1, keepdims=True)))
            pvs = [lax.dot_general(p, va_ref[:, hs], _NN, preferred_element_type=F32)
                   for p, hs in zip(probs, heads)]
            for h in range(2):
                m_next, alpha, l_next = updates[h]
                m_sc[h] = m_next
                l_sc[h] = l_next
                acc_sc[h] = alpha * state[h][2] + pvs[h]

        @pl.when(j < i)
        def _():
            block(False)

        @pl.when(j == i)
        def _():
            block(True)
            o_ref[...] = acc_sc[0] / l_sc[0] + acc_sc[1] / l_sc[1]
            lane = lax.broadcasted_iota(jnp.int32, (tq, LANE), 1)
            for h in range(2):
                hs = slice(h * LANE, (h + 1) * LANE)
                pieces = _split3(-(m_sc[h] + jnp.log(l_sc[h])))
                qb = qa_ref[:, hs].astype(F32)
                for n, piece in enumerate(pieces):
                    qb = jnp.where(lane == AUG_LSE + n, piece, qb)
                qb_ref[:, hs] = qb.astype(BF16)

        if comm:
            @pl.when((hp == nhp - 1) & (i == nq - 1) & (j == nq - 1))
            def _():
                _comm_wait(_comm_copies(comm.plan, src_refs, dst_refs, *sems))

    outs = pl.pallas_call(
        body, name=name, grid=(nhp, nq, nq),
        in_specs=[pl.BlockSpec((tq, 2 * LANE), lambda h, i, j: (i, h)),
                  pl.BlockSpec((2 * LANE, tk), lambda h, i, j: (h, jnp.minimum(j, i))),
                  pl.BlockSpec((tk, 2 * LANE), lambda h, i, j: (jnp.minimum(j, i), h))]
        + ([ANY] * len(comm.srcs) if comm else []),
        out_specs=[pl.BlockSpec((tq, LANE), lambda h, i, j: (i, h)),
                   pl.BlockSpec((tq, 2 * LANE), lambda h, i, j: (i, h))]
        + ([ANY] * len(comm.out_shapes) if comm else []),
        out_shape=[jax.ShapeDtypeStruct((T, nhp * LANE), F32), jax.ShapeDtypeStruct(qa.shape, BF16)]
        + (list(comm.out_shapes) if comm else []),
        scratch_shapes=[pltpu.VMEM((2, tq, LANE), F32), pltpu.VMEM((2, tq, LANE), F32),
                        pltpu.VMEM((2, tq, LANE), F32)] + (comm.scratch() if comm else []),
        compiler_params=(_cparams("arbitrary", "arbitrary", "arbitrary") if comm
                         else _cparams("parallel", "parallel", "arbitrary")),
    )(qa, kta, va, *(comm.srcs if comm else []))
    return outs[0], outs[1], outs[2:]


def _fox_attn_bwd(qb, qta, ka, kta, vta, doa, dota, *, name, comm=None):
    T = qb.shape[0]
    nhp = qb.shape[1] // (2 * LANE)
    tq = tk = _attn_tile(T)
    nq = T // tq

    def body(*refs):
        ((qb_ref, qta_ref, ka_ref, kta_ref, vta_ref, doa_ref, dota_ref), src_refs,
         (dq_ref, dkt_ref, dvt_ref, dcol_ref, drow_ref), dst_refs, (dkt_sc, dvt_sc, dcol_sc), sems) = (
            _split_comm_refs(refs, 7, 5, 3, comm))
        hp, j, i = pl.program_id(0), pl.program_id(1), pl.program_id(2)
        if comm:
            @pl.when((hp == 0) & (j == 0) & (i == 0))
            def _():
                _comm_start(_comm_copies(comm.plan, src_refs, dst_refs, *sems))

        @pl.when((j == 0) & (i == 0))
        def _():
            dq_ref[...] = jnp.zeros(dq_ref.shape, F32)
            drow_ref[...] = jnp.zeros(drow_ref.shape, F32)

        @pl.when(i == 0)
        def _():
            dkt_sc[...] = jnp.zeros(dkt_sc.shape, F32)
            dvt_sc[...] = jnp.zeros(dvt_sc.shape, F32)
            dcol_sc[...] = jnp.zeros(dcol_sc.shape, F32)

        def block(diagonal):
            rows = pl.ds(pl.multiple_of(i * tq, tq), tq)
            heads = [slice(h * LANE, (h + 1) * LANE) for h in range(2)]
            logits = [lax.dot_general(qb_ref[:, hs], kta_ref[hs, :], _NN, preferred_element_type=F32)
                      for hs in heads]
            dpds = [lax.dot_general(doa_ref[:, hs], vta_ref[hs, :], _NN, preferred_element_type=F32)
                    for hs in heads]
            pbs, dlbs = [], []
            for h in range(2):
                p = jnp.exp(logits[h])
                if diagonal:
                    p = jnp.where(_causal(tq, tk), p, 0.0)
                dl = p * dpds[h]
                pbs.append(p.astype(BF16))
                dlbs.append(dl.astype(BF16))
                dcol_sc[h] += jnp.sum(dl, axis=0, keepdims=True)
                drow_ref[h, rows, :] += jnp.sum(dl, axis=1, keepdims=True)
            for h, hs in enumerate(heads):
                dvt_sc[h] += lax.dot_general(dota_ref[hs, :], pbs[h], _NN, preferred_element_type=F32)
                dkt_sc[h] += lax.dot_general(qta_ref[hs, :], dlbs[h], _NN, preferred_element_type=F32)
                dq_ref[rows, hs] += lax.dot_general(dlbs[h], ka_ref[:, hs], _NN, preferred_element_type=F32)

        @pl.when(i > j)
        def _():
            block(False)

        @pl.when(i == j)
        def _():
            block(True)

        @pl.when(i == nq - 1)
        def _():
            dkt_ref[...] = jnp.concatenate([dkt_sc[0], dkt_sc[1]], axis=0)
            dvt_ref[...] = jnp.concatenate([dvt_sc[0], dvt_sc[1]], axis=0).astype(BF16)
            dcol_ref[...] = dcol_sc[...]

        if comm:
            @pl.when((hp == nhp - 1) & (j == nq - 1) & (i == nq - 1))
            def _():
                _comm_wait(_comm_copies(comm.plan, src_refs, dst_refs, *sems))

    qrow = pl.BlockSpec((tq, 2 * LANE), lambda h, j, i: (jnp.maximum(i, j), h))
    qcol = pl.BlockSpec((2 * LANE, tq), lambda h, j, i: (h, jnp.maximum(i, j)))
    krow = pl.BlockSpec((tk, 2 * LANE), lambda h, j, i: (j, h))
    kcol = pl.BlockSpec((2 * LANE, tk), lambda h, j, i: (h, j))
    tall = jax.ShapeDtypeStruct((qb.shape[1], T), F32)
    outs = pl.pallas_call(
        body, name=name, grid=(nhp, nq, nq),
        in_specs=[qrow, qcol, krow, kcol, kcol, qrow, qcol] + ([ANY] * len(comm.srcs) if comm else []),
        out_specs=[pl.BlockSpec((T, 2 * LANE), lambda h, j, i: (0, h)), kcol, kcol,
                   pl.BlockSpec((2, 1, tk), lambda h, j, i: (h, 0, j)),
                   pl.BlockSpec((2, T, 1), lambda h, j, i: (h, 0, 0))]
        + ([ANY] * len(comm.out_shapes) if comm else []),
        out_shape=[jax.ShapeDtypeStruct(qb.shape, F32), tall, jax.ShapeDtypeStruct(tall.shape, BF16),
                   jax.ShapeDtypeStruct((2 * nhp, 1, T), F32), jax.ShapeDtypeStruct((2 * nhp, T, 1), F32)]
        + (list(comm.out_shapes) if comm else []),
        scratch_shapes=[pltpu.VMEM((2, LANE, tk), F32), pltpu.VMEM((2, LANE, tk), F32),
                        pltpu.VMEM((2, 1, tk), F32)] + (comm.scratch() if comm else []),
        compiler_params=_cparams("arbitrary" if comm else "parallel", "arbitrary", "arbitrary"),
    )(qb, qta, ka, kta, vta, doa, dota, *(comm.srcs if comm else []))
    return (*outs[:5], outs[5:])


_GELU_C = math.sqrt(2.0 / math.pi)
_GELU_A = 0.044715


def _gelu(x):
    t = jnp.tanh(_GELU_C * (x + _GELU_A * (x * x * x)))
    return x * (0.5 * (1.0 + t)), t


def _gelu_grad(x, t):
    return 0.5 * (1.0 + t) + 0.5 * x * (1.0 - t * t) * (_GELU_C * (1.0 + 3.0 * _GELU_A * x * x))


def _layer_norm_stats(v):
    mu = jnp.mean(v, axis=-1, keepdims=True)
    vc = v - mu
    rstd = lax.rsqrt(jnp.mean(vc * vc, axis=-1, keepdims=True) + EPS)
    return vc * rstd, rstd


def _layer_norm_bwd(dyhat, yhat, rstd):
    return rstd * (dyhat - jnp.mean(dyhat, axis=-1, keepdims=True)
                   - yhat * jnp.mean(dyhat * yhat, axis=-1, keepdims=True))


def _sg_mask():
    t = lax.broadcasted_iota(jnp.int32, (SG_CHUNK, SG_CHUNK), 0) // SG_CAUSAL
    s = lax.broadcasted_iota(jnp.int32, (SG_CHUNK, SG_CHUNK), 1) // SG_CAUSAL
    return s <= t


def _sg_mix(ws_ref, bc_ref, vln_sc, vo_sc, tr, gd):
    mask = _sg_mask()
    for g in range(SG_GROUPS):
        wg = jnp.where(mask, ws_ref[g], 0.0).astype(BF16)
        cols = slice(g * gd, (g + 1) * gd)
        for n in range(tr // SG_CHUNK):
            rows = slice(n * SG_CHUNK, (n + 1) * SG_CHUNK)
            vo_sc[rows, cols] = lax.dot_general(wg, vln_sc[rows, cols], _NN,
                                                preferred_element_type=F32) + bc_ref[g]


def _sg_fwd(a_uv, ln_g, ln_b, ws, bcol, *, name):
    T, W = a_uv.shape[0], a_uv.shape[1] // 2
    gd = W // SG_GROUPS
    tr = _row_tile(T)

    def body(u_ref, v_ref, g_ref, b_ref, ws_ref, bc_ref, o_ref, vln_sc, vo_sc):
        u, _ = _gelu(u_ref[...])
        v, _ = _gelu(v_ref[...])
        vhat, _ = _layer_norm_stats(v)
        vln_sc[...] = (vhat * g_ref[...] + b_ref[...]).astype(BF16)
        _sg_mix(ws_ref, bc_ref, vln_sc, vo_sc, tr, gd)
        o_ref[...] = (u * vo_sc[...]).astype(BF16)

    row = pl.BlockSpec((1, W), lambda i: (0, 0))
    return pl.pallas_call(
        body, name=name, grid=(T // tr,),
        in_specs=[pl.BlockSpec((tr, W), lambda i: (i, 0)), pl.BlockSpec((tr, W), lambda i: (i, 1)), row, row,
                  pl.BlockSpec((SG_GROUPS, SG_CHUNK, SG_CHUNK), lambda i: (0, 0, 0)),
                  pl.BlockSpec((SG_GROUPS, SG_CHUNK, 1), lambda i: (0, 0, 0))],
        out_specs=pl.BlockSpec((tr, W), lambda i: (i, 0)),
        out_shape=jax.ShapeDtypeStruct((T, W), BF16),
        scratch_shapes=[pltpu.VMEM((tr, W), BF16), pltpu.VMEM((tr, W), F32)],
        compiler_params=_cparams("parallel"),
    )(a_uv, a_uv, ln_g, ln_b, ws, bcol)


def _sg_bwd(a_uv, dgate, ln_g, ln_b, ws, bcol, *, name):
    T, W = a_uv.shape[0], a_uv.shape[1] // 2
    gd = W // SG_GROUPS
    tr = _row_tile(T)

    def body(u_ref, v_ref, dg_ref, g_ref, b_ref, ws_ref, bc_ref,
             da_ref, dws_ref, dbs_ref, sums_ref, vln_sc, vo_sc, dvo_sc, dvln_sc):
        i = pl.program_id(0)

        @pl.when(i == 0)
        def _():
            dws_ref[...] = jnp.zeros(dws_ref.shape, F32)
            dbs_ref[...] = jnp.zeros(dbs_ref.shape, F32)
            sums_ref[...] = jnp.zeros(sums_ref.shape, F32)

        ua, va = u_ref[...], v_ref[...]
        u, tu = _gelu(ua)
        v, tv = _gelu(va)
        vhat, rstd = _layer_norm_stats(v)
        vln_sc[...] = (vhat * g_ref[...] + b_ref[...]).astype(BF16)
        _sg_mix(ws_ref, bc_ref, vln_sc, vo_sc, tr, gd)
        dgt = dg_ref[...]
        du = dgt * vo_sc[...]
        dvo_sc[...] = dgt * u
        mask = _sg_mask()
        for g in range(SG_GROUPS):
            wg = jnp.where(mask, ws_ref[g], 0.0).astype(BF16)
            cols = slice(g * gd, (g + 1) * gd)
            acc_w = jnp.zeros((SG_CHUNK, SG_CHUNK), F32)
            acc_b = jnp.zeros((SG_CHUNK, 1), F32)
            for n in range(tr // SG_CHUNK):
                rows = slice(n * SG_CHUNK, (n + 1) * SG_CHUNK)
                dvo = dvo_sc[rows, cols]
                dvob = dvo.astype(BF16)
                dvln_sc[rows, cols] = lax.dot_general(wg, dvob, _TN, preferred_element_type=F32)
                acc_w += lax.dot_general(dvob, vln_sc[rows, cols], _NT, preferred_element_type=F32)
                acc_b += jnp.sum(dvo, axis=1, keepdims=True)
            dws_ref[g] += jnp.where(mask, acc_w, 0.0)
            dbs_ref[g] += acc_b
        dvln = dvln_sc[...]
        sums_ref[...] += jnp.concatenate([jnp.sum(dvln * vhat, axis=0, keepdims=True),
                                          jnp.sum(dvln, axis=0, keepdims=True),
                                          jnp.zeros((6, W), F32)], axis=0)
        dv = _layer_norm_bwd(dvln * g_ref[...], vhat, rstd)
        da_ref[:, :W] = (du * _gelu_grad(ua, tu)).astype(BF16)
        da_ref[:, W:] = (dv * _gelu_grad(va, tv)).astype(BF16)

    row = pl.BlockSpec((1, W), lambda i: (0, 0))
    wspec = pl.BlockSpec((SG_GROUPS, SG_CHUNK, SG_CHUNK), lambda i: (0, 0, 0))
    bspec = pl.BlockSpec((SG_GROUPS, SG_CHUNK, 1), lambda i: (0, 0, 0))
    return pl.pallas_call(
        body, name=name, grid=(T // tr,),
        in_specs=[pl.BlockSpec((tr, W), lambda i: (i, 0)), pl.BlockSpec((tr, W), lambda i: (i, 1)),
                  pl.BlockSpec((tr, W), lambda i: (i, 0)), row, row, wspec, bspec],
        out_specs=[pl.BlockSpec((tr, 2 * W), lambda i: (i, 0)), wspec, bspec,
                   pl.BlockSpec((8, W), lambda i: (0, 0))],
        out_shape=[jax.ShapeDtypeStruct((T, 2 * W), BF16),
                   jax.ShapeDtypeStruct((SG_GROUPS, SG_CHUNK, SG_CHUNK), F32),
                   jax.ShapeDtypeStruct((SG_GROUPS, SG_CHUNK, 1), F32),
                   jax.ShapeDtypeStruct((8, W), F32)],
        scratch_shapes=[pltpu.VMEM((tr, W), BF16), pltpu.VMEM((tr, W), F32),
                        pltpu.VMEM((tr, W), F32), pltpu.VMEM((tr, W), F32)],
        compiler_params=_cparams("arbitrary"),
    )(a_uv, a_uv, dgate, ln_g, ln_b, ws, bcol)


SUBLANES = 8


def _shift_rows(xc_sc, xs_sc):
    rows = xs_sc.shape[1]
    for p in range(1, SUBLANES):
        xs_sc[p - 1] = xc_sc[pl.ds(p, rows), :]


def _rows_at(xc_sc, xs_sc, offset, tr):
    p = offset % SUBLANES
    base = offset - p
    return xc_sc[pl.ds(base, tr), :] if p == 0 else xs_sc[p - 1, pl.ds(base, tr), :]


def _shift_scratch(tr, C):
    return pltpu.VMEM((SUBLANES - 1, tr + CONV_HALO - SUBLANES, C), F32)


def _cv_glu_conv(a_ref, b_ref, ap_ref, bp_ref, w_ref, bd_ref, xc_sc, xs_sc, tr):
    i = pl.program_id(0)
    prev = ap_ref[...] * jax.nn.sigmoid(bp_ref[...])
    xc_sc[0:CONV_HALO, :] = jnp.where(i > 0, prev, 0.0)
    xc_sc[CONV_HALO:, :] = a_ref[...] * jax.nn.sigmoid(b_ref[...])
    _shift_rows(xc_sc, xs_sc)
    acc = jnp.broadcast_to(bd_ref[...], (tr, bd_ref.shape[1]))
    for k in range(CONV_WIDTH):
        acc = acc + w_ref[k:k + 1, :] * _rows_at(xc_sc, xs_sc, CONV_HALO - (CONV_WIDTH - 1) + k, tr)
    return acc


def _cv_specs(T, C, tr):
    hb = tr // CONV_HALO
    cur = lambda col: pl.BlockSpec((tr, C), lambda i: (i, col))
    prev = lambda col: pl.BlockSpec((CONV_HALO, C), lambda i: (jnp.maximum(i * hb - 1, 0), col))
    row = pl.BlockSpec((1, C), lambda i: (0, 0))
    wspec = pl.BlockSpec((CONV_HALO, C), lambda i: (0, 0))
    return cur, prev, row, wspec


def _cv_fwd(p, w_dw, b_dw, ln_g, ln_b, *, name):
    T, C = p.shape[0], p.shape[1] // 2
    tr = _row_tile(T)
    cur, prev, row, wspec = _cv_specs(T, C, tr)

    def body(a_ref, b_ref, ap_ref, bp_ref, w_ref, bd_ref, g_ref, be_ref, o_ref, xc_sc, xs_sc):
        y2 = _cv_glu_conv(a_ref, b_ref, ap_ref, bp_ref, w_ref, bd_ref, xc_sc, xs_sc, tr)
        yhat, _ = _layer_norm_stats(y2)
        yln = yhat * g_ref[...] + be_ref[...]
        o_ref[...] = (yln * jax.nn.sigmoid(yln)).astype(BF16)

    return pl.pallas_call(
        body, name=name, grid=(T // tr,),
        in_specs=[cur(0), cur(1), prev(0), prev(1), wspec, row, row, row],
        out_specs=pl.BlockSpec((tr, C), lambda i: (i, 0)),
        out_shape=jax.ShapeDtypeStruct((T, C), BF16),
        scratch_shapes=[pltpu.VMEM((tr + CONV_HALO, C), F32), _shift_scratch(tr, C)],
        compiler_params=_cparams("parallel"),
    )(p, p, p, p, w_dw, b_dw, ln_g, ln_b)


def _cv_bwd_ln(p, dy3, w_dw, b_dw, ln_g, ln_b, *, name):
    T, C = p.shape[0], p.shape[1] // 2
    tr = _row_tile(T)
    cur, prev, row, wspec = _cv_specs(T, C, tr)

    def body(a_ref, b_ref, ap_ref, bp_ref, dy_ref, w_ref, bd_ref, g_ref, be_ref,
             dy2_ref, dw_ref, sums_ref, xc_sc, xs_sc):
        i = pl.program_id(0)
        y2 = _cv_glu_conv(a_ref, b_ref, ap_ref, bp_ref, w_ref, bd_ref, xc_sc, xs_sc, tr)
        yhat, rstd = _layer_norm_stats(y2)
        yln = yhat * g_ref[...] + be_ref[...]
        s = jax.nn.sigmoid(yln)
        dyln = dy_ref[...] * (s + yln * s * (1.0 - s))
        dy2 = _layer_norm_bwd(dyln * g_ref[...], yhat, rstd)
        dy2_ref[...] = dy2
        sums = jnp.concatenate([jnp.sum(dy2, axis=0, keepdims=True),
                                jnp.sum(dyln * yhat, axis=0, keepdims=True),
                                jnp.sum(dyln, axis=0, keepdims=True),
                                jnp.zeros((5, C), F32)], axis=0)
        taps = [jnp.sum(dy2 * _rows_at(xc_sc, xs_sc, CONV_HALO - (CONV_WIDTH - 1) + k, tr), axis=0, keepdims=True)
                for k in range(CONV_WIDTH)]
        dw = jnp.concatenate(taps + [jnp.zeros((CONV_HALO - CONV_WIDTH, C), F32)], axis=0)

        @pl.when(i == 0)
        def _():
            sums_ref[...] = sums
            dw_ref[...] = dw

        @pl.when(i > 0)
        def _():
            sums_ref[...] += sums
            dw_ref[...] += dw

    blk = pl.BlockSpec((tr, C), lambda i: (i, 0))
    return pl.pallas_call(
        body, name=name, grid=(T // tr,),
        in_specs=[cur(0), cur(1), prev(0), prev(1), blk, wspec, row, row, row],
        out_specs=[blk, wspec, pl.BlockSpec((8, C), lambda i: (0, 0))],
        out_shape=[jax.ShapeDtypeStruct((T, C), F32), jax.ShapeDtypeStruct((CONV_HALO, C), F32),
                   jax.ShapeDtypeStruct((8, C), F32)],
        scratch_shapes=[pltpu.VMEM((tr + CONV_HALO, C), F32), _shift_scratch(tr, C)],
        compiler_params=_cparams("arbitrary"),
    )(p, p, p, p, dy3, w_dw, b_dw, ln_g, ln_b)


def _cv_bwd_in(p, dy2, w_dw, *, name):
    T, C = p.shape[0], p.shape[1] // 2
    tr = _row_tile(T)
    hb = tr // CONV_HALO
    nblk = T // tr
    last_halo = T // CONV_HALO - 1

    def body(a_ref, b_ref, dy_ref, dyn_ref, w_ref, dp_ref, sums_ref, xc_sc, xs_sc):
        i = pl.program_id(0)
        xc_sc[0:tr, :] = dy_ref[...]
        xc_sc[tr:, :] = jnp.where(i < nblk - 1, dyn_ref[...], 0.0)
        _shift_rows(xc_sc, xs_sc)
        dy1 = jnp.zeros((tr, C), F32)
        for k in range(CONV_WIDTH):
            dy1 = dy1 + w_ref[k:k + 1, :] * _rows_at(xc_sc, xs_sc, CONV_WIDTH - 1 - k, tr)
        a = a_ref[...]
        sb = jax.nn.sigmoid(b_ref[...])
        da = dy1 * sb
        db = dy1 * a * sb * (1.0 - sb)
        dp_ref[:, :C] = da.astype(BF16)
        dp_ref[:, C:] = db.astype(BF16)
        sums = jnp.concatenate([
            jnp.concatenate([jnp.sum(da, axis=0, keepdims=True), jnp.sum(db, axis=0, keepdims=True)], axis=1),
            jnp.zeros((7, 2 * C), F32)], axis=0)

        @pl.when(i == 0)
        def _():
            sums_ref[...] = sums

        @pl.when(i > 0)
        def _():
            sums_ref[...] += sums

    blk = lambda col: pl.BlockSpec((tr, C), lambda i: (i, col))
    return pl.pallas_call(
        body, name=name, grid=(nblk,),
        in_specs=[blk(0), blk(1), blk(0),
                  pl.BlockSpec((CONV_HALO, C), lambda i: (jnp.minimum((i + 1) * hb, last_halo), 0)),
                  pl.BlockSpec((CONV_HALO, C), lambda i: (0, 0))],
        out_specs=[pl.BlockSpec((tr, 2 * C), lambda i: (i, 0)), pl.BlockSpec((8, 2 * C), lambda i: (0, 0))],
        out_shape=[jax.ShapeDtypeStruct((T, 2 * C), BF16), jax.ShapeDtypeStruct((8, 2 * C), F32)],
        scratch_shapes=[pltpu.VMEM((tr + CONV_HALO, C), F32), _shift_scratch(tr, C)],
        compiler_params=_cparams("arbitrary"),
    )(p, p, dy2, dy2, w_dw)


def _col_tile(n, want=1024):
    best = LANE
    for t in range(LANE, min(n, want) + 1, LANE):
        if n % t == 0:
            best = t
    return best if n % LANE == 0 else n


def _mm(a, b, *, name, ta=False, tb=False, **kw):
    M = a.shape[1] if ta else a.shape[0]
    N = b.shape[0] if tb else b.shape[1]
    K = a.shape[0] if ta else a.shape[1]
    kw.setdefault('tm', _col_tile(M, 1024 if ta else 512))
    kw.setdefault('tn', _col_tile(N, 1024))
    kw.setdefault('tk', K if tb else _col_tile(K, 2048 if ta else 1024))
    return _matmul(a, b, name=name, ta=ta, tb=tb, **kw)


def _relu2_epilogue(acc):
    r = jnp.maximum(acc, 0.0)
    return (r * r,)


def _residual_epilogue(acc, x, g):
    return acc, x + g * acc


def _residual_bias_epilogue(acc, x, g, b):
    y = acc + b
    return y, x + g * y


def _dh_norm_bwd(d_act, w, x, dres, nw, sc, *, name, gate=None, below=None):
    D = x.shape[1]

    def epilogue(dh, xv, dresv, wv, scv, *more):
        gated, under = (more[:2], more[2:]) if gate else ((), more)
        r = lax.rsqrt(jnp.mean(xv * xv, axis=-1, keepdims=True) + EPS)
        n = xv * r
        scale = 1.0 + scv
        dn = dh * (wv * scale)
        dx = dresv + r * (dn - n * jnp.mean(dn * n, axis=-1, keepdims=True))
        rows = [jnp.sum(dh, axis=0, keepdims=True),
                jnp.sum(dh * (n * wv), axis=0, keepdims=True),
                jnp.sum(dh * n * scale, axis=0, keepdims=True)]
        outs = [dx]
        if gated:
            yv, gv = gated
            outs.append(dx * gv)
            rows += [jnp.sum(dx * yv, axis=0, keepdims=True), jnp.sum(dx * gv, axis=0, keepdims=True)]
        rows += [jnp.zeros((7 - len(rows), D), F32)]
        if under:
            zv, gzv = under
            outs.append(dx * gzv)
            rows.append(jnp.sum(dx * zv, axis=0, keepdims=True))
        else:
            rows.append(jnp.zeros((1, D), F32))
        return (*outs, jnp.concatenate(rows, axis=0))

    extras = [(x, 'tile'), (dres, 'tile'), (nw, 'row'), (sc, 'row')]
    for branch in (gate, below):
        if branch:
            extras += [(branch[0], 'tile'), (branch[1], 'row')]
    return _mm(d_act, w, tb=True, name=name, tn=D, extras=extras, epilogue=epilogue,
               out_dtypes=(F32,) + (BF16,) * (bool(gate) + bool(below)), row_sums=True)


def _relu2_bwd_epilogue(acc, r):
    return (acc * (2.0 * jnp.sqrt(r.astype(F32))),)


def _bias_epilogue(acc, b):
    return (acc + b,)


def _fox_forward(h1, P, j, D, carried=None):
    carried = carried or {}

    def ride(kernel):
        return carried[kernel][0] if kernel in carried else None

    def landed(kernel, outs):
        if kernel in carried:
            carried[kernel][1](outs)

    H = D // HEAD_DIM
    proj = _mm(h1, P['fox_w_in'][j], name='fox_proj', b_outer=True, tm=WIDE_ROWS)
    qg = jnp.tile(P['fox_q_norm'][j][None, :], (1, 2))
    kg = jnp.tile(P['fox_k_norm'][j][None, :], (1, 2))
    fpre_t = proj[:, 3 * D:3 * D + H].T
    bf = P['fox_b_f'][j][:, None]
    fcum = _fox_gate_fwd(fpre_t, bf, name='fox_gate_fwd')
    (qa, qta, ka, kta, va, vta), outs = _fox_prep_fwd(proj, qg, kg, fcum[:, :, None], d_model=D, name='fox_prep_fwd',
                                                     comm=ride('prep'))
    landed('prep', outs)
    o, qb, outs = _fox_attn_fwd(qa, kta, va, name='fox_attn_fwd', comm=ride('attn'))
    landed('attn', outs)
    saved = dict(proj=proj, qg=qg, kg=kg, fpre_t=fpre_t, bf=bf, o=o, qb=qb, qta=qta, ka=ka, kta=kta, vta=vta)
    return o, saved


def _fox_backward(dy, h1, S, P, j, D, comm=None):
    H = D // HEAD_DIM
    w_out, w_in = P['fox_w_out'][j], P['fox_w_in'][j]
    g = {}
    g['fox_w_out'] = _mm(S['o'], dy, ta=True, name='fox_dw_out')
    do = _mm(dy, w_out, tb=True, name='fox_do')
    doa, dota = _fox_do_prep(do, S['o'], name='fox_do_prep')
    dq, dkt, dvt, dcol, drow, comm_outs = _fox_attn_bwd(S['qb'], S['qta'], S['ka'], S['kta'], S['vta'], doa, dota,
                                                        name='fox_attn_bwd', comm=comm)
    dqp, dkp, dvp, gsum = _fox_prep_bwd(S['proj'], dq, dkt, dvt, S['qg'], S['kg'], d_model=D, name='fox_prep_bwd')
    dfpre_t, dbf = _fox_gate_bwd(dcol[:, 0, :], drow[:, :, 0], S['fpre_t'], S['bf'], name='fox_gate_bwd')
    dfpre = jnp.pad(dfpre_t.T.astype(BF16), ((0, 0), (0, LANE - H)))
    dproj = jnp.concatenate([dqp, dkp, dvp, dfpre], axis=1)
    g['fox_w_in'] = _mm(h1, dproj, ta=True, name='fox_dw_in')[:, :3 * D + H]
    g['fox_b_f'] = dbf[:, 0]
    g['fox_q_norm'] = gsum[0, :HEAD_DIM]
    g['fox_k_norm'] = gsum[1, :HEAD_DIM]
    return (dproj, w_in), g, comm_outs


def _sg_forward(h1, P, D):
    a_uv = _mm(h1, P['sg_w_in'], name='sg_in', b_outer=True, tm=WIDE_ROWS)
    bcol = P['sg_b_s'][:, :, None]
    gate = _sg_fwd(a_uv, P['sg_ln_g'], P['sg_ln_b'], P['sg_w_s'], bcol, name='sg_fwd')
    return gate, dict(a_uv=a_uv, bcol=bcol, gate=gate)


def _sg_backward(dy, h1, S, P, D):
    g = {}
    g['sg_w_out'] = _mm(S['gate'], dy, ta=True, name='sg_dw_out')
    dgate = _mm(dy, P['sg_w_out'], tb=True, name='sg_dgate')
    da, dws, dbs, sums = _sg_bwd(S['a_uv'], dgate, P['sg_ln_g'], P['sg_ln_b'], P['sg_w_s'], S['bcol'],
                                 name='sg_bwd')
    g['sg_w_s'], g['sg_b_s'] = dws, dbs[:, :, 0]
    g['sg_ln_g'], g['sg_ln_b'] = sums[0], sums[1]
    g['sg_w_in'] = _mm(h1, da, ta=True, name='sg_dw_in', out_chips=N_CHIPS)
    return (da, P['sg_w_in']), g


def _cv_forward(h1, P, D):
    p = _mm(h1, P['cv_w_pw1'], name='cv_pw1', extras=[(P['cv_b_pw1'], 'row')], epilogue=_bias_epilogue,
            b_outer=True, tm=WIDE_ROWS)
    w_dw = jnp.pad(P['cv_w_dw'], ((0, CONV_HALO - CONV_WIDTH), (0, 0)))
    y3 = _cv_fwd(p, w_dw, P['cv_b_dw'], P['cv_ln_g'], P['cv_ln_b'], name='cv_fwd')
    return y3, dict(p=p, w_dw=w_dw, y3=y3)


def _cv_backward(dy, h1, S, P, D):
    g = {}
    g['cv_w_pw2'] = _mm(S['y3'], dy, ta=True, name='cv_dw_pw2')
    dy3 = _mm(dy, P['cv_w_pw2'], tb=True, name='cv_dy3')
    dy2, dw, sums = _cv_bwd_ln(S['p'], dy3, S['w_dw'], P['cv_b_dw'], P['cv_ln_g'], P['cv_ln_b'], name='cv_bwd_ln')
    g['cv_w_dw'] = dw[:CONV_WIDTH]
    g['cv_b_dw'], g['cv_ln_g'], g['cv_ln_b'] = sums[0], sums[1], sums[2]
    dp, psum = _cv_bwd_in(S['p'], dy2, S['w_dw'], name='cv_bwd_in')
    g['cv_b_pw1'] = psum[0]
    g['cv_w_pw1'] = _mm(h1, dp, ta=True, name='cv_dw_pw1', out_chips=N_CHIPS)
    return (dp, P['cv_w_pw1']), g


class Hooks(NamedTuple):
    fwd: dict
    bwd_comm: Callable
    bwd_done: Callable


def _local_step(x, target, mod, P, hooks=None):
    T, D = x.shape
    L = mod.shape[0]
    saved = []
    for i in range(L):
        kind, j = i % N_MIXERS, i // N_MIXERS
        m = [mod[i:i + 1, k * D:(k + 1) * D] for k in range(6)]
        sh_m, sc_m, g_m, sh_f, sc_f, g_f = m
        w_mix, w_mlp = P['norm_mix'][i:i + 1], P['norm_mlp'][i:i + 1]
        h1 = _norm_mod_fwd(x, w_mix, sc_m, sh_m, name='norm_mix_fwd')
        if kind == 0:
            op, S = _fox_forward(h1, P, j, D, carried=hooks.fwd.get(i) if hooks else None)
            y, x1 = _mm(op, P['fox_w_out'][j], name='fox_out', extras=[(x, 'tile'), (g_m, 'row')],
                        epilogue=_residual_epilogue, out_dtypes=(F32, F32))
        elif kind == 1:
            op, S = _sg_forward(h1, P, D)
            y, x1 = _mm(op, P['sg_w_out'], name='sg_out', extras=[(x, 'tile'), (g_m, 'row')],
                        epilogue=_residual_epilogue, out_dtypes=(F32, F32))
        else:
            op, S = _cv_forward(h1, P, D)
            y, x1 = _mm(op, P['cv_w_pw2'], name='cv_out',
                        extras=[(x, 'tile'), (g_m, 'row'), (P['cv_b_pw2'], 'row')],
                        epilogue=_residual_bias_epilogue, out_dtypes=(F32, F32))
        h2 = _norm_mod_fwd(x1, w_mlp, sc_f, sh_f, name='norm_mlp_fwd')
        r = _mm(h2, P['w_mlp_in'][i], name='mlp_in', epilogue=_relu2_epilogue, out_dtypes=(BF16,),
                b_outer=True, tm=WIDE_ROWS)
        z, x2 = _mm(r, P['w_mlp_out'][i], name='mlp_out', extras=[(x1, 'tile'), (g_f, 'row')],
                    epilogue=_residual_epilogue, out_dtypes=(F32, F32), tk=P['w_mlp_out'][i].shape[0])
        saved.append(dict(x=x, h1=h1, S=S, y=y, x1=x1, h2=h2, r=r, z=z, m=m))
        x = x2

    loss_part, dx = _loss_head(x, target, name='loss_head')

    grads = {k: [None] * L for k in ('norm_mix', 'norm_mlp')}
    mix_grads, mat = {}, {}
    dmod = [None] * L
    for i in reversed(range(L)):
        kind, j = i % N_MIXERS, i // N_MIXERS
        sv = saved[i]
        sh_m, sc_m, g_m, sh_f, sc_f, g_f = sv['m']
        w_mix, w_mlp = P['norm_mix'][i:i + 1], P['norm_mlp'][i:i + 1]
        if i == L - 1:
            dz, dgf = _gate_bwd(dx, sv['z'], g_f, name='mlp_gate_bwd')
            dgf = dgf[0]
        mat['w_mlp_out', i] = _mm(sv['r'], dz, ta=True, name='mlp_dw_out')
        da = _mm(dz, P['w_mlp_out'][i], tb=True, name='mlp_da', extras=[(sv['r'], 'tile')],
                 epilogue=_relu2_bwd_epilogue, out_dtypes=(BF16,), b_outer=True, tm=WIDE_ROWS)
        mat['w_mlp_in', i] = _mm(sv['h2'], da, ta=True, name='mlp_dw_in', out_chips=N_CHIPS)
        dx1, dy, sums_f = _dh_norm_bwd(da, P['w_mlp_in'][i], sv['x1'], dx, w_mlp, sc_f, name='mlp_dh',
                                       gate=(sv['y'], g_m))
        if kind == 0:
            carried = hooks is not None and i == 0
            last, g, comm_outs = _fox_backward(dy, sv['h1'], sv['S'], P, j, D,
                                              comm=hooks.bwd_comm(mat) if carried else None)
            if carried:
                hooks.bwd_done(comm_outs)
        elif kind == 1:
            last, g = _sg_backward(dy, sv['h1'], sv['S'], P, D)
        else:
            last, g = _cv_backward(dy, sv['h1'], sv['S'], P, D)
            g['cv_b_pw2'] = sums_f[4]
        for k, val in g.items():
            if k in BIG:
                mat[k, j] = val
            else:
                mix_grads.setdefault(k, {})[j] = val
        if i > 0:
            dx, dz, sums_m = _dh_norm_bwd(*last, sv['x'], dx1, w_mix, sc_m, name='mix_dh',
                                          below=(saved[i - 1]['z'], saved[i - 1]['m'][5]))
        else:
            dx, sums_m = _dh_norm_bwd(*last, sv['x'], dx1, w_mix, sc_m, name='mix_dh')
        grads['norm_mlp'][i], grads['norm_mix'][i] = sums_f[2], sums_m[2]
        dmod[i] = jnp.concatenate([sums_m[0], sums_m[1], sums_f[3], sums_f[0], sums_f[1], dgf])
        dgf = sums_m[7]

    out = {k: jnp.stack(v) for k, v in grads.items()}
    for k, per_j in mix_grads.items():
        out[k] = jnp.stack([per_j[j] for j in sorted(per_j)])
    return loss_part, dx, jnp.stack(dmod), out, mat


def _all_gather8(blocks, *, name):
    n = len(blocks)

    def body(*refs):
        x_refs, out_refs = refs[:n], refs[n:2 * n]
        send_sems, recv_sems, local_sems = refs[2 * n:]
        x, y, c = _position()
        me, sibling = (x, y, c), (x, y, 1 - c)
        chips = [(1 - x, y), (x, 1 - y), (1 - x, 1 - y)]

        def slot(a, px, py, pc):
            return out_refs[a].at[4 * px + 2 * py + pc]

        def copy(a, k, blk, to, src=None):
            return pltpu.make_async_remote_copy(
                src_ref=slot(a, *blk) if src is None else src, dst_ref=slot(a, *blk),
                send_sem=send_sems.at[7 * a + k], recv_sem=recv_sems.at[7 * a + k],
                device_id=to, device_id_type=MESH)

        mine = [pltpu.make_async_copy(x_refs[a], slot(a, *me), local_sems.at[a]) for a in range(n)]
        for cp in mine:
            cp.start()
        first = []
        for j, chip in enumerate(chips):
            first += [copy(a, 1 + j, me, (*chip, c), src=x_refs[a]) for a in range(n)]
        first += [copy(a, 0, me, sibling, src=x_refs[a]) for a in range(n)]
        for cp in first:
            cp.start()
        passed = []
        for j, chip in enumerate(chips):
            for a in range(n):
                copy(a, 1 + j, (*chip, c), me).wait_recv()
                passed.append(copy(a, 4 + j, (*chip, c), sibling))
                passed[-1].start()
        for a in range(n):
            copy(a, 0, sibling, me).wait_recv()
        for j, chip in enumerate(chips):
            for a in range(n):
                copy(a, 4 + j, (*chip, 1 - c), me).wait_recv()
        for cp in first + passed:
            cp.wait_send()
        for cp in mine:
            cp.wait()

    return pl.pallas_call(
        body, name=name, in_specs=[ANY] * n, out_specs=[ANY] * n,
        out_shape=[jax.ShapeDtypeStruct((8,) + b.shape, b.dtype) for b in blocks],
        scratch_shapes=[pltpu.SemaphoreType.DMA((7 * n,)), pltpu.SemaphoreType.DMA((7 * n,)),
                        pltpu.SemaphoreType.DMA((n,))],
    )(*blocks)


def _exchange(comm, *, name, aliases=None):
    ns, no = len(comm.srcs), len(comm.out_shapes)

    def body(*refs):
        copies = _comm_copies(comm.plan, refs[:ns], refs[ns:ns + no], *refs[ns + no:])
        _comm_start(copies)
        _comm_wait(copies)

    return pl.pallas_call(
        body, name=name, in_specs=[ANY] * ns, out_specs=[ANY] * no, out_shape=list(comm.out_shapes),
        scratch_shapes=comm.scratch(), input_output_aliases=aliases or {},
    )(*comm.srcs)


def _gather_comm(halves):
    n = len(halves)

    def plan(src, out, x, y, c):
        mine = 4 * x + 2 * y + c
        remote = [(src[a], out[a].at[mine], (x, y, 1 - c), out[a].at[4 * x + 2 * y + 1 - c]) for a in range(n)]
        for fx, fy in CHIP_FLIPS:
            px, py = _flip(x, fx), _flip(y, fy)
            remote += [(src[a], out[a].at[mine], (px, py, c), out[a].at[4 * px + 2 * py + c]) for a in range(n)]
        return remote, [(src[a], out[a].at[mine]) for a in range(n)]

    return Comm(list(halves), [jax.ShapeDtypeStruct((8,) + h.shape, h.dtype) for h in halves], plan, 4 * n, n)


def _gather_forward(bufs, *, name):
    n = len(bufs)

    def plan(src, out, x, y, c):
        remote = []
        for fx, fy in CHIP_FLIPS:
            px, py = _flip(x, fx), _flip(y, fy)
            remote += [(src[a].at[4 * px + 2 * py + c], out[a].at[4 * px + 2 * py + c], (x, y, 1 - c),
                        out[a].at[4 * px + 2 * py + 1 - c]) for a in range(n)]
        return remote, []

    comm = Comm(list(bufs), [jax.ShapeDtypeStruct(b.shape, b.dtype) for b in bufs], plan, 3 * n, 0)
    return _exchange(comm, name=name, aliases={a: a for a in range(n)})


CHIP_FLIPS = ((1, 0), (0, 1), (1, 1))


def _flip(v, f):
    return 1 - v if f else v


def _sum_rows_tile(R, C, budget=3 << 20):
    best = None
    for t in range(8, R + 1, 8):
        if R % t == 0 and t * C * 4 <= budget:
            best = t
    return best if best is not None else R


def _rs_begin(gps, *, wire_dtype):
    n = len(gps)
    c_arr = jnp.reshape(_position()[2], (1,)).astype(jnp.int32)

    def plan(src, out, x, y, c):
        return [(src[a].at[b, 1 - c], out[a].at[b], (x, y, 1 - c), out[a].at[b])
                for a in range(n) for b in range(4)], []

    got1 = _exchange(Comm(list(gps), [jax.ShapeDtypeStruct((4,) + g.shape[2:], F32) for g in gps], plan, 4 * n, 0),
                     name='rs_sibling')

    def sum_chip(c_ref, mine_ref, got_ref, out_ref):
        out_ref[...] = (mine_ref[...] + got_ref[...]).astype(out_ref.dtype)

    parts = []
    for gp, g1 in zip(gps, got1):
        _, _, R, C = gp.shape
        tr = _sum_rows_tile(R, C)
        parts.append(pl.pallas_call(
            sum_chip, name='rs_sum_chip',
            grid_spec=pltpu.PrefetchScalarGridSpec(
                num_scalar_prefetch=1, grid=(4, R // tr),
                in_specs=[pl.BlockSpec((None, None, tr, C), lambda b, r, cr: (b, cr[0], r, 0)),
                          pl.BlockSpec((None, tr, C), lambda b, r, cr: (b, r, 0))],
                out_specs=pl.BlockSpec((None, tr, C), lambda b, r, cr: (b, r, 0))),
            out_shape=jax.ShapeDtypeStruct((4, R, C), wire_dtype),
            compiler_params=_cparams("parallel", "parallel"),
        )(c_arr, gp, g1))
    return got1, parts


def _rs_chips_comm(parts):
    n = len(parts)

    def plan(src, out, x, y, c):
        remote = []
        for k, (fx, fy) in enumerate(CHIP_FLIPS):
            px, py = _flip(x, fx), _flip(y, fy)
            remote += [(src[a].at[2 * px + py], out[a].at[k], (px, py, c), out[a].at[k]) for a in range(n)]
        return remote, []

    return Comm(list(parts), [jax.ShapeDtypeStruct((3,) + p.shape[1:], p.dtype) for p in parts], plan, 3 * n, 0)


def _rs_finish(gps, got1, got2):
    n = len(gps)
    x, y, c = _position()
    bc_arr = jnp.stack([2 * x + y, c]).astype(jnp.int32)

    def sum_final(bc_ref, mine_ref, got1_ref, got2_ref, out_ref):
        acc = mine_ref[...] + got1_ref[...]
        for k in range(3):
            acc = acc + got2_ref[k].astype(F32)
        out_ref[...] = acc

    halves = []
    for gp, g1, g2 in zip(gps, got1, got2):
        _, _, R, C = gp.shape
        tr = _sum_rows_tile(R, C, budget=2 << 20)
        halves.append(pl.pallas_call(
            sum_final, name='rs_sum_final',
            grid_spec=pltpu.PrefetchScalarGridSpec(
                num_scalar_prefetch=1, grid=(R // tr,),
                in_specs=[pl.BlockSpec((None, None, tr, C), lambda r, bc: (bc[0], bc[1], r, 0)),
                          pl.BlockSpec((None, tr, C), lambda r, bc: (bc[0], r, 0)),
                          pl.BlockSpec((3, tr, C), lambda r, bc: (0, r, 0))],
                out_specs=pl.BlockSpec((None, tr, C), lambda r, bc: (bc[1], r, 0))),
            out_shape=jax.ShapeDtypeStruct((2, R, C), F32),
            compiler_params=_cparams("parallel"),
        )(bc_arr, gp, g1, g2))

    def plan(src, out, x, y, c):
        return [(src[a].at[c], out[a].at[c], (x, y, 1 - c), out[a].at[1 - c]) for a in range(n)], []

    comm = Comm(halves, [jax.ShapeDtypeStruct(h.shape, F32) for h in halves], plan, n, 0)
    return _exchange(comm, name='rs_swap', aliases={a: a for a in range(n)})


def _sum8(gathered, *, name):
    _, R, C = gathered.shape

    def body(g_ref, o_ref):
        acc = g_ref[0]
        for k in range(1, 8):
            acc = acc + g_ref[k]
        o_ref[...] = acc

    return pl.pallas_call(body, name=name, out_shape=jax.ShapeDtypeStruct((R, C), F32))(gathered)


def _adamw(w, g, m, v, *, name):
    shape = w.shape
    cols = shape[-1]
    rows = w.size // cols
    tr = _sum_rows_tile(rows, cols, budget=1 << 20)

    def body(w_ref, g_ref, m_ref, v_ref, d_ref, mo_ref, vo_ref):
        gv = g_ref[...]
        mn = ADAM_B1 * m_ref[...] + (1.0 - ADAM_B1) * gv
        vn = ADAM_B2 * v_ref[...] + (1.0 - ADAM_B2) * (gv * gv)
        m_hat = mn / (1.0 - ADAM_B1 ** ADAM_STEP)
        v_hat = vn / (1.0 - ADAM_B2 ** ADAM_STEP)
        d_ref[...] = -ADAM_LR * (m_hat / (jnp.sqrt(v_hat) + ADAM_EPS) + ADAM_WD * w_ref[...])
        mo_ref[...] = mn
        vo_ref[...] = vn

    blk = pl.BlockSpec((tr, cols), lambda i: (i, 0))
    outs = pl.pallas_call(
        body, name=name, grid=(rows // tr,), in_specs=[blk] * 4, out_specs=[blk] * 3,
        out_shape=[jax.ShapeDtypeStruct((rows, cols), F32)] * 3,
        compiler_params=_cparams("parallel"),
    )(*[a.reshape(rows, cols) for a in (w, g, m, v)])
    return tuple(o.reshape(shape) for o in outs)


WEIGHTS = ['norm_mix', 'norm_mlp', 'w_ada', 'b_ada', 'w_mlp_in', 'w_mlp_out', 'fox_w_in', 'fox_b_f',
           'fox_q_norm', 'fox_k_norm', 'fox_w_out', 'sg_w_in', 'sg_ln_g', 'sg_ln_b', 'sg_w_s', 'sg_b_s',
           'sg_w_out', 'cv_w_pw1', 'cv_b_pw1', 'cv_w_dw', 'cv_b_dw', 'cv_ln_g', 'cv_ln_b', 'cv_w_pw2',
           'cv_b_pw2']
BIG = {'w_mlp_in': 2, 'w_mlp_out': 1, 'fox_w_in': 2, 'fox_w_out': 1, 'sg_w_in': 2, 'sg_w_out': 1,
       'cv_w_pw1': 2, 'cv_w_pw2': 1}
SMALL_SHARDED = ['cv_b_pw1', 'cv_w_dw', 'cv_b_dw', 'cv_ln_g', 'cv_ln_b', 'cv_b_pw2']
SMALL_GRADS = ['norm_mix', 'norm_mlp', 'fox_b_f', 'fox_q_norm', 'fox_k_norm', 'sg_ln_g', 'sg_ln_b', 'sg_w_s',
               'sg_b_s'] + SMALL_SHARDED
GRAD_WIRE_DTYPE = BF16


def _pack_rows(parts, cols):
    flat = jnp.concatenate([p.reshape(-1) for p in parts])
    rows = -(-flat.size // (8 * cols)) * 8
    return jnp.pad(flat, (0, rows * cols - flat.size)).reshape(rows, cols)


def _unpack(flat, shapes):
    out, off = [], 0
    for s in shapes:
        n = math.prod(s)
        out.append(flat[..., off:off + n].reshape(flat.shape[:-1] + tuple(s)))
        off += n
    return out


def _merge_chips(a, axis):
    a = jnp.moveaxis(a, 0, axis)
    return a.reshape(a.shape[:axis] + (a.shape[axis] * a.shape[axis + 1],) + a.shape[axis + 2:])


def _split_chips(a, axis):
    a = a.reshape(a.shape[:axis] + (4, a.shape[axis] // 4) + a.shape[axis + 1:])
    return jnp.moveaxis(a, axis, 0)


def _step(a):
    x, y, c = _position()
    me = 4 * x + 2 * y + c
    chip = 2 * x + y
    T, D = a['x'].shape[1], a['x'].shape[2]
    L = a['norm_mix'].shape[0]

    small_shapes = [(D,)] + [a[n].shape for n in SMALL_SHARDED]
    small = _all_gather8([_pack_rows([a['c']] + [a[n] for n in SMALL_SHARDED], LANE)], name='ag_small')[0]
    small = small.reshape(8, -1)
    c_all = _unpack(small, small_shapes[:1])[0]
    sharded = _unpack(small[0::2, D:], small_shapes[1:])
    P = {n: _merge_chips(v, v.ndim - 2) for n, v in zip(SMALL_SHARDED, sharded)}

    c_act = _silu_rows(c_all, name='c_act')
    mod_cols = jnp.stack([
        _mm(c_act, a['w_ada'][i], name='ada_mod', tm=8, tn=_col_tile(a['w_ada'].shape[2], 768),
            extras=[(lax.dynamic_slice_in_dim(a['b_ada'][i:i + 1], chip * a['w_ada'].shape[2],
                                              a['w_ada'].shape[2], axis=1), 'row')],
            epilogue=_bias_epilogue)
        for i in range(L)])
    mod_all = _all_gather8([mod_cols.reshape(L * 8, -1)], name='ag_mod')[0].reshape(8, L, 8, -1)
    mod = lax.dynamic_index_in_dim(mod_all[0::2], me, axis=2, keepdims=False)
    mod = jnp.moveaxis(mod, 0, 1).reshape(L, 6 * D)

    units = _matrix_units(L)
    first, with_prep, with_attn, with_last = units[:1], units[1:4], units[4:-3], units[-3:]
    last, earlier = units[:2], units[2:]
    n_heads = D // HEAD_DIM

    def half_block(unit):
        blk = a[unit[0]][unit[1]]
        return lax.dynamic_index_in_dim(blk.astype(BF16).reshape(2, blk.shape[0] // 2, blk.shape[1]), c, axis=0,
                                        keepdims=False)

    def install(group, gathered):
        for (name, idx), gth in zip(group, gathered):
            blocks = gth.reshape((4,) + a[name].shape[1:])
            if name == 'fox_w_in':
                pad = jnp.zeros((blocks.shape[1], LANE - n_heads), BF16)
                full = jnp.concatenate([blocks[0], blocks[1], blocks[2], blocks[3], pad], axis=-1)
            else:
                full = _merge_chips(blocks, BIG[name] - 1)
            if name in ('w_mlp_in', 'w_mlp_out', 'fox_w_in', 'fox_w_out'):
                P.setdefault(name, {})[idx] = full
            else:
                P[name] = full

    install(first, _all_gather8([half_block(u) for u in first], name='ag_weights_first'))
    for n in ('sg_w_s', 'sg_b_s', 'cv_w_dw'):
        P[n] = (P[n] if n in P else a[n])[0]
    for n in ('norm_mix', 'norm_mlp', 'fox_b_f', 'fox_q_norm', 'fox_k_norm', 'sg_ln_g', 'sg_ln_b'):
        P[n] = a[n]

    def split_grad(unit, grad):
        name = unit[0]
        if name == 'fox_w_in':
            grad = grad[:, :a[name].shape[2] * N_CHIPS]
        blk = grad if grad.ndim == 3 else _split_chips(grad, BIG[name] - 1)
        return blk.reshape(N_CHIPS, 2, blk.shape[1] // 2, blk.shape[2])

    state = {}

    def riding(group):
        return (_gather_comm([half_block(u) for u in group]),
                lambda outs: install(group, _gather_forward(outs, name='ag_weights_forward')))

    def bwd_comm(mat):
        state['gps'] = [split_grad(u, mat[u]) for u in earlier]
        state['got1'], parts = _rs_begin(state['gps'], wire_dtype=GRAD_WIRE_DTYPE)
        return _rs_chips_comm(parts)

    def bwd_done(outs):
        state['got2'] = outs

    last_fox = N_MIXERS * ((L - 1) // N_MIXERS)
    hooks = Hooks({0: {'prep': riding(with_prep), 'attn': riding(with_attn)}, last_fox: {'attn': riding(with_last)}},
                  bwd_comm, bwd_done)
    loss_part, grad_x, dmod, g, mat = _local_step(a['x'][0], a['loss_target'][0], mod, P, hooks)

    small_g = [dmod, loss_part[0:1, 0:1]] + [g[n] for n in SMALL_GRADS]
    small_g_shapes = [s.shape for s in small_g]
    all_small = _all_gather8([_pack_rows(small_g, LANE)], name='ag_small_grads')[0]
    summed = _sum8(all_small, name='sum_small_grads').reshape(-1)
    sums = _unpack(summed, small_g_shapes)
    loss = sums[1][0, 0]
    grads = dict(zip(SMALL_GRADS, sums[2:]))
    grads['b_ada'] = sums[0]
    for n in SMALL_SHARDED:
        blk = a[n].shape[-1]
        grads[n] = lax.dynamic_slice_in_dim(grads[n], chip * blk, blk, axis=grads[n].ndim - 1)
    dmod_all = all_small.reshape(8, -1)[:, :dmod.size].reshape(8, L, 6 * D)
    cols = a['w_ada'].shape[2]
    dmod_cols = lax.dynamic_slice_in_dim(dmod_all, chip * cols, cols, axis=2)
    pad8 = lambda t: jnp.pad(t, ((0, LANE - 8), (0, 0)))
    c_act_pad = pad8(c_act)
    grads['w_ada'] = jnp.stack([
        _mm(c_act_pad, pad8(dmod_cols[:, i]), ta=True, name='ada_dw', tn=_col_tile(cols, 768))
        for i in range(L)])

    shards = dict(zip(earlier, _rs_finish(state['gps'], state['got1'], state['got2'])))
    gps = [split_grad(u, mat[u]) for u in last]
    got1, parts = _rs_begin(gps, wire_dtype=GRAD_WIRE_DTYPE)
    got2 = _exchange(_rs_chips_comm(parts), name='rs_chips')
    shards.update(zip(last, _rs_finish(gps, got1, got2)))
    for n in BIG:
        grads[n] = jnp.stack([shards[n, idx].reshape(a[n].shape[1:]) for idx in range(a[n].shape[0])])

    deltas, new_m, new_v = {}, {}, {}
    for n in WEIGHTS:
        deltas[n], new_m[n], new_v[n] = _adamw(a[n], grads[n], a['m_' + n], a['v_' + n], name='adamw')
    return (loss, grad_x[None], *[grads[n] for n in WEIGHTS], *[deltas[n] for n in WEIGHTS],
            *[new_m[n] for n in WEIGHTS], *[new_v[n] for n in WEIGHTS])


def _matrix_units(n_layers):
    mixers = (('fox_w_in', 'fox_w_out'), ('sg_w_in', 'sg_w_out'), ('cv_w_pw1', 'cv_w_pw2'))
    units = []
    for i in range(n_layers):
        units += [(n, i // N_MIXERS) for n in mixers[i % N_MIXERS]] + [('w_mlp_in', i), ('w_mlp_out', i)]
    return units


def _silu_rows(x, *, name):
    def body(x_ref, o_ref):
        xv = x_ref[...]
        o_ref[...] = (xv * jax.nn.sigmoid(xv)).astype(BF16)

    return pl.pallas_call(body, name=name, out_shape=jax.ShapeDtypeStruct(x.shape, BF16))(x)


def kernel(x, c, norm_mix, norm_mlp, w_ada, b_ada, w_mlp_in, w_mlp_out, fox_w_in, fox_b_f, fox_q_norm, fox_k_norm, fox_w_out, sg_w_in, sg_ln_g, sg_ln_b, sg_w_s, sg_b_s, sg_w_out, cv_w_pw1, cv_b_pw1, cv_w_dw, cv_b_dw, cv_ln_g, cv_ln_b, cv_w_pw2, cv_b_pw2, loss_target, m_norm_mix, m_norm_mlp, m_w_ada, m_b_ada, m_w_mlp_in, m_w_mlp_out, m_fox_w_in, m_fox_b_f, m_fox_q_norm, m_fox_k_norm, m_fox_w_out, m_sg_w_in, m_sg_ln_g, m_sg_ln_b, m_sg_w_s, m_sg_b_s, m_sg_w_out, m_cv_w_pw1, m_cv_b_pw1, m_cv_w_dw, m_cv_b_dw, m_cv_ln_g, m_cv_ln_b, m_cv_w_pw2, m_cv_b_pw2, v_norm_mix, v_norm_mlp, v_w_ada, v_b_ada, v_w_mlp_in, v_w_mlp_out, v_fox_w_in, v_fox_b_f, v_fox_q_norm, v_fox_k_norm, v_fox_w_out, v_sg_w_in, v_sg_ln_g, v_sg_ln_b, v_sg_w_s, v_sg_b_s, v_sg_w_out, v_cv_w_pw1, v_cv_b_pw1, v_cv_w_dw, v_cv_b_dw, v_cv_ln_g, v_cv_ln_b, v_cv_w_pw2, v_cv_b_pw2):
    return _step(dict(locals()))
```

```python
import math
from typing import Callable, NamedTuple

import jax
import jax.numpy as jnp
from jax import lax
from jax.experimental import pallas as pl
from jax.experimental.pallas import tpu as pltpu

F32 = jnp.float32
BF16 = jnp.bfloat16

EPS = 1e-6
HEAD_DIM = 64
LANE = 128
CONV_WIDTH = 31
CONV_HALO = 32
SG_CHUNK = 128
SG_CAUSAL = 64
SG_GROUPS = 8
N_MIXERS = 3
N_CHIPS = 4
VMEM_LIMIT = 56 * 1024 * 1024
NEG = -1e30

ADAM_LR = 0.001
ADAM_B1 = 0.9
ADAM_B2 = 0.999
ADAM_EPS = 1e-08
ADAM_WD = 0.01
ADAM_STEP = 10

MESH = pl.DeviceIdType.MESH
ANY = pl.BlockSpec(memory_space=pl.ANY)


def _cparams(*sem):
    return pltpu.CompilerParams(dimension_semantics=sem, vmem_limit_bytes=VMEM_LIMIT)


WIDE_ROWS = 1024


def _row_tile(t, want=512):
    return min(t, want)


def _matmul(a, b, *, name, ta=False, tb=False, tm=512, tn=1024, tk=1024,
            extras=(), epilogue=None, out_dtypes=(F32,), b_outer=False, out_chips=None, row_sums=False):
    M, K = (a.shape[1], a.shape[0]) if ta else a.shape
    N = b.shape[0] if tb else b.shape[1]
    assert (b.shape[1] if tb else b.shape[0]) == K
    n_own = N // out_chips if out_chips else N
    tm, tn, tk = min(tm, M), min(tn, n_own), min(tk, K)
    assert M % tm == 0 and n_own % tn == 0 and K % tk == 0, (name, M, N, K, tm, tn, tk)
    nk = K // tk

    def spec(shape, pick):
        if b_outer:
            return pl.BlockSpec(shape, lambda j, i, k: pick(i, j, k))
        return pl.BlockSpec(shape, pick)

    a_spec = spec((tk, tm), lambda i, j, k: (k, i)) if ta else spec((tm, tk), lambda i, j, k: (i, k))
    b_spec = spec((tn, tk), lambda i, j, k: (j, k)) if tb else spec((tk, tn), lambda i, j, k: (k, j))
    ex_specs = [spec((tm, tn), lambda i, j, k: (i, j)) if kind == 'tile' else spec((1, tn), lambda i, j, k: (0, j))
                for _, kind in extras]
    dims = (((0,) if ta else (1,), (1,) if tb else (0,)), ((), ()))
    n_ex, n_out = len(extras), len(out_dtypes) + bool(row_sums)
    assert not row_sums or N == tn

    def body(*refs):
        a_ref, b_ref = refs[0], refs[1]
        ex = refs[2:2 + n_ex]
        outs = refs[2 + n_ex:2 + n_ex + n_out]

        def finish(acc):
            vals = epilogue(acc, *[r[...] for r in ex]) if epilogue else (acc,)
            for o, v in zip(outs[:len(out_dtypes)], vals):
                o[...] = v.astype(o.dtype)
            if row_sums:
                row_tile = pl.program_id(1 if b_outer else 0)

                @pl.when(row_tile == 0)
                def _():
                    outs[-1][...] = vals[-1]

                @pl.when(row_tile > 0)
                def _():
                    outs[-1][...] += vals[-1]

        part = lax.dot_general(a_ref[...].astype(BF16), b_ref[...].astype(BF16), dims,
                               preferred_element_type=F32)
        if nk == 1:
            finish(part)
        else:
            acc_ref = refs[-1]
            k = pl.program_id(2)

            @pl.when(k == 0)
            def _():
                acc_ref[...] = part

            @pl.when(k > 0)
            def _():
                acc_ref[...] += part

            @pl.when(k == nk - 1)
            def _():
                finish(acc_ref[...])

    outs = pl.pallas_call(
        body, name=name,
        grid=(N // tn, M // tm, nk) if b_outer else (M // tm, N // tn, nk),
        in_specs=[a_spec, b_spec] + ex_specs,
        out_specs=[spec((None, tm, tn), lambda i, j, k: (j // (n_own // tn), i, j % (n_own // tn)))
                   if out_chips else spec((tm, tn), lambda i, j, k: (i, j)) for _ in out_dtypes]
        + ([spec((8, tn), lambda i, j, k: (0, j))] if row_sums else []),
        out_shape=[jax.ShapeDtypeStruct((out_chips, M, n_own) if out_chips else (M, N), dt) for dt in out_dtypes]
        + ([jax.ShapeDtypeStruct((8, N), F32)] if row_sums else []),
        scratch_shapes=[pltpu.VMEM((tm, tn), F32)] if nk > 1 else [],
        compiler_params=(_cparams("arbitrary", "arbitrary", "arbitrary") if row_sums
                         else _cparams("parallel", "parallel", "arbitrary")),
    )(a, b, *[arr for arr, _ in extras])
    return outs if n_out > 1 else outs[0]


def _norm_mod_fwd(x, w, sc, sh, *, name):
    T, D = x.shape
    tr = _row_tile(T, WIDE_ROWS)

    def body(x_ref, w_ref, sc_ref, sh_ref, h_ref):
        xv = x_ref[...]
        r = lax.rsqrt(jnp.mean(xv * xv, axis=-1, keepdims=True) + EPS)
        h_ref[...] = ((xv * r) * w_ref[...] * (1.0 + sc_ref[...]) + sh_ref[...]).astype(BF16)

    row = pl.BlockSpec((1, D), lambda i: (0, 0))
    return pl.pallas_call(
        body, name=name, grid=(T // tr,),
        in_specs=[pl.BlockSpec((tr, D), lambda i: (i, 0)), row, row, row],
        out_specs=pl.BlockSpec((tr, D), lambda i: (i, 0)),
        out_shape=jax.ShapeDtypeStruct((T, D), BF16),
        compiler_params=_cparams("parallel"),
    )(x, w, sc, sh)


def _gate_bwd(dx, y, g, *, name):
    T, D = dx.shape
    tr = _row_tile(T)

    def body(dx_ref, y_ref, g_ref, dy_ref, dg_ref):
        i = pl.program_id(0)
        dxv = dx_ref[...]
        dy_ref[...] = (dxv * g_ref[...]).astype(BF16)
        part = jnp.concatenate([jnp.sum(dxv * y_ref[...], axis=0, keepdims=True),
                                jnp.zeros((7, D), F32)], axis=0)

        @pl.when(i == 0)
        def _():
            dg_ref[...] = part

        @pl.when(i > 0)
        def _():
            dg_ref[...] += part

    blk = pl.BlockSpec((tr, D), lambda i: (i, 0))
    return pl.pallas_call(
        body, name=name, grid=(T // tr,),
        in_specs=[blk, blk, pl.BlockSpec((1, D), lambda i: (0, 0))],
        out_specs=[blk, pl.BlockSpec((8, D), lambda i: (0, 0))],
        out_shape=[jax.ShapeDtypeStruct((T, D), BF16), jax.ShapeDtypeStruct((8, D), F32)],
        compiler_params=_cparams("arbitrary"),
    )(dx, y, g)


def _loss_head(y, target, *, name):
    T, D = y.shape
    tr = _row_tile(T)

    def body(y_ref, t_ref, loss_ref, dy_ref):
        i = pl.program_id(0)
        e = y_ref[...] - t_ref[...]
        dy_ref[...] = e * (1.0 / D)
        part = jnp.full((8, LANE), 0.5 / D * jnp.sum(e * e), F32)

        @pl.when(i == 0)
        def _():
            loss_ref[...] = part

        @pl.when(i > 0)
        def _():
            loss_ref[...] += part

    blk = pl.BlockSpec((tr, D), lambda i: (i, 0))
    return pl.pallas_call(
        body, name=name, grid=(T // tr,), in_specs=[blk, blk],
        out_specs=[pl.BlockSpec((8, LANE), lambda i: (0, 0)), blk],
        out_shape=[jax.ShapeDtypeStruct((8, LANE), F32), jax.ShapeDtypeStruct((T, D), F32)],
        compiler_params=_cparams("arbitrary"),
    )(y, target)


def _position():
    return lax.axis_index("x"), lax.axis_index("y"), lax.axis_index("c")


class Comm(NamedTuple):
    srcs: list
    out_shapes: list
    plan: Callable
    n_remote: int
    n_local: int

    def scratch(self):
        return [pltpu.SemaphoreType.DMA((self.n_remote,)), pltpu.SemaphoreType.DMA((self.n_remote,)),
                pltpu.SemaphoreType.DMA((max(self.n_local, 1),))]


def _comm_copies(plan, src_refs, out_refs, send_sems, recv_sems, local_sems):
    x, y, c = _position()
    remote, local = plan(src_refs, out_refs, x, y, c)

    def copy(k, s, d, peer):
        return pltpu.make_async_remote_copy(src_ref=s, dst_ref=d, send_sem=send_sems.at[k],
                                            recv_sem=recv_sems.at[k], device_id=peer, device_id_type=MESH)

    sends = [copy(k, s, d, peer) for k, (s, d, peer, _) in enumerate(remote)]
    recvs = [copy(k, s, landing, peer) for k, (s, _, peer, landing) in enumerate(remote)]
    local_copies = [pltpu.make_async_copy(s, d, local_sems.at[i]) for i, (s, d) in enumerate(local)]
    return sends, recvs, local_copies


def _comm_start(copies):
    sends, _, local_copies = copies
    for cp in local_copies + sends:
        cp.start()


def _comm_wait(copies):
    sends, recvs, local_copies = copies
    for cp in recvs:
        cp.wait_recv()
    for cp in sends:
        cp.wait_send()
    for cp in local_copies:
        cp.wait()


def _split_comm_refs(refs, n_in, n_out, n_scratch, comm):
    ns, nd = (len(comm.srcs), len(comm.out_shapes)) if comm else (0, 0)
    cuts = [n_in, ns, n_out, nd, n_scratch]
    parts, at = [], 0
    for n in cuts:
        parts.append(refs[at:at + n])
        at += n
    return (*parts, refs[at:])


AUG_F = HEAD_DIM
AUG_LSE = HEAD_DIM + 6


def _half_cols(x, lo):
    return (jnp.sum(jnp.where(lo, x, 0.0), axis=-1, keepdims=True),
            jnp.sum(jnp.where(lo, 0.0, x), axis=-1, keepdims=True))


def _half_sums(x, lo):
    s_lo, s_hi = _half_cols(x, lo)
    return jnp.where(lo, s_lo, s_hi)


def _split3(x):
    a = x.astype(BF16).astype(F32)
    r = x - a
    b = r.astype(BF16).astype(F32)
    return a, b, (r - b).astype(BF16).astype(F32)


def _aug(lane, base, terms):
    out = jnp.zeros(lane.shape, F32)
    for i, t in enumerate(terms):
        out = jnp.where(lane == base + i, t, out)
    return out


def _head_lanes(x2, h):
    return x2 if h == 0 else pltpu.roll(x2, HEAD_DIM, 1)


def _fox_prep_fwd(proj, qg, kg, fcol, *, d_model, name, comm=None):
    T = proj.shape[0]
    nhp = d_model // LANE
    tr = _row_tile(T, 2 * WIDE_ROWS)

    def body(*refs):
        ((q_ref, k_ref, v_ref, qg_ref, kg_ref, f_ref), src_refs,
         (qa_ref, qta_ref, ka_ref, kta_ref, va_ref, vta_ref), dst_refs, _, sems) = _split_comm_refs(refs, 6, 6, 0, comm)
        if comm:
            @pl.when((pl.program_id(0) == 0) & (pl.program_id(1) == 0))
            def _():
                _comm_start(_comm_copies(comm.plan, src_refs, dst_refs, *sems))
        lane = lax.broadcasted_iota(jnp.int32, (tr, LANE), 1)
        lo = lane < HEAD_DIM

        def norm(xv, g):
            ms = _half_sums(xv * xv, lo) * (1.0 / HEAD_DIM)
            return (xv * lax.rsqrt(ms + EPS)) * g

        qn = norm(q_ref[...], qg_ref[...]) * (HEAD_DIM ** -0.5)
        kn = norm(k_ref[...], kg_ref[...])
        vv = v_ref[...]
        qa, ka, va, vta = [], [], [], []
        for h in range(2):
            f1, f2, f3 = _split3(f_ref[h])
            qa.append(jnp.where(lo, _head_lanes(qn, h), _aug(lane, AUG_F, [f1, f2, f3, 1.0, 1.0, 1.0])))
            ka.append(jnp.where(lo, _head_lanes(kn, h),
                                _aug(lane, AUG_F, [1.0, 1.0, 1.0, -f1, -f2, -f3, 1.0, 1.0, 1.0])))
            va.append(jnp.where(lo if h == 0 else jnp.logical_not(lo), vv, 0.0))
            vta.append(jnp.where(lo, _head_lanes(vv, h), _aug(lane, AUG_F, [1.0, 1.0, 1.0])))
        for parts, ref, tref in ((qa, qa_ref, qta_ref), (ka, ka_ref, kta_ref), (va, va_ref, None),
                                 (vta, None, vta_ref)):
            both = jnp.concatenate(parts, axis=1)
            if ref is not None:
                ref[...] = both.astype(BF16)
            if tref is not None:
                tref[...] = both.astype(BF16).T
        if comm:
            @pl.when((pl.program_id(0) == T // tr - 1) & (pl.program_id(1) == nhp - 1))
            def _():
                _comm_wait(_comm_copies(comm.plan, src_refs, dst_refs, *sems))

    gain = pl.BlockSpec((1, LANE), lambda i, h: (0, 0))
    rows = pl.BlockSpec((tr, 2 * LANE), lambda i, h: (i, h))
    cols = pl.BlockSpec((2 * LANE, tr), lambda i, h: (h, i))
    wide, tall = jax.ShapeDtypeStruct((T, 2 * d_model), BF16), jax.ShapeDtypeStruct((2 * d_model, T), BF16)
    outs = pl.pallas_call(
        body, name=name, grid=(T // tr, nhp),
        in_specs=[pl.BlockSpec((tr, LANE), lambda i, h: (i, h)),
                  pl.BlockSpec((tr, LANE), lambda i, h: (i, nhp + h)),
                  pl.BlockSpec((tr, LANE), lambda i, h: (i, 2 * nhp + h)), gain, gain,
                  pl.BlockSpec((2, tr, 1), lambda i, h: (h, i, 0))] + ([ANY] * len(comm.srcs) if comm else []),
        out_specs=[rows, cols, rows, cols, rows, cols] + ([ANY] * len(comm.out_shapes) if comm else []),
        out_shape=[wide, tall, wide, tall, wide, tall] + (list(comm.out_shapes) if comm else []),
        scratch_shapes=comm.scratch() if comm else [],
        compiler_params=_cparams("arbitrary", "arbitrary") if comm else _cparams("parallel", "parallel"),
    )(proj, proj, proj, qg, kg, fcol, *(comm.srcs if comm else []))
    return outs[:6], outs[6:]


def _fox_do_prep(do, o, *, name):
    T, D = do.shape
    nhp = D // LANE
    tr = _row_tile(T, 2 * WIDE_ROWS)

    def body(do_ref, o_ref, doa_ref, dota_ref):
        lane = lax.broadcasted_iota(jnp.int32, (tr, LANE), 1)
        lo = lane < HEAD_DIM
        dob = do_ref[...].astype(BF16).astype(F32)
        deltas = _half_cols(dob * o_ref[...], lo)
        both = jnp.concatenate(
            [jnp.where(lo, _head_lanes(dob, h), _aug(lane, AUG_F, _split3(-deltas[h]))) for h in range(2)], axis=1)
        doa_ref[...] = both.astype(BF16)
        dota_ref[...] = both.astype(BF16).T

    blk = pl.BlockSpec((tr, LANE), lambda i, h: (i, h))
    return pl.pallas_call(
        body, name=name, grid=(T // tr, nhp), in_specs=[blk, blk],
        out_specs=[pl.BlockSpec((tr, 2 * LANE), lambda i, h: (i, h)),
                   pl.BlockSpec((2 * LANE, tr), lambda i, h: (h, i))],
        out_shape=[jax.ShapeDtypeStruct((T, 2 * D), BF16), jax.ShapeDtypeStruct((2 * D, T), BF16)],
        compiler_params=_cparams("parallel", "parallel"),
    )(do, o)


def _fox_prep_bwd(proj, dq, dkt, dvt, qg, kg, *, d_model, name):
    T = proj.shape[0]
    nhp = d_model // LANE
    tr = _row_tile(T, 2 * WIDE_ROWS)

    def body(q_ref, k_ref, dq_ref, dkt_ref, dvt_ref, qg_ref, kg_ref, dqo_ref, dko_ref, dvo_ref, sums_ref):
        first = (pl.program_id(0) == 0) & (pl.program_id(1) == 0)
        lo = lax.broadcasted_iota(jnp.int32, (tr, LANE), 1) < HEAD_DIM

        def pair(x2):
            return jnp.where(lo, x2[:, :LANE], pltpu.roll(x2[:, LANE:], HEAD_DIM, 1))

        def bwd(xv, dxhat, g):
            ms = _half_sums(xv * xv, lo) * (1.0 / HEAD_DIM)
            r = lax.rsqrt(ms + EPS)
            n = xv * r
            dn = dxhat * g
            dx = r * (dn - n * (_half_sums(dn * n, lo) * (1.0 / HEAD_DIM)))
            dg = jnp.sum(dxhat * n, axis=0, keepdims=True)
            return dx, dg + pltpu.roll(dg, HEAD_DIM, 1)

        dxq, dgq = bwd(q_ref[...], pair(dq_ref[...]) * (HEAD_DIM ** -0.5), qg_ref[...])
        dxk, dgk = bwd(k_ref[...], pair(dkt_ref[...].T), kg_ref[...])
        dqo_ref[...] = dxq.astype(BF16)
        dko_ref[...] = dxk.astype(BF16)
        dvo_ref[...] = pair(dvt_ref[...].T.astype(F32)).astype(BF16)
        part = jnp.concatenate([dgq, dgk, jnp.zeros((6, LANE), F32)], axis=0)

        @pl.when(first)
        def _():
            sums_ref[...] = part

        @pl.when(jnp.logical_not(first))
        def _():
            sums_ref[...] += part

    gain = pl.BlockSpec((1, LANE), lambda i, h: (0, 0))
    blk = pl.BlockSpec((tr, LANE), lambda i, h: (i, h))
    tall = pl.BlockSpec((2 * LANE, tr), lambda i, h: (h, i))
    return pl.pallas_call(
        body, name=name, grid=(T // tr, nhp),
        in_specs=[blk, pl.BlockSpec((tr, LANE), lambda i, h: (i, nhp + h)),
                  pl.BlockSpec((tr, 2 * LANE), lambda i, h: (i, h)), tall, tall, gain, gain],
        out_specs=[blk, blk, blk, pl.BlockSpec((8, LANE), lambda i, h: (0, 0))],
        out_shape=[jax.ShapeDtypeStruct((T, d_model), BF16)] * 3 + [jax.ShapeDtypeStruct((8, LANE), F32)],
        compiler_params=_cparams("arbitrary", "arbitrary"),
    )(proj, proj, dq, dkt, dvt, qg, kg)


def _scan_lanes(x, reverse):
    n = x.shape[-1]
    lane = lax.broadcasted_iota(jnp.int32, x.shape, 1)
    sh = 1
    while sh < n:
        if reverse:
            x = x + jnp.where(lane < n - sh, pltpu.roll(x, n - sh, 1), 0.0)
        else:
            x = x + jnp.where(lane >= sh, pltpu.roll(x, sh, 1), 0.0)
        sh *= 2
    return x


def _fox_gate_fwd(fpre_t, bf, *, name):
    def body(f_ref, b_ref, o_ref):
        xv = f_ref[...] + b_ref[...]
        logf = jnp.minimum(xv, 0.0) - jnp.log1p(jnp.exp(-jnp.abs(xv)))
        o_ref[...] = _scan_lanes(logf, reverse=False)

    return pl.pallas_call(body, name=name, out_shape=jax.ShapeDtypeStruct(fpre_t.shape, F32))(fpre_t, bf)


def _fox_gate_bwd(dcol, drow, fpre_t, bf, *, name):
    H = fpre_t.shape[0]

    def body(dc_ref, dr_ref, f_ref, b_ref, o_ref, db_ref):
        xv = f_ref[...] + b_ref[...]
        e = dc_ref[...] - dr_ref[...]
        dlogf = _scan_lanes(e, reverse=False) - e
        dpre = dlogf * (1.0 - jax.nn.sigmoid(xv))
        o_ref[...] = dpre
        db_ref[...] = jnp.broadcast_to(jnp.sum(dpre, axis=-1, keepdims=True), (H, LANE))

    return pl.pallas_call(
        body, name=name,
        out_shape=[jax.ShapeDtypeStruct(fpre_t.shape, F32), jax.ShapeDtypeStruct((H, LANE), F32)],
    )(dcol, drow, fpre_t, bf)


_NT = (((1,), (1,)), ((), ()))
_TN = (((0,), (0,)), ((), ()))
_NN = (((1,), (0,)), ((), ()))


def _attn_tile(T):
    return min(T, 1024)


def _causal(tq, tk):
    return lax.broadcasted_iota(jnp.int32, (tq, tk), 1) <= lax.broadcasted_iota(jnp.int32, (tq, tk), 0)


def _fox_attn_fwd(qa, kta, va, *, name, comm=None):
    T = qa.shape[0]
    nhp = qa.shape[1] // (2 * LANE)
    tq = tk = _attn_tile(T)
    nq = T // tq

    def body(*refs):
        (qa_ref, kta_ref, va_ref), src_refs, (o_ref, qb_ref), dst_refs, (m_sc, l_sc, acc_sc), sems = (
            _split_comm_refs(refs, 3, 2, 3, comm))
        hp, i, j = pl.program_id(0), pl.program_id(1), pl.program_id(2)
        if comm:
            @pl.when((hp == 0) & (i == 0) & (j == 0))
            def _():
                _comm_start(_comm_copies(comm.plan, src_refs, dst_refs, *sems))

        @pl.when(j == 0)
        def _():
            m_sc[...] = jnp.full(m_sc.shape, NEG, F32)
            l_sc[...] = jnp.zeros(l_sc.shape, F32)
            acc_sc[...] = jnp.zeros(acc_sc.shape, F32)

        def block(diagonal):
            heads = [slice(h * LANE, (h + 1) * LANE) for h in range(2)]
            scores = [lax.dot_general(qa_ref[:, hs], kta_ref[hs, :], _NN, preferred_element_type=F32)
                      for hs in heads]
            state = [(m_sc[h], l_sc[h], acc_sc[h]) for h in range(2)]
            probs, updates = [], []
            for s, (m_prev, l_prev, _) in zip(scores, state):
                if diagonal:
                    s = jnp.where(_causal(tq, tk), s, NEG)
                m_next = jnp.maximum(m_prev, jnp.max(s, axis=1, keepdims=True))
                p = jnp.exp(s - jnp.tile(m_next, (1, tk // LANE)))
                alpha = jnp.exp(m_prev - m_next)
                probs.append(p.astype(BF16))
                updates.append((m_next, alpha, alpha * l_prev + jnp.sum(p, axis=1, keepdims=True)))
            pvs = [lax.dot_general(p, va_ref[:, hs], _NN, preferred_element_type=F32)
                   for p, hs in zip(probs, heads)]
            for h in range(2):
                m_next, alpha, l_next = updates[h]
                m_sc[h] = m_next
                l_sc[h] = l_next
                acc_sc[h] = alpha * state[h][2] + pvs[h]

        @pl.when(j < i)
        def _():
            block(False)

        @pl.when(j == i)
        def _():
            block(True)
            o_ref[...] = acc_sc[0] / l_sc[0] + acc_sc[1] / l_sc[1]
            lane = lax.broadcasted_iota(jnp.int32, (tq, LANE), 1)
            for h in range(2):
                hs = slice(h * LANE, (h + 1) * LANE)
                pieces = _split3(-(m_sc[h] + jnp.log(l_sc[h])))
                qb = qa_ref[:, hs].astype(F32)
                for n, piece in enumerate(pieces):
                    qb = jnp.where(lane == AUG_LSE + n, piece, qb)
                qb_ref[:, hs] = qb.astype(BF16)

        if comm:
            @pl.when((hp == nhp - 1) & (i == nq - 1) & (j == nq - 1))
            def _():
                _comm_wait(_comm_copies(comm.plan, src_refs, dst_refs, *sems))

    outs = pl.pallas_call(
        body, name=name, grid=(nhp, nq, nq),
        in_specs=[pl.BlockSpec((tq, 2 * LANE), lambda h, i, j: (i, h)),
                  pl.BlockSpec((2 * LANE, tk), lambda h, i, j: (h, jnp.minimum(j, i))),
                  pl.BlockSpec((tk, 2 * LANE), lambda h, i, j: (jnp.minimum(j, i), h))]
        + ([ANY] * len(comm.srcs) if comm else []),
        out_specs=[pl.BlockSpec((tq, LANE), lambda h, i, j: (i, h)),
                   pl.BlockSpec((tq, 2 * LANE), lambda h, i, j: (i, h))]
        + ([ANY] * len(comm.out_shapes) if comm else []),
        out_shape=[jax.ShapeDtypeStruct((T, nhp * LANE), F32), jax.ShapeDtypeStruct(qa.shape, BF16)]
        + (list(comm.out_shapes) if comm else []),
        scratch_shapes=[pltpu.VMEM((2, tq, LANE), F32), pltpu.VMEM((2, tq, LANE), F32),
                        pltpu.VMEM((2, tq, LANE), F32)] + (comm.scratch() if comm else []),
        compiler_params=(_cparams("arbitrary", "arbitrary", "arbitrary") if comm
                         else _cparams("parallel", "parallel", "arbitrary")),
    )(qa, kta, va, *(comm.srcs if comm else []))
    return outs[0], outs[1], outs[2:]


def _fox_attn_bwd(qb, qta, ka, kta, vta, doa, dota, *, name, comm=None):
    T = qb.shape[0]
    nhp = qb.shape[1] // (2 * LANE)
    tq = tk = _attn_tile(T)
    nq = T // tq

    def body(*refs):
        ((qb_ref, qta_ref, ka_ref, kta_ref, vta_ref, doa_ref, dota_ref), src_refs,
         (dq_ref, dkt_ref, dvt_ref, dcol_ref, drow_ref), dst_refs, (dkt_sc, dvt_sc, dcol_sc), sems) = (
            _split_comm_refs(refs, 7, 5, 3, comm))
        hp, j, i = pl.program_id(0), pl.program_id(1), pl.program_id(2)
        if comm:
            @pl.when((hp == 0) & (j == 0) & (i == 0))
            def _():
                _comm_start(_comm_copies(comm.plan, src_refs, dst_refs, *sems))

        @pl.when((j == 0) & (i == 0))
        def _():
            dq_ref[...] = jnp.zeros(dq_ref.shape, F32)
            drow_ref[...] = jnp.zeros(drow_ref.shape, F32)

        @pl.when(i == 0)
        def _():
            dkt_sc[...] = jnp.zeros(dkt_sc.shape, F32)
            dvt_sc[...] = jnp.zeros(dvt_sc.shape, F32)
            dcol_sc[...] = jnp.zeros(dcol_sc.shape, F32)

        def block(diagonal):
            rows = pl.ds(pl.multiple_of(i * tq, tq), tq)
            heads = [slice(h * LANE, (h + 1) * LANE) for h in range(2)]
            logits = [lax.dot_general(qb_ref[:, hs], kta_ref[hs, :], _NN, preferred_element_type=F32)
                      for hs in heads]
            dpds = [lax.dot_general(doa_ref[:, hs], vta_ref[hs, :], _NN, preferred_element_type=F32)
                    for hs in heads]
            pbs, dlbs = [], []
            for h in range(2):
                p = jnp.exp(logits[h])
                if diagonal:
                    p = jnp.where(_causal(tq, tk), p, 0.0)
                dl = p * dpds[h]
                pbs.append(p.astype(BF16))
                dlbs.append(dl.astype(BF16))
                dcol_sc[h] += jnp.sum(dl, axis=0, keepdims=True)
                drow_ref[h, rows, :] += jnp.sum(dl, axis=1, keepdims=True)
            for h, hs in enumerate(heads):
                dvt_sc[h] += lax.dot_general(dota_ref[hs, :], pbs[h], _NN, preferred_element_type=F32)
                dkt_sc[h] += lax.dot_general(qta_ref[hs, :], dlbs[h], _NN, preferred_element_type=F32)
                dq_ref[rows, hs] += lax.dot_general(dlbs[h], ka_ref[:, hs], _NN, preferred_element_type=F32)

        @pl.when(i > j)
        def _():
            block(False)

        @pl.when(i == j)
        def _():
            block(True)

        @pl.when(i == nq - 1)
        def _():
            dkt_ref[...] = jnp.concatenate([dkt_sc[0], dkt_sc[1]], axis=0)
            dvt_ref[...] = jnp.concatenate([dvt_sc[0], dvt_sc[1]], axis=0).astype(BF16)
            dcol_ref[...] = dcol_sc[...]

        if comm:
            @pl.when((hp == nhp - 1) & (j == nq - 1) & (i == nq - 1))
            def _():
                _comm_wait(_comm_copies(comm.plan, src_refs, dst_refs, *sems))

    qrow = pl.BlockSpec((tq, 2 * LANE), lambda h, j, i: (jnp.maximum(i, j), h))
    qcol = pl.BlockSpec((2 * LANE, tq), lambda h, j, i: (h, jnp.maximum(i, j)))
    krow = pl.BlockSpec((tk, 2 * LANE), lambda h, j, i: (j, h))
    kcol = pl.BlockSpec((2 * LANE, tk), lambda h, j, i: (h, j))
    tall = jax.ShapeDtypeStruct((qb.shape[1], T), F32)
    outs = pl.pallas_call(
        body, name=name, grid=(nhp, nq, nq),
        in_specs=[qrow, qcol, krow, kcol, kcol, qrow, qcol] + ([ANY] * len(comm.srcs) if comm else []),
        out_specs=[pl.BlockSpec((T, 2 * LANE), lambda h, j, i: (0, h)), kcol, kcol,
                   pl.BlockSpec((2, 1, tk), lambda h, j, i: (h, 0, j)),
                   pl.BlockSpec((2, T, 1), lambda h, j, i: (h, 0, 0))]
        + ([ANY] * len(comm.out_shapes) if comm else []),
        out_shape=[jax.ShapeDtypeStruct(qb.shape, F32), tall, jax.ShapeDtypeStruct(tall.shape, BF16),
                   jax.ShapeDtypeStruct((2 * nhp, 1, T), F32), jax.ShapeDtypeStruct((2 * nhp, T, 1), F32)]
        + (list(comm.out_shapes) if comm else []),
        scratch_shapes=[pltpu.VMEM((2, LANE, tk), F32), pltpu.VMEM((2, LANE, tk), F32),
                        pltpu.VMEM((2, 1, tk), F32)] + (comm.scratch() if comm else []),
        compiler_params=_cparams("arbitrary" if comm else "parallel", "arbitrary", "arbitrary"),
    )(qb, qta, ka, kta, vta, doa, dota, *(comm.srcs if comm else []))
    return (*outs[:5], outs[5:])


_GELU_C = math.sqrt(2.0 / math.pi)
_GELU_A = 0.044715


def _gelu(x):
    t = jnp.tanh(_GELU_C * (x + _GELU_A * (x * x * x)))
    return x * (0.5 * (1.0 + t)), t


def _gelu_grad(x, t):
    return 0.5 * (1.0 + t) + 0.5 * x * (1.0 - t * t) * (_GELU_C * (1.0 + 3.0 * _GELU_A * x * x))


def _layer_norm_stats(v):
    mu = jnp.mean(v, axis=-1, keepdims=True)
    vc = v - mu
    rstd = lax.rsqrt(jnp.mean(vc * vc, axis=-1, keepdims=True) + EPS)
    return vc * rstd, rstd


def _layer_norm_bwd(dyhat, yhat, rstd):
    return rstd * (dyhat - jnp.mean(dyhat, axis=-1, keepdims=True)
                   - yhat * jnp.mean(dyhat * yhat, axis=-1, keepdims=True))


def _sg_mask():
    t = lax.broadcasted_iota(jnp.int32, (SG_CHUNK, SG_CHUNK), 0) // SG_CAUSAL
    s = lax.broadcasted_iota(jnp.int32, (SG_CHUNK, SG_CHUNK), 1) // SG_CAUSAL
    return s <= t


def _sg_mix(ws_ref, bc_ref, vln_sc, vo_sc, tr, gd):
    mask = _sg_mask()
    for g in range(SG_GROUPS):
        wg = jnp.where(mask, ws_ref[g], 0.0).astype(BF16)
        cols = slice(g * gd, (g + 1) * gd)
        for n in range(tr // SG_CHUNK):
            rows = slice(n * SG_CHUNK, (n + 1) * SG_CHUNK)
            vo_sc[rows, cols] = lax.dot_general(wg, vln_sc[rows, cols], _NN,
                                                preferred_element_type=F32) + bc_ref[g]


def _sg_fwd(a_uv, ln_g, ln_b, ws, bcol, *, name):
    T, W = a_uv.shape[0], a_uv.shape[1] // 2
    gd = W // SG_GROUPS
    tr = _row_tile(T)

    def body(u_ref, v_ref, g_ref, b_ref, ws_ref, bc_ref, o_ref, vln_sc, vo_sc):
        u, _ = _gelu(u_ref[...])
        v, _ = _gelu(v_ref[...])
        vhat, _ = _layer_norm_stats(v)
        vln_sc[...] = (vhat * g_ref[...] + b_ref[...]).astype(BF16)
        _sg_mix(ws_ref, bc_ref, vln_sc, vo_sc, tr, gd)
        o_ref[...] = (u * vo_sc[...]).astype(BF16)

    row = pl.BlockSpec((1, W), lambda i: (0, 0))
    return pl.pallas_call(
        body, name=name, grid=(T // tr,),
        in_specs=[pl.BlockSpec((tr, W), lambda i: (i, 0)), pl.BlockSpec((tr, W), lambda i: (i, 1)), row, row,
                  pl.BlockSpec((SG_GROUPS, SG_CHUNK, SG_CHUNK), lambda i: (0, 0, 0)),
                  pl.BlockSpec((SG_GROUPS, SG_CHUNK, 1), lambda i: (0, 0, 0))],
        out_specs=pl.BlockSpec((tr, W), lambda i: (i, 0)),
        out_shape=jax.ShapeDtypeStruct((T, W), BF16),
        scratch_shapes=[pltpu.VMEM((tr, W), BF16), pltpu.VMEM((tr, W), F32)],
        compiler_params=_cparams("parallel"),
    )(a_uv, a_uv, ln_g, ln_b, ws, bcol)


def _sg_bwd(a_uv, dgate, ln_g, ln_b, ws, bcol, *, name):
    T, W = a_uv.shape[0], a_uv.shape[1] // 2
    gd = W // SG_GROUPS
    tr = _row_tile(T)

    def body(u_ref, v_ref, dg_ref, g_ref, b_ref, ws_ref, bc_ref,
             da_ref, dws_ref, dbs_ref, sums_ref, vln_sc, vo_sc, dvo_sc, dvln_sc):
        i = pl.program_id(0)

        @pl.when(i == 0)
        def _():
            dws_ref[...] = jnp.zeros(dws_ref.shape, F32)
            dbs_ref[...] = jnp.zeros(dbs_ref.shape, F32)
            sums_ref[...] = jnp.zeros(sums_ref.shape, F32)

        ua, va = u_ref[...], v_ref[...]
        u, tu = _gelu(ua)
        v, tv = _gelu(va)
        vhat, rstd = _layer_norm_stats(v)
        vln_sc[...] = (vhat * g_ref[...] + b_ref[...]).astype(BF16)
        _sg_mix(ws_ref, bc_ref, vln_sc, vo_sc, tr, gd)
        dgt = dg_ref[...]
        du = dgt * vo_sc[...]
        dvo_sc[...] = dgt * u
        mask = _sg_mask()
        for g in range(SG_GROUPS):
            wg = jnp.where(mask, ws_ref[g], 0.0).astype(BF16)
            cols = slice(g * gd, (g + 1) * gd)
            acc_w = jnp.zeros((SG_CHUNK, SG_CHUNK), F32)
            acc_b = jnp.zeros((SG_CHUNK, 1), F32)
            for n in range(tr // SG_CHUNK):
                rows = slice(n * SG_CHUNK, (n + 1) * SG_CHUNK)
                dvo = dvo_sc[rows, cols]
                dvob = dvo.astype(BF16)
                dvln_sc[rows, cols] = lax.dot_general(wg, dvob, _TN, preferred_element_type=F32)
                acc_w += lax.dot_general(dvob, vln_sc[rows, cols], _NT, preferred_element_type=F32)
                acc_b += jnp.sum(dvo, axis=1, keepdims=True)
            dws_ref[g] += jnp.where(mask, acc_w, 0.0)
            dbs_ref[g] += acc_b
        dvln = dvln_sc[...]
        sums_ref[...] += jnp.concatenate([jnp.sum(dvln * vhat, axis=0, keepdims=True),
                                          jnp.sum(dvln, axis=0, keepdims=True),
                                          jnp.zeros((6, W), F32)], axis=0)
        dv = _layer_norm_bwd(dvln * g_ref[...], vhat, rstd)
        da_ref[:, :W] = (du * _gelu_grad(ua, tu)).astype(BF16)
        da_ref[:, W:] = (dv * _gelu_grad(va, tv)).astype(BF16)

    row = pl.BlockSpec((1, W), lambda i: (0, 0))
    wspec = pl.BlockSpec((SG_GROUPS, SG_CHUNK, SG_CHUNK), lambda i: (0, 0, 0))
    bspec = pl.BlockSpec((SG_GROUPS, SG_CHUNK, 1), lambda i: (0, 0, 0))
    return pl.pallas_call(
        body, name=name, grid=(T // tr,),
        in_specs=[pl.BlockSpec((tr, W), lambda i: (i, 0)), pl.BlockSpec((tr, W), lambda i: (i, 1)),
                  pl.BlockSpec((tr, W), lambda i: (i, 0)), row, row, wspec, bspec],
        out_specs=[pl.BlockSpec((tr, 2 * W), lambda i: (i, 0)), wspec, bspec,
                   pl.BlockSpec((8, W), lambda i: (0, 0))],
        out_shape=[jax.ShapeDtypeStruct((T, 2 * W), BF16),
                   jax.ShapeDtypeStruct((SG_GROUPS, SG_CHUNK, SG_CHUNK), F32),
                   jax.ShapeDtypeStruct((SG_GROUPS, SG_CHUNK, 1), F32),
                   jax.ShapeDtypeStruct((8, W), F32)],
        scratch_shapes=[pltpu.VMEM((tr, W), BF16), pltpu.VMEM((tr, W), F32),
                        pltpu.VMEM((tr, W), F32), pltpu.VMEM((tr, W), F32)],
        compiler_params=_cparams("arbitrary"),
    )(a_uv, a_uv, dgate, ln_g, ln_b, ws, bcol)


SUBLANES = 8


def _shift_rows(xc_sc, xs_sc):
    rows = xs_sc.shape[1]
    for p in range(1, SUBLANES):
        xs_sc[p - 1] = xc_sc[pl.ds(p, rows), :]


def _rows_at(xc_sc, xs_sc, offset, tr):
    p = offset % SUBLANES
    base = offset - p
    return xc_sc[pl.ds(base, tr), :] if p == 0 else xs_sc[p - 1, pl.ds(base, tr), :]


def _shift_scratch(tr, C):
    return pltpu.VMEM((SUBLANES - 1, tr + CONV_HALO - SUBLANES, C), F32)


def _cv_glu_conv(a_ref, b_ref, ap_ref, bp_ref, w_ref, bd_ref, xc_sc, xs_sc, tr):
    i = pl.program_id(0)
    prev = ap_ref[...] * jax.nn.sigmoid(bp_ref[...])
    xc_sc[0:CONV_HALO, :] = jnp.where(i > 0, prev, 0.0)
    xc_sc[CONV_HALO:, :] = a_ref[...] * jax.nn.sigmoid(b_ref[...])
    _shift_rows(xc_sc, xs_sc)
    acc = jnp.broadcast_to(bd_ref[...], (tr, bd_ref.shape[1]))
    for k in range(CONV_WIDTH):
        acc = acc + w_ref[k:k + 1, :] * _rows_at(xc_sc, xs_sc, CONV_HALO - (CONV_WIDTH - 1) + k, tr)
    return acc


def _cv_specs(T, C, tr):
    hb = tr // CONV_HALO
    cur = lambda col: pl.BlockSpec((tr, C), lambda i: (i, col))
    prev = lambda col: pl.BlockSpec((CONV_HALO, C), lambda i: (jnp.maximum(i * hb - 1, 0), col))
    row = pl.BlockSpec((1, C), lambda i: (0, 0))
    wspec = pl.BlockSpec((CONV_HALO, C), lambda i: (0, 0))
    return cur, prev, row, wspec


def _cv_fwd(p, w_dw, b_dw, ln_g, ln_b, *, name):
    T, C = p.shape[0], p.shape[1] // 2
    tr = _row_tile(T)
    cur, prev, row, wspec = _cv_specs(T, C, tr)

    def body(a_ref, b_ref, ap_ref, bp_ref, w_ref, bd_ref, g_ref, be_ref, o_ref, xc_sc, xs_sc):
        y2 = _cv_glu_conv(a_ref, b_ref, ap_ref, bp_ref, w_ref, bd_ref, xc_sc, xs_sc, tr)
        yhat, _ = _layer_norm_stats(y2)
        yln = yhat * g_ref[...] + be_ref[...]
        o_ref[...] = (yln * jax.nn.sigmoid(yln)).astype(BF16)

    return pl.pallas_call(
        body, name=name, grid=(T // tr,),
        in_specs=[cur(0), cur(1), prev(0), prev(1), wspec, row, row, row],
        out_specs=pl.BlockSpec((tr, C), lambda i: (i, 0)),
        out_shape=jax.ShapeDtypeStruct((T, C), BF16),
        scratch_shapes=[pltpu.VMEM((tr + CONV_HALO, C), F32), _shift_scratch(tr, C)],
        compiler_params=_cparams("parallel"),
    )(p, p, p, p, w_dw, b_dw, ln_g, ln_b)


def _cv_bwd_ln(p, dy3, w_dw, b_dw, ln_g, ln_b, *, name):
    T, C = p.shape[0], p.shape[1] // 2
    tr = _row_tile(T)
    cur, prev, row, wspec = _cv_specs(T, C, tr)

    def body(a_ref, b_ref, ap_ref, bp_ref, dy_ref, w_ref, bd_ref, g_ref, be_ref,
             dy2_ref, dw_ref, sums_ref, xc_sc, xs_sc):
        i = pl.program_id(0)
        y2 = _cv_glu_conv(a_ref, b_ref, ap_ref, bp_ref, w_ref, bd_ref, xc_sc, xs_sc, tr)
        yhat, rstd = _layer_norm_stats(y2)
        yln = yhat * g_ref[...] + be_ref[...]
        s = jax.nn.sigmoid(yln)
        dyln = dy_ref[...] * (s + yln * s * (1.0 - s))
        dy2 = _layer_norm_bwd(dyln * g_ref[...], yhat, rstd)
        dy2_ref[...] = dy2
        sums = jnp.concatenate([jnp.sum(dy2, axis=0, keepdims=True),
                                jnp.sum(dyln * yhat, axis=0, keepdims=True),
                                jnp.sum(dyln, axis=0, keepdims=True),
                                jnp.zeros((5, C), F32)], axis=0)
        taps = [jnp.sum(dy2 * _rows_at(xc_sc, xs_sc, CONV_HALO - (CONV_WIDTH - 1) + k, tr), axis=0, keepdims=True)
                for k in range(CONV_WIDTH)]
        dw = jnp.concatenate(taps + [jnp.zeros((CONV_HALO - CONV_WIDTH, C), F32)], axis=0)

        @pl.when(i == 0)
        def _():
            sums_ref[...] = sums
            dw_ref[...] = dw

        @pl.when(i > 0)
        def _():
            sums_ref[...] += sums
            dw_ref[...] += dw

    blk = pl.BlockSpec((tr, C), lambda i: (i, 0))
    return pl.pallas_call(
        body, name=name, grid=(T // tr,),
        in_specs=[cur(0), cur(1), prev(0), prev(1), blk, wspec, row, row, row],
        out_specs=[blk, wspec, pl.BlockSpec((8, C), lambda i: (0, 0))],
        out_shape=[jax.ShapeDtypeStruct((T, C), F32), jax.ShapeDtypeStruct((CONV_HALO, C), F32),
                   jax.ShapeDtypeStruct((8, C), F32)],
        scratch_shapes=[pltpu.VMEM((tr + CONV_HALO, C), F32), _shift_scratch(tr, C)],
        compiler_params=_cparams("arbitrary"),
    )(p, p, p, p, dy3, w_dw, b_dw, ln_g, ln_b)


def _cv_bwd_in(p, dy2, w_dw, *, name):
    T, C = p.shape[0], p.shape[1] // 2
    tr = _row_tile(T)
    hb = tr // CONV_HALO
    nblk = T // tr
    last_halo = T // CONV_HALO - 1

    def body(a_ref, b_ref, dy_ref, dyn_ref, w_ref, dp_ref, sums_ref, xc_sc, xs_sc):
        i = pl.program_id(0)
        xc_sc[0:tr, :] = dy_ref[...]
        xc_sc[tr:, :] = jnp.where(i < nblk - 1, dyn_ref[...], 0.0)
        _shift_rows(xc_sc, xs_sc)
        dy1 = jnp.zeros((tr, C), F32)
        for k in range(CONV_WIDTH):
            dy1 = dy1 + w_ref[k:k + 1, :] * _rows_at(xc_sc, xs_sc, CONV_WIDTH - 1 - k, tr)
        a = a_ref[...]
        sb = jax.nn.sigmoid(b_ref[...])
        da = dy1 * sb
        db = dy1 * a * sb * (1.0 - sb)
        dp_ref[:, :C] = da.astype(BF16)
        dp_ref[:, C:] = db.astype(BF16)
        sums = jnp.concatenate([
            jnp.concatenate([jnp.sum(da, axis=0, keepdims=True), jnp.sum(db, axis=0, keepdims=True)], axis=1),
            jnp.zeros((7, 2 * C), F32)], axis=0)

        @pl.when(i == 0)
        def _():
            sums_ref[...] = sums

        @pl.when(i > 0)
        def _():
            sums_ref[...] += sums

    blk = lambda col: pl.BlockSpec((tr, C), lambda i: (i, col))
    return pl.pallas_call(
        body, name=name, grid=(nblk,),
        in_specs=[blk(0), blk(1), blk(0),
                  pl.BlockSpec((CONV_HALO, C), lambda i: (jnp.minimum((i + 1) * hb, last_halo), 0)),
                  pl.BlockSpec((CONV_HALO, C), lambda i: (0, 0))],
        out_specs=[pl.BlockSpec((tr, 2 * C), lambda i: (i, 0)), pl.BlockSpec((8, 2 * C), lambda i: (0, 0))],
        out_shape=[jax.ShapeDtypeStruct((T, 2 * C), BF16), jax.ShapeDtypeStruct((8, 2 * C), F32)],
        scratch_shapes=[pltpu.VMEM((tr + CONV_HALO, C), F32), _shift_scratch(tr, C)],
        compiler_params=_cparams("arbitrary"),
    )(p, p, dy2, dy2, w_dw)


def _col_tile(n, want=1024):
    best = LANE
    for t in range(LANE, min(n, want) + 1, LANE):
        if n % t == 0:
            best = t
    return best if n % LANE == 0 else n


def _mm(a, b, *, name, ta=False, tb=False, **kw):
    M = a.shape[1] if ta else a.shape[0]
    N = b.shape[0] if tb else b.shape[1]
    K = a.shape[0] if ta else a.shape[1]
    kw.setdefault('tm', _col_tile(M, 1024 if ta else 512))
    kw.setdefault('tn', _col_tile(N, 1024))
    kw.setdefault('tk', K if tb else _col_tile(K, 2048 if ta else 1024))
    return _matmul(a, b, name=name, ta=ta, tb=tb, **kw)


def _relu2_epilogue(acc):
    r = jnp.maximum(acc, 0.0)
    return (r * r,)


def _residual_epilogue(acc, x, g):
    return acc, x + g * acc


def _residual_bias_epilogue(acc, x, g, b):
    y = acc + b
    return y, x + g * y


def _dh_norm_bwd(d_act, w, x, dres, nw, sc, *, name, gate=None, below=None):
    D = x.shape[1]

    def epilogue(dh, xv, dresv, wv, scv, *more):
        gated, under = (more[:2], more[2:]) if gate else ((), more)
        r = lax.rsqrt(jnp.mean(xv * xv, axis=-1, keepdims=True) + EPS)
        n = xv * r
        scale = 1.0 + scv
        dn = dh * (wv * scale)
        dx = dresv + r * (dn - n * jnp.mean(dn * n, axis=-1, keepdims=True))
        rows = [jnp.sum(dh, axis=0, keepdims=True),
                jnp.sum(dh * (n * wv), axis=0, keepdims=True),
                jnp.sum(dh * n * scale, axis=0, keepdims=True)]
        outs = [dx]
        if gated:
            yv, gv = gated
            outs.append(dx * gv)
            rows += [jnp.sum(dx * yv, axis=0, keepdims=True), jnp.sum(dx * gv, axis=0, keepdims=True)]
        rows += [jnp.zeros((7 - len(rows), D), F32)]
        if under:
            zv, gzv = under
            outs.append(dx * gzv)
            rows.append(jnp.sum(dx * zv, axis=0, keepdims=True))
        else:
            rows.append(jnp.zeros((1, D), F32))
        return (*outs, jnp.concatenate(rows, axis=0))

    extras = [(x, 'tile'), (dres, 'tile'), (nw, 'row'), (sc, 'row')]
    for branch in (gate, below):
        if branch:
            extras += [(branch[0], 'tile'), (branch[1], 'row')]
    return _mm(d_act, w, tb=True, name=name, tn=D, extras=extras, epilogue=epilogue,
               out_dtypes=(F32,) + (BF16,) * (bool(gate) + bool(below)), row_sums=True)


def _relu2_bwd_epilogue(acc, r):
    return (acc * (2.0 * jnp.sqrt(r.astype(F32))),)


def _bias_epilogue(acc, b):
    return (acc + b,)


def _fox_forward(h1, P, j, D, carried=None):
    carried = carried or {}

    def ride(kernel):
        return carried[kernel][0] if kernel in carried else None

    def landed(kernel, outs):
        if kernel in carried:
            carried[kernel][1](outs)

    H = D // HEAD_DIM
    proj = _mm(h1, P['fox_w_in'][j], name='fox_proj', b_outer=True, tm=WIDE_ROWS)
    qg = jnp.tile(P['fox_q_norm'][j][None, :], (1, 2))
    kg = jnp.tile(P['fox_k_norm'][j][None, :], (1, 2))
    fpre_t = proj[:, 3 * D:3 * D + H].T
    bf = P['fox_b_f'][j][:, None]
    fcum = _fox_gate_fwd(fpre_t, bf, name='fox_gate_fwd')
    (qa, qta, ka, kta, va, vta), outs = _fox_prep_fwd(proj, qg, kg, fcum[:, :, None], d_model=D, name='fox_prep_fwd',
                                                     comm=ride('prep'))
    landed('prep', outs)
    o, qb, outs = _fox_attn_fwd(qa, kta, va, name='fox_attn_fwd', comm=ride('attn'))
    landed('attn', outs)
    saved = dict(proj=proj, qg=qg, kg=kg, fpre_t=fpre_t, bf=bf, o=o, qb=qb, qta=qta, ka=ka, kta=kta, vta=vta)
    return o, saved


def _fox_backward(dy, h1, S, P, j, D, comm=None):
    H = D // HEAD_DIM
    w_out, w_in = P['fox_w_out'][j], P['fox_w_in'][j]
    g = {}
    g['fox_w_out'] = _mm(S['o'], dy, ta=True, name='fox_dw_out')
    do = _mm(dy, w_out, tb=True, name='fox_do')
    doa, dota = _fox_do_prep(do, S['o'], name='fox_do_prep')
    dq, dkt, dvt, dcol, drow, comm_outs = _fox_attn_bwd(S['qb'], S['qta'], S['ka'], S['kta'], S['vta'], doa, dota,
                                                        name='fox_attn_bwd', comm=comm)
    dqp, dkp, dvp, gsum = _fox_prep_bwd(S['proj'], dq, dkt, dvt, S['qg'], S['kg'], d_model=D, name='fox_prep_bwd')
    dfpre_t, dbf = _fox_gate_bwd(dcol[:, 0, :], drow[:, :, 0], S['fpre_t'], S['bf'], name='fox_gate_bwd')
    dfpre = jnp.pad(dfpre_t.T.astype(BF16), ((0, 0), (0, LANE - H)))
    dproj = jnp.concatenate([dqp, dkp, dvp, dfpre], axis=1)
    g['fox_w_in'] = _mm(h1, dproj, ta=True, name='fox_dw_in')[:, :3 * D + H]
    g['fox_b_f'] = dbf[:, 0]
    g['fox_q_norm'] = gsum[0, :HEAD_DIM]
    g['fox_k_norm'] = gsum[1, :HEAD_DIM]
    return (dproj, w_in), g, comm_outs


def _sg_forward(h1, P, D):
    a_uv = _mm(h1, P['sg_w_in'], name='sg_in', b_outer=True, tm=WIDE_ROWS)
    bcol = P['sg_b_s'][:, :, None]
    gate = _sg_fwd(a_uv, P['sg_ln_g'], P['sg_ln_b'], P['sg_w_s'], bcol, name='sg_fwd')
    return gate, dict(a_uv=a_uv, bcol=bcol, gate=gate)


def _sg_backward(dy, h1, S, P, D):
    g = {}
    g['sg_w_out'] = _mm(S['gate'], dy, ta=True, name='sg_dw_out')
    dgate = _mm(dy, P['sg_w_out'], tb=True, name='sg_dgate')
    da, dws, dbs, sums = _sg_bwd(S['a_uv'], dgate, P['sg_ln_g'], P['sg_ln_b'], P['sg_w_s'], S['bcol'],
                                 name='sg_bwd')
    g['sg_w_s'], g['sg_b_s'] = dws, dbs[:, :, 0]
    g['sg_ln_g'], g['sg_ln_b'] = sums[0], sums[1]
    g['sg_w_in'] = _mm(h1, da, ta=True, name='sg_dw_in', out_chips=N_CHIPS)
    return (da, P['sg_w_in']), g


def _cv_forward(h1, P, D):
    p = _mm(h1, P['cv_w_pw1'], name='cv_pw1', extras=[(P['cv_b_pw1'], 'row')], epilogue=_bias_epilogue,
            b_outer=True, tm=WIDE_ROWS)
    w_dw = jnp.pad(P['cv_w_dw'], ((0, CONV_HALO - CONV_WIDTH), (0, 0)))
    y3 = _cv_fwd(p, w_dw, P['cv_b_dw'], P['cv_ln_g'], P['cv_ln_b'], name='cv_fwd')
    return y3, dict(p=p, w_dw=w_dw, y3=y3)


def _cv_backward(dy, h1, S, P, D):
    g = {}
    g['cv_w_pw2'] = _mm(S['y3'], dy, ta=True, name='cv_dw_pw2')
    dy3 = _mm(dy, P['cv_w_pw2'], tb=True, name='cv_dy3')
    dy2, dw, sums = _cv_bwd_ln(S['p'], dy3, S['w_dw'], P['cv_b_dw'], P['cv_ln_g'], P['cv_ln_b'], name='cv_bwd_ln')
    g['cv_w_dw'] = dw[:CONV_WIDTH]
    g['cv_b_dw'], g['cv_ln_g'], g['cv_ln_b'] = sums[0], sums[1], sums[2]
    dp, psum = _cv_bwd_in(S['p'], dy2, S['w_dw'], name='cv_bwd_in')
    g['cv_b_pw1'] = psum[0]
    g['cv_w_pw1'] = _mm(h1, dp, ta=True, name='cv_dw_pw1', out_chips=N_CHIPS)
    return (dp, P['cv_w_pw1']), g


class Hooks(NamedTuple):
    fwd: dict
    bwd_comm: Callable
    bwd_done: Callable


def _local_step(x, target, mod, P, hooks=None):
    T, D = x.shape
    L = mod.shape[0]
    saved = []
    for i in range(L):
        kind, j = i % N_MIXERS, i // N_MIXERS
        m = [mod[i:i + 1, k * D:(k + 1) * D] for k in range(6)]
        sh_m, sc_m, g_m, sh_f, sc_f, g_f = m
        w_mix, w_mlp = P['norm_mix'][i:i + 1], P['norm_mlp'][i:i + 1]
        h1 = _norm_mod_fwd(x, w_mix, sc_m, sh_m, name='norm_mix_fwd')
        if kind == 0:
            op, S = _fox_forward(h1, P, j, D, carried=hooks.fwd.get(i) if hooks else None)
            y, x1 = _mm(op, P['fox_w_out'][j], name='fox_out', extras=[(x, 'tile'), (g_m, 'row')],
                        epilogue=_residual_epilogue, out_dtypes=(F32, F32))
        elif kind == 1:
            op, S = _sg_forward(h1, P, D)
            y, x1 = _mm(op, P['sg_w_out'], name='sg_out', extras=[(x, 'tile'), (g_m, 'row')],
                        epilogue=_residual_epilogue, out_dtypes=(F32, F32))
        else:
            op, S = _cv_forward(h1, P, D)
            y, x1 = _mm(op, P['cv_w_pw2'], name='cv_out',
                        extras=[(x, 'tile'), (g_m, 'row'), (P['cv_b_pw2'], 'row')],
                        epilogue=_residual_bias_epilogue, out_dtypes=(F32, F32))
        h2 = _norm_mod_fwd(x1, w_mlp, sc_f, sh_f, name='norm_mlp_fwd')
        r = _mm(h2, P['w_mlp_in'][i], name='mlp_in', epilogue=_relu2_epilogue, out_dtypes=(BF16,),
                b_outer=True, tm=WIDE_ROWS)
        z, x2 = _mm(r, P['w_mlp_out'][i], name='mlp_out', extras=[(x1, 'tile'), (g_f, 'row')],
                    epilogue=_residual_epilogue, out_dtypes=(F32, F32), tk=P['w_mlp_out'][i].shape[0])
        saved.append(dict(x=x, h1=h1, S=S, y=y, x1=x1, h2=h2, r=r, z=z, m=m))
        x = x2

    loss_part, dx = _loss_head(x, target, name='loss_head')

    grads = {k: [None] * L for k in ('norm_mix', 'norm_mlp')}
    mix_grads, mat = {}, {}
    dmod = [None] * L
    for i in reversed(range(L)):
        kind, j = i % N_MIXERS, i // N_MIXERS
        sv = saved[i]
        sh_m, sc_m, g_m, sh_f, sc_f, g_f = sv['m']
        w_mix, w_mlp = P['norm_mix'][i:i + 1], P['norm_mlp'][i:i + 1]
        if i == L - 1:
            dz, dgf = _gate_bwd(dx, sv['z'], g_f, name='mlp_gate_bwd')
            dgf = dgf[0]
        mat['w_mlp_out', i] = _mm(sv['r'], dz, ta=True, name='mlp_dw_out')
        da = _mm(dz, P['w_mlp_out'][i], tb=True, name='mlp_da', extras=[(sv['r'], 'tile')],
                 epilogue=_relu2_bwd_epilogue, out_dtypes=(BF16,), b_outer=True, tm=WIDE_ROWS)
        mat['w_mlp_in', i] = _mm(sv['h2'], da, ta=True, name='mlp_dw_in', out_chips=N_CHIPS)
        dx1, dy, sums_f = _dh_norm_bwd(da, P['w_mlp_in'][i], sv['x1'], dx, w_mlp, sc_f, name='mlp_dh',
                                       gate=(sv['y'], g_m))
        if kind == 0:
            carried = hooks is not None and i == 0
            last, g, comm_outs = _fox_backward(dy, sv['h1'], sv['S'], P, j, D,
                                              comm=hooks.bwd_comm(mat) if carried else None)
            if carried:
                hooks.bwd_done(comm_outs)
        elif kind == 1:
            last, g = _sg_backward(dy, sv['h1'], sv['S'], P, D)
        else:
            last, g = _cv_backward(dy, sv['h1'], sv['S'], P, D)
            g['cv_b_pw2'] = sums_f[4]
        for k, val in g.items():
            if k in BIG:
                mat[k, j] = val
            else:
                mix_grads.setdefault(k, {})[j] = val
        if i > 0:
            dx, dz, sums_m = _dh_norm_bwd(*last, sv['x'], dx1, w_mix, sc_m, name='mix_dh',
                                          below=(saved[i - 1]['z'], saved[i - 1]['m'][5]))
        else:
            dx, sums_m = _dh_norm_bwd(*last, sv['x'], dx1, w_mix, sc_m, name='mix_dh')
        grads['norm_mlp'][i], grads['norm_mix'][i] = sums_f[2], sums_m[2]
        dmod[i] = jnp.concatenate([sums_m[0], sums_m[1], sums_f[3], sums_f[0], sums_f[1], dgf])
        dgf = sums_m[7]

    out = {k: jnp.stack(v) for k, v in grads.items()}
    for k, per_j in mix_grads.items():
        out[k] = jnp.stack([per_j[j] for j in sorted(per_j)])
    return loss_part, dx, jnp.stack(dmod), out, mat


def _all_gather8(blocks, *, name):
    n = len(blocks)

    def body(*refs):
        x_refs, out_refs = refs[:n], refs[n:2 * n]
        send_sems, recv_sems, local_sems = refs[2 * n:]
        x, y, c = _position()
        me, sibling = (x, y, c), (x, y, 1 - c)
        chips = [(1 - x, y), (x, 1 - y), (1 - x, 1 - y)]

        def slot(a, px, py, pc):
            return out_refs[a].at[4 * px + 2 * py + pc]

        def copy(a, k, blk, to, src=None):
            return pltpu.make_async_remote_copy(
                src_ref=slot(a, *blk) if src is None else src, dst_ref=slot(a, *blk),
                send_sem=send_sems.at[7 * a + k], recv_sem=recv_sems.at[7 * a + k],
                device_id=to, device_id_type=MESH)

        mine = [pltpu.make_async_copy(x_refs[a], slot(a, *me), local_sems.at[a]) for a in range(n)]
        for cp in mine:
            cp.start()
        first = []
        for j, chip in enumerate(chips):
            first += [copy(a, 1 + j, me, (*chip, c), src=x_refs[a]) for a in range(n)]
        first += [copy(a, 0, me, sibling, src=x_refs[a]) for a in range(n)]
        for cp in first:
            cp.start()
        passed = []
        for j, chip in enumerate(chips):
            for a in range(n):
                copy(a, 1 + j, (*chip, c), me).wait_recv()
                passed.append(copy(a, 4 + j, (*chip, c), sibling))
                passed[-1].start()
        for a in range(n):
            copy(a, 0, sibling, me).wait_recv()
        for j, chip in enumerate(chips):
            for a in range(n):
                copy(a, 4 + j, (*chip, 1 - c), me).wait_recv()
        for cp in first + passed:
            cp.wait_send()
        for cp in mine:
            cp.wait()

    return pl.pallas_call(
        body, name=name, in_specs=[ANY] * n, out_specs=[ANY] * n,
        out_shape=[jax.ShapeDtypeStruct((8,) + b.shape, b.dtype) for b in blocks],
        scratch_shapes=[pltpu.SemaphoreType.DMA((7 * n,)), pltpu.SemaphoreType.DMA((7 * n,)),
                        pltpu.SemaphoreType.DMA((n,))],
    )(*blocks)


def _exchange(comm, *, name, aliases=None):
    ns, no = len(comm.srcs), len(comm.out_shapes)

    def body(*refs):
        copies = _comm_copies(comm.plan, refs[:ns], refs[ns:ns + no], *refs[ns + no:])
        _comm_start(copies)
        _comm_wait(copies)

    return pl.pallas_call(
        body, name=name, in_specs=[ANY] * ns, out_specs=[ANY] * no, out_shape=list(comm.out_shapes),
        scratch_shapes=comm.scratch(), input_output_aliases=aliases or {},
    )(*comm.srcs)


def _gather_comm(halves):
    n = len(halves)

    def plan(src, out, x, y, c):
        mine = 4 * x + 2 * y + c
        remote = [(src[a], out[a].at[mine], (x, y, 1 - c), out[a].at[4 * x + 2 * y + 1 - c]) for a in range(n)]
        for fx, fy in CHIP_FLIPS:
            px, py = _flip(x, fx), _flip(y, fy)
            remote += [(src[a], out[a].at[mine], (px, py, c), out[a].at[4 * px + 2 * py + c]) for a in range(n)]
        return remote, [(src[a], out[a].at[mine]) for a in range(n)]

    return Comm(list(halves), [jax.ShapeDtypeStruct((8,) + h.shape, h.dtype) for h in halves], plan, 4 * n, n)


def _gather_forward(bufs, *, name):
    n = len(bufs)

    def plan(src, out, x, y, c):
        remote = []
        for fx, fy in CHIP_FLIPS:
            px, py = _flip(x, fx), _flip(y, fy)
            remote += [(src[a].at[4 * px + 2 * py + c], out[a].at[4 * px + 2 * py + c], (x, y, 1 - c),
                        out[a].at[4 * px + 2 * py + 1 - c]) for a in range(n)]
        return remote, []

    comm = Comm(list(bufs), [jax.ShapeDtypeStruct(b.shape, b.dtype) for b in bufs], plan, 3 * n, 0)
    return _exchange(comm, name=name, aliases={a: a for a in range(n)})


CHIP_FLIPS = ((1, 0), (0, 1), (1, 1))


def _flip(v, f):
    return 1 - v if f else v


def _sum_rows_tile(R, C, budget=3 << 20):
    best = None
    for t in range(8, R + 1, 8):
        if R % t == 0 and t * C * 4 <= budget:
            best = t
    return best if best is not None else R


def _rs_begin(gps, *, wire_dtype):
    n = len(gps)
    c_arr = jnp.reshape(_position()[2], (1,)).astype(jnp.int32)

    def plan(src, out, x, y, c):
        return [(src[a].at[b, 1 - c], out[a].at[b], (x, y, 1 - c), out[a].at[b])
                for a in range(n) for b in range(4)], []

    got1 = _exchange(Comm(list(gps), [jax.ShapeDtypeStruct((4,) + g.shape[2:], F32) for g in gps], plan, 4 * n, 0),
                     name='rs_sibling')

    def sum_chip(c_ref, mine_ref, got_ref, out_ref):
        out_ref[...] = (mine_ref[...] + got_ref[...]).astype(out_ref.dtype)

    parts = []
    for gp, g1 in zip(gps, got1):
        _, _, R, C = gp.shape
        tr = _sum_rows_tile(R, C)
        parts.append(pl.pallas_call(
            sum_chip, name='rs_sum_chip',
            grid_spec=pltpu.PrefetchScalarGridSpec(
                num_scalar_prefetch=1, grid=(4, R // tr),
                in_specs=[pl.BlockSpec((None, None, tr, C), lambda b, r, cr: (b, cr[0], r, 0)),
                          pl.BlockSpec((None, tr, C), lambda b, r, cr: (b, r, 0))],
                out_specs=pl.BlockSpec((None, tr, C), lambda b, r, cr: (b, r, 0))),
            out_shape=jax.ShapeDtypeStruct((4, R, C), wire_dtype),
            compiler_params=_cparams("parallel", "parallel"),
        )(c_arr, gp, g1))
    return got1, parts


def _rs_chips_comm(parts):
    n = len(parts)

    def plan(src, out, x, y, c):
        remote = []
        for k, (fx, fy) in enumerate(CHIP_FLIPS):
            px, py = _flip(x, fx), _flip(y, fy)
            remote += [(src[a].at[2 * px + py], out[a].at[k], (px, py, c), out[a].at[k]) for a in range(n)]
        return remote, []

    return Comm(list(parts), [jax.ShapeDtypeStruct((3,) + p.shape[1:], p.dtype) for p in parts], plan, 3 * n, 0)


def _rs_finish(gps, got1, got2):
    n = len(gps)
    x, y, c = _position()
    bc_arr = jnp.stack([2 * x + y, c]).astype(jnp.int32)

    def sum_final(bc_ref, mine_ref, got1_ref, got2_ref, out_ref):
        acc = mine_ref[...] + got1_ref[...]
        for k in range(3):
            acc = acc + got2_ref[k].astype(F32)
        out_ref[...] = acc

    halves = []
    for gp, g1, g2 in zip(gps, got1, got2):
        _, _, R, C = gp.shape
        tr = _sum_rows_tile(R, C, budget=2 << 20)
        halves.append(pl.pallas_call(
            sum_final, name='rs_sum_final',
            grid_spec=pltpu.PrefetchScalarGridSpec(
                num_scalar_prefetch=1, grid=(R // tr,),
                in_specs=[pl.BlockSpec((None, None, tr, C), lambda r, bc: (bc[0], bc[1], r, 0)),
                          pl.BlockSpec((None, tr, C), lambda r, bc: (bc[0], r, 0)),
                          pl.BlockSpec((3, tr, C), lambda r, bc: (0, r, 0))],
                out_specs=pl.BlockSpec((None, tr, C), lambda r, bc: (bc[1], r, 0))),
            out_shape=jax.ShapeDtypeStruct((2, R, C), F32),
            compiler_params=_cparams("parallel"),
        )(bc_arr, gp, g1, g2))

    def plan(src, out, x, y, c):
        return [(src[a].at[c], out[a].at[c], (x, y, 1 - c), out[a].at[1 - c]) for a in range(n)], []

    comm = Comm(halves, [jax.ShapeDtypeStruct(h.shape, F32) for h in halves], plan, n, 0)
    return _exchange(comm, name='rs_swap', aliases={a: a for a in range(n)})


def _sum8(gathered, *, name):
    _, R, C = gathered.shape

    def body(g_ref, o_ref):
        acc = g_ref[0]
        for k in range(1, 8):
            acc = acc + g_ref[k]
        o_ref[...] = acc

    return pl.pallas_call(body, name=name, out_shape=jax.ShapeDtypeStruct((R, C), F32))(gathered)


def _adamw(w, g, m, v, *, name):
    shape = w.shape
    cols = shape[-1]
    rows = w.size // cols
    tr = _sum_rows_tile(rows, cols, budget=1 << 20)

    def body(w_ref, g_ref, m_ref, v_ref, d_ref, mo_ref, vo_ref):
        gv = g_ref[...]
        mn = ADAM_B1 * m_ref[...] + (1.0 - ADAM_B1) * gv
        vn = ADAM_B2 * v_ref[...] + (1.0 - ADAM_B2) * (gv * gv)
        m_hat = mn / (1.0 - ADAM_B1 ** ADAM_STEP)
        v_hat = vn / (1.0 - ADAM_B2 ** ADAM_STEP)
        d_ref[...] = -ADAM_LR * (m_hat / (jnp.sqrt(v_hat) + ADAM_EPS) + ADAM_WD * w_ref[...])
        mo_ref[...] = mn
        vo_ref[...] = vn

    blk = pl.BlockSpec((tr, cols), lambda i: (i, 0))
    outs = pl.pallas_call(
        body, name=name, grid=(rows // tr,), in_specs=[blk] * 4, out_specs=[blk] * 3,
        out_shape=[jax.ShapeDtypeStruct((rows, cols), F32)] * 3,
        compiler_params=_cparams("parallel"),
    )(*[a.reshape(rows, cols) for a in (w, g, m, v)])
    return tuple(o.reshape(shape) for o in outs)


WEIGHTS = ['norm_mix', 'norm_mlp', 'w_ada', 'b_ada', 'w_mlp_in', 'w_mlp_out', 'fox_w_in', 'fox_b_f',
           'fox_q_norm', 'fox_k_norm', 'fox_w_out', 'sg_w_in', 'sg_ln_g', 'sg_ln_b', 'sg_w_s', 'sg_b_s',
           'sg_w_out', 'cv_w_pw1', 'cv_b_pw1', 'cv_w_dw', 'cv_b_dw', 'cv_ln_g', 'cv_ln_b', 'cv_w_pw2',
           'cv_b_pw2']
BIG = {'w_mlp_in': 2, 'w_mlp_out': 1, 'fox_w_in': 2, 'fox_w_out': 1, 'sg_w_in': 2, 'sg_w_out': 1,
       'cv_w_pw1': 2, 'cv_w_pw2': 1}
SMALL_SHARDED = ['cv_b_pw1', 'cv_w_dw', 'cv_b_dw', 'cv_ln_g', 'cv_ln_b', 'cv_b_pw2']
SMALL_GRADS = ['norm_mix', 'norm_mlp', 'fox_b_f', 'fox_q_norm', 'fox_k_norm', 'sg_ln_g', 'sg_ln_b', 'sg_w_s',
               'sg_b_s'] + SMALL_SHARDED
GRAD_WIRE_DTYPE = BF16


def _pack_rows(parts, cols):
    flat = jnp.concatenate([p.reshape(-1) for p in parts])
    rows = -(-flat.size // (8 * cols)) * 8
    return jnp.pad(flat, (0, rows * cols - flat.size)).reshape(rows, cols)


def _unpack(flat, shapes):
    out, off = [], 0
    for s in shapes:
        n = math.prod(s)
        out.append(flat[..., off:off + n].reshape(flat.shape[:-1] + tuple(s)))
        off += n
    return out


def _merge_chips(a, axis):
    a = jnp.moveaxis(a, 0, axis)
    return a.reshape(a.shape[:axis] + (a.shape[axis] * a.shape[axis + 1],) + a.shape[axis + 2:])


def _split_chips(a, axis):
    a = a.reshape(a.shape[:axis] + (4, a.shape[axis] // 4) + a.shape[axis + 1:])
    return jnp.moveaxis(a, axis, 0)


def _step(a):
    x, y, c = _position()
    me = 4 * x + 2 * y + c
    chip = 2 * x + y
    T, D = a['x'].shape[1], a['x'].shape[2]
    L = a['norm_mix'].shape[0]

    small_shapes = [(D,)] + [a[n].shape for n in SMALL_SHARDED]
    small = _all_gather8([_pack_rows([a['c']] + [a[n] for n in SMALL_SHARDED], LANE)], name='ag_small')[0]
    small = small.reshape(8, -1)
    c_all = _unpack(small, small_shapes[:1])[0]
    sharded = _unpack(small[0::2, D:], small_shapes[1:])
    P = {n: _merge_chips(v, v.ndim - 2) for n, v in zip(SMALL_SHARDED, sharded)}

    c_act = _silu_rows(c_all, name='c_act')
    mod_cols = jnp.stack([
        _mm(c_act, a['w_ada'][i], name='ada_mod', tm=8, tn=_col_tile(a['w_ada'].shape[2], 768),
            extras=[(lax.dynamic_slice_in_dim(a['b_ada'][i:i + 1], chip * a['w_ada'].shape[2],
                                              a['w_ada'].shape[2], axis=1), 'row')],
            epilogue=_bias_epilogue)
        for i in range(L)])
    mod_all = _all_gather8([mod_cols.reshape(L * 8, -1)], name='ag_mod')[0].reshape(8, L, 8, -1)
    mod = lax.dynamic_index_in_dim(mod_all[0::2], me, axis=2, keepdims=False)
    mod = jnp.moveaxis(mod, 0, 1).reshape(L, 6 * D)

    units = _matrix_units(L)
    first, with_prep, with_attn, with_last = units[:1], units[1:4], units[4:-3], units[-3:]
    last, earlier = units[:2], units[2:]
    n_heads = D // HEAD_DIM

    def half_block(unit):
        blk = a[unit[0]][unit[1]]
        return lax.dynamic_index_in_dim(blk.astype(BF16).reshape(2, blk.shape[0] // 2, blk.shape[1]), c, axis=0,
                                        keepdims=False)

    def install(group, gathered):
        for (name, idx), gth in zip(group, gathered):
            blocks = gth.reshape((4,) + a[name].shape[1:])
            if name == 'fox_w_in':
                pad = jnp.zeros((blocks.shape[1], LANE - n_heads), BF16)
                full = jnp.concatenate([blocks[0], blocks[1], blocks[2], blocks[3], pad], axis=-1)
            else:
                full = _merge_chips(blocks, BIG[name] - 1)
            if name in ('w_mlp_in', 'w_mlp_out', 'fox_w_in', 'fox_w_out'):
                P.setdefault(name, {})[idx] = full
            else:
                P[name] = full

    install(first, _all_gather8([half_block(u) for u in first], name='ag_weights_first'))
    for n in ('sg_w_s', 'sg_b_s', 'cv_w_dw'):
        P[n] = (P[n] if n in P else a[n])[0]
    for n in ('norm_mix', 'norm_mlp', 'fox_b_f', 'fox_q_norm', 'fox_k_norm', 'sg_ln_g', 'sg_ln_b'):
        P[n] = a[n]

    def split_grad(unit, grad):
        name = unit[0]
        if name == 'fox_w_in':
            grad = grad[:, :a[name].shape[2] * N_CHIPS]
        blk = grad if grad.ndim == 3 else _split_chips(grad, BIG[name] - 1)
        return blk.reshape(N_CHIPS, 2, blk.shape[1] // 2, blk.shape[2])

    state = {}

    def riding(group):
        return (_gather_comm([half_block(u) for u in group]),
                lambda outs: install(group, _gather_forward(outs, name='ag_weights_forward')))

    def bwd_comm(mat):
        state['gps'] = [split_grad(u, mat[u]) for u in earlier]
        state['got1'], parts = _rs_begin(state['gps'], wire_dtype=GRAD_WIRE_DTYPE)
        return _rs_chips_comm(parts)

    def bwd_done(outs):
        state['got2'] = outs

    last_fox = N_MIXERS * ((L - 1) // N_MIXERS)
    hooks = Hooks({0: {'prep': riding(with_prep), 'attn': riding(with_attn)}, last_fox: {'attn': riding(with_last)}},
                  bwd_comm, bwd_done)
    loss_part, grad_x, dmod, g, mat = _local_step(a['x'][0], a['loss_target'][0], mod, P, hooks)

    small_g = [dmod, loss_part[0:1, 0:1]] + [g[n] for n in SMALL_GRADS]
    small_g_shapes = [s.shape for s in small_g]
    all_small = _all_gather8([_pack_rows(small_g, LANE)], name='ag_small_grads')[0]
    summed = _sum8(all_small, name='sum_small_grads').reshape(-1)
    sums = _unpack(summed, small_g_shapes)
    loss = sums[1][0, 0]
    grads = dict(zip(SMALL_GRADS, sums[2:]))
    grads['b_ada'] = sums[0]
    for n in SMALL_SHARDED:
        blk = a[n].shape[-1]
        grads[n] = lax.dynamic_slice_in_dim(grads[n], chip * blk, blk, axis=grads[n].ndim - 1)
    dmod_all = all_small.reshape(8, -1)[:, :dmod.size].reshape(8, L, 6 * D)
    cols = a['w_ada'].shape[2]
    dmod_cols = lax.dynamic_slice_in_dim(dmod_all, chip * cols, cols, axis=2)
    pad8 = lambda t: jnp.pad(t, ((0, LANE - 8), (0, 0)))
    c_act_pad = pad8(c_act)
    grads['w_ada'] = jnp.stack([
        _mm(c_act_pad, pad8(dmod_cols[:, i]), ta=True, name='ada_dw', tn=_col_tile(cols, 768))
        for i in range(L)])

    shards = dict(zip(earlier, _rs_finish(state['gps'], state['got1'], state['got2'])))
    gps = [split_grad(u, mat[u]) for u in last]
    got1, parts = _rs_begin(gps, wire_dtype=GRAD_WIRE_DTYPE)
    got2 = _exchange(_rs_chips_comm(parts), name='rs_chips')
    shards.update(zip(last, _rs_finish(gps, got1, got2)))
    for n in BIG:
        grads[n] = jnp.stack([shards[n, idx].reshape(a[n].shape[1:]) for idx in range(a[n].shape[0])])

    deltas, new_m, new_v = {}, {}, {}
    for n in WEIGHTS:
        deltas[n], new_m[n], new_v[n] = _adamw(a[n], grads[n], a['m_' + n], a['v_' + n], name='adamw')
    return (loss, grad_x[None], *[grads[n] for n in WEIGHTS], *[deltas[n] for n in WEIGHTS],
            *[new_m[n] for n in WEIGHTS], *[new_v[n] for n in WEIGHTS])


def _matrix_units(n_layers):
    mixers = (('fox_w_in', 'fox_w_out'), ('sg_w_in', 'sg_w_out'), ('cv_w_pw1', 'cv_w_pw2'))
    units = []
    for i in range(n_layers):
        units += [(n, i // N_MIXERS) for n in mixers[i % N_MIXERS]] + [('w_mlp_in', i), ('w_mlp_out', i)]
    return units


def _silu_rows(x, *, name):
    def body(x_ref, o_ref):
        xv = x_ref[...]
        o_ref[...] = (xv * jax.nn.sigmoid(xv)).astype(BF16)

    return pl.pallas_call(body, name=name, out_shape=jax.ShapeDtypeStruct(x.shape, BF16))(x)


def kernel(x, c, norm_mix, norm_mlp, w_ada, b_ada, w_mlp_in, w_mlp_out, fox_w_in, fox_b_f, fox_q_norm, fox_k_norm, fox_w_out, sg_w_in, sg_ln_g, sg_ln_b, sg_w_s, sg_b_s, sg_w_out, cv_w_pw1, cv_b_pw1, cv_w_dw, cv_b_dw, cv_ln_g, cv_ln_b, cv_w_pw2, cv_b_pw2, loss_target, m_norm_mix, m_norm_mlp, m_w_ada, m_b_ada, m_w_mlp_in, m_w_mlp_out, m_fox_w_in, m_fox_b_f, m_fox_q_norm, m_fox_k_norm, m_fox_w_out, m_sg_w_in, m_sg_ln_g, m_sg_ln_b, m_sg_w_s, m_sg_b_s, m_sg_w_out, m_cv_w_pw1, m_cv_b_pw1, m_cv_w_dw, m_cv_b_dw, m_cv_ln_g, m_cv_ln_b, m_cv_w_pw2, m_cv_b_pw2, v_norm_mix, v_norm_mlp, v_w_ada, v_b_ada, v_w_mlp_in, v_w_mlp_out, v_fox_w_in, v_fox_b_f, v_fox_q_norm, v_fox_k_norm, v_fox_w_out, v_sg_w_in, v_sg_ln_g, v_sg_ln_b, v_sg_w_s, v_sg_b_s, v_sg_w_out, v_cv_w_pw1, v_cv_b_pw1, v_cv_w_dw, v_cv_b_dw, v_cv_ln_g, v_cv_ln_b, v_cv_w_pw2, v_cv_b_pw2):
    return _step(dict(locals()))
```

```python
import math
from typing import Callable, NamedTuple

import jax
import jax.numpy as jnp
from jax import lax
from jax.experimental import pallas as pl
from jax.experimental.pallas import tpu as pltpu

F32 = jnp.float32
BF16 = jnp.bfloat16

EPS = 1e-6
HEAD_DIM = 64
LANE = 128
CONV_WIDTH = 31
CONV_HALO = 32
SG_CHUNK = 128
SG_CAUSAL = 64
SG_GROUPS = 8
N_MIXERS = 3
N_CHIPS = 4
VMEM_LIMIT = 56 * 1024 * 1024
NEG = -1e30

ADAM_LR = 0.001
ADAM_B1 = 0.9
ADAM_B2 = 0.999
ADAM_EPS = 1e-08
ADAM_WD = 0.01
ADAM_STEP = 10

MESH = pl.DeviceIdType.MESH
ANY = pl.BlockSpec(memory_space=pl.ANY)


def _cparams(*sem):
    return pltpu.CompilerParams(dimension_semantics=sem, vmem_limit_bytes=VMEM_LIMIT)


WIDE_ROWS = 1024


def _row_tile(t, want=512):
    return min(t, want)


def _matmul(a, b, *, name, ta=False, tb=False, tm=512, tn=1024, tk=1024,
            extras=(), epilogue=None, out_dtypes=(F32,), b_outer=False, out_chips=None, row_sums=False):
    M, K = (a.shape[1], a.shape[0]) if ta else a.shape
    N = b.shape[0] if tb else b.shape[1]
    assert (b.shape[1] if tb else b.shape[0]) == K
    n_own = N // out_chips if out_chips else N
    tm, tn, tk = min(tm, M), min(tn, n_own), min(tk, K)
    assert M % tm == 0 and n_own % tn == 0 and K % tk == 0, (name, M, N, K, tm, tn, tk)
    nk = K // tk

    def spec(shape, pick):
        if b_outer:
            return pl.BlockSpec(shape, lambda j, i, k: pick(i, j, k))
        return pl.BlockSpec(shape, pick)

    a_spec = spec((tk, tm), lambda i, j, k: (k, i)) if ta else spec((tm, tk), lambda i, j, k: (i, k))
    b_spec = spec((tn, tk), lambda i, j, k: (j, k)) if tb else spec((tk, tn), lambda i, j, k: (k, j))
    ex_specs = [spec((tm, tn), lambda i, j, k: (i, j)) if kind == 'tile' else spec((1, tn), lambda i, j, k: (0, j))
                for _, kind in extras]
    dims = (((0,) if ta else (1,), (1,) if tb else (0,)), ((), ()))
    n_ex, n_out = len(extras), len(out_dtypes) + bool(row_sums)
    assert not row_sums or N == tn

    def body(*refs):
        a_ref, b_ref = refs[0], refs[1]
        ex = refs[2:2 + n_ex]
        outs = refs[2 + n_ex:2 + n_ex + n_out]

        def finish(acc):
            vals = epilogue(acc, *[r[...] for r in ex]) if epilogue else (acc,)
            for o, v in zip(outs[:len(out_dtypes)], vals):
                o[...] = v.astype(o.dtype)
            if row_sums:
                row_tile = pl.program_id(1 if b_outer else 0)

                @pl.when(row_tile == 0)
                def _():
                    outs[-1][...] = vals[-1]

                @pl.when(row_tile > 0)
                def _():
                    outs[-1][...] += vals[-1]

        part = lax.dot_general(a_ref[...].astype(BF16), b_ref[...].astype(BF16), dims,
                               preferred_element_type=F32)
        if nk == 1:
            finish(part)
        else:
            acc_ref = refs[-1]
            k = pl.program_id(2)

            @pl.when(k == 0)
            def _():
                acc_ref[...] = part

            @pl.when(k > 0)
            def _():
                acc_ref[...] += part

            @pl.when(k == nk - 1)
            def _():
                finish(acc_ref[...])

    outs = pl.pallas_call(
        body, name=name,
        grid=(N // tn, M // tm, nk) if b_outer else (M // tm, N // tn, nk),
        in_specs=[a_spec, b_spec] + ex_specs,
        out_specs=[spec((None, tm, tn), lambda i, j, k: (j // (n_own // tn), i, j % (n_own // tn)))
                   if out_chips else spec((tm, tn), lambda i, j, k: (i, j)) for _ in out_dtypes]
        + ([spec((8, tn), lambda i, j, k: (0, j))] if row_sums else []),
        out_shape=[jax.ShapeDtypeStruct((out_chips, M, n_own) if out_chips else (M, N), dt) for dt in out_dtypes]
        + ([jax.ShapeDtypeStruct((8, N), F32)] if row_sums else []),
        scratch_shapes=[pltpu.VMEM((tm, tn), F32)] if nk > 1 else [],
        compiler_params=(_cparams("arbitrary", "arbitrary", "arbitrary") if row_sums
                         else _cparams("parallel", "parallel", "arbitrary")),
    )(a, b, *[arr for arr, _ in extras])
    return outs if n_out > 1 else outs[0]


def _norm_mod_fwd(x, w, sc, sh, *, name):
    T, D = x.shape
    tr = _row_tile(T, WIDE_ROWS)

    def body(x_ref, w_ref, sc_ref, sh_ref, h_ref):
        xv = x_ref[...]
        r = lax.rsqrt(jnp.mean(xv * xv, axis=-1, keepdims=True) + EPS)
        h_ref[...] = ((xv * r) * w_ref[...] * (1.0 + sc_ref[...]) + sh_ref[...]).astype(BF16)

    row = pl.BlockSpec((1, D), lambda i: (0, 0))
    return pl.pallas_call(
        body, name=name, grid=(T // tr,),
        in_specs=[pl.BlockSpec((tr, D), lambda i: (i, 0)), row, row, row],
        out_specs=pl.BlockSpec((tr, D), lambda i: (i, 0)),
        out_shape=jax.ShapeDtypeStruct((T, D), BF16),
        compiler_params=_cparams("parallel"),
    )(x, w, sc, sh)


def _loss_head(y, target, z, g, *, name):
    T, D = y.shape
    tr = _row_tile(T)

    def body(y_ref, t_ref, z_ref, g_ref, loss_ref, dy_ref, dz_ref, dg_ref):
        i = pl.program_id(0)
        e = y_ref[...] - t_ref[...]
        dy = e * (1.0 / D)
        dy_ref[...] = dy
        dz_ref[...] = (dy * g_ref[...]).astype(BF16)
        part = jnp.full((8, LANE), 0.5 / D * jnp.sum(e * e), F32)
        dg = jnp.concatenate([jnp.sum(dy * z_ref[...], axis=0, keepdims=True), jnp.zeros((7, D), F32)], axis=0)

        @pl.when(i == 0)
        def _():
            loss_ref[...] = part
            dg_ref[...] = dg

        @pl.when(i > 0)
        def _():
            loss_ref[...] += part
            dg_ref[...] += dg

    blk = pl.BlockSpec((tr, D), lambda i: (i, 0))
    return pl.pallas_call(
        body, name=name, grid=(T // tr,), in_specs=[blk, blk, blk, pl.BlockSpec((1, D), lambda i: (0, 0))],
        out_specs=[pl.BlockSpec((8, LANE), lambda i: (0, 0)), blk, blk, pl.BlockSpec((8, D), lambda i: (0, 0))],
        out_shape=[jax.ShapeDtypeStruct((8, LANE), F32), jax.ShapeDtypeStruct((T, D), F32),
                   jax.ShapeDtypeStruct((T, D), BF16), jax.ShapeDtypeStruct((8, D), F32)],
        compiler_params=_cparams("arbitrary"),
    )(y, target, z, g)


def _position():
    return lax.axis_index("x"), lax.axis_index("y"), lax.axis_index("c")


class Comm(NamedTuple):
    srcs: list
    out_shapes: list
    plan: Callable
    n_remote: int
    n_local: int

    def scratch(self):
        return [pltpu.SemaphoreType.DMA((self.n_remote,)), pltpu.SemaphoreType.DMA((self.n_remote,)),
                pltpu.SemaphoreType.DMA((max(self.n_local, 1),))]


def _comm_copies(plan, src_refs, out_refs, send_sems, recv_sems, local_sems):
    x, y, c = _position()
    remote, local = plan(src_refs, out_refs, x, y, c)

    def copy(k, s, d, peer):
        return pltpu.make_async_remote_copy(src_ref=s, dst_ref=d, send_sem=send_sems.at[k],
                                            recv_sem=recv_sems.at[k], device_id=peer, device_id_type=MESH)

    sends = [copy(k, s, d, peer) for k, (s, d, peer, _) in enumerate(remote)]
    recvs = [copy(k, s, landing, peer) for k, (s, _, peer, landing) in enumerate(remote)]
    local_copies = [pltpu.make_async_copy(s, d, local_sems.at[i]) for i, (s, d) in enumerate(local)]
    return sends, recvs, local_copies


def _comm_start(copies):
    sends, _, local_copies = copies
    for cp in local_copies + sends:
        cp.start()


def _comm_wait(copies):
    sends, recvs, local_copies = copies
    for cp in recvs:
        cp.wait_recv()
    for cp in sends:
        cp.wait_send()
    for cp in local_copies:
        cp.wait()


def _split_comm_refs(refs, n_in, n_out, n_scratch, comm):
    ns, nd = (len(comm.srcs), len(comm.out_shapes)) if comm else (0, 0)
    cuts = [n_in, ns, n_out, nd, n_scratch]
    parts, at = [], 0
    for n in cuts:
        parts.append(refs[at:at + n])
        at += n
    return (*parts, refs[at:])


AUG_F = HEAD_DIM
AUG_LSE = HEAD_DIM + 6


def _half_cols(x, lo):
    return (jnp.sum(jnp.where(lo, x, 0.0), axis=-1, keepdims=True),
            jnp.sum(jnp.where(lo, 0.0, x), axis=-1, keepdims=True))


def _half_sums(x, lo):
    s_lo, s_hi = _half_cols(x, lo)
    return jnp.where(lo, s_lo, s_hi)


def _split3(x):
    a = x.astype(BF16).astype(F32)
    r = x - a
    b = r.astype(BF16).astype(F32)
    return a, b, (r - b).astype(BF16).astype(F32)


def _aug(lane, base, terms):
    out = jnp.zeros(lane.shape, F32)
    for i, t in enumerate(terms):
        out = jnp.where(lane == base + i, t, out)
    return out


def _head_lanes(x2, h):
    return x2 if h == 0 else pltpu.roll(x2, HEAD_DIM, 1)


def _fox_prep_fwd(proj, qg, kg, fcol, *, d_model, name, comm=None):
    T = proj.shape[0]
    nhp = d_model // LANE
    tr = _row_tile(T, 2 * WIDE_ROWS)

    def body(*refs):
        ((q_ref, k_ref, v_ref, qg_ref, kg_ref, f_ref), src_refs,
         (qa_ref, qta_ref, ka_ref, kta_ref, va_ref, vta_ref), dst_refs, _, sems) = _split_comm_refs(refs, 6, 6, 0, comm)
        if comm:
            @pl.when((pl.program_id(0) == 0) & (pl.program_id(1) == 0))
            def _():
                _comm_start(_comm_copies(comm.plan, src_refs, dst_refs, *sems))
        lane = lax.broadcasted_iota(jnp.int32, (tr, LANE), 1)
        lo = lane < HEAD_DIM

        def norm(xv, g):
            ms = _half_sums(xv * xv, lo) * (1.0 / HEAD_DIM)
            return (xv * lax.rsqrt(ms + EPS)) * g

        qn = norm(q_ref[...], qg_ref[...]) * (HEAD_DIM ** -0.5)
        kn = norm(k_ref[...], kg_ref[...])
        vv = v_ref[...]
        qa, ka, va, vta = [], [], [], []
        for h in range(2):
            f1, f2, f3 = _split3(f_ref[h])
            qa.append(jnp.where(lo, _head_lanes(qn, h), _aug(lane, AUG_F, [f1, f2, f3, 1.0, 1.0, 1.0])))
            ka.append(jnp.where(lo, _head_lanes(kn, h),
                                _aug(lane, AUG_F, [1.0, 1.0, 1.0, -f1, -f2, -f3, 1.0, 1.0, 1.0])))
            va.append(jnp.where(lo if h == 0 else jnp.logical_not(lo), vv, 0.0))
            vta.append(jnp.where(lo, _head_lanes(vv, h), _aug(lane, AUG_F, [1.0, 1.0, 1.0])))
        for parts, ref, tref in ((qa, qa_ref, qta_ref), (ka, ka_ref, kta_ref), (va, va_ref, None),
                                 (vta, None, vta_ref)):
            both = jnp.concatenate(parts, axis=1)
            if ref is not None:
                ref[...] = both.astype(BF16)
            if tref is not None:
                tref[...] = both.astype(BF16).T
        if comm:
            @pl.when((pl.program_id(0) == T // tr - 1) & (pl.program_id(1) == nhp - 1))
            def _():
                _comm_wait(_comm_copies(comm.plan, src_refs, dst_refs, *sems))

    gain = pl.BlockSpec((1, LANE), lambda i, h: (0, 0))
    rows = pl.BlockSpec((tr, 2 * LANE), lambda i, h: (i, h))
    cols = pl.BlockSpec((2 * LANE, tr), lambda i, h: (h, i))
    wide, tall = jax.ShapeDtypeStruct((T, 2 * d_model), BF16), jax.ShapeDtypeStruct((2 * d_model, T), BF16)
    outs = pl.pallas_call(
        body, name=name, grid=(T // tr, nhp),
        in_specs=[pl.BlockSpec((tr, LANE), lambda i, h: (i, h)),
                  pl.BlockSpec((tr, LANE), lambda i, h: (i, nhp + h)),
                  pl.BlockSpec((tr, LANE), lambda i, h: (i, 2 * nhp + h)), gain, gain,
                  pl.BlockSpec((2, tr, 1), lambda i, h: (h, i, 0))] + ([ANY] * len(comm.srcs) if comm else []),
        out_specs=[rows, cols, rows, cols, rows, cols] + ([ANY] * len(comm.out_shapes) if comm else []),
        out_shape=[wide, tall, wide, tall, wide, tall] + (list(comm.out_shapes) if comm else []),
        scratch_shapes=comm.scratch() if comm else [],
        compiler_params=_cparams("arbitrary", "arbitrary") if comm else _cparams("parallel", "parallel"),
    )(proj, proj, proj, qg, kg, fcol, *(comm.srcs if comm else []))
    return outs[:6], outs[6:]


def _fox_do_prep(do, o, *, name):
    T, D = do.shape
    nhp = D // LANE
    tr = _row_tile(T, 2 * WIDE_ROWS)

    def body(do_ref, o_ref, doa_ref, dota_ref):
        lane = lax.broadcasted_iota(jnp.int32, (tr, LANE), 1)
        lo = lane < HEAD_DIM
        dob = do_ref[...].astype(BF16).astype(F32)
        deltas = _half_cols(dob * o_ref[...], lo)
        both = jnp.concatenate(
            [jnp.where(lo, _head_lanes(dob, h), _aug(lane, AUG_F, _split3(-deltas[h]))) for h in range(2)], axis=1)
        doa_ref[...] = both.astype(BF16)
        dota_ref[...] = both.astype(BF16).T

    blk = pl.BlockSpec((tr, LANE), lambda i, h: (i, h))
    return pl.pallas_call(
        body, name=name, grid=(T // tr, nhp), in_specs=[blk, blk],
        out_specs=[pl.BlockSpec((tr, 2 * LANE), lambda i, h: (i, h)),
                   pl.BlockSpec((2 * LANE, tr), lambda i, h: (h, i))],
        out_shape=[jax.ShapeDtypeStruct((T, 2 * D), BF16), jax.ShapeDtypeStruct((2 * D, T), BF16)],
        compiler_params=_cparams("parallel", "parallel"),
    )(do, o)


def _fox_prep_bwd(proj, dq, dkt, dvt, qg, kg, *, d_model, name):
    T = proj.shape[0]
    nhp = d_model // LANE
    tr = _row_tile(T, 2 * WIDE_ROWS)

    def body(q_ref, k_ref, dq_ref, dkt_ref, dvt_ref, qg_ref, kg_ref, dqo_ref, dko_ref, dvo_ref, sums_ref):
        first = (pl.program_id(0) == 0) & (pl.program_id(1) == 0)
        lo = lax.broadcasted_iota(jnp.int32, (tr, LANE), 1) < HEAD_DIM

        def pair(x2):
            return jnp.where(lo, x2[:, :LANE], pltpu.roll(x2[:, LANE:], HEAD_DIM, 1))

        def bwd(xv, dxhat, g):
            ms = _half_sums(xv * xv, lo) * (1.0 / HEAD_DIM)
            r = lax.rsqrt(ms + EPS)
            n = xv * r
            dn = dxhat * g
            dx = r * (dn - n * (_half_sums(dn * n, lo) * (1.0 / HEAD_DIM)))
            dg = jnp.sum(dxhat * n, axis=0, keepdims=True)
            return dx, dg + pltpu.roll(dg, HEAD_DIM, 1)

        dxq, dgq = bwd(q_ref[...], pair(dq_ref[...]) * (HEAD_DIM ** -0.5), qg_ref[...])
        dxk, dgk = bwd(k_ref[...], pair(dkt_ref[...].T), kg_ref[...])
        dqo_ref[...] = dxq.astype(BF16)
        dko_ref[...] = dxk.astype(BF16)
        dvo_ref[...] = pair(dvt_ref[...].T.astype(F32)).astype(BF16)
        part = jnp.concatenate([dgq, dgk, jnp.zeros((6, LANE), F32)], axis=0)

        @pl.when(first)
        def _():
            sums_ref[...] = part

        @pl.when(jnp.logical_not(first))
        def _():
            sums_ref[...] += part

    gain = pl.BlockSpec((1, LANE), lambda i, h: (0, 0))
    blk = pl.BlockSpec((tr, LANE), lambda i, h: (i, h))
    tall = pl.BlockSpec((2 * LANE, tr), lambda i, h: (h, i))
    return pl.pallas_call(
        body, name=name, grid=(T // tr, nhp),
        in_specs=[blk, pl.BlockSpec((tr, LANE), lambda i, h: (i, nhp + h)),
                  pl.BlockSpec((tr, 2 * LANE), lambda i, h: (i, h)), tall, tall, gain, gain],
        out_specs=[blk, blk, blk, pl.BlockSpec((8, LANE), lambda i, h: (0, 0))],
        out_shape=[jax.ShapeDtypeStruct((T, d_model), BF16)] * 3 + [jax.ShapeDtypeStruct((8, LANE), F32)],
        compiler_params=_cparams("arbitrary", "arbitrary"),
    )(proj, proj, dq, dkt, dvt, qg, kg)


def _scan_lanes(x, reverse):
    n = x.shape[-1]
    lane = lax.broadcasted_iota(jnp.int32, x.shape, 1)
    sh = 1
    while sh < n:
        if reverse:
            x = x + jnp.where(lane < n - sh, pltpu.roll(x, n - sh, 1), 0.0)
        else:
            x = x + jnp.where(lane >= sh, pltpu.roll(x, sh, 1), 0.0)
        sh *= 2
    return x


def _fox_gate_fwd(fpre_t, bf, *, name):
    def body(f_ref, b_ref, o_ref):
        xv = f_ref[...] + b_ref[...]
        logf = jnp.minimum(xv, 0.0) - jnp.log1p(jnp.exp(-jnp.abs(xv)))
        o_ref[...] = _scan_lanes(logf, reverse=False)

    return pl.pallas_call(body, name=name, out_shape=jax.ShapeDtypeStruct(fpre_t.shape, F32))(fpre_t, bf)


def _fox_gate_bwd(dcol, drow, fpre_t, bf, *, name):
    H = fpre_t.shape[0]

    def body(dc_ref, dr_ref, f_ref, b_ref, o_ref, db_ref):
        xv = f_ref[...] + b_ref[...]
        e = dc_ref[...] - dr_ref[...]
        dlogf = _scan_lanes(e, reverse=False) - e
        dpre = dlogf * (1.0 - jax.nn.sigmoid(xv))
        o_ref[...] = dpre
        db_ref[...] = jnp.broadcast_to(jnp.sum(dpre, axis=-1, keepdims=True), (H, LANE))

    return pl.pallas_call(
        body, name=name,
        out_shape=[jax.ShapeDtypeStruct(fpre_t.shape, F32), jax.ShapeDtypeStruct((H, LANE), F32)],
    )(dcol, drow, fpre_t, bf)


_NT = (((1,), (1,)), ((), ()))
_TN = (((0,), (0,)), ((), ()))
_NN = (((1,), (0,)), ((), ()))


def _attn_tile(T):
    return min(T, 1024)


def _causal(tq, tk):
    return lax.broadcasted_iota(jnp.int32, (tq, tk), 1) <= lax.broadcasted_iota(jnp.int32, (tq, tk), 0)


def _fox_attn_fwd(qa, kta, va, *, name, comm=None):
    T = qa.shape[0]
    nhp = qa.shape[1] // (2 * LANE)
    tq = tk = _attn_tile(T)
    nq = T // tq

    def body(*refs):
        (qa_ref, kta_ref, va_ref), src_refs, (o_ref, qb_ref), dst_refs, (m_sc, l_sc, acc_sc), sems = (
            _split_comm_refs(refs, 3, 2, 3, comm))
        hp, i, j = pl.program_id(0), pl.program_id(1), pl.program_id(2)
        if comm:
            @pl.when((hp == 0) & (i == 0) & (j == 0))
            def _():
                _comm_start(_comm_copies(comm.plan, src_refs, dst_refs, *sems))

        @pl.when(j == 0)
        def _():
            m_sc[...] = jnp.full(m_sc.shape, NEG, F32)
            l_sc[...] = jnp.zeros(l_sc.shape, F32)
            acc_sc[...] = jnp.zeros(acc_sc.shape, F32)

        def block(diagonal):
            heads = [slice(h * LANE, (h + 1) * LANE) for h in range(2)]
            scores = [lax.dot_general(qa_ref[:, hs], kta_ref[hs, :], _NN, preferred_element_type=F32)
                      for hs in heads]
            state = [(m_sc[h], l_sc[h], acc_sc[h]) for h in range(2)]
            probs, updates = [], []
            for s, (m_prev, l_prev, _) in zip(scores, state):
                if diagonal:
                    s = jnp.where(_causal(tq, tk), s, NEG)
                m_next = jnp.maximum(m_prev, jnp.max(s, axis=1, keepdims=True))
                p = jnp.exp(s - jnp.tile(m_next, (1, tk // LANE)))
                alpha = jnp.exp(m_prev - m_next)
                probs.append(p.astype(BF16))
                updates.append((m_next, alpha, alpha * l_prev + jnp.sum(p, axis=1, keepdims=True)))
            pvs = [lax.dot_general(p, va_ref[:, hs], _NN, preferred_element_type=F32)
                   for p, hs in zip(probs, heads)]
            for h in range(2):
                m_next, alpha, l_next = updates[h]
                m_sc[h] = m_next
                l_sc[h] = l_next
                acc_sc[h] = alpha * state[h][2] + pvs[h]

        @pl.when(j < i)
        def _():
            block(False)

        @pl.when(j == i)
        def _():
            block(True)
            o_ref[...] = acc_sc[0] / l_sc[0] + acc_sc[1] / l_sc[1]
            lane = lax.broadcasted_iota(jnp.int32, (tq, LANE), 1)
            for h in range(2):
                hs = slice(h * LANE, (h + 1) * LANE)
                pieces = _split3(-(m_sc[h] + jnp.log(l_sc[h])))
                qb = qa_ref[:, hs].astype(F32)
                for n, piece in enumerate(pieces):
                    qb = jnp.where(lane == AUG_LSE + n, piece, qb)
                qb_ref[:, hs] = qb.astype(BF16)

        if comm:
            @pl.when((hp == nhp - 1) & (i == nq - 1) & (j == nq - 1))
            def _():
                _comm_wait(_comm_copies(comm.plan, src_refs, dst_refs, *sems))

    outs = pl.pallas_call(
        body, name=name, grid=(nhp, nq, nq),
        in_specs=[pl.BlockSpec((tq, 2 * LANE), lambda h, i, j: (i, h)),
                  pl.BlockSpec((2 * LANE, tk), lambda h, i, j: (h, jnp.minimum(j, i))),
                  pl.BlockSpec((tk, 2 * LANE), lambda h, i, j: (jnp.minimum(j, i), h))]
        + ([ANY] * len(comm.srcs) if comm else []),
        out_specs=[pl.BlockSpec((tq, LANE), lambda h, i, j: (i, h)),
                   pl.BlockSpec((tq, 2 * LANE), lambda h, i, j: (i, h))]
        + ([ANY] * len(comm.out_shapes) if comm else []),
        out_shape=[jax.ShapeDtypeStruct((T, nhp * LANE), F32), jax.ShapeDtypeStruct(qa.shape, BF16)]
        + (list(comm.out_shapes) if comm else []),
        scratch_shapes=[pltpu.VMEM((2, tq, LANE), F32), pltpu.VMEM((2, tq, LANE), F32),
                        pltpu.VMEM((2, tq, LANE), F32)] + (comm.scratch() if comm else []),
        compiler_params=(_cparams("arbitrary", "arbitrary", "arbitrary") if comm
                         else _cparams("parallel", "parallel", "arbitrary")),
    )(qa, kta, va, *(comm.srcs if comm else []))
    return outs[0], outs[1], outs[2:]


def _fox_attn_bwd(qb, qta, ka, kta, vta, doa, dota, *, name, comm=None):
    T = qb.shape[0]
    nhp = qb.shape[1] // (2 * LANE)
    tq = tk = _attn_tile(T)
    nq = T // tq

    def body(*refs):
        ((qb_ref, qta_ref, ka_ref, kta_ref, vta_ref, doa_ref, dota_ref), src_refs,
         (dq_ref, dkt_ref, dvt_ref, dcol_ref, drow_ref), dst_refs, (dkt_sc, dvt_sc, dcol_sc), sems) = (
            _split_comm_refs(refs, 7, 5, 3, comm))
        hp, j, i = pl.program_id(0), pl.program_id(1), pl.program_id(2)
        if comm:
            @pl.when((hp == 0) & (j == 0) & (i == 0))
            def _():
                _comm_start(_comm_copies(comm.plan, src_refs, dst_refs, *sems))

        @pl.when((j == 0) & (i == 0))
        def _():
            dq_ref[...] = jnp.zeros(dq_ref.shape, F32)
            drow_ref[...] = jnp.zeros(drow_ref.shape, F32)

        @pl.when(i == 0)
        def _():
            dkt_sc[...] = jnp.zeros(dkt_sc.shape, F32)
            dvt_sc[...] = jnp.zeros(dvt_sc.shape, F32)
            dcol_sc[...] = jnp.zeros(dcol_sc.shape, F32)

        def block(diagonal):
            rows = pl.ds(pl.multiple_of(i * tq, tq), tq)
            heads = [slice(h * LANE, (h + 1) * LANE) for h in range(2)]
            logits = [lax.dot_general(qb_ref[:, hs], kta_ref[hs, :], _NN, preferred_element_type=F32)
                      for hs in heads]
            dpds = [lax.dot_general(doa_ref[:, hs], vta_ref[hs, :], _NN, preferred_element_type=F32)
                    for hs in heads]
            pbs, dlbs = [], []
            for h in range(2):
                p = jnp.exp(logits[h])
                if diagonal:
                    p = jnp.where(_causal(tq, tk), p, 0.0)
                dl = p * dpds[h]
                pbs.append(p.astype(BF16))
                dlbs.append(dl.astype(BF16))
                dcol_sc[h] += jnp.sum(dl, axis=0, keepdims=True)
                drow_ref[h, rows, :] += jnp.sum(dl, axis=1, keepdims=True)
            for h, hs in enumerate(heads):
                dvt_sc[h] += lax.dot_general(dota_ref[hs, :], pbs[h], _NN, preferred_element_type=F32)
                dkt_sc[h] += lax.dot_general(qta_ref[hs, :], dlbs[h], _NN, preferred_element_type=F32)
                dq_ref[rows, hs] += lax.dot_general(dlbs[h], ka_ref[:, hs], _NN, preferred_element_type=F32)

        @pl.when(i > j)
        def _():
            block(False)

        @pl.when(i == j)
        def _():
            block(True)

        @pl.when(i == nq - 1)
        def _():
            dkt_ref[...] = jnp.concatenate([dkt_sc[0], dkt_sc[1]], axis=0)
            dvt_ref[...] = jnp.concatenate([dvt_sc[0], dvt_sc[1]], axis=0).astype(BF16)
            dcol_ref[...] = dcol_sc[...]

        if comm:
            @pl.when((hp == nhp - 1) & (j == nq - 1) & (i == nq - 1))
            def _():
                _comm_wait(_comm_copies(comm.plan, src_refs, dst_refs, *sems))

    qrow = pl.BlockSpec((tq, 2 * LANE), lambda h, j, i: (jnp.maximum(i, j), h))
    qcol = pl.BlockSpec((2 * LANE, tq), lambda h, j, i: (h, jnp.maximum(i, j)))
    krow = pl.BlockSpec((tk, 2 * LANE), lambda h, j, i: (j, h))
    kcol = pl.BlockSpec((2 * LANE, tk), lambda h, j, i: (h, j))
    tall = jax.ShapeDtypeStruct((qb.shape[1], T), F32)
    outs = pl.pallas_call(
        body, name=name, grid=(nhp, nq, nq),
        in_specs=[qrow, qcol, krow, kcol, kcol, qrow, qcol] + ([ANY] * len(comm.srcs) if comm else []),
        out_specs=[pl.BlockSpec((T, 2 * LANE), lambda h, j, i: (0, h)), kcol, kcol,
                   pl.BlockSpec((2, 1, tk), lambda h, j, i: (h, 0, j)),
                   pl.BlockSpec((2, T, 1), lambda h, j, i: (h, 0, 0))]
        + ([ANY] * len(comm.out_shapes) if comm else []),
        out_shape=[jax.ShapeDtypeStruct(qb.shape, F32), tall, jax.ShapeDtypeStruct(tall.shape, BF16),
                   jax.ShapeDtypeStruct((2 * nhp, 1, T), F32), jax.ShapeDtypeStruct((2 * nhp, T, 1), F32)]
        + (list(comm.out_shapes) if comm else []),
        scratch_shapes=[pltpu.VMEM((2, LANE, tk), F32), pltpu.VMEM((2, LANE, tk), F32),
                        pltpu.VMEM((2, 1, tk), F32)] + (comm.scratch() if comm else []),
        compiler_params=_cparams("arbitrary" if comm else "parallel", "arbitrary", "arbitrary"),
    )(qb, qta, ka, kta, vta, doa, dota, *(comm.srcs if comm else []))
    return (*outs[:5], outs[5:])


_GELU_C = math.sqrt(2.0 / math.pi)
_GELU_A = 0.044715


def _gelu(x):
    t = jnp.tanh(_GELU_C * (x + _GELU_A * (x * x * x)))
    return x * (0.5 * (1.0 + t)), t


def _gelu_grad(x, t):
    return 0.5 * (1.0 + t) + 0.5 * x * (1.0 - t * t) * (_GELU_C * (1.0 + 3.0 * _GELU_A * x * x))


def _layer_norm_stats(v):
    mu = jnp.mean(v, axis=-1, keepdims=True)
    vc = v - mu
    rstd = lax.rsqrt(jnp.mean(vc * vc, axis=-1, keepdims=True) + EPS)
    return vc * rstd, rstd


def _layer_norm_bwd(dyhat, yhat, rstd):
    return rstd * (dyhat - jnp.mean(dyhat, axis=-1, keepdims=True)
                   - yhat * jnp.mean(dyhat * yhat, axis=-1, keepdims=True))


def _sg_mask():
    t = lax.broadcasted_iota(jnp.int32, (SG_CHUNK, SG_CHUNK), 0) // SG_CAUSAL
    s = lax.broadcasted_iota(jnp.int32, (SG_CHUNK, SG_CHUNK), 1) // SG_CAUSAL
    return s <= t


def _sg_mix(ws_ref, bc_ref, vln_sc, vo_sc, tr, gd):
    mask = _sg_mask()
    for g in range(SG_GROUPS):
        wg = jnp.where(mask, ws_ref[g], 0.0).astype(BF16)
        cols = slice(g * gd, (g + 1) * gd)
        for n in range(tr // SG_CHUNK):
            rows = slice(n * SG_CHUNK, (n + 1) * SG_CHUNK)
            vo_sc[rows, cols] = lax.dot_general(wg, vln_sc[rows, cols], _NN,
                                                preferred_element_type=F32) + bc_ref[g]


def _sg_fwd(a_uv, ln_g, ln_b, ws, bcol, *, name):
    T, W = a_uv.shape[0], a_uv.shape[1] // 2
    gd = W // SG_GROUPS
    tr = _row_tile(T)

    def body(u_ref, v_ref, g_ref, b_ref, ws_ref, bc_ref, o_ref, vln_sc, vo_sc):
        u, _ = _gelu(u_ref[...])
        v, _ = _gelu(v_ref[...])
        vhat, _ = _layer_norm_stats(v)
        vln_sc[...] = (vhat * g_ref[...] + b_ref[...]).astype(BF16)
        _sg_mix(ws_ref, bc_ref, vln_sc, vo_sc, tr, gd)
        o_ref[...] = (u * vo_sc[...]).astype(BF16)

    row = pl.BlockSpec((1, W), lambda i: (0, 0))
    return pl.pallas_call(
        body, name=name, grid=(T // tr,),
        in_specs=[pl.BlockSpec((tr, W), lambda i: (i, 0)), pl.BlockSpec((tr, W), lambda i: (i, 1)), row, row,
                  pl.BlockSpec((SG_GROUPS, SG_CHUNK, SG_CHUNK), lambda i: (0, 0, 0)),
                  pl.BlockSpec((SG_GROUPS, SG_CHUNK, 1), lambda i: (0, 0, 0))],
        out_specs=pl.BlockSpec((tr, W), lambda i: (i, 0)),
        out_shape=jax.ShapeDtypeStruct((T, W), BF16),
        scratch_shapes=[pltpu.VMEM((tr, W), BF16), pltpu.VMEM((tr, W), F32)],
        compiler_params=_cparams("parallel"),
    )(a_uv, a_uv, ln_g, ln_b, ws, bcol)


def _sg_bwd(a_uv, dgate, ln_g, ln_b, ws, bcol, *, name):
    T, W = a_uv.shape[0], a_uv.shape[1] // 2
    gd = W // SG_GROUPS
    tr = _row_tile(T)

    def body(u_ref, v_ref, dg_ref, g_ref, b_ref, ws_ref, bc_ref,
             da_ref, dws_ref, dbs_ref, sums_ref, vln_sc, vo_sc, dvo_sc, dvln_sc):
        i = pl.program_id(0)

        @pl.when(i == 0)
        def _():
            dws_ref[...] = jnp.zeros(dws_ref.shape, F32)
            dbs_ref[...] = jnp.zeros(dbs_ref.shape, F32)
            sums_ref[...] = jnp.zeros(sums_ref.shape, F32)

        ua, va = u_ref[...], v_ref[...]
        u, tu = _gelu(ua)
        v, tv = _gelu(va)
        vhat, rstd = _layer_norm_stats(v)
        vln_sc[...] = (vhat * g_ref[...] + b_ref[...]).astype(BF16)
        _sg_mix(ws_ref, bc_ref, vln_sc, vo_sc, tr, gd)
        dgt = dg_ref[...]
        du = dgt * vo_sc[...]
        dvo_sc[...] = dgt * u
        mask = _sg_mask()
        for g in range(SG_GROUPS):
            wg = jnp.where(mask, ws_ref[g], 0.0).astype(BF16)
            cols = slice(g * gd, (g + 1) * gd)
            acc_w = jnp.zeros((SG_CHUNK, SG_CHUNK), F32)
            acc_b = jnp.zeros((SG_CHUNK, 1), F32)
            for n in range(tr // SG_CHUNK):
                rows = slice(n * SG_CHUNK, (n + 1) * SG_CHUNK)
                dvo = dvo_sc[rows, cols]
                dvob = dvo.astype(BF16)
                dvln_sc[rows, cols] = lax.dot_general(wg, dvob, _TN, preferred_element_type=F32)
                acc_w += lax.dot_general(dvob, vln_sc[rows, cols], _NT, preferred_element_type=F32)
                acc_b += jnp.sum(dvo, axis=1, keepdims=True)
            dws_ref[g] += jnp.where(mask, acc_w, 0.0)
            dbs_ref[g] += acc_b
        dvln = dvln_sc[...]
        sums_ref[...] += jnp.concatenate([jnp.sum(dvln * vhat, axis=0, keepdims=True),
                                          jnp.sum(dvln, axis=0, keepdims=True),
                                          jnp.zeros((6, W), F32)], axis=0)
        dv = _layer_norm_bwd(dvln * g_ref[...], vhat, rstd)
        da_ref[:, :W] = (du * _gelu_grad(ua, tu)).astype(BF16)
        da_ref[:, W:] = (dv * _gelu_grad(va, tv)).astype(BF16)

    row = pl.BlockSpec((1, W), lambda i: (0, 0))
    wspec = pl.BlockSpec((SG_GROUPS, SG_CHUNK, SG_CHUNK), lambda i: (0, 0, 0))
    bspec = pl.BlockSpec((SG_GROUPS, SG_CHUNK, 1), lambda i: (0, 0, 0))
    return pl.pallas_call(
        body, name=name, grid=(T // tr,),
        in_specs=[pl.BlockSpec((tr, W), lambda i: (i, 0)), pl.BlockSpec((tr, W), lambda i: (i, 1)),
                  pl.BlockSpec((tr, W), lambda i: (i, 0)), row, row, wspec, bspec],
        out_specs=[pl.BlockSpec((tr, 2 * W), lambda i: (i, 0)), wspec, bspec,
                   pl.BlockSpec((8, W), lambda i: (0, 0))],
        out_shape=[jax.ShapeDtypeStruct((T, 2 * W), BF16),
                   jax.ShapeDtypeStruct((SG_GROUPS, SG_CHUNK, SG_CHUNK), F32),
                   jax.ShapeDtypeStruct((SG_GROUPS, SG_CHUNK, 1), F32),
                   jax.ShapeDtypeStruct((8, W), F32)],
        scratch_shapes=[pltpu.VMEM((tr, W), BF16), pltpu.VMEM((tr, W), F32),
                        pltpu.VMEM((tr, W), F32), pltpu.VMEM((tr, W), F32)],
        compiler_params=_cparams("arbitrary"),
    )(a_uv, a_uv, dgate, ln_g, ln_b, ws, bcol)


SUBLANES = 8


def _shift_rows(xc_sc, xs_sc):
    rows = xs_sc.shape[1]
    for p in range(1, SUBLANES):
        xs_sc[p - 1] = xc_sc[pl.ds(p, rows), :]


def _rows_at(xc_sc, xs_sc, offset, tr):
    p = offset % SUBLANES
    base = offset - p
    return xc_sc[pl.ds(base, tr), :] if p == 0 else xs_sc[p - 1, pl.ds(base, tr), :]


def _shift_scratch(tr, C):
    return pltpu.VMEM((SUBLANES - 1, tr + CONV_HALO - SUBLANES, C), F32)


def _cv_glu_conv(a_ref, b_ref, ap_ref, bp_ref, w_ref, bd_ref, xc_sc, xs_sc, tr):
    i = pl.program_id(0)
    prev = ap_ref[...] * jax.nn.sigmoid(bp_ref[...])
    xc_sc[0:CONV_HALO, :] = jnp.where(i > 0, prev, 0.0)
    xc_sc[CONV_HALO:, :] = a_ref[...] * jax.nn.sigmoid(b_ref[...])
    _shift_rows(xc_sc, xs_sc)
    acc = jnp.broadcast_to(bd_ref[...], (tr, bd_ref.shape[1]))
    for k in range(CONV_WIDTH):
        acc = acc + w_ref[k:k + 1, :] * _rows_at(xc_sc, xs_sc, CONV_HALO - (CONV_WIDTH - 1) + k, tr)
    return acc


def _cv_specs(T, C, tr):
    hb = tr // CONV_HALO
    cur = lambda col: pl.BlockSpec((tr, C), lambda i: (i, col))
    prev = lambda col: pl.BlockSpec((CONV_HALO, C), lambda i: (jnp.maximum(i * hb - 1, 0), col))
    row = pl.BlockSpec((1, C), lambda i: (0, 0))
    wspec = pl.BlockSpec((CONV_HALO, C), lambda i: (0, 0))
    return cur, prev, row, wspec


def _cv_fwd(p, w_dw, b_dw, ln_g, ln_b, *, name):
    T, C = p.shape[0], p.shape[1] // 2
    tr = _row_tile(T)
    cur, prev, row, wspec = _cv_specs(T, C, tr)

    def body(a_ref, b_ref, ap_ref, bp_ref, w_ref, bd_ref, g_ref, be_ref, o_ref, xc_sc, xs_sc):
        y2 = _cv_glu_conv(a_ref, b_ref, ap_ref, bp_ref, w_ref, bd_ref, xc_sc, xs_sc, tr)
        yhat, _ = _layer_norm_stats(y2)
        yln = yhat * g_ref[...] + be_ref[...]
        o_ref[...] = (yln * jax.nn.sigmoid(yln)).astype(BF16)

    return pl.pallas_call(
        body, name=name, grid=(T // tr,),
        in_specs=[cur(0), cur(1), prev(0), prev(1), wspec, row, row, row],
        out_specs=pl.BlockSpec((tr, C), lambda i: (i, 0)),
        out_shape=jax.ShapeDtypeStruct((T, C), BF16),
        scratch_shapes=[pltpu.VMEM((tr + CONV_HALO, C), F32), _shift_scratch(tr, C)],
        compiler_params=_cparams("parallel"),
    )(p, p, p, p, w_dw, b_dw, ln_g, ln_b)


def _cv_bwd_ln(p, dy3, w_dw, b_dw, ln_g, ln_b, *, name):
    T, C = p.shape[0], p.shape[1] // 2
    tr = _row_tile(T)
    cur, prev, row, wspec = _cv_specs(T, C, tr)

    def body(a_ref, b_ref, ap_ref, bp_ref, dy_ref, w_ref, bd_ref, g_ref, be_ref,
             dy2_ref, dw_ref, sums_ref, xc_sc, xs_sc):
        i = pl.program_id(0)
        y2 = _cv_glu_conv(a_ref, b_ref, ap_ref, bp_ref, w_ref, bd_ref, xc_sc, xs_sc, tr)
        yhat, rstd = _layer_norm_stats(y2)
        yln = yhat * g_ref[...] + be_ref[...]
        s = jax.nn.sigmoid(yln)
        dyln = dy_ref[...] * (s + yln * s * (1.0 - s))
        dy2 = _layer_norm_bwd(dyln * g_ref[...], yhat, rstd)
        dy2_ref[...] = dy2
        sums = jnp.concatenate([jnp.sum(dy2, axis=0, keepdims=True),
                                jnp.sum(dyln * yhat, axis=0, keepdims=True),
                                jnp.sum(dyln, axis=0, keepdims=True),
                                jnp.zeros((5, C), F32)], axis=0)
        taps = [jnp.sum(dy2 * _rows_at(xc_sc, xs_sc, CONV_HALO - (CONV_WIDTH - 1) + k, tr), axis=0, keepdims=True)
                for k in range(CONV_WIDTH)]
        dw = jnp.concatenate(taps + [jnp.zeros((CONV_HALO - CONV_WIDTH, C), F32)], axis=0)

        @pl.when(i == 0)
        def _():
            sums_ref[...] = sums
            dw_ref[...] = dw

        @pl.when(i > 0)
        def _():
            sums_ref[...] += sums
            dw_ref[...] += dw

    blk = pl.BlockSpec((tr, C), lambda i: (i, 0))
    return pl.pallas_call(
        body, name=name, grid=(T // tr,),
        in_specs=[cur(0), cur(1), prev(0), prev(1), blk, wspec, row, row, row],
        out_specs=[blk, wspec, pl.BlockSpec((8, C), lambda i: (0, 0))],
        out_shape=[jax.ShapeDtypeStruct((T, C), F32), jax.ShapeDtypeStruct((CONV_HALO, C), F32),
                   jax.ShapeDtypeStruct((8, C), F32)],
        scratch_shapes=[pltpu.VMEM((tr + CONV_HALO, C), F32), _shift_scratch(tr, C)],
        compiler_params=_cparams("arbitrary"),
    )(p, p, p, p, dy3, w_dw, b_dw, ln_g, ln_b)


def _cv_bwd_in(p, dy2, w_dw, *, name):
    T, C = p.shape[0], p.shape[1] // 2
    tr = _row_tile(T)
    hb = tr // CONV_HALO
    nblk = T // tr
    last_halo = T // CONV_HALO - 1

    def body(a_ref, b_ref, dy_ref, dyn_ref, w_ref, dp_ref, sums_ref, xc_sc, xs_sc):
        i = pl.program_id(0)
        xc_sc[0:tr, :] = dy_ref[...]
        xc_sc[tr:, :] = jnp.where(i < nblk - 1, dyn_ref[...], 0.0)
        _shift_rows(xc_sc, xs_sc)
        dy1 = jnp.zeros((tr, C), F32)
        for k in range(CONV_WIDTH):
            dy1 = dy1 + w_ref[k:k + 1, :] * _rows_at(xc_sc, xs_sc, CONV_WIDTH - 1 - k, tr)
        a = a_ref[...]
        sb = jax.nn.sigmoid(b_ref[...])
        da = dy1 * sb
        db = dy1 * a * sb * (1.0 - sb)
        dp_ref[:, :C] = da.astype(BF16)
        dp_ref[:, C:] = db.astype(BF16)
        sums = jnp.concatenate([
            jnp.concatenate([jnp.sum(da, axis=0, keepdims=True), jnp.sum(db, axis=0, keepdims=True)], axis=1),
            jnp.zeros((7, 2 * C), F32)], axis=0)

        @pl.when(i == 0)
        def _():
            sums_ref[...] = sums

        @pl.when(i > 0)
        def _():
            sums_ref[...] += sums

    blk = lambda col: pl.BlockSpec((tr, C), lambda i: (i, col))
    return pl.pallas_call(
        body, name=name, grid=(nblk,),
        in_specs=[blk(0), blk(1), blk(0),
                  pl.BlockSpec((CONV_HALO, C), lambda i: (jnp.minimum((i + 1) * hb, last_halo), 0)),
                  pl.BlockSpec((CONV_HALO, C), lambda i: (0, 0))],
        out_specs=[pl.BlockSpec((tr, 2 * C), lambda i: (i, 0)), pl.BlockSpec((8, 2 * C), lambda i: (0, 0))],
        out_shape=[jax.ShapeDtypeStruct((T, 2 * C), BF16), jax.ShapeDtypeStruct((8, 2 * C), F32)],
        scratch_shapes=[pltpu.VMEM((tr + CONV_HALO, C), F32), _shift_scratch(tr, C)],
        compiler_params=_cparams("arbitrary"),
    )(p, p, dy2, dy2, w_dw)


def _col_tile(n, want=1024):
    best = LANE
    for t in range(LANE, min(n, want) + 1, LANE):
        if n % t == 0:
            best = t
    return best if n % LANE == 0 else n


def _mm(a, b, *, name, ta=False, tb=False, **kw):
    M = a.shape[1] if ta else a.shape[0]
    N = b.shape[0] if tb else b.shape[1]
    K = a.shape[0] if ta else a.shape[1]
    kw.setdefault('tm', _col_tile(M, 1024 if ta else 512))
    kw.setdefault('tn', _col_tile(N, 1024))
    kw.setdefault('tk', K if tb else _col_tile(K, 2048 if ta else 1024))
    return _matmul(a, b, name=name, ta=ta, tb=tb, **kw)


def _relu2_epilogue(acc):
    r = jnp.maximum(acc, 0.0)
    return (r * r,)


def _residual_epilogue(acc, x, g):
    return acc, x + g * acc


def _residual_bias_epilogue(acc, x, g, b):
    y = acc + b
    return y, x + g * y


def _dh_norm_bwd(d_act, w, x, dres, nw, sc, *, name, gate=None, below=None):
    D = x.shape[1]

    def epilogue(dh, xv, dresv, wv, scv, *more):
        gated, under = (more[:2], more[2:]) if gate else ((), more)
        r = lax.rsqrt(jnp.mean(xv * xv, axis=-1, keepdims=True) + EPS)
        n = xv * r
        scale = 1.0 + scv
        dn = dh * (wv * scale)
        dx = dresv + r * (dn - n * jnp.mean(dn * n, axis=-1, keepdims=True))
        rows = [jnp.sum(dh, axis=0, keepdims=True),
                jnp.sum(dh * (n * wv), axis=0, keepdims=True),
                jnp.sum(dh * n * scale, axis=0, keepdims=True)]
        outs = [dx]
        if gated:
            yv, gv = gated
            outs.append(dx * gv)
            rows += [jnp.sum(dx * yv, axis=0, keepdims=True), jnp.sum(dx * gv, axis=0, keepdims=True)]
        rows += [jnp.zeros((7 - len(rows), D), F32)]
        if under:
            zv, gzv = under
            outs.append(dx * gzv)
            rows.append(jnp.sum(dx * zv, axis=0, keepdims=True))
        else:
            rows.append(jnp.zeros((1, D), F32))
        return (*outs, jnp.concatenate(rows, axis=0))

    extras = [(x, 'tile'), (dres, 'tile'), (nw, 'row'), (sc, 'row')]
    for branch in (gate, below):
        if branch:
            extras += [(branch[0], 'tile'), (branch[1], 'row')]
    return _mm(d_act, w, tb=True, name=name, tn=D, extras=extras, epilogue=epilogue,
               out_dtypes=(F32,) + (BF16,) * (bool(gate) + bool(below)), row_sums=True)


def _relu2_bwd_epilogue(acc, r):
    return (acc * (2.0 * jnp.sqrt(r.astype(F32))),)


def _bias_epilogue(acc, b):
    return (acc + b,)


def _fox_forward(h1, P, j, D, carried=None):
    carried = carried or {}

    def ride(kernel):
        return carried[kernel][0] if kernel in carried else None

    def landed(kernel, outs):
        if kernel in carried:
            carried[kernel][1](outs)

    H = D // HEAD_DIM
    proj = _mm(h1, P['fox_w_in'][j], name='fox_proj', b_outer=True, tm=WIDE_ROWS)
    qg = jnp.tile(P['fox_q_norm'][j][None, :], (1, 2))
    kg = jnp.tile(P['fox_k_norm'][j][None, :], (1, 2))
    fpre_t = proj[:, 3 * D:3 * D + H].T
    bf = P['fox_b_f'][j][:, None]
    fcum = _fox_gate_fwd(fpre_t, bf, name='fox_gate_fwd')
    (qa, qta, ka, kta, va, vta), outs = _fox_prep_fwd(proj, qg, kg, fcum[:, :, None], d_model=D, name='fox_prep_fwd',
                                                     comm=ride('prep'))
    landed('prep', outs)
    o, qb, outs = _fox_attn_fwd(qa, kta, va, name='fox_attn_fwd', comm=ride('attn'))
    landed('attn', outs)
    saved = dict(proj=proj, qg=qg, kg=kg, fpre_t=fpre_t, bf=bf, o=o, qb=qb, qta=qta, ka=ka, kta=kta, vta=vta)
    return o, saved


def _fox_backward(dy, h1, S, P, j, D, comm=None):
    H = D // HEAD_DIM
    w_out, w_in = P['fox_w_out'][j], P['fox_w_in'][j]
    g = {}
    g['fox_w_out'] = _mm(S['o'], dy, ta=True, name='fox_dw_out')
    do = _mm(dy, w_out, tb=True, name='fox_do')
    doa, dota = _fox_do_prep(do, S['o'], name='fox_do_prep')
    dq, dkt, dvt, dcol, drow, comm_outs = _fox_attn_bwd(S['qb'], S['qta'], S['ka'], S['kta'], S['vta'], doa, dota,
                                                        name='fox_attn_bwd', comm=comm)
    dqp, dkp, dvp, gsum = _fox_prep_bwd(S['proj'], dq, dkt, dvt, S['qg'], S['kg'], d_model=D, name='fox_prep_bwd')
    dfpre_t, dbf = _fox_gate_bwd(dcol[:, 0, :], drow[:, :, 0], S['fpre_t'], S['bf'], name='fox_gate_bwd')
    dfpre = jnp.pad(dfpre_t.T.astype(BF16), ((0, 0), (0, LANE - H)))
    dproj = jnp.concatenate([dqp, dkp, dvp, dfpre], axis=1)
    g['fox_w_in'] = _mm(h1, dproj, ta=True, name='fox_dw_in')[:, :3 * D + H]
    g['fox_b_f'] = dbf[:, 0]
    g['fox_q_norm'] = gsum[0, :HEAD_DIM]
    g['fox_k_norm'] = gsum[1, :HEAD_DIM]
    return (dproj, w_in), g, comm_outs


def _sg_forward(h1, P, D):
    a_uv = _mm(h1, P['sg_w_in'], name='sg_in', b_outer=True, tm=WIDE_ROWS)
    bcol = P['sg_b_s'][:, :, None]
    gate = _sg_fwd(a_uv, P['sg_ln_g'], P['sg_ln_b'], P['sg_w_s'], bcol, name='sg_fwd')
    return gate, dict(a_uv=a_uv, bcol=bcol, gate=gate)


def _sg_backward(dy, h1, S, P, D):
    g = {}
    g['sg_w_out'] = _mm(S['gate'], dy, ta=True, name='sg_dw_out')
    dgate = _mm(dy, P['sg_w_out'], tb=True, name='sg_dgate')
    da, dws, dbs, sums = _sg_bwd(S['a_uv'], dgate, P['sg_ln_g'], P['sg_ln_b'], P['sg_w_s'], S['bcol'],
                                 name='sg_bwd')
    g['sg_w_s'], g['sg_b_s'] = dws, dbs[:, :, 0]
    g['sg_ln_g'], g['sg_ln_b'] = sums[0], sums[1]
    g['sg_w_in'] = _mm(h1, da, ta=True, name='sg_dw_in', out_chips=N_CHIPS)
    return (da, P['sg_w_in']), g


def _cv_forward(h1, P, D):
    p = _mm(h1, P['cv_w_pw1'], name='cv_pw1', extras=[(P['cv_b_pw1'], 'row')], epilogue=_bias_epilogue,
            b_outer=True, tm=WIDE_ROWS)
    w_dw = jnp.pad(P['cv_w_dw'], ((0, CONV_HALO - CONV_WIDTH), (0, 0)))
    y3 = _cv_fwd(p, w_dw, P['cv_b_dw'], P['cv_ln_g'], P['cv_ln_b'], name='cv_fwd')
    return y3, dict(p=p, w_dw=w_dw, y3=y3)


def _cv_backward(dy, h1, S, P, D):
    g = {}
    g['cv_w_pw2'] = _mm(S['y3'], dy, ta=True, name='cv_dw_pw2')
    dy3 = _mm(dy, P['cv_w_pw2'], tb=True, name='cv_dy3')
    dy2, dw, sums = _cv_bwd_ln(S['p'], dy3, S['w_dw'], P['cv_b_dw'], P['cv_ln_g'], P['cv_ln_b'], name='cv_bwd_ln')
    g['cv_w_dw'] = dw[:CONV_WIDTH]
    g['cv_b_dw'], g['cv_ln_g'], g['cv_ln_b'] = sums[0], sums[1], sums[2]
    dp, psum = _cv_bwd_in(S['p'], dy2, S['w_dw'], name='cv_bwd_in')
    g['cv_b_pw1'] = psum[0]
    g['cv_w_pw1'] = _mm(h1, dp, ta=True, name='cv_dw_pw1', out_chips=N_CHIPS)
    return (dp, P['cv_w_pw1']), g


class Hooks(NamedTuple):
    fwd: dict
    bwd_comm: Callable
    bwd_done: Callable


def _local_step(x, target, mod, P, hooks=None):
    T, D = x.shape
    L = mod.shape[0]
    saved = []
    for i in range(L):
        kind, j = i % N_MIXERS, i // N_MIXERS
        m = [mod[i:i + 1, k * D:(k + 1) * D] for k in range(6)]
        sh_m, sc_m, g_m, sh_f, sc_f, g_f = m
        w_mix, w_mlp = P['norm_mix'][i:i + 1], P['norm_mlp'][i:i + 1]
        h1 = _norm_mod_fwd(x, w_mix, sc_m, sh_m, name='norm_mix_fwd')
        if kind == 0:
            op, S = _fox_forward(h1, P, j, D, carried=hooks.fwd.get(i) if hooks else None)
            y, x1 = _mm(op, P['fox_w_out'][j], name='fox_out', extras=[(x, 'tile'), (g_m, 'row')],
                        epilogue=_residual_epilogue, out_dtypes=(F32, F32))
        elif kind == 1:
            op, S = _sg_forward(h1, P, D)
            y, x1 = _mm(op, P['sg_w_out'], name='sg_out', extras=[(x, 'tile'), (g_m, 'row')],
                        epilogue=_residual_epilogue, out_dtypes=(F32, F32))
        else:
            op, S = _cv_forward(h1, P, D)
            y, x1 = _mm(op, P['cv_w_pw2'], name='cv_out',
                        extras=[(x, 'tile'), (g_m, 'row'), (P['cv_b_pw2'], 'row')],
                        epilogue=_residual_bias_epilogue, out_dtypes=(F32, F32))
        h2 = _norm_mod_fwd(x1, w_mlp, sc_f, sh_f, name='norm_mlp_fwd')
        r = _mm(h2, P['w_mlp_in'][i], name='mlp_in', epilogue=_relu2_epilogue, out_dtypes=(BF16,),
                b_outer=True, tm=WIDE_ROWS)
        z, x2 = _mm(r, P['w_mlp_out'][i], name='mlp_out', extras=[(x1, 'tile'), (g_f, 'row')],
                    epilogue=_residual_epilogue, out_dtypes=(F32, F32), tk=P['w_mlp_out'][i].shape[0])
        saved.append(dict(x=x, h1=h1, S=S, y=y, x1=x1, h2=h2, r=r, z=z, m=m))
        x = x2

    loss_part, dx, dz, dgf = _loss_head(x, target, saved[-1]['z'], saved[-1]['m'][5], name='loss_head')
    dgf = dgf[0]

    grads = {k: [None] * L for k in ('norm_mix', 'norm_mlp')}
    mix_grads, mat = {}, {}
    dmod = [None] * L
    for i in reversed(range(L)):
        kind, j = i % N_MIXERS, i // N_MIXERS
        sv = saved[i]
        sh_m, sc_m, g_m, sh_f, sc_f, g_f = sv['m']
        w_mix, w_mlp = P['norm_mix'][i:i + 1], P['norm_mlp'][i:i + 1]
        mat['w_mlp_out', i] = _mm(sv['r'], dz, ta=True, name='mlp_dw_out')
        da = _mm(dz, P['w_mlp_out'][i], tb=True, name='mlp_da', extras=[(sv['r'], 'tile')],
                 epilogue=_relu2_bwd_epilogue, out_dtypes=(BF16,), b_outer=True, tm=WIDE_ROWS)
        mat['w_mlp_in', i] = _mm(sv['h2'], da, ta=True, name='mlp_dw_in', out_chips=N_CHIPS)
        dx1, dy, sums_f = _dh_norm_bwd(da, P['w_mlp_in'][i], sv['x1'], dx, w_mlp, sc_f, name='mlp_dh',
                                       gate=(sv['y'], g_m))
        if kind == 0:
            carried = hooks is not None and i == 0
            last, g, comm_outs = _fox_backward(dy, sv['h1'], sv['S'], P, j, D,
                                              comm=hooks.bwd_comm(mat) if carried else None)
            if carried:
                hooks.bwd_done(comm_outs)
        elif kind == 1:
            last, g = _sg_backward(dy, sv['h1'], sv['S'], P, D)
        else:
            last, g = _cv_backward(dy, sv['h1'], sv['S'], P, D)
            g['cv_b_pw2'] = sums_f[4]
        for k, val in g.items():
            if k in BIG:
                mat[k, j] = val
            else:
                mix_grads.setdefault(k, {})[j] = val
        if i > 0:
            dx, dz, sums_m = _dh_norm_bwd(*last, sv['x'], dx1, w_mix, sc_m, name='mix_dh',
                                          below=(saved[i - 1]['z'], saved[i - 1]['m'][5]))
        else:
            dx, sums_m = _dh_norm_bwd(*last, sv['x'], dx1, w_mix, sc_m, name='mix_dh')
        grads['norm_mlp'][i], grads['norm_mix'][i] = sums_f[2], sums_m[2]
        dmod[i] = jnp.concatenate([sums_m[0], sums_m[1], sums_f[3], sums_f[0], sums_f[1], dgf])
        dgf = sums_m[7]

    out = {k: jnp.stack(v) for k, v in grads.items()}
    for k, per_j in mix_grads.items():
        out[k] = jnp.stack([per_j[j] for j in sorted(per_j)])
    return loss_part, dx, jnp.stack(dmod), out, mat


def _all_gather8(blocks, *, name):
    n = len(blocks)

    def body(*refs):
        x_refs, out_refs = refs[:n], refs[n:2 * n]
        send_sems, recv_sems, local_sems = refs[2 * n:]
        x, y, c = _position()
        me, sibling = (x, y, c), (x, y, 1 - c)
        chips = [(1 - x, y), (x, 1 - y), (1 - x, 1 - y)]

        def slot(a, px, py, pc):
            return out_refs[a].at[4 * px + 2 * py + pc]

        def copy(a, k, blk, to, src=None):
            return pltpu.make_async_remote_copy(
                src_ref=slot(a, *blk) if src is None else src, dst_ref=slot(a, *blk),
                send_sem=send_sems.at[7 * a + k], recv_sem=recv_sems.at[7 * a + k],
                device_id=to, device_id_type=MESH)

        mine = [pltpu.make_async_copy(x_refs[a], slot(a, *me), local_sems.at[a]) for a in range(n)]
        for cp in mine:
            cp.start()
        first = []
        for j, chip in enumerate(chips):
            first += [copy(a, 1 + j, me, (*chip, c), src=x_refs[a]) for a in range(n)]
        first += [copy(a, 0, me, sibling, src=x_refs[a]) for a in range(n)]
        for cp in first:
            cp.start()
        passed = []
        for j, chip in enumerate(chips):
            for a in range(n):
                copy(a, 1 + j, (*chip, c), me).wait_recv()
                passed.append(copy(a, 4 + j, (*chip, c), sibling))
                passed[-1].start()
        for a in range(n):
            copy(a, 0, sibling, me).wait_recv()
        for j, chip in enumerate(chips):
            for a in range(n):
                copy(a, 4 + j, (*chip, 1 - c), me).wait_recv()
        for cp in first + passed:
            cp.wait_send()
        for cp in mine:
            cp.wait()

    return pl.pallas_call(
        body, name=name, in_specs=[ANY] * n, out_specs=[ANY] * n,
        out_shape=[jax.ShapeDtypeStruct((8,) + b.shape, b.dtype) for b in blocks],
        scratch_shapes=[pltpu.SemaphoreType.DMA((7 * n,)), pltpu.SemaphoreType.DMA((7 * n,)),
                        pltpu.SemaphoreType.DMA((n,))],
    )(*blocks)


def _exchange(comm, *, name, aliases=None):
    ns, no = len(comm.srcs), len(comm.out_shapes)

    def body(*refs):
        copies = _comm_copies(comm.plan, refs[:ns], refs[ns:ns + no], *refs[ns + no:])
        _comm_start(copies)
        _comm_wait(copies)

    return pl.pallas_call(
        body, name=name, in_specs=[ANY] * ns, out_specs=[ANY] * no, out_shape=list(comm.out_shapes),
        scratch_shapes=comm.scratch(), input_output_aliases=aliases or {},
    )(*comm.srcs)


def _gather_comm(halves):
    n = len(halves)

    def plan(src, out, x, y, c):
        mine = 4 * x + 2 * y + c
        remote = [(src[a], out[a].at[mine], (x, y, 1 - c), out[a].at[4 * x + 2 * y + 1 - c]) for a in range(n)]
        for fx, fy in CHIP_FLIPS:
            px, py = _flip(x, fx), _flip(y, fy)
            remote += [(src[a], out[a].at[mine], (px, py, c), out[a].at[4 * px + 2 * py + c]) for a in range(n)]
        return remote, [(src[a], out[a].at[mine]) for a in range(n)]

    return Comm(list(halves), [jax.ShapeDtypeStruct((8,) + h.shape, h.dtype) for h in halves], plan, 4 * n, n)


def _gather_forward(bufs, *, name):
    n = len(bufs)

    def plan(src, out, x, y, c):
        remote = []
        for fx, fy in CHIP_FLIPS:
            px, py = _flip(x, fx), _flip(y, fy)
            remote += [(src[a].at[4 * px + 2 * py + c], out[a].at[4 * px + 2 * py + c], (x, y, 1 - c),
                        out[a].at[4 * px + 2 * py + 1 - c]) for a in range(n)]
        return remote, []

    comm = Comm(list(bufs), [jax.ShapeDtypeStruct(b.shape, b.dtype) for b in bufs], plan, 3 * n, 0)
    return _exchange(comm, name=name, aliases={a: a for a in range(n)})


CHIP_FLIPS = ((1, 0), (0, 1), (1, 1))


def _flip(v, f):
    return 1 - v if f else v


def _sum_rows_tile(R, C, budget=3 << 20):
    best = None
    for t in range(8, R + 1, 8):
        if R % t == 0 and t * C * 4 <= budget:
            best = t
    return best if best is not None else R


def _rs_begin(gps, *, wire_dtype):
    n = len(gps)
    c_arr = jnp.reshape(_position()[2], (1,)).astype(jnp.int32)

    def plan(src, out, x, y, c):
        return [(src[a].at[b, 1 - c], out[a].at[b], (x, y, 1 - c), out[a].at[b])
                for a in range(n) for b in range(4)], []

    got1 = _exchange(Comm(list(gps), [jax.ShapeDtypeStruct((4,) + g.shape[2:], F32) for g in gps], plan, 4 * n, 0),
                     name='rs_sibling')

    def sum_chip(c_ref, mine_ref, got_ref, out_ref):
        out_ref[...] = (mine_ref[...] + got_ref[...]).astype(out_ref.dtype)

    parts = []
    for gp, g1 in zip(gps, got1):
        _, _, R, C = gp.shape
        tr = _sum_rows_tile(R, C)
        parts.append(pl.pallas_call(
            sum_chip, name='rs_sum_chip',
            grid_spec=pltpu.PrefetchScalarGridSpec(
                num_scalar_prefetch=1, grid=(4, R // tr),
                in_specs=[pl.BlockSpec((None, None, tr, C), lambda b, r, cr: (b, cr[0], r, 0)),
                          pl.BlockSpec((None, tr, C), lambda b, r, cr: (b, r, 0))],
                out_specs=pl.BlockSpec((None, tr, C), lambda b, r, cr: (b, r, 0))),
            out_shape=jax.ShapeDtypeStruct((4, R, C), wire_dtype),
            compiler_params=_cparams("parallel", "parallel"),
        )(c_arr, gp, g1))
    return got1, parts


def _rs_chips_comm(parts):
    n = len(parts)

    def plan(src, out, x, y, c):
        remote = []
        for k, (fx, fy) in enumerate(CHIP_FLIPS):
            px, py = _flip(x, fx), _flip(y, fy)
            remote += [(src[a].at[2 * px + py], out[a].at[k], (px, py, c), out[a].at[k]) for a in range(n)]
        return remote, []

    return Comm(list(parts), [jax.ShapeDtypeStruct((3,) + p.shape[1:], p.dtype) for p in parts], plan, 3 * n, 0)


def _rs_finish(gps, got1, got2):
    n = len(gps)
    x, y, c = _position()
    bc_arr = jnp.stack([2 * x + y, c]).astype(jnp.int32)

    def sum_final(bc_ref, mine_ref, got1_ref, got2_ref, out_ref):
        acc = mine_ref[...] + got1_ref[...]
        for k in range(3):
            acc = acc + got2_ref[k].astype(F32)
        out_ref[...] = acc

    halves = []
    for gp, g1, g2 in zip(gps, got1, got2):
        _, _, R, C = gp.shape
        tr = _sum_rows_tile(R, C, budget=2 << 20)
        halves.append(pl.pallas_call(
            sum_final, name='rs_sum_final',
            grid_spec=pltpu.PrefetchScalarGridSpec(
                num_scalar_prefetch=1, grid=(R // tr,),
                in_specs=[pl.BlockSpec((None, None, tr, C), lambda r, bc: (bc[0], bc[1], r, 0)),
                          pl.BlockSpec((None, tr, C), lambda r, bc: (bc[0], r, 0)),
                          pl.BlockSpec((3, tr, C), lambda r, bc: (0, r, 0))],
                out_specs=pl.BlockSpec((None, tr, C), lambda r, bc: (bc[1], r, 0))),
            out_shape=jax.ShapeDtypeStruct((2, R, C), F32),
            compiler_params=_cparams("parallel"),
        )(bc_arr, gp, g1, g2))

    def plan(src, out, x, y, c):
        return [(src[a].at[c], out[a].at[c], (x, y, 1 - c), out[a].at[1 - c]) for a in range(n)], []

    comm = Comm(halves, [jax.ShapeDtypeStruct(h.shape, F32) for h in halves], plan, n, 0)
    return _exchange(comm, name='rs_swap', aliases={a: a for a in range(n)})


def _sum8(gathered, *, name):
    _, R, C = gathered.shape

    def body(g_ref, o_ref):
        acc = g_ref[0]
        for k in range(1, 8):
            acc = acc + g_ref[k]
        o_ref[...] = acc

    return pl.pallas_call(body, name=name, out_shape=jax.ShapeDtypeStruct((R, C), F32))(gathered)


def _adamw(w, g, m, v, *, name):
    shape = w.shape
    cols = shape[-1]
    rows = w.size // cols
    tr = _sum_rows_tile(rows, cols, budget=1 << 20)

    def body(w_ref, g_ref, m_ref, v_ref, d_ref, mo_ref, vo_ref):
        gv = g_ref[...]
        mn = ADAM_B1 * m_ref[...] + (1.0 - ADAM_B1) * gv
        vn = ADAM_B2 * v_ref[...] + (1.0 - ADAM_B2) * (gv * gv)
        m_hat = mn / (1.0 - ADAM_B1 ** ADAM_STEP)
        v_hat = vn / (1.0 - ADAM_B2 ** ADAM_STEP)
        d_ref[...] = -ADAM_LR * (m_hat / (jnp.sqrt(v_hat) + ADAM_EPS) + ADAM_WD * w_ref[...])
        mo_ref[...] = mn
        vo_ref[...] = vn

    blk = pl.BlockSpec((tr, cols), lambda i: (i, 0))
    outs = pl.pallas_call(
        body, name=name, grid=(rows // tr,), in_specs=[blk] * 4, out_specs=[blk] * 3,
        out_shape=[jax.ShapeDtypeStruct((rows, cols), F32)] * 3,
        compiler_params=_cparams("parallel"),
    )(*[a.reshape(rows, cols) for a in (w, g, m, v)])
    return tuple(o.reshape(shape) for o in outs)


WEIGHTS = ['norm_mix', 'norm_mlp', 'w_ada', 'b_ada', 'w_mlp_in', 'w_mlp_out', 'fox_w_in', 'fox_b_f',
           'fox_q_norm', 'fox_k_norm', 'fox_w_out', 'sg_w_in', 'sg_ln_g', 'sg_ln_b', 'sg_w_s', 'sg_b_s',
           'sg_w_out', 'cv_w_pw1', 'cv_b_pw1', 'cv_w_dw', 'cv_b_dw', 'cv_ln_g', 'cv_ln_b', 'cv_w_pw2',
           'cv_b_pw2']
BIG = {'w_mlp_in': 2, 'w_mlp_out': 1, 'fox_w_in': 2, 'fox_w_out': 1, 'sg_w_in': 2, 'sg_w_out': 1,
       'cv_w_pw1': 2, 'cv_w_pw2': 1}
SMALL_SHARDED = ['cv_b_pw1', 'cv_w_dw', 'cv_b_dw', 'cv_ln_g', 'cv_ln_b', 'cv_b_pw2']
SMALL_GRADS = ['norm_mix', 'norm_mlp', 'fox_b_f', 'fox_q_norm', 'fox_k_norm', 'sg_ln_g', 'sg_ln_b', 'sg_w_s',
               'sg_b_s'] + SMALL_SHARDED
GRAD_WIRE_DTYPE = BF16


def _pack_rows(parts, cols):
    flat = jnp.concatenate([p.reshape(-1) for p in parts])
    rows = -(-flat.size // (8 * cols)) * 8
    return jnp.pad(flat, (0, rows * cols - flat.size)).reshape(rows, cols)


def _unpack(flat, shapes):
    out, off = [], 0
    for s in shapes:
        n = math.prod(s)
        out.append(flat[..., off:off + n].reshape(flat.shape[:-1] + tuple(s)))
        off += n
    return out


def _merge_chips(a, axis):
    a = jnp.moveaxis(a, 0, axis)
    return a.reshape(a.shape[:axis] + (a.shape[axis] * a.shape[axis + 1],) + a.shape[axis + 2:])


def _split_chips(a, axis):
    a = a.reshape(a.shape[:axis] + (4, a.shape[axis] // 4) + a.shape[axis + 1:])
    return jnp.moveaxis(a, axis, 0)


def _step(a):
    x, y, c = _position()
    me = 4 * x + 2 * y + c
    chip = 2 * x + y
    T, D = a['x'].shape[1], a['x'].shape[2]
    L = a['norm_mix'].shape[0]

    small_shapes = [(D,)] + [a[n].shape for n in SMALL_SHARDED]
    small = _all_gather8([_pack_rows([a['c']] + [a[n] for n in SMALL_SHARDED], LANE)], name='ag_small')[0]
    small = small.reshape(8, -1)
    c_all = _unpack(small, small_shapes[:1])[0]
    sharded = _unpack(small[0::2, D:], small_shapes[1:])
    P = {n: _merge_chips(v, v.ndim - 2) for n, v in zip(SMALL_SHARDED, sharded)}

    c_act = _silu_rows(c_all, name='c_act')
    mod_cols = jnp.stack([
        _mm(c_act, a['w_ada'][i], name='ada_mod', tm=8, tn=_col_tile(a['w_ada'].shape[2], 768),
            extras=[(lax.dynamic_slice_in_dim(a['b_ada'][i:i + 1], chip * a['w_ada'].shape[2],
                                              a['w_ada'].shape[2], axis=1), 'row')],
            epilogue=_bias_epilogue)
        for i in range(L)])
    mod_all = _all_gather8([mod_cols.reshape(L * 8, -1)], name='ag_mod')[0].reshape(8, L, 8, -1)
    mod = lax.dynamic_index_in_dim(mod_all[0::2], me, axis=2, keepdims=False)
    mod = jnp.moveaxis(mod, 0, 1).reshape(L, 6 * D)

    units = _matrix_units(L)
    first, with_prep, with_attn, with_last = units[:1], units[1:4], units[4:-3], units[-3:]
    last, earlier = units[:2], units[2:]
    n_heads = D // HEAD_DIM

    def half_block(unit):
        blk = a[unit[0]][unit[1]]
        return lax.dynamic_index_in_dim(blk.astype(BF16).reshape(2, blk.shape[0] // 2, blk.shape[1]), c, axis=0,
                                        keepdims=False)

    def install(group, gathered):
        for (name, idx), gth in zip(group, gathered):
            blocks = gth.reshape((4,) + a[name].shape[1:])
            if name == 'fox_w_in':
                pad = jnp.zeros((blocks.shape[1], LANE - n_heads), BF16)
                full = jnp.concatenate([blocks[0], blocks[1], blocks[2], blocks[3], pad], axis=-1)
            else:
                full = _merge_chips(blocks, BIG[name] - 1)
            if name in ('w_mlp_in', 'w_mlp_out', 'fox_w_in', 'fox_w_out'):
                P.setdefault(name, {})[idx] = full
            else:
                P[name] = full

    install(first, _all_gather8([half_block(u) for u in first], name='ag_weights_first'))
    for n in ('sg_w_s', 'sg_b_s', 'cv_w_dw'):
        P[n] = (P[n] if n in P else a[n])[0]
    for n in ('norm_mix', 'norm_mlp', 'fox_b_f', 'fox_q_norm', 'fox_k_norm', 'sg_ln_g', 'sg_ln_b'):
        P[n] = a[n]

    def split_grad(unit, grad):
        name = unit[0]
        if name == 'fox_w_in':
            grad = grad[:, :a[name].shape[2] * N_CHIPS]
        blk = grad if grad.ndim == 3 else _split_chips(grad, BIG[name] - 1)
        return blk.reshape(N_CHIPS, 2, blk.shape[1] // 2, blk.shape[2])

    state = {}

    def riding(group):
        return (_gather_comm([half_block(u) for u in group]),
                lambda outs: install(group, _gather_forward(outs, name='ag_weights_forward')))

    def bwd_comm(mat):
        state['gps'] = [split_grad(u, mat[u]) for u in earlier]
        state['got1'], parts = _rs_begin(state['gps'], wire_dtype=GRAD_WIRE_DTYPE)
        return _rs_chips_comm(parts)

    def bwd_done(outs):
        state['got2'] = outs

    last_fox = N_MIXERS * ((L - 1) // N_MIXERS)
    hooks = Hooks({0: {'prep': riding(with_prep), 'attn': riding(with_attn)}, last_fox: {'attn': riding(with_last)}},
                  bwd_comm, bwd_done)
    loss_part, grad_x, dmod, g, mat = _local_step(a['x'][0], a['loss_target'][0], mod, P, hooks)

    small_g = [dmod, loss_part[0:1, 0:1]] + [g[n] for n in SMALL_GRADS]
    small_g_shapes = [s.shape for s in small_g]
    all_small = _all_gather8([_pack_rows(small_g, LANE)], name='ag_small_grads')[0]
    summed = _sum8(all_small, name='sum_small_grads').reshape(-1)
    sums = _unpack(summed, small_g_shapes)
    loss = sums[1][0, 0]
    grads = dict(zip(SMALL_GRADS, sums[2:]))
    grads['b_ada'] = sums[0]
    for n in SMALL_SHARDED:
        blk = a[n].shape[-1]
        grads[n] = lax.dynamic_slice_in_dim(grads[n], chip * blk, blk, axis=grads[n].ndim - 1)
    dmod_all = all_small.reshape(8, -1)[:, :dmod.size].reshape(8, L, 6 * D)
    cols = a['w_ada'].shape[2]
    dmod_cols = lax.dynamic_slice_in_dim(dmod_all, chip * cols, cols, axis=2)
    pad8 = lambda t: jnp.pad(t, ((0, LANE - 8), (0, 0)))
    c_act_pad = pad8(c_act)
    grads['w_ada'] = jnp.stack([
        _mm(c_act_pad, pad8(dmod_cols[:, i]), ta=True, name='ada_dw', tn=_col_tile(cols, 768))
        for i in range(L)])

    shards = dict(zip(earlier, _rs_finish(state['gps'], state['got1'], state['got2'])))
    gps = [split_grad(u, mat[u]) for u in last]
    got1, parts = _rs_begin(gps, wire_dtype=GRAD_WIRE_DTYPE)
    got2 = _exchange(_rs_chips_comm(parts), name='rs_chips')
    shards.update(zip(last, _rs_finish(gps, got1, got2)))
    for n in BIG:
        grads[n] = jnp.stack([shards[n, idx].reshape(a[n].shape[1:]) for idx in range(a[n].shape[0])])

    deltas, new_m, new_v = {}, {}, {}
    for n in WEIGHTS:
        deltas[n], new_m[n], new_v[n] = _adamw(a[n], grads[n], a['m_' + n], a['v_' + n], name='adamw')
    return (loss, grad_x[None], *[grads[n] for n in WEIGHTS], *[deltas[n] for n in WEIGHTS],
            *[new_m[n] for n in WEIGHTS], *[new_v[n] for n in WEIGHTS])


def _matrix_units(n_layers):
    mixers = (('fox_w_in', 'fox_w_out'), ('sg_w_in', 'sg_w_out'), ('cv_w_pw1', 'cv_w_pw2'))
    units = []
    for i in range(n_layers):
        units += [(n, i // N_MIXERS) for n in mixers[i % N_MIXERS]] + [('w_mlp_in', i), ('w_mlp_out', i)]
    return units


def _silu_rows(x, *, name):
    def body(x_ref, o_ref):
        xv = x_ref[...]
        o_ref[...] = (xv * jax.nn.sigmoid(xv)).astype(BF16)

    return pl.pallas_call(body, name=name, out_shape=jax.ShapeDtypeStruct(x.shape, BF16))(x)


def kernel(x, c, norm_mix, norm_mlp, w_ada, b_ada, w_mlp_in, w_mlp_out, fox_w_in, fox_b_f, fox_q_norm, fox_k_norm, fox_w_out, sg_w_in, sg_ln_g, sg_ln_b, sg_w_s, sg_b_s, sg_w_out, cv_w_pw1, cv_b_pw1, cv_w_dw, cv_b_dw, cv_ln_g, cv_ln_b, cv_w_pw2, cv_b_pw2, loss_target, m_norm_mix, m_norm_mlp, m_w_ada, m_b_ada, m_w_mlp_in, m_w_mlp_out, m_fox_w_in, m_fox_b_f, m_fox_q_norm, m_fox_k_norm, m_fox_w_out, m_sg_w_in, m_sg_ln_g, m_sg_ln_b, m_sg_w_s, m_sg_b_s, m_sg_w_out, m_cv_w_pw1, m_cv_b_pw1, m_cv_w_dw, m_cv_b_dw, m_cv_ln_g, m_cv_ln_b, m_cv_w_pw2, m_cv_b_pw2, v_norm_mix, v_norm_mlp, v_w_ada, v_b_ada, v_w_mlp_in, v_w_mlp_out, v_fox_w_in, v_fox_b_f, v_fox_q_norm, v_fox_k_norm, v_fox_w_out, v_sg_w_in, v_sg_ln_g, v_sg_ln_b, v_sg_w_s, v_sg_b_s, v_sg_w_out, v_cv_w_pw1, v_cv_b_pw1, v_cv_w_dw, v_cv_b_dw, v_cv_ln_g, v_cv_ln_b, v_cv_w_pw2, v_cv_b_pw2):
    return _step(dict(locals()))
```

```python
import math
from typing import Callable, NamedTuple

import jax
import jax.numpy as jnp
from jax import lax
from jax.experimental import pallas as pl
from jax.experimental.pallas import tpu as pltpu

F32 = jnp.float32
BF16 = jnp.bfloat16

EPS = 1e-6
HEAD_DIM = 64
LANE = 128
CONV_WIDTH = 31
CONV_HALO = 32
SG_CHUNK = 128
SG_CAUSAL = 64
SG_GROUPS = 8
N_MIXERS = 3
N_CHIPS = 4
VMEM_LIMIT = 56 * 1024 * 1024
NEG = -1e30

ADAM_LR = 0.001
ADAM_B1 = 0.9
ADAM_B2 = 0.999
ADAM_EPS = 1e-08
ADAM_WD = 0.01
ADAM_STEP = 10

MESH = pl.DeviceIdType.MESH
ANY = pl.BlockSpec(memory_space=pl.ANY)


def _cparams(*sem):
    return pltpu.CompilerParams(dimension_semantics=sem, vmem_limit_bytes=VMEM_LIMIT)


WIDE_ROWS = 1024


def _row_tile(t, want=512):
    return min(t, want)


def _matmul(a, b, *, name, ta=False, tb=False, tm=512, tn=1024, tk=1024,
            extras=(), epilogue=None, out_dtypes=(F32,), b_outer=False, out_chips=None, row_sums=False):
    M, K = (a.shape[1], a.shape[0]) if ta else a.shape
    N = b.shape[0] if tb else b.shape[1]
    assert (b.shape[1] if tb else b.shape[0]) == K
    n_own = N // out_chips if out_chips else N
    tm, tn, tk = min(tm, M), min(tn, n_own), min(tk, K)
    assert M % tm == 0 and n_own % tn == 0 and K % tk == 0, (name, M, N, K, tm, tn, tk)
    nk = K // tk

    def spec(shape, pick):
        if b_outer:
            return pl.BlockSpec(shape, lambda j, i, k: pick(i, j, k))
        return pl.BlockSpec(shape, pick)

    a_spec = spec((tk, tm), lambda i, j, k: (k, i)) if ta else spec((tm, tk), lambda i, j, k: (i, k))
    b_spec = spec((tn, tk), lambda i, j, k: (j, k)) if tb else spec((tk, tn), lambda i, j, k: (k, j))
    ex_specs = [spec((tm, tn), lambda i, j, k: (i, j)) if kind == 'tile' else spec((1, tn), lambda i, j, k: (0, j))
                for _, kind in extras]
    dims = (((0,) if ta else (1,), (1,) if tb else (0,)), ((), ()))
    n_ex, n_out = len(extras), len(out_dtypes) + bool(row_sums)
    assert not row_sums or N == tn

    def body(*refs):
        a_ref, b_ref = refs[0], refs[1]
        ex = refs[2:2 + n_ex]
        outs = refs[2 + n_ex:2 + n_ex + n_out]

        def finish(acc):
            vals = epilogue(acc, *[r[...] for r in ex]) if epilogue else (acc,)
            for o, v in zip(outs[:len(out_dtypes)], vals):
                o[...] = v.astype(o.dtype)
            if row_sums:
                row_tile = pl.program_id(1 if b_outer else 0)

                @pl.when(row_tile == 0)
                def _():
                    outs[-1][...] = vals[-1]

                @pl.when(row_tile > 0)
                def _():
                    outs[-1][...] += vals[-1]

        part = lax.dot_general(a_ref[...].astype(BF16), b_ref[...].astype(BF16), dims,
                               preferred_element_type=F32)
        if nk == 1:
            finish(part)
        else:
            acc_ref = refs[-1]
            k = pl.program_id(2)

            @pl.when(k == 0)
            def _():
                acc_ref[...] = part

            @pl.when(k > 0)
            def _():
                acc_ref[...] += part

            @pl.when(k == nk - 1)
            def _():
                finish(acc_ref[...])

    outs = pl.pallas_call(
        body, name=name,
        grid=(N // tn, M // tm, nk) if b_outer else (M // tm, N // tn, nk),
        in_specs=[a_spec, b_spec] + ex_specs,
        out_specs=[spec((None, tm, tn), lambda i, j, k: (j // (n_own // tn), i, j % (n_own // tn)))
                   if out_chips else spec((tm, tn), lambda i, j, k: (i, j)) for _ in out_dtypes]
        + ([spec((8, tn), lambda i, j, k: (0, j))] if row_sums else []),
        out_shape=[jax.ShapeDtypeStruct((out_chips, M, n_own) if out_chips else (M, N), dt) for dt in out_dtypes]
        + ([jax.ShapeDtypeStruct((8, N), F32)] if row_sums else []),
        scratch_shapes=[pltpu.VMEM((tm, tn), F32)] if nk > 1 else [],
        compiler_params=(_cparams("arbitrary", "arbitrary", "arbitrary") if row_sums
                         else _cparams("parallel", "parallel", "arbitrary")),
    )(a, b, *[arr for arr, _ in extras])
    return outs if n_out > 1 else outs[0]


def _norm_mod_fwd(x, w, sc, sh, *, name):
    T, D = x.shape
    tr = _row_tile(T, WIDE_ROWS)

    def body(x_ref, w_ref, sc_ref, sh_ref, h_ref):
        xv = x_ref[...]
        r = lax.rsqrt(jnp.mean(xv * xv, axis=-1, keepdims=True) + EPS)
        h_ref[...] = ((xv * r) * w_ref[...] * (1.0 + sc_ref[...]) + sh_ref[...]).astype(BF16)

    row = pl.BlockSpec((1, D), lambda i: (0, 0))
    return pl.pallas_call(
        body, name=name, grid=(T // tr,),
        in_specs=[pl.BlockSpec((tr, D), lambda i: (i, 0)), row, row, row],
        out_specs=pl.BlockSpec((tr, D), lambda i: (i, 0)),
        out_shape=jax.ShapeDtypeStruct((T, D), BF16),
        compiler_params=_cparams("parallel"),
    )(x, w, sc, sh)


def _loss_head(y, target, z, g, *, name):
    T, D = y.shape
    tr = _row_tile(T)

    def body(y_ref, t_ref, z_ref, g_ref, loss_ref, dy_ref, dz_ref, dg_ref):
        i = pl.program_id(0)
        e = y_ref[...] - t_ref[...]
        dy = e * (1.0 / D)
        dy_ref[...] = dy
        dz_ref[...] = (dy * g_ref[...]).astype(BF16)
        part = jnp.full((8, LANE), 0.5 / D * jnp.sum(e * e), F32)
        dg = jnp.concatenate([jnp.sum(dy * z_ref[...], axis=0, keepdims=True), jnp.zeros((7, D), F32)], axis=0)

        @pl.when(i == 0)
        def _():
            loss_ref[...] = part
            dg_ref[...] = dg

        @pl.when(i > 0)
        def _():
            loss_ref[...] += part
            dg_ref[...] += dg

    blk = pl.BlockSpec((tr, D), lambda i: (i, 0))
    return pl.pallas_call(
        body, name=name, grid=(T // tr,), in_specs=[blk, blk, blk, pl.BlockSpec((1, D), lambda i: (0, 0))],
        out_specs=[pl.BlockSpec((8, LANE), lambda i: (0, 0)), blk, blk, pl.BlockSpec((8, D), lambda i: (0, 0))],
        out_shape=[jax.ShapeDtypeStruct((8, LANE), F32), jax.ShapeDtypeStruct((T, D), F32),
                   jax.ShapeDtypeStruct((T, D), BF16), jax.ShapeDtypeStruct((8, D), F32)],
        compiler_params=_cparams("arbitrary"),
    )(y, target, z, g)


def _position():
    return lax.axis_index("x"), lax.axis_index("y"), lax.axis_index("c")


class Comm(NamedTuple):
    srcs: list
    out_shapes: list
    plan: Callable
    n_remote: int
    n_local: int

    def scratch(self):
        return [pltpu.SemaphoreType.DMA((self.n_remote,)), pltpu.SemaphoreType.DMA((self.n_remote,)),
                pltpu.SemaphoreType.DMA((max(self.n_local, 1),))]


def _comm_copies(plan, src_refs, out_refs, send_sems, recv_sems, local_sems):
    x, y, c = _position()
    remote, local = plan(src_refs, out_refs, x, y, c)

    def copy(k, s, d, peer):
        return pltpu.make_async_remote_copy(src_ref=s, dst_ref=d, send_sem=send_sems.at[k],
                                            recv_sem=recv_sems.at[k], device_id=peer, device_id_type=MESH)

    sends = [copy(k, s, d, peer) for k, (s, d, peer, _) in enumerate(remote)]
    recvs = [copy(k, s, landing, peer) for k, (s, _, peer, landing) in enumerate(remote)]
    local_copies = [pltpu.make_async_copy(s, d, local_sems.at[i]) for i, (s, d) in enumerate(local)]
    return sends, recvs, local_copies


def _comm_start(copies):
    sends, _, local_copies = copies
    for cp in local_copies + sends:
        cp.start()


def _comm_wait(copies):
    sends, recvs, local_copies = copies
    for cp in recvs:
        cp.wait_recv()
    for cp in sends:
        cp.wait_send()
    for cp in local_copies:
        cp.wait()


def _split_comm_refs(refs, n_in, n_out, n_scratch, comm):
    ns, nd = (len(comm.srcs), len(comm.out_shapes)) if comm else (0, 0)
    cuts = [n_in, ns, n_out, nd, n_scratch]
    parts, at = [], 0
    for n in cuts:
        parts.append(refs[at:at + n])
        at += n
    return (*parts, refs[at:])


AUG_F = HEAD_DIM
AUG_LSE = HEAD_DIM + 6


def _half_cols(x, lo):
    return (jnp.sum(jnp.where(lo, x, 0.0), axis=-1, keepdims=True),
            jnp.sum(jnp.where(lo, 0.0, x), axis=-1, keepdims=True))


def _half_sums(x, lo):
    s_lo, s_hi = _half_cols(x, lo)
    return jnp.where(lo, s_lo, s_hi)


def _split3(x):
    a = x.astype(BF16).astype(F32)
    r = x - a
    b = r.astype(BF16).astype(F32)
    return a, b, (r - b).astype(BF16).astype(F32)


def _aug(lane, base, terms):
    out = jnp.zeros(lane.shape, F32)
    for i, t in enumerate(terms):
        out = jnp.where(lane == base + i, t, out)
    return out


def _head_lanes(x2, h):
    return x2 if h == 0 else pltpu.roll(x2, HEAD_DIM, 1)


def _fox_prep_fwd(proj, qg, kg, fcol, *, d_model, name, comm=None):
    T = proj.shape[0]
    nhp = d_model // LANE
    tr = _row_tile(T, 2 * WIDE_ROWS)

    def body(*refs):
        ((q_ref, k_ref, v_ref, qg_ref, kg_ref, f_ref), src_refs,
         (qa_ref, qta_ref, ka_ref, kta_ref, va_ref, vta_ref), dst_refs, _, sems) = _split_comm_refs(refs, 6, 6, 0, comm)
        if comm:
            @pl.when((pl.program_id(0) == 0) & (pl.program_id(1) == 0))
            def _():
                _comm_start(_comm_copies(comm.plan, src_refs, dst_refs, *sems))
        lane = lax.broadcasted_iota(jnp.int32, (tr, LANE), 1)
        lo = lane < HEAD_DIM

        def norm(xv, g):
            ms = _half_sums(xv * xv, lo) * (1.0 / HEAD_DIM)
            return (xv * lax.rsqrt(ms + EPS)) * g

        qn = norm(q_ref[...], qg_ref[...]) * (HEAD_DIM ** -0.5)
        kn = norm(k_ref[...], kg_ref[...])
        vv = v_ref[...]
        qa, ka, va, vta = [], [], [], []
        for h in range(2):
            f1, f2, f3 = _split3(f_ref[h])
            qa.append(jnp.where(lo, _head_lanes(qn, h), _aug(lane, AUG_F, [f1, f2, f3, 1.0, 1.0, 1.0])))
            ka.append(jnp.where(lo, _head_lanes(kn, h),
                                _aug(lane, AUG_F, [1.0, 1.0, 1.0, -f1, -f2, -f3, 1.0, 1.0, 1.0])))
            va.append(jnp.where(lo if h == 0 else jnp.logical_not(lo), vv, 0.0))
            vta.append(jnp.where(lo, _head_lanes(vv, h), _aug(lane, AUG_F, [1.0, 1.0, 1.0])))
        for parts, ref, tref in ((qa, qa_ref, qta_ref), (ka, ka_ref, kta_ref), (va, va_ref, None),
                                 (vta, None, vta_ref)):
            both = jnp.concatenate(parts, axis=1)
            if ref is not None:
                ref[...] = both.astype(BF16)
            if tref is not None:
                tref[...] = both.astype(BF16).T
        if comm:
            @pl.when((pl.program_id(0) == T // tr - 1) & (pl.program_id(1) == nhp - 1))
            def _():
                _comm_wait(_comm_copies(comm.plan, src_refs, dst_refs, *sems))

    gain = pl.BlockSpec((1, LANE), lambda i, h: (0, 0))
    rows = pl.BlockSpec((tr, 2 * LANE), lambda i, h: (i, h))
    cols = pl.BlockSpec((2 * LANE, tr), lambda i, h: (h, i))
    wide, tall = jax.ShapeDtypeStruct((T, 2 * d_model), BF16), jax.ShapeDtypeStruct((2 * d_model, T), BF16)
    outs = pl.pallas_call(
        body, name=name, grid=(T // tr, nhp),
        in_specs=[pl.BlockSpec((tr, LANE), lambda i, h: (i, h)),
                  pl.BlockSpec((tr, LANE), lambda i, h: (i, nhp + h)),
                  pl.BlockSpec((tr, LANE), lambda i, h: (i, 2 * nhp + h)), gain, gain,
                  pl.BlockSpec((2, tr, 1), lambda i, h: (h, i, 0))] + ([ANY] * len(comm.srcs) if comm else []),
        out_specs=[rows, cols, rows, cols, rows, cols] + ([ANY] * len(comm.out_shapes) if comm else []),
        out_shape=[wide, tall, wide, tall, wide, tall] + (list(comm.out_shapes) if comm else []),
        scratch_shapes=comm.scratch() if comm else [],
        compiler_params=_cparams("arbitrary", "arbitrary") if comm else _cparams("parallel", "parallel"),
    )(proj, proj, proj, qg, kg, fcol, *(comm.srcs if comm else []))
    return outs[:6], outs[6:]


def _fox_do_prep(do, o, *, name):
    T, D = do.shape
    nhp = D // LANE
    tr = _row_tile(T, 2 * WIDE_ROWS)

    def body(do_ref, o_ref, doa_ref, dota_ref):
        lane = lax.broadcasted_iota(jnp.int32, (tr, LANE), 1)
        lo = lane < HEAD_DIM
        dob = do_ref[...].astype(BF16).astype(F32)
        deltas = _half_cols(dob * o_ref[...], lo)
        both = jnp.concatenate(
            [jnp.where(lo, _head_lanes(dob, h), _aug(lane, AUG_F, _split3(-deltas[h]))) for h in range(2)], axis=1)
        doa_ref[...] = both.astype(BF16)
        dota_ref[...] = both.astype(BF16).T

    blk = pl.BlockSpec((tr, LANE), lambda i, h: (i, h))
    return pl.pallas_call(
        body, name=name, grid=(T // tr, nhp), in_specs=[blk, blk],
        out_specs=[pl.BlockSpec((tr, 2 * LANE), lambda i, h: (i, h)),
                   pl.BlockSpec((2 * LANE, tr), lambda i, h: (h, i))],
        out_shape=[jax.ShapeDtypeStruct((T, 2 * D), BF16), jax.ShapeDtypeStruct((2 * D, T), BF16)],
        compiler_params=_cparams("parallel", "parallel"),
    )(do, o)


def _fox_prep_bwd(proj, dq, dkt, dvt, qg, kg, *, d_model, name):
    T = proj.shape[0]
    nhp = d_model // LANE
    tr = _row_tile(T, 2 * WIDE_ROWS)

    def body(q_ref, k_ref, dq_ref, dkt_ref, dvt_ref, qg_ref, kg_ref, dqo_ref, dko_ref, dvo_ref, sums_ref):
        first = (pl.program_id(0) == 0) & (pl.program_id(1) == 0)
        lo = lax.broadcasted_iota(jnp.int32, (tr, LANE), 1) < HEAD_DIM

        def pair(x2):
            return jnp.where(lo, x2[:, :LANE], pltpu.roll(x2[:, LANE:], HEAD_DIM, 1))

        def bwd(xv, dxhat, g):
            ms = _half_sums(xv * xv, lo) * (1.0 / HEAD_DIM)
            r = lax.rsqrt(ms + EPS)
            n = xv * r
            dn = dxhat * g
            dx = r * (dn - n * (_half_sums(dn * n, lo) * (1.0 / HEAD_DIM)))
            dg = jnp.sum(dxhat * n, axis=0, keepdims=True)
            return dx, dg + pltpu.roll(dg, HEAD_DIM, 1)

        dxq, dgq = bwd(q_ref[...], pair(dq_ref[...]) * (HEAD_DIM ** -0.5), qg_ref[...])
        dxk, dgk = bwd(k_ref[...], pair(dkt_ref[...].T), kg_ref[...])
        dqo_ref[...] = dxq.astype(BF16)
        dko_ref[...] = dxk.astype(BF16)
        dvo_ref[...] = pair(dvt_ref[...].T.astype(F32)).astype(BF16)
        part = jnp.concatenate([dgq, dgk, jnp.zeros((6, LANE), F32)], axis=0)

        @pl.when(first)
        def _():
            sums_ref[...] = part

        @pl.when(jnp.logical_not(first))
        def _():
            sums_ref[...] += part

    gain = pl.BlockSpec((1, LANE), lambda i, h: (0, 0))
    blk = pl.BlockSpec((tr, LANE), lambda i, h: (i, h))
    tall = pl.BlockSpec((2 * LANE, tr), lambda i, h: (h, i))
    return pl.pallas_call(
        body, name=name, grid=(T // tr, nhp),
        in_specs=[blk, pl.BlockSpec((tr, LANE), lambda i, h: (i, nhp + h)),
                  pl.BlockSpec((tr, 2 * LANE), lambda i, h: (i, h)), tall, tall, gain, gain],
        out_specs=[blk, blk, blk, pl.BlockSpec((8, LANE), lambda i, h: (0, 0))],
        out_shape=[jax.ShapeDtypeStruct((T, d_model), BF16)] * 3 + [jax.ShapeDtypeStruct((8, LANE), F32)],
        compiler_params=_cparams("arbitrary", "arbitrary"),
    )(proj, proj, dq, dkt, dvt, qg, kg)


def _scan_lanes(x, reverse):
    n = x.shape[-1]
    lane = lax.broadcasted_iota(jnp.int32, x.shape, 1)
    sh = 1
    while sh < n:
        if reverse:
            x = x + jnp.where(lane < n - sh, pltpu.roll(x, n - sh, 1), 0.0)
        else:
            x = x + jnp.where(lane >= sh, pltpu.roll(x, sh, 1), 0.0)
        sh *= 2
    return x


def _fox_gate_fwd(fpre_t, bf, *, name):
    def body(f_ref, b_ref, o_ref):
        xv = f_ref[...] + b_ref[...]
        logf = jnp.minimum(xv, 0.0) - jnp.log1p(jnp.exp(-jnp.abs(xv)))
        o_ref[...] = _scan_lanes(logf, reverse=False)

    return pl.pallas_call(body, name=name, out_shape=jax.ShapeDtypeStruct(fpre_t.shape, F32))(fpre_t, bf)


def _fox_gate_bwd(dcol, drow, fpre_t, bf, *, name):
    H = fpre_t.shape[0]

    def body(dc_ref, dr_ref, f_ref, b_ref, o_ref, db_ref):
        xv = f_ref[...] + b_ref[...]
        e = dc_ref[...] - dr_ref[...]
        dlogf = _scan_lanes(e, reverse=False) - e
        dpre = dlogf * (1.0 - jax.nn.sigmoid(xv))
        o_ref[...] = dpre
        db_ref[...] = jnp.broadcast_to(jnp.sum(dpre, axis=-1, keepdims=True), (H, LANE))

    return pl.pallas_call(
        body, name=name,
        out_shape=[jax.ShapeDtypeStruct(fpre_t.shape, F32), jax.ShapeDtypeStruct((H, LANE), F32)],
    )(dcol, drow, fpre_t, bf)


_NT = (((1,), (1,)), ((), ()))
_TN = (((0,), (0,)), ((), ()))
_NN = (((1,), (0,)), ((), ()))


def _attn_tile(T):
    return min(T, 1024)


def _causal(tq, tk):
    return lax.broadcasted_iota(jnp.int32, (tq, tk), 1) <= lax.broadcasted_iota(jnp.int32, (tq, tk), 0)


def _fox_attn_fwd(qa, kta, va, *, name, comm=None):
    T = qa.shape[0]
    nhp = qa.shape[1] // (2 * LANE)
    tq = tk = _attn_tile(T)
    nq = T // tq

    def body(*refs):
        (qa_ref, kta_ref, va_ref), src_refs, (o_ref, qb_ref), dst_refs, (m_sc, l_sc, acc_sc), sems = (
            _split_comm_refs(refs, 3, 2, 3, comm))
        hp, i, j = pl.program_id(0), pl.program_id(1), pl.program_id(2)
        if comm:
            @pl.when((hp == 0) & (i == 0) & (j == 0))
            def _():
                _comm_start(_comm_copies(comm.plan, src_refs, dst_refs, *sems))

        @pl.when(j == 0)
        def _():
            m_sc[...] = jnp.full(m_sc.shape, NEG, F32)
            l_sc[...] = jnp.zeros(l_sc.shape, F32)
            acc_sc[...] = jnp.zeros(acc_sc.shape, F32)

        def block(diagonal):
            heads = [slice(h * LANE, (h + 1) * LANE) for h in range(2)]
            scores = [lax.dot_general(qa_ref[:, hs], kta_ref[hs, :], _NN, preferred_element_type=F32)
                      for hs in heads]
            state = [(m_sc[h], l_sc[h], acc_sc[h]) for h in range(2)]
            probs, updates = [], []
            for s, (m_prev, l_prev, _) in zip(scores, state):
                if diagonal:
                    s = jnp.where(_causal(tq, tk), s, NEG)
                m_next = jnp.maximum(m_prev, jnp.max(s, axis=1, keepdims=True))
                p = jnp.exp(s - jnp.tile(m_next, (1, tk // LANE)))
                alpha = jnp.exp(m_prev - m_next)
                probs.append(p.astype(BF16))
                updates.append((m_next, alpha, alpha * l_prev + jnp.sum(p, axis=1, keepdims=True)))
            pvs = [lax.dot_general(p, va_ref[:, hs], _NN, preferred_element_type=F32)
                   for p, hs in zip(probs, heads)]
            for h in range(2):
                m_next, alpha, l_next = updates[h]
                m_sc[h] = m_next
                l_sc[h] = l_next
                acc_sc[h] = alpha * state[h][2] + pvs[h]

        @pl.when(j < i)
        def _():
            block(False)

        @pl.when(j == i)
        def _():
            block(True)
            o_ref[...] = acc_sc[0] / l_sc[0] + acc_sc[1] / l_sc[1]
            lane = lax.broadcasted_iota(jnp.int32, (tq, LANE), 1)
            for h in range(2):
                hs = slice(h * LANE, (h + 1) * LANE)
                pieces = _split3(-(m_sc[h] + jnp.log(l_sc[h])))
                qb = qa_ref[:, hs].astype(F32)
                for n, piece in enumerate(pieces):
                    qb = jnp.where(lane == AUG_LSE + n, piece, qb)
                qb_ref[:, hs] = qb.astype(BF16)

        if comm:
            @pl.when((hp == nhp - 1) & (i == nq - 1) & (j == nq - 1))
            def _():
                _comm_wait(_comm_copies(comm.plan, src_refs, dst_refs, *sems))

    outs = pl.pallas_call(
        body, name=name, grid=(nhp, nq, nq),
        in_specs=[pl.BlockSpec((tq, 2 * LANE), lambda h, i, j: (i, h)),
                  pl.BlockSpec((2 * LANE, tk), lambda h, i, j: (h, jnp.minimum(j, i))),
                  pl.BlockSpec((tk, 2 * LANE), lambda h, i, j: (jnp.minimum(j, i), h))]
        + ([ANY] * len(comm.srcs) if comm else []),
        out_specs=[pl.BlockSpec((tq, LANE), lambda h, i, j: (i, h)),
                   pl.BlockSpec((tq, 2 * LANE), lambda h, i, j: (i, h))]
        + ([ANY] * len(comm.out_shapes) if comm else []),
        out_shape=[jax.ShapeDtypeStruct((T, nhp * LANE), F32), jax.ShapeDtypeStruct(qa.shape, BF16)]
        + (list(comm.out_shapes) if comm else []),
        scratch_shapes=[pltpu.VMEM((2, tq, LANE), F32), pltpu.VMEM((2, tq, LANE), F32),
                        pltpu.VMEM((2, tq, LANE), F32)] + (comm.scratch() if comm else []),
        compiler_params=(_cparams("arbitrary", "arbitrary", "arbitrary") if comm
                         else _cparams("parallel", "parallel", "arbitrary")),
    )(qa, kta, va, *(comm.srcs if comm else []))
    return outs[0], outs[1], outs[2:]


def _fox_attn_bwd(qb, qta, ka, kta, vta, doa, dota, *, name, comm=None):
    T = qb.shape[0]
    nhp = qb.shape[1] // (2 * LANE)
    tq = tk = _attn_tile(T)
    nq = T // tq

    def body(*refs):
        ((qb_ref, qta_ref, ka_ref, kta_ref, vta_ref, doa_ref, dota_ref), src_refs,
         (dq_ref, dkt_ref, dvt_ref, dcol_ref, drow_ref), dst_refs, (dkt_sc, dvt_sc, dcol_sc), sems) = (
            _split_comm_refs(refs, 7, 5, 3, comm))
        hp, j, i = pl.program_id(0), pl.program_id(1), pl.program_id(2)
        if comm:
            @pl.when((hp == 0) & (j == 0) & (i == 0))
            def _():
                _comm_start(_comm_copies(comm.plan, src_refs, dst_refs, *sems))

        @pl.when((j == 0) & (i == 0))
        def _():
            dq_ref[...] = jnp.zeros(dq_ref.shape, F32)
            drow_ref[...] = jnp.zeros(drow_ref.shape, F32)

        @pl.when(i == 0)
        def _():
            dkt_sc[...] = jnp.zeros(dkt_sc.shape, F32)
            dvt_sc[...] = jnp.zeros(dvt_sc.shape, F32)
            dcol_sc[...] = jnp.zeros(dcol_sc.shape, F32)

        def block(diagonal):
            rows = pl.ds(pl.multiple_of(i * tq, tq), tq)
            heads = [slice(h * LANE, (h + 1) * LANE) for h in range(2)]
            logits = [lax.dot_general(qb_ref[:, hs], kta_ref[hs, :], _NN, preferred_element_type=F32)
                      for hs in heads]
            dpds = [lax.dot_general(doa_ref[:, hs], vta_ref[hs, :], _NN, preferred_element_type=F32)
                    for hs in heads]
            pbs, dlbs = [], []
            for h in range(2):
                p = jnp.exp(logits[h])
                if diagonal:
                    p = jnp.where(_causal(tq, tk), p, 0.0)
                dl = p * dpds[h]
                pbs.append(p.astype(BF16))
                dlbs.append(dl.astype(BF16))
                dcol_sc[h] += jnp.sum(dl, axis=0, keepdims=True)
                drow_ref[h, rows, :] += jnp.sum(dl, axis=1, keepdims=True)
            for h, hs in enumerate(heads):
                dvt_sc[h] += lax.dot_general(dota_ref[hs, :], pbs[h], _NN, preferred_element_type=F32)
                dkt_sc[h] += lax.dot_general(qta_ref[hs, :], dlbs[h], _NN, preferred_element_type=F32)
                dq_ref[rows, hs] += lax.dot_general(dlbs[h], ka_ref[:, hs], _NN, preferred_element_type=F32)

        @pl.when(i > j)
        def _():
            block(False)

        @pl.when(i == j)
        def _():
            block(True)

        @pl.when(i == nq - 1)
        def _():
            dkt_ref[...] = jnp.concatenate([dkt_sc[0], dkt_sc[1]], axis=0)
            dvt_ref[...] = jnp.concatenate([dvt_sc[0], dvt_sc[1]], axis=0).astype(BF16)
            dcol_ref[...] = dcol_sc[...]

        if comm:
            @pl.when((hp == nhp - 1) & (j == nq - 1) & (i == nq - 1))
            def _():
                _comm_wait(_comm_copies(comm.plan, src_refs, dst_refs, *sems))

    qrow = pl.BlockSpec((tq, 2 * LANE), lambda h, j, i: (jnp.maximum(i, j), h))
    qcol = pl.BlockSpec((2 * LANE, tq), lambda h, j, i: (h, jnp.maximum(i, j)))
    krow = pl.BlockSpec((tk, 2 * LANE), lambda h, j, i: (j, h))
    kcol = pl.BlockSpec((2 * LANE, tk), lambda h, j, i: (h, j))
    tall = jax.ShapeDtypeStruct((qb.shape[1], T), F32)
    outs = pl.pallas_call(
        body, name=name, grid=(nhp, nq, nq),
        in_specs=[qrow, qcol, krow, kcol, kcol, qrow, qcol] + ([ANY] * len(comm.srcs) if comm else []),
        out_specs=[pl.BlockSpec((T, 2 * LANE), lambda h, j, i: (0, h)), kcol, kcol,
                   pl.BlockSpec((2, 1, tk), lambda h, j, i: (h, 0, j)),
                   pl.BlockSpec((2, T, 1), lambda h, j, i: (h, 0, 0))]
        + ([ANY] * len(comm.out_shapes) if comm else []),
        out_shape=[jax.ShapeDtypeStruct(qb.shape, F32), tall, jax.ShapeDtypeStruct(tall.shape, BF16),
                   jax.ShapeDtypeStruct((2 * nhp, 1, T), F32), jax.ShapeDtypeStruct((2 * nhp, T, 1), F32)]
        + (list(comm.out_shapes) if comm else []),
        scratch_shapes=[pltpu.VMEM((2, LANE, tk), F32), pltpu.VMEM((2, LANE, tk), F32),
                        pltpu.VMEM((2, 1, tk), F32)] + (comm.scratch() if comm else []),
        compiler_params=_cparams("arbitrary" if comm else "parallel", "arbitrary", "arbitrary"),
    )(qb, qta, ka, kta, vta, doa, dota, *(comm.srcs if comm else []))
    return (*outs[:5], outs[5:])


_GELU_C = math.sqrt(2.0 / math.pi)
_GELU_A = 0.044715


def _gelu(x):
    t = jnp.tanh(_GELU_C * (x + _GELU_A * (x * x * x)))
    return x * (0.5 * (1.0 + t)), t


def _gelu_grad(x, t):
    return 0.5 * (1.0 + t) + 0.5 * x * (1.0 - t * t) * (_GELU_C * (1.0 + 3.0 * _GELU_A * x * x))


def _layer_norm_stats(v):
    mu = jnp.mean(v, axis=-1, keepdims=True)
    vc = v - mu
    rstd = lax.rsqrt(jnp.mean(vc * vc, axis=-1, keepdims=True) + EPS)
    return vc * rstd, rstd


def _layer_norm_bwd(dyhat, yhat, rstd):
    return rstd * (dyhat - jnp.mean(dyhat, axis=-1, keepdims=True)
                   - yhat * jnp.mean(dyhat * yhat, axis=-1, keepdims=True))


def _sg_mask():
    t = lax.broadcasted_iota(jnp.int32, (SG_CHUNK, SG_CHUNK), 0) // SG_CAUSAL
    s = lax.broadcasted_iota(jnp.int32, (SG_CHUNK, SG_CHUNK), 1) // SG_CAUSAL
    return s <= t


def _sg_mix(ws_ref, bc_ref, vln_sc, vo_sc, tr, gd):
    mask = _sg_mask()
    for g in range(SG_GROUPS):
        wg = jnp.where(mask, ws_ref[g], 0.0).astype(BF16)
        cols = slice(g * gd, (g + 1) * gd)
        for n in range(tr // SG_CHUNK):
            rows = slice(n * SG_CHUNK, (n + 1) * SG_CHUNK)
            vo_sc[rows, cols] = lax.dot_general(wg, vln_sc[rows, cols], _NN,
                                                preferred_element_type=F32) + bc_ref[g]


def _sg_fwd(a_uv, ln_g, ln_b, ws, bcol, *, name):
    T, W = a_uv.shape[0], a_uv.shape[1] // 2
    gd = W // SG_GROUPS
    tr = _row_tile(T)

    def body(u_ref, v_ref, g_ref, b_ref, ws_ref, bc_ref, o_ref, vln_sc, vo_sc):
        u, _ = _gelu(u_ref[...])
        v, _ = _gelu(v_ref[...])
        vhat, _ = _layer_norm_stats(v)
        vln_sc[...] = (vhat * g_ref[...] + b_ref[...]).astype(BF16)
        _sg_mix(ws_ref, bc_ref, vln_sc, vo_sc, tr, gd)
        o_ref[...] = (u * vo_sc[...]).astype(BF16)

    row = pl.BlockSpec((1, W), lambda i: (0, 0))
    return pl.pallas_call(
        body, name=name, grid=(T // tr,),
        in_specs=[pl.BlockSpec((tr, W), lambda i: (i, 0)), pl.BlockSpec((tr, W), lambda i: (i, 1)), row, row,
                  pl.BlockSpec((SG_GROUPS, SG_CHUNK, SG_CHUNK), lambda i: (0, 0, 0)),
                  pl.BlockSpec((SG_GROUPS, SG_CHUNK, 1), lambda i: (0, 0, 0))],
        out_specs=pl.BlockSpec((tr, W), lambda i: (i, 0)),
        out_shape=jax.ShapeDtypeStruct((T, W), BF16),
        scratch_shapes=[pltpu.VMEM((tr, W), BF16), pltpu.VMEM((tr, W), F32)],
        compiler_params=_cparams("parallel"),
    )(a_uv, a_uv, ln_g, ln_b, ws, bcol)


def _sg_bwd(a_uv, dgate, ln_g, ln_b, ws, bcol, *, name):
    T, W = a_uv.shape[0], a_uv.shape[1] // 2
    gd = W // SG_GROUPS
    tr = _row_tile(T)

    def body(u_ref, v_ref, dg_ref, g_ref, b_ref, ws_ref, bc_ref,
             da_ref, dws_ref, dbs_ref, sums_ref, vln_sc, vo_sc, dvo_sc, dvln_sc):
        i = pl.program_id(0)

        @pl.when(i == 0)
        def _():
            dws_ref[...] = jnp.zeros(dws_ref.shape, F32)
            dbs_ref[...] = jnp.zeros(dbs_ref.shape, F32)
            sums_ref[...] = jnp.zeros(sums_ref.shape, F32)

        ua, va = u_ref[...], v_ref[...]
        u, tu = _gelu(ua)
        v, tv = _gelu(va)
        vhat, rstd = _layer_norm_stats(v)
        vln_sc[...] = (vhat * g_ref[...] + b_ref[...]).astype(BF16)
        _sg_mix(ws_ref, bc_ref, vln_sc, vo_sc, tr, gd)
        dgt = dg_ref[...]
        du = dgt * vo_sc[...]
        dvo_sc[...] = dgt * u
        mask = _sg_mask()
        for g in range(SG_GROUPS):
            wg = jnp.where(mask, ws_ref[g], 0.0).astype(BF16)
            cols = slice(g * gd, (g + 1) * gd)
            acc_w = jnp.zeros((SG_CHUNK, SG_CHUNK), F32)
            acc_b = jnp.zeros((SG_CHUNK, 1), F32)
            for n in range(tr // SG_CHUNK):
                rows = slice(n * SG_CHUNK, (n + 1) * SG_CHUNK)
                dvo = dvo_sc[rows, cols]
                dvob = dvo.astype(BF16)
                dvln_sc[rows, cols] = lax.dot_general(wg, dvob, _TN, preferred_element_type=F32)
                acc_w += lax.dot_general(dvob, vln_sc[rows, cols], _NT, preferred_element_type=F32)
                acc_b += jnp.sum(dvo, axis=1, keepdims=True)
            dws_ref[g] += jnp.where(mask, acc_w, 0.0)
            dbs_ref[g] += acc_b
        dvln = dvln_sc[...]
        sums_ref[...] += jnp.concatenate([jnp.sum(dvln * vhat, axis=0, keepdims=True),
                                          jnp.sum(dvln, axis=0, keepdims=True),
                                          jnp.zeros((6, W), F32)], axis=0)
        dv = _layer_norm_bwd(dvln * g_ref[...], vhat, rstd)
        da_ref[:, :W] = (du * _gelu_grad(ua, tu)).astype(BF16)
        da_ref[:, W:] = (dv * _gelu_grad(va, tv)).astype(BF16)

    row = pl.BlockSpec((1, W), lambda i: (0, 0))
    wspec = pl.BlockSpec((SG_GROUPS, SG_CHUNK, SG_CHUNK), lambda i: (0, 0, 0))
    bspec = pl.BlockSpec((SG_GROUPS, SG_CHUNK, 1), lambda i: (0, 0, 0))
    return pl.pallas_call(
        body, name=name, grid=(T // tr,),
        in_specs=[pl.BlockSpec((tr, W), lambda i: (i, 0)), pl.BlockSpec((tr, W), lambda i: (i, 1)),
                  pl.BlockSpec((tr, W), lambda i: (i, 0)), row, row, wspec, bspec],
        out_specs=[pl.BlockSpec((tr, 2 * W), lambda i: (i, 0)), wspec, bspec,
                   pl.BlockSpec((8, W), lambda i: (0, 0))],
        out_shape=[jax.ShapeDtypeStruct((T, 2 * W), BF16),
                   jax.ShapeDtypeStruct((SG_GROUPS, SG_CHUNK, SG_CHUNK), F32),
                   jax.ShapeDtypeStruct((SG_GROUPS, SG_CHUNK, 1), F32),
                   jax.ShapeDtypeStruct((8, W), F32)],
        scratch_shapes=[pltpu.VMEM((tr, W), BF16), pltpu.VMEM((tr, W), F32),
                        pltpu.VMEM((tr, W), F32), pltpu.VMEM((tr, W), F32)],
        compiler_params=_cparams("arbitrary"),
    )(a_uv, a_uv, dgate, ln_g, ln_b, ws, bcol)


SUBLANES = 8


def _shift_rows(xc_sc, xs_sc):
    rows = xs_sc.shape[1]
    for p in range(1, SUBLANES):
        xs_sc[p - 1] = xc_sc[pl.ds(p, rows), :]


def _rows_at(xc_sc, xs_sc, offset, tr):
    p = offset % SUBLANES
    base = offset - p
    return xc_sc[pl.ds(base, tr), :] if p == 0 else xs_sc[p - 1, pl.ds(base, tr), :]


def _shift_scratch(tr, C):
    return pltpu.VMEM((SUBLANES - 1, tr + CONV_HALO - SUBLANES, C), F32)


def _cv_glu_conv(a_ref, b_ref, ap_ref, bp_ref, w_ref, bd_ref, xc_sc, xs_sc, tr):
    i = pl.program_id(0)
    prev = ap_ref[...] * jax.nn.sigmoid(bp_ref[...])
    xc_sc[0:CONV_HALO, :] = jnp.where(i > 0, prev, 0.0)
    xc_sc[CONV_HALO:, :] = a_ref[...] * jax.nn.sigmoid(b_ref[...])
    _shift_rows(xc_sc, xs_sc)
    acc = jnp.broadcast_to(bd_ref[...], (tr, bd_ref.shape[1]))
    for k in range(CONV_WIDTH):
        acc = acc + w_ref[k:k + 1, :] * _rows_at(xc_sc, xs_sc, CONV_HALO - (CONV_WIDTH - 1) + k, tr)
    return acc


def _cv_specs(T, C, tr):
    hb = tr // CONV_HALO
    cur = lambda col: pl.BlockSpec((tr, C), lambda i: (i, col))
    prev = lambda col: pl.BlockSpec((CONV_HALO, C), lambda i: (jnp.maximum(i * hb - 1, 0), col))
    row = pl.BlockSpec((1, C), lambda i: (0, 0))
    wspec = pl.BlockSpec((CONV_HALO, C), lambda i: (0, 0))
    return cur, prev, row, wspec


def _cv_fwd(p, w_dw, b_dw, ln_g, ln_b, *, name):
    T, C = p.shape[0], p.shape[1] // 2
    tr = _row_tile(T)
    cur, prev, row, wspec = _cv_specs(T, C, tr)

    def body(a_ref, b_ref, ap_ref, bp_ref, w_ref, bd_ref, g_ref, be_ref, o_ref, xc_sc, xs_sc):
        y2 = _cv_glu_conv(a_ref, b_ref, ap_ref, bp_ref, w_ref, bd_ref, xc_sc, xs_sc, tr)
        yhat, _ = _layer_norm_stats(y2)
        yln = yhat * g_ref[...] + be_ref[...]
        o_ref[...] = (yln * jax.nn.sigmoid(yln)).astype(BF16)

    return pl.pallas_call(
        body, name=name, grid=(T // tr,),
        in_specs=[cur(0), cur(1), prev(0), prev(1), wspec, row, row, row],
        out_specs=pl.BlockSpec((tr, C), lambda i: (i, 0)),
        out_shape=jax.ShapeDtypeStruct((T, C), BF16),
        scratch_shapes=[pltpu.VMEM((tr + CONV_HALO, C), F32), _shift_scratch(tr, C)],
        compiler_params=_cparams("parallel"),
    )(p, p, p, p, w_dw, b_dw, ln_g, ln_b)


def _cv_bwd_ln(p, dy3, w_dw, b_dw, ln_g, ln_b, *, name):
    T, C = p.shape[0], p.shape[1] // 2
    tr = _row_tile(T)
    cur, prev, row, wspec = _cv_specs(T, C, tr)

    def body(a_ref, b_ref, ap_ref, bp_ref, dy_ref, w_ref, bd_ref, g_ref, be_ref,
             dy2_ref, dw_ref, sums_ref, xc_sc, xs_sc):
        i = pl.program_id(0)
        y2 = _cv_glu_conv(a_ref, b_ref, ap_ref, bp_ref, w_ref, bd_ref, xc_sc, xs_sc, tr)
        yhat, rstd = _layer_norm_stats(y2)
        yln = yhat * g_ref[...] + be_ref[...]
        s = jax.nn.sigmoid(yln)
        dyln = dy_ref[...] * (s + yln * s * (1.0 - s))
        dy2 = _layer_norm_bwd(dyln * g_ref[...], yhat, rstd)
        dy2_ref[...] = dy2
        sums = jnp.concatenate([jnp.sum(dy2, axis=0, keepdims=True),
                                jnp.sum(dyln * yhat, axis=0, keepdims=True),
                                jnp.sum(dyln, axis=0, keepdims=True),
                                jnp.zeros((5, C), F32)], axis=0)
        taps = [jnp.sum(dy2 * _rows_at(xc_sc, xs_sc, CONV_HALO - (CONV_WIDTH - 1) + k, tr), axis=0, keepdims=True)
                for k in range(CONV_WIDTH)]
        dw = jnp.concatenate(taps + [jnp.zeros((CONV_HALO - CONV_WIDTH, C), F32)], axis=0)

        @pl.when(i == 0)
        def _():
            sums_ref[...] = sums
            dw_ref[...] = dw

        @pl.when(i > 0)
        def _():
            sums_ref[...] += sums
            dw_ref[...] += dw

    blk = pl.BlockSpec((tr, C), lambda i: (i, 0))
    return pl.pallas_call(
        body, name=name, grid=(T // tr,),
        in_specs=[cur(0), cur(1), prev(0), prev(1), blk, wspec, row, row, row],
        out_specs=[blk, wspec, pl.BlockSpec((8, C), lambda i: (0, 0))],
        out_shape=[jax.ShapeDtypeStruct((T, C), F32), jax.ShapeDtypeStruct((CONV_HALO, C), F32),
                   jax.ShapeDtypeStruct((8, C), F32)],
        scratch_shapes=[pltpu.VMEM((tr + CONV_HALO, C), F32), _shift_scratch(tr, C)],
        compiler_params=_cparams("arbitrary"),
    )(p, p, p, p, dy3, w_dw, b_dw, ln_g, ln_b)


def _cv_bwd_in(p, dy2, w_dw, *, name):
    T, C = p.shape[0], p.shape[1] // 2
    tr = _row_tile(T)
    hb = tr // CONV_HALO
    nblk = T // tr
    last_halo = T // CONV_HALO - 1

    def body(a_ref, b_ref, dy_ref, dyn_ref, w_ref, dp_ref, sums_ref, xc_sc, xs_sc):
        i = pl.program_id(0)
        xc_sc[0:tr, :] = dy_ref[...]
        xc_sc[tr:, :] = jnp.where(i < nblk - 1, dyn_ref[...], 0.0)
        _shift_rows(xc_sc, xs_sc)
        dy1 = jnp.zeros((tr, C), F32)
        for k in range(CONV_WIDTH):
            dy1 = dy1 + w_ref[k:k + 1, :] * _rows_at(xc_sc, xs_sc, CONV_WIDTH - 1 - k, tr)
        a = a_ref[...]
        sb = jax.nn.sigmoid(b_ref[...])
        da = dy1 * sb
        db = dy1 * a * sb * (1.0 - sb)
        dp_ref[:, :C] = da.astype(BF16)
        dp_ref[:, C:] = db.astype(BF16)
        sums = jnp.concatenate([
            jnp.concatenate([jnp.sum(da, axis=0, keepdims=True), jnp.sum(db, axis=0, keepdims=True)], axis=1),
            jnp.zeros((7, 2 * C), F32)], axis=0)

        @pl.when(i == 0)
        def _():
            sums_ref[...] = sums

        @pl.when(i > 0)
        def _():
            sums_ref[...] += sums

    blk = lambda col: pl.BlockSpec((tr, C), lambda i: (i, col))
    return pl.pallas_call(
        body, name=name, grid=(nblk,),
        in_specs=[blk(0), blk(1), blk(0),
                  pl.BlockSpec((CONV_HALO, C), lambda i: (jnp.minimum((i + 1) * hb, last_halo), 0)),
                  pl.BlockSpec((CONV_HALO, C), lambda i: (0, 0))],
        out_specs=[pl.BlockSpec((tr, 2 * C), lambda i: (i, 0)), pl.BlockSpec((8, 2 * C), lambda i: (0, 0))],
        out_shape=[jax.ShapeDtypeStruct((T, 2 * C), BF16), jax.ShapeDtypeStruct((8, 2 * C), F32)],
        scratch_shapes=[pltpu.VMEM((tr + CONV_HALO, C), F32), _shift_scratch(tr, C)],
        compiler_params=_cparams("arbitrary"),
    )(p, p, dy2, dy2, w_dw)


def _col_tile(n, want=1024):
    best = LANE
    for t in range(LANE, min(n, want) + 1, LANE):
        if n % t == 0:
            best = t
    return best if n % LANE == 0 else n


def _mm(a, b, *, name, ta=False, tb=False, **kw):
    M = a.shape[1] if ta else a.shape[0]
    N = b.shape[0] if tb else b.shape[1]
    K = a.shape[0] if ta else a.shape[1]
    kw.setdefault('tm', _col_tile(M, 1024 if ta else 512))
    kw.setdefault('tn', _col_tile(N, 1024))
    kw.setdefault('tk', K if tb else _col_tile(K, 2048 if ta else 1024))
    return _matmul(a, b, name=name, ta=ta, tb=tb, **kw)


def _relu2_epilogue(acc):
    r = jnp.maximum(acc, 0.0)
    return (r * r,)


def _residual_epilogue(acc, x, g):
    return acc, x + g * acc


def _residual_bias_epilogue(acc, x, g, b):
    y = acc + b
    return y, x + g * y


def _dh_norm_bwd(d_act, w, x, dres, nw, sc, *, name, gate=None, below=None):
    D = x.shape[1]

    def epilogue(dh, xv, dresv, wv, scv, *more):
        gated, under = (more[:2], more[2:]) if gate else ((), more)
        r = lax.rsqrt(jnp.mean(xv * xv, axis=-1, keepdims=True) + EPS)
        n = xv * r
        scale = 1.0 + scv
        dn = dh * (wv * scale)
        dx = dresv + r * (dn - n * jnp.mean(dn * n, axis=-1, keepdims=True))
        rows = [jnp.sum(dh, axis=0, keepdims=True),
                jnp.sum(dh * (n * wv), axis=0, keepdims=True),
                jnp.sum(dh * n * scale, axis=0, keepdims=True)]
        outs = [dx]
        if gated:
            yv, gv = gated
            outs.append(dx * gv)
            rows += [jnp.sum(dx * yv, axis=0, keepdims=True), jnp.sum(dx * gv, axis=0, keepdims=True)]
        rows += [jnp.zeros((7 - len(rows), D), F32)]
        if under:
            zv, gzv = under
            outs.append(dx * gzv)
            rows.append(jnp.sum(dx * zv, axis=0, keepdims=True))
        else:
            rows.append(jnp.zeros((1, D), F32))
        return (*outs, jnp.concatenate(rows, axis=0))

    extras = [(x, 'tile'), (dres, 'tile'), (nw, 'row'), (sc, 'row')]
    for branch in (gate, below):
        if branch:
            extras += [(branch[0], 'tile'), (branch[1], 'row')]
    return _mm(d_act, w, tb=True, name=name, tn=D, extras=extras, epilogue=epilogue,
               out_dtypes=(F32,) + (BF16,) * (bool(gate) + bool(below)), row_sums=True)


def _relu2_bwd_epilogue(acc, r):
    return (acc * (2.0 * jnp.sqrt(r.astype(F32))),)


def _bias_epilogue(acc, b):
    return (acc + b,)


def _fox_forward(h1, P, j, D, carried=None):
    carried = carried or {}

    def ride(kernel):
        return carried[kernel][0] if kernel in carried else None

    def landed(kernel, outs):
        if kernel in carried:
            carried[kernel][1](outs)

    H = D // HEAD_DIM
    proj = _mm(h1, P['fox_w_in'][j], name='fox_proj', b_outer=True, tm=WIDE_ROWS)
    qg = jnp.tile(P['fox_q_norm'][j][None, :], (1, 2))
    kg = jnp.tile(P['fox_k_norm'][j][None, :], (1, 2))
    fpre_t = proj[:, 3 * D:3 * D + H].T
    bf = P['fox_b_f'][j][:, None]
    fcum = _fox_gate_fwd(fpre_t, bf, name='fox_gate_fwd')
    (qa, qta, ka, kta, va, vta), outs = _fox_prep_fwd(proj, qg, kg, fcum[:, :, None], d_model=D, name='fox_prep_fwd',
                                                     comm=ride('prep'))
    landed('prep', outs)
    o, qb, outs = _fox_attn_fwd(qa, kta, va, name='fox_attn_fwd', comm=ride('attn'))
    landed('attn', outs)
    saved = dict(proj=proj, qg=qg, kg=kg, fpre_t=fpre_t, bf=bf, o=o, qb=qb, qta=qta, ka=ka, kta=kta, vta=vta)
    return o, saved


def _fox_backward(dy, h1, S, P, j, D, comm=None):
    H = D // HEAD_DIM
    w_out, w_in = P['fox_w_out'][j], P['fox_w_in'][j]
    g = {}
    g['fox_w_out'] = _mm(S['o'], dy, ta=True, name='fox_dw_out')
    do = _mm(dy, w_out, tb=True, name='fox_do')
    doa, dota = _fox_do_prep(do, S['o'], name='fox_do_prep')
    dq, dkt, dvt, dcol, drow, comm_outs = _fox_attn_bwd(S['qb'], S['qta'], S['ka'], S['kta'], S['vta'], doa, dota,
                                                        name='fox_attn_bwd', comm=comm)
    dqp, dkp, dvp, gsum = _fox_prep_bwd(S['proj'], dq, dkt, dvt, S['qg'], S['kg'], d_model=D, name='fox_prep_bwd')
    dfpre_t, dbf = _fox_gate_bwd(dcol[:, 0, :], drow[:, :, 0], S['fpre_t'], S['bf'], name='fox_gate_bwd')
    dfpre = jnp.pad(dfpre_t.T.astype(BF16), ((0, 0), (0, LANE - H)))
    dproj = jnp.concatenate([dqp, dkp, dvp, dfpre], axis=1)
    g['fox_w_in'] = _mm(h1, dproj, ta=True, name='fox_dw_in')[:, :3 * D + H]
    g['fox_b_f'] = dbf[:, 0]
    g['fox_q_norm'] = gsum[0, :HEAD_DIM]
    g['fox_k_norm'] = gsum[1, :HEAD_DIM]
    return (dproj, w_in), g, comm_outs


def _sg_forward(h1, P, D):
    a_uv = _mm(h1, P['sg_w_in'], name='sg_in', b_outer=True, tm=WIDE_ROWS)
    bcol = P['sg_b_s'][:, :, None]
    gate = _sg_fwd(a_uv, P['sg_ln_g'], P['sg_ln_b'], P['sg_w_s'], bcol, name='sg_fwd')
    return gate, dict(a_uv=a_uv, bcol=bcol, gate=gate)


def _sg_backward(dy, h1, S, P, D):
    g = {}
    g['sg_w_out'] = _mm(S['gate'], dy, ta=True, name='sg_dw_out')
    dgate = _mm(dy, P['sg_w_out'], tb=True, name='sg_dgate')
    da, dws, dbs, sums = _sg_bwd(S['a_uv'], dgate, P['sg_ln_g'], P['sg_ln_b'], P['sg_w_s'], S['bcol'],
                                 name='sg_bwd')
    g['sg_w_s'], g['sg_b_s'] = dws, dbs[:, :, 0]
    g['sg_ln_g'], g['sg_ln_b'] = sums[0], sums[1]
    g['sg_w_in'] = _mm(h1, da, ta=True, name='sg_dw_in', out_chips=N_CHIPS)
    return (da, P['sg_w_in']), g


def _cv_forward(h1, P, D):
    p = _mm(h1, P['cv_w_pw1'], name='cv_pw1', extras=[(P['cv_b_pw1'], 'row')], epilogue=_bias_epilogue,
            b_outer=True, tm=WIDE_ROWS)
    w_dw = jnp.pad(P['cv_w_dw'], ((0, CONV_HALO - CONV_WIDTH), (0, 0)))
    y3 = _cv_fwd(p, w_dw, P['cv_b_dw'], P['cv_ln_g'], P['cv_ln_b'], name='cv_fwd')
    return y3, dict(p=p, w_dw=w_dw, y3=y3)


def _cv_backward(dy, h1, S, P, D):
    g = {}
    g['cv_w_pw2'] = _mm(S['y3'], dy, ta=True, name='cv_dw_pw2')
    dy3 = _mm(dy, P['cv_w_pw2'], tb=True, name='cv_dy3')
    dy2, dw, sums = _cv_bwd_ln(S['p'], dy3, S['w_dw'], P['cv_b_dw'], P['cv_ln_g'], P['cv_ln_b'], name='cv_bwd_ln')
    g['cv_w_dw'] = dw[:CONV_WIDTH]
    g['cv_b_dw'], g['cv_ln_g'], g['cv_ln_b'] = sums[0], sums[1], sums[2]
    dp, psum = _cv_bwd_in(S['p'], dy2, S['w_dw'], name='cv_bwd_in')
    g['cv_b_pw1'] = psum[0]
    g['cv_w_pw1'] = _mm(h1, dp, ta=True, name='cv_dw_pw1', out_chips=N_CHIPS)
    return (dp, P['cv_w_pw1']), g


class Hooks(NamedTuple):
    fwd: dict
    bwd_comm: Callable
    bwd_done: Callable


def _local_step(x, target, mod, P, hooks=None):
    T, D = x.shape
    L = mod.shape[0]
    saved = []
    for i in range(L):
        kind, j = i % N_MIXERS, i // N_MIXERS
        m = [mod[i:i + 1, k * D:(k + 1) * D] for k in range(6)]
        sh_m, sc_m, g_m, sh_f, sc_f, g_f = m
        w_mix, w_mlp = P['norm_mix'][i:i + 1], P['norm_mlp'][i:i + 1]
        h1 = _norm_mod_fwd(x, w_mix, sc_m, sh_m, name='norm_mix_fwd')
        if kind == 0:
            op, S = _fox_forward(h1, P, j, D, carried=hooks.fwd.get(i) if hooks else None)
            y, x1 = _mm(op, P['fox_w_out'][j], name='fox_out', extras=[(x, 'tile'), (g_m, 'row')],
                        epilogue=_residual_epilogue, out_dtypes=(F32, F32))
        elif kind == 1:
            op, S = _sg_forward(h1, P, D)
            y, x1 = _mm(op, P['sg_w_out'], name='sg_out', extras=[(x, 'tile'), (g_m, 'row')],
                        epilogue=_residual_epilogue, out_dtypes=(F32, F32))
        else:
            op, S = _cv_forward(h1, P, D)
            y, x1 = _mm(op, P['cv_w_pw2'], name='cv_out',
                        extras=[(x, 'tile'), (g_m, 'row'), (P['cv_b_pw2'], 'row')],
                        epilogue=_residual_bias_epilogue, out_dtypes=(F32, F32))
        h2 = _norm_mod_fwd(x1, w_mlp, sc_f, sh_f, name='norm_mlp_fwd')
        r = _mm(h2, P['w_mlp_in'][i], name='mlp_in', epilogue=_relu2_epilogue, out_dtypes=(BF16,),
                b_outer=True, tm=WIDE_ROWS)
        z, x2 = _mm(r, P['w_mlp_out'][i], name='mlp_out', extras=[(x1, 'tile'), (g_f, 'row')],
                    epilogue=_residual_epilogue, out_dtypes=(F32, F32), tk=P['w_mlp_out'][i].shape[0])
        saved.append(dict(x=x, h1=h1, S=S, y=y, x1=x1, h2=h2, r=r, z=z, m=m))
        x = x2

    loss_part, dx, dz, dgf = _loss_head(x, target, saved[-1]['z'], saved[-1]['m'][5], name='loss_head')
    dgf = dgf[0]

    grads = {k: [None] * L for k in ('norm_mix', 'norm_mlp')}
    mix_grads, mat = {}, {}
    dmod = [None] * L
    for i in reversed(range(L)):
        kind, j = i % N_MIXERS, i // N_MIXERS
        sv = saved[i]
        sh_m, sc_m, g_m, sh_f, sc_f, g_f = sv['m']
        w_mix, w_mlp = P['norm_mix'][i:i + 1], P['norm_mlp'][i:i + 1]
        mat['w_mlp_out', i] = _mm(sv['r'], dz, ta=True, name='mlp_dw_out')
        da = _mm(dz, P['w_mlp_out'][i], tb=True, name='mlp_da', extras=[(sv['r'], 'tile')],
                 epilogue=_relu2_bwd_epilogue, out_dtypes=(BF16,), b_outer=True, tm=WIDE_ROWS)
        mat['w_mlp_in', i] = _mm(sv['h2'], da, ta=True, name='mlp_dw_in', out_chips=N_CHIPS)
        dx1, dy, sums_f = _dh_norm_bwd(da, P['w_mlp_in'][i], sv['x1'], dx, w_mlp, sc_f, name='mlp_dh',
                                       gate=(sv['y'], g_m))
        if kind == 0:
            carried = hooks is not None and i == 0
            last, g, comm_outs = _fox_backward(dy, sv['h1'], sv['S'], P, j, D,
                                              comm=hooks.bwd_comm(mat) if carried else None)
            if carried:
                hooks.bwd_done(comm_outs)
        elif kind == 1:
            last, g = _sg_backward(dy, sv['h1'], sv['S'], P, D)
        else:
            last, g = _cv_backward(dy, sv['h1'], sv['S'], P, D)
            g['cv_b_pw2'] = sums_f[4]
        for k, val in g.items():
            if k in BIG:
                mat[k, j] = val
            else:
                mix_grads.setdefault(k, {})[j] = val
        if i > 0:
            dx, dz, sums_m = _dh_norm_bwd(*last, sv['x'], dx1, w_mix, sc_m, name='mix_dh',
                                          below=(saved[i - 1]['z'], saved[i - 1]['m'][5]))
        else:
            dx, sums_m = _dh_norm_bwd(*last, sv['x'], dx1, w_mix, sc_m, name='mix_dh')
        grads['norm_mlp'][i], grads['norm_mix'][i] = sums_f[2], sums_m[2]
        dmod[i] = jnp.concatenate([sums_m[0], sums_m[1], sums_f[3], sums_f[0], sums_f[1], dgf])
        dgf = sums_m[7]

    out = {k: jnp.stack(v) for k, v in grads.items()}
    for k, per_j in mix_grads.items():
        out[k] = jnp.stack([per_j[j] for j in sorted(per_j)])
    return loss_part, dx, jnp.stack(dmod), out, mat


def _all_gather8(blocks, *, name):
    n = len(blocks)

    def body(*refs):
        x_refs, out_refs = refs[:n], refs[n:2 * n]
        send_sems, recv_sems, local_sems = refs[2 * n:]
        x, y, c = _position()
        me, sibling = (x, y, c), (x, y, 1 - c)
        chips = [(1 - x, y), (x, 1 - y), (1 - x, 1 - y)]

        def slot(a, px, py, pc):
            return out_refs[a].at[4 * px + 2 * py + pc]

        def copy(a, k, blk, to, src=None):
            return pltpu.make_async_remote_copy(
                src_ref=slot(a, *blk) if src is None else src, dst_ref=slot(a, *blk),
                send_sem=send_sems.at[7 * a + k], recv_sem=recv_sems.at[7 * a + k],
                device_id=to, device_id_type=MESH)

        mine = [pltpu.make_async_copy(x_refs[a], slot(a, *me), local_sems.at[a]) for a in range(n)]
        for cp in mine:
            cp.start()
        first = []
        for j, chip in enumerate(chips):
            first += [copy(a, 1 + j, me, (*chip, c), src=x_refs[a]) for a in range(n)]
        first += [copy(a, 0, me, sibling, src=x_refs[a]) for a in range(n)]
        for cp in first:
            cp.start()
        passed = []
        for j, chip in enumerate(chips):
            for a in range(n):
                copy(a, 1 + j, (*chip, c), me).wait_recv()
                passed.append(copy(a, 4 + j, (*chip, c), sibling))
                passed[-1].start()
        for a in range(n):
            copy(a, 0, sibling, me).wait_recv()
        for j, chip in enumerate(chips):
            for a in range(n):
                copy(a, 4 + j, (*chip, 1 - c), me).wait_recv()
        for cp in first + passed:
            cp.wait_send()
        for cp in mine:
            cp.wait()

    return pl.pallas_call(
        body, name=name, in_specs=[ANY] * n, out_specs=[ANY] * n,
        out_shape=[jax.ShapeDtypeStruct((8,) + b.shape, b.dtype) for b in blocks],
        scratch_shapes=[pltpu.SemaphoreType.DMA((7 * n,)), pltpu.SemaphoreType.DMA((7 * n,)),
                        pltpu.SemaphoreType.DMA((n,))],
    )(*blocks)


def _exchange(comm, *, name, aliases=None):
    ns, no = len(comm.srcs), len(comm.out_shapes)

    def body(*refs):
        copies = _comm_copies(comm.plan, refs[:ns], refs[ns:ns + no], *refs[ns + no:])
        _comm_start(copies)
        _comm_wait(copies)

    return pl.pallas_call(
        body, name=name, in_specs=[ANY] * ns, out_specs=[ANY] * no, out_shape=list(comm.out_shapes),
        scratch_shapes=comm.scratch(), input_output_aliases=aliases or {},
    )(*comm.srcs)


def _gather_comm(halves):
    n = len(halves)

    def plan(src, out, x, y, c):
        mine = 4 * x + 2 * y + c
        remote = [(src[a], out[a].at[mine], (x, y, 1 - c), out[a].at[4 * x + 2 * y + 1 - c]) for a in range(n)]
        for fx, fy in CHIP_FLIPS:
            px, py = _flip(x, fx), _flip(y, fy)
            remote += [(src[a], out[a].at[mine], (px, py, c), out[a].at[4 * px + 2 * py + c]) for a in range(n)]
        return remote, [(src[a], out[a].at[mine]) for a in range(n)]

    return Comm(list(halves), [jax.ShapeDtypeStruct((8,) + h.shape, h.dtype) for h in halves], plan, 4 * n, n)


def _gather_forward(bufs, *, name):
    n = len(bufs)

    def plan(src, out, x, y, c):
        remote = []
        for fx, fy in CHIP_FLIPS:
            px, py = _flip(x, fx), _flip(y, fy)
            remote += [(src[a].at[4 * px + 2 * py + c], out[a].at[4 * px + 2 * py + c], (x, y, 1 - c),
                        out[a].at[4 * px + 2 * py + 1 - c]) for a in range(n)]
        return remote, []

    comm = Comm(list(bufs), [jax.ShapeDtypeStruct(b.shape, b.dtype) for b in bufs], plan, 3 * n, 0)
    return _exchange(comm, name=name, aliases={a: a for a in range(n)})


CHIP_FLIPS = ((1, 0), (0, 1), (1, 1))


def _flip(v, f):
    return 1 - v if f else v


def _sum_rows_tile(R, C, budget=3 << 20):
    best = None
    for t in range(8, R + 1, 8):
        if R % t == 0 and t * C * 4 <= budget:
            best = t
    return best if best is not None else R


def _rs_begin(gps, *, wire_dtype):
    n = len(gps)
    c_arr = jnp.reshape(_position()[2], (1,)).astype(jnp.int32)

    def plan(src, out, x, y, c):
        return [(src[a].at[b, 1 - c], out[a].at[b], (x, y, 1 - c), out[a].at[b])
                for a in range(n) for b in range(4)], []

    got1 = _exchange(Comm(list(gps), [jax.ShapeDtypeStruct((4,) + g.shape[2:], F32) for g in gps], plan, 4 * n, 0),
                     name='rs_sibling')

    def sum_chip(c_ref, mine_ref, got_ref, out_ref):
        out_ref[...] = (mine_ref[...] + got_ref[...]).astype(out_ref.dtype)

    parts = []
    for gp, g1 in zip(gps, got1):
        _, _, R, C = gp.shape
        tr = _sum_rows_tile(R, C)
        parts.append(pl.pallas_call(
            sum_chip, name='rs_sum_chip',
            grid_spec=pltpu.PrefetchScalarGridSpec(
                num_scalar_prefetch=1, grid=(4, R // tr),
                in_specs=[pl.BlockSpec((None, None, tr, C), lambda b, r, cr: (b, cr[0], r, 0)),
                          pl.BlockSpec((None, tr, C), lambda b, r, cr: (b, r, 0))],
                out_specs=pl.BlockSpec((None, tr, C), lambda b, r, cr: (b, r, 0))),
            out_shape=jax.ShapeDtypeStruct((4, R, C), wire_dtype),
            compiler_params=_cparams("parallel", "parallel"),
        )(c_arr, gp, g1))
    return got1, parts


def _rs_chips_comm(parts):
    n = len(parts)

    def plan(src, out, x, y, c):
        remote = []
        for k, (fx, fy) in enumerate(CHIP_FLIPS):
            px, py = _flip(x, fx), _flip(y, fy)
            remote += [(src[a].at[2 * px + py], out[a].at[k], (px, py, c), out[a].at[k]) for a in range(n)]
        return remote, []

    return Comm(list(parts), [jax.ShapeDtypeStruct((3,) + p.shape[1:], p.dtype) for p in parts], plan, 3 * n, 0)


def _rs_finish(gps, got1, got2):
    n = len(gps)
    x, y, c = _position()
    bc_arr = jnp.stack([2 * x + y, c]).astype(jnp.int32)

    def sum_final(bc_ref, mine_ref, got1_ref, got2_ref, out_ref):
        acc = mine_ref[...] + got1_ref[...]
        for k in range(3):
            acc = acc + got2_ref[k].astype(F32)
        out_ref[...] = acc

    halves = []
    for gp, g1, g2 in zip(gps, got1, got2):
        _, _, R, C = gp.shape
        tr = _sum_rows_tile(R, C, budget=2 << 20)
        halves.append(pl.pallas_call(
            sum_final, name='rs_sum_final',
            grid_spec=pltpu.PrefetchScalarGridSpec(
                num_scalar_prefetch=1, grid=(R // tr,),
                in_specs=[pl.BlockSpec((None, None, tr, C), lambda r, bc: (bc[0], bc[1], r, 0)),
                          pl.BlockSpec((None, tr, C), lambda r, bc: (bc[0], r, 0)),
                          pl.BlockSpec((3, tr, C), lambda r, bc: (0, r, 0))],
                out_specs=pl.BlockSpec((None, tr, C), lambda r, bc: (bc[1], r, 0))),
            out_shape=jax.ShapeDtypeStruct((2, R, C), F32),
            compiler_params=_cparams("parallel"),
        )(bc_arr, gp, g1, g2))

    def plan(src, out, x, y, c):
        return [(src[a].at[c], out[a].at[c], (x, y, 1 - c), out[a].at[1 - c]) for a in range(n)], []

    comm = Comm(halves, [jax.ShapeDtypeStruct(h.shape, F32) for h in halves], plan, n, 0)
    return _exchange(comm, name='rs_swap', aliases={a: a for a in range(n)})


def _sum8(gathered, *, name):
    _, R, C = gathered.shape

    def body(g_ref, o_ref):
        acc = g_ref[0]
        for k in range(1, 8):
            acc = acc + g_ref[k]
        o_ref[...] = acc

    return pl.pallas_call(body, name=name, out_shape=jax.ShapeDtypeStruct((R, C), F32))(gathered)


def _adamw(w, g, m, v, *, name):
    shape = w.shape
    cols = shape[-1]
    rows = w.size // cols
    tr = _sum_rows_tile(rows, cols, budget=2 << 20)

    def body(w_ref, g_ref, m_ref, v_ref, d_ref, mo_ref, vo_ref):
        gv = g_ref[...]
        mn = ADAM_B1 * m_ref[...] + (1.0 - ADAM_B1) * gv
        vn = ADAM_B2 * v_ref[...] + (1.0 - ADAM_B2) * (gv * gv)
        m_hat = mn / (1.0 - ADAM_B1 ** ADAM_STEP)
        v_hat = vn / (1.0 - ADAM_B2 ** ADAM_STEP)
        d_ref[...] = -ADAM_LR * (m_hat / (jnp.sqrt(v_hat) + ADAM_EPS) + ADAM_WD * w_ref[...])
        mo_ref[...] = mn
        vo_ref[...] = vn

    blk = pl.BlockSpec((tr, cols), lambda i: (i, 0))
    outs = pl.pallas_call(
        body, name=name, grid=(rows // tr,), in_specs=[blk] * 4, out_specs=[blk] * 3,
        out_shape=[jax.ShapeDtypeStruct((rows, cols), F32)] * 3,
        compiler_params=_cparams("parallel"),
    )(*[a.reshape(rows, cols) for a in (w, g, m, v)])
    return tuple(o.reshape(shape) for o in outs)


WEIGHTS = ['norm_mix', 'norm_mlp', 'w_ada', 'b_ada', 'w_mlp_in', 'w_mlp_out', 'fox_w_in', 'fox_b_f',
           'fox_q_norm', 'fox_k_norm', 'fox_w_out', 'sg_w_in', 'sg_ln_g', 'sg_ln_b', 'sg_w_s', 'sg_b_s',
           'sg_w_out', 'cv_w_pw1', 'cv_b_pw1', 'cv_w_dw', 'cv_b_dw', 'cv_ln_g', 'cv_ln_b', 'cv_w_pw2',
           'cv_b_pw2']
BIG = {'w_mlp_in': 2, 'w_mlp_out': 1, 'fox_w_in': 2, 'fox_w_out': 1, 'sg_w_in': 2, 'sg_w_out': 1,
       'cv_w_pw1': 2, 'cv_w_pw2': 1}
SMALL_SHARDED = ['cv_b_pw1', 'cv_w_dw', 'cv_b_dw', 'cv_ln_g', 'cv_ln_b', 'cv_b_pw2']
SMALL_GRADS = ['norm_mix', 'norm_mlp', 'fox_b_f', 'fox_q_norm', 'fox_k_norm', 'sg_ln_g', 'sg_ln_b', 'sg_w_s',
               'sg_b_s'] + SMALL_SHARDED
GRAD_WIRE_DTYPE = BF16


def _pack_rows(parts, cols):
    flat = jnp.concatenate([p.reshape(-1) for p in parts])
    rows = -(-flat.size // (8 * cols)) * 8
    return jnp.pad(flat, (0, rows * cols - flat.size)).reshape(rows, cols)


def _unpack(flat, shapes):
    out, off = [], 0
    for s in shapes:
        n = math.prod(s)
        out.append(flat[..., off:off + n].reshape(flat.shape[:-1] + tuple(s)))
        off += n
    return out


def _merge_chips(a, axis):
    a = jnp.moveaxis(a, 0, axis)
    return a.reshape(a.shape[:axis] + (a.shape[axis] * a.shape[axis + 1],) + a.shape[axis + 2:])


def _split_chips(a, axis):
    a = a.reshape(a.shape[:axis] + (4, a.shape[axis] // 4) + a.shape[axis + 1:])
    return jnp.moveaxis(a, axis, 0)


def _step(a):
    x, y, c = _position()
    me = 4 * x + 2 * y + c
    chip = 2 * x + y
    T, D = a['x'].shape[1], a['x'].shape[2]
    L = a['norm_mix'].shape[0]

    small_shapes = [(D,)] + [a[n].shape for n in SMALL_SHARDED]
    small = _all_gather8([_pack_rows([a['c']] + [a[n] for n in SMALL_SHARDED], LANE)], name='ag_small')[0]
    small = small.reshape(8, -1)
    c_all = _unpack(small, small_shapes[:1])[0]
    sharded = _unpack(small[0::2, D:], small_shapes[1:])
    P = {n: _merge_chips(v, v.ndim - 2) for n, v in zip(SMALL_SHARDED, sharded)}

    c_act = _silu_rows(c_all, name='c_act')
    mod_cols = jnp.stack([
        _mm(c_act, a['w_ada'][i], name='ada_mod', tm=8, tn=_col_tile(a['w_ada'].shape[2], 768),
            extras=[(lax.dynamic_slice_in_dim(a['b_ada'][i:i + 1], chip * a['w_ada'].shape[2],
                                              a['w_ada'].shape[2], axis=1), 'row')],
            epilogue=_bias_epilogue)
        for i in range(L)])
    mod_all = _all_gather8([mod_cols.reshape(L * 8, -1)], name='ag_mod')[0].reshape(8, L, 8, -1)
    mod = lax.dynamic_index_in_dim(mod_all[0::2], me, axis=2, keepdims=False)
    mod = jnp.moveaxis(mod, 0, 1).reshape(L, 6 * D)

    units = _matrix_units(L)
    first, with_prep, with_attn, with_last = units[:1], units[1:4], units[4:-3], units[-3:]
    last, earlier = units[:2], units[2:]
    n_heads = D // HEAD_DIM

    def half_block(unit):
        blk = a[unit[0]][unit[1]]
        return lax.dynamic_index_in_dim(blk.astype(BF16).reshape(2, blk.shape[0] // 2, blk.shape[1]), c, axis=0,
                                        keepdims=False)

    def install(group, gathered):
        for (name, idx), gth in zip(group, gathered):
            blocks = gth.reshape((4,) + a[name].shape[1:])
            if name == 'fox_w_in':
                pad = jnp.zeros((blocks.shape[1], LANE - n_heads), BF16)
                full = jnp.concatenate([blocks[0], blocks[1], blocks[2], blocks[3], pad], axis=-1)
            else:
                full = _merge_chips(blocks, BIG[name] - 1)
            if name in ('w_mlp_in', 'w_mlp_out', 'fox_w_in', 'fox_w_out'):
                P.setdefault(name, {})[idx] = full
            else:
                P[name] = full

    install(first, _all_gather8([half_block(u) for u in first], name='ag_weights_first'))
    for n in ('sg_w_s', 'sg_b_s', 'cv_w_dw'):
        P[n] = (P[n] if n in P else a[n])[0]
    for n in ('norm_mix', 'norm_mlp', 'fox_b_f', 'fox_q_norm', 'fox_k_norm', 'sg_ln_g', 'sg_ln_b'):
        P[n] = a[n]

    def split_grad(unit, grad):
        name = unit[0]
        if name == 'fox_w_in':
            grad = grad[:, :a[name].shape[2] * N_CHIPS]
        blk = grad if grad.ndim == 3 else _split_chips(grad, BIG[name] - 1)
        return blk.reshape(N_CHIPS, 2, blk.shape[1] // 2, blk.shape[2])

    state = {}

    def riding(group):
        return (_gather_comm([half_block(u) for u in group]),
                lambda outs: install(group, _gather_forward(outs, name='ag_weights_forward')))

    def bwd_comm(mat):
        state['gps'] = [split_grad(u, mat[u]) for u in earlier]
        state['got1'], parts = _rs_begin(state['gps'], wire_dtype=GRAD_WIRE_DTYPE)
        return _rs_chips_comm(parts)

    def bwd_done(outs):
        state['got2'] = outs

    last_fox = N_MIXERS * ((L - 1) // N_MIXERS)
    hooks = Hooks({0: {'prep': riding(with_prep), 'attn': riding(with_attn)}, last_fox: {'attn': riding(with_last)}},
                  bwd_comm, bwd_done)
    loss_part, grad_x, dmod, g, mat = _local_step(a['x'][0], a['loss_target'][0], mod, P, hooks)

    small_g = [dmod, loss_part[0:1, 0:1]] + [g[n] for n in SMALL_GRADS]
    small_g_shapes = [s.shape for s in small_g]
    all_small = _all_gather8([_pack_rows(small_g, LANE)], name='ag_small_grads')[0]
    summed = _sum8(all_small, name='sum_small_grads').reshape(-1)
    sums = _unpack(summed, small_g_shapes)
    loss = sums[1][0, 0]
    grads = dict(zip(SMALL_GRADS, sums[2:]))
    grads['b_ada'] = sums[0]
    for n in SMALL_SHARDED:
        blk = a[n].shape[-1]
        grads[n] = lax.dynamic_slice_in_dim(grads[n], chip * blk, blk, axis=grads[n].ndim - 1)
    dmod_all = all_small.reshape(8, -1)[:, :dmod.size].reshape(8, L, 6 * D)
    cols = a['w_ada'].shape[2]
    dmod_cols = lax.dynamic_slice_in_dim(dmod_all, chip * cols, cols, axis=2)
    pad8 = lambda t: jnp.pad(t, ((0, LANE - 8), (0, 0)))
    c_act_pad = pad8(c_act)
    grads['w_ada'] = jnp.stack([
        _mm(c_act_pad, pad8(dmod_cols[:, i]), ta=True, name='ada_dw', tn=_col_tile(cols, 768))
        for i in range(L)])

    shards = dict(zip(earlier, _rs_finish(state['gps'], state['got1'], state['got2'])))
    gps = [split_grad(u, mat[u]) for u in last]
    got1, parts = _rs_begin(gps, wire_dtype=GRAD_WIRE_DTYPE)
    got2 = _exchange(_rs_chips_comm(parts), name='rs_chips')
    shards.update(zip(last, _rs_finish(gps, got1, got2)))
    for n in BIG:
        grads[n] = jnp.stack([shards[n, idx].reshape(a[n].shape[1:]) for idx in range(a[n].shape[0])])

    deltas, new_m, new_v = {}, {}, {}
    for n in WEIGHTS:
        deltas[n], new_m[n], new_v[n] = _adamw(a[n], grads[n], a['m_' + n], a['v_' + n], name='adamw')
    return (loss, grad_x[None], *[grads[n] for n in WEIGHTS], *[deltas[n] for n in WEIGHTS],
            *[new_m[n] for n in WEIGHTS], *[new_v[n] for n in WEIGHTS])


def _matrix_units(n_layers):
    mixers = (('fox_w_in', 'fox_w_out'), ('sg_w_in', 'sg_w_out'), ('cv_w_pw1', 'cv_w_pw2'))
    units = []
    for i in range(n_layers):
        units += [(n, i // N_MIXERS) for n in mixers[i % N_MIXERS]] + [('w_mlp_in', i), ('w_mlp_out', i)]
    return units


def _silu_rows(x, *, name):
    def body(x_ref, o_ref):
        xv = x_ref[...]
        o_ref[...] = (xv * jax.nn.sigmoid(xv)).astype(BF16)

    return pl.pallas_call(body, name=name, out_shape=jax.ShapeDtypeStruct(x.shape, BF16))(x)


def kernel(x, c, norm_mix, norm_mlp, w_ada, b_ada, w_mlp_in, w_mlp_out, fox_w_in, fox_b_f, fox_q_norm, fox_k_norm, fox_w_out, sg_w_in, sg_ln_g, sg_ln_b, sg_w_s, sg_b_s, sg_w_out, cv_w_pw1, cv_b_pw1, cv_w_dw, cv_b_dw, cv_ln_g, cv_ln_b, cv_w_pw2, cv_b_pw2, loss_target, m_norm_mix, m_norm_mlp, m_w_ada, m_b_ada, m_w_mlp_in, m_w_mlp_out, m_fox_w_in, m_fox_b_f, m_fox_q_norm, m_fox_k_norm, m_fox_w_out, m_sg_w_in, m_sg_ln_g, m_sg_ln_b, m_sg_w_s, m_sg_b_s, m_sg_w_out, m_cv_w_pw1, m_cv_b_pw1, m_cv_w_dw, m_cv_b_dw, m_cv_ln_g, m_cv_ln_b, m_cv_w_pw2, m_cv_b_pw2, v_norm_mix, v_norm_mlp, v_w_ada, v_b_ada, v_w_mlp_in, v_w_mlp_out, v_fox_w_in, v_fox_b_f, v_fox_q_norm, v_fox_k_norm, v_fox_w_out, v_sg_w_in, v_sg_ln_g, v_sg_ln_b, v_sg_w_s, v_sg_b_s, v_sg_w_out, v_cv_w_pw1, v_cv_b_pw1, v_cv_w_dw, v_cv_b_dw, v_cv_ln_g, v_cv_ln_b, v_cv_w_pw2, v_cv_b_pw2):
    return _step(dict(locals()))
```
